```python
import math
import jax
import jax.numpy as jnp
from jax import lax
import numpy as np

D_MODEL = 2048
BATCH = 8
SEQ = 8192
DEPTH = 2

MEM_LEN = 256
HEAD_DIM = 128
N_MIX_HEADS = D_MODEL // HEAD_DIM
MEM_HEADS = 4
MEM_W = MEM_HEADS * HEAD_DIM
ATTN_GROUPS = ((128, 1), (512, 4), (2048, 16))
ATTN_HEADS = N_MIX_HEADS - MEM_HEADS
HEADS_PER_GROUP = ATTN_HEADS // len(ATTN_GROUPS)
ATTN_W = ATTN_HEADS * HEAD_DIM
ATTN_OUT_W = HEADS_PER_GROUP * HEAD_DIM
BLK = 128
SGU_GROUPS = ATTN_HEADS
SGU_GROUP_DIM = HEAD_DIM
SGU_W = SGU_GROUPS * SGU_GROUP_DIM
SGU_CHUNK = 128
ROT_DIM = HEAD_DIM // 4
ROPE_THETA = 500000.0
FFN_HIDDEN = ((8 * D_MODEL // 3 + 255) // 256) * 256
NORM_EPS = 1e-6
LN_EPS = 1e-5
NEG_INF = -1e30

kernel_name = 'hybrid_dilated_attn_gmlp_memory_trunk'


def _rms_norm(x, g):
    xf = x.astype(jnp.float32)
    y = xf * lax.rsqrt(jnp.mean(xf * xf, axis=-1, keepdims=True) + NORM_EPS)
    return (y * g.astype(jnp.float32)).astype(x.dtype)


def _layer_norm(x, g, b):
    xf = x.astype(jnp.float32)
    mu = jnp.mean(xf, axis=-1, keepdims=True)
    var = jnp.mean(jnp.square(xf - mu), axis=-1, keepdims=True)
    y = (xf - mu) * lax.rsqrt(var + LN_EPS)
    return (y * g.astype(jnp.float32) + b.astype(jnp.float32)).astype(x.dtype)


def _partial_rotary(t, positions):
    half = ROT_DIM // 2
    inv_freq = ROPE_THETA ** (-jnp.arange(half, dtype=jnp.float32) / half)
    ang = positions.astype(jnp.float32)[:, :, None] * inv_freq
    cos = jnp.cos(ang)[:, :, None, :]
    sin = jnp.sin(ang)[:, :, None, :]
    tf = t.astype(jnp.float32)
    x1 = tf[..., :half]
    x2 = tf[..., half:ROT_DIM]
    rot = jnp.concatenate([x1 * cos - x2 * sin, x2 * cos + x1 * sin, tf[..., ROT_DIM:]], axis=-1)
    return rot.astype(t.dtype)


def _dilated_group(q, k, v, window, dilation):
    b, s, h, dh = q.shape
    n_back = window // dilation
    span = dilation * BLK
    sp = -(-s // span) * span
    length = sp // dilation
    nb = length // BLK

    def to_blocks(t):
        t = jnp.pad(t, ((0, 0), (0, sp - s), (0, 0), (0, 0)))
        t = t.reshape(b, length, dilation, h, dh).transpose(0, 2, 3, 1, 4)
        return t.reshape(b, dilation, h, nb, BLK, dh)

    def with_prev(t):
        prev = jnp.pad(t[:, :, :, :-1], ((0, 0), (0, 0), (0, 0), (1, 0), (0, 0), (0, 0)))
        return jnp.concatenate([prev, t], axis=4)

    qb = to_blocks(q)
    kb = with_prev(to_blocks(k))
    vb = with_prev(to_blocks(v))
    logits = jnp.einsum('brhnqd,brhnkd->brhnqk', qb, kb,
                        preferred_element_type=jnp.float32) * (dh ** -0.5)
    qi = jnp.arange(BLK)[:, None]
    ki = jnp.arange(2 * BLK)[None, :]
    dist = BLK + qi - ki
    band = (dist >= 0) & (dist <= n_back)
    first = (jnp.arange(nb) == 0)[:, None, None]
    mask = band[None] & (jnp.logical_not(first) | (ki >= BLK)[None])
    logits = jnp.where(mask, logits, NEG_INF)
    lse = jax.nn.logsumexp(logits, axis=-1)
    p = jnp.exp(logits - lse[..., None])
    out = jnp.einsum('brhnqk,brhnkd->brhnqd', p.astype(v.dtype), vb)
    out = out.reshape(b, dilation, h, length, dh).transpose(0, 3, 1, 2, 4).reshape(b, sp, h, dh)[:, :s]
    lse = lse.reshape(b, dilation, h, length).transpose(0, 3, 1, 2).reshape(b, sp, h)[:, :s]
    return out, lse


def _dilated_attention_mixer(h, positions, w_in):
    b, s, _ = h.shape
    proj = h @ w_in
    q, k, v, q_mem = jnp.split(proj, [ATTN_W, 2 * ATTN_W, 3 * ATTN_W], axis=-1)
    q = _partial_rotary(q.reshape(b, s, ATTN_HEADS, HEAD_DIM), positions)
    k = _partial_rotary(k.reshape(b, s, ATTN_HEADS, HEAD_DIM), positions)
    v = v.reshape(b, s, ATTN_HEADS, HEAD_DIM)
    outs, lses = [], []
    for g, (window, dilation) in enumerate(ATTN_GROUPS):
        sl = slice(g * HEADS_PER_GROUP, (g + 1) * HEADS_PER_GROUP)
        o, l = _dilated_group(q[:, :, sl], k[:, :, sl], v[:, :, sl], window, dilation)
        outs.append(o)
        lses.append(l)
    w = jax.nn.softmax(jnp.stack(lses, axis=0), axis=0)
    merged = jnp.einsum('gbsh,gbshd->bshd', w.astype(v.dtype), jnp.stack(outs, axis=0))
    return merged.reshape(b, s, ATTN_OUT_W), q_mem


def _spatial_gating_mixer(h, w_in, ln_g, ln_b, w_spatial, b_spatial):
    b, s, _ = h.shape
    proj = h @ w_in
    u, v, q_mem = jnp.split(proj, [SGU_W, 2 * SGU_W], axis=-1)
    u = jax.nn.gelu(u)
    v = _layer_norm(jax.nn.gelu(v), ln_g, ln_b)
    v = v.reshape(b, s // SGU_CHUNK, SGU_CHUNK, SGU_GROUPS, SGU_GROUP_DIM)
    causal = jnp.tril(jnp.ones((SGU_CHUNK, SGU_CHUNK), dtype=bool))
    w_s = jnp.where(causal[None], w_spatial, 0.0).astype(v.dtype)
    mixed = jnp.einsum('gts,bnsgc->bntgc', w_s, v) + b_spatial.T[None, None, :, :, None]
    return u * mixed.reshape(b, s, SGU_W), q_mem


def _memory_attention(q_mem, mem_n, w_mem_kv):
    b, s, _ = q_mem.shape
    kv = mem_n @ w_mem_kv
    k, v = jnp.split(kv, 2, axis=-1)
    q = q_mem.reshape(b, s, MEM_HEADS, HEAD_DIM)
    k = k.reshape(b, -1, MEM_HEADS, HEAD_DIM)
    v = v.reshape(b, -1, MEM_HEADS, HEAD_DIM)
    logits = jnp.einsum('bshd,bmhd->bhsm', q, k,
                        preferred_element_type=jnp.float32) * (HEAD_DIM ** -0.5)
    p = jax.nn.softmax(logits, axis=-1)
    out = jnp.einsum('bhsm,bmhd->bshd', p.astype(v.dtype), v)
    return out.reshape(b, s, MEM_W)


def _swiglu(h, w_gate, w_up, w_down):
    return (jax.nn.silu(h @ w_gate) * (h @ w_up)) @ w_down


def _fwd_setup_inputs(seed: int = 0) -> dict:
    key = jax.random.key(seed)
    ks = jax.random.split(key, 24)
    n_a = (DEPTH + 1) // 2
    n_b = DEPTH // 2

    def dense(k, shape, fan_in):
        return jax.random.normal(k, shape, jnp.float32) * (fan_in ** -0.5)

    def gain(k, shape):
        return 1.0 + 0.02 * jax.random.normal(k, shape, jnp.float32)

    def small(k, shape):
        return 0.02 * jax.random.normal(k, shape, jnp.float32)

    x = jax.random.normal(ks[0], (BATCH, SEQ, D_MODEL), jnp.float32)
    mem = jax.random.normal(ks[1], (BATCH, MEM_LEN, D_MODEL), jnp.float32)
    offset = jax.random.randint(ks[2], (BATCH, 1), 0, 4096, dtype=jnp.int32)
    positions = offset + jnp.arange(SEQ, dtype=jnp.int32)[None, :]
    return {
        'x': x,
        'mem': mem,
        'positions': positions,
        'mix_norm': gain(ks[3], (DEPTH, D_MODEL)),
        'mem_norm': gain(ks[4], (DEPTH, D_MODEL)),
        'w_mem_kv': dense(ks[5], (DEPTH, D_MODEL, 2 * MEM_W), D_MODEL),
        'ffn_norm': gain(ks[6], (DEPTH, D_MODEL)),
        'w_gate': dense(ks[7], (DEPTH, D_MODEL, FFN_HIDDEN), D_MODEL),
        'w_up': dense(ks[8], (DEPTH, D_MODEL, FFN_HIDDEN), D_MODEL),
        'w_down': dense(ks[9], (DEPTH, FFN_HIDDEN, D_MODEL), FFN_HIDDEN),
        'attn_w_in': dense(ks[10], (n_a, D_MODEL, 3 * ATTN_W + MEM_W), D_MODEL),
        'attn_w_out': dense(ks[11], (n_a, ATTN_OUT_W + MEM_W, D_MODEL), ATTN_OUT_W + MEM_W),
        'sgu_w_in': dense(ks[12], (n_b, D_MODEL, 2 * SGU_W + MEM_W), D_MODEL),
        'sgu_ln_g': gain(ks[13], (n_b, SGU_W)),
        'sgu_ln_b': small(ks[14], (n_b, SGU_W)),
        'sgu_w_spatial': dense(ks[15], (n_b, SGU_GROUPS, SGU_CHUNK, SGU_CHUNK), SGU_CHUNK),
        'sgu_b_spatial': gain(ks[16], (n_b, SGU_GROUPS, SGU_CHUNK)),
        'sgu_w_out': dense(ks[17], (n_b, SGU_W + MEM_W, D_MODEL), SGU_W + MEM_W),
        'final_norm': gain(ks[18], (D_MODEL,)),
    }


def _fwd_reference(x, mem, positions, mix_norm, mem_norm, w_mem_kv, ffn_norm, w_gate, w_up, w_down,
              attn_w_in, attn_w_out, sgu_w_in, sgu_ln_g, sgu_ln_b, sgu_w_spatial, sgu_b_spatial,
              sgu_w_out, final_norm):
    for i in range(DEPTH):
        j = i // 2
        h = _rms_norm(x, mix_norm[i])
        if i % 2 == 0:
            mix_out, q_mem = _dilated_attention_mixer(h, positions, attn_w_in[j])
            w_out = attn_w_out[j]
        else:
            mix_out, q_mem = _spatial_gating_mixer(h, sgu_w_in[j], sgu_ln_g[j], sgu_ln_b[j],
                                                   sgu_w_spatial[j], sgu_b_spatial[j])
            w_out = sgu_w_out[j]
        mem_out = _memory_attention(q_mem, _rms_norm(mem, mem_norm[i]), w_mem_kv[i])
        x = x + jnp.concatenate([mix_out, mem_out], axis=-1) @ w_out
        x = x + _swiglu(_rms_norm(x, ffn_norm[i]), w_gate[i], w_up[i], w_down[i])
    return _rms_norm(x, final_norm)


import jax as _jax
import jax.numpy as _jnp

TWIN_FORMAT = 'train_step'
FWD_PARAMS = ['x', 'mem', 'positions', 'mix_norm', 'mem_norm', 'w_mem_kv', 'ffn_norm', 'w_gate', 'w_up', 'w_down', 'attn_w_in', 'attn_w_out', 'sgu_w_in', 'sgu_ln_g', 'sgu_ln_b', 'sgu_w_spatial', 'sgu_b_spatial', 'sgu_w_out', 'final_norm']
TWIN_WEIGHTS = ['mix_norm', 'mem_norm', 'w_mem_kv', 'ffn_norm', 'w_gate', 'w_up', 'w_down', 'attn_w_in', 'attn_w_out', 'sgu_w_in', 'sgu_ln_g', 'sgu_ln_b', 'sgu_w_spatial', 'sgu_b_spatial', 'sgu_w_out', 'final_norm']
TWIN_DIFF_INPUT = 'x'
TWIN_INPUTS = ['x', 'mem', 'positions', 'mix_norm', 'mem_norm', 'w_mem_kv', 'ffn_norm', 'w_gate', 'w_up', 'w_down', 'attn_w_in', 'attn_w_out', 'sgu_w_in', 'sgu_ln_g', 'sgu_ln_b', 'sgu_w_spatial', 'sgu_b_spatial', 'sgu_w_out', 'final_norm', 'loss_target', 'm_mix_norm', 'm_mem_norm', 'm_w_mem_kv', 'm_ffn_norm', 'm_w_gate', 'm_w_up', 'm_w_down', 'm_attn_w_in', 'm_attn_w_out', 'm_sgu_w_in', 'm_sgu_ln_g', 'm_sgu_ln_b', 'm_sgu_w_spatial', 'm_sgu_b_spatial', 'm_sgu_w_out', 'm_final_norm', 'v_mix_norm', 'v_mem_norm', 'v_w_mem_kv', 'v_ffn_norm', 'v_w_gate', 'v_w_up', 'v_w_down', 'v_attn_w_in', 'v_attn_w_out', 'v_sgu_w_in', 'v_sgu_ln_g', 'v_sgu_ln_b', 'v_sgu_w_spatial', 'v_sgu_b_spatial', 'v_sgu_w_out', 'v_final_norm']
TWIN_OUTPUTS = ['loss', 'grad_x', 'grad_mix_norm', 'grad_mem_norm', 'grad_w_mem_kv', 'grad_ffn_norm', 'grad_w_gate', 'grad_w_up', 'grad_w_down', 'grad_attn_w_in', 'grad_attn_w_out', 'grad_sgu_w_in', 'grad_sgu_ln_g', 'grad_sgu_ln_b', 'grad_sgu_w_spatial', 'grad_sgu_b_spatial', 'grad_sgu_w_out', 'grad_final_norm', 'delta_mix_norm', 'delta_mem_norm', 'delta_w_mem_kv', 'delta_ffn_norm', 'delta_w_gate', 'delta_w_up', 'delta_w_down', 'delta_attn_w_in', 'delta_attn_w_out', 'delta_sgu_w_in', 'delta_sgu_ln_g', 'delta_sgu_ln_b', 'delta_sgu_w_spatial', 'delta_sgu_b_spatial', 'delta_sgu_w_out', 'delta_final_norm', 'new_m_mix_norm', 'new_m_mem_norm', 'new_m_w_mem_kv', 'new_m_ffn_norm', 'new_m_w_gate', 'new_m_w_up', 'new_m_w_down', 'new_m_attn_w_in', 'new_m_attn_w_out', 'new_m_sgu_w_in', 'new_m_sgu_ln_g', 'new_m_sgu_ln_b', 'new_m_sgu_w_spatial', 'new_m_sgu_b_spatial', 'new_m_sgu_w_out', 'new_m_final_norm', 'new_v_mix_norm', 'new_v_mem_norm', 'new_v_w_mem_kv', 'new_v_ffn_norm', 'new_v_w_gate', 'new_v_w_up', 'new_v_w_down', 'new_v_attn_w_in', 'new_v_attn_w_out', 'new_v_sgu_w_in', 'new_v_sgu_ln_g', 'new_v_sgu_ln_b', 'new_v_sgu_w_spatial', 'new_v_sgu_b_spatial', 'new_v_sgu_w_out', 'new_v_final_norm']
TWIN_LEAF_KINDS = {'loss': 'loss', 'grad_x': 'grad_x', 'grad_mix_norm': 'grad_w', 'grad_mem_norm': 'grad_w', 'grad_w_mem_kv': 'grad_w', 'grad_ffn_norm': 'grad_w', 'grad_w_gate': 'grad_w', 'grad_w_up': 'grad_w', 'grad_w_down': 'grad_w', 'grad_attn_w_in': 'grad_w', 'grad_attn_w_out': 'grad_w', 'grad_sgu_w_in': 'grad_w', 'grad_sgu_ln_g': 'grad_w', 'grad_sgu_ln_b': 'grad_w', 'grad_sgu_w_spatial': 'grad_w', 'grad_sgu_b_spatial': 'grad_w', 'grad_sgu_w_out': 'grad_w', 'grad_final_norm': 'grad_w', 'delta_mix_norm': 'delta_w', 'delta_mem_norm': 'delta_w', 'delta_w_mem_kv': 'delta_w', 'delta_ffn_norm': 'delta_w', 'delta_w_gate': 'delta_w', 'delta_w_up': 'delta_w', 'delta_w_down': 'delta_w', 'delta_attn_w_in': 'delta_w', 'delta_attn_w_out': 'delta_w', 'delta_sgu_w_in': 'delta_w', 'delta_sgu_ln_g': 'delta_w', 'delta_sgu_ln_b': 'delta_w', 'delta_sgu_w_spatial': 'delta_w', 'delta_sgu_b_spatial': 'delta_w', 'delta_sgu_w_out': 'delta_w', 'delta_final_norm': 'delta_w', 'new_m_mix_norm': 'new_m', 'new_m_mem_norm': 'new_m', 'new_m_w_mem_kv': 'new_m', 'new_m_ffn_norm': 'new_m', 'new_m_w_gate': 'new_m', 'new_m_w_up': 'new_m', 'new_m_w_down': 'new_m', 'new_m_attn_w_in': 'new_m', 'new_m_attn_w_out': 'new_m', 'new_m_sgu_w_in': 'new_m', 'new_m_sgu_ln_g': 'new_m', 'new_m_sgu_ln_b': 'new_m', 'new_m_sgu_w_spatial': 'new_m', 'new_m_sgu_b_spatial': 'new_m', 'new_m_sgu_w_out': 'new_m', 'new_m_final_norm': 'new_m', 'new_v_mix_norm': 'new_v', 'new_v_mem_norm': 'new_v', 'new_v_w_mem_kv': 'new_v', 'new_v_ffn_norm': 'new_v', 'new_v_w_gate': 'new_v', 'new_v_w_up': 'new_v', 'new_v_w_down': 'new_v', 'new_v_attn_w_in': 'new_v', 'new_v_attn_w_out': 'new_v', 'new_v_sgu_w_in': 'new_v', 'new_v_sgu_ln_g': 'new_v', 'new_v_sgu_ln_b': 'new_v', 'new_v_sgu_w_spatial': 'new_v', 'new_v_sgu_b_spatial': 'new_v', 'new_v_sgu_w_out': 'new_v', 'new_v_final_norm': 'new_v'}


def _forward(args):
    return _fwd_reference(*[args[k] for k in FWD_PARAMS])


def _output_shape():
    def fwd():
        inp = _fwd_setup_inputs(0)
        return _fwd_reference(*[inp[k] for k in FWD_PARAMS])
    out = _jax.eval_shape(fwd)
    return out.shape, out.dtype

N_MICROBATCH = 1
ADAM_LR = 0.001
ADAM_B1 = 0.9
ADAM_B2 = 0.999
ADAM_EPS = 1e-08
ADAM_WD = 0.01
ADAM_STEP = 10
PER_EXAMPLE_BATCH_AXIS = {'x': 0, 'mem': 0, 'positions': 0, 'loss_target': 0}
SHARED_INPUTS = []
_WEIGHT_DTYPES = {'mix_norm': _jnp.float32, 'mem_norm': _jnp.float32, 'w_mem_kv': _jnp.float32, 'ffn_norm': _jnp.float32, 'w_gate': _jnp.float32, 'w_up': _jnp.float32, 'w_down': _jnp.float32, 'attn_w_in': _jnp.float32, 'attn_w_out': _jnp.float32, 'sgu_w_in': _jnp.float32, 'sgu_ln_g': _jnp.float32, 'sgu_ln_b': _jnp.float32, 'sgu_w_spatial': _jnp.float32, 'sgu_b_spatial': _jnp.float32, 'sgu_w_out': _jnp.float32, 'final_norm': _jnp.float32}
MOMENT_SCALE = {'mix_norm': 6.473110e-02, 'mem_norm': 1.320529e-02, 'w_mem_kv': 1.853599e-02, 'ffn_norm': 8.992608e-02, 'w_gate': 3.947026e-02, 'w_up': 3.827033e-02, 'w_down': 6.346309e-02, 'attn_w_in': 2.303290e-02, 'attn_w_out': 2.435127e-02, 'sgu_w_in': 6.406558e-02, 'sgu_ln_g': 4.507141e-02, 'sgu_ln_b': 4.390971e-02, 'sgu_w_spatial': 4.561336e-02, 'sgu_b_spatial': 6.635413e-02, 'sgu_w_out': 7.178773e-02, 'final_norm': 3.204916e+01}


def _to_microbatches(a, axis):
    t = _jnp.moveaxis(a, axis, 0)
    t = t.reshape((N_MICROBATCH, t.shape[0] // N_MICROBATCH) + t.shape[1:])
    return _jnp.moveaxis(t, 1, axis + 1)


def setup_inputs(seed: int = 0) -> dict:
    inp = _fwd_setup_inputs(seed)
    key = _jax.random.fold_in(_jax.random.key(seed), 7919)
    shape, _ = _output_shape()
    out = dict(inp)
    out["loss_target"] = _jax.random.normal(_jax.random.fold_in(key, 0), shape, _jnp.float32)
    for i, name in enumerate(TWIN_WEIGHTS):
        w = inp[name].astype(_jnp.float32)
        if MOMENT_SCALE is None:
            s = _jnp.sqrt(_jnp.mean(_jnp.square(w)) + 1e-30)
        else:
            s = MOMENT_SCALE[name]
        km, kv = _jax.random.split(_jax.random.fold_in(key, i + 1))
        out[name] = w
        out["m_" + name] = s * _jax.random.normal(km, w.shape, _jnp.float32)
        out["v_" + name] = (s * s) * _jax.random.uniform(kv, w.shape, _jnp.float32, 0.5, 1.5)
    if N_MICROBATCH > 1:
        for name, axis in PER_EXAMPLE_BATCH_AXIS.items():
            out[name] = _to_microbatches(out[name], axis)
    return {'x': out['x'], 'mem': out['mem'], 'positions': out['positions'], 'mix_norm': out['mix_norm'], 'mem_norm': out['mem_norm'], 'w_mem_kv': out['w_mem_kv'], 'ffn_norm': out['ffn_norm'], 'w_gate': out['w_gate'], 'w_up': out['w_up'], 'w_down': out['w_down'], 'attn_w_in': out['attn_w_in'], 'attn_w_out': out['attn_w_out'], 'sgu_w_in': out['sgu_w_in'], 'sgu_ln_g': out['sgu_ln_g'], 'sgu_ln_b': out['sgu_ln_b'], 'sgu_w_spatial': out['sgu_w_spatial'], 'sgu_b_spatial': out['sgu_b_spatial'], 'sgu_w_out': out['sgu_w_out'], 'final_norm': out['final_norm'], 'loss_target': out['loss_target'], 'm_mix_norm': out['m_mix_norm'], 'm_mem_norm': out['m_mem_norm'], 'm_w_mem_kv': out['m_w_mem_kv'], 'm_ffn_norm': out['m_ffn_norm'], 'm_w_gate': out['m_w_gate'], 'm_w_up': out['m_w_up'], 'm_w_down': out['m_w_down'], 'm_attn_w_in': out['m_attn_w_in'], 'm_attn_w_out': out['m_attn_w_out'], 'm_sgu_w_in': out['m_sgu_w_in'], 'm_sgu_ln_g': out['m_sgu_ln_g'], 'm_sgu_ln_b': out['m_sgu_ln_b'], 'm_sgu_w_spatial': out['m_sgu_w_spatial'], 'm_sgu_b_spatial': out['m_sgu_b_spatial'], 'm_sgu_w_out': out['m_sgu_w_out'], 'm_final_norm': out['m_final_norm'], 'v_mix_norm': out['v_mix_norm'], 'v_mem_norm': out['v_mem_norm'], 'v_w_mem_kv': out['v_w_mem_kv'], 'v_ffn_norm': out['v_ffn_norm'], 'v_w_gate': out['v_w_gate'], 'v_w_up': out['v_w_up'], 'v_w_down': out['v_w_down'], 'v_attn_w_in': out['v_attn_w_in'], 'v_attn_w_out': out['v_attn_w_out'], 'v_sgu_w_in': out['v_sgu_w_in'], 'v_sgu_ln_g': out['v_sgu_ln_g'], 'v_sgu_ln_b': out['v_sgu_ln_b'], 'v_sgu_w_spatial': out['v_sgu_w_spatial'], 'v_sgu_b_spatial': out['v_sgu_b_spatial'], 'v_sgu_w_out': out['v_sgu_w_out'], 'v_final_norm': out['v_final_norm']}


def _loss(weights, diff, rest, loss_target):
    with _jax.named_scope("forward"):
        args = {**rest, TWIN_DIFF_INPUT: diff, **{k: w.astype(_WEIGHT_DTYPES[k]) for k, w in weights.items()}}
        y = _forward(args)
    with _jax.named_scope("loss_head"):
        err = _jnp.square(y.astype(_jnp.float32) - loss_target)
        return 0.5 * _jnp.sum(_jnp.mean(err, axis=-1)) if err.ndim else 0.5 * err


def _adamw(w, g, m, v):
    m = ADAM_B1 * m + (1.0 - ADAM_B1) * g
    v = ADAM_B2 * v + (1.0 - ADAM_B2) * _jnp.square(g)
    m_hat = m / (1.0 - ADAM_B1 ** ADAM_STEP)
    v_hat = v / (1.0 - ADAM_B2 ** ADAM_STEP)
    delta = -ADAM_LR * (m_hat / (_jnp.sqrt(v_hat) + ADAM_EPS) + ADAM_WD * w)
    return delta, m, v


def reference(x, mem, positions, mix_norm, mem_norm, w_mem_kv, ffn_norm, w_gate, w_up, w_down, attn_w_in, attn_w_out, sgu_w_in, sgu_ln_g, sgu_ln_b, sgu_w_spatial, sgu_b_spatial, sgu_w_out, final_norm, loss_target, m_mix_norm, m_mem_norm, m_w_mem_kv, m_ffn_norm, m_w_gate, m_w_up, m_w_down, m_attn_w_in, m_attn_w_out, m_sgu_w_in, m_sgu_ln_g, m_sgu_ln_b, m_sgu_w_spatial, m_sgu_b_spatial, m_sgu_w_out, m_final_norm, v_mix_norm, v_mem_norm, v_w_mem_kv, v_ffn_norm, v_w_gate, v_w_up, v_w_down, v_attn_w_in, v_attn_w_out, v_sgu_w_in, v_sgu_ln_g, v_sgu_ln_b, v_sgu_w_spatial, v_sgu_b_spatial, v_sgu_w_out, v_final_norm):
    given = dict(x=x, mem=mem, positions=positions, mix_norm=mix_norm, mem_norm=mem_norm, w_mem_kv=w_mem_kv, ffn_norm=ffn_norm, w_gate=w_gate, w_up=w_up, w_down=w_down, attn_w_in=attn_w_in, attn_w_out=attn_w_out, sgu_w_in=sgu_w_in, sgu_ln_g=sgu_ln_g, sgu_ln_b=sgu_ln_b, sgu_w_spatial=sgu_w_spatial, sgu_b_spatial=sgu_b_spatial, sgu_w_out=sgu_w_out, final_norm=final_norm, loss_target=loss_target, m_mix_norm=m_mix_norm, m_mem_norm=m_mem_norm, m_w_mem_kv=m_w_mem_kv, m_ffn_norm=m_ffn_norm, m_w_gate=m_w_gate, m_w_up=m_w_up, m_w_down=m_w_down, m_attn_w_in=m_attn_w_in, m_attn_w_out=m_attn_w_out, m_sgu_w_in=m_sgu_w_in, m_sgu_ln_g=m_sgu_ln_g, m_sgu_ln_b=m_sgu_ln_b, m_sgu_w_spatial=m_sgu_w_spatial, m_sgu_b_spatial=m_sgu_b_spatial, m_sgu_w_out=m_sgu_w_out, m_final_norm=m_final_norm, v_mix_norm=v_mix_norm, v_mem_norm=v_mem_norm, v_w_mem_kv=v_w_mem_kv, v_ffn_norm=v_ffn_norm, v_w_gate=v_w_gate, v_w_up=v_w_up, v_w_down=v_w_down, v_attn_w_in=v_attn_w_in, v_attn_w_out=v_attn_w_out, v_sgu_w_in=v_sgu_w_in, v_sgu_ln_g=v_sgu_ln_g, v_sgu_ln_b=v_sgu_ln_b, v_sgu_w_spatial=v_sgu_w_spatial, v_sgu_b_spatial=v_sgu_b_spatial, v_sgu_w_out=v_sgu_w_out, v_final_norm=v_final_norm)
    weights = {n: given[n] for n in TWIN_WEIGHTS}
    shared = {n: given[n] for n in SHARED_INPUTS}
    per_example = {n: given[n] for n in ['x', 'mem', 'positions']}
    grad_fn = _jax.value_and_grad(_loss, argnums=(0, 1))

    def one_microbatch(ex, loss_target):
        ex = dict(ex)
        diff = ex.pop(TWIN_DIFF_INPUT)
        return grad_fn(weights, diff, {**shared, **ex}, loss_target)

    if N_MICROBATCH == 1:
        loss, (grad_w, grad_x) = one_microbatch(per_example, given["loss_target"])
    else:
        def body(carry, xs):
            loss_sum, grad_sum = carry
            l_k, (gw_k, gx_k) = one_microbatch(xs[0], xs[1])
            with _jax.named_scope("update"):
                return (loss_sum + l_k, _jax.tree.map(_jnp.add, grad_sum, gw_k)), gx_k

        init = (_jnp.zeros((), _jnp.float32), _jax.tree.map(_jnp.zeros_like, weights))
        (loss, grad_w), grad_x = _jax.lax.scan(body, init, (per_example, given["loss_target"]))
    with _jax.named_scope("update"):
        delta_w, new_m, new_v = {}, {}, {}
        for n in TWIN_WEIGHTS:
            delta_w[n], new_m[n], new_v[n] = _adamw(weights[n], grad_w[n], given["m_" + n], given["v_" + n])
    return (loss, grad_x, *[grad_w[n] for n in TWIN_WEIGHTS], *[delta_w[n] for n in TWIN_WEIGHTS],
            *[new_m[n] for n in TWIN_WEIGHTS], *[new_v[n] for n in TWIN_WEIGHTS])
```

```python
import functools
import math

import jax
import jax.numpy as jnp
from jax import lax
from jax.experimental import pallas as pl
from jax.experimental.pallas import tpu as pltpu

F32 = jnp.float32
BF = jnp.bfloat16
MESH = pl.DeviceIdType.MESH

HEAD_DIM = 128
MEM_HEADS = 4
MEM_W = MEM_HEADS * HEAD_DIM
GROUP_W = 4 * HEAD_DIM
DILATIONS = (1, 4, 16)
BLK = 128
SGU_GROUPS = 12
SGU_W = SGU_GROUPS * HEAD_DIM
ROT_HALF = 16
ROPE_THETA = 500000.0
NORM_EPS = 1e-6
LN_EPS = 1e-5
NEG_INF = -1e30
SCALE = HEAD_DIM ** -0.5
ADAM_LR, ADAM_B1, ADAM_B2, ADAM_EPS, ADAM_WD, ADAM_STEP = 0.001, 0.9, 0.999, 1e-08, 0.01, 10

VMEM_LIMIT = 48 * 2 ** 20
VMEM_TILE_BUDGET = 36 * 2 ** 20
N_CHIPS = 4
LANES = 128

NT_DIMS = (((1,), (1,)), ((), ()))
TN_DIMS = (((0,), (0,)), ((), ()))
NN_DIMS = (((1,), (0,)), ((), ()))


def _params(sem):
    return pltpu.CompilerParams(dimension_semantics=sem, vmem_limit_bytes=VMEM_LIMIT)


def _pick(n, cap):
    if n <= cap:
        return n
    best = None
    for t in range(LANES, cap + 1, LANES):
        if n % t == 0:
            best = t
    assert best is not None, (n, cap)
    return best


def _pick_rows(n, cap):
    t = min(n, cap)
    while n % t:
        t //= 2
    return t


def _mm(name, dims, a_list, a_specs, b_list, b_specs, pairs, n_acc, acc_shape, grid, extras, e_specs,
        out_shapes, out_specs, epilogue, alias=None):
    na, nb, ne, no = len(a_list), len(b_list), len(extras), len(out_shapes)
    nk = grid[-1]
    has_alias = alias is not None

    def body(*refs):
        a = refs[:na]
        b = refs[na:na + nb]
        e = refs[na + nb:na + nb + ne]
        off = na + nb + ne + (1 if has_alias else 0)
        o = refs[off:off + no]
        acc = refs[off + no:]
        k = pl.program_id(len(grid) - 1)

        @pl.when(k == 0)
        def _():
            for c in acc:
                c[...] = jnp.zeros_like(c)

        for ai, bi, ci in pairs:
            acc[ci][...] += lax.dot_general(a[ai][...].astype(BF), b[bi][...].astype(BF), dims,
                                            preferred_element_type=F32)

        @pl.when(k == nk - 1)
        def _():
            outs = epilogue([c[...] for c in acc], [r[...] for r in e])
            for r, v in zip(o, outs):
                r[...] = v.astype(r.dtype)

    ins = [*a_list, *b_list, *extras]
    in_specs = [*a_specs, *b_specs, *e_specs]
    aliases = {}
    if has_alias:
        aliases = {len(ins): 0}
        ins.append(alias)
        in_specs.append(pl.BlockSpec(memory_space=pl.ANY))
    sem = ("parallel",) * (len(grid) - 1) + ("arbitrary",)
    return pl.pallas_call(
        body, out_shape=out_shapes, grid=grid, in_specs=in_specs, out_specs=out_specs,
        scratch_shapes=[pltpu.VMEM(acc_shape, F32)] * n_acc, input_output_aliases=aliases, name=name,
        compiler_params=_params(sem))(*ins)


def _tile_bytes(blocks):
    return sum(2 * math.prod(s) * jnp.dtype(d).itemsize for s, d in blocks)


def _first(acc, extra):
    return [acc[0]]


class Weight:
    def __init__(self, arr, axis):
        self.arr, self.axis = arr, axis
        _, self.layers, self.rows, self.cols = arr.shape


def _mm_nn(name, a, w, layer, out_dtype=F32, extras=(), epilogue=_first, n_out=1, out_dtypes=None):
    m, kdim = a.shape
    out_dtypes = out_dtypes or [out_dtype] * n_out
    if w.axis == "col":
        n_total = N_CHIPS * w.cols
        tn = _pick(w.cols, 1408)
        tk = _pick(kdim, 512)
        ncb = w.cols // tn
        gn, gk = N_CHIPS * ncb, kdim // tk
        b_map = lambda i, n, k: (n // ncb, layer, k, n % ncb)
    else:
        n_total = w.cols
        tn = _pick(w.cols, 1024)
        tk = _pick(w.rows, 1408)
        nkb = w.rows // tk
        gn, gk = n_total // tn, N_CHIPS * nkb
        b_map = lambda i, n, k: (k // nkb, layer, k % nkb, n)
    for tm in (1024, 512, 256, 128):
        if m % tm:
            continue
        blocks = [((tm, tk), a.dtype), ((tk, tn), BF)] + [((tm, tn), e.dtype) for e in extras]
        blocks += [((tm, tn), d) for d in out_dtypes] + [((tm, tn), BF)]
        if _tile_bytes(blocks) <= VMEM_TILE_BUDGET:
            break
    o_spec = pl.BlockSpec((tm, tn), lambda i, n, k: (i, n))
    return _mm(
        name, NN_DIMS, [a], [pl.BlockSpec((tm, tk), lambda i, n, k: (i, k))],
        [w.arr], [pl.BlockSpec((None, None, tk, tn), b_map)], [(0, 0, 0)], 1, (tm, tn), (m // tm, gn, gk),
        list(extras), [o_spec] * len(extras),
        [jax.ShapeDtypeStruct((m, n_total), d) for d in out_dtypes], [o_spec] * len(out_dtypes), epilogue)


def _gate_up(name, h, wg, wu, layer):
    m, kdim = h.shape
    tn = _pick(wg.cols, 1408)
    tk = _pick(kdim, 512)
    ncb = wg.cols // tn
    tm = 512 if m % 512 == 0 else m
    b_spec = pl.BlockSpec((None, None, tk, tn), lambda i, n, k: (n // ncb, layer, k, n % ncb))
    o_spec = pl.BlockSpec((tm, tn), lambda i, n, k: (i, n))
    n_total = N_CHIPS * wg.cols

    def epilogue(acc, extra):
        g, u = acc
        return [g, u, g * (1.0 / (1.0 + jnp.exp(-g))) * u]

    return _mm(
        name, NN_DIMS, [h], [pl.BlockSpec((tm, tk), lambda i, n, k: (i, k))], [wg.arr, wu.arr], [b_spec, b_spec],
        [(0, 0, 0), (0, 1, 1)], 2, (tm, tn), (m // tm, N_CHIPS * ncb, kdim // tk), [], [],
        [jax.ShapeDtypeStruct((m, n_total), F32)] * 2 + [jax.ShapeDtypeStruct((m, n_total), BF)], [o_spec] * 3,
        epilogue)


def _mm_nt(name, dys, ws, layer, out_dtypes=(F32,), extras=(), epilogue=_first):
    m = dys[0].shape[0]
    w0 = ws[0]
    npair = len(dys)
    if w0.axis == "col":
        k_total = w0.rows
        tko = _pick(k_total, 1024)
        tkc = _pick(w0.cols, 1408)
        nkb = w0.cols // tkc
        go, gk = k_total // tko, N_CHIPS * nkb
        b_map = lambda i, o, k: (k // nkb, layer, o, k % nkb)
    else:
        k_total = N_CHIPS * w0.rows
        tko = _pick(w0.rows, 1408)
        tkc = _pick(w0.cols, 1024)
        nob = w0.rows // tko
        go, gk = N_CHIPS * nob, w0.cols // tkc
        b_map = lambda i, o, k: (o // nob, layer, o % nob, k)
    for tm in (1024, 512, 256, 128):
        if m % tm:
            continue
        blocks = [((tm, tkc), d.dtype) for d in dys] + [((tko, tkc), BF)] * npair
        blocks += [((tm, tko), e.dtype) for e in extras] + [((tm, tko), d) for d in out_dtypes]
        blocks += [((tm, tko), BF)]
        if _tile_bytes(blocks) <= VMEM_TILE_BUDGET:
            break
    o_spec = pl.BlockSpec((tm, tko), lambda i, o, k: (i, o))
    return _mm(
        name, NT_DIMS, list(dys), [pl.BlockSpec((tm, tkc), lambda i, o, k: (i, k))] * npair,
        [w.arr for w in ws], [pl.BlockSpec((None, None, tko, tkc), b_map)] * npair,
        [(i, i, 0) for i in range(npair)], 1, (tm, tko), (m // tm, go, gk), list(extras), [o_spec] * len(extras),
        [jax.ShapeDtypeStruct((m, k_total), d) for d in out_dtypes], [o_spec] * len(out_dtypes), epilogue)


def _mm_tn(name, a, dy, w, layer, prev=None):
    m = a.shape[0]
    two = w.layers == 2
    rows2 = w.rows if two else w.rows // 2
    tkr = _pick(rows2, 1024 if w.cols <= 1024 else 512)
    tn = _pick(w.cols, 1408)
    tmk = _pick_rows(m, 512)
    nrb = rows2 // tkr
    ncb = w.cols // tn
    if w.axis == "col":
        gr, gn = w.rows // tkr, N_CHIPS * ncb
        if two:
            o_map = lambda r, n, t: (layer, n // ncb, r, n % ncb)
        else:
            o_map = lambda r, n, t: (r // nrb, n // ncb, r % nrb, n % ncb)
    else:
        per = w.rows // tkr
        gr, gn = N_CHIPS * per, ncb
        if two:
            o_map = lambda r, n, t: (layer, r // per, r % per, n)
        else:
            o_map = lambda r, n, t: ((r % per) // nrb, r // per, (r % per) % nrb, n)
    return _mm(
        name, TN_DIMS, [a], [pl.BlockSpec((tmk, tkr), lambda r, n, t: (t, r))],
        [dy], [pl.BlockSpec((tmk, tn), lambda r, n, t: (t, n))], [(0, 0, 0)], 1, (tkr, tn), (gr, gn, m // tmk), [], [],
        [jax.ShapeDtypeStruct((2, N_CHIPS, rows2, w.cols), BF)], [pl.BlockSpec((None, None, tkr, tn), o_map)],
        _first, alias=prev)[0]


def _rms_fwd(name, x, g):
    s, d = x.shape
    tr = _pick_rows(s, 512)

    def body(x_ref, g_ref, h_ref):
        xf = x_ref[...]
        r = lax.rsqrt(jnp.mean(xf * xf, axis=-1, keepdims=True) + NORM_EPS)
        h_ref[...] = (xf * r * g_ref[...]).astype(BF)

    return pl.pallas_call(
        body, out_shape=jax.ShapeDtypeStruct((s, d), BF), grid=(s // tr,),
        in_specs=[pl.BlockSpec((tr, d), lambda i: (i, 0)), pl.BlockSpec((1, d), lambda i: (0, 0))],
        out_specs=pl.BlockSpec((tr, d), lambda i: (i, 0)), name=name, compiler_params=_params(("parallel",)))(x, g)


def _rms_bwd(name, x, g, dh, dres=None):
    s, d = x.shape
    tr = _pick_rows(s, 256)
    has_res = dres is not None

    def body(*refs):
        if has_res:
            x_ref, g_ref, dh_ref, dres_ref, dx_ref, dg_ref = refs
        else:
            x_ref, g_ref, dh_ref, dx_ref, dg_ref = refs
        xf = x_ref[...]
        r = lax.rsqrt(jnp.mean(xf * xf, axis=-1, keepdims=True) + NORM_EPS)
        xr = xf * r
        dy = dh_ref[...]
        a = dy * g_ref[...]
        dx = r * (a - xr * jnp.mean(a * xr, axis=-1, keepdims=True))
        if has_res:
            dx = dx + dres_ref[...]
        dx_ref[...] = dx

        @pl.when(pl.program_id(0) == 0)
        def _():
            dg_ref[...] = jnp.zeros_like(dg_ref)

        dg_ref[...] += jnp.sum(dy * xr, axis=0, keepdims=True)

    row = pl.BlockSpec((tr, d), lambda i: (i, 0))
    vec = pl.BlockSpec((1, d), lambda i: (0, 0))
    ins = [x, g, dh] + ([dres] if has_res else [])
    in_specs = [row, vec, row] + ([row] if has_res else [])
    return pl.pallas_call(
        body, out_shape=[jax.ShapeDtypeStruct((s, d), F32), jax.ShapeDtypeStruct((1, d), F32)], grid=(s // tr,),
        in_specs=in_specs, out_specs=[row, vec], name=name, compiler_params=_params(("arbitrary",)))(*ins)


def _final_loss(x, tgt, g):
    s, d = x.shape
    tr = _pick_rows(s, 256)

    def body(x_ref, t_ref, g_ref, dx_ref, dg_ref, loss_ref):
        xf = x_ref[...]
        gain = g_ref[...]
        r = lax.rsqrt(jnp.mean(xf * xf, axis=-1, keepdims=True) + NORM_EPS)
        xr = xf * r
        err = xr * gain - t_ref[...]
        dy = err * (1.0 / d)
        a = dy * gain
        dx_ref[...] = r * (a - xr * jnp.mean(a * xr, axis=-1, keepdims=True))

        @pl.when(pl.program_id(0) == 0)
        def _():
            dg_ref[...] = jnp.zeros_like(dg_ref)
            loss_ref[...] = jnp.zeros_like(loss_ref)

        dg_ref[...] += jnp.sum(dy * xr, axis=0, keepdims=True)
        part = 0.5 * jnp.sum(jnp.mean(err * err, axis=-1, keepdims=True), axis=0, keepdims=True)
        loss_ref[...] += jnp.broadcast_to(part, loss_ref.shape)

    row = pl.BlockSpec((tr, d), lambda i: (i, 0))
    vec = pl.BlockSpec((1, d), lambda i: (0, 0))
    return pl.pallas_call(
        body, out_shape=[jax.ShapeDtypeStruct((s, d), F32), jax.ShapeDtypeStruct((1, d), F32),
                         jax.ShapeDtypeStruct((8, LANES), F32)],
        grid=(s // tr,), in_specs=[row, row, vec], out_specs=[row, vec, pl.BlockSpec((8, LANES), lambda i: (0, 0))],
        name="final_loss", compiler_params=_params(("arbitrary",)))(x, tgt, g)


def _swiglu_bwd_epilogue(acc, extra):
    dact = acc[0]
    g, u = extra
    sig = 1.0 / (1.0 + jnp.exp(-g))
    return [dact * u * sig * (1.0 + g * (1.0 - sig)), dact * g * sig]


GELU_C = math.sqrt(2.0 / math.pi)
GELU_A = 0.044715


def _gelu(x):
    return 0.5 * x * (1.0 + jnp.tanh(GELU_C * (x + GELU_A * x * x * x)))


def _gelu_grad(x):
    t = jnp.tanh(GELU_C * (x + GELU_A * x * x * x))
    return 0.5 * (1.0 + t) + 0.5 * x * (1.0 - t * t) * GELU_C * (1.0 + 3.0 * GELU_A * x * x)


def _rope_tables(positions):
    inv_freq = ROPE_THETA ** (-jnp.arange(ROT_HALF, dtype=F32) / ROT_HALF)
    ang = positions.astype(F32)[:, None] * inv_freq
    cos, sin = jnp.cos(ang), jnp.sin(ang)
    s = ang.shape[0]
    rest = HEAD_DIM - 2 * ROT_HALF
    zeros = jnp.zeros((s, ROT_HALF), F32)
    cos_t = jnp.concatenate([cos, cos, jnp.ones((s, rest), F32)], axis=1)
    sin_a = jnp.concatenate([-sin, zeros, jnp.zeros((s, rest), F32)], axis=1)
    sin_b = jnp.concatenate([zeros, sin, jnp.zeros((s, rest), F32)], axis=1)
    return cos_t, sin_a, sin_b


def _rope_heads(x, cos_t, sin_a, sin_b):
    outs = []
    for h in range(GROUP_W // HEAD_DIM):
        xh = x[:, h * HEAD_DIM:(h + 1) * HEAD_DIM]
        up = pltpu.roll(xh, HEAD_DIM - ROT_HALF, 1)
        down = pltpu.roll(xh, ROT_HALF, 1)
        outs.append(xh * cos_t + up * sin_a + down * sin_b)
    return outs


def _rope_fwd(proj, tables):
    s = proj.shape[0]
    tm = _pick_rows(s, 512)
    nparts = 9

    def body(p_ref, c_ref, sa_ref, sb_ref, *outs):
        part = pl.program_id(1)
        for p in range(nparts):
            @pl.when(part == p)
            def _(p=p):
                xv = p_ref[...]
                if p < 6:
                    heads = _rope_heads(xv, c_ref[...], sa_ref[...], sb_ref[...])
                    for h, v in enumerate(heads):
                        outs[p][:, h * HEAD_DIM:(h + 1) * HEAD_DIM] = v.astype(BF)
                else:
                    outs[p][...] = xv.astype(BF)

    tab = pl.BlockSpec((tm, HEAD_DIM), lambda i, p: (i, 0))
    o_spec = pl.BlockSpec((tm, GROUP_W), lambda i, p: (i, 0))
    return pl.pallas_call(
        body, out_shape=[jax.ShapeDtypeStruct((s, GROUP_W), BF)] * nparts, grid=(s // tm, nparts),
        in_specs=[pl.BlockSpec((tm, GROUP_W), lambda i, p: (i, p)), tab, tab, tab], out_specs=[o_spec] * nparts,
        name="rope_fwd", compiler_params=_params(("parallel", "arbitrary")))(proj, *tables)


def _rope_bwd(parts, dq_mem, tables):
    s = dq_mem.shape[0]
    tm = _pick_rows(s, 256)
    nparts = len(parts) + 1

    def body(*refs):
        ins = refs[:nparts]
        c_ref, sa_ref, sb_ref, o_ref = refs[nparts:]
        part = pl.program_id(1)
        for p in range(nparts):
            @pl.when(part == p)
            def _(p=p):
                xv = ins[p][...]
                if p < 6:
                    heads = _rope_heads(xv, c_ref[...], -sa_ref[...], -sb_ref[...])
                    for h, v in enumerate(heads):
                        o_ref[:, h * HEAD_DIM:(h + 1) * HEAD_DIM] = v.astype(BF)
                else:
                    o_ref[...] = xv.astype(BF)

    i_spec = pl.BlockSpec((tm, GROUP_W), lambda i, p: (i, 0))
    tab = pl.BlockSpec((tm, HEAD_DIM), lambda i, p: (i, 0))
    return pl.pallas_call(
        body, out_shape=jax.ShapeDtypeStruct((s, nparts * GROUP_W), BF), grid=(s // tm, nparts),
        in_specs=[i_spec] * nparts + [tab] * 3, out_specs=pl.BlockSpec((tm, GROUP_W), lambda i, p: (i, p)),
        name="rope_bwd", compiler_params=_params(("parallel", "arbitrary")))(*parts, dq_mem, *tables)


def _band_masks(n):
    qi = lax.broadcasted_iota(jnp.int32, (BLK, BLK), 0)
    ki = lax.broadcasted_iota(jnp.int32, (BLK, BLK), 1)
    return qi >= ki, jnp.logical_and(ki >= qi, n > 0)


def _dil_fwd(g, q, k, v, dil):
    s = q.shape[0]
    length = s // dil
    nb = length // BLK
    view = lambda t: t.reshape(length, dil * GROUP_W)

    def body(q_ref, kc_ref, kp_ref, vc_ref, vp_ref, o_ref, lse_ref):
        n = pl.program_id(1)
        mask_c, mask_p = _band_masks(n)
        for h in range(GROUP_W // HEAD_DIM):
            sl = slice(h * HEAD_DIM, (h + 1) * HEAD_DIM)
            qh = q_ref[:, sl]
            sc = lax.dot_general(qh, kc_ref[:, sl], NT_DIMS, preferred_element_type=F32) * SCALE
            sp = lax.dot_general(qh, kp_ref[:, sl], NT_DIMS, preferred_element_type=F32) * SCALE
            sc = jnp.where(mask_c, sc, NEG_INF)
            sp = jnp.where(mask_p, sp, NEG_INF)
            mx = jnp.maximum(jnp.max(sc, axis=-1, keepdims=True), jnp.max(sp, axis=-1, keepdims=True))
            pc = jnp.exp(sc - mx)
            pp = jnp.exp(sp - mx)
            den = jnp.sum(pc, axis=-1, keepdims=True) + jnp.sum(pp, axis=-1, keepdims=True)
            acc = jnp.dot(pc.astype(BF), vc_ref[:, sl], preferred_element_type=F32)
            acc += jnp.dot(pp.astype(BF), vp_ref[:, sl], preferred_element_type=F32)
            o_ref[:, sl] = acc / den
            lse_ref[:, sl] = jnp.broadcast_to(mx + jnp.log(den), (BLK, HEAD_DIM))

    cur = pl.BlockSpec((BLK, GROUP_W), lambda r, n: (n, r))
    prev = pl.BlockSpec((BLK, GROUP_W), lambda r, n: (jnp.maximum(n - 1, 0), r))
    o, lse = pl.pallas_call(
        body, out_shape=[jax.ShapeDtypeStruct((length, dil * GROUP_W), F32)] * 2, grid=(dil, nb),
        in_specs=[cur, cur, prev, cur, prev], out_specs=[cur, cur], name=f"dil_fwd_{g}",
        compiler_params=_params(("parallel", "arbitrary")))(view(q), view(k), view(k), view(v), view(v))
    return o.reshape(s, GROUP_W), lse.reshape(s, GROUP_W)


def _dil_bwd(g, q, k, v, do, lse, delta, dil):
    s = q.shape[0]
    length = s // dil
    nb = length // BLK
    view = lambda t: t.reshape(length, dil * GROUP_W)

    def body(q_ref, kc_ref, kp_ref, vc_ref, vp_ref, do_ref, lse_ref, dl_ref, dq_ref, dk_ref, dv_ref, ck_ref, cv_ref):
        n = pl.program_id(1)
        live = n < nb
        mask_c, mask_p = _band_masks(n)
        mask_c = jnp.logical_and(mask_c, live)
        mask_p = jnp.logical_and(mask_p, live)

        @pl.when(n == 0)
        def _():
            ck_ref[...] = jnp.zeros_like(ck_ref)
            cv_ref[...] = jnp.zeros_like(cv_ref)

        for h in range(GROUP_W // HEAD_DIM):
            sl = slice(h * HEAD_DIM, (h + 1) * HEAD_DIM)
            qh, kc, kp, vc, vp, doh = q_ref[:, sl], kc_ref[:, sl], kp_ref[:, sl], vc_ref[:, sl], vp_ref[:, sl], do_ref[:, sl]
            lse_h = lse_ref[:, sl]
            dl_h = dl_ref[:, sl]
            sc = lax.dot_general(qh, kc, NT_DIMS, preferred_element_type=F32) * SCALE
            sp = lax.dot_general(qh, kp, NT_DIMS, preferred_element_type=F32) * SCALE
            pc = jnp.where(mask_c, jnp.exp(jnp.minimum(sc - lse_h, 0.0)), 0.0)
            pp = jnp.where(mask_p, jnp.exp(jnp.minimum(sp - lse_h, 0.0)), 0.0)
            dpc = lax.dot_general(doh, vc, NT_DIMS, preferred_element_type=F32)
            dpp = lax.dot_general(doh, vp, NT_DIMS, preferred_element_type=F32)
            dsc = (pc * (dpc - dl_h) * SCALE).astype(BF)
            dsp = (pp * (dpp - dl_h) * SCALE).astype(BF)
            dq = jnp.dot(dsc, kc, preferred_element_type=F32) + jnp.dot(dsp, kp, preferred_element_type=F32)

            @pl.when(live)
            def _(dq=dq, sl=sl):
                dq_ref[:, sl] = dq

            dk_ref[:, sl] = ck_ref[:, sl] + lax.dot_general(dsp, qh, TN_DIMS, preferred_element_type=F32)
            dv_ref[:, sl] = cv_ref[:, sl] + lax.dot_general(pp.astype(BF), doh, TN_DIMS, preferred_element_type=F32)
            ck_ref[:, sl] = lax.dot_general(dsc, qh, TN_DIMS, preferred_element_type=F32)
            cv_ref[:, sl] = lax.dot_general(pc.astype(BF), doh, TN_DIMS, preferred_element_type=F32)

    last = nb - 1
    cur = pl.BlockSpec((BLK, GROUP_W), lambda r, n: (jnp.minimum(n, last), r))
    prev = pl.BlockSpec((BLK, GROUP_W), lambda r, n: (jnp.maximum(n - 1, 0), r))
    shape = jax.ShapeDtypeStruct((length, dil * GROUP_W), F32)
    dq, dk, dv = pl.pallas_call(
        body, out_shape=[shape] * 3, grid=(dil, nb + 1),
        in_specs=[cur, cur, prev, cur, prev, cur, cur, cur], out_specs=[cur, prev, prev],
        scratch_shapes=[pltpu.VMEM((BLK, GROUP_W), F32)] * 2, name=f"dil_bwd_{g}",
        compiler_params=_params(("parallel", "arbitrary")))(
            view(q), view(k), view(k), view(v), view(v), view(do), view(lse), view(delta))
    return dq.reshape(s, GROUP_W), dk.reshape(s, GROUP_W), dv.reshape(s, GROUP_W)


def _attn_merge(outs, lses):
    s = outs[0].shape[0]
    tr = _pick_rows(s, 512)

    def body(o0, o1, o2, l0, l1, l2, m_ref, lse_ref):
        a, b, c = l0[...], l1[...], l2[...]
        mx = jnp.maximum(jnp.maximum(a, b), c)
        ea, eb, ec = jnp.exp(a - mx), jnp.exp(b - mx), jnp.exp(c - mx)
        den = ea + eb + ec
        m_ref[...] = (ea * o0[...] + eb * o1[...] + ec * o2[...]) / den
        lse_ref[...] = mx + jnp.log(den)

    spec = pl.BlockSpec((tr, GROUP_W), lambda i: (i, 0))
    return pl.pallas_call(
        body, out_shape=[jax.ShapeDtypeStruct((s, GROUP_W), F32)] * 2, grid=(s // tr,), in_specs=[spec] * 6,
        out_specs=[spec] * 2, name="attn_merge", compiler_params=_params(("parallel",)))(*outs, *lses)


def _attn_delta(dcat, merged):
    s = merged.shape[0]
    tr = _pick_rows(s, 512)

    def body(d_ref, m_ref, do_ref, dl_ref):
        d = d_ref[...]
        prod = d * m_ref[...]
        do_ref[...] = d.astype(BF)
        for h in range(GROUP_W // HEAD_DIM):
            sl = slice(h * HEAD_DIM, (h + 1) * HEAD_DIM)
            dl_ref[:, sl] = jnp.broadcast_to(jnp.sum(prod[:, sl], axis=-1, keepdims=True), (tr, HEAD_DIM))

    spec = pl.BlockSpec((tr, GROUP_W), lambda i: (i, 0))
    return pl.pallas_call(
        body, out_shape=[jax.ShapeDtypeStruct((s, GROUP_W), BF), jax.ShapeDtypeStruct((s, GROUP_W), F32)],
        grid=(s // tr,), in_specs=[spec, spec], out_specs=[spec, spec], name="attn_delta",
        compiler_params=_params(("parallel",)))(dcat, merged)


def _mem_probs(qh, kh):
    sc = lax.dot_general(qh, kh, NT_DIMS, preferred_element_type=F32) * SCALE
    p = jnp.exp(sc - jnp.max(sc, axis=-1, keepdims=True))
    return p, jnp.sum(p, axis=-1, keepdims=True)


def _mem_fwd(name, proj, q_block, kv):
    s = proj.shape[0]
    tq = _pick_rows(s, 512)

    def body(q_ref, kv_ref, o_ref):
        for h in range(MEM_HEADS):
            sl = slice(h * HEAD_DIM, (h + 1) * HEAD_DIM)
            vsl = slice(MEM_W + h * HEAD_DIM, MEM_W + (h + 1) * HEAD_DIM)
            p, den = _mem_probs(q_ref[:, sl].astype(BF), kv_ref[:, sl].astype(BF))
            o_ref[:, sl] = jnp.dot(p.astype(BF), kv_ref[:, vsl].astype(BF), preferred_element_type=F32) / den

    return pl.pallas_call(
        body, out_shape=jax.ShapeDtypeStruct((s, MEM_W), F32), grid=(s // tq,),
        in_specs=[pl.BlockSpec((tq, MEM_W), lambda i: (i, q_block)), pl.BlockSpec(kv.shape, lambda i: (0, 0))],
        out_specs=pl.BlockSpec((tq, MEM_W), lambda i: (i, 0)), name=name, compiler_params=_params(("parallel",)))(proj, kv)


def _mem_bwd(name, proj, q_block, kv, dcat, d_block):
    s = proj.shape[0]
    tq = _pick_rows(s, 512)

    def body(q_ref, kv_ref, do_ref, dq_ref, dkv_ref):
        @pl.when(pl.program_id(0) == 0)
        def _():
            dkv_ref[...] = jnp.zeros_like(dkv_ref)

        for h in range(MEM_HEADS):
            sl = slice(h * HEAD_DIM, (h + 1) * HEAD_DIM)
            vsl = slice(MEM_W + h * HEAD_DIM, MEM_W + (h + 1) * HEAD_DIM)
            qh, kh, vh = q_ref[:, sl].astype(BF), kv_ref[:, sl].astype(BF), kv_ref[:, vsl].astype(BF)
            doh = do_ref[:, sl].astype(BF)
            p, den = _mem_probs(qh, kh)
            p = p / den
            dp = lax.dot_general(doh, vh, NT_DIMS, preferred_element_type=F32)
            ds = (p * (dp - jnp.sum(p * dp, axis=-1, keepdims=True)) * SCALE).astype(BF)
            dq_ref[:, sl] = jnp.dot(ds, kh, preferred_element_type=F32)
            dkv_ref[:, sl] += lax.dot_general(ds, qh, TN_DIMS, preferred_element_type=F32)
            dkv_ref[:, vsl] += lax.dot_general(p.astype(BF), doh, TN_DIMS, preferred_element_type=F32)

    whole = pl.BlockSpec(kv.shape, lambda i: (0, 0))
    return pl.pallas_call(
        body, out_shape=[jax.ShapeDtypeStruct((s, MEM_W), F32), jax.ShapeDtypeStruct(kv.shape, F32)], grid=(s // tq,),
        in_specs=[pl.BlockSpec((tq, MEM_W), lambda i: (i, q_block)), whole,
                  pl.BlockSpec((tq, MEM_W), lambda i: (i, d_block))],
        out_specs=[pl.BlockSpec((tq, MEM_W), lambda i: (i, 0)), whole], name=name,
        compiler_params=_params(("arbitrary",)))(proj, kv, dcat)


def _causal():
    t = lax.broadcasted_iota(jnp.int32, (BLK, BLK), 0)
    s = lax.broadcasted_iota(jnp.int32, (BLK, BLK), 1)
    return t >= s


def _sgu_norm(v_pre, ln_g, ln_b):
    vg = _gelu(v_pre)
    mu = jnp.mean(vg, axis=-1, keepdims=True)
    cen = vg - mu
    rstd = lax.rsqrt(jnp.mean(cen * cen, axis=-1, keepdims=True) + LN_EPS)
    xhat = cen * rstd
    return xhat, rstd, xhat * ln_g + ln_b


def _sgu_fwd(proj, ln_g, ln_b, w_sp, b_t):
    s = proj.shape[0]

    def body(u_ref, v_ref, g_ref, b_ref, w_ref, bt_ref, o_ref):
        _, _, vn = _sgu_norm(v_ref[...], g_ref[...], b_ref[...])
        vn = vn.astype(BF)
        tri = _causal()
        for grp in range(SGU_GROUPS):
            sl = slice(grp * HEAD_DIM, (grp + 1) * HEAD_DIM)
            w = jnp.where(tri, w_ref[grp], 0.0).astype(BF)
            mixed = jnp.dot(w, vn[:, sl], preferred_element_type=F32) + bt_ref[:, grp:grp + 1]
            o_ref[:, sl] = _gelu(u_ref[:, sl]) * mixed

    vec = pl.BlockSpec((1, SGU_W), lambda i: (0, 0))
    return pl.pallas_call(
        body, out_shape=jax.ShapeDtypeStruct((s, SGU_W), F32), grid=(s // BLK,),
        in_specs=[pl.BlockSpec((BLK, SGU_W), lambda i: (i, 0)), pl.BlockSpec((BLK, SGU_W), lambda i: (i, 1)), vec, vec,
                  pl.BlockSpec(w_sp.shape, lambda i: (0, 0, 0)), pl.BlockSpec(b_t.shape, lambda i: (0, 0))],
        out_specs=pl.BlockSpec((BLK, SGU_W), lambda i: (i, 0)), name="sgu_fwd",
        compiler_params=_params(("parallel",)))(proj, proj, ln_g, ln_b, w_sp, b_t)


def _sgu_bwd(proj, dcat, ln_g, ln_b, w_sp, b_t):
    s = proj.shape[0]

    def body(u_ref, v_ref, d_ref, g_ref, b_ref, w_ref, bt_ref, du_ref, dv_ref, dw_ref, db_ref, dg_ref, dbeta_ref,
             dvn_ref):
        @pl.when(pl.program_id(0) == 0)
        def _():
            dw_ref[...] = jnp.zeros_like(dw_ref)
            db_ref[...] = jnp.zeros_like(db_ref)
            dg_ref[...] = jnp.zeros_like(dg_ref)
            dbeta_ref[...] = jnp.zeros_like(dbeta_ref)

        v_pre = v_ref[...]
        gain = g_ref[...]
        xhat, rstd, vn = _sgu_norm(v_pre, gain, b_ref[...])
        vn = vn.astype(BF)
        tri = _causal()
        lane = lax.broadcasted_iota(jnp.int32, (BLK, HEAD_DIM), 1)
        db_acc = jnp.zeros((BLK, HEAD_DIM), F32)
        for grp in range(SGU_GROUPS):
            sl = slice(grp * HEAD_DIM, (grp + 1) * HEAD_DIM)
            w = jnp.where(tri, w_ref[grp], 0.0).astype(BF)
            vn_g = vn[:, sl]
            mixed = jnp.dot(w, vn_g, preferred_element_type=F32) + bt_ref[:, grp:grp + 1]
            u_pre = u_ref[:, sl]
            d_out = d_ref[:, sl]
            du_ref[:, sl] = (d_out * mixed * _gelu_grad(u_pre)).astype(BF)
            dmixed = d_out * _gelu(u_pre)
            dm = dmixed.astype(BF)
            dvn_ref[:, sl] = lax.dot_general(w, dm, TN_DIMS, preferred_element_type=F32)
            dw = lax.dot_general(dm, vn_g, NT_DIMS, preferred_element_type=F32)
            dw_ref[grp] += jnp.where(tri, dw, 0.0)
            db_acc += jnp.where(lane == grp, jnp.sum(dmixed, axis=-1, keepdims=True), 0.0)
        db_ref[...] += db_acc
        dvn = dvn_ref[...]
        dg_ref[...] += jnp.sum(dvn * xhat, axis=0, keepdims=True)
        dbeta_ref[...] += jnp.sum(dvn, axis=0, keepdims=True)
        dxh = dvn * gain
        dvg = rstd * (dxh - jnp.mean(dxh, axis=-1, keepdims=True) - xhat * jnp.mean(dxh * xhat, axis=-1, keepdims=True))
        dv_ref[...] = (dvg * _gelu_grad(v_pre)).astype(BF)

    vec = pl.BlockSpec((1, SGU_W), lambda i: (0, 0))
    row = pl.BlockSpec((BLK, SGU_W), lambda i: (i, 0))
    w_spec = pl.BlockSpec(w_sp.shape, lambda i: (0, 0, 0))
    sq = pl.BlockSpec((BLK, HEAD_DIM), lambda i: (0, 0))
    return pl.pallas_call(
        body,
        out_shape=[jax.ShapeDtypeStruct((s, SGU_W), BF), jax.ShapeDtypeStruct((s, SGU_W), BF),
                   jax.ShapeDtypeStruct(w_sp.shape, F32), jax.ShapeDtypeStruct((BLK, HEAD_DIM), F32),
                   jax.ShapeDtypeStruct((1, SGU_W), F32), jax.ShapeDtypeStruct((1, SGU_W), F32)],
        grid=(s // BLK,),
        in_specs=[row, pl.BlockSpec((BLK, SGU_W), lambda i: (i, 1)), row, vec, vec, w_spec,
                  pl.BlockSpec(b_t.shape, lambda i: (0, 0))],
        out_specs=[row, row, w_spec, sq, vec, vec], scratch_shapes=[pltpu.VMEM((BLK, SGU_W), F32)], name="sgu_bwd",
        compiler_params=_params(("arbitrary",)))(proj, proj, dcat, ln_g, ln_b, w_sp, b_t)


def _place():
    return lax.axis_index("x"), lax.axis_index("y"), lax.axis_index("c")


def _other_chips(x, y):
    return [(1 - x, y), (x, 1 - y), (1 - x, 1 - y)]


ANY = pl.BlockSpec(memory_space=pl.ANY)


def _all_gather(shards):
    n = len(shards)

    def body(*refs):
        ins, outs = refs[:n], refs[n:2 * n]
        local_sems, send_sems, recv_sems, pass_send, pass_recv = refs[2 * n:]
        x, y, c = _place()
        me = 2 * x + y
        sibling = (x, y, 1 - c)
        chips = _other_chips(x, y)

        def remote(src, dst, ssem, rsem, to):
            return pltpu.make_async_remote_copy(src_ref=src, dst_ref=dst, send_sem=ssem, recv_sem=rsem, device_id=to,
                                                device_id_type=MESH)

        started = []
        for a in range(n):
            cp = pltpu.make_async_copy(ins[a], outs[a].at[me], local_sems.at[a])
            cp.start()
            started.append(cp)
        sends = []
        for a in range(n):
            for k, (ox, oy) in enumerate(chips):
                cp = remote(ins[a].at[c], outs[a].at[me, c], send_sems.at[3 * a + k], recv_sems.at[3 * a + k], (ox, oy, c))
                cp.start()
                sends.append(cp)
        for a in range(n):
            for k, (ox, oy) in enumerate(chips):
                slab = outs[a].at[2 * ox + oy, c]
                remote(slab, slab, send_sems.at[3 * a + k], recv_sems.at[3 * a + k], (ox, oy, c)).wait_recv()
                cp = remote(slab, slab, pass_send.at[3 * a + k], pass_recv.at[3 * a + k], sibling)
                cp.start()
                sends.append(cp)
        for a in range(n):
            for k, (ox, oy) in enumerate(chips):
                slab = outs[a].at[2 * ox + oy, 1 - c]
                remote(slab, slab, pass_send.at[3 * a + k], pass_recv.at[3 * a + k], sibling).wait_recv()
        for cp in sends:
            cp.wait_send()
        for cp in started:
            cp.wait()

    return pl.pallas_call(
        body, out_shape=[jax.ShapeDtypeStruct((N_CHIPS, *s.shape), s.dtype) for s in shards], in_specs=[ANY] * n,
        out_specs=[ANY] * n,
        scratch_shapes=[pltpu.SemaphoreType.DMA((n,))] + [pltpu.SemaphoreType.DMA((3 * n,))] * 4,
        name="weight_all_gather")(*shards)


def _pair_exchange(grads, small):
    n = len(grads)

    def body(*refs):
        ins, small_ref = refs[:n], refs[n]
        outs, all_ref = refs[n + 1:2 * n + 1], refs[2 * n + 1]
        ssem, rsem, small_local, small_send, small_recv = refs[2 * n + 2:]
        x, y, c = _place()
        me = 4 * x + 2 * y + c
        sibling = (x, y, 1 - c)
        copies = []
        for a in range(n):
            cp = pltpu.make_async_remote_copy(src_ref=ins[a].at[1 - c], dst_ref=outs[a], send_sem=ssem.at[a],
                                              recv_sem=rsem.at[a], device_id=sibling, device_id_type=MESH)
            cp.start()
            copies.append(cp)
        own = pltpu.make_async_copy(small_ref, all_ref.at[me], small_local)
        own.start()
        for mask in range(1, 8):
            fx, fy, fc = (mask >> 2) & 1, (mask >> 1) & 1, mask & 1
            peer = (1 - x if fx else x, 1 - y if fy else y, 1 - c if fc else c)
            cp = pltpu.make_async_remote_copy(src_ref=small_ref, dst_ref=all_ref.at[me], send_sem=small_send.at[mask - 1],
                                              recv_sem=small_recv.at[mask - 1], device_id=peer, device_id_type=MESH)
            cp.start()
            copies.append(cp)
        for cp in copies:
            cp.wait()
        own.wait()

    out_shape = [jax.ShapeDtypeStruct(g.shape[1:], g.dtype) for g in grads]
    out_shape.append(jax.ShapeDtypeStruct((8, *small.shape), small.dtype))
    res = pl.pallas_call(
        body, out_shape=out_shape, in_specs=[ANY] * (n + 1), out_specs=[ANY] * (n + 1),
        scratch_shapes=[pltpu.SemaphoreType.DMA((n,)), pltpu.SemaphoreType.DMA((n,)), pltpu.SemaphoreType.DMA,
                        pltpu.SemaphoreType.DMA((7,)), pltpu.SemaphoreType.DMA((7,))],
        name="grad_pair_exchange")(*grads, small)
    return res[:n], res[n]


def _chip_exchange(pair_sums):
    n = len(pair_sums)

    def body(*refs):
        ins, outs = refs[:n], refs[n:2 * n]
        ssem, rsem = refs[2 * n:]
        x, y, c = _place()
        copies = []
        for a in range(n):
            for k, (ox, oy) in enumerate(_other_chips(x, y)):
                cp = pltpu.make_async_remote_copy(src_ref=ins[a].at[2 * ox + oy], dst_ref=outs[a].at[k],
                                                  send_sem=ssem.at[3 * a + k], recv_sem=rsem.at[3 * a + k],
                                                  device_id=(ox, oy, c), device_id_type=MESH)
                cp.start()
                copies.append(cp)
        for cp in copies:
            cp.wait()

    return pl.pallas_call(
        body, out_shape=[jax.ShapeDtypeStruct((3, *p.shape[1:]), p.dtype) for p in pair_sums], in_specs=[ANY] * n,
        out_specs=[ANY] * n, scratch_shapes=[pltpu.SemaphoreType.DMA((3 * n,))] * 2, name="grad_chip_exchange")(*pair_sums)


def _half_exchange(totals):
    n = len(totals)

    def body(*refs):
        ins, outs = refs[:n], refs[n:2 * n]
        lsem, ssem, rsem = refs[2 * n:]
        x, y, c = _place()
        copies = []
        for a in range(n):
            own = pltpu.make_async_copy(ins[a], outs[a].at[c], lsem.at[a])
            own.start()
            cp = pltpu.make_async_remote_copy(src_ref=ins[a], dst_ref=outs[a].at[c], send_sem=ssem.at[a],
                                              recv_sem=rsem.at[a], device_id=(x, y, 1 - c), device_id_type=MESH)
            cp.start()
            copies += [own, cp]
        for cp in copies:
            cp.wait()

    return pl.pallas_call(
        body, out_shape=[jax.ShapeDtypeStruct((2, *t.shape), t.dtype) for t in totals], in_specs=[ANY] * n,
        out_specs=[ANY] * n, scratch_shapes=[pltpu.SemaphoreType.DMA((n,))] * 3, name="grad_half_exchange")(*totals)


def _pair_sum(name, grad, got, place):
    _, _, rows, cols = grad.shape
    tr = _pick_rows(rows, 256)

    def body(p_ref, g_ref, r_ref, o_ref):
        o_ref[...] = (g_ref[...].astype(F32) + r_ref[...].astype(F32)).astype(BF)

    grid_spec = pltpu.PrefetchScalarGridSpec(
        num_scalar_prefetch=1, grid=(N_CHIPS, rows // tr),
        in_specs=[pl.BlockSpec((None, None, tr, cols), lambda j, i, p: (p[0], j, i, 0)),
                  pl.BlockSpec((None, tr, cols), lambda j, i, p: (j, i, 0))],
        out_specs=pl.BlockSpec((None, tr, cols), lambda j, i, p: (j, i, 0)))
    return pl.pallas_call(body, out_shape=jax.ShapeDtypeStruct((N_CHIPS, rows, cols), BF), grid_spec=grid_spec, name=name,
                          compiler_params=_params(("parallel", "parallel")))(place, grad, got)


def _chip_sum(name, grad, got_pair, got_chips, place):
    _, _, rows, cols = grad.shape
    tr = _pick_rows(rows, 256)

    def body(p_ref, g_ref, r_ref, b_ref, o_ref):
        tot = g_ref[...].astype(F32) + r_ref[...].astype(F32)
        for k in range(3):
            tot = tot + b_ref[k].astype(F32)
        o_ref[...] = tot

    grid_spec = pltpu.PrefetchScalarGridSpec(
        num_scalar_prefetch=1, grid=(rows // tr,),
        in_specs=[pl.BlockSpec((None, None, tr, cols), lambda i, p: (p[0], p[1], i, 0)),
                  pl.BlockSpec((None, tr, cols), lambda i, p: (p[1], i, 0)),
                  pl.BlockSpec((3, tr, cols), lambda i, p: (0, i, 0))],
        out_specs=pl.BlockSpec((tr, cols), lambda i, p: (i, 0)))
    return pl.pallas_call(body, out_shape=jax.ShapeDtypeStruct((rows, cols), F32), grid_spec=grid_spec, name=name,
                          compiler_params=_params(("parallel",)))(place, grad, got_pair, got_chips)


def _sum_devices(stacked):
    _, rows, lanes = stacked.shape
    tr = _pick_rows(rows, 512)

    def body(s_ref, o_ref):
        tot = s_ref[0]
        for k in range(1, 8):
            tot = tot + s_ref[k]
        o_ref[...] = tot

    return pl.pallas_call(
        body, out_shape=jax.ShapeDtypeStruct((rows, lanes), F32), grid=(rows // tr,),
        in_specs=[pl.BlockSpec((8, tr, lanes), lambda i: (0, i, 0))], out_specs=pl.BlockSpec((tr, lanes), lambda i: (i, 0)),
        name="small_grad_sum", compiler_params=_params(("parallel",)))(stacked)


def _adamw(name, w, g, m, v):
    rows, cols = w.shape
    tr = _pick_rows(rows, 256)
    c1 = 1.0 - ADAM_B1 ** ADAM_STEP
    c2 = 1.0 - ADAM_B2 ** ADAM_STEP

    def body(w_ref, g_ref, m_ref, v_ref, d_ref, nm_ref, nv_ref):
        gv = g_ref[...]
        nm = ADAM_B1 * m_ref[...] + (1.0 - ADAM_B1) * gv
        nv = ADAM_B2 * v_ref[...] + (1.0 - ADAM_B2) * (gv * gv)
        d_ref[...] = -ADAM_LR * ((nm / c1) / (jnp.sqrt(nv / c2) + ADAM_EPS) + ADAM_WD * w_ref[...])
        nm_ref[...] = nm
        nv_ref[...] = nv

    spec = pl.BlockSpec((tr, cols), lambda i: (i, 0))
    return pl.pallas_call(
        body, out_shape=[jax.ShapeDtypeStruct((rows, cols), F32)] * 3, grid=(rows // tr,), in_specs=[spec] * 4,
        out_specs=[spec] * 3, name=name, compiler_params=_params(("parallel",)))(w, g, m, v)


def _pack(vectors, pad_rows):
    flat = jnp.concatenate([t.reshape(-1) for t in vectors])
    rows = -(-flat.shape[0] // LANES)
    rows = -(-rows // pad_rows) * pad_rows
    return jnp.pad(flat, (0, rows * LANES - flat.shape[0])).reshape(rows, LANES)


def _unpack(packed, shapes):
    flat = packed.reshape(-1)
    out, off = [], 0
    for shp in shapes:
        size = math.prod(shp)
        out.append(flat[off:off + size].reshape(shp))
        off += size
    return out


def kernel(x, mem, positions, mix_norm, mem_norm, w_mem_kv, ffn_norm, w_gate, w_up, w_down, attn_w_in, attn_w_out, sgu_w_in, sgu_ln_g, sgu_ln_b, sgu_w_spatial, sgu_b_spatial, sgu_w_out, final_norm, loss_target, m_mix_norm, m_mem_norm, m_w_mem_kv, m_ffn_norm, m_w_gate, m_w_up, m_w_down, m_attn_w_in, m_attn_w_out, m_sgu_w_in, m_sgu_ln_g, m_sgu_ln_b, m_sgu_w_spatial, m_sgu_b_spatial, m_sgu_w_out, m_final_norm, v_mix_norm, v_mem_norm, v_w_mem_kv, v_ffn_norm, v_w_gate, v_w_up, v_w_down, v_attn_w_in, v_attn_w_out, v_sgu_w_in, v_sgu_ln_g, v_sgu_ln_b, v_sgu_w_spatial, v_sgu_b_spatial, v_sgu_w_out, v_final_norm):
    seq, d_model = x.shape[1], x.shape[2]
    x0, mem0, tgt = x[0], mem[0], loss_target[0]
    xi, yi, ci = _place()
    chip = 2 * xi + yi
    place = jnp.stack([ci, chip]).astype(jnp.int32)

    big = {"attn_w_in": (attn_w_in, "col"), "w_mem_kv": (w_mem_kv, "row"), "attn_w_out": (attn_w_out, "col"),
           "w_gate": (w_gate, "col"), "w_up": (w_up, "col"), "w_down": (w_down, "row"),
           "sgu_w_in": (sgu_w_in, "col"), "sgu_w_out": (sgu_w_out, "row")}
    names = list(big)
    halves = [big[k][0].astype(BF).reshape(2, -1, big[k][0].shape[-1]) for k in names]
    ln_pack = jnp.stack([sgu_ln_g, sgu_ln_b])
    gathered = _all_gather(halves + [ln_pack])
    weights = {}
    for k, gat in zip(names, gathered):
        shard = big[k][0]
        weights[k] = Weight(gat.reshape(N_CHIPS, *shard.shape), big[k][1])
    ln_all = gathered[-1]
    ln_g = ln_all[:, 0, 0, :].reshape(1, SGU_W)
    ln_b = ln_all[:, 1, 0, :].reshape(1, SGU_W)
    w_sp = sgu_w_spatial[0]
    b_t = sgu_b_spatial[0].T

    tables = _rope_tables(positions[0])

    def residual(acc, extra):
        return [extra[0] + acc[0]]

    def memory_kv(layer):
        mem_n = _rms_fwd(f"mem_norm_{layer}", mem0, mem_norm[layer:layer + 1])
        return mem_n, _mm_nn(f"mem_kv_{layer}", mem_n, weights["w_mem_kv"], layer)[0]

    def ffn_fwd(layer, xin):
        h = _rms_fwd(f"ffn_norm_{layer}", xin, ffn_norm[layer:layer + 1])
        g, u, act = _gate_up(f"gate_up_{layer}", h, weights["w_gate"], weights["w_up"], layer)
        xout = _mm_nn(f"down_{layer}", act, weights["w_down"], layer, extras=[xin], epilogue=residual)[0]
        return xout, (h, g, u, act)

    h0 = _rms_fwd("mix_norm_0", x0, mix_norm[0:1])
    proj0 = _mm_nn("attn_in", h0, weights["attn_w_in"], 0)[0]
    qkv = _rope_fwd(proj0, tables)
    qs, ks, vs = qkv[0:3], qkv[3:6], qkv[6:9]
    outs, lses = [], []
    for g, dil in enumerate(DILATIONS):
        o, l = _dil_fwd(g, qs[g], ks[g], vs[g], dil)
        outs.append(o)
        lses.append(l)
    merged, lse = _attn_merge(outs, lses)
    mem_n0, kv0 = memory_kv(0)
    mem_out0 = _mem_fwd("mem_fwd_0", proj0, 9, kv0)
    cat0 = jnp.concatenate([merged, mem_out0], axis=1)
    x1 = _mm_nn("attn_out", cat0, weights["attn_w_out"], 0, extras=[x0], epilogue=residual)[0]
    x2, ffn_saved0 = ffn_fwd(0, x1)

    h1 = _rms_fwd("mix_norm_1", x2, mix_norm[1:2])
    proj1 = _mm_nn("sgu_in", h1, weights["sgu_w_in"], 0)[0]
    sgu_out = _sgu_fwd(proj1, ln_g, ln_b, w_sp, b_t)
    mem_n1, kv1 = memory_kv(1)
    mem_out1 = _mem_fwd("mem_fwd_1", proj1, 6, kv1)
    cat1 = jnp.concatenate([sgu_out, mem_out1], axis=1)
    x3 = _mm_nn("sgu_out", cat1, weights["sgu_w_out"], 0, extras=[x2], epilogue=residual)[0]
    x4, ffn_saved1 = ffn_fwd(1, x3)

    d4, g_final, loss_part = _final_loss(x4, tgt, final_norm.reshape(1, d_model))
    loss = lax.psum(loss_part[0, 0], ("x", "y", "c"))

    grads = {}

    def ffn_bwd(layer, d_out, xin, saved):
        h, g, u, act = saved
        grads["w_down"] = _mm_tn(f"d_down_{layer}", act, d_out, weights["w_down"], layer, grads.get("w_down"))
        dg, du = _mm_nt(f"d_act_{layer}", [d_out], [weights["w_down"]], layer, out_dtypes=(BF, BF), extras=[g, u],
                        epilogue=_swiglu_bwd_epilogue)
        grads["w_gate"] = _mm_tn(f"d_gate_{layer}", h, dg, weights["w_gate"], layer, grads.get("w_gate"))
        grads["w_up"] = _mm_tn(f"d_up_{layer}", h, du, weights["w_up"], layer, grads.get("w_up"))
        dh = _mm_nt(f"d_ffn_h_{layer}", [dg, du], [weights["w_gate"], weights["w_up"]], layer)[0]
        return _rms_bwd(f"ffn_norm_bwd_{layer}", xin, ffn_norm[layer:layer + 1], dh, d_out)

    def memory_bwd(layer, mem_n, dkv):
        dkv = dkv.astype(BF)
        grads["w_mem_kv"] = _mm_tn(f"d_mem_kv_{layer}", mem_n, dkv, weights["w_mem_kv"], layer, grads.get("w_mem_kv"))
        d_mem_n = _mm_nt(f"d_mem_n_{layer}", [dkv], [weights["w_mem_kv"]], layer)[0]
        return _rms_bwd(f"mem_norm_bwd_{layer}", mem0, mem_norm[layer:layer + 1], d_mem_n)[1]

    d3, g_ffn1 = ffn_bwd(1, d4, x3, ffn_saved1)
    grads["sgu_w_out"] = _mm_tn("d_sgu_out", cat1, d3, weights["sgu_w_out"], 0)
    dcat1 = _mm_nt("d_cat_1", [d3], [weights["sgu_w_out"]], 0)[0]
    dq_mem1, dkv1 = _mem_bwd("mem_bwd_1", proj1, 6, kv1, dcat1, 3)
    g_mem1 = memory_bwd(1, mem_n1, dkv1)
    du_pre, dv_pre, g_wsp, g_bsp_t, g_ln_g, g_ln_b = _sgu_bwd(proj1, dcat1, ln_g, ln_b, w_sp, b_t)
    dproj1 = jnp.concatenate([du_pre, dv_pre, dq_mem1.astype(BF)], axis=1)
    grads["sgu_w_in"] = _mm_tn("d_sgu_in", h1, dproj1, weights["sgu_w_in"], 0)
    dh1 = _mm_nt("d_h_1", [dproj1], [weights["sgu_w_in"]], 0)[0]
    d2, g_mix1 = _rms_bwd("mix_norm_bwd_1", x2, mix_norm[1:2], dh1, d3)

    d1, g_ffn0 = ffn_bwd(0, d2, x1, ffn_saved0)
    grads["attn_w_out"] = _mm_tn("d_attn_out", cat0, d1, weights["attn_w_out"], 0)
    dcat0 = _mm_nt("d_cat_0", [d1], [weights["attn_w_out"]], 0)[0]
    dq_mem0, dkv0 = _mem_bwd("mem_bwd_0", proj0, 9, kv0, dcat0, 1)
    g_mem0 = memory_bwd(0, mem_n0, dkv0)
    d_merged, delta = _attn_delta(dcat0, merged)
    dqs, dks, dvs = [], [], []
    for g, dil in enumerate(DILATIONS):
        dq, dk, dv = _dil_bwd(g, qs[g], ks[g], vs[g], d_merged, lse, delta, dil)
        dqs.append(dq)
        dks.append(dk)
        dvs.append(dv)
    dproj0 = _rope_bwd(dqs + dks + dvs, dq_mem0, tables)
    grads["attn_w_in"] = _mm_tn("d_attn_in", h0, dproj0, weights["attn_w_in"], 0)
    dh0 = _mm_nt("d_h_0", [dproj0], [weights["attn_w_in"]], 0)[0]
    d0, g_mix0 = _rms_bwd("mix_norm_bwd_0", x0, mix_norm[0:1], dh0, d1)

    small_grads = [jnp.concatenate([g_mix0, g_mix1]), jnp.concatenate([g_mem0, g_mem1]),
                   jnp.concatenate([g_ffn0, g_ffn1]), g_wsp, g_bsp_t[:, :SGU_GROUPS].T, g_final, g_ln_g, g_ln_b]
    small_shapes = [t.shape for t in small_grads]
    stacked_grads = [grads[k] for k in names]
    got_pair, small_all = _pair_exchange(stacked_grads, _pack(small_grads, 8))
    pair_sums = [_pair_sum(f"pair_sum_{k}", g, r, place) for k, g, r in zip(names, stacked_grads, got_pair)]
    got_chips = _chip_exchange(pair_sums)
    totals = [_chip_sum(f"chip_sum_{k}", g, r, b, place) for k, g, r, b in zip(names, stacked_grads, got_pair, got_chips)]
    full = _half_exchange(totals)
    small_sum = _unpack(_sum_devices(small_all), small_shapes)
    g_mix, g_mem, g_ffn, g_wsp, g_bsp, g_final, g_ln_g, g_ln_b = small_sum
    shard_w = sgu_ln_g.shape[-1]
    g_ln_g = lax.dynamic_slice_in_dim(g_ln_g, chip * shard_w, shard_w, axis=1)
    g_ln_b = lax.dynamic_slice_in_dim(g_ln_b, chip * shard_w, shard_w, axis=1)

    given_m = dict(mix_norm=m_mix_norm, mem_norm=m_mem_norm, w_mem_kv=m_w_mem_kv, ffn_norm=m_ffn_norm, w_gate=m_w_gate,
                   w_up=m_w_up, w_down=m_w_down, attn_w_in=m_attn_w_in, attn_w_out=m_attn_w_out, sgu_w_in=m_sgu_w_in,
                   sgu_ln_g=m_sgu_ln_g, sgu_ln_b=m_sgu_ln_b, sgu_w_spatial=m_sgu_w_spatial,
                   sgu_b_spatial=m_sgu_b_spatial, sgu_w_out=m_sgu_w_out, final_norm=m_final_norm)
    given_v = dict(mix_norm=v_mix_norm, mem_norm=v_mem_norm, w_mem_kv=v_w_mem_kv, ffn_norm=v_ffn_norm, w_gate=v_w_gate,
                   w_up=v_w_up, w_down=v_w_down, attn_w_in=v_attn_w_in, attn_w_out=v_attn_w_out, sgu_w_in=v_sgu_w_in,
                   sgu_ln_g=v_sgu_ln_g, sgu_ln_b=v_sgu_ln_b, sgu_w_spatial=v_sgu_w_spatial,
                   sgu_b_spatial=v_sgu_b_spatial, sgu_w_out=v_sgu_w_out, final_norm=v_final_norm)
    given_w = dict(mix_norm=mix_norm, mem_norm=mem_norm, w_mem_kv=w_mem_kv, ffn_norm=ffn_norm, w_gate=w_gate, w_up=w_up,
                   w_down=w_down, attn_w_in=attn_w_in, attn_w_out=attn_w_out, sgu_w_in=sgu_w_in, sgu_ln_g=sgu_ln_g,
                   sgu_ln_b=sgu_ln_b, sgu_w_spatial=sgu_w_spatial, sgu_b_spatial=sgu_b_spatial, sgu_w_out=sgu_w_out,
                   final_norm=final_norm)
    order = ["mix_norm", "mem_norm", "w_mem_kv", "ffn_norm", "w_gate", "w_up", "w_down", "attn_w_in", "attn_w_out",
             "sgu_w_in", "sgu_ln_g", "sgu_ln_b", "sgu_w_spatial", "sgu_b_spatial", "sgu_w_out", "final_norm"]
    out_g, out_d, out_m, out_v = {}, {}, {}, {}
    for k, gfull in zip(names, full):
        shp = given_w[k].shape
        cols = shp[-1]
        g2 = gfull.reshape(-1, cols)
        d, nm, nv = _adamw(f"adamw_{k}", given_w[k].reshape(-1, cols), g2, given_m[k].reshape(-1, cols),
                           given_v[k].reshape(-1, cols))
        out_g[k], out_d[k], out_m[k], out_v[k] = (t.reshape(shp) for t in (g2, d, nm, nv))
    small_names = ["mix_norm", "mem_norm", "ffn_norm", "sgu_w_spatial", "sgu_b_spatial", "final_norm", "sgu_ln_g",
                   "sgu_ln_b"]
    small_g = [g_mix, g_mem, g_ffn, g_wsp, g_bsp, g_final, g_ln_g, g_ln_b]
    small_shapes = [given_w[k].shape for k in small_names]
    d, nm, nv = _adamw("adamw_small", _pack([given_w[k] for k in small_names], 8), _pack(small_g, 8),
                       _pack([given_m[k] for k in small_names], 8), _pack([given_v[k] for k in small_names], 8))
    for k, gk, dk, mk, vk in zip(small_names, small_g, _unpack(d, small_shapes), _unpack(nm, small_shapes),
                                 _unpack(nv, small_shapes)):
        out_g[k], out_d[k], out_m[k], out_v[k] = gk.reshape(given_w[k].shape), dk, mk, vk

    return (loss, d0[None], *[out_g[k] for k in order], *[out_d[k] for k in order], *[out_m[k] for k in order],
            *[out_v[k] for k in order])
```

```python
import functools
import math

import jax
import jax.numpy as jnp
from jax import lax
from jax.experimental import pallas as pl
from jax.experimental.pallas import tpu as pltpu

F32 = jnp.float32
BF = jnp.bfloat16
MESH = pl.DeviceIdType.MESH

HEAD_DIM = 128
MEM_HEADS = 4
MEM_W = MEM_HEADS * HEAD_DIM
GROUP_W = 4 * HEAD_DIM
DILATIONS = (1, 4, 16)
BLK = 128
SGU_GROUPS = 12
SGU_W = SGU_GROUPS * HEAD_DIM
ROT_HALF = 16
ROPE_THETA = 500000.0
NORM_EPS = 1e-6
LN_EPS = 1e-5
NEG_INF = -1e30
SCALE = HEAD_DIM ** -0.5
ADAM_LR, ADAM_B1, ADAM_B2, ADAM_EPS, ADAM_WD, ADAM_STEP = 0.001, 0.9, 0.999, 1e-08, 0.01, 10

VMEM_LIMIT = 48 * 2 ** 20
VMEM_TILE_BUDGET = 36 * 2 ** 20
N_CHIPS = 4
LANES = 128

NT_DIMS = (((1,), (1,)), ((), ()))
TN_DIMS = (((0,), (0,)), ((), ()))
NN_DIMS = (((1,), (0,)), ((), ()))


def _params(sem):
    return pltpu.CompilerParams(dimension_semantics=sem, vmem_limit_bytes=VMEM_LIMIT)


def _pick(n, cap):
    if n <= cap:
        return n
    best = None
    for t in range(LANES, cap + 1, LANES):
        if n % t == 0:
            best = t
    assert best is not None, (n, cap)
    return best


def _pick_rows(n, cap):
    t = min(n, cap)
    while n % t:
        t //= 2
    return t


def _mm(name, dims, a_list, a_specs, b_list, b_specs, pairs, n_acc, acc_shape, grid, extras, e_specs,
        out_shapes, out_specs, epilogue, alias=None):
    na, nb, ne, no = len(a_list), len(b_list), len(extras), len(out_shapes)
    nk = grid[-1]
    has_alias = alias is not None

    def products(a, b):
        sums = [None] * n_acc
        for ai, bi, ci in pairs:
            prod = lax.dot_general(a[ai][...].astype(BF), b[bi][...].astype(BF), dims, preferred_element_type=F32)
            sums[ci] = prod if sums[ci] is None else sums[ci] + prod
        return sums

    def body(*refs):
        a = refs[:na]
        b = refs[na:na + nb]
        e = refs[na + nb:na + nb + ne]
        off = na + nb + ne + (1 if has_alias else 0)
        o = refs[off:off + no]
        acc = refs[off + no:]

        def finish(sums):
            outs = epilogue(sums, [r[...] for r in e])
            for r, v in zip(o, outs):
                r[...] = v.astype(r.dtype)

        if nk == 1:
            finish(products(a, b))
            return
        k = pl.program_id(len(grid) - 1)

        @pl.when(k == 0)
        def _():
            for c, v in zip(acc, products(a, b)):
                c[...] = v

        @pl.when(jnp.logical_and(k > 0, k < nk - 1))
        def _():
            for c, v in zip(acc, products(a, b)):
                c[...] += v

        @pl.when(k == nk - 1)
        def _():
            finish([c[...] + v for c, v in zip(acc, products(a, b))])

    ins = [*a_list, *b_list, *extras]
    in_specs = [*a_specs, *b_specs, *e_specs]
    aliases = {}
    if has_alias:
        aliases = {len(ins): 0}
        ins.append(alias)
        in_specs.append(pl.BlockSpec(memory_space=pl.ANY))
    sem = ("parallel",) * (len(grid) - 1) + ("arbitrary",)
    scratch = [] if nk == 1 else [pltpu.VMEM(acc_shape, F32)] * n_acc
    return pl.pallas_call(
        body, out_shape=out_shapes, grid=grid, in_specs=in_specs, out_specs=out_specs, scratch_shapes=scratch,
        input_output_aliases=aliases, name=name, compiler_params=_params(sem))(*ins)


def _tile_bytes(blocks):
    return sum(2 * math.prod(s) * jnp.dtype(d).itemsize for s, d in blocks)


def _first(acc, extra):
    return [acc[0]]


class Weight:
    def __init__(self, arr, axis):
        self.arr, self.axis = arr, axis
        _, self.layers, self.rows, self.cols = arr.shape


def _mm_nn(name, a, w, layer, out_dtype=F32, extras=(), epilogue=_first, n_out=1, out_dtypes=None):
    m, kdim = a.shape
    out_dtypes = out_dtypes or [out_dtype] * n_out
    if w.axis == "col":
        n_total = N_CHIPS * w.cols
        tn = _pick(w.cols, 1408)
        tk = _pick(kdim, 2048)
        ncb = w.cols // tn
        gn, gk = N_CHIPS * ncb, kdim // tk
        b_map = lambda n, i, k: (n // ncb, layer, k, n % ncb)
    else:
        n_total = w.cols
        tn = _pick(w.cols, 1024)
        tk = _pick(w.rows, 1408)
        nkb = w.rows // tk
        gn, gk = n_total // tn, N_CHIPS * nkb
        b_map = lambda n, i, k: (k // nkb, layer, k % nkb, n)
    for tm in (1024, 512, 256, 128):
        if m % tm:
            continue
        blocks = [((tm, tk), a.dtype), ((tk, tn), BF)] + [((tm, tn), e.dtype) for e in extras]
        blocks += [((tm, tn), d) for d in out_dtypes] + [((tm, tn), BF)]
        if _tile_bytes(blocks) <= VMEM_TILE_BUDGET:
            break
    o_spec = pl.BlockSpec((tm, tn), lambda n, i, k: (i, n))
    return _mm(
        name, NN_DIMS, [a], [pl.BlockSpec((tm, tk), lambda n, i, k: (i, k))],
        [w.arr], [pl.BlockSpec((None, None, tk, tn), b_map)], [(0, 0, 0)], 1, (tm, tn), (gn, m // tm, gk),
        list(extras), [o_spec] * len(extras),
        [jax.ShapeDtypeStruct((m, n_total), d) for d in out_dtypes], [o_spec] * len(out_dtypes), epilogue)


def _gate_up(name, h, wg, wu, layer):
    m, kdim = h.shape
    tn = _pick(wg.cols, 1408)
    tk = _pick(kdim, 2048)
    ncb = wg.cols // tn
    for tm in (512, 256, 128):
        blocks = [((tm, tk), BF), ((tk, tn), BF), ((tk, tn), BF), ((tm, tn), F32), ((tm, tn), F32), ((tm, tn), BF)]
        if m % tm == 0 and _tile_bytes(blocks) <= VMEM_TILE_BUDGET:
            break
    b_spec = pl.BlockSpec((None, None, tk, tn), lambda n, i, k: (n // ncb, layer, k, n % ncb))
    o_spec = pl.BlockSpec((tm, tn), lambda n, i, k: (i, n))
    n_total = N_CHIPS * wg.cols

    def epilogue(acc, extra):
        g, u = acc
        return [g, u, g * (1.0 / (1.0 + jnp.exp(-g))) * u]

    return _mm(
        name, NN_DIMS, [h], [pl.BlockSpec((tm, tk), lambda n, i, k: (i, k))], [wg.arr, wu.arr], [b_spec, b_spec],
        [(0, 0, 0), (0, 1, 1)], 2, (tm, tn), (N_CHIPS * ncb, m // tm, kdim // tk), [], [],
        [jax.ShapeDtypeStruct((m, n_total), F32)] * 2 + [jax.ShapeDtypeStruct((m, n_total), BF)], [o_spec] * 3,
        epilogue)


def _mm_nt(name, dys, ws, layer, out_dtypes=(F32,), extras=(), epilogue=_first):
    m = dys[0].shape[0]
    w0 = ws[0]
    npair = len(dys)
    if w0.axis == "col":
        k_total = w0.rows
        tko = _pick(k_total, 1024)
        tkc = _pick(w0.cols, 1408)
        nkb = w0.cols // tkc
        go, gk = k_total // tko, N_CHIPS * nkb
        b_map = lambda o, i, k: (k // nkb, layer, o, k % nkb)
    else:
        k_total = N_CHIPS * w0.rows
        tko = _pick(w0.rows, 1408)
        tkc = _pick(w0.cols, 2048)
        nob = w0.rows // tko
        go, gk = N_CHIPS * nob, w0.cols // tkc
        b_map = lambda o, i, k: (o // nob, layer, o % nob, k)
    for tm in (1024, 512, 256, 128):
        if m % tm:
            continue
        blocks = [((tm, tkc), d.dtype) for d in dys] + [((tko, tkc), BF)] * npair
        blocks += [((tm, tko), e.dtype) for e in extras] + [((tm, tko), d) for d in out_dtypes]
        blocks += [((tm, tko), BF)]
        if _tile_bytes(blocks) <= VMEM_TILE_BUDGET:
            break
    o_spec = pl.BlockSpec((tm, tko), lambda o, i, k: (i, o))
    return _mm(
        name, NT_DIMS, list(dys), [pl.BlockSpec((tm, tkc), lambda o, i, k: (i, k))] * npair,
        [w.arr for w in ws], [pl.BlockSpec((None, None, tko, tkc), b_map)] * npair,
        [(i, i, 0) for i in range(npair)], 1, (tm, tko), (go, m // tm, gk), list(extras), [o_spec] * len(extras),
        [jax.ShapeDtypeStruct((m, k_total), d) for d in out_dtypes], [o_spec] * len(out_dtypes), epilogue)


def _mm_tn(name, a, dy, w, layer, prev=None):
    m = a.shape[0]
    two = w.layers == 2
    rows2 = w.rows if two else w.rows // 2
    tkr = _pick(rows2, 1408)
    tn = _pick(w.cols, 1408)
    tmk = _pick_rows(m, 1024)
    nrb = rows2 // tkr
    ncb = w.cols // tn
    if w.axis == "col":
        gr, gn = w.rows // tkr, N_CHIPS * ncb
        if two:
            o_map = lambda r, n, t: (layer, n // ncb, r, n % ncb)
        else:
            o_map = lambda r, n, t: (r // nrb, n // ncb, r % nrb, n % ncb)
    else:
        per = w.rows // tkr
        gr, gn = N_CHIPS * per, ncb
        if two:
            o_map = lambda r, n, t: (layer, r // per, r % per, n)
        else:
            o_map = lambda r, n, t: ((r % per) // nrb, r // per, (r % per) % nrb, n)
    return _mm(
        name, TN_DIMS, [a], [pl.BlockSpec((tmk, tkr), lambda r, n, t: (t, r))],
        [dy], [pl.BlockSpec((tmk, tn), lambda r, n, t: (t, n))], [(0, 0, 0)], 1, (tkr, tn), (gr, gn, m // tmk), [], [],
        [jax.ShapeDtypeStruct((2, N_CHIPS, rows2, w.cols), BF)], [pl.BlockSpec((None, None, tkr, tn), o_map)],
        _first, alias=prev)[0]


def _rms_fwd(name, x, g):
    s, d = x.shape
    tr = _pick_rows(s, 512)

    def body(x_ref, g_ref, h_ref):
        xf = x_ref[...]
        r = lax.rsqrt(jnp.mean(xf * xf, axis=-1, keepdims=True) + NORM_EPS)
        h_ref[...] = (xf * r * g_ref[...]).astype(BF)

    return pl.pallas_call(
        body, out_shape=jax.ShapeDtypeStruct((s, d), BF), grid=(s // tr,),
        in_specs=[pl.BlockSpec((tr, d), lambda i: (i, 0)), pl.BlockSpec((1, d), lambda i: (0, 0))],
        out_specs=pl.BlockSpec((tr, d), lambda i: (i, 0)), name=name, compiler_params=_params(("parallel",)))(x, g)


def _rms_bwd(name, x, g, dh, dres=None):
    s, d = x.shape
    tr = _pick_rows(s, 256)
    has_res = dres is not None

    def body(*refs):
        if has_res:
            x_ref, g_ref, dh_ref, dres_ref, dx_ref, dg_ref = refs
        else:
            x_ref, g_ref, dh_ref, dx_ref, dg_ref = refs
        xf = x_ref[...]
        r = lax.rsqrt(jnp.mean(xf * xf, axis=-1, keepdims=True) + NORM_EPS)
        xr = xf * r
        dy = dh_ref[...]
        a = dy * g_ref[...]
        dx = r * (a - xr * jnp.mean(a * xr, axis=-1, keepdims=True))
        if has_res:
            dx = dx + dres_ref[...]
        dx_ref[...] = dx

        @pl.when(pl.program_id(0) == 0)
        def _():
            dg_ref[...] = jnp.zeros_like(dg_ref)

        dg_ref[...] += jnp.sum(dy * xr, axis=0, keepdims=True)

    row = pl.BlockSpec((tr, d), lambda i: (i, 0))
    vec = pl.BlockSpec((1, d), lambda i: (0, 0))
    ins = [x, g, dh] + ([dres] if has_res else [])
    in_specs = [row, vec, row] + ([row] if has_res else [])
    return pl.pallas_call(
        body, out_shape=[jax.ShapeDtypeStruct((s, d), F32), jax.ShapeDtypeStruct((1, d), F32)], grid=(s // tr,),
        in_specs=in_specs, out_specs=[row, vec], name=name, compiler_params=_params(("arbitrary",)))(*ins)


def _final_loss(x, tgt, g):
    s, d = x.shape
    tr = _pick_rows(s, 256)

    def body(x_ref, t_ref, g_ref, dx_ref, dg_ref, loss_ref):
        xf = x_ref[...]
        gain = g_ref[...]
        r = lax.rsqrt(jnp.mean(xf * xf, axis=-1, keepdims=True) + NORM_EPS)
        xr = xf * r
        err = xr * gain - t_ref[...]
        dy = err * (1.0 / d)
        a = dy * gain
        dx_ref[...] = r * (a - xr * jnp.mean(a * xr, axis=-1, keepdims=True))

        @pl.when(pl.program_id(0) == 0)
        def _():
            dg_ref[...] = jnp.zeros_like(dg_ref)
            loss_ref[...] = jnp.zeros_like(loss_ref)

        dg_ref[...] += jnp.sum(dy * xr, axis=0, keepdims=True)
        part = 0.5 * jnp.sum(jnp.mean(err * err, axis=-1, keepdims=True), axis=0, keepdims=True)
        loss_ref[...] += jnp.broadcast_to(part, loss_ref.shape)

    row = pl.BlockSpec((tr, d), lambda i: (i, 0))
    vec = pl.BlockSpec((1, d), lambda i: (0, 0))
    return pl.pallas_call(
        body, out_shape=[jax.ShapeDtypeStruct((s, d), F32), jax.ShapeDtypeStruct((1, d), F32),
                         jax.ShapeDtypeStruct((8, LANES), F32)],
        grid=(s // tr,), in_specs=[row, row, vec], out_specs=[row, vec, pl.BlockSpec((8, LANES), lambda i: (0, 0))],
        name="final_loss", compiler_params=_params(("arbitrary",)))(x, tgt, g)


def _swiglu_bwd_epilogue(acc, extra):
    dact = acc[0]
    g, u = extra
    sig = 1.0 / (1.0 + jnp.exp(-g))
    return [dact * u * sig * (1.0 + g * (1.0 - sig)), dact * g * sig]


GELU_C = math.sqrt(2.0 / math.pi)
GELU_A = 0.044715


def _gelu(x):
    return 0.5 * x * (1.0 + jnp.tanh(GELU_C * (x + GELU_A * x * x * x)))


def _gelu_grad(x):
    t = jnp.tanh(GELU_C * (x + GELU_A * x * x * x))
    return 0.5 * (1.0 + t) + 0.5 * x * (1.0 - t * t) * GELU_C * (1.0 + 3.0 * GELU_A * x * x)


def _rope_tables(positions):
    inv_freq = ROPE_THETA ** (-jnp.arange(ROT_HALF, dtype=F32) / ROT_HALF)
    ang = positions.astype(F32)[:, None] * inv_freq
    cos, sin = jnp.cos(ang), jnp.sin(ang)
    s = ang.shape[0]
    rest = HEAD_DIM - 2 * ROT_HALF
    zeros = jnp.zeros((s, ROT_HALF), F32)
    cos_t = jnp.concatenate([cos, cos, jnp.ones((s, rest), F32)], axis=1)
    sin_a = jnp.concatenate([-sin, zeros, jnp.zeros((s, rest), F32)], axis=1)
    sin_b = jnp.concatenate([zeros, sin, jnp.zeros((s, rest), F32)], axis=1)
    return cos_t, sin_a, sin_b


def _rope_heads(x, cos_t, sin_a, sin_b):
    outs = []
    for h in range(GROUP_W // HEAD_DIM):
        xh = x[:, h * HEAD_DIM:(h + 1) * HEAD_DIM]
        up = pltpu.roll(xh, HEAD_DIM - ROT_HALF, 1)
        down = pltpu.roll(xh, ROT_HALF, 1)
        outs.append(xh * cos_t + up * sin_a + down * sin_b)
    return outs


def _rope_fwd(proj, tables):
    s = proj.shape[0]
    tm = _pick_rows(s, 512)
    nparts = 9

    def body(p_ref, c_ref, sa_ref, sb_ref, *outs):
        part = pl.program_id(1)
        for p in range(nparts):
            @pl.when(part == p)
            def _(p=p):
                xv = p_ref[...]
                if p < 6:
                    heads = _rope_heads(xv, c_ref[...], sa_ref[...], sb_ref[...])
                    for h, v in enumerate(heads):
                        outs[p][:, h * HEAD_DIM:(h + 1) * HEAD_DIM] = v.astype(BF)
                else:
                    outs[p][...] = xv.astype(BF)

    tab = pl.BlockSpec((tm, HEAD_DIM), lambda i, p: (i, 0))
    o_spec = pl.BlockSpec((tm, GROUP_W), lambda i, p: (i, 0))
    return pl.pallas_call(
        body, out_shape=[jax.ShapeDtypeStruct((s, GROUP_W), BF)] * nparts, grid=(s // tm, nparts),
        in_specs=[pl.BlockSpec((tm, GROUP_W), lambda i, p: (i, p)), tab, tab, tab], out_specs=[o_spec] * nparts,
        name="rope_fwd", compiler_params=_params(("parallel", "arbitrary")))(proj, *tables)


def _rope_bwd(parts, dq_mem, tables):
    s = dq_mem.shape[0]
    tm = _pick_rows(s, 256)
    nparts = len(parts) + 1

    def body(*refs):
        ins = refs[:nparts]
        c_ref, sa_ref, sb_ref, o_ref = refs[nparts:]
        part = pl.program_id(1)
        for p in range(nparts):
            @pl.when(part == p)
            def _(p=p):
                xv = ins[p][...]
                if p < 6:
                    heads = _rope_heads(xv, c_ref[...], -sa_ref[...], -sb_ref[...])
                    for h, v in enumerate(heads):
                        o_ref[:, h * HEAD_DIM:(h + 1) * HEAD_DIM] = v.astype(BF)
                else:
                    o_ref[...] = xv.astype(BF)

    i_spec = pl.BlockSpec((tm, GROUP_W), lambda i, p: (i, 0))
    tab = pl.BlockSpec((tm, HEAD_DIM), lambda i, p: (i, 0))
    return pl.pallas_call(
        body, out_shape=jax.ShapeDtypeStruct((s, nparts * GROUP_W), BF), grid=(s // tm, nparts),
        in_specs=[i_spec] * nparts + [tab] * 3, out_specs=pl.BlockSpec((tm, GROUP_W), lambda i, p: (i, p)),
        name="rope_bwd", compiler_params=_params(("parallel", "arbitrary")))(*parts, dq_mem, *tables)


def _band_masks(n):
    qi = lax.broadcasted_iota(jnp.int32, (BLK, BLK), 0)
    ki = lax.broadcasted_iota(jnp.int32, (BLK, BLK), 1)
    return qi >= ki, jnp.logical_and(ki >= qi, n > 0)


def _dil_fwd(g, q, k, v, dil):
    s = q.shape[0]
    length = s // dil
    nb = length // BLK
    view = lambda t: t.reshape(length, dil * GROUP_W)

    def body(q_ref, kc_ref, kp_ref, vc_ref, vp_ref, o_ref, lse_ref):
        n = pl.program_id(1)
        mask_c, mask_p = _band_masks(n)
        for h in range(GROUP_W // HEAD_DIM):
            sl = slice(h * HEAD_DIM, (h + 1) * HEAD_DIM)
            qh = q_ref[:, sl]
            sc = lax.dot_general(qh, kc_ref[:, sl], NT_DIMS, preferred_element_type=F32) * SCALE
            sp = lax.dot_general(qh, kp_ref[:, sl], NT_DIMS, preferred_element_type=F32) * SCALE
            sc = jnp.where(mask_c, sc, NEG_INF)
            sp = jnp.where(mask_p, sp, NEG_INF)
            mx = jnp.maximum(jnp.max(sc, axis=-1, keepdims=True), jnp.max(sp, axis=-1, keepdims=True))
            pc = jnp.exp(sc - mx)
            pp = jnp.exp(sp - mx)
            den = jnp.sum(pc, axis=-1, keepdims=True) + jnp.sum(pp, axis=-1, keepdims=True)
            acc = jnp.dot(pc.astype(BF), vc_ref[:, sl], preferred_element_type=F32)
            acc += jnp.dot(pp.astype(BF), vp_ref[:, sl], preferred_element_type=F32)
            o_ref[:, sl] = acc / den
            lse_ref[:, sl] = jnp.broadcast_to(mx + jnp.log(den), (BLK, HEAD_DIM))

    cur = pl.BlockSpec((BLK, GROUP_W), lambda r, n: (n, r))
    prev = pl.BlockSpec((BLK, GROUP_W), lambda r, n: (jnp.maximum(n - 1, 0), r))
    o, lse = pl.pallas_call(
        body, out_shape=[jax.ShapeDtypeStruct((length, dil * GROUP_W), F32)] * 2, grid=(dil, nb),
        in_specs=[cur, cur, prev, cur, prev], out_specs=[cur, cur], name=f"dil_fwd_{g}",
        compiler_params=_params(("parallel", "arbitrary")))(view(q), view(k), view(k), view(v), view(v))
    return o.reshape(s, GROUP_W), lse.reshape(s, GROUP_W)


def _dil_bwd(g, q, k, v, do, lse, delta, dil):
    s = q.shape[0]
    length = s // dil
    nb = length // BLK
    view = lambda t: t.reshape(length, dil * GROUP_W)

    def body(q_ref, kc_ref, kp_ref, vc_ref, vp_ref, do_ref, lse_ref, dl_ref, dq_ref, dk_ref, dv_ref, ck_ref, cv_ref):
        n = pl.program_id(1)
        live = n < nb
        mask_c, mask_p = _band_masks(n)
        mask_c = jnp.logical_and(mask_c, live)
        mask_p = jnp.logical_and(mask_p, live)

        @pl.when(n == 0)
        def _():
            ck_ref[...] = jnp.zeros_like(ck_ref)
            cv_ref[...] = jnp.zeros_like(cv_ref)

        for h in range(GROUP_W // HEAD_DIM):
            sl = slice(h * HEAD_DIM, (h + 1) * HEAD_DIM)
            qh, kc, kp, vc, vp, doh = q_ref[:, sl], kc_ref[:, sl], kp_ref[:, sl], vc_ref[:, sl], vp_ref[:, sl], do_ref[:, sl]
            lse_h = lse_ref[:, sl]
            dl_h = dl_ref[:, sl]
            sc = lax.dot_general(qh, kc, NT_DIMS, preferred_element_type=F32) * SCALE
            sp = lax.dot_general(qh, kp, NT_DIMS, preferred_element_type=F32) * SCALE
            pc = jnp.where(mask_c, jnp.exp(jnp.minimum(sc - lse_h, 0.0)), 0.0)
            pp = jnp.where(mask_p, jnp.exp(jnp.minimum(sp - lse_h, 0.0)), 0.0)
            dpc = lax.dot_general(doh, vc, NT_DIMS, preferred_element_type=F32)
            dpp = lax.dot_general(doh, vp, NT_DIMS, preferred_element_type=F32)
            dsc = (pc * (dpc - dl_h) * SCALE).astype(BF)
            dsp = (pp * (dpp - dl_h) * SCALE).astype(BF)
            dq = jnp.dot(dsc, kc, preferred_element_type=F32) + jnp.dot(dsp, kp, preferred_element_type=F32)

            @pl.when(live)
            def _(dq=dq, sl=sl):
                dq_ref[:, sl] = dq

            dk_ref[:, sl] = ck_ref[:, sl] + lax.dot_general(dsp, qh, TN_DIMS, preferred_element_type=F32)
            dv_ref[:, sl] = cv_ref[:, sl] + lax.dot_general(pp.astype(BF), doh, TN_DIMS, preferred_element_type=F32)
            ck_ref[:, sl] = lax.dot_general(dsc, qh, TN_DIMS, preferred_element_type=F32)
            cv_ref[:, sl] = lax.dot_general(pc.astype(BF), doh, TN_DIMS, preferred_element_type=F32)

    last = nb - 1
    cur = pl.BlockSpec((BLK, GROUP_W), lambda r, n: (jnp.minimum(n, last), r))
    prev = pl.BlockSpec((BLK, GROUP_W), lambda r, n: (jnp.maximum(n - 1, 0), r))
    shape = jax.ShapeDtypeStruct((length, dil * GROUP_W), F32)
    dq, dk, dv = pl.pallas_call(
        body, out_shape=[shape] * 3, grid=(dil, nb + 1),
        in_specs=[cur, cur, prev, cur, prev, cur, cur, cur], out_specs=[cur, prev, prev],
        scratch_shapes=[pltpu.VMEM((BLK, GROUP_W), F32)] * 2, name=f"dil_bwd_{g}",
        compiler_params=_params(("parallel", "arbitrary")))(
            view(q), view(k), view(k), view(v), view(v), view(do), view(lse), view(delta))
    return dq.reshape(s, GROUP_W), dk.reshape(s, GROUP_W), dv.reshape(s, GROUP_W)


def _attn_merge(outs, lses):
    s = outs[0].shape[0]
    tr = _pick_rows(s, 512)

    def body(o0, o1, o2, l0, l1, l2, m_ref, lse_ref):
        a, b, c = l0[...], l1[...], l2[...]
        mx = jnp.maximum(jnp.maximum(a, b), c)
        ea, eb, ec = jnp.exp(a - mx), jnp.exp(b - mx), jnp.exp(c - mx)
        den = ea + eb + ec
        m_ref[...] = (ea * o0[...] + eb * o1[...] + ec * o2[...]) / den
        lse_ref[...] = mx + jnp.log(den)

    spec = pl.BlockSpec((tr, GROUP_W), lambda i: (i, 0))
    return pl.pallas_call(
        body, out_shape=[jax.ShapeDtypeStruct((s, GROUP_W), F32)] * 2, grid=(s // tr,), in_specs=[spec] * 6,
        out_specs=[spec] * 2, name="attn_merge", compiler_params=_params(("parallel",)))(*outs, *lses)


def _attn_delta(dcat, merged):
    s = merged.shape[0]
    tr = _pick_rows(s, 512)

    def body(d_ref, m_ref, do_ref, dl_ref):
        d = d_ref[...]
        prod = d * m_ref[...]
        do_ref[...] = d.astype(BF)
        for h in range(GROUP_W // HEAD_DIM):
            sl = slice(h * HEAD_DIM, (h + 1) * HEAD_DIM)
            dl_ref[:, sl] = jnp.broadcast_to(jnp.sum(prod[:, sl], axis=-1, keepdims=True), (tr, HEAD_DIM))

    spec = pl.BlockSpec((tr, GROUP_W), lambda i: (i, 0))
    return pl.pallas_call(
        body, out_shape=[jax.ShapeDtypeStruct((s, GROUP_W), BF), jax.ShapeDtypeStruct((s, GROUP_W), F32)],
        grid=(s // tr,), in_specs=[spec, spec], out_specs=[spec, spec], name="attn_delta",
        compiler_params=_params(("parallel",)))(dcat, merged)


def _mem_probs(qh, kh):
    sc = lax.dot_general(qh, kh, NT_DIMS, preferred_element_type=F32) * SCALE
    p = jnp.exp(sc - jnp.max(sc, axis=-1, keepdims=True))
    return p, jnp.sum(p, axis=-1, keepdims=True)


def _mem_fwd(name, proj, q_block, kv):
    s = proj.shape[0]
    tq = _pick_rows(s, 512)

    def body(q_ref, kv_ref, o_ref):
        for h in range(MEM_HEADS):
            sl = slice(h * HEAD_DIM, (h + 1) * HEAD_DIM)
            vsl = slice(MEM_W + h * HEAD_DIM, MEM_W + (h + 1) * HEAD_DIM)
            p, den = _mem_probs(q_ref[:, sl].astype(BF), kv_ref[:, sl].astype(BF))
            o_ref[:, sl] = jnp.dot(p.astype(BF), kv_ref[:, vsl].astype(BF), preferred_element_type=F32) / den

    return pl.pallas_call(
        body, out_shape=jax.ShapeDtypeStruct((s, MEM_W), F32), grid=(s // tq,),
        in_specs=[pl.BlockSpec((tq, MEM_W), lambda i: (i, q_block)), pl.BlockSpec(kv.shape, lambda i: (0, 0))],
        out_specs=pl.BlockSpec((tq, MEM_W), lambda i: (i, 0)), name=name, compiler_params=_params(("parallel",)))(proj, kv)


def _mem_bwd(name, proj, q_block, kv, dcat, d_block):
    s = proj.shape[0]
    tq = _pick_rows(s, 512)

    def body(q_ref, kv_ref, do_ref, dq_ref, dkv_ref):
        @pl.when(pl.program_id(0) == 0)
        def _():
            dkv_ref[...] = jnp.zeros_like(dkv_ref)

        for h in range(MEM_HEADS):
            sl = slice(h * HEAD_DIM, (h + 1) * HEAD_DIM)
            vsl = slice(MEM_W + h * HEAD_DIM, MEM_W + (h + 1) * HEAD_DIM)
            qh, kh, vh = q_ref[:, sl].astype(BF), kv_ref[:, sl].astype(BF), kv_ref[:, vsl].astype(BF)
            doh = do_ref[:, sl].astype(BF)
            p, den = _mem_probs(qh, kh)
            p = p / den
            dp = lax.dot_general(doh, vh, NT_DIMS, preferred_element_type=F32)
            ds = (p * (dp - jnp.sum(p * dp, axis=-1, keepdims=True)) * SCALE).astype(BF)
            dq_ref[:, sl] = jnp.dot(ds, kh, preferred_element_type=F32)
            dkv_ref[:, sl] += lax.dot_general(ds, qh, TN_DIMS, preferred_element_type=F32)
            dkv_ref[:, vsl] += lax.dot_general(p.astype(BF), doh, TN_DIMS, preferred_element_type=F32)

    whole = pl.BlockSpec(kv.shape, lambda i: (0, 0))
    return pl.pallas_call(
        body, out_shape=[jax.ShapeDtypeStruct((s, MEM_W), F32), jax.ShapeDtypeStruct(kv.shape, F32)], grid=(s // tq,),
        in_specs=[pl.BlockSpec((tq, MEM_W), lambda i: (i, q_block)), whole,
                  pl.BlockSpec((tq, MEM_W), lambda i: (i, d_block))],
        out_specs=[pl.BlockSpec((tq, MEM_W), lambda i: (i, 0)), whole], name=name,
        compiler_params=_params(("arbitrary",)))(proj, kv, dcat)


def _causal():
    t = lax.broadcasted_iota(jnp.int32, (BLK, BLK), 0)
    s = lax.broadcasted_iota(jnp.int32, (BLK, BLK), 1)
    return t >= s


def _sgu_norm(v_pre, ln_g, ln_b):
    vg = _gelu(v_pre)
    mu = jnp.mean(vg, axis=-1, keepdims=True)
    cen = vg - mu
    rstd = lax.rsqrt(jnp.mean(cen * cen, axis=-1, keepdims=True) + LN_EPS)
    xhat = cen * rstd
    return xhat, rstd, xhat * ln_g + ln_b


def _sgu_fwd(proj, ln_g, ln_b, w_sp, b_t):
    s = proj.shape[0]

    def body(u_ref, v_ref, g_ref, b_ref, w_ref, bt_ref, o_ref):
        _, _, vn = _sgu_norm(v_ref[...], g_ref[...], b_ref[...])
        vn = vn.astype(BF)
        tri = _causal()
        for grp in range(SGU_GROUPS):
            sl = slice(grp * HEAD_DIM, (grp + 1) * HEAD_DIM)
            w = jnp.where(tri, w_ref[grp], 0.0).astype(BF)
            mixed = jnp.dot(w, vn[:, sl], preferred_element_type=F32) + bt_ref[:, grp:grp + 1]
            o_ref[:, sl] = _gelu(u_ref[:, sl]) * mixed

    vec = pl.BlockSpec((1, SGU_W), lambda i: (0, 0))
    return pl.pallas_call(
        body, out_shape=jax.ShapeDtypeStruct((s, SGU_W), F32), grid=(s // BLK,),
        in_specs=[pl.BlockSpec((BLK, SGU_W), lambda i: (i, 0)), pl.BlockSpec((BLK, SGU_W), lambda i: (i, 1)), vec, vec,
                  pl.BlockSpec(w_sp.shape, lambda i: (0, 0, 0)), pl.BlockSpec(b_t.shape, lambda i: (0, 0))],
        out_specs=pl.BlockSpec((BLK, SGU_W), lambda i: (i, 0)), name="sgu_fwd",
        compiler_params=_params(("parallel",)))(proj, proj, ln_g, ln_b, w_sp, b_t)


def _sgu_bwd(proj, dcat, ln_g, ln_b, w_sp, b_t):
    s = proj.shape[0]

    def body(u_ref, v_ref, d_ref, g_ref, b_ref, w_ref, bt_ref, du_ref, dv_ref, dw_ref, db_ref, dg_ref, dbeta_ref,
             dvn_ref):
        @pl.when(pl.program_id(0) == 0)
        def _():
            dw_ref[...] = jnp.zeros_like(dw_ref)
            db_ref[...] = jnp.zeros_like(db_ref)
            dg_ref[...] = jnp.zeros_like(dg_ref)
            dbeta_ref[...] = jnp.zeros_like(dbeta_ref)

        v_pre = v_ref[...]
        gain = g_ref[...]
        xhat, rstd, vn = _sgu_norm(v_pre, gain, b_ref[...])
        vn = vn.astype(BF)
        tri = _causal()
        lane = lax.broadcasted_iota(jnp.int32, (BLK, HEAD_DIM), 1)
        db_acc = jnp.zeros((BLK, HEAD_DIM), F32)
        for grp in range(SGU_GROUPS):
            sl = slice(grp * HEAD_DIM, (grp + 1) * HEAD_DIM)
            w = jnp.where(tri, w_ref[grp], 0.0).astype(BF)
            vn_g = vn[:, sl]
            mixed = jnp.dot(w, vn_g, preferred_element_type=F32) + bt_ref[:, grp:grp + 1]
            u_pre = u_ref[:, sl]
            d_out = d_ref[:, sl]
            du_ref[:, sl] = (d_out * mixed * _gelu_grad(u_pre)).astype(BF)
            dmixed = d_out * _gelu(u_pre)
            dm = dmixed.astype(BF)
            dvn_ref[:, sl] = lax.dot_general(w, dm, TN_DIMS, preferred_element_type=F32)
            dw = lax.dot_general(dm, vn_g, NT_DIMS, preferred_element_type=F32)
            dw_ref[grp] += jnp.where(tri, dw, 0.0)
            db_acc += jnp.where(lane == grp, jnp.sum(dmixed, axis=-1, keepdims=True), 0.0)
        db_ref[...] += db_acc
        dvn = dvn_ref[...]
        dg_ref[...] += jnp.sum(dvn * xhat, axis=0, keepdims=True)
        dbeta_ref[...] += jnp.sum(dvn, axis=0, keepdims=True)
        dxh = dvn * gain
        dvg = rstd * (dxh - jnp.mean(dxh, axis=-1, keepdims=True) - xhat * jnp.mean(dxh * xhat, axis=-1, keepdims=True))
        dv_ref[...] = (dvg * _gelu_grad(v_pre)).astype(BF)

    vec = pl.BlockSpec((1, SGU_W), lambda i: (0, 0))
    row = pl.BlockSpec((BLK, SGU_W), lambda i: (i, 0))
    w_spec = pl.BlockSpec(w_sp.shape, lambda i: (0, 0, 0))
    sq = pl.BlockSpec((BLK, HEAD_DIM), lambda i: (0, 0))
    return pl.pallas_call(
        body,
        out_shape=[jax.ShapeDtypeStruct((s, SGU_W), BF), jax.ShapeDtypeStruct((s, SGU_W), BF),
                   jax.ShapeDtypeStruct(w_sp.shape, F32), jax.ShapeDtypeStruct((BLK, HEAD_DIM), F32),
                   jax.ShapeDtypeStruct((1, SGU_W), F32), jax.ShapeDtypeStruct((1, SGU_W), F32)],
        grid=(s // BLK,),
        in_specs=[row, pl.BlockSpec((BLK, SGU_W), lambda i: (i, 1)), row, vec, vec, w_spec,
                  pl.BlockSpec(b_t.shape, lambda i: (0, 0))],
        out_specs=[row, row, w_spec, sq, vec, vec], scratch_shapes=[pltpu.VMEM((BLK, SGU_W), F32)], name="sgu_bwd",
        compiler_params=_params(("arbitrary",)))(proj, proj, dcat, ln_g, ln_b, w_sp, b_t)


def _place():
    return lax.axis_index("x"), lax.axis_index("y"), lax.axis_index("c")


def _other_chips(x, y):
    return [(1 - x, y), (x, 1 - y), (1 - x, 1 - y)]


ANY = pl.BlockSpec(memory_space=pl.ANY)


def _place_shard(name, shard, place, dtype):
    _, rows, cols = shard.shape
    tr = _pick_rows(rows, 512)

    def body(p_ref, s_ref, o_ref):
        o_ref[...] = s_ref[...].astype(dtype)

    grid_spec = pltpu.PrefetchScalarGridSpec(
        num_scalar_prefetch=1, grid=(2, rows // tr),
        in_specs=[pl.BlockSpec((None, tr, cols), lambda h, i, p: (h, i, 0))],
        out_specs=pl.BlockSpec((None, None, tr, cols), lambda h, i, p: (p[1], h, i, 0)))
    return pl.pallas_call(body, out_shape=jax.ShapeDtypeStruct((N_CHIPS, 2, rows, cols), dtype), grid_spec=grid_spec,
                          name=name, compiler_params=_params(("parallel", "parallel")))(place, shard)


def _all_gather(slabs):
    n = len(slabs)

    def body(*refs):
        outs = refs[n:2 * n]
        send_sems, recv_sems, pass_send, pass_recv = refs[2 * n:]
        x, y, c = _place()
        me = 2 * x + y
        sibling = (x, y, 1 - c)
        chips = _other_chips(x, y)

        def remote(src, dst, ssem, rsem, to):
            return pltpu.make_async_remote_copy(src_ref=src, dst_ref=dst, send_sem=ssem, recv_sem=rsem, device_id=to,
                                                device_id_type=MESH)

        sends = []
        for a in range(n):
            for k, (ox, oy) in enumerate(chips):
                mine = outs[a].at[me, c]
                cp = remote(mine, mine, send_sems.at[3 * a + k], recv_sems.at[3 * a + k], (ox, oy, c))
                cp.start()
                sends.append(cp)
        for a in range(n):
            for k, (ox, oy) in enumerate(chips):
                slab = outs[a].at[2 * ox + oy, c]
                remote(slab, slab, send_sems.at[3 * a + k], recv_sems.at[3 * a + k], (ox, oy, c)).wait_recv()
                cp = remote(slab, slab, pass_send.at[3 * a + k], pass_recv.at[3 * a + k], sibling)
                cp.start()
                sends.append(cp)
        for a in range(n):
            for k, (ox, oy) in enumerate(chips):
                slab = outs[a].at[2 * ox + oy, 1 - c]
                remote(slab, slab, pass_send.at[3 * a + k], pass_recv.at[3 * a + k], sibling).wait_recv()
        for cp in sends:
            cp.wait_send()

    return pl.pallas_call(
        body, out_shape=[jax.ShapeDtypeStruct(s.shape, s.dtype) for s in slabs], in_specs=[ANY] * n,
        out_specs=[ANY] * n, scratch_shapes=[pltpu.SemaphoreType.DMA((3 * n,))] * 4,
        input_output_aliases={a: a for a in range(n)}, name="weight_all_gather")(*slabs)


def _pair_exchange(grads, small):
    n = len(grads)

    def body(*refs):
        ins, small_ref = refs[:n], refs[n]
        outs, all_ref = refs[n + 1:2 * n + 1], refs[2 * n + 1]
        ssem, rsem, small_local, small_send, small_recv = refs[2 * n + 2:]
        x, y, c = _place()
        me = 4 * x + 2 * y + c
        sibling = (x, y, 1 - c)
        copies = []
        for a in range(n):
            cp = pltpu.make_async_remote_copy(src_ref=ins[a].at[1 - c], dst_ref=outs[a], send_sem=ssem.at[a],
                                              recv_sem=rsem.at[a], device_id=sibling, device_id_type=MESH)
            cp.start()
            copies.append(cp)
        own = pltpu.make_async_copy(small_ref, all_ref.at[me], small_local)
        own.start()
        for mask in range(1, 8):
            fx, fy, fc = (mask >> 2) & 1, (mask >> 1) & 1, mask & 1
            peer = (1 - x if fx else x, 1 - y if fy else y, 1 - c if fc else c)
            cp = pltpu.make_async_remote_copy(src_ref=small_ref, dst_ref=all_ref.at[me], send_sem=small_send.at[mask - 1],
                                              recv_sem=small_recv.at[mask - 1], device_id=peer, device_id_type=MESH)
            cp.start()
            copies.append(cp)
        for cp in copies:
            cp.wait()
        own.wait()

    out_shape = [jax.ShapeDtypeStruct(g.shape[1:], g.dtype) for g in grads]
    out_shape.append(jax.ShapeDtypeStruct((8, *small.shape), small.dtype))
    res = pl.pallas_call(
        body, out_shape=out_shape, in_specs=[ANY] * (n + 1), out_specs=[ANY] * (n + 1),
        scratch_shapes=[pltpu.SemaphoreType.DMA((n,)), pltpu.SemaphoreType.DMA((n,)), pltpu.SemaphoreType.DMA,
                        pltpu.SemaphoreType.DMA((7,)), pltpu.SemaphoreType.DMA((7,))],
        name="grad_pair_exchange")(*grads, small)
    return res[:n], res[n]


def _chip_exchange(pair_sums):
    n = len(pair_sums)

    def body(*refs):
        ins, outs = refs[:n], refs[n:2 * n]
        ssem, rsem = refs[2 * n:]
        x, y, c = _place()
        copies = []
        for a in range(n):
            for k, (ox, oy) in enumerate(_other_chips(x, y)):
                cp = pltpu.make_async_remote_copy(src_ref=ins[a].at[2 * ox + oy], dst_ref=outs[a].at[k],
                                                  send_sem=ssem.at[3 * a + k], recv_sem=rsem.at[3 * a + k],
                                                  device_id=(ox, oy, c), device_id_type=MESH)
                cp.start()
                copies.append(cp)
        for cp in copies:
            cp.wait()

    return pl.pallas_call(
        body, out_shape=[jax.ShapeDtypeStruct((3, *p.shape[1:]), p.dtype) for p in pair_sums], in_specs=[ANY] * n,
        out_specs=[ANY] * n, scratch_shapes=[pltpu.SemaphoreType.DMA((3 * n,))] * 2, name="grad_chip_exchange")(*pair_sums)


def _half_exchange(totals):
    n = len(totals)

    def body(*refs):
        outs = refs[n:2 * n]
        ssem, rsem = refs[2 * n:]
        x, y, c = _place()
        copies = []
        for a in range(n):
            mine = outs[a].at[c]
            cp = pltpu.make_async_remote_copy(src_ref=mine, dst_ref=mine, send_sem=ssem.at[a], recv_sem=rsem.at[a],
                                              device_id=(x, y, 1 - c), device_id_type=MESH)
            cp.start()
            copies.append(cp)
        for cp in copies:
            cp.wait()

    return pl.pallas_call(
        body, out_shape=[jax.ShapeDtypeStruct(t.shape, t.dtype) for t in totals], in_specs=[ANY] * n,
        out_specs=[ANY] * n, scratch_shapes=[pltpu.SemaphoreType.DMA((n,))] * 2,
        input_output_aliases={a: a for a in range(n)}, name="grad_half_exchange")(*totals)


def _pair_sum(name, grad, got, place):
    _, _, rows, cols = grad.shape
    tr = _pick_rows(rows, 256)

    def body(p_ref, g_ref, r_ref, o_ref):
        o_ref[...] = (g_ref[...].astype(F32) + r_ref[...].astype(F32)).astype(BF)

    grid_spec = pltpu.PrefetchScalarGridSpec(
        num_scalar_prefetch=1, grid=(N_CHIPS, rows // tr),
        in_specs=[pl.BlockSpec((None, None, tr, cols), lambda j, i, p: (p[0], j, i, 0)),
                  pl.BlockSpec((None, tr, cols), lambda j, i, p: (j, i, 0))],
        out_specs=pl.BlockSpec((None, tr, cols), lambda j, i, p: (j, i, 0)))
    return pl.pallas_call(body, out_shape=jax.ShapeDtypeStruct((N_CHIPS, rows, cols), BF), grid_spec=grid_spec, name=name,
                          compiler_params=_params(("parallel", "parallel")))(place, grad, got)


def _chip_sum(name, grad, got_pair, got_chips, place):
    _, _, rows, cols = grad.shape
    tr = _pick_rows(rows, 256)

    def body(p_ref, g_ref, r_ref, b_ref, o_ref):
        tot = g_ref[...].astype(F32) + r_ref[...].astype(F32)
        for k in range(3):
            tot = tot + b_ref[k].astype(F32)
        o_ref[...] = tot

    grid_spec = pltpu.PrefetchScalarGridSpec(
        num_scalar_prefetch=1, grid=(rows // tr,),
        in_specs=[pl.BlockSpec((None, None, tr, cols), lambda i, p: (p[0], p[1], i, 0)),
                  pl.BlockSpec((None, tr, cols), lambda i, p: (p[1], i, 0)),
                  pl.BlockSpec((3, tr, cols), lambda i, p: (0, i, 0))],
        out_specs=pl.BlockSpec((None, tr, cols), lambda i, p: (p[0], i, 0)))
    return pl.pallas_call(body, out_shape=jax.ShapeDtypeStruct((2, rows, cols), F32), grid_spec=grid_spec, name=name,
                          compiler_params=_params(("parallel",)))(place, grad, got_pair, got_chips)


def _sum_devices(stacked):
    _, rows, lanes = stacked.shape
    tr = _pick_rows(rows, 512)

    def body(s_ref, o_ref):
        tot = s_ref[0]
        for k in range(1, 8):
            tot = tot + s_ref[k]
        o_ref[...] = tot

    return pl.pallas_call(
        body, out_shape=jax.ShapeDtypeStruct((rows, lanes), F32), grid=(rows // tr,),
        in_specs=[pl.BlockSpec((8, tr, lanes), lambda i: (0, i, 0))], out_specs=pl.BlockSpec((tr, lanes), lambda i: (i, 0)),
        name="small_grad_sum", compiler_params=_params(("parallel",)))(stacked)


def _adamw(name, w, g, m, v):
    rows, cols = w.shape
    tr = _pick_rows(rows, 256)
    c1 = 1.0 - ADAM_B1 ** ADAM_STEP
    c2 = 1.0 - ADAM_B2 ** ADAM_STEP

    def body(w_ref, g_ref, m_ref, v_ref, d_ref, nm_ref, nv_ref):
        gv = g_ref[...]
        nm = ADAM_B1 * m_ref[...] + (1.0 - ADAM_B1) * gv
        nv = ADAM_B2 * v_ref[...] + (1.0 - ADAM_B2) * (gv * gv)
        d_ref[...] = -ADAM_LR * ((nm / c1) / (jnp.sqrt(nv / c2) + ADAM_EPS) + ADAM_WD * w_ref[...])
        nm_ref[...] = nm
        nv_ref[...] = nv

    spec = pl.BlockSpec((tr, cols), lambda i: (i, 0))
    return pl.pallas_call(
        body, out_shape=[jax.ShapeDtypeStruct((rows, cols), F32)] * 3, grid=(rows // tr,), in_specs=[spec] * 4,
        out_specs=[spec] * 3, name=name, compiler_params=_params(("parallel",)))(w, g, m, v)


def _pack(vectors, pad_rows):
    flat = jnp.concatenate([t.reshape(-1) for t in vectors])
    rows = -(-flat.shape[0] // LANES)
    rows = -(-rows // pad_rows) * pad_rows
    return jnp.pad(flat, (0, rows * LANES - flat.shape[0])).reshape(rows, LANES)


def _unpack(packed, shapes):
    flat = packed.reshape(-1)
    out, off = [], 0
    for shp in shapes:
        size = math.prod(shp)
        out.append(flat[off:off + size].reshape(shp))
        off += size
    return out


def kernel(x, mem, positions, mix_norm, mem_norm, w_mem_kv, ffn_norm, w_gate, w_up, w_down, attn_w_in, attn_w_out, sgu_w_in, sgu_ln_g, sgu_ln_b, sgu_w_spatial, sgu_b_spatial, sgu_w_out, final_norm, loss_target, m_mix_norm, m_mem_norm, m_w_mem_kv, m_ffn_norm, m_w_gate, m_w_up, m_w_down, m_attn_w_in, m_attn_w_out, m_sgu_w_in, m_sgu_ln_g, m_sgu_ln_b, m_sgu_w_spatial, m_sgu_b_spatial, m_sgu_w_out, m_final_norm, v_mix_norm, v_mem_norm, v_w_mem_kv, v_ffn_norm, v_w_gate, v_w_up, v_w_down, v_attn_w_in, v_attn_w_out, v_sgu_w_in, v_sgu_ln_g, v_sgu_ln_b, v_sgu_w_spatial, v_sgu_b_spatial, v_sgu_w_out, v_final_norm):
    seq, d_model = x.shape[1], x.shape[2]
    x0, mem0, tgt = x[0], mem[0], loss_target[0]
    xi, yi, ci = _place()
    chip = 2 * xi + yi
    place = jnp.stack([ci, chip]).astype(jnp.int32)

    big = {"attn_w_in": (attn_w_in, "col"), "w_mem_kv": (w_mem_kv, "row"), "attn_w_out": (attn_w_out, "col"),
           "w_gate": (w_gate, "col"), "w_up": (w_up, "col"), "w_down": (w_down, "row"),
           "sgu_w_in": (sgu_w_in, "col"), "sgu_w_out": (sgu_w_out, "row")}
    names = list(big)
    slabs = [_place_shard(f"place_{k}", big[k][0].reshape(2, -1, big[k][0].shape[-1]), place, BF) for k in names]
    ln_pack = jnp.stack([sgu_ln_g, sgu_ln_b])
    gathered = _all_gather(slabs + [_place_shard("place_ln", ln_pack, place, F32)])
    weights = {}
    for k, gat in zip(names, gathered):
        shard = big[k][0]
        weights[k] = Weight(gat.reshape(N_CHIPS, *shard.shape), big[k][1])
    ln_all = gathered[-1]
    ln_g = ln_all[:, 0, 0, :].reshape(1, SGU_W)
    ln_b = ln_all[:, 1, 0, :].reshape(1, SGU_W)
    w_sp = sgu_w_spatial[0]
    b_t = sgu_b_spatial[0].T

    tables = _rope_tables(positions[0])

    def residual(acc, extra):
        return [extra[0] + acc[0]]

    def memory_kv(layer):
        mem_n = _rms_fwd(f"mem_norm_{layer}", mem0, mem_norm[layer:layer + 1])
        return mem_n, _mm_nn(f"mem_kv_{layer}", mem_n, weights["w_mem_kv"], layer)[0]

    def ffn_fwd(layer, xin):
        h = _rms_fwd(f"ffn_norm_{layer}", xin, ffn_norm[layer:layer + 1])
        g, u, act = _gate_up(f"gate_up_{layer}", h, weights["w_gate"], weights["w_up"], layer)
        xout = _mm_nn(f"down_{layer}", act, weights["w_down"], layer, extras=[xin], epilogue=residual)[0]
        return xout, (h, g, u, act)

    h0 = _rms_fwd("mix_norm_0", x0, mix_norm[0:1])
    proj0 = _mm_nn("attn_in", h0, weights["attn_w_in"], 0)[0]
    qkv = _rope_fwd(proj0, tables)
    qs, ks, vs = qkv[0:3], qkv[3:6], qkv[6:9]
    outs, lses = [], []
    for g, dil in enumerate(DILATIONS):
        o, l = _dil_fwd(g, qs[g], ks[g], vs[g], dil)
        outs.append(o)
        lses.append(l)
    merged, lse = _attn_merge(outs, lses)
    mem_n0, kv0 = memory_kv(0)
    mem_out0 = _mem_fwd("mem_fwd_0", proj0, 9, kv0)
    cat0 = jnp.concatenate([merged, mem_out0], axis=1)
    x1 = _mm_nn("attn_out", cat0, weights["attn_w_out"], 0, extras=[x0], epilogue=residual)[0]
    x2, ffn_saved0 = ffn_fwd(0, x1)

    h1 = _rms_fwd("mix_norm_1", x2, mix_norm[1:2])
    proj1 = _mm_nn("sgu_in", h1, weights["sgu_w_in"], 0)[0]
    sgu_out = _sgu_fwd(proj1, ln_g, ln_b, w_sp, b_t)
    mem_n1, kv1 = memory_kv(1)
    mem_out1 = _mem_fwd("mem_fwd_1", proj1, 6, kv1)
    cat1 = jnp.concatenate([sgu_out, mem_out1], axis=1)
    x3 = _mm_nn("sgu_out", cat1, weights["sgu_w_out"], 0, extras=[x2], epilogue=residual)[0]
    x4, ffn_saved1 = ffn_fwd(1, x3)

    d4, g_final, loss_part = _final_loss(x4, tgt, final_norm.reshape(1, d_model))
    loss = lax.psum(loss_part[0, 0], ("x", "y", "c"))

    grads = {}

    def ffn_bwd(layer, d_out, xin, saved):
        h, g, u, act = saved
        grads["w_down"] = _mm_tn(f"d_down_{layer}", act, d_out, weights["w_down"], layer, grads.get("w_down"))
        dg, du = _mm_nt(f"d_act_{layer}", [d_out], [weights["w_down"]], layer, out_dtypes=(BF, BF), extras=[g, u],
                        epilogue=_swiglu_bwd_epilogue)
        grads["w_gate"] = _mm_tn(f"d_gate_{layer}", h, dg, weights["w_gate"], layer, grads.get("w_gate"))
        grads["w_up"] = _mm_tn(f"d_up_{layer}", h, du, weights["w_up"], layer, grads.get("w_up"))
        dh = _mm_nt(f"d_ffn_h_{layer}", [dg, du], [weights["w_gate"], weights["w_up"]], layer)[0]
        return _rms_bwd(f"ffn_norm_bwd_{layer}", xin, ffn_norm[layer:layer + 1], dh, d_out)

    def memory_bwd(layer, mem_n, dkv):
        dkv = dkv.astype(BF)
        grads["w_mem_kv"] = _mm_tn(f"d_mem_kv_{layer}", mem_n, dkv, weights["w_mem_kv"], layer, grads.get("w_mem_kv"))
        d_mem_n = _mm_nt(f"d_mem_n_{layer}", [dkv], [weights["w_mem_kv"]], layer)[0]
        return _rms_bwd(f"mem_norm_bwd_{layer}", mem0, mem_norm[layer:layer + 1], d_mem_n)[1]

    d3, g_ffn1 = ffn_bwd(1, d4, x3, ffn_saved1)
    grads["sgu_w_out"] = _mm_tn("d_sgu_out", cat1, d3, weights["sgu_w_out"], 0)
    dcat1 = _mm_nt("d_cat_1", [d3], [weights["sgu_w_out"]], 0)[0]
    dq_mem1, dkv1 = _mem_bwd("mem_bwd_1", proj1, 6, kv1, dcat1, 3)
    g_mem1 = memory_bwd(1, mem_n1, dkv1)
    du_pre, dv_pre, g_wsp, g_bsp_t, g_ln_g, g_ln_b = _sgu_bwd(proj1, dcat1, ln_g, ln_b, w_sp, b_t)
    dproj1 = jnp.concatenate([du_pre, dv_pre, dq_mem1.astype(BF)], axis=1)
    grads["sgu_w_in"] = _mm_tn("d_sgu_in", h1, dproj1, weights["sgu_w_in"], 0)
    dh1 = _mm_nt("d_h_1", [dproj1], [weights["sgu_w_in"]], 0)[0]
    d2, g_mix1 = _rms_bwd("mix_norm_bwd_1", x2, mix_norm[1:2], dh1, d3)

    d1, g_ffn0 = ffn_bwd(0, d2, x1, ffn_saved0)
    grads["attn_w_out"] = _mm_tn("d_attn_out", cat0, d1, weights["attn_w_out"], 0)
    dcat0 = _mm_nt("d_cat_0", [d1], [weights["attn_w_out"]], 0)[0]
    dq_mem0, dkv0 = _mem_bwd("mem_bwd_0", proj0, 9, kv0, dcat0, 1)
    g_mem0 = memory_bwd(0, mem_n0, dkv0)
    d_merged, delta = _attn_delta(dcat0, merged)
    dqs, dks, dvs = [], [], []
    for g, dil in enumerate(DILATIONS):
        dq, dk, dv = _dil_bwd(g, qs[g], ks[g], vs[g], d_merged, lse, delta, dil)
        dqs.append(dq)
        dks.append(dk)
        dvs.append(dv)
    dproj0 = _rope_bwd(dqs + dks + dvs, dq_mem0, tables)
    grads["attn_w_in"] = _mm_tn("d_attn_in", h0, dproj0, weights["attn_w_in"], 0)
    dh0 = _mm_nt("d_h_0", [dproj0], [weights["attn_w_in"]], 0)[0]
    d0, g_mix0 = _rms_bwd("mix_norm_bwd_0", x0, mix_norm[0:1], dh0, d1)

    small_grads = [jnp.concatenate([g_mix0, g_mix1]), jnp.concatenate([g_mem0, g_mem1]),
                   jnp.concatenate([g_ffn0, g_ffn1]), g_wsp, g_bsp_t[:, :SGU_GROUPS].T, g_final, g_ln_g, g_ln_b]
    small_shapes = [t.shape for t in small_grads]
    stacked_grads = [grads[k] for k in names]
    got_pair, small_all = _pair_exchange(stacked_grads, _pack(small_grads, 8))
    pair_sums = [_pair_sum(f"pair_sum_{k}", g, r, place) for k, g, r in zip(names, stacked_grads, got_pair)]
    got_chips = _chip_exchange(pair_sums)
    totals = [_chip_sum(f"chip_sum_{k}", g, r, b, place) for k, g, r, b in zip(names, stacked_grads, got_pair, got_chips)]
    full = _half_exchange(totals)
    small_sum = _unpack(_sum_devices(small_all), small_shapes)
    g_mix, g_mem, g_ffn, g_wsp, g_bsp, g_final, g_ln_g, g_ln_b = small_sum
    shard_w = sgu_ln_g.shape[-1]
    g_ln_g = lax.dynamic_slice_in_dim(g_ln_g, chip * shard_w, shard_w, axis=1)
    g_ln_b = lax.dynamic_slice_in_dim(g_ln_b, chip * shard_w, shard_w, axis=1)

    given_m = dict(mix_norm=m_mix_norm, mem_norm=m_mem_norm, w_mem_kv=m_w_mem_kv, ffn_norm=m_ffn_norm, w_gate=m_w_gate,
                   w_up=m_w_up, w_down=m_w_down, attn_w_in=m_attn_w_in, attn_w_out=m_attn_w_out, sgu_w_in=m_sgu_w_in,
                   sgu_ln_g=m_sgu_ln_g, sgu_ln_b=m_sgu_ln_b, sgu_w_spatial=m_sgu_w_spatial,
                   sgu_b_spatial=m_sgu_b_spatial, sgu_w_out=m_sgu_w_out, final_norm=m_final_norm)
    given_v = dict(mix_norm=v_mix_norm, mem_norm=v_mem_norm, w_mem_kv=v_w_mem_kv, ffn_norm=v_ffn_norm, w_gate=v_w_gate,
                   w_up=v_w_up, w_down=v_w_down, attn_w_in=v_attn_w_in, attn_w_out=v_attn_w_out, sgu_w_in=v_sgu_w_in,
                   sgu_ln_g=v_sgu_ln_g, sgu_ln_b=v_sgu_ln_b, sgu_w_spatial=v_sgu_w_spatial,
                   sgu_b_spatial=v_sgu_b_spatial, sgu_w_out=v_sgu_w_out, final_norm=v_final_norm)
    given_w = dict(mix_norm=mix_norm, mem_norm=mem_norm, w_mem_kv=w_mem_kv, ffn_norm=ffn_norm, w_gate=w_gate, w_up=w_up,
                   w_down=w_down, attn_w_in=attn_w_in, attn_w_out=attn_w_out, sgu_w_in=sgu_w_in, sgu_ln_g=sgu_ln_g,
                   sgu_ln_b=sgu_ln_b, sgu_w_spatial=sgu_w_spatial, sgu_b_spatial=sgu_b_spatial, sgu_w_out=sgu_w_out,
                   final_norm=final_norm)
    order = ["mix_norm", "mem_norm", "w_mem_kv", "ffn_norm", "w_gate", "w_up", "w_down", "attn_w_in", "attn_w_out",
             "sgu_w_in", "sgu_ln_g", "sgu_ln_b", "sgu_w_spatial", "sgu_b_spatial", "sgu_w_out", "final_norm"]
    out_g, out_d, out_m, out_v = {}, {}, {}, {}
    for k, gfull in zip(names, full):
        shp = given_w[k].shape
        cols = shp[-1]
        g2 = gfull.reshape(-1, cols)
        d, nm, nv = _adamw(f"adamw_{k}", given_w[k].reshape(-1, cols), g2, given_m[k].reshape(-1, cols),
                           given_v[k].reshape(-1, cols))
        out_g[k], out_d[k], out_m[k], out_v[k] = (t.reshape(shp) for t in (g2, d, nm, nv))
    small_names = ["mix_norm", "mem_norm", "ffn_norm", "sgu_w_spatial", "sgu_b_spatial", "final_norm", "sgu_ln_g",
                   "sgu_ln_b"]
    small_g = [g_mix, g_mem, g_ffn, g_wsp, g_bsp, g_final, g_ln_g, g_ln_b]
    small_shapes = [given_w[k].shape for k in small_names]
    d, nm, nv = _adamw("adamw_small", _pack([given_w[k] for k in small_names], 8), _pack(small_g, 8),
                       _pack([given_m[k] for k in small_names], 8), _pack([given_v[k] for k in small_names], 8))
    for k, gk, dk, mk, vk in zip(small_names, small_g, _unpack(d, small_shapes), _unpack(nm, small_shapes),
                                 _unpack(nv, small_shapes)):
        out_g[k], out_d[k], out_m[k], out_v[k] = gk.reshape(given_w[k].shape), dk, mk, vk

    return (loss, d0[None], *[out_g[k] for k in order], *[out_d[k] for k in order], *[out_m[k] for k in order],
            *[out_v[k] for k in order])
```

```python
import math

import jax
import jax.numpy as jnp
from jax import lax
from jax.experimental import pallas as pl
from jax.experimental.pallas import tpu as pltpu

F32 = jnp.float32
BF = jnp.bfloat16
MESH = pl.DeviceIdType.MESH

HEAD_DIM = 128
MEM_HEADS = 4
MEM_W = MEM_HEADS * HEAD_DIM
GROUP_W = 4 * HEAD_DIM
DILATIONS = (1, 4, 16)
BLK = 128
SGU_GROUPS = 12
SGU_W = SGU_GROUPS * HEAD_DIM
ROT_HALF = 16
ROPE_THETA = 500000.0
NORM_EPS = 1e-6
LN_EPS = 1e-5
NEG_INF = -1e30
SCALE = HEAD_DIM ** -0.5
ADAM_LR, ADAM_B1, ADAM_B2, ADAM_EPS, ADAM_WD, ADAM_STEP = 0.001, 0.9, 0.999, 1e-08, 0.01, 10

VMEM_LIMIT = 48 * 2 ** 20
VMEM_TILE_BUDGET = 36 * 2 ** 20
N_CHIPS = 4
N_DEV = 8
LANES = 128

NT_DIMS = (((1,), (1,)), ((), ()))
TN_DIMS = (((0,), (0,)), ((), ()))
NN_DIMS = (((1,), (0,)), ((), ()))

ANY = pl.BlockSpec(memory_space=pl.ANY)
HBM = pl.BlockSpec(memory_space=pltpu.HBM)
SEM = pl.BlockSpec(memory_space=pltpu.SEMAPHORE)
EFFECT = pltpu.SideEffectType.DATAFLOW_SIDE_EFFECTING


def _params(sem):
    return pltpu.CompilerParams(dimension_semantics=sem, vmem_limit_bytes=VMEM_LIMIT)


def _pick(n, cap):
    if n <= cap:
        return n
    best = None
    for t in range(LANES, cap + 1, LANES):
        if n % t == 0:
            best = t
    assert best is not None, (n, cap)
    return best


def _pick_rows(n, cap):
    t = min(n, cap)
    while n % t:
        t //= 2
    return t


def _mm(name, dims, a_list, a_specs, b_list, b_specs, pairs, n_acc, acc_shape, grid, extras, e_specs,
        out_shapes, out_specs, epilogue, after=()):
    na, nb, ne, no = len(a_list), len(b_list), len(extras), len(out_shapes)
    nk = grid[-1]

    def products(a, b):
        sums = [None] * n_acc
        for ai, bi, ci in pairs:
            prod = lax.dot_general(a[ai][...].astype(BF), b[bi][...].astype(BF), dims, preferred_element_type=F32)
            sums[ci] = prod if sums[ci] is None else sums[ci] + prod
        return sums

    def body(*refs):
        a = refs[:na]
        b = refs[na:na + nb]
        e = refs[na + nb:na + nb + ne]
        off = na + nb + ne + len(after)
        o = refs[off:off + no]
        acc = refs[off + no:]

        def finish(sums):
            outs = epilogue(sums, [r[...] for r in e])
            for r, v in zip(o, outs):
                r[...] = v.astype(r.dtype)

        if nk == 1:
            finish(products(a, b))
            return
        k = pl.program_id(len(grid) - 1)

        @pl.when(k == 0)
        def _():
            for c, v in zip(acc, products(a, b)):
                c[...] = v

        @pl.when(jnp.logical_and(k > 0, k < nk - 1))
        def _():
            for c, v in zip(acc, products(a, b)):
                c[...] += v

        @pl.when(k == nk - 1)
        def _():
            finish([c[...] + v for c, v in zip(acc, products(a, b))])

    ins = [*a_list, *b_list, *extras, *after]
    in_specs = [*a_specs, *b_specs, *e_specs, *([ANY] * len(after))]
    sem = ("parallel",) * (len(grid) - 1) + ("arbitrary",)
    scratch = [] if nk == 1 else [pltpu.VMEM(acc_shape, F32)] * n_acc
    return pl.pallas_call(
        body, out_shape=out_shapes, grid=grid, in_specs=in_specs, out_specs=out_specs, scratch_shapes=scratch,
        name=name, compiler_params=_params(sem))(*ins)


def _tile_bytes(blocks):
    return sum(2 * math.prod(s) * jnp.dtype(d).itemsize for s, d in blocks)


def _first(acc, extra):
    return [acc[0]]


class Weight:
    def __init__(self, arr, axis):
        self.arr, self.axis = arr, axis
        _, self.rows, self.cols = arr.shape


def _mm_nn(name, a, w, extras=(), epilogue=_first, out_dtypes=(F32,), after=()):
    m, kdim = a.shape
    if w.axis == "col":
        n_total = N_CHIPS * w.cols
        tn = _pick(w.cols, 1408)
        tk = _pick(kdim, 2048)
        ncb = w.cols // tn
        gn, gk = N_CHIPS * ncb, kdim // tk
        b_map = lambda n, i, k: (n // ncb, k, n % ncb)
    else:
        n_total = w.cols
        tn = _pick(w.cols, 1024)
        tk = _pick(w.rows, 1408)
        nkb = w.rows // tk
        gn, gk = n_total // tn, N_CHIPS * nkb
        b_map = lambda n, i, k: (k // nkb, k % nkb, n)
    for tm in (1024, 512, 256, 128):
        if m % tm:
            continue
        blocks = [((tm, tk), a.dtype), ((tk, tn), BF)] + [((tm, tn), e.dtype) for e in extras]
        blocks += [((tm, tn), d) for d in out_dtypes] + [((tm, tn), BF)]
        if _tile_bytes(blocks) <= VMEM_TILE_BUDGET:
            break
    o_spec = pl.BlockSpec((tm, tn), lambda n, i, k: (i, n))
    return _mm(
        name, NN_DIMS, [a], [pl.BlockSpec((tm, tk), lambda n, i, k: (i, k))],
        [w.arr], [pl.BlockSpec((None, tk, tn), b_map)], [(0, 0, 0)], 1, (tm, tn), (gn, m // tm, gk),
        list(extras), [o_spec] * len(extras),
        [jax.ShapeDtypeStruct((m, n_total), d) for d in out_dtypes], [o_spec] * len(out_dtypes), epilogue, after)


def _gate_up(name, h, wg, wu):
    m, kdim = h.shape
    tn = _pick(wg.cols, 1408)
    tk = _pick(kdim, 2048)
    ncb = wg.cols // tn
    for tm in (512, 256, 128):
        blocks = [((tm, tk), BF), ((tk, tn), BF), ((tk, tn), BF), ((tm, tn), F32), ((tm, tn), F32), ((tm, tn), BF)]
        if m % tm == 0 and _tile_bytes(blocks) <= VMEM_TILE_BUDGET:
            break
    b_spec = pl.BlockSpec((None, tk, tn), lambda n, i, k: (n // ncb, k, n % ncb))
    o_spec = pl.BlockSpec((tm, tn), lambda n, i, k: (i, n))
    n_total = N_CHIPS * wg.cols

    def epilogue(acc, extra):
        g, u = acc
        return [g, u, g * (1.0 / (1.0 + jnp.exp(-g))) * u]

    return _mm(
        name, NN_DIMS, [h], [pl.BlockSpec((tm, tk), lambda n, i, k: (i, k))], [wg.arr, wu.arr], [b_spec, b_spec],
        [(0, 0, 0), (0, 1, 1)], 2, (tm, tn), (N_CHIPS * ncb, m // tm, kdim // tk), [], [],
        [jax.ShapeDtypeStruct((m, n_total), F32)] * 2 + [jax.ShapeDtypeStruct((m, n_total), BF)], [o_spec] * 3,
        epilogue)


def _mm_nt(name, dys, ws, out_dtypes=(F32,), extras=(), epilogue=_first, after=()):
    m = dys[0].shape[0]
    w0 = ws[0]
    npair = len(dys)
    if w0.axis == "col":
        k_total = w0.rows
        tko = _pick(k_total, 1024)
        tkc = _pick(w0.cols, 1408)
        nkb = w0.cols // tkc
        go, gk = k_total // tko, N_CHIPS * nkb
        b_map = lambda o, i, k: (k // nkb, o, k % nkb)
    else:
        k_total = N_CHIPS * w0.rows
        tko = _pick(w0.rows, 1408)
        tkc = _pick(w0.cols, 2048)
        nob = w0.rows // tko
        go, gk = N_CHIPS * nob, w0.cols // tkc
        b_map = lambda o, i, k: (o // nob, o % nob, k)
    for tm in (1024, 512, 256, 128):
        if m % tm:
            continue
        blocks = [((tm, tkc), d.dtype) for d in dys] + [((tko, tkc), BF)] * npair
        blocks += [((tm, tko), e.dtype) for e in extras] + [((tm, tko), d) for d in out_dtypes]
        blocks += [((tm, tko), BF)]
        if _tile_bytes(blocks) <= VMEM_TILE_BUDGET:
            break
    o_spec = pl.BlockSpec((tm, tko), lambda o, i, k: (i, o))
    return _mm(
        name, NT_DIMS, list(dys), [pl.BlockSpec((tm, tkc), lambda o, i, k: (i, k))] * npair,
        [w.arr for w in ws], [pl.BlockSpec((None, tko, tkc), b_map)] * npair,
        [(i, i, 0) for i in range(npair)], 1, (tm, tko), (go, m // tm, gk), list(extras), [o_spec] * len(extras),
        [jax.ShapeDtypeStruct((m, k_total), d) for d in out_dtypes], [o_spec] * len(out_dtypes), epilogue, after)


def _mm_tn(name, a, dy, w, after=()):
    m = a.shape[0]
    rows2 = w.rows // 2
    tn = _pick(w.cols, 1408)
    tmk = _pick_rows(m, 1024)
    ncb = w.cols // tn
    epilogue = _first
    if rows2 % LANES:
        tkr = w.rows
        assert w.axis == "row"
        gr, gn = N_CHIPS, ncb
        o_spec = pl.BlockSpec((2, None, rows2, tn), lambda r, n, t: (0, r, 0, n))
        epilogue = lambda acc, extra: [acc[0].reshape(2, rows2, tn)]
    else:
        tkr = _pick(rows2, 1408)
        nrb = rows2 // tkr
        if w.axis == "col":
            gr, gn = w.rows // tkr, N_CHIPS * ncb
            o_map = lambda r, n, t: (r // nrb, n // ncb, r % nrb, n % ncb)
        else:
            per = w.rows // tkr
            gr, gn = N_CHIPS * per, ncb
            o_map = lambda r, n, t: ((r % per) // nrb, r // per, (r % per) % nrb, n)
        o_spec = pl.BlockSpec((None, None, tkr, tn), o_map)
    return _mm(
        name, TN_DIMS, [a], [pl.BlockSpec((tmk, tkr), lambda r, n, t: (t, r))],
        [dy], [pl.BlockSpec((tmk, tn), lambda r, n, t: (t, n))], [(0, 0, 0)], 1, (tkr, tn), (gr, gn, m // tmk), [], [],
        [jax.ShapeDtypeStruct((2, N_CHIPS, rows2, w.cols), BF)], [o_spec], epilogue, after)[0]


def _rms_fwd(name, x, g, after=()):
    s, d = x.shape
    tr = _pick_rows(s, 512)

    def body(x_ref, g_ref, *rest):
        h_ref = rest[-1]
        xf = x_ref[...]
        r = lax.rsqrt(jnp.mean(xf * xf, axis=-1, keepdims=True) + NORM_EPS)
        h_ref[...] = (xf * r * g_ref[...]).astype(BF)

    return pl.pallas_call(
        body, out_shape=jax.ShapeDtypeStruct((s, d), BF), grid=(s // tr,),
        in_specs=[pl.BlockSpec((tr, d), lambda i: (i, 0)), pl.BlockSpec((1, d), lambda i: (0, 0))] + [ANY] * len(after),
        out_specs=pl.BlockSpec((tr, d), lambda i: (i, 0)), name=name, compiler_params=_params(("parallel",)))(x, g, *after)


def _rms_bwd(name, x, g, dh, dres=None):
    s, d = x.shape
    tr = _pick_rows(s, 256)
    has_res = dres is not None

    def body(*refs):
        if has_res:
            x_ref, g_ref, dh_ref, dres_ref, dx_ref, dg_ref = refs
        else:
            x_ref, g_ref, dh_ref, dx_ref, dg_ref = refs
        xf = x_ref[...]
        r = lax.rsqrt(jnp.mean(xf * xf, axis=-1, keepdims=True) + NORM_EPS)
        xr = xf * r
        dy = dh_ref[...]
        a = dy * g_ref[...]
        dx = r * (a - xr * jnp.mean(a * xr, axis=-1, keepdims=True))
        if has_res:
            dx = dx + dres_ref[...]
        dx_ref[...] = dx

        @pl.when(pl.program_id(0) == 0)
        def _():
            dg_ref[...] = jnp.zeros_like(dg_ref)

        dg_ref[...] += jnp.sum(dy * xr, axis=0, keepdims=True)

    row = pl.BlockSpec((tr, d), lambda i: (i, 0))
    vec = pl.BlockSpec((1, d), lambda i: (0, 0))
    ins = [x, g, dh] + ([dres] if has_res else [])
    in_specs = [row, vec, row] + ([row] if has_res else [])
    return pl.pallas_call(
        body, out_shape=[jax.ShapeDtypeStruct((s, d), F32), jax.ShapeDtypeStruct((1, d), F32)], grid=(s // tr,),
        in_specs=in_specs, out_specs=[row, vec], name=name, compiler_params=_params(("arbitrary",)))(*ins)


def _final_loss(x, tgt, g):
    s, d = x.shape
    tr = _pick_rows(s, 256)

    def body(x_ref, t_ref, g_ref, dx_ref, dg_ref, loss_ref):
        xf = x_ref[...]
        gain = g_ref[...]
        r = lax.rsqrt(jnp.mean(xf * xf, axis=-1, keepdims=True) + NORM_EPS)
        xr = xf * r
        err = xr * gain - t_ref[...]
        dy = err * (1.0 / d)
        a = dy * gain
        dx_ref[...] = r * (a - xr * jnp.mean(a * xr, axis=-1, keepdims=True))

        @pl.when(pl.program_id(0) == 0)
        def _():
            dg_ref[...] = jnp.zeros_like(dg_ref)
            loss_ref[...] = jnp.zeros_like(loss_ref)

        dg_ref[...] += jnp.sum(dy * xr, axis=0, keepdims=True)
        part = 0.5 * jnp.sum(jnp.mean(err * err, axis=-1, keepdims=True), axis=0, keepdims=True)
        loss_ref[...] += jnp.broadcast_to(part, loss_ref.shape)

    row = pl.BlockSpec((tr, d), lambda i: (i, 0))
    vec = pl.BlockSpec((1, d), lambda i: (0, 0))
    return pl.pallas_call(
        body, out_shape=[jax.ShapeDtypeStruct((s, d), F32), jax.ShapeDtypeStruct((1, d), F32),
                         jax.ShapeDtypeStruct((8, LANES), F32)],
        grid=(s // tr,), in_specs=[row, row, vec], out_specs=[row, vec, pl.BlockSpec((8, LANES), lambda i: (0, 0))],
        name="final_loss", compiler_params=_params(("arbitrary",)))(x, tgt, g)


def _swiglu_bwd_epilogue(acc, extra):
    dact = acc[0]
    g, u = extra
    sig = 1.0 / (1.0 + jnp.exp(-g))
    return [dact * u * sig * (1.0 + g * (1.0 - sig)), dact * g * sig]


GELU_C = math.sqrt(2.0 / math.pi)
GELU_A = 0.044715


def _gelu(x):
    return 0.5 * x * (1.0 + jnp.tanh(GELU_C * (x + GELU_A * x * x * x)))


def _gelu_grad(x):
    t = jnp.tanh(GELU_C * (x + GELU_A * x * x * x))
    return 0.5 * (1.0 + t) + 0.5 * x * (1.0 - t * t) * GELU_C * (1.0 + 3.0 * GELU_A * x * x)


def _rope_tables(positions):
    inv_freq = ROPE_THETA ** (-jnp.arange(ROT_HALF, dtype=F32) / ROT_HALF)
    ang = positions.astype(F32)[:, None] * inv_freq
    cos, sin = jnp.cos(ang), jnp.sin(ang)
    s = ang.shape[0]
    rest = HEAD_DIM - 2 * ROT_HALF
    zeros = jnp.zeros((s, ROT_HALF), F32)
    cos_t = jnp.concatenate([cos, cos, jnp.ones((s, rest), F32)], axis=1)
    sin_a = jnp.concatenate([-sin, zeros, jnp.zeros((s, rest), F32)], axis=1)
    sin_b = jnp.concatenate([zeros, sin, jnp.zeros((s, rest), F32)], axis=1)
    return cos_t, sin_a, sin_b


def _rope_heads(x, cos_t, sin_a, sin_b):
    outs = []
    for h in range(GROUP_W // HEAD_DIM):
        xh = x[:, h * HEAD_DIM:(h + 1) * HEAD_DIM]
        up = pltpu.roll(xh, HEAD_DIM - ROT_HALF, 1)
        down = pltpu.roll(xh, ROT_HALF, 1)
        outs.append(xh * cos_t + up * sin_a + down * sin_b)
    return outs


def _rope_fwd(proj, tables):
    s = proj.shape[0]
    tm = _pick_rows(s, 512)
    nparts = 9

    def body(p_ref, c_ref, sa_ref, sb_ref, *outs):
        part = pl.program_id(1)
        for p in range(nparts):
            @pl.when(part == p)
            def _(p=p):
                xv = p_ref[...]
                if p < 6:
                    heads = _rope_heads(xv, c_ref[...], sa_ref[...], sb_ref[...])
                    for h, v in enumerate(heads):
                        outs[p][:, h * HEAD_DIM:(h + 1) * HEAD_DIM] = v.astype(BF)
                else:
                    outs[p][...] = xv.astype(BF)

    tab = pl.BlockSpec((tm, HEAD_DIM), lambda i, p: (i, 0))
    o_spec = pl.BlockSpec((tm, GROUP_W), lambda i, p: (i, 0))
    return pl.pallas_call(
        body, out_shape=[jax.ShapeDtypeStruct((s, GROUP_W), BF)] * nparts, grid=(s // tm, nparts),
        in_specs=[pl.BlockSpec((tm, GROUP_W), lambda i, p: (i, p)), tab, tab, tab], out_specs=[o_spec] * nparts,
        name="rope_fwd", compiler_params=_params(("parallel", "arbitrary")))(proj, *tables)


def _rope_bwd(parts, dq_mem, tables):
    s = dq_mem.shape[0]
    tm = _pick_rows(s, 256)
    nparts = len(parts) + 1

    def body(*refs):
        ins = refs[:nparts]
        c_ref, sa_ref, sb_ref, o_ref = refs[nparts:]
        part = pl.program_id(1)
        for p in range(nparts):
            @pl.when(part == p)
            def _(p=p):
                xv = ins[p][...]
                if p < 6:
                    heads = _rope_heads(xv, c_ref[...], -sa_ref[...], -sb_ref[...])
                    for h, v in enumerate(heads):
                        o_ref[:, h * HEAD_DIM:(h + 1) * HEAD_DIM] = v.astype(BF)
                else:
                    o_ref[...] = xv.astype(BF)

    i_spec = pl.BlockSpec((tm, GROUP_W), lambda i, p: (i, 0))
    tab = pl.BlockSpec((tm, HEAD_DIM), lambda i, p: (i, 0))
    return pl.pallas_call(
        body, out_shape=jax.ShapeDtypeStruct((s, nparts * GROUP_W), BF), grid=(s // tm, nparts),
        in_specs=[i_spec] * nparts + [tab] * 3, out_specs=pl.BlockSpec((tm, GROUP_W), lambda i, p: (i, p)),
        name="rope_bwd", compiler_params=_params(("parallel", "arbitrary")))(*parts, dq_mem, *tables)


def _band_masks(n):
    qi = lax.broadcasted_iota(jnp.int32, (BLK, BLK), 0)
    ki = lax.broadcasted_iota(jnp.int32, (BLK, BLK), 1)
    return qi >= ki, jnp.logical_and(ki >= qi, n > 0)


def _dil_fwd(g, q, k, v, dil):
    s = q.shape[0]
    length = s // dil
    nb = length // BLK
    view = lambda t: t.reshape(length, dil * GROUP_W)

    def body(q_ref, kc_ref, kp_ref, vc_ref, vp_ref, o_ref, lse_ref):
        n = pl.program_id(1)
        mask_c, mask_p = _band_masks(n)
        for h in range(GROUP_W // HEAD_DIM):
            sl = slice(h * HEAD_DIM, (h + 1) * HEAD_DIM)
            qh = q_ref[:, sl]
            sc = lax.dot_general(qh, kc_ref[:, sl], NT_DIMS, preferred_element_type=F32) * SCALE
            sp = lax.dot_general(qh, kp_ref[:, sl], NT_DIMS, preferred_element_type=F32) * SCALE
            sc = jnp.where(mask_c, sc, NEG_INF)
            sp = jnp.where(mask_p, sp, NEG_INF)
            mx = jnp.maximum(jnp.max(sc, axis=-1, keepdims=True), jnp.max(sp, axis=-1, keepdims=True))
            pc = jnp.exp(sc - mx)
            pp = jnp.exp(sp - mx)
            den = jnp.sum(pc, axis=-1, keepdims=True) + jnp.sum(pp, axis=-1, keepdims=True)
            acc = jnp.dot(pc.astype(BF), vc_ref[:, sl], preferred_element_type=F32)
            acc += jnp.dot(pp.astype(BF), vp_ref[:, sl], preferred_element_type=F32)
            o_ref[:, sl] = acc / den
            lse_ref[:, sl] = jnp.broadcast_to(mx + jnp.log(den), (BLK, HEAD_DIM))

    cur = pl.BlockSpec((BLK, GROUP_W), lambda r, n: (n, r))
    prev = pl.BlockSpec((BLK, GROUP_W), lambda r, n: (jnp.maximum(n - 1, 0), r))
    o, lse = pl.pallas_call(
        body, out_shape=[jax.ShapeDtypeStruct((length, dil * GROUP_W), F32)] * 2, grid=(dil, nb),
        in_specs=[cur, cur, prev, cur, prev], out_specs=[cur, cur], name=f"dil_fwd_{g}",
        compiler_params=_params(("parallel", "arbitrary")))(view(q), view(k), view(k), view(v), view(v))
    return o.reshape(s, GROUP_W), lse.reshape(s, GROUP_W)


def _dil_bwd(g, q, k, v, do, lse, delta, dil):
    s = q.shape[0]
    length = s // dil
    nb = length // BLK
    view = lambda t: t.reshape(length, dil * GROUP_W)

    def body(q_ref, kc_ref, kp_ref, vc_ref, vp_ref, do_ref, lse_ref, dl_ref, dq_ref, dk_ref, dv_ref, ck_ref, cv_ref):
        n = pl.program_id(1)
        live = n < nb
        mask_c, mask_p = _band_masks(n)
        mask_c = jnp.logical_and(mask_c, live)
        mask_p = jnp.logical_and(mask_p, live)

        @pl.when(n == 0)
        def _():
            ck_ref[...] = jnp.zeros_like(ck_ref)
            cv_ref[...] = jnp.zeros_like(cv_ref)

        for h in range(GROUP_W // HEAD_DIM):
            sl = slice(h * HEAD_DIM, (h + 1) * HEAD_DIM)
            qh, kc, kp, vc, vp, doh = q_ref[:, sl], kc_ref[:, sl], kp_ref[:, sl], vc_ref[:, sl], vp_ref[:, sl], do_ref[:, sl]
            lse_h = lse_ref[:, sl]
            dl_h = dl_ref[:, sl]
            sc = lax.dot_general(qh, kc, NT_DIMS, preferred_element_type=F32) * SCALE
            sp = lax.dot_general(qh, kp, NT_DIMS, preferred_element_type=F32) * SCALE
            pc = jnp.where(mask_c, jnp.exp(jnp.minimum(sc - lse_h, 0.0)), 0.0)
            pp = jnp.where(mask_p, jnp.exp(jnp.minimum(sp - lse_h, 0.0)), 0.0)
            dpc = lax.dot_general(doh, vc, NT_DIMS, preferred_element_type=F32)
            dpp = lax.dot_general(doh, vp, NT_DIMS, preferred_element_type=F32)
            dsc = (pc * (dpc - dl_h) * SCALE).astype(BF)
            dsp = (pp * (dpp - dl_h) * SCALE).astype(BF)
            dq = jnp.dot(dsc, kc, preferred_element_type=F32) + jnp.dot(dsp, kp, preferred_element_type=F32)

            @pl.when(live)
            def _(dq=dq, sl=sl):
                dq_ref[:, sl] = dq

            dk_ref[:, sl] = ck_ref[:, sl] + lax.dot_general(dsp, qh, TN_DIMS, preferred_element_type=F32)
            dv_ref[:, sl] = cv_ref[:, sl] + lax.dot_general(pp.astype(BF), doh, TN_DIMS, preferred_element_type=F32)
            ck_ref[:, sl] = lax.dot_general(dsc, qh, TN_DIMS, preferred_element_type=F32)
            cv_ref[:, sl] = lax.dot_general(pc.astype(BF), doh, TN_DIMS, preferred_element_type=F32)

    last = nb - 1
    cur = pl.BlockSpec((BLK, GROUP_W), lambda r, n: (jnp.minimum(n, last), r))
    prev = pl.BlockSpec((BLK, GROUP_W), lambda r, n: (jnp.maximum(n - 1, 0), r))
    shape = jax.ShapeDtypeStruct((length, dil * GROUP_W), F32)
    dq, dk, dv = pl.pallas_call(
        body, out_shape=[shape] * 3, grid=(dil, nb + 1),
        in_specs=[cur, cur, prev, cur, prev, cur, cur, cur], out_specs=[cur, prev, prev],
        scratch_shapes=[pltpu.VMEM((BLK, GROUP_W), F32)] * 2, name=f"dil_bwd_{g}",
        compiler_params=_params(("parallel", "arbitrary")))(
            view(q), view(k), view(k), view(v), view(v), view(do), view(lse), view(delta))
    return dq.reshape(s, GROUP_W), dk.reshape(s, GROUP_W), dv.reshape(s, GROUP_W)


def _attn_merge(outs, lses):
    s = outs[0].shape[0]
    tr = _pick_rows(s, 512)

    def body(o0, o1, o2, l0, l1, l2, m_ref, lse_ref):
        a, b, c = l0[...], l1[...], l2[...]
        mx = jnp.maximum(jnp.maximum(a, b), c)
        ea, eb, ec = jnp.exp(a - mx), jnp.exp(b - mx), jnp.exp(c - mx)
        den = ea + eb + ec
        m_ref[...] = (ea * o0[...] + eb * o1[...] + ec * o2[...]) / den
        lse_ref[...] = mx + jnp.log(den)

    spec = pl.BlockSpec((tr, GROUP_W), lambda i: (i, 0))
    return pl.pallas_call(
        body, out_shape=[jax.ShapeDtypeStruct((s, GROUP_W), F32)] * 2, grid=(s // tr,), in_specs=[spec] * 6,
        out_specs=[spec] * 2, name="attn_merge", compiler_params=_params(("parallel",)))(*outs, *lses)


def _attn_delta(dcat, merged, after=()):
    s = merged.shape[0]
    tr = _pick_rows(s, 512)

    def body(d_ref, m_ref, *rest):
        do_ref, dl_ref = rest[-2:]
        d = d_ref[...]
        prod = d * m_ref[...]
        do_ref[...] = d.astype(BF)
        for h in range(GROUP_W // HEAD_DIM):
            sl = slice(h * HEAD_DIM, (h + 1) * HEAD_DIM)
            dl_ref[:, sl] = jnp.broadcast_to(jnp.sum(prod[:, sl], axis=-1, keepdims=True), (tr, HEAD_DIM))

    spec = pl.BlockSpec((tr, GROUP_W), lambda i: (i, 0))
    return pl.pallas_call(
        body, out_shape=[jax.ShapeDtypeStruct((s, GROUP_W), BF), jax.ShapeDtypeStruct((s, GROUP_W), F32)],
        grid=(s // tr,), in_specs=[spec, spec] + [ANY] * len(after), out_specs=[spec, spec], name="attn_delta",
        compiler_params=_params(("parallel",)))(dcat, merged, *after)


def _mem_probs(qh, kh):
    sc = lax.dot_general(qh, kh, NT_DIMS, preferred_element_type=F32) * SCALE
    p = jnp.exp(sc - jnp.max(sc, axis=-1, keepdims=True))
    return p, jnp.sum(p, axis=-1, keepdims=True)


def _mem_fwd(name, proj, q_block, kv):
    s = proj.shape[0]
    tq = _pick_rows(s, 512)

    def body(q_ref, kv_ref, o_ref):
        for h in range(MEM_HEADS):
            sl = slice(h * HEAD_DIM, (h + 1) * HEAD_DIM)
            vsl = slice(MEM_W + h * HEAD_DIM, MEM_W + (h + 1) * HEAD_DIM)
            p, den = _mem_probs(q_ref[:, sl].astype(BF), kv_ref[:, sl].astype(BF))
            o_ref[:, sl] = jnp.dot(p.astype(BF), kv_ref[:, vsl].astype(BF), preferred_element_type=F32) / den

    return pl.pallas_call(
        body, out_shape=jax.ShapeDtypeStruct((s, MEM_W), F32), grid=(s // tq,),
        in_specs=[pl.BlockSpec((tq, MEM_W), lambda i: (i, q_block)), pl.BlockSpec(kv.shape, lambda i: (0, 0))],
        out_specs=pl.BlockSpec((tq, MEM_W), lambda i: (i, 0)), name=name, compiler_params=_params(("parallel",)))(proj, kv)


def _mem_bwd(name, proj, q_block, kv, dcat, d_block):
    s = proj.shape[0]
    tq = _pick_rows(s, 512)

    def body(q_ref, kv_ref, do_ref, dq_ref, dkv_ref):
        @pl.when(pl.program_id(0) == 0)
        def _():
            dkv_ref[...] = jnp.zeros_like(dkv_ref)

        for h in range(MEM_HEADS):
            sl = slice(h * HEAD_DIM, (h + 1) * HEAD_DIM)
            vsl = slice(MEM_W + h * HEAD_DIM, MEM_W + (h + 1) * HEAD_DIM)
            qh, kh, vh = q_ref[:, sl].astype(BF), kv_ref[:, sl].astype(BF), kv_ref[:, vsl].astype(BF)
            doh = do_ref[:, sl].astype(BF)
            p, den = _mem_probs(qh, kh)
            p = p / den
            dp = lax.dot_general(doh, vh, NT_DIMS, preferred_element_type=F32)
            ds = (p * (dp - jnp.sum(p * dp, axis=-1, keepdims=True)) * SCALE).astype(BF)
            dq_ref[:, sl] = jnp.dot(ds, kh, preferred_element_type=F32)
            dkv_ref[:, sl] += lax.dot_general(ds, qh, TN_DIMS, preferred_element_type=F32)
            dkv_ref[:, vsl] += lax.dot_general(p.astype(BF), doh, TN_DIMS, preferred_element_type=F32)

    whole = pl.BlockSpec(kv.shape, lambda i: (0, 0))
    return pl.pallas_call(
        body, out_shape=[jax.ShapeDtypeStruct((s, MEM_W), F32), jax.ShapeDtypeStruct(kv.shape, F32)], grid=(s // tq,),
        in_specs=[pl.BlockSpec((tq, MEM_W), lambda i: (i, q_block)), whole,
                  pl.BlockSpec((tq, MEM_W), lambda i: (i, d_block))],
        out_specs=[pl.BlockSpec((tq, MEM_W), lambda i: (i, 0)), whole], name=name,
        compiler_params=_params(("arbitrary",)))(proj, kv, dcat)


def _causal():
    t = lax.broadcasted_iota(jnp.int32, (BLK, BLK), 0)
    s = lax.broadcasted_iota(jnp.int32, (BLK, BLK), 1)
    return t >= s


def _sgu_norm(v_pre, ln_g, ln_b):
    vg = _gelu(v_pre)
    mu = jnp.mean(vg, axis=-1, keepdims=True)
    cen = vg - mu
    rstd = lax.rsqrt(jnp.mean(cen * cen, axis=-1, keepdims=True) + LN_EPS)
    xhat = cen * rstd
    return xhat, rstd, xhat * ln_g + ln_b


def _sgu_fwd(proj, ln_g, ln_b, w_sp, b_t):
    s = proj.shape[0]

    def body(u_ref, v_ref, g_ref, b_ref, w_ref, bt_ref, o_ref):
        _, _, vn = _sgu_norm(v_ref[...], g_ref[...], b_ref[...])
        vn = vn.astype(BF)
        tri = _causal()
        for grp in range(SGU_GROUPS):
            sl = slice(grp * HEAD_DIM, (grp + 1) * HEAD_DIM)
            w = jnp.where(tri, w_ref[grp], 0.0).astype(BF)
            mixed = jnp.dot(w, vn[:, sl], preferred_element_type=F32) + bt_ref[:, grp:grp + 1]
            o_ref[:, sl] = _gelu(u_ref[:, sl]) * mixed

    vec = pl.BlockSpec((1, SGU_W), lambda i: (0, 0))
    return pl.pallas_call(
        body, out_shape=jax.ShapeDtypeStruct((s, SGU_W), F32), grid=(s // BLK,),
        in_specs=[pl.BlockSpec((BLK, SGU_W), lambda i: (i, 0)), pl.BlockSpec((BLK, SGU_W), lambda i: (i, 1)), vec, vec,
                  pl.BlockSpec(w_sp.shape, lambda i: (0, 0, 0)), pl.BlockSpec(b_t.shape, lambda i: (0, 0))],
        out_specs=pl.BlockSpec((BLK, SGU_W), lambda i: (i, 0)), name="sgu_fwd",
        compiler_params=_params(("parallel",)))(proj, proj, ln_g, ln_b, w_sp, b_t)


def _sgu_bwd(proj, dcat, ln_g, ln_b, w_sp, b_t):
    s = proj.shape[0]

    def body(u_ref, v_ref, d_ref, g_ref, b_ref, w_ref, bt_ref, du_ref, dv_ref, dw_ref, db_ref, dg_ref, dbeta_ref,
             dvn_ref):
        @pl.when(pl.program_id(0) == 0)
        def _():
            dw_ref[...] = jnp.zeros_like(dw_ref)
            db_ref[...] = jnp.zeros_like(db_ref)
            dg_ref[...] = jnp.zeros_like(dg_ref)
            dbeta_ref[...] = jnp.zeros_like(dbeta_ref)

        v_pre = v_ref[...]
        gain = g_ref[...]
        xhat, rstd, vn = _sgu_norm(v_pre, gain, b_ref[...])
        vn = vn.astype(BF)
        tri = _causal()
        lane = lax.broadcasted_iota(jnp.int32, (BLK, HEAD_DIM), 1)
        db_acc = jnp.zeros((BLK, HEAD_DIM), F32)
        for grp in range(SGU_GROUPS):
            sl = slice(grp * HEAD_DIM, (grp + 1) * HEAD_DIM)
            w = jnp.where(tri, w_ref[grp], 0.0).astype(BF)
            vn_g = vn[:, sl]
            mixed = jnp.dot(w, vn_g, preferred_element_type=F32) + bt_ref[:, grp:grp + 1]
            u_pre = u_ref[:, sl]
            d_out = d_ref[:, sl]
            du_ref[:, sl] = (d_out * mixed * _gelu_grad(u_pre)).astype(BF)
            dmixed = d_out * _gelu(u_pre)
            dm = dmixed.astype(BF)
            dvn_ref[:, sl] = lax.dot_general(w, dm, TN_DIMS, preferred_element_type=F32)
            dw = lax.dot_general(dm, vn_g, NT_DIMS, preferred_element_type=F32)
            dw_ref[grp] += jnp.where(tri, dw, 0.0)
            db_acc += jnp.where(lane == grp, jnp.sum(dmixed, axis=-1, keepdims=True), 0.0)
        db_ref[...] += db_acc
        dvn = dvn_ref[...]
        dg_ref[...] += jnp.sum(dvn * xhat, axis=0, keepdims=True)
        dbeta_ref[...] += jnp.sum(dvn, axis=0, keepdims=True)
        dxh = dvn * gain
        dvg = rstd * (dxh - jnp.mean(dxh, axis=-1, keepdims=True) - xhat * jnp.mean(dxh * xhat, axis=-1, keepdims=True))
        dv_ref[...] = (dvg * _gelu_grad(v_pre)).astype(BF)

    vec = pl.BlockSpec((1, SGU_W), lambda i: (0, 0))
    row = pl.BlockSpec((BLK, SGU_W), lambda i: (i, 0))
    w_spec = pl.BlockSpec(w_sp.shape, lambda i: (0, 0, 0))
    sq = pl.BlockSpec((BLK, HEAD_DIM), lambda i: (0, 0))
    return pl.pallas_call(
        body,
        out_shape=[jax.ShapeDtypeStruct((s, SGU_W), BF), jax.ShapeDtypeStruct((s, SGU_W), BF),
                   jax.ShapeDtypeStruct(w_sp.shape, F32), jax.ShapeDtypeStruct((BLK, HEAD_DIM), F32),
                   jax.ShapeDtypeStruct((1, SGU_W), F32), jax.ShapeDtypeStruct((1, SGU_W), F32)],
        grid=(s // BLK,),
        in_specs=[row, pl.BlockSpec((BLK, SGU_W), lambda i: (i, 1)), row, vec, vec, w_spec,
                  pl.BlockSpec(b_t.shape, lambda i: (0, 0))],
        out_specs=[row, row, w_spec, sq, vec, vec], scratch_shapes=[pltpu.VMEM((BLK, SGU_W), F32)], name="sgu_bwd",
        compiler_params=_params(("arbitrary",)))(proj, proj, dcat, ln_g, ln_b, w_sp, b_t)


def _place():
    return lax.axis_index("x"), lax.axis_index("y"), lax.axis_index("c")


def _other_chips(x, y):
    return [(1 - x, y), (x, 1 - y), (1 - x, 1 - y)]


def _peer(x, y, c, mask):
    return (1 - x if mask & 4 else x, 1 - y if mask & 2 else y, 1 - c if mask & 1 else c)


def _in_hbm(a):
    return pltpu.with_memory_space_constraint(a, pltpu.HBM)


def _token_spec():
    return jax.ShapeDtypeStruct((8, LANES), F32), pl.BlockSpec(memory_space=pltpu.VMEM)


def _remote(src, dst, ssem, rsem, to):
    return pltpu.make_async_remote_copy(src_ref=src, dst_ref=dst, send_sem=ssem, recv_sem=rsem, device_id=to,
                                        device_id_type=MESH)


def _place_shard(name, src, layer, place, dtype):
    _, rows, cols = src.shape
    tr = _pick_rows(rows, 512)

    def body(p_ref, s_ref, o_ref):
        o_ref[...] = s_ref[...].astype(dtype)

    grid_spec = pltpu.PrefetchScalarGridSpec(
        num_scalar_prefetch=1, grid=(rows // tr,),
        in_specs=[pl.BlockSpec((None, tr, cols), lambda i, p: (layer, i, 0))],
        out_specs=pl.BlockSpec((None, tr, cols), lambda i, p: (p[1], i, 0)))
    return pl.pallas_call(body, out_shape=jax.ShapeDtypeStruct((N_CHIPS, rows, cols), dtype), grid_spec=grid_spec,
                          name=name, compiler_params=_params(("parallel",)))(place, src)


def _gather_copies(bufs, ssem, rsem):
    x, y, c = _place()
    me = 2 * x + y
    copies = []
    for ai, buf in enumerate(bufs):
        for k, (ox, oy) in enumerate(_other_chips(x, y)):
            copies.append(_remote(buf.at[me], buf.at[me], ssem.at[3 * ai + k], rsem.at[3 * ai + k], (ox, oy, c)))
    return copies


def _reduce_copies(grads, lands, ssem, rsem):
    x, y, c = _place()
    copies = []
    for a, (gr, land) in enumerate(zip(grads, lands)):
        for mask in range(1, N_DEV):
            px, py, pc = _peer(x, y, c, mask)
            copies.append(_remote(gr.at[pc, 2 * px + py], land.at[mask - 1], ssem.at[7 * a + mask - 1],
                                  rsem.at[7 * a + mask - 1], (px, py, pc)))
    return copies


def _half_copies(totals, ssem, rsem):
    x, y, c = _place()
    return [_remote(t.at[c], t.at[c], ssem.at[a], rsem.at[a], (x, y, 1 - c)) for a, t in enumerate(totals)]


def _gather_start(groups):
    flat = [s for grp in groups for s in grp]
    n, ng = len(flat), len(groups)

    def body(*refs):
        ins = refs[:n]
        sems = refs[n:n + 2 * ng]
        token = refs[-1]
        idx = 0
        for gi, grp in enumerate(groups):
            for cp in _gather_copies(ins[idx:idx + len(grp)], sems[2 * gi], sems[2 * gi + 1]):
                cp.start()
            idx += len(grp)
        token[...] = jnp.zeros_like(token)

    tok_shape, tok_spec = _token_spec()
    sem_shapes = []
    for grp in groups:
        sem_shapes += [pltpu.SemaphoreType.DMA((3 * len(grp),))] * 2
    res = pl.pallas_call(
        body, name="gather_start",
        out_shape=(*sem_shapes, *[pltpu.HBM(s.shape, s.dtype) for s in flat], tok_shape),
        in_specs=[HBM] * n, out_specs=(*[SEM] * (2 * ng), *[HBM] * n, tok_spec),
        input_output_aliases={i: 2 * ng + i for i in range(n)},
        compiler_params=pltpu.CompilerParams(has_side_effects=EFFECT))(*[_in_hbm(s) for s in flat])
    out, idx = [], 2 * ng
    for gi, grp in enumerate(groups):
        out.append((res[2 * gi], res[2 * gi + 1], list(res[idx:idx + len(grp)])))
        idx += len(grp)
    return out, res[-1]


def _gather_wait(name, ssem, rsem, slabs, after):
    n = len(slabs)

    def body(*refs):
        for cp in _gather_copies(refs[:n], refs[n], refs[n + 1]):
            cp.wait_send()
            cp.wait_recv()

    return pl.pallas_call(
        body, name=name, out_shape=tuple(pltpu.HBM(s.shape, s.dtype) for s in slabs),
        in_specs=[HBM] * n + [SEM, SEM] + [ANY] * len(after), out_specs=tuple([HBM] * n),
        input_output_aliases={i: i for i in range(n)},
        compiler_params=pltpu.CompilerParams(has_side_effects=EFFECT))(*slabs, ssem, rsem, *after)


def _reduce_start(name, grads):
    n = len(grads)
    lands = [lax.empty((N_DEV - 1, *g.shape[2:]), g.dtype) for g in grads]

    def body(*refs):
        token = refs[-1]
        for cp in _reduce_copies(refs[:n], refs[n:2 * n], refs[2 * n], refs[2 * n + 1]):
            cp.start()
        token[...] = jnp.zeros_like(token)

    tok_shape, tok_spec = _token_spec()
    sems = [pltpu.SemaphoreType.DMA((7 * n,))] * 2
    res = pl.pallas_call(
        body, name=name,
        out_shape=(*sems, *[pltpu.HBM(g.shape, g.dtype) for g in grads], *[pltpu.HBM(l.shape, l.dtype) for l in lands],
                   tok_shape),
        in_specs=[HBM] * (2 * n), out_specs=(SEM, SEM, *[HBM] * (2 * n), tok_spec),
        input_output_aliases={i: 2 + i for i in range(2 * n)},
        compiler_params=pltpu.CompilerParams(has_side_effects=EFFECT))(*[_in_hbm(t) for t in (*grads, *lands)])
    return res[0], res[1], list(res[2:2 + n]), list(res[2 + n:2 + 2 * n]), res[-1]


def _reduce_wait(name, ssem, rsem, grads, lands, after):
    n = len(grads)

    def body(*refs):
        for cp in _reduce_copies(refs[:n], refs[n:2 * n], refs[2 * n], refs[2 * n + 1]):
            cp.wait_send()
            cp.wait_recv()

    res = pl.pallas_call(
        body, name=name, out_shape=tuple(pltpu.HBM(t.shape, t.dtype) for t in (*grads, *lands)),
        in_specs=[HBM] * (2 * n) + [SEM, SEM] + [ANY] * len(after), out_specs=tuple([HBM] * (2 * n)),
        input_output_aliases={i: i for i in range(2 * n)},
        compiler_params=pltpu.CompilerParams(has_side_effects=EFFECT))(*grads, *lands, ssem, rsem, *after)
    return list(res[:n]), list(res[n:])


def _sum_pieces(name, grad, land, place):
    _, _, rows, cols = grad.shape
    tr = _pick_rows(rows, 256)

    def body(p_ref, g_ref, l_ref, o_ref):
        tot = g_ref[...].astype(F32)
        for k in range(N_DEV - 1):
            tot = tot + l_ref[k].astype(F32)
        o_ref[...] = tot

    grid_spec = pltpu.PrefetchScalarGridSpec(
        num_scalar_prefetch=1, grid=(rows // tr,),
        in_specs=[pl.BlockSpec((None, None, tr, cols), lambda i, p: (p[0], p[1], i, 0)),
                  pl.BlockSpec((N_DEV - 1, tr, cols), lambda i, p: (0, i, 0))],
        out_specs=pl.BlockSpec((None, tr, cols), lambda i, p: (p[0], i, 0)))
    return pl.pallas_call(body, out_shape=jax.ShapeDtypeStruct((2, rows, cols), F32), grid_spec=grid_spec, name=name,
                          compiler_params=_params(("parallel",)))(place, grad, land)


def _half_start(name, totals):
    n = len(totals)

    def body(*refs):
        token = refs[-1]
        for cp in _half_copies(refs[:n], refs[n], refs[n + 1]):
            cp.start()
        token[...] = jnp.zeros_like(token)

    tok_shape, tok_spec = _token_spec()
    res = pl.pallas_call(
        body, name=name,
        out_shape=(pltpu.SemaphoreType.DMA((n,)), pltpu.SemaphoreType.DMA((n,)),
                   *[pltpu.HBM(t.shape, t.dtype) for t in totals], tok_shape),
        in_specs=[HBM] * n, out_specs=(SEM, SEM, *[HBM] * n, tok_spec),
        input_output_aliases={i: 2 + i for i in range(n)},
        compiler_params=pltpu.CompilerParams(has_side_effects=EFFECT))(*[_in_hbm(t) for t in totals])
    return res[0], res[1], list(res[2:2 + n]), res[-1]


def _half_wait(name, ssem, rsem, totals, after):
    n = len(totals)

    def body(*refs):
        for cp in _half_copies(refs[:n], refs[n], refs[n + 1]):
            cp.wait_send()
            cp.wait_recv()

    res = pl.pallas_call(
        body, name=name, out_shape=tuple(pltpu.HBM(t.shape, t.dtype) for t in totals),
        in_specs=[HBM] * n + [SEM, SEM] + [ANY] * len(after), out_specs=tuple([HBM] * n),
        input_output_aliases={i: i for i in range(n)},
        compiler_params=pltpu.CompilerParams(has_side_effects=EFFECT))(*totals, ssem, rsem, *after)
    return list(res)


def _small_exchange(small):
    def body(small_ref, all_ref, local_sem, send_sems, recv_sems):
        x, y, c = _place()
        me = 4 * x + 2 * y + c
        own = pltpu.make_async_copy(small_ref, all_ref.at[me], local_sem)
        own.start()
        copies = []
        for mask in range(1, N_DEV):
            cp = _remote(small_ref, all_ref.at[me], send_sems.at[mask - 1], recv_sems.at[mask - 1], _peer(x, y, c, mask))
            cp.start()
            copies.append(cp)
        for cp in copies:
            cp.wait()
        own.wait()

    return pl.pallas_call(
        body, out_shape=jax.ShapeDtypeStruct((N_DEV, *small.shape), small.dtype), in_specs=[ANY], out_specs=ANY,
        scratch_shapes=[pltpu.SemaphoreType.DMA, pltpu.SemaphoreType.DMA((7,)), pltpu.SemaphoreType.DMA((7,))],
        name="small_grad_exchange")(small)


def _sum_devices(stacked):
    _, rows, lanes = stacked.shape
    tr = _pick_rows(rows, 512)

    def body(s_ref, o_ref):
        tot = s_ref[0]
        for k in range(1, N_DEV):
            tot = tot + s_ref[k]
        o_ref[...] = tot

    return pl.pallas_call(
        body, out_shape=jax.ShapeDtypeStruct((rows, lanes), F32), grid=(rows // tr,),
        in_specs=[pl.BlockSpec((N_DEV, tr, lanes), lambda i: (0, i, 0))], out_specs=pl.BlockSpec((tr, lanes), lambda i: (i, 0)),
        name="small_grad_sum", compiler_params=_params(("parallel",)))(stacked)


def _adamw(name, w, g, m, v, layer, prev=None):
    layers, rows, cols = w.shape
    tr = _pick_rows(rows, 256)
    c1 = 1.0 - ADAM_B1 ** ADAM_STEP
    c2 = 1.0 - ADAM_B2 ** ADAM_STEP

    def body(w_ref, g_ref, m_ref, v_ref, *rest):
        go_ref, d_ref, nm_ref, nv_ref = rest[-4:]
        gv = g_ref[...]
        nm = ADAM_B1 * m_ref[...] + (1.0 - ADAM_B1) * gv
        nv = ADAM_B2 * v_ref[...] + (1.0 - ADAM_B2) * (gv * gv)
        go_ref[...] = gv
        d_ref[...] = -ADAM_LR * ((nm / c1) / (jnp.sqrt(nv / c2) + ADAM_EPS) + ADAM_WD * w_ref[...])
        nm_ref[...] = nm
        nv_ref[...] = nv

    spec = pl.BlockSpec((None, tr, cols), lambda i: (layer, i, 0))
    prev = list(prev) if prev is not None else []
    return pl.pallas_call(
        body, out_shape=[jax.ShapeDtypeStruct((layers, rows, cols), F32)] * 4, grid=(rows // tr,),
        in_specs=[spec, pl.BlockSpec((tr, cols), lambda i: (i, 0)), spec, spec] + [ANY] * len(prev),
        out_specs=[spec] * 4, input_output_aliases={4 + i: i for i in range(len(prev))}, name=name,
        compiler_params=_params(("parallel",)))(w, g, m, v, *prev)


def _pack(vectors, pad_rows):
    flat = jnp.concatenate([t.reshape(-1) for t in vectors])
    rows = -(-flat.shape[0] // LANES)
    rows = -(-rows // pad_rows) * pad_rows
    return jnp.pad(flat, (0, rows * LANES - flat.shape[0])).reshape(rows, LANES)


def _unpack(packed, shapes):
    flat = packed.reshape(-1)
    out, off = [], 0
    for shp in shapes:
        size = math.prod(shp)
        out.append(flat[off:off + size].reshape(shp))
        off += size
    return out


def kernel(x, mem, positions, mix_norm, mem_norm, w_mem_kv, ffn_norm, w_gate, w_up, w_down, attn_w_in, attn_w_out, sgu_w_in, sgu_ln_g, sgu_ln_b, sgu_w_spatial, sgu_b_spatial, sgu_w_out, final_norm, loss_target, m_mix_norm, m_mem_norm, m_w_mem_kv, m_ffn_norm, m_w_gate, m_w_up, m_w_down, m_attn_w_in, m_attn_w_out, m_sgu_w_in, m_sgu_ln_g, m_sgu_ln_b, m_sgu_w_spatial, m_sgu_b_spatial, m_sgu_w_out, m_final_norm, v_mix_norm, v_mem_norm, v_w_mem_kv, v_ffn_norm, v_w_gate, v_w_up, v_w_down, v_attn_w_in, v_attn_w_out, v_sgu_w_in, v_sgu_ln_g, v_sgu_ln_b, v_sgu_w_spatial, v_sgu_b_spatial, v_sgu_w_out, v_final_norm):
    d_model = x.shape[2]
    x0, mem0, tgt = x[0], mem[0], loss_target[0]
    xi, yi, ci = _place()
    chip = 2 * xi + yi
    place = jnp.stack([ci, chip]).astype(jnp.int32)

    given_w = dict(mix_norm=mix_norm, mem_norm=mem_norm, w_mem_kv=w_mem_kv, ffn_norm=ffn_norm, w_gate=w_gate, w_up=w_up,
                   w_down=w_down, attn_w_in=attn_w_in, attn_w_out=attn_w_out, sgu_w_in=sgu_w_in, sgu_ln_g=sgu_ln_g,
                   sgu_ln_b=sgu_ln_b, sgu_w_spatial=sgu_w_spatial, sgu_b_spatial=sgu_b_spatial, sgu_w_out=sgu_w_out,
                   final_norm=final_norm)
    given_m = dict(mix_norm=m_mix_norm, mem_norm=m_mem_norm, w_mem_kv=m_w_mem_kv, ffn_norm=m_ffn_norm, w_gate=m_w_gate,
                   w_up=m_w_up, w_down=m_w_down, attn_w_in=m_attn_w_in, attn_w_out=m_attn_w_out, sgu_w_in=m_sgu_w_in,
                   sgu_ln_g=m_sgu_ln_g, sgu_ln_b=m_sgu_ln_b, sgu_w_spatial=m_sgu_w_spatial,
                   sgu_b_spatial=m_sgu_b_spatial, sgu_w_out=m_sgu_w_out, final_norm=m_final_norm)
    given_v = dict(mix_norm=v_mix_norm, mem_norm=v_mem_norm, w_mem_kv=v_w_mem_kv, ffn_norm=v_ffn_norm, w_gate=v_w_gate,
                   w_up=v_w_up, w_down=v_w_down, attn_w_in=v_attn_w_in, attn_w_out=v_attn_w_out, sgu_w_in=v_sgu_w_in,
                   sgu_ln_g=v_sgu_ln_g, sgu_ln_b=v_sgu_ln_b, sgu_w_spatial=v_sgu_w_spatial,
                   sgu_b_spatial=v_sgu_b_spatial, sgu_w_out=v_sgu_w_out, final_norm=v_final_norm)

    units = {"attn_w_in": ("attn_w_in", 0, "col"), "w_mem_kv0": ("w_mem_kv", 0, "row"), "attn_w_out": ("attn_w_out", 0, "col"),
             "w_gate0": ("w_gate", 0, "col"), "w_up0": ("w_up", 0, "col"), "w_down0": ("w_down", 0, "row"),
             "sgu_w_in": ("sgu_w_in", 0, "col"), "w_mem_kv1": ("w_mem_kv", 1, "row"), "sgu_w_out": ("sgu_w_out", 0, "row"),
             "w_gate1": ("w_gate", 1, "col"), "w_up1": ("w_up", 1, "col"), "w_down1": ("w_down", 1, "row")}
    gather_groups = [["attn_w_in"], ["w_mem_kv0", "attn_w_out"], ["w_gate0", "w_up0"],
                     ["w_down0", "sgu_w_in", "w_mem_kv1", "ln"], ["sgu_w_out", "w_gate1", "w_up1"], ["w_down1"]]

    slabs = {u: _place_shard(f"place_{u}", given_w[arr], layer, place, BF) for u, (arr, layer, _) in units.items()}
    slabs["ln"] = _place_shard("place_ln", jnp.concatenate([sgu_ln_g, sgu_ln_b])[None], 0, place, F32)
    in_flight, token = _gather_start([[slabs[u] for u in grp] for grp in gather_groups])
    weights = {}

    def arrive(gi, after):
        ssem, rsem, arrs = in_flight[gi]
        for u, full in zip(gather_groups[gi], _gather_wait(f"gather_wait_{gi}", ssem, rsem, arrs, after)):
            weights[u] = full if u == "ln" else Weight(full, units[u][2])

    w_sp = sgu_w_spatial[0]
    b_t = sgu_b_spatial[0].T
    tables = _rope_tables(positions[0])

    def residual(acc, extra):
        return [extra[0] + acc[0]]

    def memory_kv(layer):
        mem_n = _rms_fwd(f"mem_norm_{layer}", mem0, mem_norm[layer:layer + 1])
        return mem_n, _mm_nn(f"mem_kv_{layer}", mem_n, weights[f"w_mem_kv{layer}"])[0]

    h0 = _rms_fwd("mix_norm_0", x0, mix_norm[0:1], after=[token])
    arrive(0, [h0])
    proj0 = _mm_nn("attn_in", h0, weights["attn_w_in"])[0]
    arrive(1, [proj0])
    qkv = _rope_fwd(proj0, tables)
    qs, ks, vs = qkv[0:3], qkv[3:6], qkv[6:9]
    outs, lses = [], []
    for g, dil in enumerate(DILATIONS):
        o, l = _dil_fwd(g, qs[g], ks[g], vs[g], dil)
        outs.append(o)
        lses.append(l)
    merged, lse = _attn_merge(outs, lses)
    mem_n0, kv0 = memory_kv(0)
    mem_out0 = _mem_fwd("mem_fwd_0", proj0, 9, kv0)
    cat0 = jnp.concatenate([merged, mem_out0], axis=1)
    x1 = _mm_nn("attn_out", cat0, weights["attn_w_out"], extras=[x0], epilogue=residual)[0]
    arrive(2, [x1])
    hf0 = _rms_fwd("ffn_norm_0", x1, ffn_norm[0:1])
    g0, u0, act0 = _gate_up("gate_up_0", hf0, weights["w_gate0"], weights["w_up0"])
    arrive(3, [act0])
    x2 = _mm_nn("down_0", act0, weights["w_down0"], extras=[x1], epilogue=residual)[0]

    ln_all = weights["ln"]
    ln_g = ln_all[:, 0, :].reshape(1, SGU_W)
    ln_b = ln_all[:, 1, :].reshape(1, SGU_W)
    h1 = _rms_fwd("mix_norm_1", x2, mix_norm[1:2])
    proj1 = _mm_nn("sgu_in", h1, weights["sgu_w_in"])[0]
    arrive(4, [proj1])
    sgu_out = _sgu_fwd(proj1, ln_g, ln_b, w_sp, b_t)
    mem_n1, kv1 = memory_kv(1)
    mem_out1 = _mem_fwd("mem_fwd_1", proj1, 6, kv1)
    cat1 = jnp.concatenate([sgu_out, mem_out1], axis=1)
    x3 = _mm_nn("sgu_out", cat1, weights["sgu_w_out"], extras=[x2], epilogue=residual)[0]
    hf1 = _rms_fwd("ffn_norm_1", x3, ffn_norm[1:2])
    g1, u1, act1 = _gate_up("gate_up_1", hf1, weights["w_gate1"], weights["w_up1"])
    arrive(5, [act1])
    x4 = _mm_nn("down_1", act1, weights["w_down1"], extras=[x3], epilogue=residual)[0]

    d4, g_final, loss_part = _final_loss(x4, tgt, final_norm.reshape(1, d_model))
    loss = lax.psum(loss_part[0, 0], ("x", "y", "c"))

    outputs = {}

    def start_reduce(tag, names, grads):
        ssem, rsem, grads, lands, tok = _reduce_start(f"reduce_start_{tag}", grads)
        return dict(tag=tag, names=names, ssem=ssem, rsem=rsem, grads=grads, lands=lands), tok

    def finish_reduce(st, after):
        grads, lands = _reduce_wait(f"reduce_wait_{st['tag']}", st["ssem"], st["rsem"], st["grads"], st["lands"], after)
        totals = [_sum_pieces(f"sum_{u}", g, l, place) for u, g, l in zip(st["names"], grads, lands)]
        ssem, rsem, totals, tok = _half_start(f"half_start_{st['tag']}", totals)
        return dict(tag=st["tag"], names=st["names"], ssem=ssem, rsem=rsem, totals=totals), tok

    def finish_update(st, after):
        totals = _half_wait(f"half_wait_{st['tag']}", st["ssem"], st["rsem"], st["totals"], after)
        for u, tot in zip(st["names"], totals):
            arr, layer, _ = units[u]
            w = given_w[arr]
            outputs[arr] = _adamw(f"adamw_{u}", w, tot.reshape(w.shape[1:]), given_m[arr], given_v[arr], layer,
                                  outputs.get(arr))

    def ffn_bwd(layer, d_out, xin, h, g, u, act):
        wd, wg, wu = weights[f"w_down{layer}"], weights[f"w_gate{layer}"], weights[f"w_up{layer}"]
        gr_down = _mm_tn(f"d_down_{layer}", act, d_out, wd)
        dg, du = _mm_nt(f"d_act_{layer}", [d_out], [wd], out_dtypes=(BF, BF), extras=[g, u], epilogue=_swiglu_bwd_epilogue)
        gr_gate = _mm_tn(f"d_gate_{layer}", h, dg, wg)
        gr_up = _mm_tn(f"d_up_{layer}", h, du, wu)
        st, tok = start_reduce(f"ffn{layer}", [f"w_down{layer}", f"w_gate{layer}", f"w_up{layer}"], [gr_down, gr_gate, gr_up])
        dh = _mm_nt(f"d_ffn_h_{layer}", [dg, du], [wg, wu], after=[tok])[0]
        d_in, g_norm = _rms_bwd(f"ffn_norm_bwd_{layer}", xin, ffn_norm[layer:layer + 1], dh, d_out)
        return st, d_in, g_norm, dg

    def memory_bwd(layer, mem_n, dkv):
        dkv = dkv.astype(BF)
        wkv = weights[f"w_mem_kv{layer}"]
        gr = _mm_tn(f"d_mem_kv_{layer}", mem_n, dkv, wkv)
        d_mem_n = _mm_nt(f"d_mem_n_{layer}", [dkv], [wkv])[0]
        return gr, _rms_bwd(f"mem_norm_bwd_{layer}", mem0, mem_norm[layer:layer + 1], d_mem_n)[1]

    st_ffn1, d3, g_ffn1, _ = ffn_bwd(1, d4, x3, hf1, g1, u1, act1)
    gr_sgu_out = _mm_tn("d_sgu_out", cat1, d3, weights["sgu_w_out"])
    dcat1 = _mm_nt("d_cat_1", [d3], [weights["sgu_w_out"]])[0]
    st_ffn1, tok = finish_reduce(st_ffn1, [dcat1])
    dq_mem1, dkv1 = _mem_bwd("mem_bwd_1", proj1, 6, kv1, dcat1, 3)
    gr_kv1, g_mem1 = memory_bwd(1, mem_n1, dkv1)
    du_pre, dv_pre, g_wsp, g_bsp_t, g_ln_g, g_ln_b = _sgu_bwd(proj1, dcat1, ln_g, ln_b, w_sp, b_t)
    dproj1 = jnp.concatenate([du_pre, dv_pre, dq_mem1.astype(BF)], axis=1)
    gr_sgu_in = _mm_tn("d_sgu_in", h1, dproj1, weights["sgu_w_in"], after=[tok])
    finish_update(st_ffn1, [gr_sgu_in])
    st_mix1, tok = start_reduce("mix1", ["sgu_w_out", "w_mem_kv1", "sgu_w_in"], [gr_sgu_out, gr_kv1, gr_sgu_in])
    dh1 = _mm_nt("d_h_1", [dproj1], [weights["sgu_w_in"]], after=[tok])[0]
    d2, g_mix1 = _rms_bwd("mix_norm_bwd_1", x2, mix_norm[1:2], dh1, d3)

    st_ffn0, d1, g_ffn0, dg0 = ffn_bwd(0, d2, x1, hf0, g0, u0, act0)
    gr_attn_out = _mm_tn("d_attn_out", cat0, d1, weights["attn_w_out"])
    st_mix1, tok = finish_reduce(st_mix1, [gr_attn_out])
    dcat0 = _mm_nt("d_cat_0", [d1], [weights["attn_w_out"]], after=[tok])[0]
    dq_mem0, dkv0 = _mem_bwd("mem_bwd_0", proj0, 9, kv0, dcat0, 1)
    finish_update(st_mix1, [dq_mem0])
    gr_kv0, g_mem0 = memory_bwd(0, mem_n0, dkv0)
    st_ffn0, tok = finish_reduce(st_ffn0, [g_mem0])
    d_merged, delta = _attn_delta(dcat0, merged, after=[tok])
    dqs, dks, dvs = [], [], []
    for g, dil in enumerate(DILATIONS):
        dq, dk, dv = _dil_bwd(g, qs[g], ks[g], vs[g], d_merged, lse, delta, dil)
        dqs.append(dq)
        dks.append(dk)
        dvs.append(dv)
    dproj0 = _rope_bwd(dqs + dks + dvs, dq_mem0, tables)
    finish_update(st_ffn0, [dproj0])
    gr_attn_in = _mm_tn("d_attn_in", h0, dproj0, weights["attn_w_in"])
    st_mix0, tok = start_reduce("mix0", ["attn_w_out", "w_mem_kv0", "attn_w_in"], [gr_attn_out, gr_kv0, gr_attn_in])
    dh0 = _mm_nt("d_h_0", [dproj0], [weights["attn_w_in"]], after=[tok])[0]
    d0, g_mix0 = _rms_bwd("mix_norm_bwd_0", x0, mix_norm[0:1], dh0, d1)

    small_grads = [jnp.concatenate([g_mix0, g_mix1]), jnp.concatenate([g_mem0, g_mem1]),
                   jnp.concatenate([g_ffn0, g_ffn1]), g_wsp, g_bsp_t[:, :SGU_GROUPS].T, g_final, g_ln_g, g_ln_b]
    small_all = _small_exchange(_pack(small_grads, 8))
    st_mix0, tok = finish_reduce(st_mix0, [small_all])
    g_mix, g_mem, g_ffn, g_wsp, g_bsp, g_final, g_ln_g, g_ln_b = _unpack(_sum_devices(small_all),
                                                                         [t.shape for t in small_grads])
    shard_w = sgu_ln_g.shape[-1]
    g_ln_g = lax.dynamic_slice_in_dim(g_ln_g, chip * shard_w, shard_w, axis=1)
    g_ln_b = lax.dynamic_slice_in_dim(g_ln_b, chip * shard_w, shard_w, axis=1)
    small_names = ["mix_norm", "mem_norm", "ffn_norm", "sgu_w_spatial", "sgu_b_spatial", "final_norm", "sgu_ln_g",
                   "sgu_ln_b"]
    small_g = [g_mix, g_mem, g_ffn, g_wsp, g_bsp, g_final, g_ln_g, g_ln_b]
    small_shapes = [given_w[k].shape for k in small_names]
    packed = [_pack(t, 8) for t in ([given_w[k] for k in small_names], small_g, [given_m[k] for k in small_names],
                                    [given_v[k] for k in small_names])]
    small_out = _adamw("adamw_small", packed[0][None], packed[1], packed[2][None], packed[3][None], 0)
    finish_update(st_mix0, [small_out[0]])
    for k, gk, dk, mk, vk in zip(small_names, *[_unpack(t[0], small_shapes) for t in small_out]):
        outputs[k] = (gk, dk, mk, vk)

    order = ["mix_norm", "mem_norm", "w_mem_kv", "ffn_norm", "w_gate", "w_up", "w_down", "attn_w_in", "attn_w_out",
             "sgu_w_in", "sgu_ln_g", "sgu_ln_b", "sgu_w_spatial", "sgu_b_spatial", "sgu_w_out", "final_norm"]
    return (loss, d0[None], *[outputs[k][0] for k in order], *[outputs[k][1] for k in order],
            *[outputs[k][2] for k in order], *[outputs[k][3] for k in order])
```

```python
import math

import jax
import jax.numpy as jnp
from jax import lax
from jax.experimental import pallas as pl
from jax.experimental.pallas import tpu as pltpu

F32 = jnp.float32
BF = jnp.bfloat16
MESH = pl.DeviceIdType.MESH

HEAD_DIM = 128
MEM_HEADS = 4
MEM_W = MEM_HEADS * HEAD_DIM
GROUP_W = 4 * HEAD_DIM
DILATIONS = (1, 4, 16)
BLK = 128
SGU_GROUPS = 12
SGU_W = SGU_GROUPS * HEAD_DIM
ROT_HALF = 16
ROPE_THETA = 500000.0
NORM_EPS = 1e-6
LN_EPS = 1e-5
NEG_INF = -1e30
SCALE = HEAD_DIM ** -0.5
ADAM_LR, ADAM_B1, ADAM_B2, ADAM_EPS, ADAM_WD, ADAM_STEP = 0.001, 0.9, 0.999, 1e-08, 0.01, 10

VMEM_LIMIT = 48 * 2 ** 20
VMEM_TILE_BUDGET = 36 * 2 ** 20
N_CHIPS = 4
N_DEV = 8
LANES = 128
EPILOGUE_CHUNK = 256

NT_DIMS = (((1,), (1,)), ((), ()))
TN_DIMS = (((0,), (0,)), ((), ()))
NN_DIMS = (((1,), (0,)), ((), ()))

ANY = pl.BlockSpec(memory_space=pl.ANY)
HBM = pl.BlockSpec(memory_space=pltpu.HBM)
SEM = pl.BlockSpec(memory_space=pltpu.SEMAPHORE)
EFFECT = pltpu.SideEffectType.DATAFLOW_SIDE_EFFECTING


def _params(sem):
    return pltpu.CompilerParams(dimension_semantics=sem, vmem_limit_bytes=VMEM_LIMIT)


def _pick(n, cap):
    if n <= cap:
        return n
    best = None
    for t in range(LANES, cap + 1, LANES):
        if n % t == 0:
            best = t
    assert best is not None, (n, cap)
    return best


def _pick_rows(n, cap):
    t = min(n, cap)
    while n % t:
        t //= 2
    return t


def _mm(name, dims, a_list, a_specs, b_list, b_specs, pairs, n_acc, acc_shape, grid, extras, e_specs,
        out_shapes, out_specs, epilogue, after=(), col_chunk=None, store=None):
    na, nb, ne, no = len(a_list), len(b_list), len(extras), len(out_shapes)
    nk = grid[-1]

    def products(a, b, cols=None):
        sums = [None] * n_acc
        for ai, bi, ci in pairs:
            bv = b[bi]
            if cols is None:
                bv = bv[...]
            elif dims == NT_DIMS:
                bv = bv[cols, :]
            else:
                bv = bv[:, cols]
            if bv.ndim == 3:
                bv = bv.reshape(-1, bv.shape[-1])
            prod = lax.dot_general(a[ai][...].astype(BF), bv.astype(BF), dims, preferred_element_type=F32)
            sums[ci] = prod if sums[ci] is None else sums[ci] + prod
        return sums

    def body(*refs):
        a = refs[:na]
        b = refs[na:na + nb]
        e = refs[na + nb:na + nb + ne]
        off = na + nb + ne + len(after)
        o = refs[off:off + no]
        acc = refs[off + no:]

        def finish(sums):
            outs = epilogue(sums, [r[...] for r in e])
            if store is not None:
                store(o, outs)
                return
            for r, v in zip(o, outs):
                r[...] = v.astype(r.dtype)

        if nk == 1 and col_chunk:
            width = acc_shape[1]
            for c0 in range(0, width, col_chunk):
                cols = slice(c0, min(c0 + col_chunk, width))
                outs = epilogue(products(a, b, cols), [r[:, cols] for r in e])
                for r, v in zip(o, outs):
                    r[:, cols] = v.astype(r.dtype)
            return
        if nk == 1:
            finish(products(a, b))
            return
        k = pl.program_id(len(grid) - 1)

        @pl.when(k == 0)
        def _():
            for c, v in zip(acc, products(a, b)):
                c[...] = v

        @pl.when(jnp.logical_and(k > 0, k < nk - 1))
        def _():
            for c, v in zip(acc, products(a, b)):
                c[...] += v

        @pl.when(k == nk - 1)
        def _():
            finish([c[...] + v for c, v in zip(acc, products(a, b))])

    ins = [*a_list, *b_list, *extras, *after]
    in_specs = [*a_specs, *b_specs, *e_specs, *([ANY] * len(after))]
    sem = ("parallel",) * (len(grid) - 1) + ("arbitrary",)
    scratch = [] if nk == 1 else [pltpu.VMEM(acc_shape, F32)] * n_acc
    return pl.pallas_call(
        body, out_shape=out_shapes, grid=grid, in_specs=in_specs, out_specs=out_specs, scratch_shapes=scratch,
        name=name, compiler_params=_params(sem))(*ins)


def _tile_bytes(blocks):
    return sum(2 * math.prod(s) * jnp.dtype(d).itemsize for s, d in blocks)


def _first(acc, extra):
    return [acc[0]]


class Weight:
    def __init__(self, arr, axis):
        self.arr, self.axis = arr, axis
        _, self.rows, self.cols = arr.shape


def _mm_nn(name, a, w, extras=(), epilogue=_first, out_dtypes=(F32,), after=()):
    m, kdim = a.shape
    b_spec = None
    if w.axis == "col":
        n_total = N_CHIPS * w.cols
        tn = _pick(w.cols, 1408)
        tk = _pick(kdim, 2048)
        ncb = w.cols // tn
        gn, gk = N_CHIPS * ncb, kdim // tk
        b_map = lambda n, i, k: (n // ncb, k, n % ncb)
    elif kdim <= 2048:
        n_total = w.cols
        tn = _pick(w.cols, 1024)
        tk = kdim
        gn, gk = n_total // tn, 1
        b_spec = pl.BlockSpec((N_CHIPS, w.rows, tn), lambda n, i, k: (0, 0, n))
    else:
        n_total = w.cols
        tn = _pick(w.cols, 1024)
        tk = _pick(w.rows, 1408)
        nkb = w.rows // tk
        gn, gk = n_total // tn, N_CHIPS * nkb
        b_map = lambda n, i, k: (k // nkb, k % nkb, n)
    if b_spec is None:
        b_spec = pl.BlockSpec((None, tk, tn), b_map)
    for tm in (1024, 512, 256, 128):
        if m % tm:
            continue
        blocks = [((tm, tk), a.dtype), ((tk, tn), BF)] + [((tm, tn), e.dtype) for e in extras]
        blocks += [((tm, tn), d) for d in out_dtypes] + [((tm, tn), BF)]
        if _tile_bytes(blocks) <= VMEM_TILE_BUDGET:
            break
    o_spec = pl.BlockSpec((tm, tn), lambda n, i, k: (i, n))
    return _mm(
        name, NN_DIMS, [a], [pl.BlockSpec((tm, tk), lambda n, i, k: (i, k))],
        [w.arr], [b_spec], [(0, 0, 0)], 1, (tm, tn), (gn, m // tm, gk),
        list(extras), [o_spec] * len(extras),
        [jax.ShapeDtypeStruct((m, n_total), d) for d in out_dtypes], [o_spec] * len(out_dtypes), epilogue, after)


def _gate_up(name, h, wg, wu):
    m, kdim = h.shape
    tn = _pick(wg.cols, 1408)
    tk = _pick(kdim, 2048)
    ncb = wg.cols // tn
    for tm in (512, 256, 128):
        blocks = [((tm, tk), BF), ((tk, tn), BF), ((tk, tn), BF), ((tm, tn), F32), ((tm, tn), F32), ((tm, tn), BF)]
        if m % tm == 0 and _tile_bytes(blocks) <= VMEM_TILE_BUDGET:
            break
    b_spec = pl.BlockSpec((None, tk, tn), lambda n, i, k: (n // ncb, k, n % ncb))
    o_spec = pl.BlockSpec((tm, tn), lambda n, i, k: (i, n))
    n_total = N_CHIPS * wg.cols

    def epilogue(acc, extra):
        g, u = acc
        return [g, u, g * (1.0 / (1.0 + jnp.exp(-g))) * u]

    return _mm(
        name, NN_DIMS, [h], [pl.BlockSpec((tm, tk), lambda n, i, k: (i, k))], [wg.arr, wu.arr], [b_spec, b_spec],
        [(0, 0, 0), (0, 1, 1)], 2, (tm, tn), (N_CHIPS * ncb, m // tm, kdim // tk), [], [],
        [jax.ShapeDtypeStruct((m, n_total), F32)] * 2 + [jax.ShapeDtypeStruct((m, n_total), BF)], [o_spec] * 3,
        epilogue, col_chunk=EPILOGUE_CHUNK)


def _mm_nt(name, dys, ws, out_dtypes=(F32,), extras=(), epilogue=_first, after=(), col_chunk=None):
    m = dys[0].shape[0]
    w0 = ws[0]
    npair = len(dys)
    if w0.axis == "col":
        k_total = w0.rows
        tko = _pick(k_total, 1024)
        tkc = _pick(w0.cols, 1408)
        nkb = w0.cols // tkc
        go, gk = k_total // tko, N_CHIPS * nkb
        b_map = lambda o, i, k: (k // nkb, o, k % nkb)
    else:
        k_total = N_CHIPS * w0.rows
        tko = _pick(w0.rows, 1408)
        tkc = _pick(w0.cols, 2048)
        nob = w0.rows // tko
        go, gk = N_CHIPS * nob, w0.cols // tkc
        b_map = lambda o, i, k: (o // nob, o % nob, k)
    for tm in (1024, 512, 256, 128):
        if m % tm:
            continue
        blocks = [((tm, tkc), d.dtype) for d in dys] + [((tko, tkc), BF)] * npair
        blocks += [((tm, tko), e.dtype) for e in extras] + [((tm, tko), d) for d in out_dtypes]
        blocks += [((tm, tko), BF)]
        if _tile_bytes(blocks) <= VMEM_TILE_BUDGET:
            break
    o_spec = pl.BlockSpec((tm, tko), lambda o, i, k: (i, o))
    return _mm(
        name, NT_DIMS, list(dys), [pl.BlockSpec((tm, tkc), lambda o, i, k: (i, k))] * npair,
        [w.arr for w in ws], [pl.BlockSpec((None, tko, tkc), b_map)] * npair,
        [(i, i, 0) for i in range(npair)], 1, (tm, tko), (go, m // tm, gk), list(extras), [o_spec] * len(extras),
        [jax.ShapeDtypeStruct((m, k_total), d) for d in out_dtypes], [o_spec] * len(out_dtypes), epilogue, after,
        col_chunk if gk == 1 else None)


def _mm_tn(name, a, dy, w, after=()):
    m, k_total = a.shape
    rows2 = w.rows // 2
    tn = _pick(w.cols, 1408)
    ncb = w.cols // tn
    epilogue, store = _first, None
    if k_total <= 2048 and w.axis == "col":
        tkr = k_total
        gr, gn = 1, N_CHIPS * ncb
        o_spec = pl.BlockSpec((2, None, rows2, tn), lambda r, n, t: (0, n // ncb, 0, n % ncb))
        epilogue = lambda acc, extra: [acc[0].reshape(2, rows2, tn)]
    elif k_total <= 2048:
        tkr = k_total
        gr, gn = 1, ncb
        o_spec = pl.BlockSpec((2, N_CHIPS, rows2, tn), lambda r, n, t: (0, 0, 0, n))

        def store(o_refs, outs):
            for j in range(N_CHIPS):
                for h in range(2):
                    lo = (2 * j + h) * rows2
                    o_refs[0][h, j] = outs[0][lo:lo + rows2].astype(BF)
    elif rows2 % LANES:
        tkr = w.rows
        assert w.axis == "row"
        gr, gn = N_CHIPS, ncb
        o_spec = pl.BlockSpec((2, None, rows2, tn), lambda r, n, t: (0, r, 0, n))
        epilogue = lambda acc, extra: [acc[0].reshape(2, rows2, tn)]
    else:
        tkr = _pick(rows2, 1408)
        nrb = rows2 // tkr
        if w.axis == "col":
            gr, gn = w.rows // tkr, N_CHIPS * ncb
            o_map = lambda r, n, t: (r // nrb, n // ncb, r % nrb, n % ncb)
        else:
            per = w.rows // tkr
            gr, gn = N_CHIPS * per, ncb
            o_map = lambda r, n, t: ((r % per) // nrb, r // per, (r % per) % nrb, n)
        o_spec = pl.BlockSpec((None, None, tkr, tn), o_map)
    for tmk in (1024, 512, 256, 128):
        blocks = [((tmk, tkr), a.dtype), ((tmk, tn), dy.dtype), ((tkr, tn), BF), ((tkr, tn), BF)]
        if m % tmk == 0 and _tile_bytes(blocks) <= VMEM_TILE_BUDGET:
            break
    return _mm(
        name, TN_DIMS, [a], [pl.BlockSpec((tmk, tkr), lambda r, n, t: (t, r))],
        [dy], [pl.BlockSpec((tmk, tn), lambda r, n, t: (t, n))], [(0, 0, 0)], 1, (tkr, tn), (gr, gn, m // tmk), [], [],
        [jax.ShapeDtypeStruct((2, N_CHIPS, rows2, w.cols), BF)], [o_spec], epilogue, after, store=store)[0]


def _rms_fwd(name, x, g, after=()):
    s, d = x.shape
    tr = _pick_rows(s, 512)

    def body(x_ref, g_ref, *rest):
        h_ref = rest[-1]
        xf = x_ref[...]
        r = lax.rsqrt(jnp.mean(xf * xf, axis=-1, keepdims=True) + NORM_EPS)
        h_ref[...] = (xf * r * g_ref[...]).astype(BF)

    return pl.pallas_call(
        body, out_shape=jax.ShapeDtypeStruct((s, d), BF), grid=(s // tr,),
        in_specs=[pl.BlockSpec((tr, d), lambda i: (i, 0)), pl.BlockSpec((1, d), lambda i: (0, 0))] + [ANY] * len(after),
        out_specs=pl.BlockSpec((tr, d), lambda i: (i, 0)), name=name, compiler_params=_params(("parallel",)))(x, g, *after)


def _rms_bwd(name, x, g, dh, dres=None):
    s, d = x.shape
    tr = _pick_rows(s, 256)
    has_res = dres is not None

    def body(*refs):
        if has_res:
            x_ref, g_ref, dh_ref, dres_ref, dx_ref, dg_ref = refs
        else:
            x_ref, g_ref, dh_ref, dx_ref, dg_ref = refs
        xf = x_ref[...]
        r = lax.rsqrt(jnp.mean(xf * xf, axis=-1, keepdims=True) + NORM_EPS)
        xr = xf * r
        dy = dh_ref[...]
        a = dy * g_ref[...]
        dx = r * (a - xr * jnp.mean(a * xr, axis=-1, keepdims=True))
        if has_res:
            dx = dx + dres_ref[...]
        dx_ref[...] = dx

        @pl.when(pl.program_id(0) == 0)
        def _():
            dg_ref[...] = jnp.zeros_like(dg_ref)

        dg_ref[...] += jnp.sum(dy * xr, axis=0, keepdims=True)

    row = pl.BlockSpec((tr, d), lambda i: (i, 0))
    vec = pl.BlockSpec((1, d), lambda i: (0, 0))
    ins = [x, g, dh] + ([dres] if has_res else [])
    in_specs = [row, vec, row] + ([row] if has_res else [])
    return pl.pallas_call(
        body, out_shape=[jax.ShapeDtypeStruct((s, d), F32), jax.ShapeDtypeStruct((1, d), F32)], grid=(s // tr,),
        in_specs=in_specs, out_specs=[row, vec], name=name, compiler_params=_params(("arbitrary",)))(*ins)


def _final_loss(x, tgt, g):
    s, d = x.shape
    tr = _pick_rows(s, 256)

    def body(x_ref, t_ref, g_ref, dx_ref, dg_ref, loss_ref):
        xf = x_ref[...]
        gain = g_ref[...]
        r = lax.rsqrt(jnp.mean(xf * xf, axis=-1, keepdims=True) + NORM_EPS)
        xr = xf * r
        err = xr * gain - t_ref[...]
        dy = err * (1.0 / d)
        a = dy * gain
        dx_ref[...] = r * (a - xr * jnp.mean(a * xr, axis=-1, keepdims=True))

        @pl.when(pl.program_id(0) == 0)
        def _():
            dg_ref[...] = jnp.zeros_like(dg_ref)
            loss_ref[...] = jnp.zeros_like(loss_ref)

        dg_ref[...] += jnp.sum(dy * xr, axis=0, keepdims=True)
        part = 0.5 * jnp.sum(jnp.mean(err * err, axis=-1, keepdims=True), axis=0, keepdims=True)
        loss_ref[...] += jnp.broadcast_to(part, loss_ref.shape)

    row = pl.BlockSpec((tr, d), lambda i: (i, 0))
    vec = pl.BlockSpec((1, d), lambda i: (0, 0))
    return pl.pallas_call(
        body, out_shape=[jax.ShapeDtypeStruct((s, d), F32), jax.ShapeDtypeStruct((1, d), F32),
                         jax.ShapeDtypeStruct((8, LANES), F32)],
        grid=(s // tr,), in_specs=[row, row, vec], out_specs=[row, vec, pl.BlockSpec((8, LANES), lambda i: (0, 0))],
        name="final_loss", compiler_params=_params(("arbitrary",)))(x, tgt, g)


def _swiglu_bwd_epilogue(acc, extra):
    dact = acc[0]
    g, u = extra
    sig = 1.0 / (1.0 + jnp.exp(-g))
    return [dact * u * sig * (1.0 + g * (1.0 - sig)), dact * g * sig]


GELU_C = math.sqrt(2.0 / math.pi)
GELU_A = 0.044715


def _gelu(x):
    return 0.5 * x * (1.0 + jnp.tanh(GELU_C * (x + GELU_A * x * x * x)))


def _gelu_grad(x):
    t = jnp.tanh(GELU_C * (x + GELU_A * x * x * x))
    return 0.5 * (1.0 + t) + 0.5 * x * (1.0 - t * t) * GELU_C * (1.0 + 3.0 * GELU_A * x * x)


def _rope_tables(positions):
    inv_freq = ROPE_THETA ** (-jnp.arange(ROT_HALF, dtype=F32) / ROT_HALF)
    ang = positions.astype(F32)[:, None] * inv_freq
    cos, sin = jnp.cos(ang), jnp.sin(ang)
    s = ang.shape[0]
    rest = HEAD_DIM - 2 * ROT_HALF
    zeros = jnp.zeros((s, ROT_HALF), F32)
    cos_t = jnp.concatenate([cos, cos, jnp.ones((s, rest), F32)], axis=1)
    sin_a = jnp.concatenate([-sin, zeros, jnp.zeros((s, rest), F32)], axis=1)
    sin_b = jnp.concatenate([zeros, sin, jnp.zeros((s, rest), F32)], axis=1)
    return cos_t, sin_a, sin_b


def _rope_heads(x, cos_t, sin_a, sin_b):
    outs = []
    for h in range(GROUP_W // HEAD_DIM):
        xh = x[:, h * HEAD_DIM:(h + 1) * HEAD_DIM]
        up = pltpu.roll(xh, HEAD_DIM - ROT_HALF, 1)
        down = pltpu.roll(xh, ROT_HALF, 1)
        outs.append(xh * cos_t + up * sin_a + down * sin_b)
    return outs


def _rope_fwd(proj, tables):
    s = proj.shape[0]
    tm = _pick_rows(s, 512)
    nparts = 9

    def body(p_ref, c_ref, sa_ref, sb_ref, *outs):
        part = pl.program_id(1)
        for p in range(nparts):
            @pl.when(part == p)
            def _(p=p):
                xv = p_ref[...]
                if p < 6:
                    heads = _rope_heads(xv, c_ref[...], sa_ref[...], sb_ref[...])
                    for h, v in enumerate(heads):
                        outs[p][:, h * HEAD_DIM:(h + 1) * HEAD_DIM] = v.astype(BF)
                else:
                    outs[p][...] = xv.astype(BF)

    tab = pl.BlockSpec((tm, HEAD_DIM), lambda i, p: (i, 0))
    o_spec = pl.BlockSpec((tm, GROUP_W), lambda i, p: (i, 0))
    return pl.pallas_call(
        body, out_shape=[jax.ShapeDtypeStruct((s, GROUP_W), BF)] * nparts, grid=(s // tm, nparts),
        in_specs=[pl.BlockSpec((tm, GROUP_W), lambda i, p: (i, p)), tab, tab, tab], out_specs=[o_spec] * nparts,
        name="rope_fwd", compiler_params=_params(("parallel", "arbitrary")))(proj, *tables)


def _rope_bwd(parts, dq_mem, tables):
    s = dq_mem.shape[0]
    tm = _pick_rows(s, 256)
    nparts = len(parts) + 1

    def body(*refs):
        ins = refs[:nparts]
        c_ref, sa_ref, sb_ref, o_ref = refs[nparts:]
        part = pl.program_id(1)
        for p in range(nparts):
            @pl.when(part == p)
            def _(p=p):
                xv = ins[p][...]
                if p < 6:
                    heads = _rope_heads(xv, c_ref[...], -sa_ref[...], -sb_ref[...])
                    for h, v in enumerate(heads):
                        o_ref[:, h * HEAD_DIM:(h + 1) * HEAD_DIM] = v.astype(BF)
                else:
                    o_ref[...] = xv.astype(BF)

    i_spec = pl.BlockSpec((tm, GROUP_W), lambda i, p: (i, 0))
    tab = pl.BlockSpec((tm, HEAD_DIM), lambda i, p: (i, 0))
    return pl.pallas_call(
        body, out_shape=jax.ShapeDtypeStruct((s, nparts * GROUP_W), BF), grid=(s // tm, nparts),
        in_specs=[i_spec] * nparts + [tab] * 3, out_specs=pl.BlockSpec((tm, GROUP_W), lambda i, p: (i, p)),
        name="rope_bwd", compiler_params=_params(("parallel", "arbitrary")))(*parts, dq_mem, *tables)


def _band_masks(n):
    qi = lax.broadcasted_iota(jnp.int32, (BLK, BLK), 0)
    ki = lax.broadcasted_iota(jnp.int32, (BLK, BLK), 1)
    return qi >= ki, jnp.logical_and(ki >= qi, n > 0)


def _dil_fwd(g, q, k, v, dil):
    s = q.shape[0]
    length = s // dil
    nb = length // BLK
    view = lambda t: t.reshape(length, dil * GROUP_W)

    def body(q_ref, kc_ref, kp_ref, vc_ref, vp_ref, o_ref, lse_ref):
        n = pl.program_id(1)
        mask_c, mask_p = _band_masks(n)
        for h in range(GROUP_W // HEAD_DIM):
            sl = slice(h * HEAD_DIM, (h + 1) * HEAD_DIM)
            qh = q_ref[:, sl]
            sc = lax.dot_general(qh, kc_ref[:, sl], NT_DIMS, preferred_element_type=F32) * SCALE
            sp = lax.dot_general(qh, kp_ref[:, sl], NT_DIMS, preferred_element_type=F32) * SCALE
            sc = jnp.where(mask_c, sc, NEG_INF)
            sp = jnp.where(mask_p, sp, NEG_INF)
            mx = jnp.maximum(jnp.max(sc, axis=-1, keepdims=True), jnp.max(sp, axis=-1, keepdims=True))
            pc = jnp.exp(sc - mx)
            pp = jnp.exp(sp - mx)
            den = jnp.sum(pc, axis=-1, keepdims=True) + jnp.sum(pp, axis=-1, keepdims=True)
            acc = jnp.dot(pc.astype(BF), vc_ref[:, sl], preferred_element_type=F32)
            acc += jnp.dot(pp.astype(BF), vp_ref[:, sl], preferred_element_type=F32)
            o_ref[:, sl] = acc / den
            lse_ref[:, sl] = jnp.broadcast_to(mx + jnp.log(den), (BLK, HEAD_DIM))

    cur = pl.BlockSpec((BLK, GROUP_W), lambda r, n: (n, r))
    prev = pl.BlockSpec((BLK, GROUP_W), lambda r, n: (jnp.maximum(n - 1, 0), r))
    o, lse = pl.pallas_call(
        body, out_shape=[jax.ShapeDtypeStruct((length, dil * GROUP_W), F32)] * 2, grid=(dil, nb),
        in_specs=[cur, cur, prev, cur, prev], out_specs=[cur, cur], name=f"dil_fwd_{g}",
        compiler_params=_params(("parallel", "arbitrary")))(view(q), view(k), view(k), view(v), view(v))
    return o.reshape(s, GROUP_W), lse.reshape(s, GROUP_W)


def _dil_bwd(g, q, k, v, do, lse, delta, dil):
    s = q.shape[0]
    length = s // dil
    nb = length // BLK
    view = lambda t: t.reshape(length, dil * GROUP_W)

    def body(q_ref, kc_ref, kp_ref, vc_ref, vp_ref, do_ref, lse_ref, dl_ref, dq_ref, dk_ref, dv_ref, ck_ref, cv_ref):
        n = pl.program_id(1)
        live = n < nb
        mask_c, mask_p = _band_masks(n)
        mask_c = jnp.logical_and(mask_c, live)
        mask_p = jnp.logical_and(mask_p, live)

        @pl.when(n == 0)
        def _():
            ck_ref[...] = jnp.zeros_like(ck_ref)
            cv_ref[...] = jnp.zeros_like(cv_ref)

        for h in range(GROUP_W // HEAD_DIM):
            sl = slice(h * HEAD_DIM, (h + 1) * HEAD_DIM)
            qh, kc, kp, vc, vp, doh = q_ref[:, sl], kc_ref[:, sl], kp_ref[:, sl], vc_ref[:, sl], vp_ref[:, sl], do_ref[:, sl]
            lse_h = lse_ref[:, sl]
            dl_h = dl_ref[:, sl]
            sc = lax.dot_general(qh, kc, NT_DIMS, preferred_element_type=F32) * SCALE
            sp = lax.dot_general(qh, kp, NT_DIMS, preferred_element_type=F32) * SCALE
            pc = jnp.where(mask_c, jnp.exp(jnp.minimum(sc - lse_h, 0.0)), 0.0)
            pp = jnp.where(mask_p, jnp.exp(jnp.minimum(sp - lse_h, 0.0)), 0.0)
            dpc = lax.dot_general(doh, vc, NT_DIMS, preferred_element_type=F32)
            dpp = lax.dot_general(doh, vp, NT_DIMS, preferred_element_type=F32)
            dsc = (pc * (dpc - dl_h) * SCALE).astype(BF)
            dsp = (pp * (dpp - dl_h) * SCALE).astype(BF)
            dq = jnp.dot(dsc, kc, preferred_element_type=F32) + jnp.dot(dsp, kp, preferred_element_type=F32)

            @pl.when(live)
            def _(dq=dq, sl=sl):
                dq_ref[:, sl] = dq

            dk_ref[:, sl] = ck_ref[:, sl] + lax.dot_general(dsp, qh, TN_DIMS, preferred_element_type=F32)
            dv_ref[:, sl] = cv_ref[:, sl] + lax.dot_general(pp.astype(BF), doh, TN_DIMS, preferred_element_type=F32)
            ck_ref[:, sl] = lax.dot_general(dsc, qh, TN_DIMS, preferred_element_type=F32)
            cv_ref[:, sl] = lax.dot_general(pc.astype(BF), doh, TN_DIMS, preferred_element_type=F32)

    last = nb - 1
    cur = pl.BlockSpec((BLK, GROUP_W), lambda r, n: (jnp.minimum(n, last), r))
    prev = pl.BlockSpec((BLK, GROUP_W), lambda r, n: (jnp.maximum(n - 1, 0), r))
    shape = jax.ShapeDtypeStruct((length, dil * GROUP_W), F32)
    dq, dk, dv = pl.pallas_call(
        body, out_shape=[shape] * 3, grid=(dil, nb + 1),
        in_specs=[cur, cur, prev, cur, prev, cur, cur, cur], out_specs=[cur, prev, prev],
        scratch_shapes=[pltpu.VMEM((BLK, GROUP_W), F32)] * 2, name=f"dil_bwd_{g}",
        compiler_params=_params(("parallel", "arbitrary")))(
            view(q), view(k), view(k), view(v), view(v), view(do), view(lse), view(delta))
    return dq.reshape(s, GROUP_W), dk.reshape(s, GROUP_W), dv.reshape(s, GROUP_W)


def _attn_merge(outs, lses):
    s = outs[0].shape[0]
    tr = _pick_rows(s, 512)

    def body(o0, o1, o2, l0, l1, l2, m_ref, lse_ref):
        a, b, c = l0[...], l1[...], l2[...]
        mx = jnp.maximum(jnp.maximum(a, b), c)
        ea, eb, ec = jnp.exp(a - mx), jnp.exp(b - mx), jnp.exp(c - mx)
        den = ea + eb + ec
        m_ref[...] = (ea * o0[...] + eb * o1[...] + ec * o2[...]) / den
        lse_ref[...] = mx + jnp.log(den)

    spec = pl.BlockSpec((tr, GROUP_W), lambda i: (i, 0))
    return pl.pallas_call(
        body, out_shape=[jax.ShapeDtypeStruct((s, GROUP_W), F32)] * 2, grid=(s // tr,), in_specs=[spec] * 6,
        out_specs=[spec] * 2, name="attn_merge", compiler_params=_params(("parallel",)))(*outs, *lses)


def _attn_delta(dcat, merged, after=()):
    s = merged.shape[0]
    tr = _pick_rows(s, 512)

    def body(d_ref, m_ref, *rest):
        do_ref, dl_ref = rest[-2:]
        d = d_ref[...]
        prod = d * m_ref[...]
        do_ref[...] = d.astype(BF)
        for h in range(GROUP_W // HEAD_DIM):
            sl = slice(h * HEAD_DIM, (h + 1) * HEAD_DIM)
            dl_ref[:, sl] = jnp.broadcast_to(jnp.sum(prod[:, sl], axis=-1, keepdims=True), (tr, HEAD_DIM))

    spec = pl.BlockSpec((tr, GROUP_W), lambda i: (i, 0))
    return pl.pallas_call(
        body, out_shape=[jax.ShapeDtypeStruct((s, GROUP_W), BF), jax.ShapeDtypeStruct((s, GROUP_W), F32)],
        grid=(s // tr,), in_specs=[spec, spec] + [ANY] * len(after), out_specs=[spec, spec], name="attn_delta",
        compiler_params=_params(("parallel",)))(dcat, merged, *after)


def _mem_probs(qh, kh):
    sc = lax.dot_general(qh, kh, NT_DIMS, preferred_element_type=F32) * SCALE
    p = jnp.exp(sc - jnp.max(sc, axis=-1, keepdims=True))
    return p, jnp.sum(p, axis=-1, keepdims=True)


def _mem_fwd(name, proj, q_block, kv):
    s = proj.shape[0]
    tq = _pick_rows(s, 512)

    def body(q_ref, kv_ref, o_ref):
        for h in range(MEM_HEADS):
            sl = slice(h * HEAD_DIM, (h + 1) * HEAD_DIM)
            vsl = slice(MEM_W + h * HEAD_DIM, MEM_W + (h + 1) * HEAD_DIM)
            p, den = _mem_probs(q_ref[:, sl].astype(BF), kv_ref[:, sl].astype(BF))
            o_ref[:, sl] = jnp.dot(p.astype(BF), kv_ref[:, vsl].astype(BF), preferred_element_type=F32) / den

    return pl.pallas_call(
        body, out_shape=jax.ShapeDtypeStruct((s, MEM_W), F32), grid=(s // tq,),
        in_specs=[pl.BlockSpec((tq, MEM_W), lambda i: (i, q_block)), pl.BlockSpec(kv.shape, lambda i: (0, 0))],
        out_specs=pl.BlockSpec((tq, MEM_W), lambda i: (i, 0)), name=name, compiler_params=_params(("parallel",)))(proj, kv)


def _mem_bwd(name, proj, q_block, kv, dcat, d_block):
    s = proj.shape[0]
    tq = _pick_rows(s, 512)

    def body(q_ref, kv_ref, do_ref, dq_ref, dkv_ref):
        @pl.when(pl.program_id(0) == 0)
        def _():
            dkv_ref[...] = jnp.zeros_like(dkv_ref)

        for h in range(MEM_HEADS):
            sl = slice(h * HEAD_DIM, (h + 1) * HEAD_DIM)
            vsl = slice(MEM_W + h * HEAD_DIM, MEM_W + (h + 1) * HEAD_DIM)
            qh, kh, vh = q_ref[:, sl].astype(BF), kv_ref[:, sl].astype(BF), kv_ref[:, vsl].astype(BF)
            doh = do_ref[:, sl].astype(BF)
            p, den = _mem_probs(qh, kh)
            p = p / den
            dp = lax.dot_general(doh, vh, NT_DIMS, preferred_element_type=F32)
            ds = (p * (dp - jnp.sum(p * dp, axis=-1, keepdims=True)) * SCALE).astype(BF)
            dq_ref[:, sl] = jnp.dot(ds, kh, preferred_element_type=F32)
            dkv_ref[:, sl] += lax.dot_general(ds, qh, TN_DIMS, preferred_element_type=F32)
            dkv_ref[:, vsl] += lax.dot_general(p.astype(BF), doh, TN_DIMS, preferred_element_type=F32)

    whole = pl.BlockSpec(kv.shape, lambda i: (0, 0))
    return pl.pallas_call(
        body, out_shape=[jax.ShapeDtypeStruct((s, MEM_W), F32), jax.ShapeDtypeStruct(kv.shape, F32)], grid=(s // tq,),
        in_specs=[pl.BlockSpec((tq, MEM_W), lambda i: (i, q_block)), whole,
                  pl.BlockSpec((tq, MEM_W), lambda i: (i, d_block))],
        out_specs=[pl.BlockSpec((tq, MEM_W), lambda i: (i, 0)), whole], name=name,
        compiler_params=_params(("arbitrary",)))(proj, kv, dcat)


def _causal():
    t = lax.broadcasted_iota(jnp.int32, (BLK, BLK), 0)
    s = lax.broadcasted_iota(jnp.int32, (BLK, BLK), 1)
    return t >= s


def _sgu_norm(v_pre, ln_g, ln_b):
    vg = _gelu(v_pre)
    mu = jnp.mean(vg, axis=-1, keepdims=True)
    cen = vg - mu
    rstd = lax.rsqrt(jnp.mean(cen * cen, axis=-1, keepdims=True) + LN_EPS)
    xhat = cen * rstd
    return xhat, rstd, xhat * ln_g + ln_b


def _sgu_fwd(proj, ln_g, ln_b, w_sp, b_t):
    s = proj.shape[0]

    def body(u_ref, v_ref, g_ref, b_ref, w_ref, bt_ref, o_ref):
        _, _, vn = _sgu_norm(v_ref[...], g_ref[...], b_ref[...])
        vn = vn.astype(BF)
        tri = _causal()
        for grp in range(SGU_GROUPS):
            sl = slice(grp * HEAD_DIM, (grp + 1) * HEAD_DIM)
            w = jnp.where(tri, w_ref[grp], 0.0).astype(BF)
            mixed = jnp.dot(w, vn[:, sl], preferred_element_type=F32) + bt_ref[:, grp:grp + 1]
            o_ref[:, sl] = _gelu(u_ref[:, sl]) * mixed

    vec = pl.BlockSpec((1, SGU_W), lambda i: (0, 0))
    return pl.pallas_call(
        body, out_shape=jax.ShapeDtypeStruct((s, SGU_W), F32), grid=(s // BLK,),
        in_specs=[pl.BlockSpec((BLK, SGU_W), lambda i: (i, 0)), pl.BlockSpec((BLK, SGU_W), lambda i: (i, 1)), vec, vec,
                  pl.BlockSpec(w_sp.shape, lambda i: (0, 0, 0)), pl.BlockSpec(b_t.shape, lambda i: (0, 0))],
        out_specs=pl.BlockSpec((BLK, SGU_W), lambda i: (i, 0)), name="sgu_fwd",
        compiler_params=_params(("parallel",)))(proj, proj, ln_g, ln_b, w_sp, b_t)


def _sgu_bwd(proj, dcat, ln_g, ln_b, w_sp, b_t):
    s = proj.shape[0]

    def body(u_ref, v_ref, d_ref, g_ref, b_ref, w_ref, bt_ref, du_ref, dv_ref, dw_ref, db_ref, dg_ref, dbeta_ref,
             dvn_ref):
        @pl.when(pl.program_id(0) == 0)
        def _():
            dw_ref[...] = jnp.zeros_like(dw_ref)
            db_ref[...] = jnp.zeros_like(db_ref)
            dg_ref[...] = jnp.zeros_like(dg_ref)
            dbeta_ref[...] = jnp.zeros_like(dbeta_ref)

        v_pre = v_ref[...]
        gain = g_ref[...]
        xhat, rstd, vn = _sgu_norm(v_pre, gain, b_ref[...])
        vn = vn.astype(BF)
        tri = _causal()
        lane = lax.broadcasted_iota(jnp.int32, (BLK, HEAD_DIM), 1)
        db_acc = jnp.zeros((BLK, HEAD_DIM), F32)
        for grp in range(SGU_GROUPS):
            sl = slice(grp * HEAD_DIM, (grp + 1) * HEAD_DIM)
            w = jnp.where(tri, w_ref[grp], 0.0).astype(BF)
            vn_g = vn[:, sl]
            mixed = jnp.dot(w, vn_g, preferred_element_type=F32) + bt_ref[:, grp:grp + 1]
            u_pre = u_ref[:, sl]
            d_out = d_ref[:, sl]
            du_ref[:, sl] = (d_out * mixed * _gelu_grad(u_pre)).astype(BF)
            dmixed = d_out * _gelu(u_pre)
            dm = dmixed.astype(BF)
            dvn_ref[:, sl] = lax.dot_general(w, dm, TN_DIMS, preferred_element_type=F32)
            dw = lax.dot_general(dm, vn_g, NT_DIMS, preferred_element_type=F32)
            dw_ref[grp] += jnp.where(tri, dw, 0.0)
            db_acc += jnp.where(lane == grp, jnp.sum(dmixed, axis=-1, keepdims=True), 0.0)
        db_ref[...] += db_acc
        dvn = dvn_ref[...]
        dg_ref[...] += jnp.sum(dvn * xhat, axis=0, keepdims=True)
        dbeta_ref[...] += jnp.sum(dvn, axis=0, keepdims=True)
        dxh = dvn * gain
        dvg = rstd * (dxh - jnp.mean(dxh, axis=-1, keepdims=True) - xhat * jnp.mean(dxh * xhat, axis=-1, keepdims=True))
        dv_ref[...] = (dvg * _gelu_grad(v_pre)).astype(BF)

    vec = pl.BlockSpec((1, SGU_W), lambda i: (0, 0))
    row = pl.BlockSpec((BLK, SGU_W), lambda i: (i, 0))
    w_spec = pl.BlockSpec(w_sp.shape, lambda i: (0, 0, 0))
    sq = pl.BlockSpec((BLK, HEAD_DIM), lambda i: (0, 0))
    return pl.pallas_call(
        body,
        out_shape=[jax.ShapeDtypeStruct((s, SGU_W), BF), jax.ShapeDtypeStruct((s, SGU_W), BF),
                   jax.ShapeDtypeStruct(w_sp.shape, F32), jax.ShapeDtypeStruct((BLK, HEAD_DIM), F32),
                   jax.ShapeDtypeStruct((1, SGU_W), F32), jax.ShapeDtypeStruct((1, SGU_W), F32)],
        grid=(s // BLK,),
        in_specs=[row, pl.BlockSpec((BLK, SGU_W), lambda i: (i, 1)), row, vec, vec, w_spec,
                  pl.BlockSpec(b_t.shape, lambda i: (0, 0))],
        out_specs=[row, row, w_spec, sq, vec, vec], scratch_shapes=[pltpu.VMEM((BLK, SGU_W), F32)], name="sgu_bwd",
        compiler_params=_params(("arbitrary",)))(proj, proj, dcat, ln_g, ln_b, w_sp, b_t)


def _place():
    return lax.axis_index("x"), lax.axis_index("y"), lax.axis_index("c")


def _other_chips(x, y):
    return [(1 - x, y), (x, 1 - y), (1 - x, 1 - y)]


def _peer(x, y, c, mask):
    return (1 - x if mask & 4 else x, 1 - y if mask & 2 else y, 1 - c if mask & 1 else c)


def _in_hbm(a):
    return pltpu.with_memory_space_constraint(a, pltpu.HBM)


def _token_spec():
    return jax.ShapeDtypeStruct((8, LANES), F32), pl.BlockSpec(memory_space=pltpu.VMEM)


def _remote(src, dst, ssem, rsem, to):
    return pltpu.make_async_remote_copy(src_ref=src, dst_ref=dst, send_sem=ssem, recv_sem=rsem, device_id=to,
                                        device_id_type=MESH)


def _place_shard(name, src, layer, place, dtype):
    _, rows, cols = src.shape
    tr = _pick_rows(rows, 512)

    def body(p_ref, s_ref, o_ref):
        o_ref[...] = s_ref[...].astype(dtype)

    grid_spec = pltpu.PrefetchScalarGridSpec(
        num_scalar_prefetch=1, grid=(rows // tr,),
        in_specs=[pl.BlockSpec((None, tr, cols), lambda i, p: (layer, i, 0))],
        out_specs=pl.BlockSpec((None, tr, cols), lambda i, p: (p[1], i, 0)))
    return pl.pallas_call(body, out_shape=jax.ShapeDtypeStruct((N_CHIPS, rows, cols), dtype), grid_spec=grid_spec,
                          name=name, compiler_params=_params(("parallel",)))(place, src)


def _gather_copies(bufs, ssem, rsem):
    x, y, c = _place()
    me = 2 * x + y
    copies = []
    for ai, buf in enumerate(bufs):
        for k, (ox, oy) in enumerate(_other_chips(x, y)):
            copies.append(_remote(buf.at[me], buf.at[me], ssem.at[3 * ai + k], rsem.at[3 * ai + k], (ox, oy, c)))
    return copies


def _reduce_copies(grads, lands, ssem, rsem):
    x, y, c = _place()
    copies = []
    for a, (gr, land) in enumerate(zip(grads, lands)):
        for mask in range(1, N_DEV):
            px, py, pc = _peer(x, y, c, mask)
            copies.append(_remote(gr.at[pc, 2 * px + py], land.at[mask - 1], ssem.at[7 * a + mask - 1],
                                  rsem.at[7 * a + mask - 1], (px, py, pc)))
    return copies


def _half_copies(totals, ssem, rsem):
    x, y, c = _place()
    return [_remote(t.at[c], t.at[c], ssem.at[a], rsem.at[a], (x, y, 1 - c)) for a, t in enumerate(totals)]


def _gather_start(groups):
    flat = [s for grp in groups for s in grp]
    n, ng = len(flat), len(groups)

    def body(*refs):
        ins = refs[:n]
        sems = refs[n:n + 2 * ng]
        token = refs[-1]
        idx = 0
        for gi, grp in enumerate(groups):
            for cp in _gather_copies(ins[idx:idx + len(grp)], sems[2 * gi], sems[2 * gi + 1]):
                cp.start()
            idx += len(grp)
        token[...] = jnp.zeros_like(token)

    tok_shape, tok_spec = _token_spec()
    sem_shapes = []
    for grp in groups:
        sem_shapes += [pltpu.SemaphoreType.DMA((3 * len(grp),))] * 2
    res = pl.pallas_call(
        body, name="gather_start",
        out_shape=(*sem_shapes, *[pltpu.HBM(s.shape, s.dtype) for s in flat], tok_shape),
        in_specs=[HBM] * n, out_specs=(*[SEM] * (2 * ng), *[HBM] * n, tok_spec),
        input_output_aliases={i: 2 * ng + i for i in range(n)},
        compiler_params=pltpu.CompilerParams(has_side_effects=EFFECT))(*[_in_hbm(s) for s in flat])
    out, idx = [], 2 * ng
    for gi, grp in enumerate(groups):
        out.append((res[2 * gi], res[2 * gi + 1], list(res[idx:idx + len(grp)])))
        idx += len(grp)
    return out, res[-1]


def _gather_wait(name, ssem, rsem, slabs, after):
    n = len(slabs)

    def body(*refs):
        for cp in _gather_copies(refs[:n], refs[n], refs[n + 1]):
            cp.wait_send()
            cp.wait_recv()

    return pl.pallas_call(
        body, name=name, out_shape=tuple(pltpu.HBM(s.shape, s.dtype) for s in slabs),
        in_specs=[HBM] * n + [SEM, SEM] + [ANY] * len(after), out_specs=tuple([HBM] * n),
        input_output_aliases={i: i for i in range(n)},
        compiler_params=pltpu.CompilerParams(has_side_effects=EFFECT))(*slabs, ssem, rsem, *after)


def _reduce_start(name, grads):
    n = len(grads)
    lands = [lax.empty((N_DEV - 1, *g.shape[2:]), g.dtype) for g in grads]

    def body(*refs):
        token = refs[-1]
        for cp in _reduce_copies(refs[:n], refs[n:2 * n], refs[2 * n], refs[2 * n + 1]):
            cp.start()
        token[...] = jnp.zeros_like(token)

    tok_shape, tok_spec = _token_spec()
    sems = [pltpu.SemaphoreType.DMA((7 * n,))] * 2
    res = pl.pallas_call(
        body, name=name,
        out_shape=(*sems, *[pltpu.HBM(g.shape, g.dtype) for g in grads], *[pltpu.HBM(l.shape, l.dtype) for l in lands],
                   tok_shape),
        in_specs=[HBM] * (2 * n), out_specs=(SEM, SEM, *[HBM] * (2 * n), tok_spec),
        input_output_aliases={i: 2 + i for i in range(2 * n)},
        compiler_params=pltpu.CompilerParams(has_side_effects=EFFECT))(*[_in_hbm(t) for t in (*grads, *lands)])
    return res[0], res[1], list(res[2:2 + n]), list(res[2 + n:2 + 2 * n]), res[-1]


def _reduce_wait(name, ssem, rsem, grads, lands, after):
    n = len(grads)

    def body(*refs):
        for cp in _reduce_copies(refs[:n], refs[n:2 * n], refs[2 * n], refs[2 * n + 1]):
            cp.wait_send()
            cp.wait_recv()

    res = pl.pallas_call(
        body, name=name, out_shape=tuple(pltpu.HBM(t.shape, t.dtype) for t in (*grads, *lands)),
        in_specs=[HBM] * (2 * n) + [SEM, SEM] + [ANY] * len(after), out_specs=tuple([HBM] * (2 * n)),
        input_output_aliases={i: i for i in range(2 * n)},
        compiler_params=pltpu.CompilerParams(has_side_effects=EFFECT))(*grads, *lands, ssem, rsem, *after)
    return list(res[:n]), list(res[n:])


def _sum_pieces(name, grad, land, place):
    _, _, rows, cols = grad.shape
    tr = _pick_rows(rows, 256)

    def body(p_ref, g_ref, l_ref, o_ref):
        tot = g_ref[...].astype(F32)
        for k in range(N_DEV - 1):
            tot = tot + l_ref[k].astype(F32)
        o_ref[...] = tot

    grid_spec = pltpu.PrefetchScalarGridSpec(
        num_scalar_prefetch=1, grid=(rows // tr,),
        in_specs=[pl.BlockSpec((None, None, tr, cols), lambda i, p: (p[0], p[1], i, 0)),
                  pl.BlockSpec((N_DEV - 1, tr, cols), lambda i, p: (0, i, 0))],
        out_specs=pl.BlockSpec((None, tr, cols), lambda i, p: (p[0], i, 0)))
    return pl.pallas_call(body, out_shape=jax.ShapeDtypeStruct((2, rows, cols), F32), grid_spec=grid_spec, name=name,
                          compiler_params=_params(("parallel",)))(place, grad, land)


def _half_start(name, totals):
    n = len(totals)

    def body(*refs):
        token = refs[-1]
        for cp in _half_copies(refs[:n], refs[n], refs[n + 1]):
            cp.start()
        token[...] = jnp.zeros_like(token)

    tok_shape, tok_spec = _token_spec()
    res = pl.pallas_call(
        body, name=name,
        out_shape=(pltpu.SemaphoreType.DMA((n,)), pltpu.SemaphoreType.DMA((n,)),
                   *[pltpu.HBM(t.shape, t.dtype) for t in totals], tok_shape),
        in_specs=[HBM] * n, out_specs=(SEM, SEM, *[HBM] * n, tok_spec),
        input_output_aliases={i: 2 + i for i in range(n)},
        compiler_params=pltpu.CompilerParams(has_side_effects=EFFECT))(*[_in_hbm(t) for t in totals])
    return res[0], res[1], list(res[2:2 + n]), res[-1]


def _half_wait(name, ssem, rsem, totals, after):
    n = len(totals)

    def body(*refs):
        for cp in _half_copies(refs[:n], refs[n], refs[n + 1]):
            cp.wait_send()
            cp.wait_recv()

    res = pl.pallas_call(
        body, name=name, out_shape=tuple(pltpu.HBM(t.shape, t.dtype) for t in totals),
        in_specs=[HBM] * n + [SEM, SEM] + [ANY] * len(after), out_specs=tuple([HBM] * n),
        input_output_aliases={i: i for i in range(n)},
        compiler_params=pltpu.CompilerParams(has_side_effects=EFFECT))(*totals, ssem, rsem, *after)
    return list(res)


def _small_exchange(small):
    def body(small_ref, all_ref, local_sem, send_sems, recv_sems):
        x, y, c = _place()
        me = 4 * x + 2 * y + c
        own = pltpu.make_async_copy(small_ref, all_ref.at[me], local_sem)
        own.start()
        copies = []
        for mask in range(1, N_DEV):
            cp = _remote(small_ref, all_ref.at[me], send_sems.at[mask - 1], recv_sems.at[mask - 1], _peer(x, y, c, mask))
            cp.start()
            copies.append(cp)
        for cp in copies:
            cp.wait()
        own.wait()

    return pl.pallas_call(
        body, out_shape=jax.ShapeDtypeStruct((N_DEV, *small.shape), small.dtype), in_specs=[ANY], out_specs=ANY,
        scratch_shapes=[pltpu.SemaphoreType.DMA, pltpu.SemaphoreType.DMA((7,)), pltpu.SemaphoreType.DMA((7,))],
        name="small_grad_exchange")(small)


def _sum_devices(stacked):
    _, rows, lanes = stacked.shape
    tr = _pick_rows(rows, 512)

    def body(s_ref, o_ref):
        tot = s_ref[0]
        for k in range(1, N_DEV):
            tot = tot + s_ref[k]
        o_ref[...] = tot

    return pl.pallas_call(
        body, out_shape=jax.ShapeDtypeStruct((rows, lanes), F32), grid=(rows // tr,),
        in_specs=[pl.BlockSpec((N_DEV, tr, lanes), lambda i: (0, i, 0))], out_specs=pl.BlockSpec((tr, lanes), lambda i: (i, 0)),
        name="small_grad_sum", compiler_params=_params(("parallel",)))(stacked)


def _adamw(name, w, g, m, v, layer, prev=None):
    layers, rows, cols = w.shape
    tr = _pick_rows(rows, 256)
    c1 = 1.0 - ADAM_B1 ** ADAM_STEP
    c2 = 1.0 - ADAM_B2 ** ADAM_STEP

    def body(w_ref, g_ref, m_ref, v_ref, *rest):
        go_ref, d_ref, nm_ref, nv_ref = rest[-4:]
        gv = g_ref[...]
        nm = ADAM_B1 * m_ref[...] + (1.0 - ADAM_B1) * gv
        nv = ADAM_B2 * v_ref[...] + (1.0 - ADAM_B2) * (gv * gv)
        go_ref[...] = gv
        d_ref[...] = -ADAM_LR * ((nm / c1) / (jnp.sqrt(nv / c2) + ADAM_EPS) + ADAM_WD * w_ref[...])
        nm_ref[...] = nm
        nv_ref[...] = nv

    spec = pl.BlockSpec((None, tr, cols), lambda i: (layer, i, 0))
    prev = list(prev) if prev is not None else []
    return pl.pallas_call(
        body, out_shape=[jax.ShapeDtypeStruct((layers, rows, cols), F32)] * 4, grid=(rows // tr,),
        in_specs=[spec, pl.BlockSpec((tr, cols), lambda i: (i, 0)), spec, spec] + [ANY] * len(prev),
        out_specs=[spec] * 4, input_output_aliases={4 + i: i for i in range(len(prev))}, name=name,
        compiler_params=_params(("parallel",)))(w, g, m, v, *prev)


def _pack(vectors, pad_rows):
    flat = jnp.concatenate([t.reshape(-1) for t in vectors])
    rows = -(-flat.shape[0] // LANES)
    rows = -(-rows // pad_rows) * pad_rows
    return jnp.pad(flat, (0, rows * LANES - flat.shape[0])).reshape(rows, LANES)


def _unpack(packed, shapes):
    flat = packed.reshape(-1)
    out, off = [], 0
    for shp in shapes:
        size = math.prod(shp)
        out.append(flat[off:off + size].reshape(shp))
        off += size
    return out


def kernel(x, mem, positions, mix_norm, mem_norm, w_mem_kv, ffn_norm, w_gate, w_up, w_down, attn_w_in, attn_w_out, sgu_w_in, sgu_ln_g, sgu_ln_b, sgu_w_spatial, sgu_b_spatial, sgu_w_out, final_norm, loss_target, m_mix_norm, m_mem_norm, m_w_mem_kv, m_ffn_norm, m_w_gate, m_w_up, m_w_down, m_attn_w_in, m_attn_w_out, m_sgu_w_in, m_sgu_ln_g, m_sgu_ln_b, m_sgu_w_spatial, m_sgu_b_spatial, m_sgu_w_out, m_final_norm, v_mix_norm, v_mem_norm, v_w_mem_kv, v_ffn_norm, v_w_gate, v_w_up, v_w_down, v_attn_w_in, v_attn_w_out, v_sgu_w_in, v_sgu_ln_g, v_sgu_ln_b, v_sgu_w_spatial, v_sgu_b_spatial, v_sgu_w_out, v_final_norm):
    d_model = x.shape[2]
    x0, mem0, tgt = x[0], mem[0], loss_target[0]
    xi, yi, ci = _place()
    chip = 2 * xi + yi
    place = jnp.stack([ci, chip]).astype(jnp.int32)

    given_w = dict(mix_norm=mix_norm, mem_norm=mem_norm, w_mem_kv=w_mem_kv, ffn_norm=ffn_norm, w_gate=w_gate, w_up=w_up,
                   w_down=w_down, attn_w_in=attn_w_in, attn_w_out=attn_w_out, sgu_w_in=sgu_w_in, sgu_ln_g=sgu_ln_g,
                   sgu_ln_b=sgu_ln_b, sgu_w_spatial=sgu_w_spatial, sgu_b_spatial=sgu_b_spatial, sgu_w_out=sgu_w_out,
                   final_norm=final_norm)
    given_m = dict(mix_norm=m_mix_norm, mem_norm=m_mem_norm, w_mem_kv=m_w_mem_kv, ffn_norm=m_ffn_norm, w_gate=m_w_gate,
                   w_up=m_w_up, w_down=m_w_down, attn_w_in=m_attn_w_in, attn_w_out=m_attn_w_out, sgu_w_in=m_sgu_w_in,
                   sgu_ln_g=m_sgu_ln_g, sgu_ln_b=m_sgu_ln_b, sgu_w_spatial=m_sgu_w_spatial,
                   sgu_b_spatial=m_sgu_b_spatial, sgu_w_out=m_sgu_w_out, final_norm=m_final_norm)
    given_v = dict(mix_norm=v_mix_norm, mem_norm=v_mem_norm, w_mem_kv=v_w_mem_kv, ffn_norm=v_ffn_norm, w_gate=v_w_gate,
                   w_up=v_w_up, w_down=v_w_down, attn_w_in=v_attn_w_in, attn_w_out=v_attn_w_out, sgu_w_in=v_sgu_w_in,
                   sgu_ln_g=v_sgu_ln_g, sgu_ln_b=v_sgu_ln_b, sgu_w_spatial=v_sgu_w_spatial,
                   sgu_b_spatial=v_sgu_b_spatial, sgu_w_out=v_sgu_w_out, final_norm=v_final_norm)

    units = {"attn_w_in": ("attn_w_in", 0, "col"), "w_mem_kv0": ("w_mem_kv", 0, "row"), "attn_w_out": ("attn_w_out", 0, "col"),
             "w_gate0": ("w_gate", 0, "col"), "w_up0": ("w_up", 0, "col"), "w_down0": ("w_down", 0, "row"),
             "sgu_w_in": ("sgu_w_in", 0, "col"), "w_mem_kv1": ("w_mem_kv", 1, "row"), "sgu_w_out": ("sgu_w_out", 0, "row"),
             "w_gate1": ("w_gate", 1, "col"), "w_up1": ("w_up", 1, "col"), "w_down1": ("w_down", 1, "row")}
    gather_groups = [["attn_w_in"], ["w_mem_kv0", "attn_w_out"], ["w_gate0", "w_up0"],
                     ["w_down0", "sgu_w_in", "w_mem_kv1", "ln"], ["sgu_w_out", "w_gate1", "w_up1"], ["w_down1"]]

    slabs = {u: _place_shard(f"place_{u}", given_w[arr], layer, place, BF) for u, (arr, layer, _) in units.items()}
    slabs["ln"] = _place_shard("place_ln", jnp.concatenate([sgu_ln_g, sgu_ln_b])[None], 0, place, F32)
    in_flight, token = _gather_start([[slabs[u] for u in grp] for grp in gather_groups])
    weights = {}

    def arrive(gi, after):
        ssem, rsem, arrs = in_flight[gi]
        for u, full in zip(gather_groups[gi], _gather_wait(f"gather_wait_{gi}", ssem, rsem, arrs, after)):
            weights[u] = full if u == "ln" else Weight(full, units[u][2])

    w_sp = sgu_w_spatial[0]
    b_t = sgu_b_spatial[0].T
    tables = _rope_tables(positions[0])

    def residual(acc, extra):
        return [extra[0] + acc[0]]

    def memory_kv(layer):
        mem_n = _rms_fwd(f"mem_norm_{layer}", mem0, mem_norm[layer:layer + 1])
        return mem_n, _mm_nn(f"mem_kv_{layer}", mem_n, weights[f"w_mem_kv{layer}"])[0]

    h0 = _rms_fwd("mix_norm_0", x0, mix_norm[0:1], after=[token])
    arrive(0, [h0])
    proj0 = _mm_nn("attn_in", h0, weights["attn_w_in"])[0]
    arrive(1, [proj0])
    qkv = _rope_fwd(proj0, tables)
    qs, ks, vs = qkv[0:3], qkv[3:6], qkv[6:9]
    outs, lses = [], []
    for g, dil in enumerate(DILATIONS):
        o, l = _dil_fwd(g, qs[g], ks[g], vs[g], dil)
        outs.append(o)
        lses.append(l)
    merged, lse = _attn_merge(outs, lses)
    mem_n0, kv0 = memory_kv(0)
    mem_out0 = _mem_fwd("mem_fwd_0", proj0, 9, kv0)
    cat0 = jnp.concatenate([merged, mem_out0], axis=1)
    x1 = _mm_nn("attn_out", cat0, weights["attn_w_out"], extras=[x0], epilogue=residual)[0]
    arrive(2, [x1])
    hf0 = _rms_fwd("ffn_norm_0", x1, ffn_norm[0:1])
    g0, u0, act0 = _gate_up("gate_up_0", hf0, weights["w_gate0"], weights["w_up0"])
    arrive(3, [act0])
    x2 = _mm_nn("down_0", act0, weights["w_down0"], extras=[x1], epilogue=residual)[0]

    ln_all = weights["ln"]
    ln_g = ln_all[:, 0, :].reshape(1, SGU_W)
    ln_b = ln_all[:, 1, :].reshape(1, SGU_W)
    h1 = _rms_fwd("mix_norm_1", x2, mix_norm[1:2])
    proj1 = _mm_nn("sgu_in", h1, weights["sgu_w_in"])[0]
    arrive(4, [proj1])
    sgu_out = _sgu_fwd(proj1, ln_g, ln_b, w_sp, b_t)
    mem_n1, kv1 = memory_kv(1)
    mem_out1 = _mem_fwd("mem_fwd_1", proj1, 6, kv1)
    cat1 = jnp.concatenate([sgu_out, mem_out1], axis=1)
    x3 = _mm_nn("sgu_out", cat1, weights["sgu_w_out"], extras=[x2], epilogue=residual)[0]
    hf1 = _rms_fwd("ffn_norm_1", x3, ffn_norm[1:2])
    g1, u1, act1 = _gate_up("gate_up_1", hf1, weights["w_gate1"], weights["w_up1"])
    arrive(5, [act1])
    x4 = _mm_nn("down_1", act1, weights["w_down1"], extras=[x3], epilogue=residual)[0]

    d4, g_final, loss_part = _final_loss(x4, tgt, final_norm.reshape(1, d_model))
    loss = lax.psum(loss_part[0, 0], ("x", "y", "c"))

    outputs = {}

    def start_reduce(tag, names, grads):
        ssem, rsem, grads, lands, tok = _reduce_start(f"reduce_start_{tag}", grads)
        return dict(tag=tag, names=names, ssem=ssem, rsem=rsem, grads=grads, lands=lands), tok

    def finish_reduce(st, after):
        grads, lands = _reduce_wait(f"reduce_wait_{st['tag']}", st["ssem"], st["rsem"], st["grads"], st["lands"], after)
        totals = [_sum_pieces(f"sum_{u}", g, l, place) for u, g, l in zip(st["names"], grads, lands)]
        ssem, rsem, totals, tok = _half_start(f"half_start_{st['tag']}", totals)
        return dict(tag=st["tag"], names=st["names"], ssem=ssem, rsem=rsem, totals=totals), tok

    def finish_update(st, after):
        totals = _half_wait(f"half_wait_{st['tag']}", st["ssem"], st["rsem"], st["totals"], after)
        for u, tot in zip(st["names"], totals):
            arr, layer, _ = units[u]
            w = given_w[arr]
            outputs[arr] = _adamw(f"adamw_{u}", w, tot.reshape(w.shape[1:]), given_m[arr], given_v[arr], layer,
                                  outputs.get(arr))

    def ffn_bwd(layer, d_out, xin, h, g, u, act):
        wd, wg, wu = weights[f"w_down{layer}"], weights[f"w_gate{layer}"], weights[f"w_up{layer}"]
        gr_down = _mm_tn(f"d_down_{layer}", act, d_out, wd)
        dg, du = _mm_nt(f"d_act_{layer}", [d_out], [wd], out_dtypes=(BF, BF), extras=[g, u], epilogue=_swiglu_bwd_epilogue,
                        col_chunk=EPILOGUE_CHUNK)
        gr_gate = _mm_tn(f"d_gate_{layer}", h, dg, wg)
        gr_up = _mm_tn(f"d_up_{layer}", h, du, wu)
        st, tok = start_reduce(f"ffn{layer}", [f"w_down{layer}", f"w_gate{layer}", f"w_up{layer}"], [gr_down, gr_gate, gr_up])
        dh = _mm_nt(f"d_ffn_h_{layer}", [dg, du], [wg, wu], after=[tok])[0]
        d_in, g_norm = _rms_bwd(f"ffn_norm_bwd_{layer}", xin, ffn_norm[layer:layer + 1], dh, d_out)
        return st, d_in, g_norm, dg

    def memory_bwd(layer, mem_n, dkv):
        dkv = dkv.astype(BF)
        wkv = weights[f"w_mem_kv{layer}"]
        gr = _mm_tn(f"d_mem_kv_{layer}", mem_n, dkv, wkv)
        d_mem_n = _mm_nt(f"d_mem_n_{layer}", [dkv], [wkv])[0]
        return gr, _rms_bwd(f"mem_norm_bwd_{layer}", mem0, mem_norm[layer:layer + 1], d_mem_n)[1]

    st_ffn1, d3, g_ffn1, _ = ffn_bwd(1, d4, x3, hf1, g1, u1, act1)
    gr_sgu_out = _mm_tn("d_sgu_out", cat1, d3, weights["sgu_w_out"])
    dcat1 = _mm_nt("d_cat_1", [d3], [weights["sgu_w_out"]])[0]
    st_ffn1, tok = finish_reduce(st_ffn1, [dcat1])
    dq_mem1, dkv1 = _mem_bwd("mem_bwd_1", proj1, 6, kv1, dcat1, 3)
    gr_kv1, g_mem1 = memory_bwd(1, mem_n1, dkv1)
    du_pre, dv_pre, g_wsp, g_bsp_t, g_ln_g, g_ln_b = _sgu_bwd(proj1, dcat1, ln_g, ln_b, w_sp, b_t)
    dproj1 = jnp.concatenate([du_pre, dv_pre, dq_mem1.astype(BF)], axis=1)
    gr_sgu_in = _mm_tn("d_sgu_in", h1, dproj1, weights["sgu_w_in"], after=[tok])
    finish_update(st_ffn1, [gr_sgu_in])
    st_mix1, tok = start_reduce("mix1", ["sgu_w_out", "w_mem_kv1", "sgu_w_in"], [gr_sgu_out, gr_kv1, gr_sgu_in])
    dh1 = _mm_nt("d_h_1", [dproj1], [weights["sgu_w_in"]], after=[tok])[0]
    d2, g_mix1 = _rms_bwd("mix_norm_bwd_1", x2, mix_norm[1:2], dh1, d3)

    st_ffn0, d1, g_ffn0, dg0 = ffn_bwd(0, d2, x1, hf0, g0, u0, act0)
    gr_attn_out = _mm_tn("d_attn_out", cat0, d1, weights["attn_w_out"])
    st_mix1, tok = finish_reduce(st_mix1, [gr_attn_out])
    dcat0 = _mm_nt("d_cat_0", [d1], [weights["attn_w_out"]], after=[tok])[0]
    dq_mem0, dkv0 = _mem_bwd("mem_bwd_0", proj0, 9, kv0, dcat0, 1)
    finish_update(st_mix1, [dq_mem0])
    gr_kv0, g_mem0 = memory_bwd(0, mem_n0, dkv0)
    st_ffn0, tok = finish_reduce(st_ffn0, [g_mem0])
    d_merged, delta = _attn_delta(dcat0, merged, after=[tok])
    dqs, dks, dvs = [], [], []
    for g, dil in enumerate(DILATIONS):
        dq, dk, dv = _dil_bwd(g, qs[g], ks[g], vs[g], d_merged, lse, delta, dil)
        dqs.append(dq)
        dks.append(dk)
        dvs.append(dv)
    dproj0 = _rope_bwd(dqs + dks + dvs, dq_mem0, tables)
    finish_update(st_ffn0, [dproj0])
    gr_attn_in = _mm_tn("d_attn_in", h0, dproj0, weights["attn_w_in"])
    st_mix0, tok = start_reduce("mix0", ["attn_w_out", "w_mem_kv0", "attn_w_in"], [gr_attn_out, gr_kv0, gr_attn_in])
    dh0 = _mm_nt("d_h_0", [dproj0], [weights["attn_w_in"]], after=[tok])[0]
    d0, g_mix0 = _rms_bwd("mix_norm_bwd_0", x0, mix_norm[0:1], dh0, d1)

    small_grads = [jnp.concatenate([g_mix0, g_mix1]), jnp.concatenate([g_mem0, g_mem1]),
                   jnp.concatenate([g_ffn0, g_ffn1]), g_wsp, g_bsp_t[:, :SGU_GROUPS].T, g_final, g_ln_g, g_ln_b]
    small_all = _small_exchange(_pack(small_grads, LANES))
    st_mix0, tok = finish_reduce(st_mix0, [small_all])
    g_mix, g_mem, g_ffn, g_wsp, g_bsp, g_final, g_ln_g, g_ln_b = _unpack(_sum_devices(small_all),
                                                                         [t.shape for t in small_grads])
    shard_w = sgu_ln_g.shape[-1]
    g_ln_g = lax.dynamic_slice_in_dim(g_ln_g, chip * shard_w, shard_w, axis=1)
    g_ln_b = lax.dynamic_slice_in_dim(g_ln_b, chip * shard_w, shard_w, axis=1)
    small_names = ["mix_norm", "mem_norm", "ffn_norm", "sgu_w_spatial", "sgu_b_spatial", "final_norm", "sgu_ln_g",
                   "sgu_ln_b"]
    small_g = [g_mix, g_mem, g_ffn, g_wsp, g_bsp, g_final, g_ln_g, g_ln_b]
    small_shapes = [given_w[k].shape for k in small_names]
    packed = [_pack(t, LANES) for t in ([given_w[k] for k in small_names], small_g, [given_m[k] for k in small_names],
                                    [given_v[k] for k in small_names])]
    small_out = _adamw("adamw_small", packed[0][None], packed[1], packed[2][None], packed[3][None], 0)
    finish_update(st_mix0, [small_out[0]])
    for k, gk, dk, mk, vk in zip(small_names, *[_unpack(t[0], small_shapes) for t in small_out]):
        outputs[k] = (gk, dk, mk, vk)

    order = ["mix_norm", "mem_norm", "w_mem_kv", "ffn_norm", "w_gate", "w_up", "w_down", "attn_w_in", "attn_w_out",
             "sgu_w_in", "sgu_ln_g", "sgu_ln_b", "sgu_w_spatial", "sgu_b_spatial", "sgu_w_out", "final_norm"]
    return (loss, d0[None], *[outputs[k][0] for k in order], *[outputs[k][1] for k in order],
            *[outputs[k][2] for k in order], *[outputs[k][3] for k in order])
```

```python
import math

import jax
import jax.numpy as jnp
from jax import lax
from jax.experimental import pallas as pl
from jax.experimental.pallas import tpu as pltpu

F32 = jnp.float32
BF = jnp.bfloat16
MESH = pl.DeviceIdType.MESH

HEAD_DIM = 128
MEM_HEADS = 4
MEM_W = MEM_HEADS * HEAD_DIM
GROUP_W = 4 * HEAD_DIM
DILATIONS = (1, 4, 16)
BLK = 128
SGU_GROUPS = 12
SGU_W = SGU_GROUPS * HEAD_DIM
ROT_HALF = 16
ROPE_THETA = 500000.0
NORM_EPS = 1e-6
LN_EPS = 1e-5
NEG_INF = -1e30
SCALE = HEAD_DIM ** -0.5
ADAM_LR, ADAM_B1, ADAM_B2, ADAM_EPS, ADAM_WD, ADAM_STEP = 0.001, 0.9, 0.999, 1e-08, 0.01, 10

VMEM_LIMIT = 48 * 2 ** 20
VMEM_TILE_BUDGET = 36 * 2 ** 20
N_CHIPS = 4
N_DEV = 8
LANES = 128
EPILOGUE_CHUNK = 256

NT_DIMS = (((1,), (1,)), ((), ()))
TN_DIMS = (((0,), (0,)), ((), ()))
NN_DIMS = (((1,), (0,)), ((), ()))

ANY = pl.BlockSpec(memory_space=pl.ANY)
HBM = pl.BlockSpec(memory_space=pltpu.HBM)
SEM = pl.BlockSpec(memory_space=pltpu.SEMAPHORE)
EFFECT = pltpu.SideEffectType.DATAFLOW_SIDE_EFFECTING


def _params(sem):
    return pltpu.CompilerParams(dimension_semantics=sem, vmem_limit_bytes=VMEM_LIMIT)


def _pick(n, cap):
    if n <= cap:
        return n
    best = None
    for t in range(LANES, cap + 1, LANES):
        if n % t == 0:
            best = t
    assert best is not None, (n, cap)
    return best


def _pick_rows(n, cap):
    t = min(n, cap)
    while n % t:
        t //= 2
    return t


def _mm(name, dims, a_list, a_specs, b_list, b_specs, pairs, n_acc, acc_shape, grid, extras, e_specs,
        out_shapes, out_specs, epilogue, after=(), col_chunk=None, store=None):
    na, nb, ne, no = len(a_list), len(b_list), len(extras), len(out_shapes)
    nk = grid[-1]

    def products(a, b, cols=None):
        sums = [None] * n_acc
        for ai, bi, ci in pairs:
            bv = b[bi]
            if cols is None:
                bv = bv[...]
            elif dims == NT_DIMS:
                bv = bv[cols, :]
            else:
                bv = bv[:, cols]
            if bv.ndim == 3:
                bv = bv.reshape(-1, bv.shape[-1])
            prod = lax.dot_general(a[ai][...].astype(BF), bv.astype(BF), dims, preferred_element_type=F32)
            sums[ci] = prod if sums[ci] is None else sums[ci] + prod
        return sums

    def body(*refs):
        a = refs[:na]
        b = refs[na:na + nb]
        e = refs[na + nb:na + nb + ne]
        off = na + nb + ne + len(after)
        o = refs[off:off + no]
        acc = refs[off + no:]

        def finish(sums):
            outs = epilogue(sums, [r[...] for r in e])
            if store is not None:
                store(o, outs)
                return
            for r, v in zip(o, outs):
                r[...] = v.astype(r.dtype)

        if nk == 1 and col_chunk:
            width = acc_shape[1]
            for c0 in range(0, width, col_chunk):
                cols = slice(c0, min(c0 + col_chunk, width))
                outs = epilogue(products(a, b, cols), [r[:, cols] for r in e])
                for r, v in zip(o, outs):
                    r[:, cols] = v.astype(r.dtype)
            return
        if nk == 1:
            finish(products(a, b))
            return
        k = pl.program_id(len(grid) - 1)

        @pl.when(k == 0)
        def _():
            for c, v in zip(acc, products(a, b)):
                c[...] = v

        @pl.when(jnp.logical_and(k > 0, k < nk - 1))
        def _():
            for c, v in zip(acc, products(a, b)):
                c[...] += v

        @pl.when(k == nk - 1)
        def _():
            finish([c[...] + v for c, v in zip(acc, products(a, b))])

    ins = [*a_list, *b_list, *extras, *after]
    in_specs = [*a_specs, *b_specs, *e_specs, *([ANY] * len(after))]
    sem = ("parallel",) * (len(grid) - 1) + ("arbitrary",)
    scratch = [] if nk == 1 else [pltpu.VMEM(acc_shape, F32)] * n_acc
    return pl.pallas_call(
        body, out_shape=out_shapes, grid=grid, in_specs=in_specs, out_specs=out_specs, scratch_shapes=scratch,
        name=name, compiler_params=_params(sem))(*ins)


def _tile_bytes(blocks, single=()):
    size = lambda s, d: math.prod(s) * jnp.dtype(d).itemsize
    return sum(2 * size(s, d) for s, d in blocks) + sum(size(s, d) for s, d in single)


def _first(acc, extra):
    return [acc[0]]


class Weight:
    def __init__(self, arr, axis):
        self.arr, self.axis = arr, axis
        _, self.rows, self.cols = arr.shape


def _mm_nn(name, a, w, extras=(), epilogue=_first, out_dtypes=(F32,), after=()):
    m, kdim = a.shape
    b_spec = None
    if w.axis == "col":
        n_total = N_CHIPS * w.cols
        tn = _pick(w.cols, 1408)
        tk = _pick(kdim, 2048)
        ncb = w.cols // tn
        gn, gk = N_CHIPS * ncb, kdim // tk
        b_map = lambda n, i, k: (n // ncb, k, n % ncb)
    elif kdim <= 2048:
        n_total = w.cols
        tn = _pick(w.cols, 1024)
        tk = kdim
        gn, gk = n_total // tn, 1
        b_spec = pl.BlockSpec((N_CHIPS, w.rows, tn), lambda n, i, k: (0, 0, n))
    else:
        n_total = w.cols
        tn = _pick(w.cols, 1024)
        tk = _pick(w.rows, 1408)
        nkb = w.rows // tk
        gn, gk = n_total // tn, N_CHIPS * nkb
        b_map = lambda n, i, k: (k // nkb, k % nkb, n)
    if b_spec is None:
        b_spec = pl.BlockSpec((None, tk, tn), b_map)
    for tm in (1024, 512, 256, 128):
        if m % tm:
            continue
        blocks = [((tm, tk), a.dtype), ((tk, tn), BF)] + [((tm, tn), e.dtype) for e in extras]
        blocks += [((tm, tn), d) for d in out_dtypes] + [((tm, tn), BF)]
        if _tile_bytes(blocks) <= VMEM_TILE_BUDGET:
            break
    o_spec = pl.BlockSpec((tm, tn), lambda n, i, k: (i, n))
    return _mm(
        name, NN_DIMS, [a], [pl.BlockSpec((tm, tk), lambda n, i, k: (i, k))],
        [w.arr], [b_spec], [(0, 0, 0)], 1, (tm, tn), (gn, m // tm, gk),
        list(extras), [o_spec] * len(extras),
        [jax.ShapeDtypeStruct((m, n_total), d) for d in out_dtypes], [o_spec] * len(out_dtypes), epilogue, after)


def _gate_up(name, h, wg, wu):
    m, kdim = h.shape
    tn = _pick(wg.cols, 1408)
    tk = _pick(kdim, 2048)
    ncb = wg.cols // tn
    single = kdim == tk
    for tm in (1024, 512, 256, 128):
        blocks = [((tm, tk), BF)] + [((tm, tn), BF)] * 3
        weights = [((tk, tn), BF)] * 2
        if m % tm == 0 and _tile_bytes(blocks + ([] if single else weights), weights if single else ()) <= VMEM_TILE_BUDGET:
            break
    b_spec = pl.BlockSpec((None, tk, tn), lambda n, i, k: (n // ncb, k, n % ncb),
                          pipeline_mode=pl.Buffered(1) if single else None)
    o_spec = pl.BlockSpec((tm, tn), lambda n, i, k: (i, n))
    n_total = N_CHIPS * wg.cols

    def epilogue(acc, extra):
        g, u = acc
        return [g, u, g * (1.0 / (1.0 + jnp.exp(-g))) * u]

    return _mm(
        name, NN_DIMS, [h], [pl.BlockSpec((tm, tk), lambda n, i, k: (i, k))], [wg.arr, wu.arr], [b_spec, b_spec],
        [(0, 0, 0), (0, 1, 1)], 2, (tm, tn), (N_CHIPS * ncb, m // tm, kdim // tk), [], [],
        [jax.ShapeDtypeStruct((m, n_total), BF)] * 3, [o_spec] * 3, epilogue, col_chunk=EPILOGUE_CHUNK)


def _mm_nt(name, dys, ws, out_dtypes=(F32,), extras=(), epilogue=_first, after=(), col_chunk=None):
    m = dys[0].shape[0]
    w0 = ws[0]
    npair = len(dys)
    if w0.axis == "col":
        k_total = w0.rows
        tko = _pick(k_total, 1024)
        tkc = _pick(w0.cols, 1408)
        nkb = w0.cols // tkc
        go, gk = k_total // tko, N_CHIPS * nkb
        b_map = lambda o, i, k: (k // nkb, o, k % nkb)
    else:
        k_total = N_CHIPS * w0.rows
        tko = _pick(w0.rows, 1408)
        tkc = _pick(w0.cols, 2048)
        nob = w0.rows // tko
        go, gk = N_CHIPS * nob, w0.cols // tkc
        b_map = lambda o, i, k: (o // nob, o % nob, k)
    single = gk == 1
    for tm in (1024, 512, 256, 128):
        if m % tm:
            continue
        blocks = [((tm, tkc), d.dtype) for d in dys]
        blocks += [((tm, tko), e.dtype) for e in extras] + [((tm, tko), d) for d in out_dtypes]
        blocks += [((tm, tko), BF)]
        weights = [((tko, tkc), BF)] * npair
        if _tile_bytes(blocks + ([] if single else weights), weights if single else ()) <= VMEM_TILE_BUDGET:
            break
    o_spec = pl.BlockSpec((tm, tko), lambda o, i, k: (i, o))
    return _mm(
        name, NT_DIMS, list(dys), [pl.BlockSpec((tm, tkc), lambda o, i, k: (i, k))] * npair,
        [w.arr for w in ws],
        [pl.BlockSpec((None, tko, tkc), b_map, pipeline_mode=pl.Buffered(1) if single else None)] * npair,
        [(i, i, 0) for i in range(npair)], 1, (tm, tko), (go, m // tm, gk), list(extras), [o_spec] * len(extras),
        [jax.ShapeDtypeStruct((m, k_total), d) for d in out_dtypes], [o_spec] * len(out_dtypes), epilogue, after,
        col_chunk if gk == 1 else None)


def _mm_tn(name, a, dy, w, after=()):
    m, k_total = a.shape
    rows2 = w.rows // 2
    tn = _pick(w.cols, 1408)
    ncb = w.cols // tn
    epilogue, store = _first, None
    if k_total <= 2048 and w.axis == "col":
        tkr = k_total
        gr, gn = 1, N_CHIPS * ncb
        o_spec = pl.BlockSpec((2, None, rows2, tn), lambda r, n, t: (0, n // ncb, 0, n % ncb))
        epilogue = lambda acc, extra: [acc[0].reshape(2, rows2, tn)]
    elif k_total <= 2048:
        tkr = k_total
        gr, gn = 1, ncb
        o_spec = pl.BlockSpec((2, N_CHIPS, rows2, tn), lambda r, n, t: (0, 0, 0, n))

        def store(o_refs, outs):
            for j in range(N_CHIPS):
                for h in range(2):
                    lo = (2 * j + h) * rows2
                    o_refs[0][h, j] = outs[0][lo:lo + rows2].astype(BF)
    elif rows2 % LANES:
        tkr = w.rows
        assert w.axis == "row"
        gr, gn = N_CHIPS, ncb
        o_spec = pl.BlockSpec((2, None, rows2, tn), lambda r, n, t: (0, r, 0, n))
        epilogue = lambda acc, extra: [acc[0].reshape(2, rows2, tn)]
    else:
        tkr = _pick(rows2, 1408)
        nrb = rows2 // tkr
        if w.axis == "col":
            gr, gn = w.rows // tkr, N_CHIPS * ncb
            o_map = lambda r, n, t: (r // nrb, n // ncb, r % nrb, n % ncb)
        else:
            per = w.rows // tkr
            gr, gn = N_CHIPS * per, ncb
            o_map = lambda r, n, t: ((r % per) // nrb, r // per, (r % per) % nrb, n)
        o_spec = pl.BlockSpec((None, None, tkr, tn), o_map)
    for tmk in (1024, 512, 256, 128):
        blocks = [((tmk, tkr), a.dtype), ((tmk, tn), dy.dtype), ((tkr, tn), BF), ((tkr, tn), BF)]
        if m % tmk == 0 and _tile_bytes(blocks) <= VMEM_TILE_BUDGET:
            break
    return _mm(
        name, TN_DIMS, [a], [pl.BlockSpec((tmk, tkr), lambda r, n, t: (t, r))],
        [dy], [pl.BlockSpec((tmk, tn), lambda r, n, t: (t, n))], [(0, 0, 0)], 1, (tkr, tn), (gr, gn, m // tmk), [], [],
        [jax.ShapeDtypeStruct((2, N_CHIPS, rows2, w.cols), BF)], [o_spec], epilogue, after, store=store)[0]


def _rms_fwd(name, x, g, after=()):
    s, d = x.shape
    tr = _pick_rows(s, 512)

    def body(x_ref, g_ref, *rest):
        h_ref = rest[-1]
        xf = x_ref[...]
        r = lax.rsqrt(jnp.mean(xf * xf, axis=-1, keepdims=True) + NORM_EPS)
        h_ref[...] = (xf * r * g_ref[...]).astype(BF)

    return pl.pallas_call(
        body, out_shape=jax.ShapeDtypeStruct((s, d), BF), grid=(s // tr,),
        in_specs=[pl.BlockSpec((tr, d), lambda i: (i, 0)), pl.BlockSpec((1, d), lambda i: (0, 0))] + [ANY] * len(after),
        out_specs=pl.BlockSpec((tr, d), lambda i: (i, 0)), name=name, compiler_params=_params(("parallel",)))(x, g, *after)


def _rms_bwd(name, x, g, dh, dres=None):
    s, d = x.shape
    tr = _pick_rows(s, 256)
    has_res = dres is not None

    def body(*refs):
        if has_res:
            x_ref, g_ref, dh_ref, dres_ref, dx_ref, dg_ref = refs
        else:
            x_ref, g_ref, dh_ref, dx_ref, dg_ref = refs
        xf = x_ref[...]
        r = lax.rsqrt(jnp.mean(xf * xf, axis=-1, keepdims=True) + NORM_EPS)
        xr = xf * r
        dy = dh_ref[...]
        a = dy * g_ref[...]
        dx = r * (a - xr * jnp.mean(a * xr, axis=-1, keepdims=True))
        if has_res:
            dx = dx + dres_ref[...]
        dx_ref[...] = dx

        @pl.when(pl.program_id(0) == 0)
        def _():
            dg_ref[...] = jnp.zeros_like(dg_ref)

        dg_ref[...] += jnp.sum(dy * xr, axis=0, keepdims=True)

    row = pl.BlockSpec((tr, d), lambda i: (i, 0))
    vec = pl.BlockSpec((1, d), lambda i: (0, 0))
    ins = [x, g, dh] + ([dres] if has_res else [])
    in_specs = [row, vec, row] + ([row] if has_res else [])
    return pl.pallas_call(
        body, out_shape=[jax.ShapeDtypeStruct((s, d), F32), jax.ShapeDtypeStruct((1, d), F32)], grid=(s // tr,),
        in_specs=in_specs, out_specs=[row, vec], name=name, compiler_params=_params(("arbitrary",)))(*ins)


def _final_loss(x, tgt, g):
    s, d = x.shape
    tr = _pick_rows(s, 256)

    def body(x_ref, t_ref, g_ref, dx_ref, dg_ref, loss_ref):
        xf = x_ref[...]
        gain = g_ref[...]
        r = lax.rsqrt(jnp.mean(xf * xf, axis=-1, keepdims=True) + NORM_EPS)
        xr = xf * r
        err = xr * gain - t_ref[...]
        dy = err * (1.0 / d)
        a = dy * gain
        dx_ref[...] = r * (a - xr * jnp.mean(a * xr, axis=-1, keepdims=True))

        @pl.when(pl.program_id(0) == 0)
        def _():
            dg_ref[...] = jnp.zeros_like(dg_ref)
            loss_ref[...] = jnp.zeros_like(loss_ref)

        dg_ref[...] += jnp.sum(dy * xr, axis=0, keepdims=True)
        part = 0.5 * jnp.sum(jnp.mean(err * err, axis=-1, keepdims=True), axis=0, keepdims=True)
        loss_ref[...] += jnp.broadcast_to(part, loss_ref.shape)

    row = pl.BlockSpec((tr, d), lambda i: (i, 0))
    vec = pl.BlockSpec((1, d), lambda i: (0, 0))
    return pl.pallas_call(
        body, out_shape=[jax.ShapeDtypeStruct((s, d), F32), jax.ShapeDtypeStruct((1, d), F32),
                         jax.ShapeDtypeStruct((8, LANES), F32)],
        grid=(s // tr,), in_specs=[row, row, vec], out_specs=[row, vec, pl.BlockSpec((8, LANES), lambda i: (0, 0))],
        name="final_loss", compiler_params=_params(("arbitrary",)))(x, tgt, g)


def _swiglu_bwd_epilogue(acc, extra):
    dact = acc[0]
    g, u = extra[0].astype(F32), extra[1].astype(F32)
    sig = 1.0 / (1.0 + jnp.exp(-g))
    return [dact * u * sig * (1.0 + g * (1.0 - sig)), dact * g * sig]


GELU_C = math.sqrt(2.0 / math.pi)
GELU_A = 0.044715


def _gelu(x):
    return 0.5 * x * (1.0 + jnp.tanh(GELU_C * (x + GELU_A * x * x * x)))


def _gelu_grad(x):
    t = jnp.tanh(GELU_C * (x + GELU_A * x * x * x))
    return 0.5 * (1.0 + t) + 0.5 * x * (1.0 - t * t) * GELU_C * (1.0 + 3.0 * GELU_A * x * x)


def _rope_tables(positions):
    inv_freq = ROPE_THETA ** (-jnp.arange(ROT_HALF, dtype=F32) / ROT_HALF)
    ang = positions.astype(F32)[:, None] * inv_freq
    cos, sin = jnp.cos(ang), jnp.sin(ang)
    s = ang.shape[0]
    rest = HEAD_DIM - 2 * ROT_HALF
    zeros = jnp.zeros((s, ROT_HALF), F32)
    cos_t = jnp.concatenate([cos, cos, jnp.ones((s, rest), F32)], axis=1)
    sin_a = jnp.concatenate([-sin, zeros, jnp.zeros((s, rest), F32)], axis=1)
    sin_b = jnp.concatenate([zeros, sin, jnp.zeros((s, rest), F32)], axis=1)
    return cos_t, sin_a, sin_b


def _rope_head(xh, cos_t, sin_a, sin_b):
    up = pltpu.roll(xh, HEAD_DIM - ROT_HALF, 1)
    down = pltpu.roll(xh, ROT_HALF, 1)
    return xh * cos_t + up * sin_a + down * sin_b


def _residue(r, rows, dil):
    return slice(None) if dil == 1 else pl.ds(r, rows, stride=dil)


ROPE_TILE = 256
N_PARTS = 9
HEADS_PER_GROUP = GROUP_W // HEAD_DIM
N_HEADS_IN = N_PARTS * HEADS_PER_GROUP


def _rope_fwd(proj, tables):
    s = proj.shape[0]
    tm = _pick_rows(s, ROPE_TILE)

    def body(*refs):
        heads = refs[:N_HEADS_IN]
        c_ref, sa_ref, sb_ref = refs[N_HEADS_IN:N_HEADS_IN + 3]
        outs = refs[N_HEADS_IN + 3:]
        for g, dil in enumerate(DILATIONS):
            rows = tm // dil
            for r in range(dil):
                rs = _residue(r, rows, dil)
                cos_t, sin_a, sin_b = c_ref[rs, :], sa_ref[rs, :], sb_ref[rs, :]
                for kind in range(3):
                    part = 3 * kind + g
                    for h in range(HEADS_PER_GROUP):
                        xh = heads[part * HEADS_PER_GROUP + h][rs, :]
                        if kind < 2:
                            xh = _rope_head(xh, cos_t, sin_a, sin_b)
                        outs[part][r, :, h * HEAD_DIM:(h + 1) * HEAD_DIM] = xh.astype(BF)

    tab = pl.BlockSpec((tm, HEAD_DIM), lambda i: (i, 0))
    head_specs = [pl.BlockSpec((tm, HEAD_DIM), lambda i, j=j: (i, j)) for j in range(N_HEADS_IN)]
    shapes, specs = [], []
    for part in range(N_PARTS):
        dil = DILATIONS[part % 3]
        shapes.append(jax.ShapeDtypeStruct((dil, s // dil, GROUP_W), BF))
        specs.append(pl.BlockSpec((dil, tm // dil, GROUP_W), lambda i: (0, i, 0)))
    return pl.pallas_call(
        body, out_shape=shapes, grid=(s // tm,), in_specs=head_specs + [tab, tab, tab], out_specs=specs,
        name="rope_fwd", compiler_params=_params(("parallel",)))(*([proj] * N_HEADS_IN), *tables)


def _rope_bwd(parts, tables, into):
    s = into.shape[0]
    tm = _pick_rows(s, ROPE_TILE)

    def body(*refs):
        ins = refs[:N_PARTS]
        c_ref, sa_ref, sb_ref, into_ref, o_ref, scr = refs[N_PARTS:]
        for g, dil in enumerate(DILATIONS):
            rows = tm // dil
            for r in range(dil):
                rs = _residue(r, rows, dil)
                cos_t, sin_a, sin_b = c_ref[rs, :], -sa_ref[rs, :], -sb_ref[rs, :]
                for kind in range(3):
                    part = 3 * kind + g
                    for h in range(HEADS_PER_GROUP):
                        xh = ins[part][r, :, h * HEAD_DIM:(h + 1) * HEAD_DIM]
                        if kind < 2:
                            xh = _rope_head(xh, cos_t, sin_a, sin_b)
                        scr[part * HEADS_PER_GROUP + h, rs, :] = xh
        for j in range(N_HEADS_IN):
            o_ref[:, j * HEAD_DIM:(j + 1) * HEAD_DIM] = scr[j].astype(BF)

    tab = pl.BlockSpec((tm, HEAD_DIM), lambda i: (i, 0))
    i_specs = [pl.BlockSpec((DILATIONS[p % 3], tm // DILATIONS[p % 3], GROUP_W), lambda i: (0, i, 0))
               for p in range(N_PARTS)]
    return pl.pallas_call(
        body, out_shape=jax.ShapeDtypeStruct(into.shape, into.dtype), grid=(s // tm,),
        in_specs=i_specs + [tab] * 3 + [ANY], out_specs=pl.BlockSpec((tm, N_PARTS * GROUP_W), lambda i: (i, 0)),
        scratch_shapes=[pltpu.VMEM((N_HEADS_IN, tm, HEAD_DIM), F32)], input_output_aliases={N_PARTS + 3: 0},
        name="rope_bwd", compiler_params=_params(("parallel",)))(*parts, *tables, into)


def _band_mask(n):
    qi = lax.broadcasted_iota(jnp.int32, (BLK, 2 * BLK), 0)
    ki = lax.broadcasted_iota(jnp.int32, (BLK, 2 * BLK), 1)
    prev = jnp.logical_and(jnp.logical_and(ki < BLK, ki >= qi), n > 0)
    return jnp.logical_or(prev, jnp.logical_and(ki >= BLK, qi >= ki - BLK))


def _dil_fwd(g, q, k, v):
    dil, length, _ = q.shape
    nb = length // BLK

    def body(q_ref, kc_ref, kp_ref, vc_ref, vp_ref, o_ref, lse_ref):
        mask = _band_mask(pl.program_id(1))
        for h in range(GROUP_W // HEAD_DIM):
            sl = slice(h * HEAD_DIM, (h + 1) * HEAD_DIM)
            keys = jnp.concatenate([kp_ref[:, sl], kc_ref[:, sl]], axis=0)
            vals = jnp.concatenate([vp_ref[:, sl], vc_ref[:, sl]], axis=0)
            sc = lax.dot_general(q_ref[:, sl], keys, NT_DIMS, preferred_element_type=F32) * SCALE
            sc = jnp.where(mask, sc, NEG_INF)
            mx = jnp.max(sc, axis=-1, keepdims=True)
            p = jnp.exp(sc - mx)
            den = jnp.sum(p, axis=-1, keepdims=True)
            o_ref[:, sl] = jnp.dot(p.astype(BF), vals, preferred_element_type=F32) / den
            lse_ref[:, sl] = jnp.broadcast_to(mx + jnp.log(den), (BLK, HEAD_DIM))

    cur = pl.BlockSpec((None, BLK, GROUP_W), lambda r, n: (r, n, 0))
    prev = pl.BlockSpec((None, BLK, GROUP_W), lambda r, n: (r, jnp.maximum(n - 1, 0), 0))
    return pl.pallas_call(
        body, out_shape=[jax.ShapeDtypeStruct(q.shape, F32)] * 2, grid=(dil, nb),
        in_specs=[cur, cur, prev, cur, prev], out_specs=[cur, cur], name=f"dil_fwd_{g}",
        compiler_params=_params(("parallel", "arbitrary")))(q, k, k, v, v)


def _dil_bwd(g, q, k, v, do, lse, delta):
    dil, length, _ = q.shape
    nb = length // BLK

    def body(q_ref, kc_ref, kp_ref, vc_ref, vp_ref, do_ref, lse_ref, dl_ref, dq_ref, dk_ref, dv_ref, ck_ref, cv_ref):
        n = pl.program_id(1)
        live = n < nb
        mask = jnp.logical_and(_band_mask(n), live)

        @pl.when(n == 0)
        def _():
            ck_ref[...] = jnp.zeros_like(ck_ref)
            cv_ref[...] = jnp.zeros_like(cv_ref)

        for h in range(GROUP_W // HEAD_DIM):
            sl = slice(h * HEAD_DIM, (h + 1) * HEAD_DIM)
            qh, doh = q_ref[:, sl], do_ref[:, sl]
            keys = jnp.concatenate([kp_ref[:, sl], kc_ref[:, sl]], axis=0)
            vals = jnp.concatenate([vp_ref[:, sl], vc_ref[:, sl]], axis=0)
            lse_h = lse_ref[:, h * HEAD_DIM:h * HEAD_DIM + 1]
            dl_h = dl_ref[:, h * HEAD_DIM:h * HEAD_DIM + 1]
            sc = lax.dot_general(qh, keys, NT_DIMS, preferred_element_type=F32) * SCALE
            p = jnp.where(mask, jnp.exp(jnp.minimum(sc - lse_h, 0.0)), 0.0)
            dp = lax.dot_general(doh, vals, NT_DIMS, preferred_element_type=F32)
            ds = (p * (dp - dl_h) * SCALE).astype(BF)
            dq = jnp.dot(ds, keys, preferred_element_type=F32)

            @pl.when(live)
            def _(dq=dq, sl=sl):
                dq_ref[:, sl] = dq

            dk = lax.dot_general(ds, qh, TN_DIMS, preferred_element_type=F32)
            dv = lax.dot_general(p.astype(BF), doh, TN_DIMS, preferred_element_type=F32)
            dk_ref[:, sl] = ck_ref[:, sl] + dk[:BLK]
            dv_ref[:, sl] = cv_ref[:, sl] + dv[:BLK]
            ck_ref[:, sl] = dk[BLK:]
            cv_ref[:, sl] = dv[BLK:]

    last = nb - 1
    cur = pl.BlockSpec((None, BLK, GROUP_W), lambda r, n: (r, jnp.minimum(n, last), 0))
    prev = pl.BlockSpec((None, BLK, GROUP_W), lambda r, n: (r, jnp.maximum(n - 1, 0), 0))
    return pl.pallas_call(
        body, out_shape=[jax.ShapeDtypeStruct(q.shape, F32)] * 3, grid=(dil, nb + 1),
        in_specs=[cur, cur, prev, cur, prev, cur, cur, cur], out_specs=[cur, prev, prev],
        scratch_shapes=[pltpu.VMEM((BLK, GROUP_W), F32)] * 2, name=f"dil_bwd_{g}",
        compiler_params=_params(("parallel", "arbitrary")))(q, k, k, v, v, do, lse, delta)


def _major_specs(s, tm, dtype):
    shapes = [jax.ShapeDtypeStruct((dil, s // dil, GROUP_W), dtype) for dil in DILATIONS]
    specs = [pl.BlockSpec((dil, tm // dil, GROUP_W), lambda i: (0, i, 0)) for dil in DILATIONS]
    return shapes, specs


def _attn_merge(outs, lses):
    s = outs[0].shape[1]
    tm = _pick_rows(s, ROPE_TILE)

    def body(o0, o1, o2, l0, l1, l2, m_ref, e0, e1, e2, so1, so2, sl1, sl2, se):
        for h in range(HEADS_PER_GROUP):
            sl = slice(h * HEAD_DIM, (h + 1) * HEAD_DIM)
            for dil, src, dst in ((DILATIONS[1], o1, so1), (DILATIONS[2], o2, so2), (DILATIONS[1], l1, sl1),
                                  (DILATIONS[2], l2, sl2)):
                for r in range(dil):
                    dst[h, _residue(r, tm // dil, dil), :] = src[r, :, sl]
            a, b, c = l0[0, :, sl], sl1[h], sl2[h]
            mx = jnp.maximum(jnp.maximum(a, b), c)
            ea, eb, ec = jnp.exp(a - mx), jnp.exp(b - mx), jnp.exp(c - mx)
            den = ea + eb + ec
            m_ref[:, sl] = (ea * o0[0, :, sl] + eb * so1[h] + ec * so2[h]) / den
            se[h] = mx + jnp.log(den)
            for dil, dst in zip(DILATIONS, (e0, e1, e2)):
                for r in range(dil):
                    dst[r, :, sl] = se[h, _residue(r, tm // dil, dil), :]

    shapes, specs = _major_specs(s, tm, F32)
    nat = pl.BlockSpec((tm, GROUP_W), lambda i: (i, 0))
    res = pl.pallas_call(
        body, out_shape=[jax.ShapeDtypeStruct((s, GROUP_W + MEM_W), F32)] + shapes, grid=(s // tm,), in_specs=specs * 2,
        out_specs=[nat] + specs, scratch_shapes=[pltpu.VMEM((HEADS_PER_GROUP, tm, HEAD_DIM), F32)] * 5,
        name="attn_merge", compiler_params=_params(("parallel",)))(*outs, *lses)
    return res[0], res[1:]


def _attn_delta(dcat, merged, after=()):
    s = merged.shape[0]
    tm = _pick_rows(s, ROPE_TILE)

    def body(*refs):
        d_refs, m_ref = refs[:HEADS_PER_GROUP], refs[HEADS_PER_GROUP]
        do_refs, dl_refs, scr = refs[-7:-4], refs[-4:-1], refs[-1]
        for h in range(HEADS_PER_GROUP):
            sl = slice(h * HEAD_DIM, (h + 1) * HEAD_DIM)
            prod = d_refs[h][...] * m_ref[:, sl]
            scr[h] = jnp.broadcast_to(jnp.sum(prod, axis=-1, keepdims=True), (tm, HEAD_DIM))
            for dil, do_ref, dl_ref in zip(DILATIONS, do_refs, dl_refs):
                for r in range(dil):
                    rs = _residue(r, tm // dil, dil)
                    do_ref[r, :, sl] = d_refs[h][rs, :].astype(BF)
                    dl_ref[r, :, sl] = scr[h, rs, :]

    nat = pl.BlockSpec((tm, GROUP_W), lambda i: (i, 0))
    head_specs = [pl.BlockSpec((tm, HEAD_DIM), lambda i, h=h: (i, h)) for h in range(HEADS_PER_GROUP)]
    bf_shapes, specs = _major_specs(s, tm, BF)
    f_shapes, _ = _major_specs(s, tm, F32)
    res = pl.pallas_call(
        body, out_shape=bf_shapes + f_shapes, grid=(s // tm,), in_specs=head_specs + [nat] + [ANY] * len(after),
        out_specs=specs * 2, scratch_shapes=[pltpu.VMEM((HEADS_PER_GROUP, tm, HEAD_DIM), F32)], name="attn_delta",
        compiler_params=_params(("parallel",)))(*([dcat] * HEADS_PER_GROUP), merged, *after)
    return res[:3], res[3:]


def _mem_probs(qh, kh):
    sc = lax.dot_general(qh, kh, NT_DIMS, preferred_element_type=F32) * SCALE
    p = jnp.exp(sc - jnp.max(sc, axis=-1, keepdims=True))
    return p, jnp.sum(p, axis=-1, keepdims=True)


def _mem_fwd(name, proj, q_block, kv, into, out_block):
    s = proj.shape[0]
    tq = _pick_rows(s, 512)

    def body(q_ref, kv_ref, into_ref, o_ref):
        for h in range(MEM_HEADS):
            sl = slice(h * HEAD_DIM, (h + 1) * HEAD_DIM)
            vsl = slice(MEM_W + h * HEAD_DIM, MEM_W + (h + 1) * HEAD_DIM)
            p, den = _mem_probs(q_ref[:, sl].astype(BF), kv_ref[:, sl].astype(BF))
            o_ref[:, sl] = jnp.dot(p.astype(BF), kv_ref[:, vsl].astype(BF), preferred_element_type=F32) / den

    return pl.pallas_call(
        body, out_shape=jax.ShapeDtypeStruct(into.shape, into.dtype), grid=(s // tq,),
        in_specs=[pl.BlockSpec((tq, MEM_W), lambda i: (i, q_block)), pl.BlockSpec(kv.shape, lambda i: (0, 0)), ANY],
        out_specs=pl.BlockSpec((tq, MEM_W), lambda i: (i, out_block)), input_output_aliases={2: 0}, name=name,
        compiler_params=_params(("parallel",)))(proj, kv, into)


def _mem_bwd(name, proj, q_block, kv, dcat, d_block, width):
    s = proj.shape[0]
    tq = _pick_rows(s, 512)

    def body(q_ref, kv_ref, do_ref, dq_ref, dkv_ref):
        @pl.when(pl.program_id(0) == 0)
        def _():
            dkv_ref[...] = jnp.zeros_like(dkv_ref)

        for h in range(MEM_HEADS):
            sl = slice(h * HEAD_DIM, (h + 1) * HEAD_DIM)
            vsl = slice(MEM_W + h * HEAD_DIM, MEM_W + (h + 1) * HEAD_DIM)
            qh, kh, vh = q_ref[:, sl].astype(BF), kv_ref[:, sl].astype(BF), kv_ref[:, vsl].astype(BF)
            doh = do_ref[:, sl].astype(BF)
            p, den = _mem_probs(qh, kh)
            p = p / den
            dp = lax.dot_general(doh, vh, NT_DIMS, preferred_element_type=F32)
            ds = (p * (dp - jnp.sum(p * dp, axis=-1, keepdims=True)) * SCALE).astype(BF)
            dq_ref[:, sl] = jnp.dot(ds, kh, preferred_element_type=F32).astype(BF)
            dkv_ref[:, sl] += lax.dot_general(ds, qh, TN_DIMS, preferred_element_type=F32)
            dkv_ref[:, vsl] += lax.dot_general(p.astype(BF), doh, TN_DIMS, preferred_element_type=F32)

    whole = pl.BlockSpec(kv.shape, lambda i: (0, 0))
    return pl.pallas_call(
        body, out_shape=[jax.ShapeDtypeStruct((s, width), BF), jax.ShapeDtypeStruct(kv.shape, F32)], grid=(s // tq,),
        in_specs=[pl.BlockSpec((tq, MEM_W), lambda i: (i, q_block)), whole,
                  pl.BlockSpec((tq, MEM_W), lambda i: (i, d_block))],
        out_specs=[pl.BlockSpec((tq, MEM_W), lambda i: (i, width // MEM_W - 1)), whole], name=name,
        compiler_params=_params(("arbitrary",)))(proj, kv, dcat)


def _causal():
    t = lax.broadcasted_iota(jnp.int32, (BLK, BLK), 0)
    s = lax.broadcasted_iota(jnp.int32, (BLK, BLK), 1)
    return t >= s


def _sgu_norm(v_pre, ln_g, ln_b):
    vg = _gelu(v_pre)
    mu = jnp.mean(vg, axis=-1, keepdims=True)
    cen = vg - mu
    rstd = lax.rsqrt(jnp.mean(cen * cen, axis=-1, keepdims=True) + LN_EPS)
    xhat = cen * rstd
    return xhat, rstd, xhat * ln_g + ln_b


def _sgu_fwd(proj, ln_g, ln_b, w_sp, b_t):
    s = proj.shape[0]

    def body(u_ref, v_ref, g_ref, b_ref, w_ref, bt_ref, o_ref):
        _, _, vn = _sgu_norm(v_ref[...], g_ref[...], b_ref[...])
        vn = vn.astype(BF)
        tri = _causal()
        for grp in range(SGU_GROUPS):
            sl = slice(grp * HEAD_DIM, (grp + 1) * HEAD_DIM)
            w = jnp.where(tri, w_ref[grp], 0.0).astype(BF)
            mixed = jnp.dot(w, vn[:, sl], preferred_element_type=F32) + bt_ref[:, grp:grp + 1]
            o_ref[:, sl] = _gelu(u_ref[:, sl]) * mixed

    vec = pl.BlockSpec((1, SGU_W), lambda i: (0, 0))
    return pl.pallas_call(
        body, out_shape=jax.ShapeDtypeStruct((s, SGU_W + MEM_W), F32), grid=(s // BLK,),
        in_specs=[pl.BlockSpec((BLK, SGU_W), lambda i: (i, 0)), pl.BlockSpec((BLK, SGU_W), lambda i: (i, 1)), vec, vec,
                  pl.BlockSpec(w_sp.shape, lambda i: (0, 0, 0)), pl.BlockSpec(b_t.shape, lambda i: (0, 0))],
        out_specs=pl.BlockSpec((BLK, SGU_W), lambda i: (i, 0)), name="sgu_fwd",
        compiler_params=_params(("parallel",)))(proj, proj, ln_g, ln_b, w_sp, b_t)


def _sgu_bwd(proj, dcat, ln_g, ln_b, w_sp, b_t, into):
    s = proj.shape[0]

    def body(u_ref, v_ref, d_ref, g_ref, b_ref, w_ref, bt_ref, into_ref, dp_ref, dw_ref, db_ref, dg_ref, dbeta_ref,
             dvn_ref):
        @pl.when(pl.program_id(0) == 0)
        def _():
            dw_ref[...] = jnp.zeros_like(dw_ref)
            db_ref[...] = jnp.zeros_like(db_ref)
            dg_ref[...] = jnp.zeros_like(dg_ref)
            dbeta_ref[...] = jnp.zeros_like(dbeta_ref)

        v_pre = v_ref[...]
        gain = g_ref[...]
        xhat, rstd, vn = _sgu_norm(v_pre, gain, b_ref[...])
        vn = vn.astype(BF)
        tri = _causal()
        lane = lax.broadcasted_iota(jnp.int32, (BLK, HEAD_DIM), 1)
        db_acc = jnp.zeros((BLK, HEAD_DIM), F32)
        for grp in range(SGU_GROUPS):
            sl = slice(grp * HEAD_DIM, (grp + 1) * HEAD_DIM)
            w = jnp.where(tri, w_ref[grp], 0.0).astype(BF)
            vn_g = vn[:, sl]
            mixed = jnp.dot(w, vn_g, preferred_element_type=F32) + bt_ref[:, grp:grp + 1]
            u_pre = u_ref[:, sl]
            d_out = d_ref[:, sl]
            dp_ref[:, sl] = (d_out * mixed * _gelu_grad(u_pre)).astype(BF)
            dmixed = d_out * _gelu(u_pre)
            dm = dmixed.astype(BF)
            dvn_ref[:, sl] = lax.dot_general(w, dm, TN_DIMS, preferred_element_type=F32)
            dw = lax.dot_general(dm, vn_g, NT_DIMS, preferred_element_type=F32)
            dw_ref[grp] += jnp.where(tri, dw, 0.0)
            db_acc += jnp.where(lane == grp, jnp.sum(dmixed, axis=-1, keepdims=True), 0.0)
        db_ref[...] += db_acc
        dvn = dvn_ref[...]
        dg_ref[...] += jnp.sum(dvn * xhat, axis=0, keepdims=True)
        dbeta_ref[...] += jnp.sum(dvn, axis=0, keepdims=True)
        dxh = dvn * gain
        dvg = rstd * (dxh - jnp.mean(dxh, axis=-1, keepdims=True) - xhat * jnp.mean(dxh * xhat, axis=-1, keepdims=True))
        dp_ref[:, SGU_W:] = (dvg * _gelu_grad(v_pre)).astype(BF)

    vec = pl.BlockSpec((1, SGU_W), lambda i: (0, 0))
    row = pl.BlockSpec((BLK, SGU_W), lambda i: (i, 0))
    w_spec = pl.BlockSpec(w_sp.shape, lambda i: (0, 0, 0))
    sq = pl.BlockSpec((BLK, HEAD_DIM), lambda i: (0, 0))
    return pl.pallas_call(
        body,
        out_shape=[jax.ShapeDtypeStruct(into.shape, into.dtype),
                   jax.ShapeDtypeStruct(w_sp.shape, F32), jax.ShapeDtypeStruct((BLK, HEAD_DIM), F32),
                   jax.ShapeDtypeStruct((1, SGU_W), F32), jax.ShapeDtypeStruct((1, SGU_W), F32)],
        grid=(s // BLK,),
        in_specs=[row, pl.BlockSpec((BLK, SGU_W), lambda i: (i, 1)), row, vec, vec, w_spec,
                  pl.BlockSpec(b_t.shape, lambda i: (0, 0)), ANY],
        out_specs=[pl.BlockSpec((BLK, 2 * SGU_W), lambda i: (i, 0)), w_spec, sq, vec, vec],
        scratch_shapes=[pltpu.VMEM((BLK, SGU_W), F32)], input_output_aliases={7: 0}, name="sgu_bwd",
        compiler_params=_params(("arbitrary",)))(proj, proj, dcat, ln_g, ln_b, w_sp, b_t, into)


def _place():
    return lax.axis_index("x"), lax.axis_index("y"), lax.axis_index("c")


def _other_chips(x, y):
    return [(1 - x, y), (x, 1 - y), (1 - x, 1 - y)]


def _peer(x, y, c, mask):
    return (1 - x if mask & 4 else x, 1 - y if mask & 2 else y, 1 - c if mask & 1 else c)


def _in_hbm(a):
    return pltpu.with_memory_space_constraint(a, pltpu.HBM)


def _token_spec():
    return jax.ShapeDtypeStruct((8, LANES), F32), pl.BlockSpec(memory_space=pltpu.VMEM)


def _remote(src, dst, ssem, rsem, to):
    return pltpu.make_async_remote_copy(src_ref=src, dst_ref=dst, send_sem=ssem, recv_sem=rsem, device_id=to,
                                        device_id_type=MESH)


def _place_shard(name, src, layer, place, dtype, after=()):
    _, rows, cols = src.shape
    tr = _pick_rows(rows, 512)

    def body(p_ref, s_ref, *rest):
        rest[-1][...] = s_ref[...].astype(dtype)

    grid_spec = pltpu.PrefetchScalarGridSpec(
        num_scalar_prefetch=1, grid=(rows // tr,),
        in_specs=[pl.BlockSpec((None, tr, cols), lambda i, p: (layer, i, 0))] + [ANY] * len(after),
        out_specs=pl.BlockSpec((None, tr, cols), lambda i, p: (p[1], i, 0)))
    return pl.pallas_call(body, out_shape=jax.ShapeDtypeStruct((N_CHIPS, rows, cols), dtype), grid_spec=grid_spec,
                          name=name, compiler_params=_params(("parallel",)))(place, src, *after)


def _gather_copies(bufs, ssem, rsem):
    x, y, c = _place()
    me = 2 * x + y
    copies = []
    for ai, buf in enumerate(bufs):
        for k, (ox, oy) in enumerate(_other_chips(x, y)):
            copies.append(_remote(buf.at[me], buf.at[me], ssem.at[3 * ai + k], rsem.at[3 * ai + k], (ox, oy, c)))
    return copies


def _reduce_copies(grads, lands, ssem, rsem):
    x, y, c = _place()
    copies = []
    for a, (gr, land) in enumerate(zip(grads, lands)):
        for mask in range(1, N_DEV):
            px, py, pc = _peer(x, y, c, mask)
            copies.append(_remote(gr.at[pc, 2 * px + py], land.at[mask - 1], ssem.at[7 * a + mask - 1],
                                  rsem.at[7 * a + mask - 1], (px, py, pc)))
    return copies


def _half_copies(totals, ssem, rsem):
    x, y, c = _place()
    return [_remote(t.at[c], t.at[c], ssem.at[a], rsem.at[a], (x, y, 1 - c)) for a, t in enumerate(totals)]


def _gather_start(name, groups):
    flat = [s for grp in groups for s in grp]
    n, ng = len(flat), len(groups)

    def body(*refs):
        ins = refs[:n]
        sems = refs[n:n + 2 * ng]
        token = refs[-1]
        idx = 0
        for gi, grp in enumerate(groups):
            for cp in _gather_copies(ins[idx:idx + len(grp)], sems[2 * gi], sems[2 * gi + 1]):
                cp.start()
            idx += len(grp)
        token[...] = jnp.zeros_like(token)

    tok_shape, tok_spec = _token_spec()
    sem_shapes = []
    for grp in groups:
        sem_shapes += [pltpu.SemaphoreType.DMA((3 * len(grp),))] * 2
    res = pl.pallas_call(
        body, name=name,
        out_shape=(*sem_shapes, *[pltpu.HBM(s.shape, s.dtype) for s in flat], tok_shape),
        in_specs=[HBM] * n, out_specs=(*[SEM] * (2 * ng), *[HBM] * n, tok_spec),
        input_output_aliases={i: 2 * ng + i for i in range(n)},
        compiler_params=pltpu.CompilerParams(has_side_effects=EFFECT))(*[_in_hbm(s) for s in flat])
    out, idx = [], 2 * ng
    for gi, grp in enumerate(groups):
        out.append((res[2 * gi], res[2 * gi + 1], list(res[idx:idx + len(grp)])))
        idx += len(grp)
    return out, res[-1]


def _gather_wait(name, ssem, rsem, slabs, after):
    n = len(slabs)

    def body(*refs):
        for cp in _gather_copies(refs[:n], refs[n], refs[n + 1]):
            cp.wait_send()
            cp.wait_recv()

    return pl.pallas_call(
        body, name=name, out_shape=tuple(pltpu.HBM(s.shape, s.dtype) for s in slabs),
        in_specs=[HBM] * n + [SEM, SEM] + [ANY] * len(after), out_specs=tuple([HBM] * n),
        input_output_aliases={i: i for i in range(n)},
        compiler_params=pltpu.CompilerParams(has_side_effects=EFFECT))(*slabs, ssem, rsem, *after)


def _reduce_start(name, grads):
    n = len(grads)
    lands = [lax.empty((N_DEV - 1, *g.shape[2:]), g.dtype) for g in grads]

    def body(*refs):
        token = refs[-1]
        for cp in _reduce_copies(refs[:n], refs[n:2 * n], refs[2 * n], refs[2 * n + 1]):
            cp.start()
        token[...] = jnp.zeros_like(token)

    tok_shape, tok_spec = _token_spec()
    sems = [pltpu.SemaphoreType.DMA((7 * n,))] * 2
    res = pl.pallas_call(
        body, name=name,
        out_shape=(*sems, *[pltpu.HBM(g.shape, g.dtype) for g in grads], *[pltpu.HBM(l.shape, l.dtype) for l in lands],
                   tok_shape),
        in_specs=[HBM] * (2 * n), out_specs=(SEM, SEM, *[HBM] * (2 * n), tok_spec),
        input_output_aliases={i: 2 + i for i in range(2 * n)},
        compiler_params=pltpu.CompilerParams(has_side_effects=EFFECT))(*[_in_hbm(t) for t in (*grads, *lands)])
    return res[0], res[1], list(res[2:2 + n]), list(res[2 + n:2 + 2 * n]), res[-1]


def _reduce_wait(name, ssem, rsem, grads, lands, after):
    n = len(grads)

    def body(*refs):
        for cp in _reduce_copies(refs[:n], refs[n:2 * n], refs[2 * n], refs[2 * n + 1]):
            cp.wait_send()
            cp.wait_recv()

    res = pl.pallas_call(
        body, name=name, out_shape=tuple(pltpu.HBM(t.shape, t.dtype) for t in (*grads, *lands)),
        in_specs=[HBM] * (2 * n) + [SEM, SEM] + [ANY] * len(after), out_specs=tuple([HBM] * (2 * n)),
        input_output_aliases={i: i for i in range(2 * n)},
        compiler_params=pltpu.CompilerParams(has_side_effects=EFFECT))(*grads, *lands, ssem, rsem, *after)
    return list(res[:n]), list(res[n:])


def _sum_pieces(name, grad, land, place):
    _, _, rows, cols = grad.shape
    tr = _pick_rows(rows, 256)

    def body(p_ref, g_ref, l_ref, o_ref):
        tot = g_ref[...].astype(F32)
        for k in range(N_DEV - 1):
            tot = tot + l_ref[k].astype(F32)
        o_ref[...] = tot

    grid_spec = pltpu.PrefetchScalarGridSpec(
        num_scalar_prefetch=1, grid=(rows // tr,),
        in_specs=[pl.BlockSpec((None, None, tr, cols), lambda i, p: (p[0], p[1], i, 0)),
                  pl.BlockSpec((N_DEV - 1, tr, cols), lambda i, p: (0, i, 0))],
        out_specs=pl.BlockSpec((None, tr, cols), lambda i, p: (p[0], i, 0)))
    return pl.pallas_call(body, out_shape=jax.ShapeDtypeStruct((2, rows, cols), F32), grid_spec=grid_spec, name=name,
                          compiler_params=_params(("parallel",)))(place, grad, land)


def _half_start(name, totals):
    n = len(totals)

    def body(*refs):
        token = refs[-1]
        for cp in _half_copies(refs[:n], refs[n], refs[n + 1]):
            cp.start()
        token[...] = jnp.zeros_like(token)

    tok_shape, tok_spec = _token_spec()
    res = pl.pallas_call(
        body, name=name,
        out_shape=(pltpu.SemaphoreType.DMA((n,)), pltpu.SemaphoreType.DMA((n,)),
                   *[pltpu.HBM(t.shape, t.dtype) for t in totals], tok_shape),
        in_specs=[HBM] * n, out_specs=(SEM, SEM, *[HBM] * n, tok_spec),
        input_output_aliases={i: 2 + i for i in range(n)},
        compiler_params=pltpu.CompilerParams(has_side_effects=EFFECT))(*[_in_hbm(t) for t in totals])
    return res[0], res[1], list(res[2:2 + n]), res[-1]


def _half_wait(name, ssem, rsem, totals, after):
    n = len(totals)

    def body(*refs):
        for cp in _half_copies(refs[:n], refs[n], refs[n + 1]):
            cp.wait_send()
            cp.wait_recv()

    res = pl.pallas_call(
        body, name=name, out_shape=tuple(pltpu.HBM(t.shape, t.dtype) for t in totals),
        in_specs=[HBM] * n + [SEM, SEM] + [ANY] * len(after), out_specs=tuple([HBM] * n),
        input_output_aliases={i: i for i in range(n)},
        compiler_params=pltpu.CompilerParams(has_side_effects=EFFECT))(*totals, ssem, rsem, *after)
    return list(res)


def _small_copies(bufs, ssem, rsem):
    x, y, c = _place()
    mine = bufs[0].at[4 * x + 2 * y + c]
    return [_remote(mine, mine, ssem.at[mask - 1], rsem.at[mask - 1], _peer(x, y, c, mask)) for mask in range(1, N_DEV)]


def _small_start(name, slots):
    def body(s_ref, ssem, rsem, thru, token):
        for cp in _small_copies([s_ref], ssem, rsem):
            cp.start()
        token[...] = jnp.zeros_like(token)

    tok_shape, tok_spec = _token_spec()
    sems = [pltpu.SemaphoreType.DMA((N_DEV - 1,))] * 2
    return pl.pallas_call(
        body, name=name, out_shape=(*sems, pltpu.HBM(slots.shape, slots.dtype), tok_shape), in_specs=[HBM],
        out_specs=(SEM, SEM, HBM, tok_spec), input_output_aliases={0: 2},
        compiler_params=pltpu.CompilerParams(has_side_effects=EFFECT))(_in_hbm(slots))


def _small_wait(name, ssem, rsem, slots, after):
    def body(*refs):
        for cp in _small_copies([refs[0]], refs[1], refs[2]):
            cp.wait_send()
            cp.wait_recv()

    return pl.pallas_call(
        body, name=name, out_shape=pltpu.HBM(slots.shape, slots.dtype), in_specs=[HBM, SEM, SEM] + [ANY] * len(after),
        out_specs=HBM, input_output_aliases={0: 0},
        compiler_params=pltpu.CompilerParams(has_side_effects=EFFECT))(slots, ssem, rsem, *after)


def _own_slot(small, me):
    return lax.dynamic_update_slice(jnp.zeros((N_DEV, *small.shape), small.dtype), small[None], (me, 0, 0))


def _sum_devices(name, stacked):
    _, rows, lanes = stacked.shape
    tr = _pick_rows(rows, 512)

    def body(s_ref, o_ref):
        tot = s_ref[0]
        for k in range(1, N_DEV):
            tot = tot + s_ref[k]
        o_ref[...] = tot

    return pl.pallas_call(
        body, out_shape=jax.ShapeDtypeStruct((rows, lanes), F32), grid=(rows // tr,),
        in_specs=[pl.BlockSpec((N_DEV, tr, lanes), lambda i: (0, i, 0))], out_specs=pl.BlockSpec((tr, lanes), lambda i: (i, 0)),
        name=name, compiler_params=_params(("parallel",)))(stacked)


def _adamw(name, w, g, m, v, layer, prev=None):
    layers, rows, cols = w.shape
    tr = _pick_rows(rows, 256)
    c1 = 1.0 - ADAM_B1 ** ADAM_STEP
    c2 = 1.0 - ADAM_B2 ** ADAM_STEP

    def body(w_ref, g_ref, m_ref, v_ref, *rest):
        go_ref, d_ref, nm_ref, nv_ref = rest[-4:]
        gv = g_ref[...]
        nm = ADAM_B1 * m_ref[...] + (1.0 - ADAM_B1) * gv
        nv = ADAM_B2 * v_ref[...] + (1.0 - ADAM_B2) * (gv * gv)
        go_ref[...] = gv
        d_ref[...] = -ADAM_LR * ((nm / c1) / (jnp.sqrt(nv / c2) + ADAM_EPS) + ADAM_WD * w_ref[...])
        nm_ref[...] = nm
        nv_ref[...] = nv

    spec = pl.BlockSpec((None, tr, cols), lambda i: (layer, i, 0))
    prev = list(prev) if prev is not None else []
    return pl.pallas_call(
        body, out_shape=[jax.ShapeDtypeStruct((layers, rows, cols), F32)] * 4, grid=(rows // tr,),
        in_specs=[spec, pl.BlockSpec((tr, cols), lambda i: (i, 0)), spec, spec] + [ANY] * len(prev),
        out_specs=[spec] * 4, input_output_aliases={4 + i: i for i in range(len(prev))}, name=name,
        compiler_params=_params(("parallel",)))(w, g, m, v, *prev)


def _pack(vectors, pad_rows):
    flat = jnp.concatenate([t.reshape(-1) for t in vectors])
    rows = -(-flat.shape[0] // LANES)
    rows = -(-rows // pad_rows) * pad_rows
    return jnp.pad(flat, (0, rows * LANES - flat.shape[0])).reshape(rows, LANES)


def _unpack(packed, shapes):
    flat = packed.reshape(-1)
    out, off = [], 0
    for shp in shapes:
        size = math.prod(shp)
        out.append(flat[off:off + size].reshape(shp))
        off += size
    return out


def kernel(x, mem, positions, mix_norm, mem_norm, w_mem_kv, ffn_norm, w_gate, w_up, w_down, attn_w_in, attn_w_out, sgu_w_in, sgu_ln_g, sgu_ln_b, sgu_w_spatial, sgu_b_spatial, sgu_w_out, final_norm, loss_target, m_mix_norm, m_mem_norm, m_w_mem_kv, m_ffn_norm, m_w_gate, m_w_up, m_w_down, m_attn_w_in, m_attn_w_out, m_sgu_w_in, m_sgu_ln_g, m_sgu_ln_b, m_sgu_w_spatial, m_sgu_b_spatial, m_sgu_w_out, m_final_norm, v_mix_norm, v_mem_norm, v_w_mem_kv, v_ffn_norm, v_w_gate, v_w_up, v_w_down, v_attn_w_in, v_attn_w_out, v_sgu_w_in, v_sgu_ln_g, v_sgu_ln_b, v_sgu_w_spatial, v_sgu_b_spatial, v_sgu_w_out, v_final_norm):
    d_model = x.shape[2]
    x0, mem0, tgt = x[0], mem[0], loss_target[0]
    xi, yi, ci = _place()
    chip = 2 * xi + yi
    place = jnp.stack([ci, chip]).astype(jnp.int32)

    given_w = dict(mix_norm=mix_norm, mem_norm=mem_norm, w_mem_kv=w_mem_kv, ffn_norm=ffn_norm, w_gate=w_gate, w_up=w_up,
                   w_down=w_down, attn_w_in=attn_w_in, attn_w_out=attn_w_out, sgu_w_in=sgu_w_in, sgu_ln_g=sgu_ln_g,
                   sgu_ln_b=sgu_ln_b, sgu_w_spatial=sgu_w_spatial, sgu_b_spatial=sgu_b_spatial, sgu_w_out=sgu_w_out,
                   final_norm=final_norm)
    given_m = dict(mix_norm=m_mix_norm, mem_norm=m_mem_norm, w_mem_kv=m_w_mem_kv, ffn_norm=m_ffn_norm, w_gate=m_w_gate,
                   w_up=m_w_up, w_down=m_w_down, attn_w_in=m_attn_w_in, attn_w_out=m_attn_w_out, sgu_w_in=m_sgu_w_in,
                   sgu_ln_g=m_sgu_ln_g, sgu_ln_b=m_sgu_ln_b, sgu_w_spatial=m_sgu_w_spatial,
                   sgu_b_spatial=m_sgu_b_spatial, sgu_w_out=m_sgu_w_out, final_norm=m_final_norm)
    given_v = dict(mix_norm=v_mix_norm, mem_norm=v_mem_norm, w_mem_kv=v_w_mem_kv, ffn_norm=v_ffn_norm, w_gate=v_w_gate,
                   w_up=v_w_up, w_down=v_w_down, attn_w_in=v_attn_w_in, attn_w_out=v_attn_w_out, sgu_w_in=v_sgu_w_in,
                   sgu_ln_g=v_sgu_ln_g, sgu_ln_b=v_sgu_ln_b, sgu_w_spatial=v_sgu_w_spatial,
                   sgu_b_spatial=v_sgu_b_spatial, sgu_w_out=v_sgu_w_out, final_norm=v_final_norm)

    units = {"attn_w_in": ("attn_w_in", 0, "col"), "w_mem_kv0": ("w_mem_kv", 0, "row"), "attn_w_out": ("attn_w_out", 0, "col"),
             "w_gate0": ("w_gate", 0, "col"), "w_up0": ("w_up", 0, "col"), "w_down0": ("w_down", 0, "row"),
             "sgu_w_in": ("sgu_w_in", 0, "col"), "w_mem_kv1": ("w_mem_kv", 1, "row"), "sgu_w_out": ("sgu_w_out", 0, "row"),
             "w_gate1": ("w_gate", 1, "col"), "w_up1": ("w_up", 1, "col"), "w_down1": ("w_down", 1, "row")}
    gather_groups = [["attn_w_in"], ["w_mem_kv0", "attn_w_out"], ["w_gate0", "w_up0"],
                     ["w_down0", "sgu_w_in", "w_mem_kv1", "ln"], ["sgu_w_out", "w_gate1", "w_up1"], ["w_down1"]]

    first = _place_shard("place_attn_w_in", attn_w_in, 0, place, BF)
    in_flight, token = _gather_start("gather_start_0", [[first]])
    slabs = {u: _place_shard(f"place_{u}", given_w[arr], layer, place, BF, after=[token])
             for u, (arr, layer, _) in units.items() if u != "attn_w_in"}
    slabs["ln"] = _place_shard("place_ln", jnp.concatenate([sgu_ln_g, sgu_ln_b])[None], 0, place, F32, after=[token])
    rest, token = _gather_start("gather_start_1", [[slabs[u] for u in grp] for grp in gather_groups[1:]])
    in_flight += rest
    weights = {}

    def arrive(gi, after):
        ssem, rsem, arrs = in_flight[gi]
        for u, full in zip(gather_groups[gi], _gather_wait(f"gather_wait_{gi}", ssem, rsem, arrs, after)):
            weights[u] = full if u == "ln" else Weight(full, units[u][2])

    w_sp = sgu_w_spatial[0]
    b_t = sgu_b_spatial[0].T
    tables = _rope_tables(positions[0])

    def residual(acc, extra):
        return [extra[0] + acc[0]]

    def memory_kv(layer):
        mem_n = _rms_fwd(f"mem_norm_{layer}", mem0, mem_norm[layer:layer + 1])
        return mem_n, _mm_nn(f"mem_kv_{layer}", mem_n, weights[f"w_mem_kv{layer}"])[0]

    h0 = _rms_fwd("mix_norm_0", x0, mix_norm[0:1], after=[token])
    arrive(0, [h0])
    proj0 = _mm_nn("attn_in", h0, weights["attn_w_in"])[0]
    arrive(1, [proj0])
    qkv = _rope_fwd(proj0, tables)
    qs, ks, vs = qkv[0:3], qkv[3:6], qkv[6:9]
    outs, lses = [], []
    for g in range(len(DILATIONS)):
        o, l = _dil_fwd(g, qs[g], ks[g], vs[g])
        outs.append(o)
        lses.append(l)
    merged, lse = _attn_merge(outs, lses)
    mem_n0, kv0 = memory_kv(0)
    cat0 = _mem_fwd("mem_fwd_0", proj0, 9, kv0, merged, 1)
    x1 = _mm_nn("attn_out", cat0, weights["attn_w_out"], extras=[x0], epilogue=residual)[0]
    arrive(2, [x1])
    hf0 = _rms_fwd("ffn_norm_0", x1, ffn_norm[0:1])
    g0, u0, act0 = _gate_up("gate_up_0", hf0, weights["w_gate0"], weights["w_up0"])
    arrive(3, [act0])
    x2 = _mm_nn("down_0", act0, weights["w_down0"], extras=[x1], epilogue=residual)[0]

    ln_all = weights["ln"]
    ln_g = ln_all[:, 0, :].reshape(1, SGU_W)
    ln_b = ln_all[:, 1, :].reshape(1, SGU_W)
    h1 = _rms_fwd("mix_norm_1", x2, mix_norm[1:2])
    proj1 = _mm_nn("sgu_in", h1, weights["sgu_w_in"])[0]
    arrive(4, [proj1])
    sgu_out = _sgu_fwd(proj1, ln_g, ln_b, w_sp, b_t)
    mem_n1, kv1 = memory_kv(1)
    cat1 = _mem_fwd("mem_fwd_1", proj1, 6, kv1, sgu_out, 3)
    x3 = _mm_nn("sgu_out", cat1, weights["sgu_w_out"], extras=[x2], epilogue=residual)[0]
    hf1 = _rms_fwd("ffn_norm_1", x3, ffn_norm[1:2])
    g1, u1, act1 = _gate_up("gate_up_1", hf1, weights["w_gate1"], weights["w_up1"])
    arrive(5, [act1])
    x4 = _mm_nn("down_1", act1, weights["w_down1"], extras=[x3], epilogue=residual)[0]

    d4, g_final, loss_part = _final_loss(x4, tgt, final_norm.reshape(1, d_model))
    loss = lax.psum(loss_part[0, 0], ("x", "y", "c"))

    outputs = {}

    def start_reduce(tag, names, grads):
        ssem, rsem, grads, lands, tok = _reduce_start(f"reduce_start_{tag}", grads)
        return dict(tag=tag, names=names, ssem=ssem, rsem=rsem, grads=grads, lands=lands), tok

    def finish_reduce(st, after):
        grads, lands = _reduce_wait(f"reduce_wait_{st['tag']}", st["ssem"], st["rsem"], st["grads"], st["lands"], after)
        totals = [_sum_pieces(f"sum_{u}", g, l, place) for u, g, l in zip(st["names"], grads, lands)]
        ssem, rsem, totals, tok = _half_start(f"half_start_{st['tag']}", totals)
        return dict(tag=st["tag"], names=st["names"], ssem=ssem, rsem=rsem, totals=totals), tok

    def finish_update(st, after):
        totals = _half_wait(f"half_wait_{st['tag']}", st["ssem"], st["rsem"], st["totals"], after)
        for u, tot in zip(st["names"], totals):
            arr, layer, _ = units[u]
            w = given_w[arr]
            outputs[arr] = _adamw(f"adamw_{u}", w, tot.reshape(w.shape[1:]), given_m[arr], given_v[arr], layer,
                                  outputs.get(arr))

    def ffn_bwd(layer, d_out, xin, h, g, u, act):
        wd, wg, wu = weights[f"w_down{layer}"], weights[f"w_gate{layer}"], weights[f"w_up{layer}"]
        gr_down = _mm_tn(f"d_down_{layer}", act, d_out, wd)
        dg, du = _mm_nt(f"d_act_{layer}", [d_out], [wd], out_dtypes=(BF, BF), extras=[g, u], epilogue=_swiglu_bwd_epilogue,
                        col_chunk=EPILOGUE_CHUNK)
        gr_gate = _mm_tn(f"d_gate_{layer}", h, dg, wg)
        gr_up = _mm_tn(f"d_up_{layer}", h, du, wu)
        st, tok = start_reduce(f"ffn{layer}", [f"w_down{layer}", f"w_gate{layer}", f"w_up{layer}"], [gr_down, gr_gate, gr_up])
        dh = _mm_nt(f"d_ffn_h_{layer}", [dg, du], [wg, wu], after=[tok])[0]
        d_in, g_norm = _rms_bwd(f"ffn_norm_bwd_{layer}", xin, ffn_norm[layer:layer + 1], dh, d_out)
        return st, d_in, g_norm, dg

    def memory_bwd(layer, mem_n, dkv):
        dkv = dkv.astype(BF)
        wkv = weights[f"w_mem_kv{layer}"]
        gr = _mm_tn(f"d_mem_kv_{layer}", mem_n, dkv, wkv)
        d_mem_n = _mm_nt(f"d_mem_n_{layer}", [dkv], [wkv])[0]
        return gr, _rms_bwd(f"mem_norm_bwd_{layer}", mem0, mem_norm[layer:layer + 1], d_mem_n)[1]

    st_ffn1, d3, g_ffn1, _ = ffn_bwd(1, d4, x3, hf1, g1, u1, act1)
    gr_sgu_out = _mm_tn("d_sgu_out", cat1, d3, weights["sgu_w_out"])
    dcat1 = _mm_nt("d_cat_1", [d3], [weights["sgu_w_out"]])[0]
    st_ffn1, tok = finish_reduce(st_ffn1, [dcat1])
    dproj1, dkv1 = _mem_bwd("mem_bwd_1", proj1, 6, kv1, dcat1, 3, proj1.shape[1])
    gr_kv1, g_mem1 = memory_bwd(1, mem_n1, dkv1)
    dproj1, g_wsp, g_bsp_t, g_ln_g, g_ln_b = _sgu_bwd(proj1, dcat1, ln_g, ln_b, w_sp, b_t, dproj1)
    gr_sgu_in = _mm_tn("d_sgu_in", h1, dproj1, weights["sgu_w_in"], after=[tok])
    finish_update(st_ffn1, [gr_sgu_in])
    st_mix1, tok = start_reduce("mix1", ["sgu_w_out", "w_mem_kv1", "sgu_w_in"], [gr_sgu_out, gr_kv1, gr_sgu_in])
    dh1 = _mm_nt("d_h_1", [dproj1], [weights["sgu_w_in"]], after=[tok])[0]
    d2, g_mix1 = _rms_bwd("mix_norm_bwd_1", x2, mix_norm[1:2], dh1, d3)

    st_ffn0, d1, g_ffn0, dg0 = ffn_bwd(0, d2, x1, hf0, g0, u0, act0)
    dev = 4 * xi + 2 * yi + ci
    small_a = [g_mix1, g_mem1, jnp.concatenate([g_ffn0, g_ffn1]), g_wsp, g_bsp_t[:, :SGU_GROUPS].T, g_final, g_ln_g, g_ln_b]
    sa_ssem, sa_rsem, sa_slots, tok = _small_start("small_start_a", _own_slot(_pack(small_a, LANES), dev))
    gr_attn_out = _mm_tn("d_attn_out", cat0, d1, weights["attn_w_out"], after=[tok])
    st_mix1, tok = finish_reduce(st_mix1, [gr_attn_out])
    dcat0 = _mm_nt("d_cat_0", [d1], [weights["attn_w_out"]], after=[tok])[0]
    dproj0, dkv0 = _mem_bwd("mem_bwd_0", proj0, 9, kv0, dcat0, 1, proj0.shape[1])
    finish_update(st_mix1, [dkv0])
    gr_kv0, g_mem0 = memory_bwd(0, mem_n0, dkv0)
    st_ffn0, tok = finish_reduce(st_ffn0, [g_mem0])
    d_merged, delta = _attn_delta(dcat0, cat0, after=[tok])
    dqs, dks, dvs = [], [], []
    for g in range(len(DILATIONS)):
        dq, dk, dv = _dil_bwd(g, qs[g], ks[g], vs[g], d_merged[g], lse[g], delta[g])
        dqs.append(dq)
        dks.append(dk)
        dvs.append(dv)
    dproj0 = _rope_bwd(dqs + dks + dvs, tables, dproj0)
    finish_update(st_ffn0, [dproj0])
    gr_attn_in = _mm_tn("d_attn_in", h0, dproj0, weights["attn_w_in"])
    st_mix0, tok = start_reduce("mix0", ["attn_w_out", "w_mem_kv0", "attn_w_in"], [gr_attn_out, gr_kv0, gr_attn_in])
    dh0 = _mm_nt("d_h_0", [dproj0], [weights["attn_w_in"]], after=[tok])[0]
    d0, g_mix0 = _rms_bwd("mix_norm_bwd_0", x0, mix_norm[0:1], dh0, d1)

    small_b = [g_mix0, g_mem0]
    sb_ssem, sb_rsem, sb_slots, tok = _small_start("small_start_b", _own_slot(_pack(small_b, 8), dev))
    sa_slots = _small_wait("small_wait_a", sa_ssem, sa_rsem, sa_slots, [tok])
    g_mix1, g_mem1, g_ffn, g_wsp, g_bsp, g_final, g_ln_g, g_ln_b = _unpack(_sum_devices("small_sum_a", sa_slots),
                                                                           [t.shape for t in small_a])
    sb_slots = _small_wait("small_wait_b", sb_ssem, sb_rsem, sb_slots, [g_final])
    g_mix0, g_mem0 = _unpack(_sum_devices("small_sum_b", sb_slots), [t.shape for t in small_b])
    st_mix0, tok = finish_reduce(st_mix0, [g_mix0])
    g_mix, g_mem = jnp.concatenate([g_mix0, g_mix1]), jnp.concatenate([g_mem0, g_mem1])
    shard_w = sgu_ln_g.shape[-1]
    g_ln_g = lax.dynamic_slice_in_dim(g_ln_g, chip * shard_w, shard_w, axis=1)
    g_ln_b = lax.dynamic_slice_in_dim(g_ln_b, chip * shard_w, shard_w, axis=1)
    small_names = ["mix_norm", "mem_norm", "ffn_norm", "sgu_w_spatial", "sgu_b_spatial", "final_norm", "sgu_ln_g",
                   "sgu_ln_b"]
    small_g = [g_mix, g_mem, g_ffn, g_wsp, g_bsp, g_final, g_ln_g, g_ln_b]
    small_shapes = [given_w[k].shape for k in small_names]
    packed = [_pack(t, LANES) for t in ([given_w[k] for k in small_names], small_g, [given_m[k] for k in small_names],
                                    [given_v[k] for k in small_names])]
    small_out = _adamw("adamw_small", packed[0][None], packed[1], packed[2][None], packed[3][None], 0)
    finish_update(st_mix0, [small_out[0]])
    for k, gk, dk, mk, vk in zip(small_names, *[_unpack(t[0], small_shapes) for t in small_out]):
        outputs[k] = (gk, dk, mk, vk)

    order = ["mix_norm", "mem_norm", "w_mem_kv", "ffn_norm", "w_gate", "w_up", "w_down", "attn_w_in", "attn_w_out",
             "sgu_w_in", "sgu_ln_g", "sgu_ln_b", "sgu_w_spatial", "sgu_b_spatial", "sgu_w_out", "final_norm"]
    return (loss, d0[None], *[outputs[k][0] for k in order], *[outputs[k][1] for k in order],
            *[outputs[k][2] for k in order], *[outputs[k][3] for k in order])
```

```python
import math

import jax
import jax.numpy as jnp
from jax import lax
from jax.experimental import pallas as pl
from jax.experimental.pallas import tpu as pltpu

F32 = jnp.float32
BF = jnp.bfloat16
MESH = pl.DeviceIdType.MESH

HEAD_DIM = 128
MEM_HEADS = 4
MEM_W = MEM_HEADS * HEAD_DIM
GROUP_W = 4 * HEAD_DIM
DILATIONS = (1, 4, 16)
BLK = 128
SGU_GROUPS = 12
SGU_W = SGU_GROUPS * HEAD_DIM
ROT_HALF = 16
ROPE_THETA = 500000.0
NORM_EPS = 1e-6
LN_EPS = 1e-5
NEG_INF = -1e30
SCALE = HEAD_DIM ** -0.5
ADAM_LR, ADAM_B1, ADAM_B2, ADAM_EPS, ADAM_WD, ADAM_STEP = 0.001, 0.9, 0.999, 1e-08, 0.01, 10

VMEM_LIMIT = 48 * 2 ** 20
VMEM_TILE_BUDGET = 36 * 2 ** 20
N_CHIPS = 4
N_DEV = 8
LANES = 128
EPILOGUE_CHUNK = 256

NT_DIMS = (((1,), (1,)), ((), ()))
TN_DIMS = (((0,), (0,)), ((), ()))
NN_DIMS = (((1,), (0,)), ((), ()))

ANY = pl.BlockSpec(memory_space=pl.ANY)
HBM = pl.BlockSpec(memory_space=pltpu.HBM)
SEM = pl.BlockSpec(memory_space=pltpu.SEMAPHORE)
EFFECT = pltpu.SideEffectType.DATAFLOW_SIDE_EFFECTING


def _params(sem):
    return pltpu.CompilerParams(dimension_semantics=sem, vmem_limit_bytes=VMEM_LIMIT)


def _pick(n, cap):
    if n <= cap:
        return n
    best = None
    for t in range(LANES, cap + 1, LANES):
        if n % t == 0:
            best = t
    assert best is not None, (n, cap)
    return best


def _pick_rows(n, cap):
    t = min(n, cap)
    while n % t:
        t //= 2
    return t


def _mm(name, dims, a_list, a_specs, b_list, b_specs, pairs, n_acc, acc_shape, grid, extras, e_specs,
        out_shapes, out_specs, epilogue, after=(), col_chunk=None, store=None, shard_width=None):
    na, nb, ne, no = len(a_list), len(b_list), len(extras), len(out_shapes)
    nk = grid[-1]

    def products(a, b, cols=None):
        sums = [None] * n_acc
        for ai, bi, ci in pairs:
            bv = b[bi]
            if cols is None:
                bv = bv[...]
            elif dims == NT_DIMS:
                bv = bv[cols, :]
            else:
                bv = bv[:, cols]
            if bv.ndim == 3:
                bv = bv.reshape(-1, bv.shape[-1])
            prod = lax.dot_general(a[ai][...].astype(BF), bv.astype(BF), dims, preferred_element_type=F32)
            sums[ci] = prod if sums[ci] is None else sums[ci] + prod
        return sums

    def body(*refs):
        a = refs[:na]
        b = refs[na:na + nb]
        e = refs[na + nb:na + nb + ne]
        off = na + nb + ne + len(after)
        o = refs[off:off + no]
        acc = refs[off + no:]

        def finish(sums):
            outs = epilogue(sums, [r[...] for r in e])
            if store is not None:
                store(o, outs)
                return
            for r, v in zip(o, outs):
                r[...] = v.astype(r.dtype)

        if nk == 1 and shard_width:
            (ai, bi, _), = pairs
            av = a[ai][...].astype(BF)
            if dims == NT_DIMS:
                total = None
                for j in range(N_CHIPS):
                    cols = slice(j * shard_width, (j + 1) * shard_width)
                    prod = lax.dot_general(av[:, cols], b[bi][j].astype(BF), dims, preferred_element_type=F32)
                    total = prod if total is None else total + prod
                finish([total])
                return
            for j in range(N_CHIPS):
                cols = slice(j * shard_width, (j + 1) * shard_width)
                prod = lax.dot_general(av, b[bi][j].astype(BF), dims, preferred_element_type=F32)
                outs = epilogue([prod], [r[:, cols] for r in e])
                for r, v in zip(o, outs):
                    r[:, cols] = v.astype(r.dtype)
            return
        if nk == 1 and col_chunk:
            width = acc_shape[1]
            for c0 in range(0, width, col_chunk):
                cols = slice(c0, min(c0 + col_chunk, width))
                outs = epilogue(products(a, b, cols), [r[:, cols] for r in e])
                for r, v in zip(o, outs):
                    r[:, cols] = v.astype(r.dtype)
            return
        if nk == 1:
            finish(products(a, b))
            return
        k = pl.program_id(len(grid) - 1)

        @pl.when(k == 0)
        def _():
            for c, v in zip(acc, products(a, b)):
                c[...] = v

        @pl.when(jnp.logical_and(k > 0, k < nk - 1))
        def _():
            for c, v in zip(acc, products(a, b)):
                c[...] += v

        @pl.when(k == nk - 1)
        def _():
            finish([c[...] + v for c, v in zip(acc, products(a, b))])

    ins = [*a_list, *b_list, *extras, *after]
    in_specs = [*a_specs, *b_specs, *e_specs, *([ANY] * len(after))]
    sem = ("parallel",) * (len(grid) - 1) + ("arbitrary",)
    scratch = [] if nk == 1 else [pltpu.VMEM(acc_shape, F32)] * n_acc
    return pl.pallas_call(
        body, out_shape=out_shapes, grid=grid, in_specs=in_specs, out_specs=out_specs, scratch_shapes=scratch,
        name=name, compiler_params=_params(sem))(*ins)


def _tile_bytes(blocks, single=()):
    size = lambda s, d: math.prod(s) * jnp.dtype(d).itemsize
    return sum(2 * size(s, d) for s, d in blocks) + sum(size(s, d) for s, d in single)


def _first(acc, extra):
    return [acc[0]]


class Weight:
    def __init__(self, arr, axis):
        self.arr, self.axis = arr, axis
        _, self.rows, self.cols = arr.shape


SMALL_WEIGHT_BYTES = 8 * 2 ** 20


def _is_small(w):
    return w.arr.size * w.arr.dtype.itemsize <= SMALL_WEIGHT_BYTES


def _mm_nn(name, a, w, extras=(), epilogue=_first, out_dtypes=(F32,), after=()):
    m, kdim = a.shape
    b_spec, shard_width = None, None
    if w.axis == "col" and _is_small(w):
        n_total = tn = N_CHIPS * w.cols
        tk, gn, gk = kdim, 1, 1
        shard_width = w.cols
        b_spec = pl.BlockSpec((N_CHIPS, kdim, w.cols), lambda n, i, k: (0, 0, 0))
    elif w.axis == "col":
        n_total = N_CHIPS * w.cols
        tn = _pick(w.cols, 1408)
        tk = _pick(kdim, 2048)
        ncb = w.cols // tn
        gn, gk = N_CHIPS * ncb, kdim // tk
        b_map = lambda n, i, k: (n // ncb, k, n % ncb)
    elif kdim <= 2048:
        n_total = w.cols
        tn = _pick(w.cols, 1024)
        tk = kdim
        gn, gk = n_total // tn, 1
        b_spec = pl.BlockSpec((N_CHIPS, w.rows, tn), lambda n, i, k: (0, 0, n))
    else:
        n_total = w.cols
        tn = _pick(w.cols, 1024)
        tk = _pick(w.rows, 1408)
        nkb = w.rows // tk
        gn, gk = n_total // tn, N_CHIPS * nkb
        b_map = lambda n, i, k: (k // nkb, k % nkb, n)
    if b_spec is None:
        b_spec = pl.BlockSpec((None, tk, tn), b_map)
    for tm in (1024, 512, 256, 128):
        if m % tm:
            continue
        blocks = [((tm, tk), a.dtype), ((tk, tn), BF)] + [((tm, tn), e.dtype) for e in extras]
        blocks += [((tm, tn), d) for d in out_dtypes] + [((tm, tn), BF)]
        if _tile_bytes(blocks) <= VMEM_TILE_BUDGET:
            break
    o_spec = pl.BlockSpec((tm, tn), lambda n, i, k: (i, n))
    return _mm(
        name, NN_DIMS, [a], [pl.BlockSpec((tm, tk), lambda n, i, k: (i, k))],
        [w.arr], [b_spec], [(0, 0, 0)], 1, (tm, tn), (gn, m // tm, gk),
        list(extras), [o_spec] * len(extras),
        [jax.ShapeDtypeStruct((m, n_total), d) for d in out_dtypes], [o_spec] * len(out_dtypes), epilogue, after,
        shard_width=shard_width)


def _gate_up(name, h, wg, wu):
    m, kdim = h.shape
    tn = _pick(wg.cols, 1408)
    tk = _pick(kdim, 2048)
    ncb = wg.cols // tn
    single = kdim == tk
    for tm in (1024, 512, 256, 128):
        blocks = [((tm, tk), BF)] + [((tm, tn), BF)] * 3
        weights = [((tk, tn), BF)] * 2
        if m % tm == 0 and _tile_bytes(blocks + ([] if single else weights), weights if single else ()) <= VMEM_TILE_BUDGET:
            break
    b_spec = pl.BlockSpec((None, tk, tn), lambda n, i, k: (n // ncb, k, n % ncb),
                          pipeline_mode=pl.Buffered(1) if single else None)
    o_spec = pl.BlockSpec((tm, tn), lambda n, i, k: (i, n))
    n_total = N_CHIPS * wg.cols

    def epilogue(acc, extra):
        g, u = acc
        return [g, u, g * (1.0 / (1.0 + jnp.exp(-g))) * u]

    return _mm(
        name, NN_DIMS, [h], [pl.BlockSpec((tm, tk), lambda n, i, k: (i, k))], [wg.arr, wu.arr], [b_spec, b_spec],
        [(0, 0, 0), (0, 1, 1)], 2, (tm, tn), (N_CHIPS * ncb, m // tm, kdim // tk), [], [],
        [jax.ShapeDtypeStruct((m, n_total), BF)] * 3, [o_spec] * 3, epilogue, col_chunk=EPILOGUE_CHUNK)


def _mm_nt(name, dys, ws, out_dtypes=(F32,), extras=(), epilogue=_first, after=(), col_chunk=None):
    m = dys[0].shape[0]
    w0 = ws[0]
    npair = len(dys)
    b_spec, shard_width = None, None
    if w0.axis == "col" and npair == 1 and _is_small(w0):
        k_total = tko = w0.rows
        tkc = N_CHIPS * w0.cols
        go, gk = 1, 1
        shard_width = w0.cols
        b_spec = pl.BlockSpec(w0.arr.shape, lambda o, i, k: (0, 0, 0))
    elif w0.axis == "col":
        k_total = w0.rows
        tko = _pick(k_total, 1024)
        tkc = _pick(w0.cols, 1408)
        nkb = w0.cols // tkc
        go, gk = k_total // tko, N_CHIPS * nkb
        b_map = lambda o, i, k: (k // nkb, o, k % nkb)
    else:
        k_total = N_CHIPS * w0.rows
        tko = _pick(w0.rows, 1408)
        tkc = _pick(w0.cols, 2048)
        nob = w0.rows // tko
        go, gk = N_CHIPS * nob, w0.cols // tkc
        b_map = lambda o, i, k: (o // nob, o % nob, k)
    single = gk == 1
    for tm in (1024, 512, 256, 128):
        if m % tm:
            continue
        blocks = [((tm, tkc), d.dtype) for d in dys]
        blocks += [((tm, tko), e.dtype) for e in extras] + [((tm, tko), d) for d in out_dtypes]
        blocks += [((tm, tko), BF)]
        weights = [((tko, tkc), BF)] * npair
        if _tile_bytes(blocks + ([] if single else weights), weights if single else ()) <= VMEM_TILE_BUDGET:
            break
    if b_spec is None:
        b_spec = pl.BlockSpec((None, tko, tkc), b_map, pipeline_mode=pl.Buffered(1) if single else None)
    o_spec = pl.BlockSpec((tm, tko), lambda o, i, k: (i, o))
    return _mm(
        name, NT_DIMS, list(dys), [pl.BlockSpec((tm, tkc), lambda o, i, k: (i, k))] * npair,
        [w.arr for w in ws], [b_spec] * npair,
        [(i, i, 0) for i in range(npair)], 1, (tm, tko), (go, m // tm, gk), list(extras), [o_spec] * len(extras),
        [jax.ShapeDtypeStruct((m, k_total), d) for d in out_dtypes], [o_spec] * len(out_dtypes), epilogue, after,
        col_chunk if gk == 1 and shard_width is None else None, shard_width=shard_width)


def _mm_tn(name, a, dy, w, after=()):
    m, k_total = a.shape
    rows2 = w.rows // 2
    tn = _pick(w.cols, 1408)
    ncb = w.cols // tn
    epilogue, store = _first, None
    if k_total <= 2048 and w.axis == "col" and _is_small(w):
        tkr, tn = k_total, N_CHIPS * w.cols
        gr, gn = 1, 1
        o_spec = pl.BlockSpec((2, N_CHIPS, rows2, w.cols), lambda r, n, t: (0, 0, 0, 0))

        def store(o_refs, outs):
            for j in range(N_CHIPS):
                for h in range(2):
                    o_refs[0][h, j] = outs[0][h * rows2:(h + 1) * rows2, j * w.cols:(j + 1) * w.cols].astype(BF)
    elif k_total <= 2048 and w.axis == "col":
        tkr = k_total
        gr, gn = 1, N_CHIPS * ncb
        o_spec = pl.BlockSpec((2, None, rows2, tn), lambda r, n, t: (0, n // ncb, 0, n % ncb))
        epilogue = lambda acc, extra: [acc[0].reshape(2, rows2, tn)]
    elif k_total <= 2048:
        tkr = k_total
        gr, gn = 1, ncb
        o_spec = pl.BlockSpec((2, N_CHIPS, rows2, tn), lambda r, n, t: (0, 0, 0, n))

        def store(o_refs, outs):
            for j in range(N_CHIPS):
                for h in range(2):
                    lo = (2 * j + h) * rows2
                    o_refs[0][h, j] = outs[0][lo:lo + rows2].astype(BF)
    elif rows2 % LANES:
        tkr = w.rows
        assert w.axis == "row"
        gr, gn = N_CHIPS, ncb
        o_spec = pl.BlockSpec((2, None, rows2, tn), lambda r, n, t: (0, r, 0, n))
        epilogue = lambda acc, extra: [acc[0].reshape(2, rows2, tn)]
    else:
        tkr = _pick(rows2, 1408)
        nrb = rows2 // tkr
        if w.axis == "col":
            gr, gn = w.rows // tkr, N_CHIPS * ncb
            o_map = lambda r, n, t: (r // nrb, n // ncb, r % nrb, n % ncb)
        else:
            per = w.rows // tkr
            gr, gn = N_CHIPS * per, ncb
            o_map = lambda r, n, t: ((r % per) // nrb, r // per, (r % per) % nrb, n)
        o_spec = pl.BlockSpec((None, None, tkr, tn), o_map)
    for tmk in (1024, 512, 256, 128):
        blocks = [((tmk, tkr), a.dtype), ((tmk, tn), dy.dtype), ((tkr, tn), BF), ((tkr, tn), BF)]
        if m % tmk == 0 and _tile_bytes(blocks) <= VMEM_TILE_BUDGET:
            break
    return _mm(
        name, TN_DIMS, [a], [pl.BlockSpec((tmk, tkr), lambda r, n, t: (t, r))],
        [dy], [pl.BlockSpec((tmk, tn), lambda r, n, t: (t, n))], [(0, 0, 0)], 1, (tkr, tn), (gr, gn, m // tmk), [], [],
        [jax.ShapeDtypeStruct((2, N_CHIPS, rows2, w.cols), BF)], [o_spec], epilogue, after, store=store)[0]


def _rms_fwd(name, x, g, after=()):
    s, d = x.shape
    tr = _pick_rows(s, 512)

    def body(x_ref, g_ref, *rest):
        h_ref = rest[-1]
        xf = x_ref[...]
        r = lax.rsqrt(jnp.mean(xf * xf, axis=-1, keepdims=True) + NORM_EPS)
        h_ref[...] = (xf * r * g_ref[...]).astype(BF)

    return pl.pallas_call(
        body, out_shape=jax.ShapeDtypeStruct((s, d), BF), grid=(s // tr,),
        in_specs=[pl.BlockSpec((tr, d), lambda i: (i, 0)), pl.BlockSpec((1, d), lambda i: (0, 0))] + [ANY] * len(after),
        out_specs=pl.BlockSpec((tr, d), lambda i: (i, 0)), name=name, compiler_params=_params(("parallel",)))(x, g, *after)


def _rms_bwd(name, x, g, dh, dres=None):
    s, d = x.shape
    tr = _pick_rows(s, 256)
    has_res = dres is not None

    def body(*refs):
        if has_res:
            x_ref, g_ref, dh_ref, dres_ref, dx_ref, dg_ref = refs
        else:
            x_ref, g_ref, dh_ref, dx_ref, dg_ref = refs
        xf = x_ref[...]
        r = lax.rsqrt(jnp.mean(xf * xf, axis=-1, keepdims=True) + NORM_EPS)
        xr = xf * r
        dy = dh_ref[...]
        a = dy * g_ref[...]
        dx = r * (a - xr * jnp.mean(a * xr, axis=-1, keepdims=True))
        if has_res:
            dx = dx + dres_ref[...]
        dx_ref[...] = dx

        @pl.when(pl.program_id(0) == 0)
        def _():
            dg_ref[...] = jnp.zeros_like(dg_ref)

        dg_ref[...] += jnp.sum(dy * xr, axis=0, keepdims=True)

    row = pl.BlockSpec((tr, d), lambda i: (i, 0))
    vec = pl.BlockSpec((1, d), lambda i: (0, 0))
    ins = [x, g, dh] + ([dres] if has_res else [])
    in_specs = [row, vec, row] + ([row] if has_res else [])
    return pl.pallas_call(
        body, out_shape=[jax.ShapeDtypeStruct((s, d), F32), jax.ShapeDtypeStruct((1, d), F32)], grid=(s // tr,),
        in_specs=in_specs, out_specs=[row, vec], name=name, compiler_params=_params(("arbitrary",)))(*ins)


def _final_loss(x, tgt, g):
    s, d = x.shape
    tr = _pick_rows(s, 256)

    def body(x_ref, t_ref, g_ref, dx_ref, dg_ref, loss_ref):
        xf = x_ref[...]
        gain = g_ref[...]
        r = lax.rsqrt(jnp.mean(xf * xf, axis=-1, keepdims=True) + NORM_EPS)
        xr = xf * r
        err = xr * gain - t_ref[...]
        dy = err * (1.0 / d)
        a = dy * gain
        dx_ref[...] = r * (a - xr * jnp.mean(a * xr, axis=-1, keepdims=True))

        @pl.when(pl.program_id(0) == 0)
        def _():
            dg_ref[...] = jnp.zeros_like(dg_ref)
            loss_ref[...] = jnp.zeros_like(loss_ref)

        dg_ref[...] += jnp.sum(dy * xr, axis=0, keepdims=True)
        part = 0.5 * jnp.sum(jnp.mean(err * err, axis=-1, keepdims=True), axis=0, keepdims=True)
        loss_ref[...] += jnp.broadcast_to(part, loss_ref.shape)

    row = pl.BlockSpec((tr, d), lambda i: (i, 0))
    vec = pl.BlockSpec((1, d), lambda i: (0, 0))
    return pl.pallas_call(
        body, out_shape=[jax.ShapeDtypeStruct((s, d), F32), jax.ShapeDtypeStruct((1, d), F32),
                         jax.ShapeDtypeStruct((8, LANES), F32)],
        grid=(s // tr,), in_specs=[row, row, vec], out_specs=[row, vec, pl.BlockSpec((8, LANES), lambda i: (0, 0))],
        name="final_loss", compiler_params=_params(("arbitrary",)))(x, tgt, g)


def _swiglu_bwd_epilogue(acc, extra):
    dact = acc[0]
    g, u = extra[0].astype(F32), extra[1].astype(F32)
    sig = 1.0 / (1.0 + jnp.exp(-g))
    return [dact * u * sig * (1.0 + g * (1.0 - sig)), dact * g * sig]


GELU_C = math.sqrt(2.0 / math.pi)
GELU_A = 0.044715


def _gelu(x):
    return 0.5 * x * (1.0 + jnp.tanh(GELU_C * (x + GELU_A * x * x * x)))


def _gelu_grad(x):
    t = jnp.tanh(GELU_C * (x + GELU_A * x * x * x))
    return 0.5 * (1.0 + t) + 0.5 * x * (1.0 - t * t) * GELU_C * (1.0 + 3.0 * GELU_A * x * x)


def _rope_tables(positions):
    inv_freq = ROPE_THETA ** (-jnp.arange(ROT_HALF, dtype=F32) / ROT_HALF)
    ang = positions.astype(F32)[:, None] * inv_freq
    cos, sin = jnp.cos(ang), jnp.sin(ang)
    s = ang.shape[0]
    rest = HEAD_DIM - 2 * ROT_HALF
    zeros = jnp.zeros((s, ROT_HALF), F32)
    cos_t = jnp.concatenate([cos, cos, jnp.ones((s, rest), F32)], axis=1)
    sin_a = jnp.concatenate([-sin, zeros, jnp.zeros((s, rest), F32)], axis=1)
    sin_b = jnp.concatenate([zeros, sin, jnp.zeros((s, rest), F32)], axis=1)
    return cos_t, sin_a, sin_b


def _rope_head(xh, cos_t, sin_a, sin_b):
    up = pltpu.roll(xh, HEAD_DIM - ROT_HALF, 1)
    down = pltpu.roll(xh, ROT_HALF, 1)
    return xh * cos_t + up * sin_a + down * sin_b


def _residue(r, rows, dil):
    return slice(None) if dil == 1 else pl.ds(r, rows, stride=dil)


ROPE_TILE = 256
N_PARTS = 9
HEADS_PER_GROUP = GROUP_W // HEAD_DIM
N_HEADS_IN = N_PARTS * HEADS_PER_GROUP


def _rope_fwd(proj, tables):
    s = proj.shape[0]
    tm = _pick_rows(s, ROPE_TILE)

    def body(*refs):
        heads = refs[:N_HEADS_IN]
        c_ref, sa_ref, sb_ref = refs[N_HEADS_IN:N_HEADS_IN + 3]
        outs = refs[N_HEADS_IN + 3:]
        for g, dil in enumerate(DILATIONS):
            rows = tm // dil
            for r in range(dil):
                rs = _residue(r, rows, dil)
                cos_t, sin_a, sin_b = c_ref[rs, :], sa_ref[rs, :], sb_ref[rs, :]
                for kind in range(3):
                    part = 3 * kind + g
                    for h in range(HEADS_PER_GROUP):
                        xh = heads[part * HEADS_PER_GROUP + h][rs, :]
                        if kind < 2:
                            xh = _rope_head(xh, cos_t, sin_a, sin_b)
                        outs[part][r, :, h * HEAD_DIM:(h + 1) * HEAD_DIM] = xh.astype(BF)

    tab = pl.BlockSpec((tm, HEAD_DIM), lambda i: (i, 0))
    head_specs = [pl.BlockSpec((tm, HEAD_DIM), lambda i, j=j: (i, j)) for j in range(N_HEADS_IN)]
    shapes, specs = [], []
    for part in range(N_PARTS):
        dil = DILATIONS[part % 3]
        shapes.append(jax.ShapeDtypeStruct((dil, s // dil, GROUP_W), BF))
        specs.append(pl.BlockSpec((dil, tm // dil, GROUP_W), lambda i: (0, i, 0)))
    return pl.pallas_call(
        body, out_shape=shapes, grid=(s // tm,), in_specs=head_specs + [tab, tab, tab], out_specs=specs,
        name="rope_fwd", compiler_params=_params(("parallel",)))(*([proj] * N_HEADS_IN), *tables)


def _rope_bwd(parts, tables, into):
    s = into.shape[0]
    tm = _pick_rows(s, ROPE_TILE)

    def body(*refs):
        ins = refs[:N_PARTS]
        c_ref, sa_ref, sb_ref, into_ref, o_ref, scr = refs[N_PARTS:]
        for g, dil in enumerate(DILATIONS):
            rows = tm // dil
            for r in range(dil):
                rs = _residue(r, rows, dil)
                cos_t, sin_a, sin_b = c_ref[rs, :], -sa_ref[rs, :], -sb_ref[rs, :]
                for kind in range(3):
                    part = 3 * kind + g
                    for h in range(HEADS_PER_GROUP):
                        xh = ins[part][r, :, h * HEAD_DIM:(h + 1) * HEAD_DIM]
                        if kind < 2:
                            xh = _rope_head(xh, cos_t, sin_a, sin_b)
                        scr[part * HEADS_PER_GROUP + h, rs, :] = xh
        for j in range(N_HEADS_IN):
            o_ref[:, j * HEAD_DIM:(j + 1) * HEAD_DIM] = scr[j].astype(BF)

    tab = pl.BlockSpec((tm, HEAD_DIM), lambda i: (i, 0))
    i_specs = [pl.BlockSpec((DILATIONS[p % 3], tm // DILATIONS[p % 3], GROUP_W), lambda i: (0, i, 0))
               for p in range(N_PARTS)]
    return pl.pallas_call(
        body, out_shape=jax.ShapeDtypeStruct(into.shape, into.dtype), grid=(s // tm,),
        in_specs=i_specs + [tab] * 3 + [ANY], out_specs=pl.BlockSpec((tm, N_PARTS * GROUP_W), lambda i: (i, 0)),
        scratch_shapes=[pltpu.VMEM((N_HEADS_IN, tm, HEAD_DIM), F32)], input_output_aliases={N_PARTS + 3: 0},
        name="rope_bwd", compiler_params=_params(("parallel",)))(*parts, *tables, into)


def _band_mask(n):
    qi = lax.broadcasted_iota(jnp.int32, (BLK, 2 * BLK), 0)
    ki = lax.broadcasted_iota(jnp.int32, (BLK, 2 * BLK), 1)
    prev = jnp.logical_and(jnp.logical_and(ki < BLK, ki >= qi), n > 0)
    return jnp.logical_or(prev, jnp.logical_and(ki >= BLK, qi >= ki - BLK))


Q_BLOCKS = 2
Q_ROWS = Q_BLOCKS * BLK


def _dil_specs(n_steps):
    last = n_steps - 1
    own = pl.BlockSpec((None, Q_ROWS, GROUP_W), lambda r, n: (r, jnp.minimum(n, last), 0))
    before = pl.BlockSpec((None, BLK, GROUP_W), lambda r, n: (r, jnp.maximum(Q_BLOCKS * n - 1, 0), 0))
    return own, before


def _dil_fwd(g, q, k, v):
    dil, length, _ = q.shape
    n_steps = length // Q_ROWS

    def body(q_ref, ko_ref, kb_ref, vo_ref, vb_ref, o_ref, lse_ref):
        n = pl.program_id(1)
        for h in range(GROUP_W // HEAD_DIM):
            sl = slice(h * HEAD_DIM, (h + 1) * HEAD_DIM)
            keys = jnp.concatenate([kb_ref[:, sl], ko_ref[:, sl]], axis=0)
            vals = jnp.concatenate([vb_ref[:, sl], vo_ref[:, sl]], axis=0)
            for j in range(Q_BLOCKS):
                rows, win = slice(j * BLK, (j + 1) * BLK), slice(j * BLK, (j + 2) * BLK)
                sc = lax.dot_general(q_ref[rows, sl], keys[win], NT_DIMS, preferred_element_type=F32) * SCALE
                sc = jnp.where(_band_mask(Q_BLOCKS * n + j), sc, NEG_INF)
                mx = jnp.max(sc, axis=-1, keepdims=True)
                p = jnp.exp(sc - mx)
                den = jnp.sum(p, axis=-1, keepdims=True)
                o_ref[rows, sl] = jnp.dot(p.astype(BF), vals[win], preferred_element_type=F32) / den
                lse_ref[rows, sl] = jnp.broadcast_to(mx + jnp.log(den), (BLK, HEAD_DIM))

    own, before = _dil_specs(n_steps)
    return pl.pallas_call(
        body, out_shape=[jax.ShapeDtypeStruct(q.shape, F32)] * 2, grid=(dil, n_steps),
        in_specs=[own, own, before, own, before], out_specs=[own, own], name=f"dil_fwd_{g}",
        compiler_params=_params(("parallel", "arbitrary")))(q, k, k, v, v)


def _dil_bwd(g, q, k, v, do, lse, delta):
    dil, length, _ = q.shape
    n_steps = length // Q_ROWS

    def body(q_ref, ko_ref, kb_ref, vo_ref, vb_ref, do_ref, lse_ref, dl_ref, dq_ref, dk_ref, dv_ref, ck_ref, cv_ref):
        n = pl.program_id(1)
        live = n < n_steps

        @pl.when(n == 0)
        def _():
            ck_ref[...] = jnp.zeros_like(ck_ref)
            cv_ref[...] = jnp.zeros_like(cv_ref)

        for h in range(GROUP_W // HEAD_DIM):
            sl = slice(h * HEAD_DIM, (h + 1) * HEAD_DIM)
            keys = jnp.concatenate([kb_ref[:, sl], ko_ref[:, sl]], axis=0)
            vals = jnp.concatenate([vb_ref[:, sl], vo_ref[:, sl]], axis=0)
            dks, dvs = [], []
            for j in range(Q_BLOCKS):
                rows, win = slice(j * BLK, (j + 1) * BLK), slice(j * BLK, (j + 2) * BLK)
                qh, doh = q_ref[rows, sl], do_ref[rows, sl]
                lse_h = lse_ref[rows, h * HEAD_DIM:h * HEAD_DIM + 1]
                dl_h = dl_ref[rows, h * HEAD_DIM:h * HEAD_DIM + 1]
                mask = jnp.logical_and(_band_mask(Q_BLOCKS * n + j), live)
                sc = lax.dot_general(qh, keys[win], NT_DIMS, preferred_element_type=F32) * SCALE
                p = jnp.where(mask, jnp.exp(jnp.minimum(sc - lse_h, 0.0)), 0.0)
                dp = lax.dot_general(doh, vals[win], NT_DIMS, preferred_element_type=F32)
                ds = (p * (dp - dl_h) * SCALE).astype(BF)
                dq = jnp.dot(ds, keys[win], preferred_element_type=F32)

                @pl.when(live)
                def _(dq=dq, rows=rows, sl=sl):
                    dq_ref[rows, sl] = dq

                dks.append(lax.dot_general(ds, qh, TN_DIMS, preferred_element_type=F32))
                dvs.append(lax.dot_general(p.astype(BF), doh, TN_DIMS, preferred_element_type=F32))
            for out_ref, carry, parts in ((dk_ref, ck_ref, dks), (dv_ref, cv_ref, dvs)):
                out_ref[:Q_ROWS - BLK, sl] = carry[:Q_ROWS - BLK, sl]
                out_ref[Q_ROWS - BLK:, sl] = carry[Q_ROWS - BLK:, sl] + parts[0][:BLK]
                for j in range(Q_BLOCKS - 1):
                    carry[j * BLK:(j + 1) * BLK, sl] = parts[j][BLK:] + parts[j + 1][:BLK]
                carry[Q_ROWS - BLK:, sl] = parts[-1][BLK:]

    own, before = _dil_specs(n_steps)
    behind = pl.BlockSpec((None, Q_ROWS, GROUP_W), lambda r, n: (r, jnp.maximum(n - 1, 0), 0))
    return pl.pallas_call(
        body, out_shape=[jax.ShapeDtypeStruct(q.shape, F32)] * 3, grid=(dil, n_steps + 1),
        in_specs=[own, own, before, own, before, own, own, own], out_specs=[own, behind, behind],
        scratch_shapes=[pltpu.VMEM((Q_ROWS, GROUP_W), F32)] * 2, name=f"dil_bwd_{g}",
        compiler_params=_params(("parallel", "arbitrary")))(q, k, k, v, v, do, lse, delta)


def _major_specs(s, tm, dtype):
    shapes = [jax.ShapeDtypeStruct((dil, s // dil, GROUP_W), dtype) for dil in DILATIONS]
    specs = [pl.BlockSpec((dil, tm // dil, GROUP_W), lambda i: (0, i, 0)) for dil in DILATIONS]
    return shapes, specs


def _attn_merge(outs, lses):
    s = outs[0].shape[1]
    tm = _pick_rows(s, ROPE_TILE)

    def body(o0, o1, o2, l0, l1, l2, m_ref, e0, e1, e2, so1, so2, sl1, sl2, se):
        for h in range(HEADS_PER_GROUP):
            sl = slice(h * HEAD_DIM, (h + 1) * HEAD_DIM)
            for dil, src, dst in ((DILATIONS[1], o1, so1), (DILATIONS[2], o2, so2), (DILATIONS[1], l1, sl1),
                                  (DILATIONS[2], l2, sl2)):
                for r in range(dil):
                    dst[h, _residue(r, tm // dil, dil), :] = src[r, :, sl]
            a, b, c = l0[0, :, sl], sl1[h], sl2[h]
            mx = jnp.maximum(jnp.maximum(a, b), c)
            ea, eb, ec = jnp.exp(a - mx), jnp.exp(b - mx), jnp.exp(c - mx)
            den = ea + eb + ec
            m_ref[:, sl] = ((ea * o0[0, :, sl] + eb * so1[h] + ec * so2[h]) / den).astype(BF)
            se[h] = mx + jnp.log(den)
            for dil, dst in zip(DILATIONS, (e0, e1, e2)):
                for r in range(dil):
                    dst[r, :, sl] = se[h, _residue(r, tm // dil, dil), :]

    shapes, specs = _major_specs(s, tm, F32)
    nat = pl.BlockSpec((tm, GROUP_W), lambda i: (i, 0))
    res = pl.pallas_call(
        body, out_shape=[jax.ShapeDtypeStruct((s, GROUP_W + MEM_W), BF)] + shapes, grid=(s // tm,), in_specs=specs * 2,
        out_specs=[nat] + specs, scratch_shapes=[pltpu.VMEM((HEADS_PER_GROUP, tm, HEAD_DIM), F32)] * 5,
        name="attn_merge", compiler_params=_params(("parallel",)))(*outs, *lses)
    return res[0], res[1:]


def _attn_delta(dcat, merged, after=()):
    s = merged.shape[0]
    tm = _pick_rows(s, ROPE_TILE)

    def body(*refs):
        d_refs, m_ref = refs[:HEADS_PER_GROUP], refs[HEADS_PER_GROUP]
        do_refs, dl_refs, scr = refs[-7:-4], refs[-4:-1], refs[-1]
        for h in range(HEADS_PER_GROUP):
            sl = slice(h * HEAD_DIM, (h + 1) * HEAD_DIM)
            prod = d_refs[h][...] * m_ref[:, sl].astype(F32)
            scr[h] = jnp.broadcast_to(jnp.sum(prod, axis=-1, keepdims=True), (tm, HEAD_DIM))
            for dil, do_ref, dl_ref in zip(DILATIONS, do_refs, dl_refs):
                for r in range(dil):
                    rs = _residue(r, tm // dil, dil)
                    do_ref[r, :, sl] = d_refs[h][rs, :].astype(BF)
                    dl_ref[r, :, sl] = scr[h, rs, :]

    nat = pl.BlockSpec((tm, GROUP_W), lambda i: (i, 0))
    head_specs = [pl.BlockSpec((tm, HEAD_DIM), lambda i, h=h: (i, h)) for h in range(HEADS_PER_GROUP)]
    bf_shapes, specs = _major_specs(s, tm, BF)
    f_shapes, _ = _major_specs(s, tm, F32)
    res = pl.pallas_call(
        body, out_shape=bf_shapes + f_shapes, grid=(s // tm,), in_specs=head_specs + [nat] + [ANY] * len(after),
        out_specs=specs * 2, scratch_shapes=[pltpu.VMEM((HEADS_PER_GROUP, tm, HEAD_DIM), F32)], name="attn_delta",
        compiler_params=_params(("parallel",)))(*([dcat] * HEADS_PER_GROUP), merged, *after)
    return res[:3], res[3:]


def _mem_probs(qh, kh):
    sc = lax.dot_general(qh, kh, NT_DIMS, preferred_element_type=F32) * SCALE
    p = jnp.exp(sc - jnp.max(sc, axis=-1, keepdims=True))
    return p, jnp.sum(p, axis=-1, keepdims=True)


def _mem_fwd(name, proj, q_block, kv, into, out_block):
    s = proj.shape[0]
    tq = _pick_rows(s, 512)

    def body(q_ref, kv_ref, into_ref, o_ref):
        for h in range(MEM_HEADS):
            sl = slice(h * HEAD_DIM, (h + 1) * HEAD_DIM)
            vsl = slice(MEM_W + h * HEAD_DIM, MEM_W + (h + 1) * HEAD_DIM)
            p, den = _mem_probs(q_ref[:, sl].astype(BF), kv_ref[:, sl].astype(BF))
            out = jnp.dot(p.astype(BF), kv_ref[:, vsl].astype(BF), preferred_element_type=F32) / den
            o_ref[:, sl] = out.astype(o_ref.dtype)

    return pl.pallas_call(
        body, out_shape=jax.ShapeDtypeStruct(into.shape, into.dtype), grid=(s // tq,),
        in_specs=[pl.BlockSpec((tq, MEM_W), lambda i: (i, q_block)), pl.BlockSpec(kv.shape, lambda i: (0, 0)), ANY],
        out_specs=pl.BlockSpec((tq, MEM_W), lambda i: (i, out_block)), input_output_aliases={2: 0}, name=name,
        compiler_params=_params(("parallel",)))(proj, kv, into)


def _mem_bwd(name, proj, q_block, kv, dcat, d_block, width):
    s = proj.shape[0]
    tq = _pick_rows(s, 512)

    def body(q_ref, kv_ref, do_ref, dq_ref, dkv_ref):
        @pl.when(pl.program_id(0) == 0)
        def _():
            dkv_ref[...] = jnp.zeros_like(dkv_ref)

        for h in range(MEM_HEADS):
            sl = slice(h * HEAD_DIM, (h + 1) * HEAD_DIM)
            vsl = slice(MEM_W + h * HEAD_DIM, MEM_W + (h + 1) * HEAD_DIM)
            qh, kh, vh = q_ref[:, sl].astype(BF), kv_ref[:, sl].astype(BF), kv_ref[:, vsl].astype(BF)
            doh = do_ref[:, sl].astype(BF)
            p, den = _mem_probs(qh, kh)
            p = p / den
            dp = lax.dot_general(doh, vh, NT_DIMS, preferred_element_type=F32)
            ds = (p * (dp - jnp.sum(p * dp, axis=-1, keepdims=True)) * SCALE).astype(BF)
            dq_ref[:, sl] = jnp.dot(ds, kh, preferred_element_type=F32).astype(BF)
            dkv_ref[:, sl] += lax.dot_general(ds, qh, TN_DIMS, preferred_element_type=F32)
            dkv_ref[:, vsl] += lax.dot_general(p.astype(BF), doh, TN_DIMS, preferred_element_type=F32)

    whole = pl.BlockSpec(kv.shape, lambda i: (0, 0))
    return pl.pallas_call(
        body, out_shape=[jax.ShapeDtypeStruct((s, width), BF), jax.ShapeDtypeStruct(kv.shape, F32)], grid=(s // tq,),
        in_specs=[pl.BlockSpec((tq, MEM_W), lambda i: (i, q_block)), whole,
                  pl.BlockSpec((tq, MEM_W), lambda i: (i, d_block))],
        out_specs=[pl.BlockSpec((tq, MEM_W), lambda i: (i, width // MEM_W - 1)), whole], name=name,
        compiler_params=_params(("arbitrary",)))(proj, kv, dcat)


def _causal():
    t = lax.broadcasted_iota(jnp.int32, (BLK, BLK), 0)
    s = lax.broadcasted_iota(jnp.int32, (BLK, BLK), 1)
    return t >= s


def _sgu_norm(v_pre, ln_g, ln_b):
    vg = _gelu(v_pre)
    mu = jnp.mean(vg, axis=-1, keepdims=True)
    cen = vg - mu
    rstd = lax.rsqrt(jnp.mean(cen * cen, axis=-1, keepdims=True) + LN_EPS)
    xhat = cen * rstd
    return xhat, rstd, xhat * ln_g + ln_b


def _sgu_fwd(proj, ln_g, ln_b, w_sp, b_t):
    s = proj.shape[0]

    def body(u_ref, v_ref, g_ref, b_ref, w_ref, bt_ref, o_ref):
        _, _, vn = _sgu_norm(v_ref[...], g_ref[...], b_ref[...])
        vn = vn.astype(BF)
        tri = _causal()
        for grp in range(SGU_GROUPS):
            sl = slice(grp * HEAD_DIM, (grp + 1) * HEAD_DIM)
            w = jnp.where(tri, w_ref[grp], 0.0).astype(BF)
            mixed = jnp.dot(w, vn[:, sl], preferred_element_type=F32) + bt_ref[:, grp:grp + 1]
            o_ref[:, sl] = (_gelu(u_ref[:, sl]) * mixed).astype(BF)

    vec = pl.BlockSpec((1, SGU_W), lambda i: (0, 0))
    return pl.pallas_call(
        body, out_shape=jax.ShapeDtypeStruct((s, SGU_W + MEM_W), BF), grid=(s // BLK,),
        in_specs=[pl.BlockSpec((BLK, SGU_W), lambda i: (i, 0)), pl.BlockSpec((BLK, SGU_W), lambda i: (i, 1)), vec, vec,
                  pl.BlockSpec(w_sp.shape, lambda i: (0, 0, 0)), pl.BlockSpec(b_t.shape, lambda i: (0, 0))],
        out_specs=pl.BlockSpec((BLK, SGU_W), lambda i: (i, 0)), name="sgu_fwd",
        compiler_params=_params(("parallel",)))(proj, proj, ln_g, ln_b, w_sp, b_t)


def _sgu_bwd(proj, dcat, ln_g, ln_b, w_sp, b_t, into):
    s = proj.shape[0]

    def body(u_ref, v_ref, d_ref, g_ref, b_ref, w_ref, bt_ref, into_ref, dp_ref, dw_ref, db_ref, dg_ref, dbeta_ref,
             dvn_ref):
        @pl.when(pl.program_id(0) == 0)
        def _():
            dw_ref[...] = jnp.zeros_like(dw_ref)
            db_ref[...] = jnp.zeros_like(db_ref)
            dg_ref[...] = jnp.zeros_like(dg_ref)
            dbeta_ref[...] = jnp.zeros_like(dbeta_ref)

        v_pre = v_ref[...]
        gain = g_ref[...]
        xhat, rstd, vn = _sgu_norm(v_pre, gain, b_ref[...])
        vn = vn.astype(BF)
        tri = _causal()
        lane = lax.broadcasted_iota(jnp.int32, (BLK, HEAD_DIM), 1)
        db_acc = jnp.zeros((BLK, HEAD_DIM), F32)
        for grp in range(SGU_GROUPS):
            sl = slice(grp * HEAD_DIM, (grp + 1) * HEAD_DIM)
            w = jnp.where(tri, w_ref[grp], 0.0).astype(BF)
            vn_g = vn[:, sl]
            mixed = jnp.dot(w, vn_g, preferred_element_type=F32) + bt_ref[:, grp:grp + 1]
            u_pre = u_ref[:, sl]
            d_out = d_ref[:, sl]
            dp_ref[:, sl] = (d_out * mixed * _gelu_grad(u_pre)).astype(BF)
            dmixed = d_out * _gelu(u_pre)
            dm = dmixed.astype(BF)
            dvn_ref[:, sl] = lax.dot_general(w, dm, TN_DIMS, preferred_element_type=F32)
            dw = lax.dot_general(dm, vn_g, NT_DIMS, preferred_element_type=F32)
            dw_ref[grp] += jnp.where(tri, dw, 0.0)
            db_acc += jnp.where(lane == grp, jnp.sum(dmixed, axis=-1, keepdims=True), 0.0)
        db_ref[...] += db_acc
        dvn = dvn_ref[...]
        dg_ref[...] += jnp.sum(dvn * xhat, axis=0, keepdims=True)
        dbeta_ref[...] += jnp.sum(dvn, axis=0, keepdims=True)
        dxh = dvn * gain
        dvg = rstd * (dxh - jnp.mean(dxh, axis=-1, keepdims=True) - xhat * jnp.mean(dxh * xhat, axis=-1, keepdims=True))
        dp_ref[:, SGU_W:] = (dvg * _gelu_grad(v_pre)).astype(BF)

    vec = pl.BlockSpec((1, SGU_W), lambda i: (0, 0))
    row = pl.BlockSpec((BLK, SGU_W), lambda i: (i, 0))
    w_spec = pl.BlockSpec(w_sp.shape, lambda i: (0, 0, 0))
    sq = pl.BlockSpec((BLK, HEAD_DIM), lambda i: (0, 0))
    return pl.pallas_call(
        body,
        out_shape=[jax.ShapeDtypeStruct(into.shape, into.dtype),
                   jax.ShapeDtypeStruct(w_sp.shape, F32), jax.ShapeDtypeStruct((BLK, HEAD_DIM), F32),
                   jax.ShapeDtypeStruct((1, SGU_W), F32), jax.ShapeDtypeStruct((1, SGU_W), F32)],
        grid=(s // BLK,),
        in_specs=[row, pl.BlockSpec((BLK, SGU_W), lambda i: (i, 1)), row, vec, vec, w_spec,
                  pl.BlockSpec(b_t.shape, lambda i: (0, 0)), ANY],
        out_specs=[pl.BlockSpec((BLK, 2 * SGU_W), lambda i: (i, 0)), w_spec, sq, vec, vec],
        scratch_shapes=[pltpu.VMEM((BLK, SGU_W), F32)], input_output_aliases={7: 0}, name="sgu_bwd",
        compiler_params=_params(("arbitrary",)))(proj, proj, dcat, ln_g, ln_b, w_sp, b_t, into)


def _place():
    return lax.axis_index("x"), lax.axis_index("y"), lax.axis_index("c")


def _other_chips(x, y):
    return [(1 - x, y), (x, 1 - y), (1 - x, 1 - y)]


def _peer(x, y, c, mask):
    return (1 - x if mask & 4 else x, 1 - y if mask & 2 else y, 1 - c if mask & 1 else c)


def _in_hbm(a):
    return pltpu.with_memory_space_constraint(a, pltpu.HBM)


def _token_spec():
    return jax.ShapeDtypeStruct((8, LANES), F32), pl.BlockSpec(memory_space=pltpu.VMEM)


def _remote(src, dst, ssem, rsem, to):
    return pltpu.make_async_remote_copy(src_ref=src, dst_ref=dst, send_sem=ssem, recv_sem=rsem, device_id=to,
                                        device_id_type=MESH)


def _place_shard(name, src, layer, place, dtype, after=()):
    _, rows, cols = src.shape
    tr = _pick_rows(rows, 512)

    def body(p_ref, s_ref, *rest):
        rest[-1][...] = s_ref[...].astype(dtype)

    grid_spec = pltpu.PrefetchScalarGridSpec(
        num_scalar_prefetch=1, grid=(rows // tr,),
        in_specs=[pl.BlockSpec((None, tr, cols), lambda i, p: (layer, i, 0))] + [ANY] * len(after),
        out_specs=pl.BlockSpec((None, tr, cols), lambda i, p: (p[1], i, 0)))
    return pl.pallas_call(body, out_shape=jax.ShapeDtypeStruct((N_CHIPS, rows, cols), dtype), grid_spec=grid_spec,
                          name=name, compiler_params=_params(("parallel",)))(place, src, *after)


def _gather_copies(bufs, ssem, rsem):
    x, y, c = _place()
    me = 2 * x + y
    copies = []
    for ai, buf in enumerate(bufs):
        for k, (ox, oy) in enumerate(_other_chips(x, y)):
            copies.append(_remote(buf.at[me], buf.at[me], ssem.at[3 * ai + k], rsem.at[3 * ai + k], (ox, oy, c)))
    return copies


def _reduce_copies(grads, lands, ssem, rsem):
    x, y, c = _place()
    copies = []
    for a, (gr, land) in enumerate(zip(grads, lands)):
        for mask in range(1, N_DEV):
            px, py, pc = _peer(x, y, c, mask)
            copies.append(_remote(gr.at[pc, 2 * px + py], land.at[mask - 1], ssem.at[7 * a + mask - 1],
                                  rsem.at[7 * a + mask - 1], (px, py, pc)))
    return copies


def _half_copies(totals, ssem, rsem):
    x, y, c = _place()
    return [_remote(t.at[c], t.at[c], ssem.at[a], rsem.at[a], (x, y, 1 - c)) for a, t in enumerate(totals)]


def _gather_start(name, groups):
    flat = [s for grp in groups for s in grp]
    n, ng = len(flat), len(groups)

    def body(*refs):
        ins = refs[:n]
        sems = refs[n:n + 2 * ng]
        token = refs[-1]
        idx = 0
        for gi, grp in enumerate(groups):
            for cp in _gather_copies(ins[idx:idx + len(grp)], sems[2 * gi], sems[2 * gi + 1]):
                cp.start()
            idx += len(grp)
        token[...] = jnp.zeros_like(token)

    tok_shape, tok_spec = _token_spec()
    sem_shapes = []
    for grp in groups:
        sem_shapes += [pltpu.SemaphoreType.DMA((3 * len(grp),))] * 2
    res = pl.pallas_call(
        body, name=name,
        out_shape=(*sem_shapes, *[pltpu.HBM(s.shape, s.dtype) for s in flat], tok_shape),
        in_specs=[HBM] * n, out_specs=(*[SEM] * (2 * ng), *[HBM] * n, tok_spec),
        input_output_aliases={i: 2 * ng + i for i in range(n)},
        compiler_params=pltpu.CompilerParams(has_side_effects=EFFECT))(*[_in_hbm(s) for s in flat])
    out, idx = [], 2 * ng
    for gi, grp in enumerate(groups):
        out.append((res[2 * gi], res[2 * gi + 1], list(res[idx:idx + len(grp)])))
        idx += len(grp)
    return out, res[-1]


def _gather_wait(name, ssem, rsem, slabs, after):
    n = len(slabs)

    def body(*refs):
        for cp in _gather_copies(refs[:n], refs[n], refs[n + 1]):
            cp.wait_send()
            cp.wait_recv()

    return pl.pallas_call(
        body, name=name, out_shape=tuple(pltpu.HBM(s.shape, s.dtype) for s in slabs),
        in_specs=[HBM] * n + [SEM, SEM] + [ANY] * len(after), out_specs=tuple([HBM] * n),
        input_output_aliases={i: i for i in range(n)},
        compiler_params=pltpu.CompilerParams(has_side_effects=EFFECT))(*slabs, ssem, rsem, *after)


def _reduce_start(name, grads):
    n = len(grads)
    lands = [lax.empty((N_DEV - 1, *g.shape[2:]), g.dtype) for g in grads]

    def body(*refs):
        token = refs[-1]
        for cp in _reduce_copies(refs[:n], refs[n:2 * n], refs[2 * n], refs[2 * n + 1]):
            cp.start()
        token[...] = jnp.zeros_like(token)

    tok_shape, tok_spec = _token_spec()
    sems = [pltpu.SemaphoreType.DMA((7 * n,))] * 2
    res = pl.pallas_call(
        body, name=name,
        out_shape=(*sems, *[pltpu.HBM(g.shape, g.dtype) for g in grads], *[pltpu.HBM(l.shape, l.dtype) for l in lands],
                   tok_shape),
        in_specs=[HBM] * (2 * n), out_specs=(SEM, SEM, *[HBM] * (2 * n), tok_spec),
        input_output_aliases={i: 2 + i for i in range(2 * n)},
        compiler_params=pltpu.CompilerParams(has_side_effects=EFFECT))(*[_in_hbm(t) for t in (*grads, *lands)])
    return res[0], res[1], list(res[2:2 + n]), list(res[2 + n:2 + 2 * n]), res[-1]


def _reduce_wait(name, ssem, rsem, grads, lands, after):
    n = len(grads)

    def body(*refs):
        for cp in _reduce_copies(refs[:n], refs[n:2 * n], refs[2 * n], refs[2 * n + 1]):
            cp.wait_send()
            cp.wait_recv()

    res = pl.pallas_call(
        body, name=name, out_shape=tuple(pltpu.HBM(t.shape, t.dtype) for t in (*grads, *lands)),
        in_specs=[HBM] * (2 * n) + [SEM, SEM] + [ANY] * len(after), out_specs=tuple([HBM] * (2 * n)),
        input_output_aliases={i: i for i in range(2 * n)},
        compiler_params=pltpu.CompilerParams(has_side_effects=EFFECT))(*grads, *lands, ssem, rsem, *after)
    return list(res[:n]), list(res[n:])


def _sum_pieces(name, grad, land, place):
    _, _, rows, cols = grad.shape
    tr = _pick_rows(rows, 256)

    def body(p_ref, g_ref, l_ref, o_ref):
        tot = g_ref[...].astype(F32)
        for k in range(N_DEV - 1):
            tot = tot + l_ref[k].astype(F32)
        o_ref[...] = tot

    grid_spec = pltpu.PrefetchScalarGridSpec(
        num_scalar_prefetch=1, grid=(rows // tr,),
        in_specs=[pl.BlockSpec((None, None, tr, cols), lambda i, p: (p[0], p[1], i, 0)),
                  pl.BlockSpec((N_DEV - 1, tr, cols), lambda i, p: (0, i, 0))],
        out_specs=pl.BlockSpec((None, tr, cols), lambda i, p: (p[0], i, 0)))
    return pl.pallas_call(body, out_shape=jax.ShapeDtypeStruct((2, rows, cols), F32), grid_spec=grid_spec, name=name,
                          compiler_params=_params(("parallel",)))(place, grad, land)


def _half_start(name, totals):
    n = len(totals)

    def body(*refs):
        token = refs[-1]
        for cp in _half_copies(refs[:n], refs[n], refs[n + 1]):
            cp.start()
        token[...] = jnp.zeros_like(token)

    tok_shape, tok_spec = _token_spec()
    res = pl.pallas_call(
        body, name=name,
        out_shape=(pltpu.SemaphoreType.DMA((n,)), pltpu.SemaphoreType.DMA((n,)),
                   *[pltpu.HBM(t.shape, t.dtype) for t in totals], tok_shape),
        in_specs=[HBM] * n, out_specs=(SEM, SEM, *[HBM] * n, tok_spec),
        input_output_aliases={i: 2 + i for i in range(n)},
        compiler_params=pltpu.CompilerParams(has_side_effects=EFFECT))(*[_in_hbm(t) for t in totals])
    return res[0], res[1], list(res[2:2 + n]), res[-1]


def _half_wait(name, ssem, rsem, totals, after):
    n = len(totals)

    def body(*refs):
        for cp in _half_copies(refs[:n], refs[n], refs[n + 1]):
            cp.wait_send()
            cp.wait_recv()

    res = pl.pallas_call(
        body, name=name, out_shape=tuple(pltpu.HBM(t.shape, t.dtype) for t in totals),
        in_specs=[HBM] * n + [SEM, SEM] + [ANY] * len(after), out_specs=tuple([HBM] * n),
        input_output_aliases={i: i for i in range(n)},
        compiler_params=pltpu.CompilerParams(has_side_effects=EFFECT))(*totals, ssem, rsem, *after)
    return list(res)


def _small_copies(bufs, ssem, rsem):
    x, y, c = _place()
    mine = bufs[0].at[4 * x + 2 * y + c]
    return [_remote(mine, mine, ssem.at[mask - 1], rsem.at[mask - 1], _peer(x, y, c, mask)) for mask in range(1, N_DEV)]


def _small_start(name, slots):
    def body(s_ref, ssem, rsem, thru, token):
        for cp in _small_copies([s_ref], ssem, rsem):
            cp.start()
        token[...] = jnp.zeros_like(token)

    tok_shape, tok_spec = _token_spec()
    sems = [pltpu.SemaphoreType.DMA((N_DEV - 1,))] * 2
    return pl.pallas_call(
        body, name=name, out_shape=(*sems, pltpu.HBM(slots.shape, slots.dtype), tok_shape), in_specs=[HBM],
        out_specs=(SEM, SEM, HBM, tok_spec), input_output_aliases={0: 2},
        compiler_params=pltpu.CompilerParams(has_side_effects=EFFECT))(_in_hbm(slots))


def _small_wait(name, ssem, rsem, slots, after):
    def body(*refs):
        for cp in _small_copies([refs[0]], refs[1], refs[2]):
            cp.wait_send()
            cp.wait_recv()

    return pl.pallas_call(
        body, name=name, out_shape=pltpu.HBM(slots.shape, slots.dtype), in_specs=[HBM, SEM, SEM] + [ANY] * len(after),
        out_specs=HBM, input_output_aliases={0: 0},
        compiler_params=pltpu.CompilerParams(has_side_effects=EFFECT))(slots, ssem, rsem, *after)


def _own_slot(small, me):
    return lax.dynamic_update_slice(jnp.zeros((N_DEV, *small.shape), small.dtype), small[None], (me, 0, 0))


def _sum_devices(name, stacked):
    _, rows, lanes = stacked.shape
    tr = _pick_rows(rows, 512)

    def body(s_ref, o_ref):
        tot = s_ref[0]
        for k in range(1, N_DEV):
            tot = tot + s_ref[k]
        o_ref[...] = tot

    return pl.pallas_call(
        body, out_shape=jax.ShapeDtypeStruct((rows, lanes), F32), grid=(rows // tr,),
        in_specs=[pl.BlockSpec((N_DEV, tr, lanes), lambda i: (0, i, 0))], out_specs=pl.BlockSpec((tr, lanes), lambda i: (i, 0)),
        name=name, compiler_params=_params(("parallel",)))(stacked)


def _adamw(name, w, g, m, v, layer, prev=None):
    layers, rows, cols = w.shape
    tr = _pick_rows(rows, 256)
    c1 = 1.0 - ADAM_B1 ** ADAM_STEP
    c2 = 1.0 - ADAM_B2 ** ADAM_STEP

    def body(w_ref, g_ref, m_ref, v_ref, *rest):
        go_ref, d_ref, nm_ref, nv_ref = rest[-4:]
        gv = g_ref[...]
        nm = ADAM_B1 * m_ref[...] + (1.0 - ADAM_B1) * gv
        nv = ADAM_B2 * v_ref[...] + (1.0 - ADAM_B2) * (gv * gv)
        go_ref[...] = gv
        d_ref[...] = -ADAM_LR * ((nm / c1) / (jnp.sqrt(nv / c2) + ADAM_EPS) + ADAM_WD * w_ref[...])
        nm_ref[...] = nm
        nv_ref[...] = nv

    spec = pl.BlockSpec((None, tr, cols), lambda i: (layer, i, 0))
    prev = list(prev) if prev is not None else []
    return pl.pallas_call(
        body, out_shape=[jax.ShapeDtypeStruct((layers, rows, cols), F32)] * 4, grid=(rows // tr,),
        in_specs=[spec, pl.BlockSpec((tr, cols), lambda i: (i, 0)), spec, spec] + [ANY] * len(prev),
        out_specs=[spec] * 4, input_output_aliases={4 + i: i for i in range(len(prev))}, name=name,
        compiler_params=_params(("parallel",)))(w, g, m, v, *prev)


def _pack(vectors, pad_rows):
    flat = jnp.concatenate([t.reshape(-1) for t in vectors])
    rows = -(-flat.shape[0] // LANES)
    rows = -(-rows // pad_rows) * pad_rows
    return jnp.pad(flat, (0, rows * LANES - flat.shape[0])).reshape(rows, LANES)


def _unpack(packed, shapes):
    flat = packed.reshape(-1)
    out, off = [], 0
    for shp in shapes:
        size = math.prod(shp)
        out.append(flat[off:off + size].reshape(shp))
        off += size
    return out


def kernel(x, mem, positions, mix_norm, mem_norm, w_mem_kv, ffn_norm, w_gate, w_up, w_down, attn_w_in, attn_w_out, sgu_w_in, sgu_ln_g, sgu_ln_b, sgu_w_spatial, sgu_b_spatial, sgu_w_out, final_norm, loss_target, m_mix_norm, m_mem_norm, m_w_mem_kv, m_ffn_norm, m_w_gate, m_w_up, m_w_down, m_attn_w_in, m_attn_w_out, m_sgu_w_in, m_sgu_ln_g, m_sgu_ln_b, m_sgu_w_spatial, m_sgu_b_spatial, m_sgu_w_out, m_final_norm, v_mix_norm, v_mem_norm, v_w_mem_kv, v_ffn_norm, v_w_gate, v_w_up, v_w_down, v_attn_w_in, v_attn_w_out, v_sgu_w_in, v_sgu_ln_g, v_sgu_ln_b, v_sgu_w_spatial, v_sgu_b_spatial, v_sgu_w_out, v_final_norm):
    d_model = x.shape[2]
    x0, mem0, tgt = x[0], mem[0], loss_target[0]
    xi, yi, ci = _place()
    chip = 2 * xi + yi
    place = jnp.stack([ci, chip]).astype(jnp.int32)

    given_w = dict(mix_norm=mix_norm, mem_norm=mem_norm, w_mem_kv=w_mem_kv, ffn_norm=ffn_norm, w_gate=w_gate, w_up=w_up,
                   w_down=w_down, attn_w_in=attn_w_in, attn_w_out=attn_w_out, sgu_w_in=sgu_w_in, sgu_ln_g=sgu_ln_g,
                   sgu_ln_b=sgu_ln_b, sgu_w_spatial=sgu_w_spatial, sgu_b_spatial=sgu_b_spatial, sgu_w_out=sgu_w_out,
                   final_norm=final_norm)
    given_m = dict(mix_norm=m_mix_norm, mem_norm=m_mem_norm, w_mem_kv=m_w_mem_kv, ffn_norm=m_ffn_norm, w_gate=m_w_gate,
                   w_up=m_w_up, w_down=m_w_down, attn_w_in=m_attn_w_in, attn_w_out=m_attn_w_out, sgu_w_in=m_sgu_w_in,
                   sgu_ln_g=m_sgu_ln_g, sgu_ln_b=m_sgu_ln_b, sgu_w_spatial=m_sgu_w_spatial,
                   sgu_b_spatial=m_sgu_b_spatial, sgu_w_out=m_sgu_w_out, final_norm=m_final_norm)
    given_v = dict(mix_norm=v_mix_norm, mem_norm=v_mem_norm, w_mem_kv=v_w_mem_kv, ffn_norm=v_ffn_norm, w_gate=v_w_gate,
                   w_up=v_w_up, w_down=v_w_down, attn_w_in=v_attn_w_in, attn_w_out=v_attn_w_out, sgu_w_in=v_sgu_w_in,
                   sgu_ln_g=v_sgu_ln_g, sgu_ln_b=v_sgu_ln_b, sgu_w_spatial=v_sgu_w_spatial,
                   sgu_b_spatial=v_sgu_b_spatial, sgu_w_out=v_sgu_w_out, final_norm=v_final_norm)

    units = {"attn_w_in": ("attn_w_in", 0, "col"), "w_mem_kv0": ("w_mem_kv", 0, "row"), "attn_w_out": ("attn_w_out", 0, "col"),
             "w_gate0": ("w_gate", 0, "col"), "w_up0": ("w_up", 0, "col"), "w_down0": ("w_down", 0, "row"),
             "sgu_w_in": ("sgu_w_in", 0, "col"), "w_mem_kv1": ("w_mem_kv", 1, "row"), "sgu_w_out": ("sgu_w_out", 0, "row"),
             "w_gate1": ("w_gate", 1, "col"), "w_up1": ("w_up", 1, "col"), "w_down1": ("w_down", 1, "row")}
    gather_groups = [["attn_w_in"], ["w_mem_kv0", "attn_w_out"], ["w_gate0", "w_up0"],
                     ["w_down0", "sgu_w_in", "w_mem_kv1", "ln"], ["sgu_w_out", "w_gate1", "w_up1"], ["w_down1"]]

    first = _place_shard("place_attn_w_in", attn_w_in, 0, place, BF)
    in_flight, token = _gather_start("gather_start_0", [[first]])
    slabs = {u: _place_shard(f"place_{u}", given_w[arr], layer, place, BF, after=[token])
             for u, (arr, layer, _) in units.items() if u != "attn_w_in"}
    slabs["ln"] = _place_shard("place_ln", jnp.concatenate([sgu_ln_g, sgu_ln_b])[None], 0, place, F32, after=[token])
    rest, token = _gather_start("gather_start_1", [[slabs[u] for u in grp] for grp in gather_groups[1:]])
    in_flight += rest
    weights = {}

    def arrive(gi, after):
        ssem, rsem, arrs = in_flight[gi]
        for u, full in zip(gather_groups[gi], _gather_wait(f"gather_wait_{gi}", ssem, rsem, arrs, after)):
            weights[u] = full if u == "ln" else Weight(full, units[u][2])

    w_sp = sgu_w_spatial[0]
    b_t = sgu_b_spatial[0].T
    tables = _rope_tables(positions[0])

    def residual(acc, extra):
        return [extra[0] + acc[0]]

    def memory_kv(layer):
        mem_n = _rms_fwd(f"mem_norm_{layer}", mem0, mem_norm[layer:layer + 1])
        return mem_n, _mm_nn(f"mem_kv_{layer}", mem_n, weights[f"w_mem_kv{layer}"])[0]

    h0 = _rms_fwd("mix_norm_0", x0, mix_norm[0:1], after=[token])
    arrive(0, [h0])
    proj0 = _mm_nn("attn_in", h0, weights["attn_w_in"])[0]
    arrive(1, [proj0])
    qkv = _rope_fwd(proj0, tables)
    qs, ks, vs = qkv[0:3], qkv[3:6], qkv[6:9]
    outs, lses = [], []
    for g in range(len(DILATIONS)):
        o, l = _dil_fwd(g, qs[g], ks[g], vs[g])
        outs.append(o)
        lses.append(l)
    merged, lse = _attn_merge(outs, lses)
    mem_n0, kv0 = memory_kv(0)
    cat0 = _mem_fwd("mem_fwd_0", proj0, 9, kv0, merged, 1)
    x1 = _mm_nn("attn_out", cat0, weights["attn_w_out"], extras=[x0], epilogue=residual)[0]
    arrive(2, [x1])
    hf0 = _rms_fwd("ffn_norm_0", x1, ffn_norm[0:1])
    g0, u0, act0 = _gate_up("gate_up_0", hf0, weights["w_gate0"], weights["w_up0"])
    arrive(3, [act0])
    x2 = _mm_nn("down_0", act0, weights["w_down0"], extras=[x1], epilogue=residual)[0]

    ln_all = weights["ln"]
    ln_g = ln_all[:, 0, :].reshape(1, SGU_W)
    ln_b = ln_all[:, 1, :].reshape(1, SGU_W)
    h1 = _rms_fwd("mix_norm_1", x2, mix_norm[1:2])
    proj1 = _mm_nn("sgu_in", h1, weights["sgu_w_in"])[0]
    arrive(4, [proj1])
    sgu_out = _sgu_fwd(proj1, ln_g, ln_b, w_sp, b_t)
    mem_n1, kv1 = memory_kv(1)
    cat1 = _mem_fwd("mem_fwd_1", proj1, 6, kv1, sgu_out, 3)
    x3 = _mm_nn("sgu_out", cat1, weights["sgu_w_out"], extras=[x2], epilogue=residual)[0]
    hf1 = _rms_fwd("ffn_norm_1", x3, ffn_norm[1:2])
    g1, u1, act1 = _gate_up("gate_up_1", hf1, weights["w_gate1"], weights["w_up1"])
    arrive(5, [act1])
    x4 = _mm_nn("down_1", act1, weights["w_down1"], extras=[x3], epilogue=residual)[0]

    d4, g_final, loss_part = _final_loss(x4, tgt, final_norm.reshape(1, d_model))
    loss = lax.psum(loss_part[0, 0], ("x", "y", "c"))

    outputs = {}

    def start_reduce(tag, names, grads):
        ssem, rsem, grads, lands, tok = _reduce_start(f"reduce_start_{tag}", grads)
        return dict(tag=tag, names=names, ssem=ssem, rsem=rsem, grads=grads, lands=lands), tok

    def finish_reduce(st, after):
        grads, lands = _reduce_wait(f"reduce_wait_{st['tag']}", st["ssem"], st["rsem"], st["grads"], st["lands"], after)
        totals = [_sum_pieces(f"sum_{u}", g, l, place) for u, g, l in zip(st["names"], grads, lands)]
        ssem, rsem, totals, tok = _half_start(f"half_start_{st['tag']}", totals)
        return dict(tag=st["tag"], names=st["names"], ssem=ssem, rsem=rsem, totals=totals), tok

    def finish_update(st, after):
        totals = _half_wait(f"half_wait_{st['tag']}", st["ssem"], st["rsem"], st["totals"], after)
        for u, tot in zip(st["names"], totals):
            arr, layer, _ = units[u]
            w = given_w[arr]
            outputs[arr] = _adamw(f"adamw_{u}", w, tot.reshape(w.shape[1:]), given_m[arr], given_v[arr], layer,
                                  outputs.get(arr))

    def ffn_bwd(layer, d_out, xin, h, g, u, act):
        wd, wg, wu = weights[f"w_down{layer}"], weights[f"w_gate{layer}"], weights[f"w_up{layer}"]
        gr_down = _mm_tn(f"d_down_{layer}", act, d_out, wd)
        dg, du = _mm_nt(f"d_act_{layer}", [d_out], [wd], out_dtypes=(BF, BF), extras=[g, u], epilogue=_swiglu_bwd_epilogue,
                        col_chunk=EPILOGUE_CHUNK)
        gr_gate = _mm_tn(f"d_gate_{layer}", h, dg, wg)
        gr_up = _mm_tn(f"d_up_{layer}", h, du, wu)
        st, tok = start_reduce(f"ffn{layer}", [f"w_down{layer}", f"w_gate{layer}", f"w_up{layer}"], [gr_down, gr_gate, gr_up])
        dh = _mm_nt(f"d_ffn_h_{layer}", [dg, du], [wg, wu], after=[tok])[0]
        d_in, g_norm = _rms_bwd(f"ffn_norm_bwd_{layer}", xin, ffn_norm[layer:layer + 1], dh, d_out)
        return st, d_in, g_norm, dg

    def memory_bwd(layer, mem_n, dkv):
        dkv = dkv.astype(BF)
        wkv = weights[f"w_mem_kv{layer}"]
        gr = _mm_tn(f"d_mem_kv_{layer}", mem_n, dkv, wkv)
        d_mem_n = _mm_nt(f"d_mem_n_{layer}", [dkv], [wkv])[0]
        return gr, _rms_bwd(f"mem_norm_bwd_{layer}", mem0, mem_norm[layer:layer + 1], d_mem_n)[1]

    st_ffn1, d3, g_ffn1, _ = ffn_bwd(1, d4, x3, hf1, g1, u1, act1)
    gr_sgu_out = _mm_tn("d_sgu_out", cat1, d3, weights["sgu_w_out"])
    dcat1 = _mm_nt("d_cat_1", [d3], [weights["sgu_w_out"]])[0]
    st_ffn1, tok = finish_reduce(st_ffn1, [dcat1])
    dproj1, dkv1 = _mem_bwd("mem_bwd_1", proj1, 6, kv1, dcat1, 3, proj1.shape[1])
    gr_kv1, g_mem1 = memory_bwd(1, mem_n1, dkv1)
    dproj1, g_wsp, g_bsp_t, g_ln_g, g_ln_b = _sgu_bwd(proj1, dcat1, ln_g, ln_b, w_sp, b_t, dproj1)
    gr_sgu_in = _mm_tn("d_sgu_in", h1, dproj1, weights["sgu_w_in"], after=[tok])
    finish_update(st_ffn1, [gr_sgu_in])
    st_mix1, tok = start_reduce("mix1", ["sgu_w_out", "w_mem_kv1", "sgu_w_in"], [gr_sgu_out, gr_kv1, gr_sgu_in])
    dh1 = _mm_nt("d_h_1", [dproj1], [weights["sgu_w_in"]], after=[tok])[0]
    d2, g_mix1 = _rms_bwd("mix_norm_bwd_1", x2, mix_norm[1:2], dh1, d3)

    st_ffn0, d1, g_ffn0, dg0 = ffn_bwd(0, d2, x1, hf0, g0, u0, act0)
    dev = 4 * xi + 2 * yi + ci
    small_a = [g_mix1, g_mem1, jnp.concatenate([g_ffn0, g_ffn1]), g_wsp, g_bsp_t[:, :SGU_GROUPS].T, g_final, g_ln_g, g_ln_b]
    sa_ssem, sa_rsem, sa_slots, tok = _small_start("small_start_a", _own_slot(_pack(small_a, LANES), dev))
    gr_attn_out = _mm_tn("d_attn_out", cat0, d1, weights["attn_w_out"], after=[tok])
    st_mix1, tok = finish_reduce(st_mix1, [gr_attn_out])
    dcat0 = _mm_nt("d_cat_0", [d1], [weights["attn_w_out"]], after=[tok])[0]
    dproj0, dkv0 = _mem_bwd("mem_bwd_0", proj0, 9, kv0, dcat0, 1, proj0.shape[1])
    finish_update(st_mix1, [dkv0])
    gr_kv0, g_mem0 = memory_bwd(0, mem_n0, dkv0)
    st_ffn0, tok = finish_reduce(st_ffn0, [g_mem0])
    d_merged, delta = _attn_delta(dcat0, cat0, after=[tok])
    dqs, dks, dvs = [], [], []
    for g in range(len(DILATIONS)):
        dq, dk, dv = _dil_bwd(g, qs[g], ks[g], vs[g], d_merged[g], lse[g], delta[g])
        dqs.append(dq)
        dks.append(dk)
        dvs.append(dv)
    dproj0 = _rope_bwd(dqs + dks + dvs, tables, dproj0)
    finish_update(st_ffn0, [dproj0])
    gr_attn_in = _mm_tn("d_attn_in", h0, dproj0, weights["attn_w_in"])
    st_mix0, tok = start_reduce("mix0", ["attn_w_out", "w_mem_kv0", "attn_w_in"], [gr_attn_out, gr_kv0, gr_attn_in])
    dh0 = _mm_nt("d_h_0", [dproj0], [weights["attn_w_in"]], after=[tok])[0]
    d0, g_mix0 = _rms_bwd("mix_norm_bwd_0", x0, mix_norm[0:1], dh0, d1)

    small_b = [g_mix0, g_mem0]
    sb_ssem, sb_rsem, sb_slots, tok = _small_start("small_start_b", _own_slot(_pack(small_b, 8), dev))
    sa_slots = _small_wait("small_wait_a", sa_ssem, sa_rsem, sa_slots, [tok])
    g_mix1, g_mem1, g_ffn, g_wsp, g_bsp, g_final, g_ln_g, g_ln_b = _unpack(_sum_devices("small_sum_a", sa_slots),
                                                                           [t.shape for t in small_a])
    sb_slots = _small_wait("small_wait_b", sb_ssem, sb_rsem, sb_slots, [g_final])
    g_mix0, g_mem0 = _unpack(_sum_devices("small_sum_b", sb_slots), [t.shape for t in small_b])
    st_mix0, tok = finish_reduce(st_mix0, [g_mix0])
    g_mix, g_mem = jnp.concatenate([g_mix0, g_mix1]), jnp.concatenate([g_mem0, g_mem1])
    shard_w = sgu_ln_g.shape[-1]
    g_ln_g = lax.dynamic_slice_in_dim(g_ln_g, chip * shard_w, shard_w, axis=1)
    g_ln_b = lax.dynamic_slice_in_dim(g_ln_b, chip * shard_w, shard_w, axis=1)
    small_names = ["mix_norm", "mem_norm", "ffn_norm", "sgu_w_spatial", "sgu_b_spatial", "final_norm", "sgu_ln_g",
                   "sgu_ln_b"]
    small_g = [g_mix, g_mem, g_ffn, g_wsp, g_bsp, g_final, g_ln_g, g_ln_b]
    small_shapes = [given_w[k].shape for k in small_names]
    packed = [_pack(t, LANES) for t in ([given_w[k] for k in small_names], small_g, [given_m[k] for k in small_names],
                                    [given_v[k] for k in small_names])]
    small_out = _adamw("adamw_small", packed[0][None], packed[1], packed[2][None], packed[3][None], 0)
    finish_update(st_mix0, [small_out[0]])
    for k, gk, dk, mk, vk in zip(small_names, *[_unpack(t[0], small_shapes) for t in small_out]):
        outputs[k] = (gk, dk, mk, vk)

    order = ["mix_norm", "mem_norm", "w_mem_kv", "ffn_norm", "w_gate", "w_up", "w_down", "attn_w_in", "attn_w_out",
             "sgu_w_in", "sgu_ln_g", "sgu_ln_b", "sgu_w_spatial", "sgu_b_spatial", "sgu_w_out", "final_norm"]
    return (loss, d0[None], *[outputs[k][0] for k in order], *[outputs[k][1] for k in order],
            *[outputs[k][2] for k in order], *[outputs[k][3] for k in order])
```

```python
import math

import jax
import jax.numpy as jnp
from jax import lax
from jax.experimental import pallas as pl
from jax.experimental.pallas import tpu as pltpu

F32 = jnp.float32
BF = jnp.bfloat16
MESH = pl.DeviceIdType.MESH

HEAD_DIM = 128
MEM_HEADS = 4
MEM_W = MEM_HEADS * HEAD_DIM
GROUP_W = 4 * HEAD_DIM
DILATIONS = (1, 4, 16)
BLK = 128
SGU_GROUPS = 12
SGU_W = SGU_GROUPS * HEAD_DIM
ROT_HALF = 16
ROPE_THETA = 500000.0
NORM_EPS = 1e-6
LN_EPS = 1e-5
NEG_INF = -1e30
SCALE = HEAD_DIM ** -0.5
ADAM_LR, ADAM_B1, ADAM_B2, ADAM_EPS, ADAM_WD, ADAM_STEP = 0.001, 0.9, 0.999, 1e-08, 0.01, 10

VMEM_LIMIT = 48 * 2 ** 20
VMEM_TILE_BUDGET = 38 * 2 ** 20
N_CHIPS = 4
N_DEV = 8
LANES = 128
EPILOGUE_CHUNK = 256

NT_DIMS = (((1,), (1,)), ((), ()))
TN_DIMS = (((0,), (0,)), ((), ()))
NN_DIMS = (((1,), (0,)), ((), ()))

ANY = pl.BlockSpec(memory_space=pl.ANY)
HBM = pl.BlockSpec(memory_space=pltpu.HBM)
SEM = pl.BlockSpec(memory_space=pltpu.SEMAPHORE)
EFFECT = pltpu.SideEffectType.DATAFLOW_SIDE_EFFECTING


def _params(sem):
    return pltpu.CompilerParams(dimension_semantics=sem, vmem_limit_bytes=VMEM_LIMIT)


def _pick(n, cap):
    if n <= cap:
        return n
    best = None
    for t in range(LANES, cap + 1, LANES):
        if n % t == 0:
            best = t
    assert best is not None, (n, cap)
    return best


def _pick_rows(n, cap):
    t = min(n, cap)
    while n % t:
        t //= 2
    return t


def _mm(name, dims, a_list, a_specs, b_list, b_specs, pairs, n_acc, acc_shape, grid, extras, e_specs,
        out_shapes, out_specs, epilogue, after=(), col_chunk=None, store=None, shard_width=None):
    na, nb, ne, no = len(a_list), len(b_list), len(extras), len(out_shapes)
    nk = grid[-1]

    def products(a, b, cols=None):
        sums = [None] * n_acc
        for ai, bi, ci in pairs:
            bv = b[bi]
            if cols is None:
                bv = bv[...]
            elif dims == NT_DIMS:
                bv = bv[cols, :]
            else:
                bv = bv[:, cols]
            if bv.ndim == 3:
                bv = bv.reshape(-1, bv.shape[-1])
            prod = lax.dot_general(a[ai][...].astype(BF), bv.astype(BF), dims, preferred_element_type=F32)
            sums[ci] = prod if sums[ci] is None else sums[ci] + prod
        return sums

    def body(*refs):
        a = refs[:na]
        b = refs[na:na + nb]
        e = refs[na + nb:na + nb + ne]
        off = na + nb + ne + len(after)
        o = refs[off:off + no]
        acc = refs[off + no:]

        def finish(sums):
            outs = epilogue(sums, [r[...] for r in e])
            if store is not None:
                store(o, outs)
                return
            for r, v in zip(o, outs):
                r[...] = v.astype(r.dtype)

        if nk == 1 and shard_width:
            (ai, bi, _), = pairs
            av = a[ai][...].astype(BF)
            if dims == NT_DIMS:
                total = None
                for j in range(N_CHIPS):
                    cols = slice(j * shard_width, (j + 1) * shard_width)
                    prod = lax.dot_general(av[:, cols], b[bi][j].astype(BF), dims, preferred_element_type=F32)
                    total = prod if total is None else total + prod
                finish([total])
                return
            for j in range(N_CHIPS):
                cols = slice(j * shard_width, (j + 1) * shard_width)
                prod = lax.dot_general(av, b[bi][j].astype(BF), dims, preferred_element_type=F32)
                outs = epilogue([prod], [r[:, cols] for r in e])
                for r, v in zip(o, outs):
                    r[:, cols] = v.astype(r.dtype)
            return
        if nk == 1 and col_chunk:
            width = acc_shape[1]
            left = [r[...].astype(BF) for r in a]
            for c0 in range(0, width, col_chunk):
                cols = slice(c0, min(c0 + col_chunk, width))
                outs = epilogue(products(left, b, cols), [r[:, cols] for r in e])
                for r, v in zip(o, outs):
                    r[:, cols] = v.astype(r.dtype)
            return
        if nk == 1:
            finish(products(a, b))
            return
        k = pl.program_id(len(grid) - 1)

        @pl.when(k == 0)
        def _():
            for c, v in zip(acc, products(a, b)):
                c[...] = v

        @pl.when(jnp.logical_and(k > 0, k < nk - 1))
        def _():
            for c, v in zip(acc, products(a, b)):
                c[...] += v

        @pl.when(k == nk - 1)
        def _():
            finish([c[...] + v for c, v in zip(acc, products(a, b))])

    ins = [*a_list, *b_list, *extras, *after]
    in_specs = [*a_specs, *b_specs, *e_specs, *([ANY] * len(after))]
    sem = ("parallel",) * (len(grid) - 1) + ("arbitrary",)
    scratch = [] if nk == 1 else [pltpu.VMEM(acc_shape, F32)] * n_acc
    return pl.pallas_call(
        body, out_shape=out_shapes, grid=grid, in_specs=in_specs, out_specs=out_specs, scratch_shapes=scratch,
        name=name, compiler_params=_params(sem))(*ins)


def _tile_bytes(blocks, single=()):
    size = lambda s, d: math.prod(s) * jnp.dtype(d).itemsize
    return sum(2 * size(s, d) for s, d in blocks) + sum(size(s, d) for s, d in single)


def _first(acc, extra):
    return [acc[0]]


class Weight:
    def __init__(self, arr, axis):
        self.arr, self.axis = arr, axis
        _, self.rows, self.cols = arr.shape


SMALL_WEIGHT_BYTES = 8 * 2 ** 20


def _is_small(w):
    return w.arr.size * w.arr.dtype.itemsize <= SMALL_WEIGHT_BYTES


def _mm_nn(name, a, w, extras=(), epilogue=_first, out_dtypes=(F32,), after=()):
    m, kdim = a.shape
    b_spec, shard_width = None, None
    if w.axis == "col" and _is_small(w):
        n_total = tn = N_CHIPS * w.cols
        tk, gn, gk = kdim, 1, 1
        shard_width = w.cols
        b_spec = pl.BlockSpec((N_CHIPS, kdim, w.cols), lambda n, i, k: (0, 0, 0))
    elif w.axis == "col":
        n_total = N_CHIPS * w.cols
        tn = _pick(w.cols, 1408)
        tk = _pick(kdim, 2048)
        ncb = w.cols // tn
        gn, gk = N_CHIPS * ncb, kdim // tk
        b_map = lambda n, i, k: (n // ncb, k, n % ncb)
    elif kdim <= 2048:
        n_total = w.cols
        tn = _pick(w.cols, 1024)
        tk = kdim
        gn, gk = n_total // tn, 1
        b_spec = pl.BlockSpec((N_CHIPS, w.rows, tn), lambda n, i, k: (0, 0, n))
    else:
        n_total = w.cols
        tn = _pick(w.cols, 1024)
        tk = _pick(w.rows, 1408)
        nkb = w.rows // tk
        gn, gk = n_total // tn, N_CHIPS * nkb
        b_map = lambda n, i, k: (k // nkb, k % nkb, n)
    if b_spec is None:
        b_spec = pl.BlockSpec((None, tk, tn), b_map)
    for tm in (1024, 512, 256, 128):
        if m % tm:
            continue
        blocks = [((tm, tk), a.dtype), ((tk, tn), BF)] + [((tm, tn), e.dtype) for e in extras]
        blocks += [((tm, tn), d) for d in out_dtypes] + [((tm, tn), BF)]
        if _tile_bytes(blocks) <= VMEM_TILE_BUDGET:
            break
    o_spec = pl.BlockSpec((tm, tn), lambda n, i, k: (i, n))
    return _mm(
        name, NN_DIMS, [a], [pl.BlockSpec((tm, tk), lambda n, i, k: (i, k))],
        [w.arr], [b_spec], [(0, 0, 0)], 1, (tm, tn), (gn, m // tm, gk),
        list(extras), [o_spec] * len(extras),
        [jax.ShapeDtypeStruct((m, n_total), d) for d in out_dtypes], [o_spec] * len(out_dtypes), epilogue, after,
        shard_width=shard_width)


def _gate_up(name, h, wg, wu):
    m, kdim = h.shape
    tn = _pick(wg.cols, 1408)
    tk = _pick(kdim, 2048)
    ncb = wg.cols // tn
    single = kdim == tk
    for tm in (1024, 512, 256, 128):
        blocks = [((tm, tk), BF)] + [((tm, tn), BF)] * 3
        weights = [((tk, tn), BF)] * 2
        if m % tm == 0 and _tile_bytes(blocks + ([] if single else weights), weights if single else ()) <= VMEM_TILE_BUDGET:
            break
    b_spec = pl.BlockSpec((None, tk, tn), lambda n, i, k: (n // ncb, k, n % ncb),
                          pipeline_mode=pl.Buffered(1) if single else None)
    o_spec = pl.BlockSpec((tm, tn), lambda n, i, k: (i, n))
    n_total = N_CHIPS * wg.cols

    def epilogue(acc, extra):
        g, u = acc
        return [g, u, g * (1.0 / (1.0 + jnp.exp(-g))) * u]

    return _mm(
        name, NN_DIMS, [h], [pl.BlockSpec((tm, tk), lambda n, i, k: (i, k))], [wg.arr, wu.arr], [b_spec, b_spec],
        [(0, 0, 0), (0, 1, 1)], 2, (tm, tn), (N_CHIPS * ncb, m // tm, kdim // tk), [], [],
        [jax.ShapeDtypeStruct((m, n_total), BF)] * 3, [o_spec] * 3, epilogue, col_chunk=EPILOGUE_CHUNK)


def _mm_nt(name, dys, ws, out_dtypes=(F32,), extras=(), epilogue=_first, after=(), col_chunk=None):
    m = dys[0].shape[0]
    w0 = ws[0]
    npair = len(dys)
    b_spec, shard_width = None, None
    if w0.axis == "col" and npair == 1 and _is_small(w0):
        k_total = tko = w0.rows
        tkc = N_CHIPS * w0.cols
        go, gk = 1, 1
        shard_width = w0.cols
        b_spec = pl.BlockSpec(w0.arr.shape, lambda o, i, k: (0, 0, 0))
    elif w0.axis == "col":
        k_total = w0.rows
        tko = _pick(k_total, 1024)
        tkc = _pick(w0.cols, 1408)
        nkb = w0.cols // tkc
        go, gk = k_total // tko, N_CHIPS * nkb
        b_map = lambda o, i, k: (k // nkb, o, k % nkb)
    else:
        k_total = N_CHIPS * w0.rows
        tko = _pick(w0.rows, 1408)
        tkc = _pick(w0.cols, 2048)
        nob = w0.rows // tko
        go, gk = N_CHIPS * nob, w0.cols // tkc
        b_map = lambda o, i, k: (o // nob, o % nob, k)
    single = gk == 1
    for tm in (1024, 512, 256, 128):
        if m % tm:
            continue
        blocks = [((tm, tkc), d.dtype) for d in dys]
        blocks += [((tm, tko), e.dtype) for e in extras] + [((tm, tko), d) for d in out_dtypes]
        blocks += [((tm, tko), BF)]
        weights = [((tko, tkc), BF)] * npair
        if _tile_bytes(blocks + ([] if single else weights), weights if single else ()) <= VMEM_TILE_BUDGET:
            break
    if b_spec is None:
        b_spec = pl.BlockSpec((None, tko, tkc), b_map, pipeline_mode=pl.Buffered(1) if single else None)
    o_spec = pl.BlockSpec((tm, tko), lambda o, i, k: (i, o))
    return _mm(
        name, NT_DIMS, list(dys), [pl.BlockSpec((tm, tkc), lambda o, i, k: (i, k))] * npair,
        [w.arr for w in ws], [b_spec] * npair,
        [(i, i, 0) for i in range(npair)], 1, (tm, tko), (go, m // tm, gk), list(extras), [o_spec] * len(extras),
        [jax.ShapeDtypeStruct((m, k_total), d) for d in out_dtypes], [o_spec] * len(out_dtypes), epilogue, after,
        col_chunk if gk == 1 and shard_width is None else None, shard_width=shard_width)


def _mm_tn(name, a, dy, w, after=()):
    m, k_total = a.shape
    rows2 = w.rows // 2
    tn = _pick(w.cols, 1408)
    ncb = w.cols // tn
    epilogue, store = _first, None
    if k_total <= 2048 and w.axis == "col" and _is_small(w):
        tkr, tn = k_total, N_CHIPS * w.cols
        gr, gn = 1, 1
        o_spec = pl.BlockSpec((2, N_CHIPS, rows2, w.cols), lambda r, n, t: (0, 0, 0, 0))

        def store(o_refs, outs):
            for j in range(N_CHIPS):
                for h in range(2):
                    o_refs[0][h, j] = outs[0][h * rows2:(h + 1) * rows2, j * w.cols:(j + 1) * w.cols].astype(BF)
    elif k_total <= 2048 and w.axis == "col":
        tkr = k_total
        gr, gn = 1, N_CHIPS * ncb
        o_spec = pl.BlockSpec((2, None, rows2, tn), lambda r, n, t: (0, n // ncb, 0, n % ncb))
        epilogue = lambda acc, extra: [acc[0].reshape(2, rows2, tn)]
    elif k_total <= 2048:
        tkr = k_total
        gr, gn = 1, ncb
        o_spec = pl.BlockSpec((2, N_CHIPS, rows2, tn), lambda r, n, t: (0, 0, 0, n))

        def store(o_refs, outs):
            for j in range(N_CHIPS):
                for h in range(2):
                    lo = (2 * j + h) * rows2
                    o_refs[0][h, j] = outs[0][lo:lo + rows2].astype(BF)
    elif rows2 % LANES:
        tkr = w.rows
        assert w.axis == "row"
        gr, gn = N_CHIPS, ncb
        o_spec = pl.BlockSpec((2, None, rows2, tn), lambda r, n, t: (0, r, 0, n))
        epilogue = lambda acc, extra: [acc[0].reshape(2, rows2, tn)]
    else:
        tkr = _pick(rows2, 1408)
        nrb = rows2 // tkr
        if w.axis == "col":
            gr, gn = w.rows // tkr, N_CHIPS * ncb
            o_map = lambda r, n, t: (r // nrb, n // ncb, r % nrb, n % ncb)
        else:
            per = w.rows // tkr
            gr, gn = N_CHIPS * per, ncb
            o_map = lambda r, n, t: ((r % per) // nrb, r // per, (r % per) % nrb, n)
        o_spec = pl.BlockSpec((None, None, tkr, tn), o_map)
    for tmk in (1024, 512, 256, 128):
        blocks = [((tmk, tkr), a.dtype), ((tmk, tn), dy.dtype), ((tkr, tn), BF), ((tkr, tn), BF)]
        if m % tmk == 0 and _tile_bytes(blocks) <= VMEM_TILE_BUDGET:
            break
    return _mm(
        name, TN_DIMS, [a], [pl.BlockSpec((tmk, tkr), lambda r, n, t: (t, r))],
        [dy], [pl.BlockSpec((tmk, tn), lambda r, n, t: (t, n))], [(0, 0, 0)], 1, (tkr, tn), (gr, gn, m // tmk), [], [],
        [jax.ShapeDtypeStruct((2, N_CHIPS, rows2, w.cols), BF)], [o_spec], epilogue, after, store=store)[0]


def _rms_fwd(name, x, g, after=()):
    s, d = x.shape
    tr = _pick_rows(s, 512)

    def body(x_ref, g_ref, *rest):
        h_ref = rest[-1]
        xf = x_ref[...]
        r = lax.rsqrt(jnp.mean(xf * xf, axis=-1, keepdims=True) + NORM_EPS)
        h_ref[...] = (xf * r * g_ref[...]).astype(BF)

    return pl.pallas_call(
        body, out_shape=jax.ShapeDtypeStruct((s, d), BF), grid=(s // tr,),
        in_specs=[pl.BlockSpec((tr, d), lambda i: (i, 0)), pl.BlockSpec((1, d), lambda i: (0, 0))] + [ANY] * len(after),
        out_specs=pl.BlockSpec((tr, d), lambda i: (i, 0)), name=name, compiler_params=_params(("parallel",)))(x, g, *after)


def _rms_bwd(name, x, g, dh, dres=None):
    s, d = x.shape
    tr = _pick_rows(s, 256)
    has_res = dres is not None

    def body(*refs):
        if has_res:
            x_ref, g_ref, dh_ref, dres_ref, dx_ref, dg_ref = refs
        else:
            x_ref, g_ref, dh_ref, dx_ref, dg_ref = refs
        xf = x_ref[...]
        r = lax.rsqrt(jnp.mean(xf * xf, axis=-1, keepdims=True) + NORM_EPS)
        xr = xf * r
        dy = dh_ref[...]
        a = dy * g_ref[...]
        dx = r * (a - xr * jnp.mean(a * xr, axis=-1, keepdims=True))
        if has_res:
            dx = dx + dres_ref[...]
        dx_ref[...] = dx

        @pl.when(pl.program_id(0) == 0)
        def _():
            dg_ref[...] = jnp.zeros_like(dg_ref)

        dg_ref[...] += jnp.sum(dy * xr, axis=0, keepdims=True)

    row = pl.BlockSpec((tr, d), lambda i: (i, 0))
    vec = pl.BlockSpec((1, d), lambda i: (0, 0))
    ins = [x, g, dh] + ([dres] if has_res else [])
    in_specs = [row, vec, row] + ([row] if has_res else [])
    return pl.pallas_call(
        body, out_shape=[jax.ShapeDtypeStruct((s, d), F32), jax.ShapeDtypeStruct((1, d), F32)], grid=(s // tr,),
        in_specs=in_specs, out_specs=[row, vec], name=name, compiler_params=_params(("arbitrary",)))(*ins)


def _final_loss(x, tgt, g):
    s, d = x.shape
    tr = _pick_rows(s, 256)

    def body(x_ref, t_ref, g_ref, dx_ref, dg_ref, loss_ref):
        xf = x_ref[...]
        gain = g_ref[...]
        r = lax.rsqrt(jnp.mean(xf * xf, axis=-1, keepdims=True) + NORM_EPS)
        xr = xf * r
        err = xr * gain - t_ref[...]
        dy = err * (1.0 / d)
        a = dy * gain
        dx_ref[...] = r * (a - xr * jnp.mean(a * xr, axis=-1, keepdims=True))

        @pl.when(pl.program_id(0) == 0)
        def _():
            dg_ref[...] = jnp.zeros_like(dg_ref)
            loss_ref[...] = jnp.zeros_like(loss_ref)

        dg_ref[...] += jnp.sum(dy * xr, axis=0, keepdims=True)
        part = 0.5 * jnp.sum(jnp.mean(err * err, axis=-1, keepdims=True), axis=0, keepdims=True)
        loss_ref[...] += jnp.broadcast_to(part, loss_ref.shape)

    row = pl.BlockSpec((tr, d), lambda i: (i, 0))
    vec = pl.BlockSpec((1, d), lambda i: (0, 0))
    return pl.pallas_call(
        body, out_shape=[jax.ShapeDtypeStruct((s, d), F32), jax.ShapeDtypeStruct((1, d), F32),
                         jax.ShapeDtypeStruct((8, LANES), F32)],
        grid=(s // tr,), in_specs=[row, row, vec], out_specs=[row, vec, pl.BlockSpec((8, LANES), lambda i: (0, 0))],
        name="final_loss", compiler_params=_params(("arbitrary",)))(x, tgt, g)


def _swiglu_bwd_epilogue(acc, extra):
    dact = acc[0]
    g, u = extra[0].astype(F32), extra[1].astype(F32)
    sig = 1.0 / (1.0 + jnp.exp(-g))
    return [dact * u * sig * (1.0 + g * (1.0 - sig)), dact * g * sig]


GELU_C = math.sqrt(2.0 / math.pi)
GELU_A = 0.044715


def _gelu(x):
    return 0.5 * x * (1.0 + jnp.tanh(GELU_C * (x + GELU_A * x * x * x)))


def _gelu_both(x):
    x2 = x * x
    t = jnp.tanh(GELU_C * (x + GELU_A * x2 * x))
    half = 0.5 * (1.0 + t)
    return x * half, half + 0.5 * x * (1.0 - t * t) * GELU_C * (1.0 + 3.0 * GELU_A * x2)


def _rope_tables(positions):
    inv_freq = ROPE_THETA ** (-jnp.arange(ROT_HALF, dtype=F32) / ROT_HALF)
    ang = positions.astype(F32)[:, None] * inv_freq
    cos, sin = jnp.cos(ang), jnp.sin(ang)
    s = ang.shape[0]
    rest = HEAD_DIM - 2 * ROT_HALF
    zeros = jnp.zeros((s, ROT_HALF), F32)
    cos_t = jnp.concatenate([cos, cos, jnp.ones((s, rest), F32)], axis=1)
    sin_a = jnp.concatenate([-sin, zeros, jnp.zeros((s, rest), F32)], axis=1)
    sin_b = jnp.concatenate([zeros, sin, jnp.zeros((s, rest), F32)], axis=1)
    return cos_t, sin_a, sin_b


def _rope_head(xh, cos_t, sin_a, sin_b):
    up = pltpu.roll(xh, HEAD_DIM - ROT_HALF, 1)
    down = pltpu.roll(xh, ROT_HALF, 1)
    return xh * cos_t + up * sin_a + down * sin_b


def _residue(r, rows, dil):
    return slice(None) if dil == 1 else pl.ds(r, rows, stride=dil)


ROPE_TILE = 256
N_PARTS = 9
HEADS_PER_GROUP = GROUP_W // HEAD_DIM
N_HEADS_IN = N_PARTS * HEADS_PER_GROUP


def _rope_fwd(proj, tables):
    s = proj.shape[0]
    tm = _pick_rows(s, ROPE_TILE)

    def body(*refs):
        heads = refs[:N_HEADS_IN]
        c_ref, sa_ref, sb_ref = refs[N_HEADS_IN:N_HEADS_IN + 3]
        outs = refs[N_HEADS_IN + 3:]
        for g, dil in enumerate(DILATIONS):
            rows = tm // dil
            for r in range(dil):
                rs = _residue(r, rows, dil)
                cos_t, sin_a, sin_b = c_ref[rs, :], sa_ref[rs, :], sb_ref[rs, :]
                for kind in range(3):
                    part = 3 * kind + g
                    for h in range(HEADS_PER_GROUP):
                        xh = heads[part * HEADS_PER_GROUP + h][rs, :]
                        if kind < 2:
                            xh = _rope_head(xh, cos_t, sin_a, sin_b)
                        outs[part][r, :, h * HEAD_DIM:(h + 1) * HEAD_DIM] = xh.astype(BF)

    tab = pl.BlockSpec((tm, HEAD_DIM), lambda i: (i, 0))
    head_specs = [pl.BlockSpec((tm, HEAD_DIM), lambda i, j=j: (i, j)) for j in range(N_HEADS_IN)]
    shapes, specs = [], []
    for part in range(N_PARTS):
        dil = DILATIONS[part % 3]
        shapes.append(jax.ShapeDtypeStruct((dil, s // dil, GROUP_W), BF))
        specs.append(pl.BlockSpec((dil, tm // dil, GROUP_W), lambda i: (0, i, 0)))
    return pl.pallas_call(
        body, out_shape=shapes, grid=(s // tm,), in_specs=head_specs + [tab, tab, tab], out_specs=specs,
        name="rope_fwd", compiler_params=_params(("parallel",)))(*([proj] * N_HEADS_IN), *tables)


def _rope_bwd(parts, tables, into):
    s = into.shape[0]
    tm = _pick_rows(s, ROPE_TILE)

    def body(*refs):
        ins = refs[:N_PARTS]
        c_ref, sa_ref, sb_ref, into_ref, o_ref, scr = refs[N_PARTS:]
        for g, dil in enumerate(DILATIONS):
            rows = tm // dil
            for r in range(dil):
                rs = _residue(r, rows, dil)
                cos_t, sin_a, sin_b = c_ref[rs, :], -sa_ref[rs, :], -sb_ref[rs, :]
                for kind in range(3):
                    part = 3 * kind + g
                    for h in range(HEADS_PER_GROUP):
                        xh = ins[part][r, :, h * HEAD_DIM:(h + 1) * HEAD_DIM]
                        if kind < 2:
                            xh = _rope_head(xh, cos_t, sin_a, sin_b)
                        scr[part * HEADS_PER_GROUP + h, rs, :] = xh
        for j in range(N_HEADS_IN):
            o_ref[:, j * HEAD_DIM:(j + 1) * HEAD_DIM] = scr[j].astype(BF)

    tab = pl.BlockSpec((tm, HEAD_DIM), lambda i: (i, 0))
    i_specs = [pl.BlockSpec((DILATIONS[p % 3], tm // DILATIONS[p % 3], GROUP_W), lambda i: (0, i, 0))
               for p in range(N_PARTS)]
    return pl.pallas_call(
        body, out_shape=jax.ShapeDtypeStruct(into.shape, into.dtype), grid=(s // tm,),
        in_specs=i_specs + [tab] * 3 + [ANY], out_specs=pl.BlockSpec((tm, N_PARTS * GROUP_W), lambda i: (i, 0)),
        scratch_shapes=[pltpu.VMEM((N_HEADS_IN, tm, HEAD_DIM), F32)], input_output_aliases={N_PARTS + 3: 0},
        name="rope_bwd", compiler_params=_params(("parallel",)))(*parts, *tables, into)


def _band_mask(n):
    qi = lax.broadcasted_iota(jnp.int32, (BLK, 2 * BLK), 0)
    ki = lax.broadcasted_iota(jnp.int32, (BLK, 2 * BLK), 1)
    prev = jnp.logical_and(jnp.logical_and(ki < BLK, ki >= qi), n > 0)
    return jnp.logical_or(prev, jnp.logical_and(ki >= BLK, qi >= ki - BLK))


Q_BLOCKS = 2
Q_ROWS = Q_BLOCKS * BLK


def _dil_specs(n_steps):
    last = n_steps - 1
    own = pl.BlockSpec((None, Q_ROWS, GROUP_W), lambda r, n: (r, jnp.minimum(n, last), 0))
    before = pl.BlockSpec((None, BLK, GROUP_W), lambda r, n: (r, jnp.maximum(Q_BLOCKS * n - 1, 0), 0))
    return own, before


def _dil_fwd(g, q, k, v):
    dil, length, _ = q.shape
    n_steps = length // Q_ROWS

    def body(q_ref, ko_ref, kb_ref, vo_ref, vb_ref, o_ref, lse_ref):
        n = pl.program_id(1)
        for h in range(GROUP_W // HEAD_DIM):
            sl = slice(h * HEAD_DIM, (h + 1) * HEAD_DIM)
            keys = jnp.concatenate([kb_ref[:, sl], ko_ref[:, sl]], axis=0)
            vals = jnp.concatenate([vb_ref[:, sl], vo_ref[:, sl]], axis=0)
            for j in range(Q_BLOCKS):
                rows, win = slice(j * BLK, (j + 1) * BLK), slice(j * BLK, (j + 2) * BLK)
                sc = lax.dot_general(q_ref[rows, sl], keys[win], NT_DIMS, preferred_element_type=F32) * SCALE
                sc = jnp.where(_band_mask(Q_BLOCKS * n + j), sc, NEG_INF)
                mx = jnp.max(sc, axis=-1, keepdims=True)
                p = jnp.exp(sc - mx)
                den = jnp.sum(p, axis=-1, keepdims=True)
                o_ref[rows, sl] = jnp.dot(p.astype(BF), vals[win], preferred_element_type=F32) / den
                lse_ref[rows, sl] = jnp.broadcast_to(mx + jnp.log(den), (BLK, HEAD_DIM))

    own, before = _dil_specs(n_steps)
    return pl.pallas_call(
        body, out_shape=[jax.ShapeDtypeStruct(q.shape, F32)] * 2, grid=(dil, n_steps),
        in_specs=[own, own, before, own, before], out_specs=[own, own], name=f"dil_fwd_{g}",
        compiler_params=_params(("parallel", "arbitrary")))(q, k, k, v, v)


def _dil_bwd(g, q, k, v, do, lse, delta):
    dil, length, _ = q.shape
    n_steps = length // Q_ROWS

    def body(q_ref, ko_ref, kb_ref, vo_ref, vb_ref, do_ref, lse_ref, dl_ref, dq_ref, dk_ref, dv_ref, ck_ref, cv_ref):
        n = pl.program_id(1)
        live = n < n_steps

        @pl.when(n == 0)
        def _():
            ck_ref[...] = jnp.zeros_like(ck_ref)
            cv_ref[...] = jnp.zeros_like(cv_ref)

        @pl.when(jnp.logical_not(live))
        def _():
            dk_ref[...] = ck_ref[...]
            dv_ref[...] = cv_ref[...]

        @pl.when(live)
        def _():
            for h in range(GROUP_W // HEAD_DIM):
                sl = slice(h * HEAD_DIM, (h + 1) * HEAD_DIM)
                keys = jnp.concatenate([kb_ref[:, sl], ko_ref[:, sl]], axis=0)
                vals = jnp.concatenate([vb_ref[:, sl], vo_ref[:, sl]], axis=0)
                dks, dvs = [], []
                for j in range(Q_BLOCKS):
                    rows, win = slice(j * BLK, (j + 1) * BLK), slice(j * BLK, (j + 2) * BLK)
                    qh, doh = q_ref[rows, sl], do_ref[rows, sl]
                    lse_h = lse_ref[rows, h * HEAD_DIM:h * HEAD_DIM + 1]
                    dl_h = dl_ref[rows, h * HEAD_DIM:h * HEAD_DIM + 1]
                    sc = lax.dot_general(qh, keys[win], NT_DIMS, preferred_element_type=F32) * SCALE
                    p = jnp.where(_band_mask(Q_BLOCKS * n + j), jnp.exp(jnp.minimum(sc - lse_h, 0.0)), 0.0)
                    dp = lax.dot_general(doh, vals[win], NT_DIMS, preferred_element_type=F32)
                    ds = (p * (dp - dl_h) * SCALE).astype(BF)
                    dq_ref[rows, sl] = jnp.dot(ds, keys[win], preferred_element_type=F32)
                    dks.append(lax.dot_general(ds, qh, TN_DIMS, preferred_element_type=F32))
                    dvs.append(lax.dot_general(p.astype(BF), doh, TN_DIMS, preferred_element_type=F32))
                for out_ref, carry, parts in ((dk_ref, ck_ref, dks), (dv_ref, cv_ref, dvs)):
                    out_ref[:Q_ROWS - BLK, sl] = carry[:Q_ROWS - BLK, sl]
                    out_ref[Q_ROWS - BLK:, sl] = carry[Q_ROWS - BLK:, sl] + parts[0][:BLK]
                    for j in range(Q_BLOCKS - 1):
                        carry[j * BLK:(j + 1) * BLK, sl] = parts[j][BLK:] + parts[j + 1][:BLK]
                    carry[Q_ROWS - BLK:, sl] = parts[-1][BLK:]

    own, before = _dil_specs(n_steps)
    behind = pl.BlockSpec((None, Q_ROWS, GROUP_W), lambda r, n: (r, jnp.maximum(n - 1, 0), 0))
    return pl.pallas_call(
        body, out_shape=[jax.ShapeDtypeStruct(q.shape, F32)] * 3, grid=(dil, n_steps + 1),
        in_specs=[own, own, before, own, before, own, own, own], out_specs=[own, behind, behind],
        scratch_shapes=[pltpu.VMEM((Q_ROWS, GROUP_W), F32)] * 2, name=f"dil_bwd_{g}",
        compiler_params=_params(("parallel", "arbitrary")))(q, k, k, v, v, do, lse, delta)


def _major_specs(s, tm, dtype):
    shapes = [jax.ShapeDtypeStruct((dil, s // dil, GROUP_W), dtype) for dil in DILATIONS]
    specs = [pl.BlockSpec((dil, tm // dil, GROUP_W), lambda i: (0, i, 0)) for dil in DILATIONS]
    return shapes, specs


def _attn_merge(outs, lses):
    s = outs[0].shape[1]
    tm = _pick_rows(s, ROPE_TILE)

    def body(o0, o1, o2, l0, l1, l2, m_ref, e0, e1, e2, so1, so2, sl1, sl2, se):
        for h in range(HEADS_PER_GROUP):
            sl = slice(h * HEAD_DIM, (h + 1) * HEAD_DIM)
            for dil, src, dst in ((DILATIONS[1], o1, so1), (DILATIONS[2], o2, so2), (DILATIONS[1], l1, sl1),
                                  (DILATIONS[2], l2, sl2)):
                for r in range(dil):
                    dst[h, _residue(r, tm // dil, dil), :] = src[r, :, sl]
            a, b, c = l0[0, :, sl], sl1[h], sl2[h]
            mx = jnp.maximum(jnp.maximum(a, b), c)
            ea, eb, ec = jnp.exp(a - mx), jnp.exp(b - mx), jnp.exp(c - mx)
            den = ea + eb + ec
            m_ref[:, sl] = ((ea * o0[0, :, sl] + eb * so1[h] + ec * so2[h]) / den).astype(BF)
            se[h] = mx + jnp.log(den)
            for dil, dst in zip(DILATIONS, (e0, e1, e2)):
                for r in range(dil):
                    dst[r, :, sl] = se[h, _residue(r, tm // dil, dil), :]

    shapes, specs = _major_specs(s, tm, F32)
    nat = pl.BlockSpec((tm, GROUP_W), lambda i: (i, 0))
    res = pl.pallas_call(
        body, out_shape=[jax.ShapeDtypeStruct((s, GROUP_W + MEM_W), BF)] + shapes, grid=(s // tm,), in_specs=specs * 2,
        out_specs=[nat] + specs, scratch_shapes=[pltpu.VMEM((HEADS_PER_GROUP, tm, HEAD_DIM), F32)] * 5,
        name="attn_merge", compiler_params=_params(("parallel",)))(*outs, *lses)
    return res[0], res[1:]


def _attn_delta(dcat, merged, after=()):
    s = merged.shape[0]
    tm = _pick_rows(s, ROPE_TILE)

    def body(*refs):
        d_refs, m_ref = refs[:HEADS_PER_GROUP], refs[HEADS_PER_GROUP]
        do_refs, dl_refs, scr = refs[-7:-4], refs[-4:-1], refs[-1]
        for h in range(HEADS_PER_GROUP):
            sl = slice(h * HEAD_DIM, (h + 1) * HEAD_DIM)
            prod = d_refs[h][...] * m_ref[:, sl].astype(F32)
            scr[h] = jnp.broadcast_to(jnp.sum(prod, axis=-1, keepdims=True), (tm, HEAD_DIM))
            for dil, do_ref, dl_ref in zip(DILATIONS, do_refs, dl_refs):
                for r in range(dil):
                    rs = _residue(r, tm // dil, dil)
                    do_ref[r, :, sl] = d_refs[h][rs, :].astype(BF)
                    dl_ref[r, :, sl] = scr[h, rs, :]

    nat = pl.BlockSpec((tm, GROUP_W), lambda i: (i, 0))
    head_specs = [pl.BlockSpec((tm, HEAD_DIM), lambda i, h=h: (i, h)) for h in range(HEADS_PER_GROUP)]
    bf_shapes, specs = _major_specs(s, tm, BF)
    f_shapes, _ = _major_specs(s, tm, F32)
    res = pl.pallas_call(
        body, out_shape=bf_shapes + f_shapes, grid=(s // tm,), in_specs=head_specs + [nat] + [ANY] * len(after),
        out_specs=specs * 2, scratch_shapes=[pltpu.VMEM((HEADS_PER_GROUP, tm, HEAD_DIM), F32)], name="attn_delta",
        compiler_params=_params(("parallel",)))(*([dcat] * HEADS_PER_GROUP), merged, *after)
    return res[:3], res[3:]


def _mem_probs(qh, kh):
    sc = lax.dot_general(qh, kh, NT_DIMS, preferred_element_type=F32) * SCALE
    p = jnp.exp(sc - jnp.max(sc, axis=-1, keepdims=True))
    return p, jnp.sum(p, axis=-1, keepdims=True)


def _mem_fwd(name, proj, q_block, kv, into, out_block):
    s = proj.shape[0]
    tq = _pick_rows(s, 512)

    def body(q_ref, kv_ref, into_ref, o_ref):
        for h in range(MEM_HEADS):
            sl = slice(h * HEAD_DIM, (h + 1) * HEAD_DIM)
            vsl = slice(MEM_W + h * HEAD_DIM, MEM_W + (h + 1) * HEAD_DIM)
            p, den = _mem_probs(q_ref[:, sl].astype(BF), kv_ref[:, sl].astype(BF))
            out = jnp.dot(p.astype(BF), kv_ref[:, vsl].astype(BF), preferred_element_type=F32) / den
            o_ref[:, sl] = out.astype(o_ref.dtype)

    return pl.pallas_call(
        body, out_shape=jax.ShapeDtypeStruct(into.shape, into.dtype), grid=(s // tq,),
        in_specs=[pl.BlockSpec((tq, MEM_W), lambda i: (i, q_block)), pl.BlockSpec(kv.shape, lambda i: (0, 0)), ANY],
        out_specs=pl.BlockSpec((tq, MEM_W), lambda i: (i, out_block)), input_output_aliases={2: 0}, name=name,
        compiler_params=_params(("parallel",)))(proj, kv, into)


def _mem_bwd(name, proj, q_block, kv, dcat, d_block, width):
    s = proj.shape[0]
    tq = _pick_rows(s, 512)

    def body(q_ref, kv_ref, do_ref, dq_ref, dkv_ref):
        @pl.when(pl.program_id(0) == 0)
        def _():
            dkv_ref[...] = jnp.zeros_like(dkv_ref)

        for h in range(MEM_HEADS):
            sl = slice(h * HEAD_DIM, (h + 1) * HEAD_DIM)
            vsl = slice(MEM_W + h * HEAD_DIM, MEM_W + (h + 1) * HEAD_DIM)
            qh, kh, vh = q_ref[:, sl].astype(BF), kv_ref[:, sl].astype(BF), kv_ref[:, vsl].astype(BF)
            doh = do_ref[:, sl].astype(BF)
            p, den = _mem_probs(qh, kh)
            p = p / den
            dp = lax.dot_general(doh, vh, NT_DIMS, preferred_element_type=F32)
            ds = (p * (dp - jnp.sum(p * dp, axis=-1, keepdims=True)) * SCALE).astype(BF)
            dq_ref[:, sl] = jnp.dot(ds, kh, preferred_element_type=F32).astype(BF)
            dkv_ref[:, sl] += lax.dot_general(ds, qh, TN_DIMS, preferred_element_type=F32)
            dkv_ref[:, vsl] += lax.dot_general(p.astype(BF), doh, TN_DIMS, preferred_element_type=F32)

    whole = pl.BlockSpec(kv.shape, lambda i: (0, 0))
    return pl.pallas_call(
        body, out_shape=[jax.ShapeDtypeStruct((s, width), BF), jax.ShapeDtypeStruct(kv.shape, F32)], grid=(s // tq,),
        in_specs=[pl.BlockSpec((tq, MEM_W), lambda i: (i, q_block)), whole,
                  pl.BlockSpec((tq, MEM_W), lambda i: (i, d_block))],
        out_specs=[pl.BlockSpec((tq, MEM_W), lambda i: (i, width // MEM_W - 1)), whole], name=name,
        compiler_params=_params(("arbitrary",)))(proj, kv, dcat)


def _causal():
    t = lax.broadcasted_iota(jnp.int32, (BLK, BLK), 0)
    s = lax.broadcasted_iota(jnp.int32, (BLK, BLK), 1)
    return t >= s


def _sgu_norm(vg, ln_g, ln_b):
    mu = jnp.mean(vg, axis=-1, keepdims=True)
    cen = vg - mu
    rstd = lax.rsqrt(jnp.mean(cen * cen, axis=-1, keepdims=True) + LN_EPS)
    xhat = cen * rstd
    return xhat, rstd, xhat * ln_g + ln_b


def _sgu_fwd(proj, ln_g, ln_b, w_sp, b_t):
    s = proj.shape[0]

    def body(u_ref, v_ref, g_ref, b_ref, w_ref, bt_ref, o_ref):
        _, _, vn = _sgu_norm(_gelu(v_ref[...]), g_ref[...], b_ref[...])
        vn = vn.astype(BF)
        tri = _causal()
        for grp in range(SGU_GROUPS):
            sl = slice(grp * HEAD_DIM, (grp + 1) * HEAD_DIM)
            w = jnp.where(tri, w_ref[grp], 0.0).astype(BF)
            mixed = jnp.dot(w, vn[:, sl], preferred_element_type=F32) + bt_ref[:, grp:grp + 1]
            o_ref[:, sl] = (_gelu(u_ref[:, sl]) * mixed).astype(BF)

    vec = pl.BlockSpec((1, SGU_W), lambda i: (0, 0))
    return pl.pallas_call(
        body, out_shape=jax.ShapeDtypeStruct((s, SGU_W + MEM_W), BF), grid=(s // BLK,),
        in_specs=[pl.BlockSpec((BLK, SGU_W), lambda i: (i, 0)), pl.BlockSpec((BLK, SGU_W), lambda i: (i, 1)), vec, vec,
                  pl.BlockSpec(w_sp.shape, lambda i: (0, 0, 0)), pl.BlockSpec(b_t.shape, lambda i: (0, 0))],
        out_specs=pl.BlockSpec((BLK, SGU_W), lambda i: (i, 0)), name="sgu_fwd",
        compiler_params=_params(("parallel",)))(proj, proj, ln_g, ln_b, w_sp, b_t)


def _sgu_bwd(proj, dcat, ln_g, ln_b, w_sp, b_t, into):
    s = proj.shape[0]

    def body(u_ref, v_ref, d_ref, g_ref, b_ref, w_ref, bt_ref, into_ref, dp_ref, dw_ref, db_ref, dg_ref, dbeta_ref,
             dvn_ref):
        @pl.when(pl.program_id(0) == 0)
        def _():
            dw_ref[...] = jnp.zeros_like(dw_ref)
            db_ref[...] = jnp.zeros_like(db_ref)
            dg_ref[...] = jnp.zeros_like(dg_ref)
            dbeta_ref[...] = jnp.zeros_like(dbeta_ref)

        gain = g_ref[...]
        vg, v_slope = _gelu_both(v_ref[...])
        xhat, rstd, vn = _sgu_norm(vg, gain, b_ref[...])
        vn = vn.astype(BF)
        tri = _causal()
        lane = lax.broadcasted_iota(jnp.int32, (BLK, HEAD_DIM), 1)
        db_acc = jnp.zeros((BLK, HEAD_DIM), F32)
        for grp in range(SGU_GROUPS):
            sl = slice(grp * HEAD_DIM, (grp + 1) * HEAD_DIM)
            w = jnp.where(tri, w_ref[grp], 0.0).astype(BF)
            vn_g = vn[:, sl]
            mixed = jnp.dot(w, vn_g, preferred_element_type=F32) + bt_ref[:, grp:grp + 1]
            u_act, u_slope = _gelu_both(u_ref[:, sl])
            d_out = d_ref[:, sl]
            dp_ref[:, sl] = (d_out * mixed * u_slope).astype(BF)
            dmixed = d_out * u_act
            dm = dmixed.astype(BF)
            dvn_ref[:, sl] = lax.dot_general(w, dm, TN_DIMS, preferred_element_type=F32)
            dw = lax.dot_general(dm, vn_g, NT_DIMS, preferred_element_type=F32)
            dw_ref[grp] += jnp.where(tri, dw, 0.0)
            db_acc += jnp.where(lane == grp, jnp.sum(dmixed, axis=-1, keepdims=True), 0.0)
        db_ref[...] += db_acc
        dvn = dvn_ref[...]
        dg_ref[...] += jnp.sum(dvn * xhat, axis=0, keepdims=True)
        dbeta_ref[...] += jnp.sum(dvn, axis=0, keepdims=True)
        dxh = dvn * gain
        dvg = rstd * (dxh - jnp.mean(dxh, axis=-1, keepdims=True) - xhat * jnp.mean(dxh * xhat, axis=-1, keepdims=True))
        dp_ref[:, SGU_W:] = (dvg * v_slope).astype(BF)

    vec = pl.BlockSpec((1, SGU_W), lambda i: (0, 0))
    row = pl.BlockSpec((BLK, SGU_W), lambda i: (i, 0))
    w_spec = pl.BlockSpec(w_sp.shape, lambda i: (0, 0, 0))
    sq = pl.BlockSpec((BLK, HEAD_DIM), lambda i: (0, 0))
    return pl.pallas_call(
        body,
        out_shape=[jax.ShapeDtypeStruct(into.shape, into.dtype),
                   jax.ShapeDtypeStruct(w_sp.shape, F32), jax.ShapeDtypeStruct((BLK, HEAD_DIM), F32),
                   jax.ShapeDtypeStruct((1, SGU_W), F32), jax.ShapeDtypeStruct((1, SGU_W), F32)],
        grid=(s // BLK,),
        in_specs=[row, pl.BlockSpec((BLK, SGU_W), lambda i: (i, 1)), row, vec, vec, w_spec,
                  pl.BlockSpec(b_t.shape, lambda i: (0, 0)), ANY],
        out_specs=[pl.BlockSpec((BLK, 2 * SGU_W), lambda i: (i, 0)), w_spec, sq, vec, vec],
        scratch_shapes=[pltpu.VMEM((BLK, SGU_W), F32)], input_output_aliases={7: 0}, name="sgu_bwd",
        compiler_params=_params(("arbitrary",)))(proj, proj, dcat, ln_g, ln_b, w_sp, b_t, into)


def _place():
    return lax.axis_index("x"), lax.axis_index("y"), lax.axis_index("c")


def _other_chips(x, y):
    return [(1 - x, y), (x, 1 - y), (1 - x, 1 - y)]


def _peer(x, y, c, mask):
    return (1 - x if mask & 4 else x, 1 - y if mask & 2 else y, 1 - c if mask & 1 else c)


def _in_hbm(a):
    return pltpu.with_memory_space_constraint(a, pltpu.HBM)


def _token_spec():
    return jax.ShapeDtypeStruct((8, LANES), F32), pl.BlockSpec(memory_space=pltpu.VMEM)


def _remote(src, dst, ssem, rsem, to):
    return pltpu.make_async_remote_copy(src_ref=src, dst_ref=dst, send_sem=ssem, recv_sem=rsem, device_id=to,
                                        device_id_type=MESH)


def _place_shard(name, src, layer, place, dtype, after=()):
    _, rows, cols = src.shape
    tr = _pick_rows(rows, 512)

    def body(p_ref, s_ref, *rest):
        rest[-1][...] = s_ref[...].astype(dtype)

    grid_spec = pltpu.PrefetchScalarGridSpec(
        num_scalar_prefetch=1, grid=(rows // tr,),
        in_specs=[pl.BlockSpec((None, tr, cols), lambda i, p: (layer, i, 0))] + [ANY] * len(after),
        out_specs=pl.BlockSpec((None, tr, cols), lambda i, p: (p[1], i, 0)))
    return pl.pallas_call(body, out_shape=jax.ShapeDtypeStruct((N_CHIPS, rows, cols), dtype), grid_spec=grid_spec,
                          name=name, compiler_params=_params(("parallel",)))(place, src, *after)


def _gather_copies(bufs, ssem, rsem):
    x, y, c = _place()
    me = 2 * x + y
    copies = []
    for ai, buf in enumerate(bufs):
        for k, (ox, oy) in enumerate(_other_chips(x, y)):
            copies.append(_remote(buf.at[me], buf.at[me], ssem.at[3 * ai + k], rsem.at[3 * ai + k], (ox, oy, c)))
    return copies


def _reduce_copies(grads, lands, ssem, rsem):
    x, y, c = _place()
    copies = []
    for a, (gr, land) in enumerate(zip(grads, lands)):
        for mask in range(1, N_DEV):
            px, py, pc = _peer(x, y, c, mask)
            copies.append(_remote(gr.at[pc, 2 * px + py], land.at[mask - 1], ssem.at[7 * a + mask - 1],
                                  rsem.at[7 * a + mask - 1], (px, py, pc)))
    return copies


def _half_copies(totals, ssem, rsem):
    x, y, c = _place()
    return [_remote(t.at[c], t.at[c], ssem.at[a], rsem.at[a], (x, y, 1 - c)) for a, t in enumerate(totals)]


def _gather_start(name, groups):
    flat = [s for grp in groups for s in grp]
    n, ng = len(flat), len(groups)

    def body(*refs):
        ins = refs[:n]
        sems = refs[n:n + 2 * ng]
        token = refs[-1]
        idx = 0
        for gi, grp in enumerate(groups):
            for cp in _gather_copies(ins[idx:idx + len(grp)], sems[2 * gi], sems[2 * gi + 1]):
                cp.start()
            idx += len(grp)
        token[...] = jnp.zeros_like(token)

    tok_shape, tok_spec = _token_spec()
    sem_shapes = []
    for grp in groups:
        sem_shapes += [pltpu.SemaphoreType.DMA((3 * len(grp),))] * 2
    res = pl.pallas_call(
        body, name=name,
        out_shape=(*sem_shapes, *[pltpu.HBM(s.shape, s.dtype) for s in flat], tok_shape),
        in_specs=[HBM] * n, out_specs=(*[SEM] * (2 * ng), *[HBM] * n, tok_spec),
        input_output_aliases={i: 2 * ng + i for i in range(n)},
        compiler_params=pltpu.CompilerParams(has_side_effects=EFFECT))(*[_in_hbm(s) for s in flat])
    out, idx = [], 2 * ng
    for gi, grp in enumerate(groups):
        out.append((res[2 * gi], res[2 * gi + 1], list(res[idx:idx + len(grp)])))
        idx += len(grp)
    return out, res[-1]


def _gather_wait(name, ssem, rsem, slabs, after):
    n = len(slabs)

    def body(*refs):
        for cp in _gather_copies(refs[:n], refs[n], refs[n + 1]):
            cp.wait_send()
            cp.wait_recv()

    return pl.pallas_call(
        body, name=name, out_shape=tuple(pltpu.HBM(s.shape, s.dtype) for s in slabs),
        in_specs=[HBM] * n + [SEM, SEM] + [ANY] * len(after), out_specs=tuple([HBM] * n),
        input_output_aliases={i: i for i in range(n)},
        compiler_params=pltpu.CompilerParams(has_side_effects=EFFECT))(*slabs, ssem, rsem, *after)


def _reduce_start(name, grads):
    n = len(grads)
    lands = [lax.empty((N_DEV - 1, *g.shape[2:]), g.dtype) for g in grads]

    def body(*refs):
        token = refs[-1]
        for cp in _reduce_copies(refs[:n], refs[n:2 * n], refs[2 * n], refs[2 * n + 1]):
            cp.start()
        token[...] = jnp.zeros_like(token)

    tok_shape, tok_spec = _token_spec()
    sems = [pltpu.SemaphoreType.DMA((7 * n,))] * 2
    res = pl.pallas_call(
        body, name=name,
        out_shape=(*sems, *[pltpu.HBM(g.shape, g.dtype) for g in grads], *[pltpu.HBM(l.shape, l.dtype) for l in lands],
                   tok_shape),
        in_specs=[HBM] * (2 * n), out_specs=(SEM, SEM, *[HBM] * (2 * n), tok_spec),
        input_output_aliases={i: 2 + i for i in range(2 * n)},
        compiler_params=pltpu.CompilerParams(has_side_effects=EFFECT))(*[_in_hbm(t) for t in (*grads, *lands)])
    return res[0], res[1], list(res[2:2 + n]), list(res[2 + n:2 + 2 * n]), res[-1]


def _reduce_wait(name, ssem, rsem, grads, lands, after):
    n = len(grads)

    def body(*refs):
        for cp in _reduce_copies(refs[:n], refs[n:2 * n], refs[2 * n], refs[2 * n + 1]):
            cp.wait_send()
            cp.wait_recv()

    res = pl.pallas_call(
        body, name=name, out_shape=tuple(pltpu.HBM(t.shape, t.dtype) for t in (*grads, *lands)),
        in_specs=[HBM] * (2 * n) + [SEM, SEM] + [ANY] * len(after), out_specs=tuple([HBM] * (2 * n)),
        input_output_aliases={i: i for i in range(2 * n)},
        compiler_params=pltpu.CompilerParams(has_side_effects=EFFECT))(*grads, *lands, ssem, rsem, *after)
    return list(res[:n]), list(res[n:])


def _sum_pieces(name, grad, land, place):
    _, _, rows, cols = grad.shape
    tr = _pick_rows(rows, 256)

    def body(p_ref, g_ref, l_ref, o_ref):
        tot = g_ref[...].astype(F32)
        for k in range(N_DEV - 1):
            tot = tot + l_ref[k].astype(F32)
        o_ref[...] = tot

    grid_spec = pltpu.PrefetchScalarGridSpec(
        num_scalar_prefetch=1, grid=(rows // tr,),
        in_specs=[pl.BlockSpec((None, None, tr, cols), lambda i, p: (p[0], p[1], i, 0)),
                  pl.BlockSpec((N_DEV - 1, tr, cols), lambda i, p: (0, i, 0))],
        out_specs=pl.BlockSpec((None, tr, cols), lambda i, p: (p[0], i, 0)))
    return pl.pallas_call(body, out_shape=jax.ShapeDtypeStruct((2, rows, cols), F32), grid_spec=grid_spec, name=name,
                          compiler_params=_params(("parallel",)))(place, grad, land)


def _half_start(name, totals):
    n = len(totals)

    def body(*refs):
        token = refs[-1]
        for cp in _half_copies(refs[:n], refs[n], refs[n + 1]):
            cp.start()
        token[...] = jnp.zeros_like(token)

    tok_shape, tok_spec = _token_spec()
    res = pl.pallas_call(
        body, name=name,
        out_shape=(pltpu.SemaphoreType.DMA((n,)), pltpu.SemaphoreType.DMA((n,)),
                   *[pltpu.HBM(t.shape, t.dtype) for t in totals], tok_shape),
        in_specs=[HBM] * n, out_specs=(SEM, SEM, *[HBM] * n, tok_spec),
        input_output_aliases={i: 2 + i for i in range(n)},
        compiler_params=pltpu.CompilerParams(has_side_effects=EFFECT))(*[_in_hbm(t) for t in totals])
    return res[0], res[1], list(res[2:2 + n]), res[-1]


def _half_wait(name, ssem, rsem, totals, after):
    n = len(totals)

    def body(*refs):
        for cp in _half_copies(refs[:n], refs[n], refs[n + 1]):
            cp.wait_send()
            cp.wait_recv()

    res = pl.pallas_call(
        body, name=name, out_shape=tuple(pltpu.HBM(t.shape, t.dtype) for t in totals),
        in_specs=[HBM] * n + [SEM, SEM] + [ANY] * len(after), out_specs=tuple([HBM] * n),
        input_output_aliases={i: i for i in range(n)},
        compiler_params=pltpu.CompilerParams(has_side_effects=EFFECT))(*totals, ssem, rsem, *after)
    return list(res)


def _small_copies(bufs, ssem, rsem):
    x, y, c = _place()
    mine = bufs[0].at[4 * x + 2 * y + c]
    return [_remote(mine, mine, ssem.at[mask - 1], rsem.at[mask - 1], _peer(x, y, c, mask)) for mask in range(1, N_DEV)]


def _small_start(name, slots):
    def body(s_ref, ssem, rsem, thru, token):
        for cp in _small_copies([s_ref], ssem, rsem):
            cp.start()
        token[...] = jnp.zeros_like(token)

    tok_shape, tok_spec = _token_spec()
    sems = [pltpu.SemaphoreType.DMA((N_DEV - 1,))] * 2
    return pl.pallas_call(
        body, name=name, out_shape=(*sems, pltpu.HBM(slots.shape, slots.dtype), tok_shape), in_specs=[HBM],
        out_specs=(SEM, SEM, HBM, tok_spec), input_output_aliases={0: 2},
        compiler_params=pltpu.CompilerParams(has_side_effects=EFFECT))(_in_hbm(slots))


def _small_wait(name, ssem, rsem, slots, after):
    def body(*refs):
        for cp in _small_copies([refs[0]], refs[1], refs[2]):
            cp.wait_send()
            cp.wait_recv()

    return pl.pallas_call(
        body, name=name, out_shape=pltpu.HBM(slots.shape, slots.dtype), in_specs=[HBM, SEM, SEM] + [ANY] * len(after),
        out_specs=HBM, input_output_aliases={0: 0},
        compiler_params=pltpu.CompilerParams(has_side_effects=EFFECT))(slots, ssem, rsem, *after)


def _own_slot(small, me):
    return lax.dynamic_update_slice(jnp.zeros((N_DEV, *small.shape), small.dtype), small[None], (me, 0, 0))


def _sum_devices(name, stacked):
    _, rows, lanes = stacked.shape
    tr = _pick_rows(rows, 512)

    def body(s_ref, o_ref):
        tot = s_ref[0]
        for k in range(1, N_DEV):
            tot = tot + s_ref[k]
        o_ref[...] = tot

    return pl.pallas_call(
        body, out_shape=jax.ShapeDtypeStruct((rows, lanes), F32), grid=(rows // tr,),
        in_specs=[pl.BlockSpec((N_DEV, tr, lanes), lambda i: (0, i, 0))], out_specs=pl.BlockSpec((tr, lanes), lambda i: (i, 0)),
        name=name, compiler_params=_params(("parallel",)))(stacked)


def _adamw(name, w, g, m, v, layer, prev=None):
    layers, rows, cols = w.shape
    tr = _pick_rows(rows, 256)
    c1 = 1.0 - ADAM_B1 ** ADAM_STEP
    c2 = 1.0 - ADAM_B2 ** ADAM_STEP

    def body(w_ref, g_ref, m_ref, v_ref, *rest):
        go_ref, d_ref, nm_ref, nv_ref = rest[-4:]
        gv = g_ref[...]
        nm = ADAM_B1 * m_ref[...] + (1.0 - ADAM_B1) * gv
        nv = ADAM_B2 * v_ref[...] + (1.0 - ADAM_B2) * (gv * gv)
        go_ref[...] = gv
        d_ref[...] = -ADAM_LR * ((nm / c1) / (jnp.sqrt(nv / c2) + ADAM_EPS) + ADAM_WD * w_ref[...])
        nm_ref[...] = nm
        nv_ref[...] = nv

    spec = pl.BlockSpec((None, tr, cols), lambda i: (layer, i, 0))
    prev = list(prev) if prev is not None else []
    return pl.pallas_call(
        body, out_shape=[jax.ShapeDtypeStruct((layers, rows, cols), F32)] * 4, grid=(rows // tr,),
        in_specs=[spec, pl.BlockSpec((tr, cols), lambda i: (i, 0)), spec, spec] + [ANY] * len(prev),
        out_specs=[spec] * 4, input_output_aliases={4 + i: i for i in range(len(prev))}, name=name,
        compiler_params=_params(("parallel",)))(w, g, m, v, *prev)


def _pack(vectors, pad_rows):
    flat = jnp.concatenate([t.reshape(-1) for t in vectors])
    rows = -(-flat.shape[0] // LANES)
    rows = -(-rows // pad_rows) * pad_rows
    return jnp.pad(flat, (0, rows * LANES - flat.shape[0])).reshape(rows, LANES)


def _unpack(packed, shapes):
    flat = packed.reshape(-1)
    out, off = [], 0
    for shp in shapes:
        size = math.prod(shp)
        out.append(flat[off:off + size].reshape(shp))
        off += size
    return out


def kernel(x, mem, positions, mix_norm, mem_norm, w_mem_kv, ffn_norm, w_gate, w_up, w_down, attn_w_in, attn_w_out, sgu_w_in, sgu_ln_g, sgu_ln_b, sgu_w_spatial, sgu_b_spatial, sgu_w_out, final_norm, loss_target, m_mix_norm, m_mem_norm, m_w_mem_kv, m_ffn_norm, m_w_gate, m_w_up, m_w_down, m_attn_w_in, m_attn_w_out, m_sgu_w_in, m_sgu_ln_g, m_sgu_ln_b, m_sgu_w_spatial, m_sgu_b_spatial, m_sgu_w_out, m_final_norm, v_mix_norm, v_mem_norm, v_w_mem_kv, v_ffn_norm, v_w_gate, v_w_up, v_w_down, v_attn_w_in, v_attn_w_out, v_sgu_w_in, v_sgu_ln_g, v_sgu_ln_b, v_sgu_w_spatial, v_sgu_b_spatial, v_sgu_w_out, v_final_norm):
    d_model = x.shape[2]
    x0, mem0, tgt = x[0], mem[0], loss_target[0]
    xi, yi, ci = _place()
    chip = 2 * xi + yi
    place = jnp.stack([ci, chip]).astype(jnp.int32)

    given_w = dict(mix_norm=mix_norm, mem_norm=mem_norm, w_mem_kv=w_mem_kv, ffn_norm=ffn_norm, w_gate=w_gate, w_up=w_up,
                   w_down=w_down, attn_w_in=attn_w_in, attn_w_out=attn_w_out, sgu_w_in=sgu_w_in, sgu_ln_g=sgu_ln_g,
                   sgu_ln_b=sgu_ln_b, sgu_w_spatial=sgu_w_spatial, sgu_b_spatial=sgu_b_spatial, sgu_w_out=sgu_w_out,
                   final_norm=final_norm)
    given_m = dict(mix_norm=m_mix_norm, mem_norm=m_mem_norm, w_mem_kv=m_w_mem_kv, ffn_norm=m_ffn_norm, w_gate=m_w_gate,
                   w_up=m_w_up, w_down=m_w_down, attn_w_in=m_attn_w_in, attn_w_out=m_attn_w_out, sgu_w_in=m_sgu_w_in,
                   sgu_ln_g=m_sgu_ln_g, sgu_ln_b=m_sgu_ln_b, sgu_w_spatial=m_sgu_w_spatial,
                   sgu_b_spatial=m_sgu_b_spatial, sgu_w_out=m_sgu_w_out, final_norm=m_final_norm)
    given_v = dict(mix_norm=v_mix_norm, mem_norm=v_mem_norm, w_mem_kv=v_w_mem_kv, ffn_norm=v_ffn_norm, w_gate=v_w_gate,
                   w_up=v_w_up, w_down=v_w_down, attn_w_in=v_attn_w_in, attn_w_out=v_attn_w_out, sgu_w_in=v_sgu_w_in,
                   sgu_ln_g=v_sgu_ln_g, sgu_ln_b=v_sgu_ln_b, sgu_w_spatial=v_sgu_w_spatial,
                   sgu_b_spatial=v_sgu_b_spatial, sgu_w_out=v_sgu_w_out, final_norm=v_final_norm)

    units = {"attn_w_in": ("attn_w_in", 0, "col"), "w_mem_kv0": ("w_mem_kv", 0, "row"), "attn_w_out": ("attn_w_out", 0, "col"),
             "w_gate0": ("w_gate", 0, "col"), "w_up0": ("w_up", 0, "col"), "w_down0": ("w_down", 0, "row"),
             "sgu_w_in": ("sgu_w_in", 0, "col"), "w_mem_kv1": ("w_mem_kv", 1, "row"), "sgu_w_out": ("sgu_w_out", 0, "row"),
             "w_gate1": ("w_gate", 1, "col"), "w_up1": ("w_up", 1, "col"), "w_down1": ("w_down", 1, "row")}
    gather_groups = [["attn_w_in"], ["w_mem_kv0", "attn_w_out"], ["w_gate0", "w_up0"],
                     ["w_down0", "sgu_w_in", "w_mem_kv1", "ln"], ["sgu_w_out", "w_gate1", "w_up1"], ["w_down1"]]

    first = _place_shard("place_attn_w_in", attn_w_in, 0, place, BF)
    in_flight, token = _gather_start("gather_start_0", [[first]])
    slabs = {u: _place_shard(f"place_{u}", given_w[arr], layer, place, BF, after=[token])
             for u, (arr, layer, _) in units.items() if u != "attn_w_in"}
    slabs["ln"] = _place_shard("place_ln", jnp.concatenate([sgu_ln_g, sgu_ln_b])[None], 0, place, F32, after=[token])
    rest, token = _gather_start("gather_start_1", [[slabs[u] for u in grp] for grp in gather_groups[1:]])
    in_flight += rest
    weights = {}

    def arrive(gi, after):
        ssem, rsem, arrs = in_flight[gi]
        for u, full in zip(gather_groups[gi], _gather_wait(f"gather_wait_{gi}", ssem, rsem, arrs, after)):
            weights[u] = full if u == "ln" else Weight(full, units[u][2])

    w_sp = sgu_w_spatial[0]
    b_t = sgu_b_spatial[0].T
    tables = _rope_tables(positions[0])

    def residual(acc, extra):
        return [extra[0] + acc[0]]

    def memory_kv(layer):
        mem_n = _rms_fwd(f"mem_norm_{layer}", mem0, mem_norm[layer:layer + 1])
        return mem_n, _mm_nn(f"mem_kv_{layer}", mem_n, weights[f"w_mem_kv{layer}"])[0]

    h0 = _rms_fwd("mix_norm_0", x0, mix_norm[0:1], after=[token])
    arrive(0, [h0])
    proj0 = _mm_nn("attn_in", h0, weights["attn_w_in"])[0]
    arrive(1, [proj0])
    qkv = _rope_fwd(proj0, tables)
    qs, ks, vs = qkv[0:3], qkv[3:6], qkv[6:9]
    outs, lses = [], []
    for g in range(len(DILATIONS)):
        o, l = _dil_fwd(g, qs[g], ks[g], vs[g])
        outs.append(o)
        lses.append(l)
    merged, lse = _attn_merge(outs, lses)
    mem_n0, kv0 = memory_kv(0)
    cat0 = _mem_fwd("mem_fwd_0", proj0, 9, kv0, merged, 1)
    x1 = _mm_nn("attn_out", cat0, weights["attn_w_out"], extras=[x0], epilogue=residual)[0]
    arrive(2, [x1])
    hf0 = _rms_fwd("ffn_norm_0", x1, ffn_norm[0:1])
    g0, u0, act0 = _gate_up("gate_up_0", hf0, weights["w_gate0"], weights["w_up0"])
    arrive(3, [act0])
    x2 = _mm_nn("down_0", act0, weights["w_down0"], extras=[x1], epilogue=residual)[0]

    ln_all = weights["ln"]
    ln_g = ln_all[:, 0, :].reshape(1, SGU_W)
    ln_b = ln_all[:, 1, :].reshape(1, SGU_W)
    h1 = _rms_fwd("mix_norm_1", x2, mix_norm[1:2])
    proj1 = _mm_nn("sgu_in", h1, weights["sgu_w_in"])[0]
    arrive(4, [proj1])
    sgu_out = _sgu_fwd(proj1, ln_g, ln_b, w_sp, b_t)
    mem_n1, kv1 = memory_kv(1)
    cat1 = _mem_fwd("mem_fwd_1", proj1, 6, kv1, sgu_out, 3)
    x3 = _mm_nn("sgu_out", cat1, weights["sgu_w_out"], extras=[x2], epilogue=residual)[0]
    hf1 = _rms_fwd("ffn_norm_1", x3, ffn_norm[1:2])
    g1, u1, act1 = _gate_up("gate_up_1", hf1, weights["w_gate1"], weights["w_up1"])
    arrive(5, [act1])
    x4 = _mm_nn("down_1", act1, weights["w_down1"], extras=[x3], epilogue=residual)[0]

    d4, g_final, loss_part = _final_loss(x4, tgt, final_norm.reshape(1, d_model))
    loss = lax.psum(loss_part[0, 0], ("x", "y", "c"))

    outputs = {}

    def start_reduce(tag, names, grads):
        ssem, rsem, grads, lands, tok = _reduce_start(f"reduce_start_{tag}", grads)
        return dict(tag=tag, names=names, ssem=ssem, rsem=rsem, grads=grads, lands=lands), tok

    def finish_reduce(st, after):
        grads, lands = _reduce_wait(f"reduce_wait_{st['tag']}", st["ssem"], st["rsem"], st["grads"], st["lands"], after)
        totals = [_sum_pieces(f"sum_{u}", g, l, place) for u, g, l in zip(st["names"], grads, lands)]
        ssem, rsem, totals, tok = _half_start(f"half_start_{st['tag']}", totals)
        return dict(tag=st["tag"], names=st["names"], ssem=ssem, rsem=rsem, totals=totals), tok

    def finish_update(st, after):
        totals = _half_wait(f"half_wait_{st['tag']}", st["ssem"], st["rsem"], st["totals"], after)
        for u, tot in zip(st["names"], totals):
            arr, layer, _ = units[u]
            w = given_w[arr]
            outputs[arr] = _adamw(f"adamw_{u}", w, tot.reshape(w.shape[1:]), given_m[arr], given_v[arr], layer,
                                  outputs.get(arr))

    def ffn_bwd(layer, d_out, xin, h, g, u, act):
        wd, wg, wu = weights[f"w_down{layer}"], weights[f"w_gate{layer}"], weights[f"w_up{layer}"]
        gr_down = _mm_tn(f"d_down_{layer}", act, d_out, wd)
        dg, du = _mm_nt(f"d_act_{layer}", [d_out], [wd], out_dtypes=(BF, BF), extras=[g, u], epilogue=_swiglu_bwd_epilogue,
                        col_chunk=EPILOGUE_CHUNK)
        gr_gate = _mm_tn(f"d_gate_{layer}", h, dg, wg)
        gr_up = _mm_tn(f"d_up_{layer}", h, du, wu)
        st, tok = start_reduce(f"ffn{layer}", [f"w_down{layer}", f"w_gate{layer}", f"w_up{layer}"], [gr_down, gr_gate, gr_up])
        dh = _mm_nt(f"d_ffn_h_{layer}", [dg, du], [wg, wu], after=[tok])[0]
        d_in, g_norm = _rms_bwd(f"ffn_norm_bwd_{layer}", xin, ffn_norm[layer:layer + 1], dh, d_out)
        return st, d_in, g_norm, dg

    def memory_bwd(layer, mem_n, dkv):
        dkv = dkv.astype(BF)
        wkv = weights[f"w_mem_kv{layer}"]
        gr = _mm_tn(f"d_mem_kv_{layer}", mem_n, dkv, wkv)
        d_mem_n = _mm_nt(f"d_mem_n_{layer}", [dkv], [wkv])[0]
        return gr, _rms_bwd(f"mem_norm_bwd_{layer}", mem0, mem_norm[layer:layer + 1], d_mem_n)[1]

    st_ffn1, d3, g_ffn1, _ = ffn_bwd(1, d4, x3, hf1, g1, u1, act1)
    gr_sgu_out = _mm_tn("d_sgu_out", cat1, d3, weights["sgu_w_out"])
    dcat1 = _mm_nt("d_cat_1", [d3], [weights["sgu_w_out"]])[0]
    st_ffn1, tok = finish_reduce(st_ffn1, [dcat1])
    dproj1, dkv1 = _mem_bwd("mem_bwd_1", proj1, 6, kv1, dcat1, 3, proj1.shape[1])
    gr_kv1, g_mem1 = memory_bwd(1, mem_n1, dkv1)
    dproj1, g_wsp, g_bsp_t, g_ln_g, g_ln_b = _sgu_bwd(proj1, dcat1, ln_g, ln_b, w_sp, b_t, dproj1)
    gr_sgu_in = _mm_tn("d_sgu_in", h1, dproj1, weights["sgu_w_in"], after=[tok])
    finish_update(st_ffn1, [gr_sgu_in])
    st_mix1, tok = start_reduce("mix1", ["sgu_w_out", "w_mem_kv1", "sgu_w_in"], [gr_sgu_out, gr_kv1, gr_sgu_in])
    dh1 = _mm_nt("d_h_1", [dproj1], [weights["sgu_w_in"]], after=[tok])[0]
    d2, g_mix1 = _rms_bwd("mix_norm_bwd_1", x2, mix_norm[1:2], dh1, d3)

    st_ffn0, d1, g_ffn0, dg0 = ffn_bwd(0, d2, x1, hf0, g0, u0, act0)
    dev = 4 * xi + 2 * yi + ci
    small_a = [g_mix1, g_mem1, jnp.concatenate([g_ffn0, g_ffn1]), g_wsp, g_bsp_t[:, :SGU_GROUPS].T, g_final, g_ln_g, g_ln_b]
    sa_ssem, sa_rsem, sa_slots, tok = _small_start("small_start_a", _own_slot(_pack(small_a, LANES), dev))
    gr_attn_out = _mm_tn("d_attn_out", cat0, d1, weights["attn_w_out"], after=[tok])
    st_mix1, tok = finish_reduce(st_mix1, [gr_attn_out])
    dcat0 = _mm_nt("d_cat_0", [d1], [weights["attn_w_out"]], after=[tok])[0]
    dproj0, dkv0 = _mem_bwd("mem_bwd_0", proj0, 9, kv0, dcat0, 1, proj0.shape[1])
    finish_update(st_mix1, [dkv0])
    gr_kv0, g_mem0 = memory_bwd(0, mem_n0, dkv0)
    st_ffn0, tok = finish_reduce(st_ffn0, [g_mem0])
    d_merged, delta = _attn_delta(dcat0, cat0, after=[tok])
    dqs, dks, dvs = [], [], []
    for g in range(len(DILATIONS)):
        dq, dk, dv = _dil_bwd(g, qs[g], ks[g], vs[g], d_merged[g], lse[g], delta[g])
        dqs.append(dq)
        dks.append(dk)
        dvs.append(dv)
    dproj0 = _rope_bwd(dqs + dks + dvs, tables, dproj0)
    finish_update(st_ffn0, [dproj0])
    gr_attn_in = _mm_tn("d_attn_in", h0, dproj0, weights["attn_w_in"])
    st_mix0, tok = start_reduce("mix0", ["attn_w_out", "w_mem_kv0", "attn_w_in"], [gr_attn_out, gr_kv0, gr_attn_in])
    dh0 = _mm_nt("d_h_0", [dproj0], [weights["attn_w_in"]], after=[tok])[0]
    d0, g_mix0 = _rms_bwd("mix_norm_bwd_0", x0, mix_norm[0:1], dh0, d1)

    small_b = [g_mix0, g_mem0]
    sb_ssem, sb_rsem, sb_slots, tok = _small_start("small_start_b", _own_slot(_pack(small_b, 8), dev))
    sa_slots = _small_wait("small_wait_a", sa_ssem, sa_rsem, sa_slots, [tok])
    g_mix1, g_mem1, g_ffn, g_wsp, g_bsp, g_final, g_ln_g, g_ln_b = _unpack(_sum_devices("small_sum_a", sa_slots),
                                                                           [t.shape for t in small_a])
    sb_slots = _small_wait("small_wait_b", sb_ssem, sb_rsem, sb_slots, [g_final])
    g_mix0, g_mem0 = _unpack(_sum_devices("small_sum_b", sb_slots), [t.shape for t in small_b])
    st_mix0, tok = finish_reduce(st_mix0, [g_mix0])
    g_mix, g_mem = jnp.concatenate([g_mix0, g_mix1]), jnp.concatenate([g_mem0, g_mem1])
    shard_w = sgu_ln_g.shape[-1]
    g_ln_g = lax.dynamic_slice_in_dim(g_ln_g, chip * shard_w, shard_w, axis=1)
    g_ln_b = lax.dynamic_slice_in_dim(g_ln_b, chip * shard_w, shard_w, axis=1)
    small_names = ["mix_norm", "mem_norm", "ffn_norm", "sgu_w_spatial", "sgu_b_spatial", "final_norm", "sgu_ln_g",
                   "sgu_ln_b"]
    small_g = [g_mix, g_mem, g_ffn, g_wsp, g_bsp, g_final, g_ln_g, g_ln_b]
    small_shapes = [given_w[k].shape for k in small_names]
    packed = [_pack(t, LANES) for t in ([given_w[k] for k in small_names], small_g, [given_m[k] for k in small_names],
                                    [given_v[k] for k in small_names])]
    small_out = _adamw("adamw_small", packed[0][None], packed[1], packed[2][None], packed[3][None], 0)
    finish_update(st_mix0, [small_out[0]])
    for k, gk, dk, mk, vk in zip(small_names, *[_unpack(t[0], small_shapes) for t in small_out]):
        outputs[k] = (gk, dk, mk, vk)

    order = ["mix_norm", "mem_norm", "w_mem_kv", "ffn_norm", "w_gate", "w_up", "w_down", "attn_w_in", "attn_w_out",
             "sgu_w_in", "sgu_ln_g", "sgu_ln_b", "sgu_w_spatial", "sgu_b_spatial", "sgu_w_out", "final_norm"]
    return (loss, d0[None], *[outputs[k][0] for k in order], *[outputs[k][1] for k in order],
            *[outputs[k][2] for k in order], *[outputs[k][3] for k in order])
```

```python
import math

import jax
import jax.numpy as jnp
from jax import lax
from jax.experimental import pallas as pl
from jax.experimental.pallas import tpu as pltpu

F32 = jnp.float32
BF = jnp.bfloat16
MESH = pl.DeviceIdType.MESH

HEAD_DIM = 128
MEM_HEADS = 4
MEM_W = MEM_HEADS * HEAD_DIM
GROUP_W = 4 * HEAD_DIM
DILATIONS = (1, 4, 16)
BLK = 128
SGU_GROUPS = 12
SGU_W = SGU_GROUPS * HEAD_DIM
ROT_HALF = 16
ROPE_THETA = 500000.0
NORM_EPS = 1e-6
LN_EPS = 1e-5
NEG_INF = -1e30
SCALE = HEAD_DIM ** -0.5
ADAM_LR, ADAM_B1, ADAM_B2, ADAM_EPS, ADAM_WD, ADAM_STEP = 0.001, 0.9, 0.999, 1e-08, 0.01, 10

VMEM_LIMIT = 48 * 2 ** 20
VMEM_TILE_BUDGET = 38 * 2 ** 20
N_CHIPS = 4
N_DEV = 8
LANES = 128
EPILOGUE_CHUNK = 256

NT_DIMS = (((1,), (1,)), ((), ()))
TN_DIMS = (((0,), (0,)), ((), ()))
NN_DIMS = (((1,), (0,)), ((), ()))

ANY = pl.BlockSpec(memory_space=pl.ANY)
HBM = pl.BlockSpec(memory_space=pltpu.HBM)
SEM = pl.BlockSpec(memory_space=pltpu.SEMAPHORE)
EFFECT = pltpu.SideEffectType.DATAFLOW_SIDE_EFFECTING


def _params(sem):
    return pltpu.CompilerParams(dimension_semantics=sem, vmem_limit_bytes=VMEM_LIMIT)


def _pick(n, cap):
    if n <= cap:
        return n
    best = None
    for t in range(LANES, cap + 1, LANES):
        if n % t == 0:
            best = t
    assert best is not None, (n, cap)
    return best


def _pick_rows(n, cap):
    t = min(n, cap)
    while n % t:
        t //= 2
    return t


def _mm(name, dims, a_list, a_specs, b_list, b_specs, pairs, n_acc, acc_shape, grid, extras, e_specs,
        out_shapes, out_specs, epilogue, after=(), col_chunk=None, store=None, shard_width=None, norm_gain=None):
    na, nb, ne, no = len(a_list), len(b_list), len(extras), len(out_shapes)
    nk = grid[-1]
    ng = 0 if norm_gain is None else 1

    def products(a, b, cols=None):
        sums = [None] * n_acc
        for ai, bi, ci in pairs:
            bv = b[bi]
            if cols is None:
                bv = bv[...]
            elif dims == NT_DIMS:
                bv = bv[cols, :]
            else:
                bv = bv[:, cols]
            if bv.ndim == 3:
                bv = bv.reshape(-1, bv.shape[-1])
            prod = lax.dot_general(a[ai][...].astype(BF), bv.astype(BF), dims, preferred_element_type=F32)
            sums[ci] = prod if sums[ci] is None else sums[ci] + prod
        return sums

    def body(*refs):
        a = refs[:na]
        b = refs[na:na + nb]
        e = refs[na + nb:na + nb + ne]
        off = na + nb + ne + ng + len(after)
        o = refs[off:off + no]
        acc = refs[off + no:]

        def normed():
            if ng:
                xf = o[0][...]
                r = lax.rsqrt(jnp.mean(xf * xf, axis=-1, keepdims=True) + NORM_EPS)
                o[-1][...] = (xf * r * refs[na + nb + ne][...]).astype(o[-1].dtype)

        def finish(sums):
            outs = epilogue(sums, [r[...] for r in e])
            if store is not None:
                store(o, outs)
                return
            for r, v in zip(o, outs):
                r[...] = v.astype(r.dtype)
            normed()

        if nk == 1 and shard_width:
            (ai, bi, _), = pairs
            av = a[ai][...].astype(BF)
            if dims == NT_DIMS:
                total = None
                for j in range(N_CHIPS):
                    cols = slice(j * shard_width, (j + 1) * shard_width)
                    prod = lax.dot_general(av[:, cols], b[bi][j].astype(BF), dims, preferred_element_type=F32)
                    total = prod if total is None else total + prod
                finish([total])
                return
            for j in range(N_CHIPS):
                cols = slice(j * shard_width, (j + 1) * shard_width)
                prod = lax.dot_general(av, b[bi][j].astype(BF), dims, preferred_element_type=F32)
                outs = epilogue([prod], [r[:, cols] for r in e])
                for r, v in zip(o, outs):
                    r[:, cols] = v.astype(r.dtype)
            normed()
            return
        if nk == 1 and col_chunk:
            width = acc_shape[1]
            left = [r[...].astype(BF) for r in a]
            for c0 in range(0, width, col_chunk):
                cols = slice(c0, min(c0 + col_chunk, width))
                outs = epilogue(products(left, b, cols), [r[:, cols] for r in e])
                for r, v in zip(o, outs):
                    r[:, cols] = v.astype(r.dtype)
            return
        if nk == 1:
            finish(products(a, b))
            return
        k = pl.program_id(len(grid) - 1)

        @pl.when(k == 0)
        def _():
            for c, v in zip(acc, products(a, b)):
                c[...] = v

        @pl.when(jnp.logical_and(k > 0, k < nk - 1))
        def _():
            for c, v in zip(acc, products(a, b)):
                c[...] += v

        @pl.when(k == nk - 1)
        def _():
            finish([c[...] + v for c, v in zip(acc, products(a, b))])

    gains = [] if norm_gain is None else [norm_gain]
    ins = [*a_list, *b_list, *extras, *gains, *after]
    in_specs = [*a_specs, *b_specs, *e_specs, *[pl.BlockSpec(g.shape, lambda *_: (0, 0)) for g in gains],
                *([ANY] * len(after))]
    sem = ("parallel",) * (len(grid) - 1) + ("arbitrary",)
    scratch = [] if nk == 1 else [pltpu.VMEM(acc_shape, F32)] * n_acc
    return pl.pallas_call(
        body, out_shape=out_shapes, grid=grid, in_specs=in_specs, out_specs=out_specs, scratch_shapes=scratch,
        name=name, compiler_params=_params(sem))(*ins)


def _tile_bytes(blocks, single=()):
    size = lambda s, d: math.prod(s) * jnp.dtype(d).itemsize
    return sum(2 * size(s, d) for s, d in blocks) + sum(size(s, d) for s, d in single)


def _first(acc, extra):
    return [acc[0]]


class Weight:
    def __init__(self, arr, axis):
        self.arr, self.axis = arr, axis
        _, self.rows, self.cols = arr.shape


SMALL_WEIGHT_BYTES = 8 * 2 ** 20


def _is_small(w):
    return w.arr.size * w.arr.dtype.itemsize <= SMALL_WEIGHT_BYTES


def _mm_nn(name, a, w, extras=(), epilogue=_first, out_dtypes=(F32,), after=(), norm_gain=None):
    m, kdim = a.shape
    b_spec, shard_width = None, None
    weight_buffers = 2
    if norm_gain is not None:
        out_dtypes = (*out_dtypes, BF)
    if w.axis == "col" and _is_small(w):
        n_total = tn = N_CHIPS * w.cols
        tk, gn, gk = kdim, 1, 1
        shard_width = w.cols
        b_spec = pl.BlockSpec((N_CHIPS, kdim, w.cols), lambda n, i, k: (0, 0, 0))
    elif w.axis == "col":
        n_total = N_CHIPS * w.cols
        tn = _pick(w.cols, 1408)
        tk = _pick(kdim, 2048)
        ncb = w.cols // tn
        gn, gk = N_CHIPS * ncb, kdim // tk
        b_map = lambda n, i, k: (n // ncb, k, n % ncb)
    elif kdim <= 2048 and norm_gain is not None:
        n_total = tn = w.cols
        tk, gn, gk = kdim, 1, 1
        weight_buffers = 1
        b_spec = pl.BlockSpec(w.arr.shape, lambda n, i, k: (0, 0, 0), pipeline_mode=pl.Buffered(1))
    elif kdim <= 2048:
        n_total = w.cols
        tn = _pick(w.cols, 1024)
        tk = kdim
        gn, gk = n_total // tn, 1
        b_spec = pl.BlockSpec((N_CHIPS, w.rows, tn), lambda n, i, k: (0, 0, n))
    else:
        n_total = w.cols
        tn = _pick(w.cols, 1024)
        tk = _pick(w.rows, 1408)
        nkb = w.rows // tk
        gn, gk = n_total // tn, N_CHIPS * nkb
        b_map = lambda n, i, k: (k // nkb, k % nkb, n)
    if b_spec is None:
        b_spec = pl.BlockSpec((None, tk, tn), b_map)
    for tm in (1024, 512, 256, 128):
        if m % tm:
            continue
        blocks = [((tm, tk), a.dtype)] + [((tm, tn), e.dtype) for e in extras]
        blocks += [((tm, tn), d) for d in out_dtypes] + [((tm, tn), BF)]
        weight = [((tk, tn), BF)]
        if _tile_bytes(blocks + (weight if weight_buffers == 2 else []), weight if weight_buffers == 1 else ()) <= VMEM_TILE_BUDGET:
            break
    assert norm_gain is None or tn == n_total, name
    o_spec = pl.BlockSpec((tm, tn), lambda n, i, k: (i, n))
    return _mm(
        name, NN_DIMS, [a], [pl.BlockSpec((tm, tk), lambda n, i, k: (i, k))],
        [w.arr], [b_spec], [(0, 0, 0)], 1, (tm, tn), (gn, m // tm, gk),
        list(extras), [o_spec] * len(extras),
        [jax.ShapeDtypeStruct((m, n_total), d) for d in out_dtypes], [o_spec] * len(out_dtypes), epilogue, after,
        shard_width=shard_width, norm_gain=norm_gain)


def _gate_up(name, h, wg, wu):
    m, kdim = h.shape
    tn = _pick(wg.cols, 1408)
    tk = _pick(kdim, 2048)
    ncb = wg.cols // tn
    single = kdim == tk
    for tm in (1024, 512, 256, 128):
        blocks = [((tm, tk), BF)] + [((tm, tn), BF)] * 3
        weights = [((tk, tn), BF)] * 2
        if m % tm == 0 and _tile_bytes(blocks + ([] if single else weights), weights if single else ()) <= VMEM_TILE_BUDGET:
            break
    b_spec = pl.BlockSpec((None, tk, tn), lambda n, i, k: (n // ncb, k, n % ncb),
                          pipeline_mode=pl.Buffered(1) if single else None)
    o_spec = pl.BlockSpec((tm, tn), lambda n, i, k: (i, n))
    n_total = N_CHIPS * wg.cols

    def epilogue(acc, extra):
        g, u = acc
        return [g, u, g * (1.0 / (1.0 + jnp.exp(-g))) * u]

    return _mm(
        name, NN_DIMS, [h], [pl.BlockSpec((tm, tk), lambda n, i, k: (i, k))], [wg.arr, wu.arr], [b_spec, b_spec],
        [(0, 0, 0), (0, 1, 1)], 2, (tm, tn), (N_CHIPS * ncb, m // tm, kdim // tk), [], [],
        [jax.ShapeDtypeStruct((m, n_total), BF)] * 3, [o_spec] * 3, epilogue, col_chunk=EPILOGUE_CHUNK)


def _mm_nt(name, dys, ws, out_dtypes=(F32,), extras=(), epilogue=_first, after=(), col_chunk=None):
    m = dys[0].shape[0]
    w0 = ws[0]
    npair = len(dys)
    b_spec, shard_width = None, None
    if w0.axis == "col" and npair == 1 and _is_small(w0):
        k_total = tko = w0.rows
        tkc = N_CHIPS * w0.cols
        go, gk = 1, 1
        shard_width = w0.cols
        b_spec = pl.BlockSpec(w0.arr.shape, lambda o, i, k: (0, 0, 0))
    elif w0.axis == "col":
        k_total = w0.rows
        tko = _pick(k_total, 1024)
        tkc = _pick(w0.cols, 1408)
        nkb = w0.cols // tkc
        go, gk = k_total // tko, N_CHIPS * nkb
        b_map = lambda o, i, k: (k // nkb, o, k % nkb)
    else:
        k_total = N_CHIPS * w0.rows
        tko = _pick(w0.rows, 1408)
        tkc = _pick(w0.cols, 2048)
        nob = w0.rows // tko
        go, gk = N_CHIPS * nob, w0.cols // tkc
        b_map = lambda o, i, k: (o // nob, o % nob, k)
    single = gk == 1
    for tm in (1024, 512, 256, 128):
        if m % tm:
            continue
        blocks = [((tm, tkc), d.dtype) for d in dys]
        blocks += [((tm, tko), e.dtype) for e in extras] + [((tm, tko), d) for d in out_dtypes]
        blocks += [((tm, tko), BF)]
        weights = [((tko, tkc), BF)] * npair
        if _tile_bytes(blocks + ([] if single else weights), weights if single else ()) <= VMEM_TILE_BUDGET:
            break
    if b_spec is None:
        b_spec = pl.BlockSpec((None, tko, tkc), b_map, pipeline_mode=pl.Buffered(1) if single else None)
    o_spec = pl.BlockSpec((tm, tko), lambda o, i, k: (i, o))
    return _mm(
        name, NT_DIMS, list(dys), [pl.BlockSpec((tm, tkc), lambda o, i, k: (i, k))] * npair,
        [w.arr for w in ws], [b_spec] * npair,
        [(i, i, 0) for i in range(npair)], 1, (tm, tko), (go, m // tm, gk), list(extras), [o_spec] * len(extras),
        [jax.ShapeDtypeStruct((m, k_total), d) for d in out_dtypes], [o_spec] * len(out_dtypes), epilogue, after,
        col_chunk if gk == 1 and shard_width is None else None, shard_width=shard_width)


def _mm_tn(name, a, dy, w, after=()):
    m, k_total = a.shape
    rows2 = w.rows // 2
    tn = _pick(w.cols, 1408)
    ncb = w.cols // tn
    epilogue, store = _first, None
    if k_total <= 2048 and w.axis == "col" and _is_small(w):
        tkr, tn = k_total, N_CHIPS * w.cols
        gr, gn = 1, 1
        o_spec = pl.BlockSpec((2, N_CHIPS, rows2, w.cols), lambda r, n, t: (0, 0, 0, 0))

        def store(o_refs, outs):
            for j in range(N_CHIPS):
                for h in range(2):
                    o_refs[0][h, j] = outs[0][h * rows2:(h + 1) * rows2, j * w.cols:(j + 1) * w.cols].astype(BF)
    elif k_total <= 2048 and w.axis == "col":
        tkr = k_total
        gr, gn = 1, N_CHIPS * ncb
        o_spec = pl.BlockSpec((2, None, rows2, tn), lambda r, n, t: (0, n // ncb, 0, n % ncb))
        epilogue = lambda acc, extra: [acc[0].reshape(2, rows2, tn)]
    elif k_total <= 2048:
        tkr = k_total
        gr, gn = 1, ncb
        o_spec = pl.BlockSpec((2, N_CHIPS, rows2, tn), lambda r, n, t: (0, 0, 0, n))

        def store(o_refs, outs):
            for j in range(N_CHIPS):
                for h in range(2):
                    lo = (2 * j + h) * rows2
                    o_refs[0][h, j] = outs[0][lo:lo + rows2].astype(BF)
    elif rows2 % LANES:
        tkr = w.rows
        assert w.axis == "row"
        gr, gn = N_CHIPS, ncb
        o_spec = pl.BlockSpec((2, None, rows2, tn), lambda r, n, t: (0, r, 0, n))
        epilogue = lambda acc, extra: [acc[0].reshape(2, rows2, tn)]
    else:
        tkr = _pick(rows2, 1408)
        nrb = rows2 // tkr
        if w.axis == "col":
            gr, gn = w.rows // tkr, N_CHIPS * ncb
            o_map = lambda r, n, t: (r // nrb, n // ncb, r % nrb, n % ncb)
        else:
            per = w.rows // tkr
            gr, gn = N_CHIPS * per, ncb
            o_map = lambda r, n, t: ((r % per) // nrb, r // per, (r % per) % nrb, n)
        o_spec = pl.BlockSpec((None, None, tkr, tn), o_map)
    for tmk in (1024, 512, 256, 128):
        blocks = [((tmk, tkr), a.dtype), ((tmk, tn), dy.dtype), ((tkr, tn), BF), ((tkr, tn), BF)]
        if m % tmk == 0 and _tile_bytes(blocks) <= VMEM_TILE_BUDGET:
            break
    return _mm(
        name, TN_DIMS, [a], [pl.BlockSpec((tmk, tkr), lambda r, n, t: (t, r))],
        [dy], [pl.BlockSpec((tmk, tn), lambda r, n, t: (t, n))], [(0, 0, 0)], 1, (tkr, tn), (gr, gn, m // tmk), [], [],
        [jax.ShapeDtypeStruct((2, N_CHIPS, rows2, w.cols), BF)], [o_spec], epilogue, after, store=store)[0]


def _rms_fwd(name, x, g, after=()):
    s, d = x.shape
    tr = _pick_rows(s, 512)

    def body(x_ref, g_ref, *rest):
        h_ref = rest[-1]
        xf = x_ref[...]
        r = lax.rsqrt(jnp.mean(xf * xf, axis=-1, keepdims=True) + NORM_EPS)
        h_ref[...] = (xf * r * g_ref[...]).astype(BF)

    return pl.pallas_call(
        body, out_shape=jax.ShapeDtypeStruct((s, d), BF), grid=(s // tr,),
        in_specs=[pl.BlockSpec((tr, d), lambda i: (i, 0)), pl.BlockSpec((1, d), lambda i: (0, 0))] + [ANY] * len(after),
        out_specs=pl.BlockSpec((tr, d), lambda i: (i, 0)), name=name, compiler_params=_params(("parallel",)))(x, g, *after)


def _rms_bwd(name, x, g, dh, dres=None):
    s, d = x.shape
    tr = _pick_rows(s, 256)
    has_res = dres is not None

    def body(*refs):
        if has_res:
            x_ref, g_ref, dh_ref, dres_ref, dx_ref, dg_ref = refs
        else:
            x_ref, g_ref, dh_ref, dx_ref, dg_ref = refs
        xf = x_ref[...]
        r = lax.rsqrt(jnp.mean(xf * xf, axis=-1, keepdims=True) + NORM_EPS)
        xr = xf * r
        dy = dh_ref[...].astype(F32)
        a = dy * g_ref[...]
        dx = r * (a - xr * jnp.mean(a * xr, axis=-1, keepdims=True))
        if has_res:
            dx = dx + dres_ref[...]
        dx_ref[...] = dx

        @pl.when(pl.program_id(0) == 0)
        def _():
            dg_ref[...] = jnp.zeros_like(dg_ref)

        dg_ref[...] += jnp.sum(dy * xr, axis=0, keepdims=True)

    row = pl.BlockSpec((tr, d), lambda i: (i, 0))
    vec = pl.BlockSpec((1, d), lambda i: (0, 0))
    ins = [x, g, dh] + ([dres] if has_res else [])
    in_specs = [row, vec, row] + ([row] if has_res else [])
    return pl.pallas_call(
        body, out_shape=[jax.ShapeDtypeStruct((s, d), F32), jax.ShapeDtypeStruct((1, d), F32)], grid=(s // tr,),
        in_specs=in_specs, out_specs=[row, vec], name=name, compiler_params=_params(("arbitrary",)))(*ins)


def _final_loss(x, tgt, g):
    s, d = x.shape
    tr = _pick_rows(s, 256)

    def body(x_ref, t_ref, g_ref, dx_ref, dg_ref, loss_ref):
        xf = x_ref[...]
        gain = g_ref[...]
        r = lax.rsqrt(jnp.mean(xf * xf, axis=-1, keepdims=True) + NORM_EPS)
        xr = xf * r
        err = xr * gain - t_ref[...]
        dy = err * (1.0 / d)
        a = dy * gain
        dx_ref[...] = r * (a - xr * jnp.mean(a * xr, axis=-1, keepdims=True))

        @pl.when(pl.program_id(0) == 0)
        def _():
            dg_ref[...] = jnp.zeros_like(dg_ref)
            loss_ref[...] = jnp.zeros_like(loss_ref)

        dg_ref[...] += jnp.sum(dy * xr, axis=0, keepdims=True)
        part = 0.5 * jnp.sum(jnp.mean(err * err, axis=-1, keepdims=True), axis=0, keepdims=True)
        loss_ref[...] += jnp.broadcast_to(part, loss_ref.shape)

    row = pl.BlockSpec((tr, d), lambda i: (i, 0))
    vec = pl.BlockSpec((1, d), lambda i: (0, 0))
    return pl.pallas_call(
        body, out_shape=[jax.ShapeDtypeStruct((s, d), F32), jax.ShapeDtypeStruct((1, d), F32),
                         jax.ShapeDtypeStruct((8, LANES), F32)],
        grid=(s // tr,), in_specs=[row, row, vec], out_specs=[row, vec, pl.BlockSpec((8, LANES), lambda i: (0, 0))],
        name="final_loss", compiler_params=_params(("arbitrary",)))(x, tgt, g)


def _swiglu_bwd_epilogue(acc, extra):
    dact = acc[0]
    g, u = extra[0].astype(F32), extra[1].astype(F32)
    sig = 1.0 / (1.0 + jnp.exp(-g))
    return [dact * u * sig * (1.0 + g * (1.0 - sig)), dact * g * sig]


GELU_C = math.sqrt(2.0 / math.pi)
GELU_A = 0.044715


def _gelu(x):
    return 0.5 * x * (1.0 + jnp.tanh(GELU_C * (x + GELU_A * x * x * x)))


def _gelu_both(x):
    x2 = x * x
    t = jnp.tanh(GELU_C * (x + GELU_A * x2 * x))
    half = 0.5 * (1.0 + t)
    return x * half, half + 0.5 * x * (1.0 - t * t) * GELU_C * (1.0 + 3.0 * GELU_A * x2)


def _rope_tables(positions):
    inv_freq = ROPE_THETA ** (-jnp.arange(ROT_HALF, dtype=F32) / ROT_HALF)
    ang = positions.astype(F32)[:, None] * inv_freq
    cos, sin = jnp.cos(ang), jnp.sin(ang)
    s = ang.shape[0]
    rest = HEAD_DIM - 2 * ROT_HALF
    zeros = jnp.zeros((s, ROT_HALF), F32)
    cos_t = jnp.concatenate([cos, cos, jnp.ones((s, rest), F32)], axis=1)
    sin_a = jnp.concatenate([-sin, zeros, jnp.zeros((s, rest), F32)], axis=1)
    sin_b = jnp.concatenate([zeros, sin, jnp.zeros((s, rest), F32)], axis=1)
    return cos_t, sin_a, sin_b


def _rope_head(xh, cos_t, sin_a, sin_b):
    up = pltpu.roll(xh, HEAD_DIM - ROT_HALF, 1)
    down = pltpu.roll(xh, ROT_HALF, 1)
    return xh * cos_t + up * sin_a + down * sin_b


def _residue(r, rows, dil):
    return slice(None) if dil == 1 else pl.ds(r, rows, stride=dil)


ROPE_TILE = 256
N_PARTS = 9
HEADS_PER_GROUP = GROUP_W // HEAD_DIM
N_HEADS_IN = N_PARTS * HEADS_PER_GROUP


def _rope_fwd(proj, tables):
    s = proj.shape[0]
    tm = _pick_rows(s, ROPE_TILE)

    def body(*refs):
        heads = refs[:N_HEADS_IN]
        c_ref, sa_ref, sb_ref = refs[N_HEADS_IN:N_HEADS_IN + 3]
        outs = refs[N_HEADS_IN + 3:]
        for g, dil in enumerate(DILATIONS):
            rows = tm // dil
            for r in range(dil):
                rs = _residue(r, rows, dil)
                cos_t, sin_a, sin_b = c_ref[rs, :], sa_ref[rs, :], sb_ref[rs, :]
                for kind in range(3):
                    part = 3 * kind + g
                    for h in range(HEADS_PER_GROUP):
                        xh = heads[part * HEADS_PER_GROUP + h][rs, :]
                        if kind < 2:
                            xh = _rope_head(xh, cos_t, sin_a, sin_b)
                        outs[part][r, :, h * HEAD_DIM:(h + 1) * HEAD_DIM] = xh.astype(BF)

    tab = pl.BlockSpec((tm, HEAD_DIM), lambda i: (i, 0))
    head_specs = [pl.BlockSpec((tm, HEAD_DIM), lambda i, j=j: (i, j)) for j in range(N_HEADS_IN)]
    shapes, specs = [], []
    for part in range(N_PARTS):
        dil = DILATIONS[part % 3]
        shapes.append(jax.ShapeDtypeStruct((dil, s // dil, GROUP_W), BF))
        specs.append(pl.BlockSpec((dil, tm // dil, GROUP_W), lambda i: (0, i, 0)))
    return pl.pallas_call(
        body, out_shape=shapes, grid=(s // tm,), in_specs=head_specs + [tab, tab, tab], out_specs=specs,
        name="rope_fwd", compiler_params=_params(("parallel",)))(*([proj] * N_HEADS_IN), *tables)


def _rope_bwd(parts, tables, into):
    s = into.shape[0]
    tm = _pick_rows(s, ROPE_TILE)

    def body(*refs):
        ins = refs[:N_PARTS]
        c_ref, sa_ref, sb_ref, into_ref, o_ref, scr = refs[N_PARTS:]
        for g, dil in enumerate(DILATIONS):
            rows = tm // dil
            for r in range(dil):
                rs = _residue(r, rows, dil)
                cos_t, sin_a, sin_b = c_ref[rs, :], -sa_ref[rs, :], -sb_ref[rs, :]
                for kind in range(3):
                    part = 3 * kind + g
                    for h in range(HEADS_PER_GROUP):
                        xh = ins[part][r, :, h * HEAD_DIM:(h + 1) * HEAD_DIM]
                        if kind < 2:
                            xh = _rope_head(xh, cos_t, sin_a, sin_b)
                        scr[part * HEADS_PER_GROUP + h, rs, :] = xh
        for j in range(N_HEADS_IN):
            o_ref[:, j * HEAD_DIM:(j + 1) * HEAD_DIM] = scr[j].astype(BF)

    tab = pl.BlockSpec((tm, HEAD_DIM), lambda i: (i, 0))
    i_specs = [pl.BlockSpec((DILATIONS[p % 3], tm // DILATIONS[p % 3], GROUP_W), lambda i: (0, i, 0))
               for p in range(N_PARTS)]
    return pl.pallas_call(
        body, out_shape=jax.ShapeDtypeStruct(into.shape, into.dtype), grid=(s // tm,),
        in_specs=i_specs + [tab] * 3 + [ANY], out_specs=pl.BlockSpec((tm, N_PARTS * GROUP_W), lambda i: (i, 0)),
        scratch_shapes=[pltpu.VMEM((N_HEADS_IN, tm, HEAD_DIM), F32)], input_output_aliases={N_PARTS + 3: 0},
        name="rope_bwd", compiler_params=_params(("parallel",)))(*parts, *tables, into)


def _band_mask(n):
    qi = lax.broadcasted_iota(jnp.int32, (BLK, 2 * BLK), 0)
    ki = lax.broadcasted_iota(jnp.int32, (BLK, 2 * BLK), 1)
    prev = jnp.logical_and(jnp.logical_and(ki < BLK, ki >= qi), n > 0)
    return jnp.logical_or(prev, jnp.logical_and(ki >= BLK, qi >= ki - BLK))


Q_BLOCKS = 2
Q_ROWS = Q_BLOCKS * BLK


def _dil_specs(n_steps):
    last = n_steps - 1
    own = pl.BlockSpec((None, Q_ROWS, GROUP_W), lambda r, n: (r, jnp.minimum(n, last), 0))
    before = pl.BlockSpec((None, BLK, GROUP_W), lambda r, n: (r, jnp.maximum(Q_BLOCKS * n - 1, 0), 0))
    return own, before


def _dil_fwd(g, q, k, v):
    dil, length, _ = q.shape
    n_steps = length // Q_ROWS

    def body(q_ref, ko_ref, kb_ref, vo_ref, vb_ref, o_ref, lse_ref):
        n = pl.program_id(1)
        for h in range(GROUP_W // HEAD_DIM):
            sl = slice(h * HEAD_DIM, (h + 1) * HEAD_DIM)
            keys = jnp.concatenate([kb_ref[:, sl], ko_ref[:, sl]], axis=0)
            vals = jnp.concatenate([vb_ref[:, sl], vo_ref[:, sl]], axis=0)
            for j in range(Q_BLOCKS):
                rows, win = slice(j * BLK, (j + 1) * BLK), slice(j * BLK, (j + 2) * BLK)
                sc = lax.dot_general(q_ref[rows, sl], keys[win], NT_DIMS, preferred_element_type=F32) * SCALE
                sc = jnp.where(_band_mask(Q_BLOCKS * n + j), sc, NEG_INF)
                mx = jnp.max(sc, axis=-1, keepdims=True)
                p = jnp.exp(sc - mx)
                den = jnp.sum(p, axis=-1, keepdims=True)
                o_ref[rows, sl] = jnp.dot(p.astype(BF), vals[win], preferred_element_type=F32) / den
                lse_ref[rows, sl] = jnp.broadcast_to(mx + jnp.log(den), (BLK, HEAD_DIM))

    own, before = _dil_specs(n_steps)
    return pl.pallas_call(
        body, out_shape=[jax.ShapeDtypeStruct(q.shape, F32)] * 2, grid=(dil, n_steps),
        in_specs=[own, own, before, own, before], out_specs=[own, own], name=f"dil_fwd_{g}",
        compiler_params=_params(("parallel", "arbitrary")))(q, k, k, v, v)


def _dil_bwd(g, q, k, v, do, lse, delta):
    dil, length, _ = q.shape
    n_steps = length // Q_ROWS

    def body(q_ref, ko_ref, kb_ref, vo_ref, vb_ref, do_ref, lse_ref, dl_ref, dq_ref, dk_ref, dv_ref, ck_ref, cv_ref):
        n = pl.program_id(1)
        live = n < n_steps

        @pl.when(n == 0)
        def _():
            ck_ref[...] = jnp.zeros_like(ck_ref)
            cv_ref[...] = jnp.zeros_like(cv_ref)

        @pl.when(jnp.logical_not(live))
        def _():
            dk_ref[...] = ck_ref[...]
            dv_ref[...] = cv_ref[...]

        @pl.when(live)
        def _():
            for h in range(GROUP_W // HEAD_DIM):
                sl = slice(h * HEAD_DIM, (h + 1) * HEAD_DIM)
                keys = jnp.concatenate([kb_ref[:, sl], ko_ref[:, sl]], axis=0)
                vals = jnp.concatenate([vb_ref[:, sl], vo_ref[:, sl]], axis=0)
                dks, dvs = [], []
                for j in range(Q_BLOCKS):
                    rows, win = slice(j * BLK, (j + 1) * BLK), slice(j * BLK, (j + 2) * BLK)
                    qh, doh = q_ref[rows, sl], do_ref[rows, sl]
                    lse_h = lse_ref[rows, h * HEAD_DIM:h * HEAD_DIM + 1]
                    dl_h = dl_ref[rows, h * HEAD_DIM:h * HEAD_DIM + 1]
                    sc = lax.dot_general(qh, keys[win], NT_DIMS, preferred_element_type=F32) * SCALE
                    p = jnp.where(_band_mask(Q_BLOCKS * n + j), jnp.exp(jnp.minimum(sc - lse_h, 0.0)), 0.0)
                    dp = lax.dot_general(doh, vals[win], NT_DIMS, preferred_element_type=F32)
                    ds = (p * (dp - dl_h) * SCALE).astype(BF)
                    dq_ref[rows, sl] = jnp.dot(ds, keys[win], preferred_element_type=F32)
                    dks.append(lax.dot_general(ds, qh, TN_DIMS, preferred_element_type=F32))
                    dvs.append(lax.dot_general(p.astype(BF), doh, TN_DIMS, preferred_element_type=F32))
                for out_ref, carry, parts in ((dk_ref, ck_ref, dks), (dv_ref, cv_ref, dvs)):
                    out_ref[:Q_ROWS - BLK, sl] = carry[:Q_ROWS - BLK, sl]
                    out_ref[Q_ROWS - BLK:, sl] = carry[Q_ROWS - BLK:, sl] + parts[0][:BLK]
                    for j in range(Q_BLOCKS - 1):
                        carry[j * BLK:(j + 1) * BLK, sl] = parts[j][BLK:] + parts[j + 1][:BLK]
                    carry[Q_ROWS - BLK:, sl] = parts[-1][BLK:]

    own, before = _dil_specs(n_steps)
    behind = pl.BlockSpec((None, Q_ROWS, GROUP_W), lambda r, n: (r, jnp.maximum(n - 1, 0), 0))
    return pl.pallas_call(
        body, out_shape=[jax.ShapeDtypeStruct(q.shape, F32)] * 3, grid=(dil, n_steps + 1),
        in_specs=[own, own, before, own, before, own, own, own], out_specs=[own, behind, behind],
        scratch_shapes=[pltpu.VMEM((Q_ROWS, GROUP_W), F32)] * 2, name=f"dil_bwd_{g}",
        compiler_params=_params(("parallel", "arbitrary")))(q, k, k, v, v, do, lse, delta)


def _major_specs(s, tm, dtype):
    shapes = [jax.ShapeDtypeStruct((dil, s // dil, GROUP_W), dtype) for dil in DILATIONS]
    specs = [pl.BlockSpec((dil, tm // dil, GROUP_W), lambda i: (0, i, 0)) for dil in DILATIONS]
    return shapes, specs


def _attn_merge(outs, lses):
    s = outs[0].shape[1]
    tm = _pick_rows(s, ROPE_TILE)

    def body(o0, o1, o2, l0, l1, l2, m_ref, e0, e1, e2, so1, so2, sl1, sl2, se):
        for h in range(HEADS_PER_GROUP):
            sl = slice(h * HEAD_DIM, (h + 1) * HEAD_DIM)
            for dil, src, dst in ((DILATIONS[1], o1, so1), (DILATIONS[2], o2, so2), (DILATIONS[1], l1, sl1),
                                  (DILATIONS[2], l2, sl2)):
                for r in range(dil):
                    dst[h, _residue(r, tm // dil, dil), :] = src[r, :, sl]
            a, b, c = l0[0, :, sl], sl1[h], sl2[h]
            mx = jnp.maximum(jnp.maximum(a, b), c)
            ea, eb, ec = jnp.exp(a - mx), jnp.exp(b - mx), jnp.exp(c - mx)
            den = ea + eb + ec
            m_ref[:, sl] = ((ea * o0[0, :, sl] + eb * so1[h] + ec * so2[h]) / den).astype(BF)
            se[h] = mx + jnp.log(den)
            for dil, dst in zip(DILATIONS, (e0, e1, e2)):
                for r in range(dil):
                    dst[r, :, sl] = se[h, _residue(r, tm // dil, dil), :]

    shapes, specs = _major_specs(s, tm, F32)
    nat = pl.BlockSpec((tm, GROUP_W), lambda i: (i, 0))
    res = pl.pallas_call(
        body, out_shape=[jax.ShapeDtypeStruct((s, GROUP_W + MEM_W), BF)] + shapes, grid=(s // tm,), in_specs=specs * 2,
        out_specs=[nat] + specs, scratch_shapes=[pltpu.VMEM((HEADS_PER_GROUP, tm, HEAD_DIM), F32)] * 5,
        name="attn_merge", compiler_params=_params(("parallel",)))(*outs, *lses)
    return res[0], res[1:]


def _attn_delta(dcat, merged, after=()):
    s = merged.shape[0]
    tm = _pick_rows(s, ROPE_TILE)

    def body(*refs):
        d_refs, m_ref = refs[:HEADS_PER_GROUP], refs[HEADS_PER_GROUP]
        do_refs, dl_refs, scr = refs[-7:-4], refs[-4:-1], refs[-1]
        for h in range(HEADS_PER_GROUP):
            sl = slice(h * HEAD_DIM, (h + 1) * HEAD_DIM)
            prod = d_refs[h][...] * m_ref[:, sl].astype(F32)
            scr[h] = jnp.broadcast_to(jnp.sum(prod, axis=-1, keepdims=True), (tm, HEAD_DIM))
            for dil, do_ref, dl_ref in zip(DILATIONS, do_refs, dl_refs):
                for r in range(dil):
                    rs = _residue(r, tm // dil, dil)
                    do_ref[r, :, sl] = d_refs[h][rs, :].astype(BF)
                    dl_ref[r, :, sl] = scr[h, rs, :]

    nat = pl.BlockSpec((tm, GROUP_W), lambda i: (i, 0))
    head_specs = [pl.BlockSpec((tm, HEAD_DIM), lambda i, h=h: (i, h)) for h in range(HEADS_PER_GROUP)]
    bf_shapes, specs = _major_specs(s, tm, BF)
    f_shapes, _ = _major_specs(s, tm, F32)
    res = pl.pallas_call(
        body, out_shape=bf_shapes + f_shapes, grid=(s // tm,), in_specs=head_specs + [nat] + [ANY] * len(after),
        out_specs=specs * 2, scratch_shapes=[pltpu.VMEM((HEADS_PER_GROUP, tm, HEAD_DIM), F32)], name="attn_delta",
        compiler_params=_params(("parallel",)))(*([dcat] * HEADS_PER_GROUP), merged, *after)
    return res[:3], res[3:]


def _mem_probs(qh, kh):
    sc = lax.dot_general(qh, kh, NT_DIMS, preferred_element_type=F32) * SCALE
    p = jnp.exp(sc - jnp.max(sc, axis=-1, keepdims=True))
    return p, jnp.sum(p, axis=-1, keepdims=True)


def _mem_fwd(name, proj, q_block, kv, into, out_block):
    s = proj.shape[0]
    tq = _pick_rows(s, 512)

    def body(q_ref, kv_ref, into_ref, o_ref):
        for h in range(MEM_HEADS):
            sl = slice(h * HEAD_DIM, (h + 1) * HEAD_DIM)
            vsl = slice(MEM_W + h * HEAD_DIM, MEM_W + (h + 1) * HEAD_DIM)
            p, den = _mem_probs(q_ref[:, sl].astype(BF), kv_ref[:, sl].astype(BF))
            out = jnp.dot(p.astype(BF), kv_ref[:, vsl].astype(BF), preferred_element_type=F32) / den
            o_ref[:, sl] = out.astype(o_ref.dtype)

    return pl.pallas_call(
        body, out_shape=jax.ShapeDtypeStruct(into.shape, into.dtype), grid=(s // tq,),
        in_specs=[pl.BlockSpec((tq, MEM_W), lambda i: (i, q_block)), pl.BlockSpec(kv.shape, lambda i: (0, 0)), ANY],
        out_specs=pl.BlockSpec((tq, MEM_W), lambda i: (i, out_block)), input_output_aliases={2: 0}, name=name,
        compiler_params=_params(("parallel",)))(proj, kv, into)


def _mem_bwd(name, proj, q_block, kv, dcat, d_block, width):
    s = proj.shape[0]
    tq = _pick_rows(s, 512)

    def body(q_ref, kv_ref, do_ref, dq_ref, dkv_ref):
        @pl.when(pl.program_id(0) == 0)
        def _():
            dkv_ref[...] = jnp.zeros_like(dkv_ref)

        for h in range(MEM_HEADS):
            sl = slice(h * HEAD_DIM, (h + 1) * HEAD_DIM)
            vsl = slice(MEM_W + h * HEAD_DIM, MEM_W + (h + 1) * HEAD_DIM)
            qh, kh, vh = q_ref[:, sl].astype(BF), kv_ref[:, sl].astype(BF), kv_ref[:, vsl].astype(BF)
            doh = do_ref[:, sl].astype(BF)
            p, den = _mem_probs(qh, kh)
            p = p / den
            dp = lax.dot_general(doh, vh, NT_DIMS, preferred_element_type=F32)
            ds = (p * (dp - jnp.sum(p * dp, axis=-1, keepdims=True)) * SCALE).astype(BF)
            dq_ref[:, sl] = jnp.dot(ds, kh, preferred_element_type=F32).astype(BF)
            dkv_ref[:, sl] += lax.dot_general(ds, qh, TN_DIMS, preferred_element_type=F32)
            dkv_ref[:, vsl] += lax.dot_general(p.astype(BF), doh, TN_DIMS, preferred_element_type=F32)

    whole = pl.BlockSpec(kv.shape, lambda i: (0, 0))
    return pl.pallas_call(
        body, out_shape=[jax.ShapeDtypeStruct((s, width), BF), jax.ShapeDtypeStruct(kv.shape, F32)], grid=(s // tq,),
        in_specs=[pl.BlockSpec((tq, MEM_W), lambda i: (i, q_block)), whole,
                  pl.BlockSpec((tq, MEM_W), lambda i: (i, d_block))],
        out_specs=[pl.BlockSpec((tq, MEM_W), lambda i: (i, width // MEM_W - 1)), whole], name=name,
        compiler_params=_params(("arbitrary",)))(proj, kv, dcat)


def _causal():
    t = lax.broadcasted_iota(jnp.int32, (BLK, BLK), 0)
    s = lax.broadcasted_iota(jnp.int32, (BLK, BLK), 1)
    return t >= s


def _sgu_norm(vg, ln_g, ln_b):
    mu = jnp.mean(vg, axis=-1, keepdims=True)
    cen = vg - mu
    rstd = lax.rsqrt(jnp.mean(cen * cen, axis=-1, keepdims=True) + LN_EPS)
    xhat = cen * rstd
    return xhat, rstd, xhat * ln_g + ln_b


def _sgu_fwd(proj, ln_g, ln_b, w_sp, b_t):
    s = proj.shape[0]

    def body(u_ref, v_ref, g_ref, b_ref, w_ref, bt_ref, o_ref):
        _, _, vn = _sgu_norm(_gelu(v_ref[...].astype(F32)), g_ref[...], b_ref[...])
        vn = vn.astype(BF)
        tri = _causal()
        for grp in range(SGU_GROUPS):
            sl = slice(grp * HEAD_DIM, (grp + 1) * HEAD_DIM)
            w = jnp.where(tri, w_ref[grp], 0.0).astype(BF)
            mixed = jnp.dot(w, vn[:, sl], preferred_element_type=F32) + bt_ref[:, grp:grp + 1]
            o_ref[:, sl] = (_gelu(u_ref[:, sl].astype(F32)) * mixed).astype(BF)

    vec = pl.BlockSpec((1, SGU_W), lambda i: (0, 0))
    return pl.pallas_call(
        body, out_shape=jax.ShapeDtypeStruct((s, SGU_W + MEM_W), BF), grid=(s // BLK,),
        in_specs=[pl.BlockSpec((BLK, SGU_W), lambda i: (i, 0)), pl.BlockSpec((BLK, SGU_W), lambda i: (i, 1)), vec, vec,
                  pl.BlockSpec(w_sp.shape, lambda i: (0, 0, 0)), pl.BlockSpec(b_t.shape, lambda i: (0, 0))],
        out_specs=pl.BlockSpec((BLK, SGU_W), lambda i: (i, 0)), name="sgu_fwd",
        compiler_params=_params(("parallel",)))(proj, proj, ln_g, ln_b, w_sp, b_t)


def _sgu_bwd(proj, dcat, ln_g, ln_b, w_sp, b_t, into):
    s = proj.shape[0]

    def body(u_ref, v_ref, d_ref, g_ref, b_ref, w_ref, bt_ref, into_ref, dp_ref, dw_ref, db_ref, dg_ref, dbeta_ref,
             dvn_ref):
        @pl.when(pl.program_id(0) == 0)
        def _():
            dw_ref[...] = jnp.zeros_like(dw_ref)
            db_ref[...] = jnp.zeros_like(db_ref)
            dg_ref[...] = jnp.zeros_like(dg_ref)
            dbeta_ref[...] = jnp.zeros_like(dbeta_ref)

        gain = g_ref[...]
        vg, v_slope = _gelu_both(v_ref[...].astype(F32))
        xhat, rstd, vn = _sgu_norm(vg, gain, b_ref[...])
        vn = vn.astype(BF)
        tri = _causal()
        lane = lax.broadcasted_iota(jnp.int32, (BLK, HEAD_DIM), 1)
        db_acc = jnp.zeros((BLK, HEAD_DIM), F32)
        for grp in range(SGU_GROUPS):
            sl = slice(grp * HEAD_DIM, (grp + 1) * HEAD_DIM)
            w = jnp.where(tri, w_ref[grp], 0.0).astype(BF)
            vn_g = vn[:, sl]
            mixed = jnp.dot(w, vn_g, preferred_element_type=F32) + bt_ref[:, grp:grp + 1]
            u_act, u_slope = _gelu_both(u_ref[:, sl].astype(F32))
            d_out = d_ref[:, sl].astype(F32)
            dp_ref[:, sl] = (d_out * mixed * u_slope).astype(BF)
            dmixed = d_out * u_act
            dm = dmixed.astype(BF)
            dvn_ref[:, sl] = lax.dot_general(w, dm, TN_DIMS, preferred_element_type=F32)
            dw = lax.dot_general(dm, vn_g, NT_DIMS, preferred_element_type=F32)
            dw_ref[grp] += jnp.where(tri, dw, 0.0)
            db_acc += jnp.where(lane == grp, jnp.sum(dmixed, axis=-1, keepdims=True), 0.0)
        db_ref[...] += db_acc
        dvn = dvn_ref[...]
        dg_ref[...] += jnp.sum(dvn * xhat, axis=0, keepdims=True)
        dbeta_ref[...] += jnp.sum(dvn, axis=0, keepdims=True)
        dxh = dvn * gain
        dvg = rstd * (dxh - jnp.mean(dxh, axis=-1, keepdims=True) - xhat * jnp.mean(dxh * xhat, axis=-1, keepdims=True))
        dp_ref[:, SGU_W:] = (dvg * v_slope).astype(BF)

    vec = pl.BlockSpec((1, SGU_W), lambda i: (0, 0))
    row = pl.BlockSpec((BLK, SGU_W), lambda i: (i, 0))
    w_spec = pl.BlockSpec(w_sp.shape, lambda i: (0, 0, 0))
    sq = pl.BlockSpec((BLK, HEAD_DIM), lambda i: (0, 0))
    return pl.pallas_call(
        body,
        out_shape=[jax.ShapeDtypeStruct(into.shape, into.dtype),
                   jax.ShapeDtypeStruct(w_sp.shape, F32), jax.ShapeDtypeStruct((BLK, HEAD_DIM), F32),
                   jax.ShapeDtypeStruct((1, SGU_W), F32), jax.ShapeDtypeStruct((1, SGU_W), F32)],
        grid=(s // BLK,),
        in_specs=[row, pl.BlockSpec((BLK, SGU_W), lambda i: (i, 1)), row, vec, vec, w_spec,
                  pl.BlockSpec(b_t.shape, lambda i: (0, 0)), ANY],
        out_specs=[pl.BlockSpec((BLK, 2 * SGU_W), lambda i: (i, 0)), w_spec, sq, vec, vec],
        scratch_shapes=[pltpu.VMEM((BLK, SGU_W), F32)], input_output_aliases={7: 0}, name="sgu_bwd",
        compiler_params=_params(("arbitrary",)))(proj, proj, dcat, ln_g, ln_b, w_sp, b_t, into)


def _place():
    return lax.axis_index("x"), lax.axis_index("y"), lax.axis_index("c")


def _other_chips(x, y):
    return [(1 - x, y), (x, 1 - y), (1 - x, 1 - y)]


def _peer(x, y, c, mask):
    return (1 - x if mask & 4 else x, 1 - y if mask & 2 else y, 1 - c if mask & 1 else c)


def _in_hbm(a):
    return pltpu.with_memory_space_constraint(a, pltpu.HBM)


def _token_spec():
    return jax.ShapeDtypeStruct((8, LANES), F32), pl.BlockSpec(memory_space=pltpu.VMEM)


def _remote(src, dst, ssem, rsem, to):
    return pltpu.make_async_remote_copy(src_ref=src, dst_ref=dst, send_sem=ssem, recv_sem=rsem, device_id=to,
                                        device_id_type=MESH)


def _place_shard(name, src, layer, place, dtype, after=()):
    _, rows, cols = src.shape
    tr = _pick_rows(rows, 512)

    def body(p_ref, s_ref, *rest):
        rest[-1][...] = s_ref[...].astype(dtype)

    grid_spec = pltpu.PrefetchScalarGridSpec(
        num_scalar_prefetch=1, grid=(rows // tr,),
        in_specs=[pl.BlockSpec((None, tr, cols), lambda i, p: (layer, i, 0))] + [ANY] * len(after),
        out_specs=pl.BlockSpec((None, tr, cols), lambda i, p: (p[1], i, 0)))
    return pl.pallas_call(body, out_shape=jax.ShapeDtypeStruct((N_CHIPS, rows, cols), dtype), grid_spec=grid_spec,
                          name=name, compiler_params=_params(("parallel",)))(place, src, *after)


def _gather_copies(bufs, ssem, rsem):
    x, y, c = _place()
    me = 2 * x + y
    copies = []
    for ai, buf in enumerate(bufs):
        for k, (ox, oy) in enumerate(_other_chips(x, y)):
            copies.append(_remote(buf.at[me], buf.at[me], ssem.at[3 * ai + k], rsem.at[3 * ai + k], (ox, oy, c)))
    return copies


def _reduce_copies(grads, lands, ssem, rsem):
    x, y, c = _place()
    copies = []
    for a, (gr, land) in enumerate(zip(grads, lands)):
        for mask in range(1, N_DEV):
            px, py, pc = _peer(x, y, c, mask)
            copies.append(_remote(gr.at[pc, 2 * px + py], land.at[mask - 1], ssem.at[7 * a + mask - 1],
                                  rsem.at[7 * a + mask - 1], (px, py, pc)))
    return copies


def _half_copies(totals, ssem, rsem):
    x, y, c = _place()
    return [_remote(t.at[c], t.at[c], ssem.at[a], rsem.at[a], (x, y, 1 - c)) for a, t in enumerate(totals)]


def _gather_start(name, groups):
    flat = [s for grp in groups for s in grp]
    n, ng = len(flat), len(groups)

    def body(*refs):
        ins = refs[:n]
        sems = refs[n:n + 2 * ng]
        token = refs[-1]
        idx = 0
        for gi, grp in enumerate(groups):
            for cp in _gather_copies(ins[idx:idx + len(grp)], sems[2 * gi], sems[2 * gi + 1]):
                cp.start()
            idx += len(grp)
        token[...] = jnp.zeros_like(token)

    tok_shape, tok_spec = _token_spec()
    sem_shapes = []
    for grp in groups:
        sem_shapes += [pltpu.SemaphoreType.DMA((3 * len(grp),))] * 2
    res = pl.pallas_call(
        body, name=name,
        out_shape=(*sem_shapes, *[pltpu.HBM(s.shape, s.dtype) for s in flat], tok_shape),
        in_specs=[HBM] * n, out_specs=(*[SEM] * (2 * ng), *[HBM] * n, tok_spec),
        input_output_aliases={i: 2 * ng + i for i in range(n)},
        compiler_params=pltpu.CompilerParams(has_side_effects=EFFECT))(*[_in_hbm(s) for s in flat])
    out, idx = [], 2 * ng
    for gi, grp in enumerate(groups):
        out.append((res[2 * gi], res[2 * gi + 1], list(res[idx:idx + len(grp)])))
        idx += len(grp)
    return out, res[-1]


def _gather_wait(name, ssem, rsem, slabs, after):
    n = len(slabs)

    def body(*refs):
        for cp in _gather_copies(refs[:n], refs[n], refs[n + 1]):
            cp.wait_send()
            cp.wait_recv()

    return pl.pallas_call(
        body, name=name, out_shape=tuple(pltpu.HBM(s.shape, s.dtype) for s in slabs),
        in_specs=[HBM] * n + [SEM, SEM] + [ANY] * len(after), out_specs=tuple([HBM] * n),
        input_output_aliases={i: i for i in range(n)},
        compiler_params=pltpu.CompilerParams(has_side_effects=EFFECT))(*slabs, ssem, rsem, *after)


def _reduce_start(name, grads):
    n = len(grads)
    lands = [lax.empty((N_DEV - 1, *g.shape[2:]), g.dtype) for g in grads]

    def body(*refs):
        token = refs[-1]
        for cp in _reduce_copies(refs[:n], refs[n:2 * n], refs[2 * n], refs[2 * n + 1]):
            cp.start()
        token[...] = jnp.zeros_like(token)

    tok_shape, tok_spec = _token_spec()
    sems = [pltpu.SemaphoreType.DMA((7 * n,))] * 2
    res = pl.pallas_call(
        body, name=name,
        out_shape=(*sems, *[pltpu.HBM(g.shape, g.dtype) for g in grads], *[pltpu.HBM(l.shape, l.dtype) for l in lands],
                   tok_shape),
        in_specs=[HBM] * (2 * n), out_specs=(SEM, SEM, *[HBM] * (2 * n), tok_spec),
        input_output_aliases={i: 2 + i for i in range(2 * n)},
        compiler_params=pltpu.CompilerParams(has_side_effects=EFFECT))(*[_in_hbm(t) for t in (*grads, *lands)])
    return res[0], res[1], list(res[2:2 + n]), list(res[2 + n:2 + 2 * n]), res[-1]


def _reduce_wait(name, ssem, rsem, grads, lands, after):
    n = len(grads)

    def body(*refs):
        for cp in _reduce_copies(refs[:n], refs[n:2 * n], refs[2 * n], refs[2 * n + 1]):
            cp.wait_send()
            cp.wait_recv()

    res = pl.pallas_call(
        body, name=name, out_shape=tuple(pltpu.HBM(t.shape, t.dtype) for t in (*grads, *lands)),
        in_specs=[HBM] * (2 * n) + [SEM, SEM] + [ANY] * len(after), out_specs=tuple([HBM] * (2 * n)),
        input_output_aliases={i: i for i in range(2 * n)},
        compiler_params=pltpu.CompilerParams(has_side_effects=EFFECT))(*grads, *lands, ssem, rsem, *after)
    return list(res[:n]), list(res[n:])


def _sum_pieces(name, grad, land, place):
    _, _, rows, cols = grad.shape
    tr = _pick_rows(rows, 256)

    def body(p_ref, g_ref, l_ref, o_ref):
        tot = g_ref[...].astype(F32)
        for k in range(N_DEV - 1):
            tot = tot + l_ref[k].astype(F32)
        o_ref[...] = tot

    grid_spec = pltpu.PrefetchScalarGridSpec(
        num_scalar_prefetch=1, grid=(rows // tr,),
        in_specs=[pl.BlockSpec((None, None, tr, cols), lambda i, p: (p[0], p[1], i, 0)),
                  pl.BlockSpec((N_DEV - 1, tr, cols), lambda i, p: (0, i, 0))],
        out_specs=pl.BlockSpec((None, tr, cols), lambda i, p: (p[0], i, 0)))
    return pl.pallas_call(body, out_shape=jax.ShapeDtypeStruct((2, rows, cols), F32), grid_spec=grid_spec, name=name,
                          compiler_params=_params(("parallel",)))(place, grad, land)


def _half_start(name, totals):
    n = len(totals)

    def body(*refs):
        token = refs[-1]
        for cp in _half_copies(refs[:n], refs[n], refs[n + 1]):
            cp.start()
        token[...] = jnp.zeros_like(token)

    tok_shape, tok_spec = _token_spec()
    res = pl.pallas_call(
        body, name=name,
        out_shape=(pltpu.SemaphoreType.DMA((n,)), pltpu.SemaphoreType.DMA((n,)),
                   *[pltpu.HBM(t.shape, t.dtype) for t in totals], tok_shape),
        in_specs=[HBM] * n, out_specs=(SEM, SEM, *[HBM] * n, tok_spec),
        input_output_aliases={i: 2 + i for i in range(n)},
        compiler_params=pltpu.CompilerParams(has_side_effects=EFFECT))(*[_in_hbm(t) for t in totals])
    return res[0], res[1], list(res[2:2 + n]), res[-1]


def _half_wait(name, ssem, rsem, totals, after):
    n = len(totals)

    def body(*refs):
        for cp in _half_copies(refs[:n], refs[n], refs[n + 1]):
            cp.wait_send()
            cp.wait_recv()

    res = pl.pallas_call(
        body, name=name, out_shape=tuple(pltpu.HBM(t.shape, t.dtype) for t in totals),
        in_specs=[HBM] * n + [SEM, SEM] + [ANY] * len(after), out_specs=tuple([HBM] * n),
        input_output_aliases={i: i for i in range(n)},
        compiler_params=pltpu.CompilerParams(has_side_effects=EFFECT))(*totals, ssem, rsem, *after)
    return list(res)


def _small_copies(bufs, ssem, rsem):
    x, y, c = _place()
    mine = bufs[0].at[4 * x + 2 * y + c]
    return [_remote(mine, mine, ssem.at[mask - 1], rsem.at[mask - 1], _peer(x, y, c, mask)) for mask in range(1, N_DEV)]


def _small_start(name, slots):
    def body(s_ref, ssem, rsem, thru, token):
        for cp in _small_copies([s_ref], ssem, rsem):
            cp.start()
        token[...] = jnp.zeros_like(token)

    tok_shape, tok_spec = _token_spec()
    sems = [pltpu.SemaphoreType.DMA((N_DEV - 1,))] * 2
    return pl.pallas_call(
        body, name=name, out_shape=(*sems, pltpu.HBM(slots.shape, slots.dtype), tok_shape), in_specs=[HBM],
        out_specs=(SEM, SEM, HBM, tok_spec), input_output_aliases={0: 2},
        compiler_params=pltpu.CompilerParams(has_side_effects=EFFECT))(_in_hbm(slots))


def _small_wait(name, ssem, rsem, slots, after):
    def body(*refs):
        for cp in _small_copies([refs[0]], refs[1], refs[2]):
            cp.wait_send()
            cp.wait_recv()

    return pl.pallas_call(
        body, name=name, out_shape=pltpu.HBM(slots.shape, slots.dtype), in_specs=[HBM, SEM, SEM] + [ANY] * len(after),
        out_specs=HBM, input_output_aliases={0: 0},
        compiler_params=pltpu.CompilerParams(has_side_effects=EFFECT))(slots, ssem, rsem, *after)


def _own_slot(small, me):
    return lax.dynamic_update_slice(jnp.zeros((N_DEV, *small.shape), small.dtype), small[None], (me, 0, 0))


def _sum_devices(name, stacked):
    _, rows, lanes = stacked.shape
    tr = _pick_rows(rows, 512)

    def body(s_ref, o_ref):
        tot = s_ref[0]
        for k in range(1, N_DEV):
            tot = tot + s_ref[k]
        o_ref[...] = tot

    return pl.pallas_call(
        body, out_shape=jax.ShapeDtypeStruct((rows, lanes), F32), grid=(rows // tr,),
        in_specs=[pl.BlockSpec((N_DEV, tr, lanes), lambda i: (0, i, 0))], out_specs=pl.BlockSpec((tr, lanes), lambda i: (i, 0)),
        name=name, compiler_params=_params(("parallel",)))(stacked)


def _adamw(name, w, g, m, v, layer, prev=None):
    layers, rows, cols = w.shape
    tr = _pick_rows(rows, 256)
    c1 = 1.0 - ADAM_B1 ** ADAM_STEP
    c2 = 1.0 - ADAM_B2 ** ADAM_STEP

    def body(w_ref, g_ref, m_ref, v_ref, *rest):
        go_ref, d_ref, nm_ref, nv_ref = rest[-4:]
        gv = g_ref[...]
        nm = ADAM_B1 * m_ref[...] + (1.0 - ADAM_B1) * gv
        nv = ADAM_B2 * v_ref[...] + (1.0 - ADAM_B2) * (gv * gv)
        go_ref[...] = gv
        d_ref[...] = -ADAM_LR * ((nm / c1) / (jnp.sqrt(nv / c2) + ADAM_EPS) + ADAM_WD * w_ref[...])
        nm_ref[...] = nm
        nv_ref[...] = nv

    spec = pl.BlockSpec((None, tr, cols), lambda i: (layer, i, 0))
    prev = list(prev) if prev is not None else []
    return pl.pallas_call(
        body, out_shape=[jax.ShapeDtypeStruct((layers, rows, cols), F32)] * 4, grid=(rows // tr,),
        in_specs=[spec, pl.BlockSpec((tr, cols), lambda i: (i, 0)), spec, spec] + [ANY] * len(prev),
        out_specs=[spec] * 4, input_output_aliases={4 + i: i for i in range(len(prev))}, name=name,
        compiler_params=_params(("parallel",)))(w, g, m, v, *prev)


def _pack(vectors, pad_rows):
    flat = jnp.concatenate([t.reshape(-1) for t in vectors])
    rows = -(-flat.shape[0] // LANES)
    rows = -(-rows // pad_rows) * pad_rows
    return jnp.pad(flat, (0, rows * LANES - flat.shape[0])).reshape(rows, LANES)


def _unpack(packed, shapes):
    flat = packed.reshape(-1)
    out, off = [], 0
    for shp in shapes:
        size = math.prod(shp)
        out.append(flat[off:off + size].reshape(shp))
        off += size
    return out


def kernel(x, mem, positions, mix_norm, mem_norm, w_mem_kv, ffn_norm, w_gate, w_up, w_down, attn_w_in, attn_w_out, sgu_w_in, sgu_ln_g, sgu_ln_b, sgu_w_spatial, sgu_b_spatial, sgu_w_out, final_norm, loss_target, m_mix_norm, m_mem_norm, m_w_mem_kv, m_ffn_norm, m_w_gate, m_w_up, m_w_down, m_attn_w_in, m_attn_w_out, m_sgu_w_in, m_sgu_ln_g, m_sgu_ln_b, m_sgu_w_spatial, m_sgu_b_spatial, m_sgu_w_out, m_final_norm, v_mix_norm, v_mem_norm, v_w_mem_kv, v_ffn_norm, v_w_gate, v_w_up, v_w_down, v_attn_w_in, v_attn_w_out, v_sgu_w_in, v_sgu_ln_g, v_sgu_ln_b, v_sgu_w_spatial, v_sgu_b_spatial, v_sgu_w_out, v_final_norm):
    d_model = x.shape[2]
    x0, mem0, tgt = x[0], mem[0], loss_target[0]
    xi, yi, ci = _place()
    chip = 2 * xi + yi
    place = jnp.stack([ci, chip]).astype(jnp.int32)

    given_w = dict(mix_norm=mix_norm, mem_norm=mem_norm, w_mem_kv=w_mem_kv, ffn_norm=ffn_norm, w_gate=w_gate, w_up=w_up,
                   w_down=w_down, attn_w_in=attn_w_in, attn_w_out=attn_w_out, sgu_w_in=sgu_w_in, sgu_ln_g=sgu_ln_g,
                   sgu_ln_b=sgu_ln_b, sgu_w_spatial=sgu_w_spatial, sgu_b_spatial=sgu_b_spatial, sgu_w_out=sgu_w_out,
                   final_norm=final_norm)
    given_m = dict(mix_norm=m_mix_norm, mem_norm=m_mem_norm, w_mem_kv=m_w_mem_kv, ffn_norm=m_ffn_norm, w_gate=m_w_gate,
                   w_up=m_w_up, w_down=m_w_down, attn_w_in=m_attn_w_in, attn_w_out=m_attn_w_out, sgu_w_in=m_sgu_w_in,
                   sgu_ln_g=m_sgu_ln_g, sgu_ln_b=m_sgu_ln_b, sgu_w_spatial=m_sgu_w_spatial,
                   sgu_b_spatial=m_sgu_b_spatial, sgu_w_out=m_sgu_w_out, final_norm=m_final_norm)
    given_v = dict(mix_norm=v_mix_norm, mem_norm=v_mem_norm, w_mem_kv=v_w_mem_kv, ffn_norm=v_ffn_norm, w_gate=v_w_gate,
                   w_up=v_w_up, w_down=v_w_down, attn_w_in=v_attn_w_in, attn_w_out=v_attn_w_out, sgu_w_in=v_sgu_w_in,
                   sgu_ln_g=v_sgu_ln_g, sgu_ln_b=v_sgu_ln_b, sgu_w_spatial=v_sgu_w_spatial,
                   sgu_b_spatial=v_sgu_b_spatial, sgu_w_out=v_sgu_w_out, final_norm=v_final_norm)

    units = {"attn_w_in": ("attn_w_in", 0, "col"), "w_mem_kv0": ("w_mem_kv", 0, "row"), "attn_w_out": ("attn_w_out", 0, "col"),
             "w_gate0": ("w_gate", 0, "col"), "w_up0": ("w_up", 0, "col"), "w_down0": ("w_down", 0, "row"),
             "sgu_w_in": ("sgu_w_in", 0, "col"), "w_mem_kv1": ("w_mem_kv", 1, "row"), "sgu_w_out": ("sgu_w_out", 0, "row"),
             "w_gate1": ("w_gate", 1, "col"), "w_up1": ("w_up", 1, "col"), "w_down1": ("w_down", 1, "row")}
    gather_groups = [["attn_w_in"], ["w_mem_kv0", "attn_w_out"], ["w_gate0", "w_up0"],
                     ["w_down0", "sgu_w_in", "w_mem_kv1", "ln"], ["sgu_w_out", "w_gate1", "w_up1"], ["w_down1"]]

    first = _place_shard("place_attn_w_in", attn_w_in, 0, place, BF)
    in_flight, token = _gather_start("gather_start_0", [[first]])
    slabs = {u: _place_shard(f"place_{u}", given_w[arr], layer, place, BF, after=[token])
             for u, (arr, layer, _) in units.items() if u != "attn_w_in"}
    slabs["ln"] = _place_shard("place_ln", jnp.concatenate([sgu_ln_g, sgu_ln_b])[None], 0, place, F32, after=[token])
    rest, token = _gather_start("gather_start_1", [[slabs[u] for u in grp] for grp in gather_groups[1:]])
    in_flight += rest
    weights = {}

    def arrive(gi, after):
        ssem, rsem, arrs = in_flight[gi]
        for u, full in zip(gather_groups[gi], _gather_wait(f"gather_wait_{gi}", ssem, rsem, arrs, after)):
            weights[u] = full if u == "ln" else Weight(full, units[u][2])

    w_sp = sgu_w_spatial[0]
    b_t = sgu_b_spatial[0].T
    tables = _rope_tables(positions[0])

    def residual(acc, extra):
        return [extra[0] + acc[0]]

    def memory_kv(layer):
        mem_n = _rms_fwd(f"mem_norm_{layer}", mem0, mem_norm[layer:layer + 1])
        return mem_n, _mm_nn(f"mem_kv_{layer}", mem_n, weights[f"w_mem_kv{layer}"])[0]

    h0 = _rms_fwd("mix_norm_0", x0, mix_norm[0:1], after=[token])
    arrive(0, [h0])
    proj0 = _mm_nn("attn_in", h0, weights["attn_w_in"])[0]
    arrive(1, [proj0])
    qkv = _rope_fwd(proj0, tables)
    qs, ks, vs = qkv[0:3], qkv[3:6], qkv[6:9]
    outs, lses = [], []
    for g in range(len(DILATIONS)):
        o, l = _dil_fwd(g, qs[g], ks[g], vs[g])
        outs.append(o)
        lses.append(l)
    merged, lse = _attn_merge(outs, lses)
    mem_n0, kv0 = memory_kv(0)
    cat0 = _mem_fwd("mem_fwd_0", proj0, 9, kv0, merged, 1)
    x1, hf0 = _mm_nn("attn_out", cat0, weights["attn_w_out"], extras=[x0], epilogue=residual, norm_gain=ffn_norm[0:1])
    arrive(2, [x1])
    g0, u0, act0 = _gate_up("gate_up_0", hf0, weights["w_gate0"], weights["w_up0"])
    arrive(3, [act0])
    x2 = _mm_nn("down_0", act0, weights["w_down0"], extras=[x1], epilogue=residual)[0]

    ln_all = weights["ln"]
    ln_g = ln_all[:, 0, :].reshape(1, SGU_W)
    ln_b = ln_all[:, 1, :].reshape(1, SGU_W)
    h1 = _rms_fwd("mix_norm_1", x2, mix_norm[1:2])
    proj1 = _mm_nn("sgu_in", h1, weights["sgu_w_in"], out_dtypes=(BF,))[0]
    arrive(4, [proj1])
    sgu_out = _sgu_fwd(proj1, ln_g, ln_b, w_sp, b_t)
    mem_n1, kv1 = memory_kv(1)
    cat1 = _mem_fwd("mem_fwd_1", proj1, 6, kv1, sgu_out, 3)
    x3, hf1 = _mm_nn("sgu_out", cat1, weights["sgu_w_out"], extras=[x2], epilogue=residual, norm_gain=ffn_norm[1:2])
    g1, u1, act1 = _gate_up("gate_up_1", hf1, weights["w_gate1"], weights["w_up1"])
    arrive(5, [act1])
    x4 = _mm_nn("down_1", act1, weights["w_down1"], extras=[x3], epilogue=residual)[0]

    d4, g_final, loss_part = _final_loss(x4, tgt, final_norm.reshape(1, d_model))
    loss = lax.psum(loss_part[0, 0], ("x", "y", "c"))

    outputs = {}

    def start_reduce(tag, names, grads):
        ssem, rsem, grads, lands, tok = _reduce_start(f"reduce_start_{tag}", grads)
        return dict(tag=tag, names=names, ssem=ssem, rsem=rsem, grads=grads, lands=lands), tok

    def finish_reduce(st, after):
        grads, lands = _reduce_wait(f"reduce_wait_{st['tag']}", st["ssem"], st["rsem"], st["grads"], st["lands"], after)
        totals = [_sum_pieces(f"sum_{u}", g, l, place) for u, g, l in zip(st["names"], grads, lands)]
        ssem, rsem, totals, tok = _half_start(f"half_start_{st['tag']}", totals)
        return dict(tag=st["tag"], names=st["names"], ssem=ssem, rsem=rsem, totals=totals), tok

    def finish_update(st, after):
        totals = _half_wait(f"half_wait_{st['tag']}", st["ssem"], st["rsem"], st["totals"], after)
        for u, tot in zip(st["names"], totals):
            arr, layer, _ = units[u]
            w = given_w[arr]
            outputs[arr] = _adamw(f"adamw_{u}", w, tot.reshape(w.shape[1:]), given_m[arr], given_v[arr], layer,
                                  outputs.get(arr))

    def ffn_bwd(layer, d_out, xin, h, g, u, act):
        wd, wg, wu = weights[f"w_down{layer}"], weights[f"w_gate{layer}"], weights[f"w_up{layer}"]
        gr_down = _mm_tn(f"d_down_{layer}", act, d_out, wd)
        dg, du = _mm_nt(f"d_act_{layer}", [d_out], [wd], out_dtypes=(BF, BF), extras=[g, u], epilogue=_swiglu_bwd_epilogue,
                        col_chunk=EPILOGUE_CHUNK)
        gr_gate = _mm_tn(f"d_gate_{layer}", h, dg, wg)
        gr_up = _mm_tn(f"d_up_{layer}", h, du, wu)
        st, tok = start_reduce(f"ffn{layer}", [f"w_down{layer}", f"w_gate{layer}", f"w_up{layer}"], [gr_down, gr_gate, gr_up])
        dh = _mm_nt(f"d_ffn_h_{layer}", [dg, du], [wg, wu], out_dtypes=(BF,), after=[tok])[0]
        d_in, g_norm = _rms_bwd(f"ffn_norm_bwd_{layer}", xin, ffn_norm[layer:layer + 1], dh, d_out)
        return st, d_in, g_norm, dg

    def memory_bwd(layer, mem_n, dkv):
        dkv = dkv.astype(BF)
        wkv = weights[f"w_mem_kv{layer}"]
        gr = _mm_tn(f"d_mem_kv_{layer}", mem_n, dkv, wkv)
        d_mem_n = _mm_nt(f"d_mem_n_{layer}", [dkv], [wkv])[0]
        return gr, _rms_bwd(f"mem_norm_bwd_{layer}", mem0, mem_norm[layer:layer + 1], d_mem_n)[1]

    st_ffn1, d3, g_ffn1, _ = ffn_bwd(1, d4, x3, hf1, g1, u1, act1)
    gr_sgu_out = _mm_tn("d_sgu_out", cat1, d3, weights["sgu_w_out"])
    dcat1 = _mm_nt("d_cat_1", [d3], [weights["sgu_w_out"]], out_dtypes=(BF,))[0]
    st_ffn1, tok = finish_reduce(st_ffn1, [dcat1])
    dproj1, dkv1 = _mem_bwd("mem_bwd_1", proj1, 6, kv1, dcat1, 3, proj1.shape[1])
    gr_kv1, g_mem1 = memory_bwd(1, mem_n1, dkv1)
    dproj1, g_wsp, g_bsp_t, g_ln_g, g_ln_b = _sgu_bwd(proj1, dcat1, ln_g, ln_b, w_sp, b_t, dproj1)
    gr_sgu_in = _mm_tn("d_sgu_in", h1, dproj1, weights["sgu_w_in"], after=[tok])
    finish_update(st_ffn1, [gr_sgu_in])
    st_mix1, tok = start_reduce("mix1", ["sgu_w_out", "w_mem_kv1", "sgu_w_in"], [gr_sgu_out, gr_kv1, gr_sgu_in])
    dh1 = _mm_nt("d_h_1", [dproj1], [weights["sgu_w_in"]], out_dtypes=(BF,), after=[tok])[0]
    d2, g_mix1 = _rms_bwd("mix_norm_bwd_1", x2, mix_norm[1:2], dh1, d3)

    st_ffn0, d1, g_ffn0, dg0 = ffn_bwd(0, d2, x1, hf0, g0, u0, act0)
    dev = 4 * xi + 2 * yi + ci
    small_a = [g_mix1, g_mem1, jnp.concatenate([g_ffn0, g_ffn1]), g_wsp, g_bsp_t[:, :SGU_GROUPS].T, g_final, g_ln_g, g_ln_b]
    sa_ssem, sa_rsem, sa_slots, tok = _small_start("small_start_a", _own_slot(_pack(small_a, LANES), dev))
    gr_attn_out = _mm_tn("d_attn_out", cat0, d1, weights["attn_w_out"], after=[tok])
    st_mix1, tok = finish_reduce(st_mix1, [gr_attn_out])
    dcat0 = _mm_nt("d_cat_0", [d1], [weights["attn_w_out"]], after=[tok])[0]
    dproj0, dkv0 = _mem_bwd("mem_bwd_0", proj0, 9, kv0, dcat0, 1, proj0.shape[1])
    finish_update(st_mix1, [dkv0])
    gr_kv0, g_mem0 = memory_bwd(0, mem_n0, dkv0)
    st_ffn0, tok = finish_reduce(st_ffn0, [g_mem0])
    d_merged, delta = _attn_delta(dcat0, cat0, after=[tok])
    dqs, dks, dvs = [], [], []
    for g in range(len(DILATIONS)):
        dq, dk, dv = _dil_bwd(g, qs[g], ks[g], vs[g], d_merged[g], lse[g], delta[g])
        dqs.append(dq)
        dks.append(dk)
        dvs.append(dv)
    dproj0 = _rope_bwd(dqs + dks + dvs, tables, dproj0)
    finish_update(st_ffn0, [dproj0])
    gr_attn_in = _mm_tn("d_attn_in", h0, dproj0, weights["attn_w_in"])
    st_mix0, tok = start_reduce("mix0", ["attn_w_out", "w_mem_kv0", "attn_w_in"], [gr_attn_out, gr_kv0, gr_attn_in])
    dh0 = _mm_nt("d_h_0", [dproj0], [weights["attn_w_in"]], out_dtypes=(BF,), after=[tok])[0]
    d0, g_mix0 = _rms_bwd("mix_norm_bwd_0", x0, mix_norm[0:1], dh0, d1)

    small_b = [g_mix0, g_mem0]
    sb_ssem, sb_rsem, sb_slots, tok = _small_start("small_start_b", _own_slot(_pack(small_b, 8), dev))
    sa_slots = _small_wait("small_wait_a", sa_ssem, sa_rsem, sa_slots, [tok])
    g_mix1, g_mem1, g_ffn, g_wsp, g_bsp, g_final, g_ln_g, g_ln_b = _unpack(_sum_devices("small_sum_a", sa_slots),
                                                                           [t.shape for t in small_a])
    sb_slots = _small_wait("small_wait_b", sb_ssem, sb_rsem, sb_slots, [g_final])
    g_mix0, g_mem0 = _unpack(_sum_devices("small_sum_b", sb_slots), [t.shape for t in small_b])
    st_mix0, tok = finish_reduce(st_mix0, [g_mix0])
    g_mix, g_mem = jnp.concatenate([g_mix0, g_mix1]), jnp.concatenate([g_mem0, g_mem1])
    shard_w = sgu_ln_g.shape[-1]
    g_ln_g = lax.dynamic_slice_in_dim(g_ln_g, chip * shard_w, shard_w, axis=1)
    g_ln_b = lax.dynamic_slice_in_dim(g_ln_b, chip * shard_w, shard_w, axis=1)
    small_names = ["mix_norm", "mem_norm", "ffn_norm", "sgu_w_spatial", "sgu_b_spatial", "final_norm", "sgu_ln_g",
                   "sgu_ln_b"]
    small_g = [g_mix, g_mem, g_ffn, g_wsp, g_bsp, g_final, g_ln_g, g_ln_b]
    small_shapes = [given_w[k].shape for k in small_names]
    packed = [_pack(t, LANES) for t in ([given_w[k] for k in small_names], small_g, [given_m[k] for k in small_names],
                                    [given_v[k] for k in small_names])]
    small_out = _adamw("adamw_small", packed[0][None], packed[1], packed[2][None], packed[3][None], 0)
    finish_update(st_mix0, [small_out[0]])
    for k, gk, dk, mk, vk in zip(small_names, *[_unpack(t[0], small_shapes) for t in small_out]):
        outputs[k] = (gk, dk, mk, vk)

    order = ["mix_norm", "mem_norm", "w_mem_kv", "ffn_norm", "w_gate", "w_up", "w_down", "attn_w_in", "attn_w_out",
             "sgu_w_in", "sgu_ln_g", "sgu_ln_b", "sgu_w_spatial", "sgu_b_spatial", "sgu_w_out", "final_norm"]
    return (loss, d0[None], *[outputs[k][0] for k in order], *[outputs[k][1] for k in order],
            *[outputs[k][2] for k in order], *[outputs[k][3] for k in order])
```

```python
import math

import jax
import jax.numpy as jnp
from jax import lax
from jax.experimental import pallas as pl
from jax.experimental.pallas import tpu as pltpu

F32 = jnp.float32
BF = jnp.bfloat16
MESH = pl.DeviceIdType.MESH

HEAD_DIM = 128
MEM_HEADS = 4
MEM_W = MEM_HEADS * HEAD_DIM
GROUP_W = 4 * HEAD_DIM
DILATIONS = (1, 4, 16)
BLK = 128
SGU_GROUPS = 12
SGU_W = SGU_GROUPS * HEAD_DIM
ROT_HALF = 16
ROPE_THETA = 500000.0
NORM_EPS = 1e-6
LN_EPS = 1e-5
NEG_INF = -1e30
SCALE = HEAD_DIM ** -0.5
ADAM_LR, ADAM_B1, ADAM_B2, ADAM_EPS, ADAM_WD, ADAM_STEP = 0.001, 0.9, 0.999, 1e-08, 0.01, 10

VMEM_LIMIT = 48 * 2 ** 20
VMEM_TILE_BUDGET = 38 * 2 ** 20
N_CHIPS = 4
N_DEV = 8
LANES = 128
EPILOGUE_CHUNK = 256

NT_DIMS = (((1,), (1,)), ((), ()))
TN_DIMS = (((0,), (0,)), ((), ()))
NN_DIMS = (((1,), (0,)), ((), ()))

ANY = pl.BlockSpec(memory_space=pl.ANY)
HBM = pl.BlockSpec(memory_space=pltpu.HBM)
SEM = pl.BlockSpec(memory_space=pltpu.SEMAPHORE)
EFFECT = pltpu.SideEffectType.DATAFLOW_SIDE_EFFECTING


def _params(sem):
    return pltpu.CompilerParams(dimension_semantics=sem, vmem_limit_bytes=VMEM_LIMIT)


def _pick(n, cap):
    if n <= cap:
        return n
    best = None
    for t in range(LANES, cap + 1, LANES):
        if n % t == 0:
            best = t
    assert best is not None, (n, cap)
    return best


def _pick_rows(n, cap):
    t = min(n, cap)
    while n % t:
        t //= 2
    return t


def _mm(name, dims, a_list, a_specs, b_list, b_specs, pairs, n_acc, acc_shape, grid, extras, e_specs,
        out_shapes, out_specs, epilogue, after=(), col_chunk=None, store=None, shard_width=None, norm_gain=None):
    na, nb, ne, no = len(a_list), len(b_list), len(extras), len(out_shapes)
    nk = grid[-1]
    ng = 0 if norm_gain is None else 1

    def products(a, b, cols=None):
        sums = [None] * n_acc
        for ai, bi, ci in pairs:
            bv = b[bi]
            if cols is None:
                bv = bv[...]
            elif dims == NT_DIMS:
                bv = bv[cols, :]
            else:
                bv = bv[:, cols]
            if bv.ndim == 3:
                bv = bv.reshape(-1, bv.shape[-1])
            prod = lax.dot_general(a[ai][...].astype(BF), bv.astype(BF), dims, preferred_element_type=F32)
            sums[ci] = prod if sums[ci] is None else sums[ci] + prod
        return sums

    def body(*refs):
        a = refs[:na]
        b = refs[na:na + nb]
        e = refs[na + nb:na + nb + ne]
        off = na + nb + ne + ng + len(after)
        o = refs[off:off + no]
        acc = refs[off + no:]

        def normed():
            if ng:
                xf = o[0][...]
                r = lax.rsqrt(jnp.mean(xf * xf, axis=-1, keepdims=True) + NORM_EPS)
                o[-1][...] = (xf * r * refs[na + nb + ne][...]).astype(o[-1].dtype)

        def finish(sums):
            outs = epilogue(sums, [r[...] for r in e])
            if store is not None:
                store(o, outs)
                return
            for r, v in zip(o, outs):
                r[...] = v.astype(r.dtype)
            normed()

        if nk == 1 and shard_width:
            (ai, bi, _), = pairs
            av = a[ai][...].astype(BF)
            if dims == NT_DIMS:
                total = None
                for j in range(N_CHIPS):
                    cols = slice(j * shard_width, (j + 1) * shard_width)
                    prod = lax.dot_general(av[:, cols], b[bi][j].astype(BF), dims, preferred_element_type=F32)
                    total = prod if total is None else total + prod
                finish([total])
                return
            for j in range(N_CHIPS):
                cols = slice(j * shard_width, (j + 1) * shard_width)
                prod = lax.dot_general(av, b[bi][j].astype(BF), dims, preferred_element_type=F32)
                outs = epilogue([prod], [r[:, cols] for r in e])
                for r, v in zip(o, outs):
                    r[:, cols] = v.astype(r.dtype)
            normed()
            return
        if nk == 1 and col_chunk:
            width = acc_shape[1]
            left = [r[...].astype(BF) for r in a]
            for c0 in range(0, width, col_chunk):
                cols = slice(c0, min(c0 + col_chunk, width))
                outs = epilogue(products(left, b, cols), [r[:, cols] for r in e])
                for r, v in zip(o, outs):
                    r[:, cols] = v.astype(r.dtype)
            return
        if nk == 1:
            finish(products(a, b))
            return
        k = pl.program_id(len(grid) - 1)

        @pl.when(k == 0)
        def _():
            for c, v in zip(acc, products(a, b)):
                c[...] = v

        @pl.when(jnp.logical_and(k > 0, k < nk - 1))
        def _():
            for c, v in zip(acc, products(a, b)):
                c[...] += v

        @pl.when(k == nk - 1)
        def _():
            finish([c[...] + v for c, v in zip(acc, products(a, b))])

    gains = [] if norm_gain is None else [norm_gain]
    ins = [*a_list, *b_list, *extras, *gains, *after]
    in_specs = [*a_specs, *b_specs, *e_specs, *[pl.BlockSpec(g.shape, lambda *_: (0, 0)) for g in gains],
                *([ANY] * len(after))]
    sem = ("parallel",) * (len(grid) - 1) + ("arbitrary",)
    scratch = [] if nk == 1 else [pltpu.VMEM(acc_shape, F32)] * n_acc
    return pl.pallas_call(
        body, out_shape=out_shapes, grid=grid, in_specs=in_specs, out_specs=out_specs, scratch_shapes=scratch,
        name=name, compiler_params=_params(sem))(*ins)


def _tile_bytes(blocks, single=()):
    size = lambda s, d: math.prod(s) * jnp.dtype(d).itemsize
    return sum(2 * size(s, d) for s, d in blocks) + sum(size(s, d) for s, d in single)


def _first(acc, extra):
    return [acc[0]]


def _sigmoid(x):
    return 0.5 * (1.0 + jnp.tanh(0.5 * x))


class Weight:
    def __init__(self, arr, axis):
        self.arr, self.axis = arr, axis
        _, self.rows, self.cols = arr.shape


SMALL_WEIGHT_BYTES = 8 * 2 ** 20


def _is_small(w):
    return w.arr.size * w.arr.dtype.itemsize <= SMALL_WEIGHT_BYTES


def _mm_nn(name, a, w, extras=(), epilogue=_first, out_dtypes=(F32,), after=(), norm_gain=None):
    m, kdim = a.shape
    b_spec, shard_width = None, None
    weight_buffers = 2
    if norm_gain is not None:
        out_dtypes = (*out_dtypes, BF)
    if w.axis == "col" and _is_small(w):
        n_total = tn = N_CHIPS * w.cols
        tk, gn, gk = kdim, 1, 1
        shard_width = w.cols
        b_spec = pl.BlockSpec((N_CHIPS, kdim, w.cols), lambda n, i, k: (0, 0, 0))
    elif w.axis == "col":
        n_total = N_CHIPS * w.cols
        tn = _pick(w.cols, 1408)
        tk = _pick(kdim, 2048)
        ncb = w.cols // tn
        gn, gk = N_CHIPS * ncb, kdim // tk
        b_map = lambda n, i, k: (n // ncb, k, n % ncb)
    elif kdim <= 2048 and norm_gain is not None:
        n_total = tn = w.cols
        tk, gn, gk = kdim, 1, 1
        weight_buffers = 1
        b_spec = pl.BlockSpec(w.arr.shape, lambda n, i, k: (0, 0, 0), pipeline_mode=pl.Buffered(1))
    elif kdim <= 2048:
        n_total = w.cols
        tn = _pick(w.cols, 1024)
        tk = kdim
        gn, gk = n_total // tn, 1
        b_spec = pl.BlockSpec((N_CHIPS, w.rows, tn), lambda n, i, k: (0, 0, n))
    else:
        n_total = w.cols
        tn = _pick(w.cols, 1024)
        tk = _pick(w.rows, 1408)
        nkb = w.rows // tk
        gn, gk = n_total // tn, N_CHIPS * nkb
        b_map = lambda n, i, k: (k // nkb, k % nkb, n)
    if b_spec is None:
        b_spec = pl.BlockSpec((None, tk, tn), b_map)
    for tm in (1024, 512, 256, 128):
        if m % tm:
            continue
        blocks = [((tm, tk), a.dtype)] + [((tm, tn), e.dtype) for e in extras]
        blocks += [((tm, tn), d) for d in out_dtypes] + [((tm, tn), BF)]
        weight = [((tk, tn), BF)]
        if _tile_bytes(blocks + (weight if weight_buffers == 2 else []), weight if weight_buffers == 1 else ()) <= VMEM_TILE_BUDGET:
            break
    assert norm_gain is None or tn == n_total, name
    o_spec = pl.BlockSpec((tm, tn), lambda n, i, k: (i, n))
    return _mm(
        name, NN_DIMS, [a], [pl.BlockSpec((tm, tk), lambda n, i, k: (i, k))],
        [w.arr], [b_spec], [(0, 0, 0)], 1, (tm, tn), (gn, m // tm, gk),
        list(extras), [o_spec] * len(extras),
        [jax.ShapeDtypeStruct((m, n_total), d) for d in out_dtypes], [o_spec] * len(out_dtypes), epilogue, after,
        shard_width=shard_width, norm_gain=norm_gain)


def _gate_up(name, h, wg, wu):
    m, kdim = h.shape
    tn = _pick(wg.cols, 1408)
    tk = _pick(kdim, 2048)
    ncb = wg.cols // tn
    single = kdim == tk
    for tm in (1024, 512, 256, 128):
        blocks = [((tm, tk), BF)] + [((tm, tn), BF)] * 3
        weights = [((tk, tn), BF)] * 2
        if m % tm == 0 and _tile_bytes(blocks + ([] if single else weights), weights if single else ()) <= VMEM_TILE_BUDGET:
            break
    b_spec = pl.BlockSpec((None, tk, tn), lambda n, i, k: (n // ncb, k, n % ncb),
                          pipeline_mode=pl.Buffered(1) if single else None)
    o_spec = pl.BlockSpec((tm, tn), lambda n, i, k: (i, n))
    n_total = N_CHIPS * wg.cols

    def epilogue(acc, extra):
        g, u = acc
        return [g, u, g * _sigmoid(g) * u]

    return _mm(
        name, NN_DIMS, [h], [pl.BlockSpec((tm, tk), lambda n, i, k: (i, k))], [wg.arr, wu.arr], [b_spec, b_spec],
        [(0, 0, 0), (0, 1, 1)], 2, (tm, tn), (N_CHIPS * ncb, m // tm, kdim // tk), [], [],
        [jax.ShapeDtypeStruct((m, n_total), BF)] * 3, [o_spec] * 3, epilogue, col_chunk=EPILOGUE_CHUNK)


def _mm_nt(name, dys, ws, out_dtypes=(F32,), extras=(), epilogue=_first, after=(), col_chunk=None):
    m = dys[0].shape[0]
    w0 = ws[0]
    npair = len(dys)
    b_spec, shard_width = None, None
    if w0.axis == "col" and npair == 1 and _is_small(w0):
        k_total = tko = w0.rows
        tkc = N_CHIPS * w0.cols
        go, gk = 1, 1
        shard_width = w0.cols
        b_spec = pl.BlockSpec(w0.arr.shape, lambda o, i, k: (0, 0, 0))
    elif w0.axis == "col":
        k_total = w0.rows
        tko = _pick(k_total, 1024)
        tkc = _pick(w0.cols, 1408)
        nkb = w0.cols // tkc
        go, gk = k_total // tko, N_CHIPS * nkb
        b_map = lambda o, i, k: (k // nkb, o, k % nkb)
    else:
        k_total = N_CHIPS * w0.rows
        tko = _pick(w0.rows, 1408)
        tkc = _pick(w0.cols, 2048)
        nob = w0.rows // tko
        go, gk = N_CHIPS * nob, w0.cols // tkc
        b_map = lambda o, i, k: (o // nob, o % nob, k)
    single = gk == 1
    for tm in (1024, 512, 256, 128):
        if m % tm:
            continue
        blocks = [((tm, tkc), d.dtype) for d in dys]
        blocks += [((tm, tko), e.dtype) for e in extras] + [((tm, tko), d) for d in out_dtypes]
        blocks += [((tm, tko), BF)]
        weights = [((tko, tkc), BF)] * npair
        if _tile_bytes(blocks + ([] if single else weights), weights if single else ()) <= VMEM_TILE_BUDGET:
            break
    if b_spec is None:
        b_spec = pl.BlockSpec((None, tko, tkc), b_map, pipeline_mode=pl.Buffered(1) if single else None)
    o_spec = pl.BlockSpec((tm, tko), lambda o, i, k: (i, o))
    return _mm(
        name, NT_DIMS, list(dys), [pl.BlockSpec((tm, tkc), lambda o, i, k: (i, k))] * npair,
        [w.arr for w in ws], [b_spec] * npair,
        [(i, i, 0) for i in range(npair)], 1, (tm, tko), (go, m // tm, gk), list(extras), [o_spec] * len(extras),
        [jax.ShapeDtypeStruct((m, k_total), d) for d in out_dtypes], [o_spec] * len(out_dtypes), epilogue, after,
        col_chunk if gk == 1 and shard_width is None else None, shard_width=shard_width)


def _mm_tn(name, a, dy, w, after=()):
    m, k_total = a.shape
    rows2 = w.rows // 2
    tn = _pick(w.cols, 1408)
    ncb = w.cols // tn
    epilogue, store = _first, None
    if k_total <= 2048 and w.axis == "col" and _is_small(w):
        tkr, tn = k_total, N_CHIPS * w.cols
        gr, gn = 1, 1
        o_spec = pl.BlockSpec((2, N_CHIPS, rows2, w.cols), lambda r, n, t: (0, 0, 0, 0))

        def store(o_refs, outs):
            for j in range(N_CHIPS):
                for h in range(2):
                    o_refs[0][h, j] = outs[0][h * rows2:(h + 1) * rows2, j * w.cols:(j + 1) * w.cols].astype(BF)
    elif k_total <= 2048 and w.axis == "col":
        tkr = k_total
        gr, gn = 1, N_CHIPS * ncb
        o_spec = pl.BlockSpec((2, None, rows2, tn), lambda r, n, t: (0, n // ncb, 0, n % ncb))
        epilogue = lambda acc, extra: [acc[0].reshape(2, rows2, tn)]
    elif k_total <= 2048:
        tkr = k_total
        gr, gn = 1, ncb
        o_spec = pl.BlockSpec((2, N_CHIPS, rows2, tn), lambda r, n, t: (0, 0, 0, n))

        def store(o_refs, outs):
            for j in range(N_CHIPS):
                for h in range(2):
                    lo = (2 * j + h) * rows2
                    o_refs[0][h, j] = outs[0][lo:lo + rows2].astype(BF)
    elif rows2 % LANES:
        tkr = w.rows
        assert w.axis == "row"
        gr, gn = N_CHIPS, ncb
        o_spec = pl.BlockSpec((2, None, rows2, tn), lambda r, n, t: (0, r, 0, n))
        epilogue = lambda acc, extra: [acc[0].reshape(2, rows2, tn)]
    else:
        tkr = _pick(rows2, 1408)
        nrb = rows2 // tkr
        if w.axis == "col":
            gr, gn = w.rows // tkr, N_CHIPS * ncb
            o_map = lambda r, n, t: (r // nrb, n // ncb, r % nrb, n % ncb)
        else:
            per = w.rows // tkr
            gr, gn = N_CHIPS * per, ncb
            o_map = lambda r, n, t: ((r % per) // nrb, r // per, (r % per) % nrb, n)
        o_spec = pl.BlockSpec((None, None, tkr, tn), o_map)
    for tmk in (1024, 512, 256, 128):
        blocks = [((tmk, tkr), a.dtype), ((tmk, tn), dy.dtype), ((tkr, tn), BF), ((tkr, tn), BF)]
        if m % tmk == 0 and _tile_bytes(blocks) <= VMEM_TILE_BUDGET:
            break
    return _mm(
        name, TN_DIMS, [a], [pl.BlockSpec((tmk, tkr), lambda r, n, t: (t, r))],
        [dy], [pl.BlockSpec((tmk, tn), lambda r, n, t: (t, n))], [(0, 0, 0)], 1, (tkr, tn), (gr, gn, m // tmk), [], [],
        [jax.ShapeDtypeStruct((2, N_CHIPS, rows2, w.cols), BF)], [o_spec], epilogue, after, store=store)[0]


def _rms_fwd(name, x, g, after=()):
    s, d = x.shape
    tr = _pick_rows(s, 512)

    def body(x_ref, g_ref, *rest):
        h_ref = rest[-1]
        xf = x_ref[...]
        r = lax.rsqrt(jnp.mean(xf * xf, axis=-1, keepdims=True) + NORM_EPS)
        h_ref[...] = (xf * r * g_ref[...]).astype(BF)

    return pl.pallas_call(
        body, out_shape=jax.ShapeDtypeStruct((s, d), BF), grid=(s // tr,),
        in_specs=[pl.BlockSpec((tr, d), lambda i: (i, 0)), pl.BlockSpec((1, d), lambda i: (0, 0))] + [ANY] * len(after),
        out_specs=pl.BlockSpec((tr, d), lambda i: (i, 0)), name=name, compiler_params=_params(("parallel",)))(x, g, *after)


def _rms_bwd(name, x, g, dh, dres=None):
    s, d = x.shape
    tr = _pick_rows(s, 256)
    has_res = dres is not None

    def body(*refs):
        if has_res:
            x_ref, g_ref, dh_ref, dres_ref, dx_ref, dxb_ref, dg_ref = refs
        else:
            x_ref, g_ref, dh_ref, dx_ref, dxb_ref, dg_ref = refs
        xf = x_ref[...]
        r = lax.rsqrt(jnp.mean(xf * xf, axis=-1, keepdims=True) + NORM_EPS)
        xr = xf * r
        dy = dh_ref[...].astype(F32)
        a = dy * g_ref[...]
        dx = r * (a - xr * jnp.mean(a * xr, axis=-1, keepdims=True))
        if has_res:
            dx = dx + dres_ref[...]
        dx_ref[...] = dx
        dxb_ref[...] = dx.astype(BF)

        @pl.when(pl.program_id(0) == 0)
        def _():
            dg_ref[...] = jnp.zeros_like(dg_ref)

        dg_ref[...] += jnp.sum(dy * xr, axis=0, keepdims=True)

    row = pl.BlockSpec((tr, d), lambda i: (i, 0))
    vec = pl.BlockSpec((1, d), lambda i: (0, 0))
    ins = [x, g, dh] + ([dres] if has_res else [])
    in_specs = [row, vec, row] + ([row] if has_res else [])
    return pl.pallas_call(
        body, out_shape=[jax.ShapeDtypeStruct((s, d), F32), jax.ShapeDtypeStruct((s, d), BF),
                         jax.ShapeDtypeStruct((1, d), F32)],
        grid=(s // tr,), in_specs=in_specs, out_specs=[row, row, vec], name=name,
        compiler_params=_params(("arbitrary",)))(*ins)


def _final_loss(x, tgt, g):
    s, d = x.shape
    tr = _pick_rows(s, 256)

    def body(x_ref, t_ref, g_ref, dx_ref, dxb_ref, dg_ref, loss_ref):
        xf = x_ref[...]
        gain = g_ref[...]
        r = lax.rsqrt(jnp.mean(xf * xf, axis=-1, keepdims=True) + NORM_EPS)
        xr = xf * r
        err = xr * gain - t_ref[...]
        dy = err * (1.0 / d)
        a = dy * gain
        dx = r * (a - xr * jnp.mean(a * xr, axis=-1, keepdims=True))
        dx_ref[...] = dx
        dxb_ref[...] = dx.astype(BF)

        @pl.when(pl.program_id(0) == 0)
        def _():
            dg_ref[...] = jnp.zeros_like(dg_ref)
            loss_ref[...] = jnp.zeros_like(loss_ref)

        dg_ref[...] += jnp.sum(dy * xr, axis=0, keepdims=True)
        part = 0.5 * jnp.sum(jnp.mean(err * err, axis=-1, keepdims=True), axis=0, keepdims=True)
        loss_ref[...] += jnp.broadcast_to(part, loss_ref.shape)

    row = pl.BlockSpec((tr, d), lambda i: (i, 0))
    vec = pl.BlockSpec((1, d), lambda i: (0, 0))
    return pl.pallas_call(
        body, out_shape=[jax.ShapeDtypeStruct((s, d), F32), jax.ShapeDtypeStruct((s, d), BF),
                         jax.ShapeDtypeStruct((1, d), F32), jax.ShapeDtypeStruct((8, LANES), F32)],
        grid=(s // tr,), in_specs=[row, row, vec],
        out_specs=[row, row, vec, pl.BlockSpec((8, LANES), lambda i: (0, 0))],
        name="final_loss", compiler_params=_params(("arbitrary",)))(x, tgt, g)


def _swiglu_bwd_epilogue(acc, extra):
    dact = acc[0]
    g, u = extra[0].astype(F32), extra[1].astype(F32)
    sig = _sigmoid(g)
    return [dact * u * sig * (1.0 + g * (1.0 - sig)), dact * g * sig]


GELU_C = math.sqrt(2.0 / math.pi)
GELU_A = 0.044715


def _gelu(x):
    return 0.5 * x * (1.0 + jnp.tanh(GELU_C * (x + GELU_A * x * x * x)))


def _gelu_both(x):
    x2 = x * x
    t = jnp.tanh(GELU_C * (x + GELU_A * x2 * x))
    half = 0.5 * (1.0 + t)
    return x * half, half + 0.5 * x * (1.0 - t * t) * GELU_C * (1.0 + 3.0 * GELU_A * x2)


def _rope_tables(positions):
    inv_freq = ROPE_THETA ** (-jnp.arange(ROT_HALF, dtype=F32) / ROT_HALF)
    ang = positions.astype(F32)[:, None] * inv_freq
    cos, sin = jnp.cos(ang), jnp.sin(ang)
    s = ang.shape[0]
    rest = HEAD_DIM - 2 * ROT_HALF
    zeros = jnp.zeros((s, ROT_HALF), F32)
    cos_t = jnp.concatenate([cos, cos, jnp.ones((s, rest), F32)], axis=1)
    sin_a = jnp.concatenate([-sin, zeros, jnp.zeros((s, rest), F32)], axis=1)
    sin_b = jnp.concatenate([zeros, sin, jnp.zeros((s, rest), F32)], axis=1)
    return cos_t, sin_a, sin_b


def _rope_head(xh, cos_t, sin_a, sin_b):
    up = pltpu.roll(xh, HEAD_DIM - ROT_HALF, 1)
    down = pltpu.roll(xh, ROT_HALF, 1)
    return xh * cos_t + up * sin_a + down * sin_b


def _residue(r, rows, dil):
    return slice(None) if dil == 1 else pl.ds(r, rows, stride=dil)


ROPE_TILE = 256
N_PARTS = 9
HEADS_PER_GROUP = GROUP_W // HEAD_DIM
N_HEADS_IN = N_PARTS * HEADS_PER_GROUP


def _rope_fwd(proj, tables):
    s = proj.shape[0]
    tm = _pick_rows(s, ROPE_TILE)

    def body(*refs):
        heads = refs[:N_HEADS_IN]
        c_ref, sa_ref, sb_ref = refs[N_HEADS_IN:N_HEADS_IN + 3]
        outs = refs[N_HEADS_IN + 3:]
        for g, dil in enumerate(DILATIONS):
            rows = tm // dil
            for r in range(dil):
                rs = _residue(r, rows, dil)
                cos_t, sin_a, sin_b = c_ref[rs, :], sa_ref[rs, :], sb_ref[rs, :]
                for kind in range(3):
                    part = 3 * kind + g
                    for h in range(HEADS_PER_GROUP):
                        xh = heads[part * HEADS_PER_GROUP + h][rs, :]
                        if kind < 2:
                            xh = _rope_head(xh, cos_t, sin_a, sin_b)
                        outs[part][r, :, h * HEAD_DIM:(h + 1) * HEAD_DIM] = xh.astype(BF)

    tab = pl.BlockSpec((tm, HEAD_DIM), lambda i: (i, 0))
    head_specs = [pl.BlockSpec((tm, HEAD_DIM), lambda i, j=j: (i, j)) for j in range(N_HEADS_IN)]
    shapes, specs = [], []
    for part in range(N_PARTS):
        dil = DILATIONS[part % 3]
        shapes.append(jax.ShapeDtypeStruct((dil, s // dil, GROUP_W), BF))
        specs.append(pl.BlockSpec((dil, tm // dil, GROUP_W), lambda i: (0, i, 0)))
    return pl.pallas_call(
        body, out_shape=shapes, grid=(s // tm,), in_specs=head_specs + [tab, tab, tab], out_specs=specs,
        name="rope_fwd", compiler_params=_params(("parallel",)))(*([proj] * N_HEADS_IN), *tables)


def _rope_bwd(parts, tables, into):
    s = into.shape[0]
    tm = _pick_rows(s, ROPE_TILE)

    def body(*refs):
        ins = refs[:N_PARTS]
        c_ref, sa_ref, sb_ref, into_ref, o_ref, scr = refs[N_PARTS:]
        for g, dil in enumerate(DILATIONS):
            rows = tm // dil
            for r in range(dil):
                rs = _residue(r, rows, dil)
                cos_t, sin_a, sin_b = c_ref[rs, :], -sa_ref[rs, :], -sb_ref[rs, :]
                for kind in range(3):
                    part = 3 * kind + g
                    for h in range(HEADS_PER_GROUP):
                        xh = ins[part][r, :, h * HEAD_DIM:(h + 1) * HEAD_DIM]
                        if kind < 2:
                            xh = _rope_head(xh, cos_t, sin_a, sin_b)
                        scr[part * HEADS_PER_GROUP + h, rs, :] = xh
        for j in range(N_HEADS_IN):
            o_ref[:, j * HEAD_DIM:(j + 1) * HEAD_DIM] = scr[j].astype(BF)

    tab = pl.BlockSpec((tm, HEAD_DIM), lambda i: (i, 0))
    i_specs = [pl.BlockSpec((DILATIONS[p % 3], tm // DILATIONS[p % 3], GROUP_W), lambda i: (0, i, 0))
               for p in range(N_PARTS)]
    return pl.pallas_call(
        body, out_shape=jax.ShapeDtypeStruct(into.shape, into.dtype), grid=(s // tm,),
        in_specs=i_specs + [tab] * 3 + [ANY], out_specs=pl.BlockSpec((tm, N_PARTS * GROUP_W), lambda i: (i, 0)),
        scratch_shapes=[pltpu.VMEM((N_HEADS_IN, tm, HEAD_DIM), F32)], input_output_aliases={N_PARTS + 3: 0},
        name="rope_bwd", compiler_params=_params(("parallel",)))(*parts, *tables, into)


def _band_mask(n):
    qi = lax.broadcasted_iota(jnp.int32, (BLK, 2 * BLK), 0)
    ki = lax.broadcasted_iota(jnp.int32, (BLK, 2 * BLK), 1)
    prev = jnp.logical_and(jnp.logical_and(ki < BLK, ki >= qi), n > 0)
    return jnp.logical_or(prev, jnp.logical_and(ki >= BLK, qi >= ki - BLK))


Q_BLOCKS = 2
Q_ROWS = Q_BLOCKS * BLK


def _dil_specs(n_steps):
    last = n_steps - 1
    own = pl.BlockSpec((None, Q_ROWS, GROUP_W), lambda r, n: (r, jnp.minimum(n, last), 0))
    before = pl.BlockSpec((None, BLK, GROUP_W), lambda r, n: (r, jnp.maximum(Q_BLOCKS * n - 1, 0), 0))
    return own, before


def _dil_fwd(g, q, k, v):
    dil, length, _ = q.shape
    n_steps = length // Q_ROWS

    def body(q_ref, ko_ref, kb_ref, vo_ref, vb_ref, o_ref, lse_ref):
        n = pl.program_id(1)
        for h in range(GROUP_W // HEAD_DIM):
            sl = slice(h * HEAD_DIM, (h + 1) * HEAD_DIM)
            keys = jnp.concatenate([kb_ref[:, sl], ko_ref[:, sl]], axis=0)
            vals = jnp.concatenate([vb_ref[:, sl], vo_ref[:, sl]], axis=0)
            for j in range(Q_BLOCKS):
                rows, win = slice(j * BLK, (j + 1) * BLK), slice(j * BLK, (j + 2) * BLK)
                sc = lax.dot_general(q_ref[rows, sl], keys[win], NT_DIMS, preferred_element_type=F32) * SCALE
                sc = jnp.where(_band_mask(Q_BLOCKS * n + j), sc, NEG_INF)
                mx = jnp.max(sc, axis=-1, keepdims=True)
                p = jnp.exp(sc - mx)
                den = jnp.sum(p, axis=-1, keepdims=True)
                o_ref[rows, sl] = jnp.dot(p.astype(BF), vals[win], preferred_element_type=F32) / den
                lse_ref[rows, sl] = jnp.broadcast_to(mx + jnp.log(den), (BLK, HEAD_DIM))

    own, before = _dil_specs(n_steps)
    return pl.pallas_call(
        body, out_shape=[jax.ShapeDtypeStruct(q.shape, F32)] * 2, grid=(dil, n_steps),
        in_specs=[own, own, before, own, before], out_specs=[own, own], name=f"dil_fwd_{g}",
        compiler_params=_params(("parallel", "arbitrary")))(q, k, k, v, v)


def _dil_bwd(g, q, k, v, do, lse, delta):
    dil, length, _ = q.shape
    n_steps = length // Q_ROWS

    def body(q_ref, ko_ref, kb_ref, vo_ref, vb_ref, do_ref, lse_ref, dl_ref, dq_ref, dk_ref, dv_ref, ck_ref, cv_ref):
        n = pl.program_id(1)
        live = n < n_steps

        @pl.when(n == 0)
        def _():
            ck_ref[...] = jnp.zeros_like(ck_ref)
            cv_ref[...] = jnp.zeros_like(cv_ref)

        @pl.when(jnp.logical_not(live))
        def _():
            dk_ref[...] = ck_ref[...]
            dv_ref[...] = cv_ref[...]

        @pl.when(live)
        def _():
            for h in range(GROUP_W // HEAD_DIM):
                sl = slice(h * HEAD_DIM, (h + 1) * HEAD_DIM)
                keys = jnp.concatenate([kb_ref[:, sl], ko_ref[:, sl]], axis=0)
                vals = jnp.concatenate([vb_ref[:, sl], vo_ref[:, sl]], axis=0)
                dks, dvs = [], []
                for j in range(Q_BLOCKS):
                    rows, win = slice(j * BLK, (j + 1) * BLK), slice(j * BLK, (j + 2) * BLK)
                    qh, doh = q_ref[rows, sl], do_ref[rows, sl]
                    lse_h = lse_ref[rows, h * HEAD_DIM:h * HEAD_DIM + 1]
                    dl_h = dl_ref[rows, h * HEAD_DIM:h * HEAD_DIM + 1]
                    sc = lax.dot_general(qh, keys[win], NT_DIMS, preferred_element_type=F32) * SCALE
                    p = jnp.where(_band_mask(Q_BLOCKS * n + j), jnp.exp(jnp.minimum(sc - lse_h, 0.0)), 0.0)
                    dp = lax.dot_general(doh, vals[win], NT_DIMS, preferred_element_type=F32)
                    ds = (p * (dp - dl_h) * SCALE).astype(BF)
                    dq_ref[rows, sl] = jnp.dot(ds, keys[win], preferred_element_type=F32)
                    dks.append(lax.dot_general(ds, qh, TN_DIMS, preferred_element_type=F32))
                    dvs.append(lax.dot_general(p.astype(BF), doh, TN_DIMS, preferred_element_type=F32))
                for out_ref, carry, parts in ((dk_ref, ck_ref, dks), (dv_ref, cv_ref, dvs)):
                    out_ref[:Q_ROWS - BLK, sl] = carry[:Q_ROWS - BLK, sl]
                    out_ref[Q_ROWS - BLK:, sl] = carry[Q_ROWS - BLK:, sl] + parts[0][:BLK]
                    for j in range(Q_BLOCKS - 1):
                        carry[j * BLK:(j + 1) * BLK, sl] = parts[j][BLK:] + parts[j + 1][:BLK]
                    carry[Q_ROWS - BLK:, sl] = parts[-1][BLK:]

    own, before = _dil_specs(n_steps)
    behind = pl.BlockSpec((None, Q_ROWS, GROUP_W), lambda r, n: (r, jnp.maximum(n - 1, 0), 0))
    return pl.pallas_call(
        body, out_shape=[jax.ShapeDtypeStruct(q.shape, F32)] * 3, grid=(dil, n_steps + 1),
        in_specs=[own, own, before, own, before, own, own, own], out_specs=[own, behind, behind],
        scratch_shapes=[pltpu.VMEM((Q_ROWS, GROUP_W), F32)] * 2, name=f"dil_bwd_{g}",
        compiler_params=_params(("parallel", "arbitrary")))(q, k, k, v, v, do, lse, delta)


def _major_specs(s, tm, dtype):
    shapes = [jax.ShapeDtypeStruct((dil, s // dil, GROUP_W), dtype) for dil in DILATIONS]
    specs = [pl.BlockSpec((dil, tm // dil, GROUP_W), lambda i: (0, i, 0)) for dil in DILATIONS]
    return shapes, specs


def _attn_merge(outs, lses):
    s = outs[0].shape[1]
    tm = _pick_rows(s, ROPE_TILE)

    def body(o0, o1, o2, l0, l1, l2, m_ref, e0, e1, e2, so1, so2, sl1, sl2, se):
        for h in range(HEADS_PER_GROUP):
            sl = slice(h * HEAD_DIM, (h + 1) * HEAD_DIM)
            for dil, src, dst in ((DILATIONS[1], o1, so1), (DILATIONS[2], o2, so2), (DILATIONS[1], l1, sl1),
                                  (DILATIONS[2], l2, sl2)):
                for r in range(dil):
                    dst[h, _residue(r, tm // dil, dil), :] = src[r, :, sl]
            a, b, c = l0[0, :, sl], sl1[h], sl2[h]
            mx = jnp.maximum(jnp.maximum(a, b), c)
            ea, eb, ec = jnp.exp(a - mx), jnp.exp(b - mx), jnp.exp(c - mx)
            den = ea + eb + ec
            m_ref[:, sl] = ((ea * o0[0, :, sl] + eb * so1[h] + ec * so2[h]) / den).astype(BF)
            se[h] = mx + jnp.log(den)
            for dil, dst in zip(DILATIONS, (e0, e1, e2)):
                for r in range(dil):
                    dst[r, :, sl] = se[h, _residue(r, tm // dil, dil), :]

    shapes, specs = _major_specs(s, tm, F32)
    nat = pl.BlockSpec((tm, GROUP_W), lambda i: (i, 0))
    res = pl.pallas_call(
        body, out_shape=[jax.ShapeDtypeStruct((s, GROUP_W + MEM_W), BF)] + shapes, grid=(s // tm,), in_specs=specs * 2,
        out_specs=[nat] + specs, scratch_shapes=[pltpu.VMEM((HEADS_PER_GROUP, tm, HEAD_DIM), F32)] * 5,
        name="attn_merge", compiler_params=_params(("parallel",)))(*outs, *lses)
    return res[0], res[1:]


def _attn_delta(dcat, merged, after=()):
    s = merged.shape[0]
    tm = _pick_rows(s, ROPE_TILE)

    def body(*refs):
        d_refs, m_ref = refs[:HEADS_PER_GROUP], refs[HEADS_PER_GROUP]
        do_refs, dl_refs, scr = refs[-7:-4], refs[-4:-1], refs[-1]
        for h in range(HEADS_PER_GROUP):
            sl = slice(h * HEAD_DIM, (h + 1) * HEAD_DIM)
            prod = d_refs[h][...] * m_ref[:, sl].astype(F32)
            scr[h] = jnp.broadcast_to(jnp.sum(prod, axis=-1, keepdims=True), (tm, HEAD_DIM))
            for dil, do_ref, dl_ref in zip(DILATIONS, do_refs, dl_refs):
                for r in range(dil):
                    rs = _residue(r, tm // dil, dil)
                    do_ref[r, :, sl] = d_refs[h][rs, :].astype(BF)
                    dl_ref[r, :, sl] = scr[h, rs, :]

    nat = pl.BlockSpec((tm, GROUP_W), lambda i: (i, 0))
    head_specs = [pl.BlockSpec((tm, HEAD_DIM), lambda i, h=h: (i, h)) for h in range(HEADS_PER_GROUP)]
    bf_shapes, specs = _major_specs(s, tm, BF)
    f_shapes, _ = _major_specs(s, tm, F32)
    res = pl.pallas_call(
        body, out_shape=bf_shapes + f_shapes, grid=(s // tm,), in_specs=head_specs + [nat] + [ANY] * len(after),
        out_specs=specs * 2, scratch_shapes=[pltpu.VMEM((HEADS_PER_GROUP, tm, HEAD_DIM), F32)], name="attn_delta",
        compiler_params=_params(("parallel",)))(*([dcat] * HEADS_PER_GROUP), merged, *after)
    return res[:3], res[3:]


def _mem_probs(qh, kh):
    sc = lax.dot_general(qh, kh, NT_DIMS, preferred_element_type=F32) * SCALE
    p = jnp.exp(sc - jnp.max(sc, axis=-1, keepdims=True))
    return p, jnp.sum(p, axis=-1, keepdims=True)


def _mem_fwd(name, proj, q_block, kv, into, out_block):
    s = proj.shape[0]
    tq = _pick_rows(s, 512)

    def body(q_ref, kv_ref, into_ref, o_ref):
        for h in range(MEM_HEADS):
            sl = slice(h * HEAD_DIM, (h + 1) * HEAD_DIM)
            vsl = slice(MEM_W + h * HEAD_DIM, MEM_W + (h + 1) * HEAD_DIM)
            p, den = _mem_probs(q_ref[:, sl].astype(BF), kv_ref[:, sl].astype(BF))
            out = jnp.dot(p.astype(BF), kv_ref[:, vsl].astype(BF), preferred_element_type=F32) / den
            o_ref[:, sl] = out.astype(o_ref.dtype)

    return pl.pallas_call(
        body, out_shape=jax.ShapeDtypeStruct(into.shape, into.dtype), grid=(s // tq,),
        in_specs=[pl.BlockSpec((tq, MEM_W), lambda i: (i, q_block)), pl.BlockSpec(kv.shape, lambda i: (0, 0)), ANY],
        out_specs=pl.BlockSpec((tq, MEM_W), lambda i: (i, out_block)), input_output_aliases={2: 0}, name=name,
        compiler_params=_params(("parallel",)))(proj, kv, into)


def _mem_bwd(name, proj, q_block, kv, dcat, d_block, width):
    s = proj.shape[0]
    tq = _pick_rows(s, 512)

    def body(q_ref, kv_ref, do_ref, dq_ref, dkv_ref):
        @pl.when(pl.program_id(0) == 0)
        def _():
            dkv_ref[...] = jnp.zeros_like(dkv_ref)

        for h in range(MEM_HEADS):
            sl = slice(h * HEAD_DIM, (h + 1) * HEAD_DIM)
            vsl = slice(MEM_W + h * HEAD_DIM, MEM_W + (h + 1) * HEAD_DIM)
            qh, kh, vh = q_ref[:, sl].astype(BF), kv_ref[:, sl].astype(BF), kv_ref[:, vsl].astype(BF)
            doh = do_ref[:, sl].astype(BF)
            p, den = _mem_probs(qh, kh)
            p = p / den
            dp = lax.dot_general(doh, vh, NT_DIMS, preferred_element_type=F32)
            ds = (p * (dp - jnp.sum(p * dp, axis=-1, keepdims=True)) * SCALE).astype(BF)
            dq_ref[:, sl] = jnp.dot(ds, kh, preferred_element_type=F32).astype(BF)
            dkv_ref[:, sl] += lax.dot_general(ds, qh, TN_DIMS, preferred_element_type=F32)
            dkv_ref[:, vsl] += lax.dot_general(p.astype(BF), doh, TN_DIMS, preferred_element_type=F32)

    whole = pl.BlockSpec(kv.shape, lambda i: (0, 0))
    return pl.pallas_call(
        body, out_shape=[jax.ShapeDtypeStruct((s, width), BF), jax.ShapeDtypeStruct(kv.shape, F32)], grid=(s // tq,),
        in_specs=[pl.BlockSpec((tq, MEM_W), lambda i: (i, q_block)), whole,
                  pl.BlockSpec((tq, MEM_W), lambda i: (i, d_block))],
        out_specs=[pl.BlockSpec((tq, MEM_W), lambda i: (i, width // MEM_W - 1)), whole], name=name,
        compiler_params=_params(("arbitrary",)))(proj, kv, dcat)


def _causal():
    t = lax.broadcasted_iota(jnp.int32, (BLK, BLK), 0)
    s = lax.broadcasted_iota(jnp.int32, (BLK, BLK), 1)
    return t >= s


def _sgu_norm(vg, ln_g, ln_b):
    mu = jnp.mean(vg, axis=-1, keepdims=True)
    cen = vg - mu
    rstd = lax.rsqrt(jnp.mean(cen * cen, axis=-1, keepdims=True) + LN_EPS)
    xhat = cen * rstd
    return xhat, rstd, xhat * ln_g + ln_b


def _sgu_fwd(proj, ln_g, ln_b, w_sp, b_t):
    s = proj.shape[0]

    def body(u_ref, v_ref, g_ref, b_ref, w_ref, bt_ref, o_ref):
        _, _, vn = _sgu_norm(_gelu(v_ref[...].astype(F32)), g_ref[...], b_ref[...])
        vn = vn.astype(BF)
        tri = _causal()
        for grp in range(SGU_GROUPS):
            sl = slice(grp * HEAD_DIM, (grp + 1) * HEAD_DIM)
            w = jnp.where(tri, w_ref[grp], 0.0).astype(BF)
            mixed = jnp.dot(w, vn[:, sl], preferred_element_type=F32) + bt_ref[:, grp:grp + 1]
            o_ref[:, sl] = (_gelu(u_ref[:, sl].astype(F32)) * mixed).astype(BF)

    vec = pl.BlockSpec((1, SGU_W), lambda i: (0, 0))
    return pl.pallas_call(
        body, out_shape=jax.ShapeDtypeStruct((s, SGU_W + MEM_W), BF), grid=(s // BLK,),
        in_specs=[pl.BlockSpec((BLK, SGU_W), lambda i: (i, 0)), pl.BlockSpec((BLK, SGU_W), lambda i: (i, 1)), vec, vec,
                  pl.BlockSpec(w_sp.shape, lambda i: (0, 0, 0)), pl.BlockSpec(b_t.shape, lambda i: (0, 0))],
        out_specs=pl.BlockSpec((BLK, SGU_W), lambda i: (i, 0)), name="sgu_fwd",
        compiler_params=_params(("parallel",)))(proj, proj, ln_g, ln_b, w_sp, b_t)


def _sgu_bwd(proj, dcat, ln_g, ln_b, w_sp, b_t, into):
    s = proj.shape[0]

    def body(u_ref, v_ref, d_ref, g_ref, b_ref, w_ref, bt_ref, into_ref, dp_ref, dw_ref, db_ref, dg_ref, dbeta_ref,
             dvn_ref):
        @pl.when(pl.program_id(0) == 0)
        def _():
            dw_ref[...] = jnp.zeros_like(dw_ref)
            db_ref[...] = jnp.zeros_like(db_ref)
            dg_ref[...] = jnp.zeros_like(dg_ref)
            dbeta_ref[...] = jnp.zeros_like(dbeta_ref)

        gain = g_ref[...]
        vg, v_slope = _gelu_both(v_ref[...].astype(F32))
        xhat, rstd, vn = _sgu_norm(vg, gain, b_ref[...])
        vn = vn.astype(BF)
        tri = _causal()
        lane = lax.broadcasted_iota(jnp.int32, (BLK, HEAD_DIM), 1)
        db_acc = jnp.zeros((BLK, HEAD_DIM), F32)
        for grp in range(SGU_GROUPS):
            sl = slice(grp * HEAD_DIM, (grp + 1) * HEAD_DIM)
            w = jnp.where(tri, w_ref[grp], 0.0).astype(BF)
            vn_g = vn[:, sl]
            mixed = jnp.dot(w, vn_g, preferred_element_type=F32) + bt_ref[:, grp:grp + 1]
            u_act, u_slope = _gelu_both(u_ref[:, sl].astype(F32))
            d_out = d_ref[:, sl].astype(F32)
            dp_ref[:, sl] = (d_out * mixed * u_slope).astype(BF)
            dmixed = d_out * u_act
            dm = dmixed.astype(BF)
            dvn_ref[:, sl] = lax.dot_general(w, dm, TN_DIMS, preferred_element_type=F32)
            dw = lax.dot_general(dm, vn_g, NT_DIMS, preferred_element_type=F32)
            dw_ref[grp] += jnp.where(tri, dw, 0.0)
            db_acc += jnp.where(lane == grp, jnp.sum(dmixed, axis=-1, keepdims=True), 0.0)
        db_ref[...] += db_acc
        dvn = dvn_ref[...]
        dg_ref[...] += jnp.sum(dvn * xhat, axis=0, keepdims=True)
        dbeta_ref[...] += jnp.sum(dvn, axis=0, keepdims=True)
        dxh = dvn * gain
        dvg = rstd * (dxh - jnp.mean(dxh, axis=-1, keepdims=True) - xhat * jnp.mean(dxh * xhat, axis=-1, keepdims=True))
        dp_ref[:, SGU_W:] = (dvg * v_slope).astype(BF)

    vec = pl.BlockSpec((1, SGU_W), lambda i: (0, 0))
    row = pl.BlockSpec((BLK, SGU_W), lambda i: (i, 0))
    w_spec = pl.BlockSpec(w_sp.shape, lambda i: (0, 0, 0))
    sq = pl.BlockSpec((BLK, HEAD_DIM), lambda i: (0, 0))
    return pl.pallas_call(
        body,
        out_shape=[jax.ShapeDtypeStruct(into.shape, into.dtype),
                   jax.ShapeDtypeStruct(w_sp.shape, F32), jax.ShapeDtypeStruct((BLK, HEAD_DIM), F32),
                   jax.ShapeDtypeStruct((1, SGU_W), F32), jax.ShapeDtypeStruct((1, SGU_W), F32)],
        grid=(s // BLK,),
        in_specs=[row, pl.BlockSpec((BLK, SGU_W), lambda i: (i, 1)), row, vec, vec, w_spec,
                  pl.BlockSpec(b_t.shape, lambda i: (0, 0)), ANY],
        out_specs=[pl.BlockSpec((BLK, 2 * SGU_W), lambda i: (i, 0)), w_spec, sq, vec, vec],
        scratch_shapes=[pltpu.VMEM((BLK, SGU_W), F32)], input_output_aliases={7: 0}, name="sgu_bwd",
        compiler_params=_params(("arbitrary",)))(proj, proj, dcat, ln_g, ln_b, w_sp, b_t, into)


def _place():
    return lax.axis_index("x"), lax.axis_index("y"), lax.axis_index("c")


def _other_chips(x, y):
    return [(1 - x, y), (x, 1 - y), (1 - x, 1 - y)]


def _peer(x, y, c, mask):
    return (1 - x if mask & 4 else x, 1 - y if mask & 2 else y, 1 - c if mask & 1 else c)


def _in_hbm(a):
    return pltpu.with_memory_space_constraint(a, pltpu.HBM)


def _token_spec():
    return jax.ShapeDtypeStruct((8, LANES), F32), pl.BlockSpec(memory_space=pltpu.VMEM)


def _remote(src, dst, ssem, rsem, to):
    return pltpu.make_async_remote_copy(src_ref=src, dst_ref=dst, send_sem=ssem, recv_sem=rsem, device_id=to,
                                        device_id_type=MESH)


def _place_shard(name, src, layer, place, dtype, after=()):
    _, rows, cols = src.shape
    tr = _pick_rows(rows, 512)

    def body(p_ref, s_ref, *rest):
        rest[-1][...] = s_ref[...].astype(dtype)

    grid_spec = pltpu.PrefetchScalarGridSpec(
        num_scalar_prefetch=1, grid=(rows // tr,),
        in_specs=[pl.BlockSpec((None, tr, cols), lambda i, p: (layer, i, 0))] + [ANY] * len(after),
        out_specs=pl.BlockSpec((None, tr, cols), lambda i, p: (p[1], i, 0)))
    return pl.pallas_call(body, out_shape=jax.ShapeDtypeStruct((N_CHIPS, rows, cols), dtype), grid_spec=grid_spec,
                          name=name, compiler_params=_params(("parallel",)))(place, src, *after)


def _gather_copies(bufs, ssem, rsem):
    x, y, c = _place()
    me = 2 * x + y
    copies = []
    for ai, buf in enumerate(bufs):
        for k, (ox, oy) in enumerate(_other_chips(x, y)):
            copies.append(_remote(buf.at[me], buf.at[me], ssem.at[3 * ai + k], rsem.at[3 * ai + k], (ox, oy, c)))
    return copies


def _reduce_copies(grads, lands, ssem, rsem):
    x, y, c = _place()
    copies = []
    for a, (gr, land) in enumerate(zip(grads, lands)):
        for mask in range(1, N_DEV):
            px, py, pc = _peer(x, y, c, mask)
            copies.append(_remote(gr.at[pc, 2 * px + py], land.at[mask - 1], ssem.at[7 * a + mask - 1],
                                  rsem.at[7 * a + mask - 1], (px, py, pc)))
    return copies


def _half_copies(totals, ssem, rsem):
    x, y, c = _place()
    return [_remote(t.at[c], t.at[c], ssem.at[a], rsem.at[a], (x, y, 1 - c)) for a, t in enumerate(totals)]


def _gather_start(name, groups):
    flat = [s for grp in groups for s in grp]
    n, ng = len(flat), len(groups)

    def body(*refs):
        ins = refs[:n]
        sems = refs[n:n + 2 * ng]
        token = refs[-1]
        idx = 0
        for gi, grp in enumerate(groups):
            for cp in _gather_copies(ins[idx:idx + len(grp)], sems[2 * gi], sems[2 * gi + 1]):
                cp.start()
            idx += len(grp)
        token[...] = jnp.zeros_like(token)

    tok_shape, tok_spec = _token_spec()
    sem_shapes = []
    for grp in groups:
        sem_shapes += [pltpu.SemaphoreType.DMA((3 * len(grp),))] * 2
    res = pl.pallas_call(
        body, name=name,
        out_shape=(*sem_shapes, *[pltpu.HBM(s.shape, s.dtype) for s in flat], tok_shape),
        in_specs=[HBM] * n, out_specs=(*[SEM] * (2 * ng), *[HBM] * n, tok_spec),
        input_output_aliases={i: 2 * ng + i for i in range(n)},
        compiler_params=pltpu.CompilerParams(has_side_effects=EFFECT))(*[_in_hbm(s) for s in flat])
    out, idx = [], 2 * ng
    for gi, grp in enumerate(groups):
        out.append((res[2 * gi], res[2 * gi + 1], list(res[idx:idx + len(grp)])))
        idx += len(grp)
    return out, res[-1]


def _gather_wait(name, ssem, rsem, slabs, after):
    n = len(slabs)

    def body(*refs):
        for cp in _gather_copies(refs[:n], refs[n], refs[n + 1]):
            cp.wait_send()
            cp.wait_recv()

    return pl.pallas_call(
        body, name=name, out_shape=tuple(pltpu.HBM(s.shape, s.dtype) for s in slabs),
        in_specs=[HBM] * n + [SEM, SEM] + [ANY] * len(after), out_specs=tuple([HBM] * n),
        input_output_aliases={i: i for i in range(n)},
        compiler_params=pltpu.CompilerParams(has_side_effects=EFFECT))(*slabs, ssem, rsem, *after)


def _reduce_start(name, grads):
    n = len(grads)
    lands = [lax.empty((N_DEV - 1, *g.shape[2:]), g.dtype) for g in grads]

    def body(*refs):
        token = refs[-1]
        for cp in _reduce_copies(refs[:n], refs[n:2 * n], refs[2 * n], refs[2 * n + 1]):
            cp.start()
        token[...] = jnp.zeros_like(token)

    tok_shape, tok_spec = _token_spec()
    sems = [pltpu.SemaphoreType.DMA((7 * n,))] * 2
    res = pl.pallas_call(
        body, name=name,
        out_shape=(*sems, *[pltpu.HBM(g.shape, g.dtype) for g in grads], *[pltpu.HBM(l.shape, l.dtype) for l in lands],
                   tok_shape),
        in_specs=[HBM] * (2 * n), out_specs=(SEM, SEM, *[HBM] * (2 * n), tok_spec),
        input_output_aliases={i: 2 + i for i in range(2 * n)},
        compiler_params=pltpu.CompilerParams(has_side_effects=EFFECT))(*[_in_hbm(t) for t in (*grads, *lands)])
    return res[0], res[1], list(res[2:2 + n]), list(res[2 + n:2 + 2 * n]), res[-1]


def _reduce_wait(name, ssem, rsem, grads, lands, after):
    n = len(grads)

    def body(*refs):
        for cp in _reduce_copies(refs[:n], refs[n:2 * n], refs[2 * n], refs[2 * n + 1]):
            cp.wait_send()
            cp.wait_recv()

    res = pl.pallas_call(
        body, name=name, out_shape=tuple(pltpu.HBM(t.shape, t.dtype) for t in (*grads, *lands)),
        in_specs=[HBM] * (2 * n) + [SEM, SEM] + [ANY] * len(after), out_specs=tuple([HBM] * (2 * n)),
        input_output_aliases={i: i for i in range(2 * n)},
        compiler_params=pltpu.CompilerParams(has_side_effects=EFFECT))(*grads, *lands, ssem, rsem, *after)
    return list(res[:n]), list(res[n:])


def _sum_pieces(name, grad, land, place):
    _, _, rows, cols = grad.shape
    tr = _pick_rows(rows, 256)

    def body(p_ref, g_ref, l_ref, o_ref):
        tot = g_ref[...].astype(F32)
        for k in range(N_DEV - 1):
            tot = tot + l_ref[k].astype(F32)
        o_ref[...] = tot

    grid_spec = pltpu.PrefetchScalarGridSpec(
        num_scalar_prefetch=1, grid=(rows // tr,),
        in_specs=[pl.BlockSpec((None, None, tr, cols), lambda i, p: (p[0], p[1], i, 0)),
                  pl.BlockSpec((N_DEV - 1, tr, cols), lambda i, p: (0, i, 0))],
        out_specs=pl.BlockSpec((None, tr, cols), lambda i, p: (p[0], i, 0)))
    return pl.pallas_call(body, out_shape=jax.ShapeDtypeStruct((2, rows, cols), F32), grid_spec=grid_spec, name=name,
                          compiler_params=_params(("parallel",)))(place, grad, land)


def _half_start(name, totals):
    n = len(totals)

    def body(*refs):
        token = refs[-1]
        for cp in _half_copies(refs[:n], refs[n], refs[n + 1]):
            cp.start()
        token[...] = jnp.zeros_like(token)

    tok_shape, tok_spec = _token_spec()
    res = pl.pallas_call(
        body, name=name,
        out_shape=(pltpu.SemaphoreType.DMA((n,)), pltpu.SemaphoreType.DMA((n,)),
                   *[pltpu.HBM(t.shape, t.dtype) for t in totals], tok_shape),
        in_specs=[HBM] * n, out_specs=(SEM, SEM, *[HBM] * n, tok_spec),
        input_output_aliases={i: 2 + i for i in range(n)},
        compiler_params=pltpu.CompilerParams(has_side_effects=EFFECT))(*[_in_hbm(t) for t in totals])
    return res[0], res[1], list(res[2:2 + n]), res[-1]


def _half_wait(name, ssem, rsem, totals, after):
    n = len(totals)

    def body(*refs):
        for cp in _half_copies(refs[:n], refs[n], refs[n + 1]):
            cp.wait_send()
            cp.wait_recv()

    res = pl.pallas_call(
        body, name=name, out_shape=tuple(pltpu.HBM(t.shape, t.dtype) for t in totals),
        in_specs=[HBM] * n + [SEM, SEM] + [ANY] * len(after), out_specs=tuple([HBM] * n),
        input_output_aliases={i: i for i in range(n)},
        compiler_params=pltpu.CompilerParams(has_side_effects=EFFECT))(*totals, ssem, rsem, *after)
    return list(res)


def _small_copies(bufs, ssem, rsem):
    x, y, c = _place()
    mine = bufs[0].at[4 * x + 2 * y + c]
    return [_remote(mine, mine, ssem.at[mask - 1], rsem.at[mask - 1], _peer(x, y, c, mask)) for mask in range(1, N_DEV)]


def _small_start(name, slots):
    def body(s_ref, ssem, rsem, thru, token):
        for cp in _small_copies([s_ref], ssem, rsem):
            cp.start()
        token[...] = jnp.zeros_like(token)

    tok_shape, tok_spec = _token_spec()
    sems = [pltpu.SemaphoreType.DMA((N_DEV - 1,))] * 2
    return pl.pallas_call(
        body, name=name, out_shape=(*sems, pltpu.HBM(slots.shape, slots.dtype), tok_shape), in_specs=[HBM],
        out_specs=(SEM, SEM, HBM, tok_spec), input_output_aliases={0: 2},
        compiler_params=pltpu.CompilerParams(has_side_effects=EFFECT))(_in_hbm(slots))


def _small_wait(name, ssem, rsem, slots, after):
    def body(*refs):
        for cp in _small_copies([refs[0]], refs[1], refs[2]):
            cp.wait_send()
            cp.wait_recv()

    return pl.pallas_call(
        body, name=name, out_shape=pltpu.HBM(slots.shape, slots.dtype), in_specs=[HBM, SEM, SEM] + [ANY] * len(after),
        out_specs=HBM, input_output_aliases={0: 0},
        compiler_params=pltpu.CompilerParams(has_side_effects=EFFECT))(slots, ssem, rsem, *after)


def _own_slot(small, me):
    return lax.dynamic_update_slice(jnp.zeros((N_DEV, *small.shape), small.dtype), small[None], (me, 0, 0))


def _sum_devices(name, stacked):
    _, rows, lanes = stacked.shape
    tr = _pick_rows(rows, 512)

    def body(s_ref, o_ref):
        tot = s_ref[0]
        for k in range(1, N_DEV):
            tot = tot + s_ref[k]
        o_ref[...] = tot

    return pl.pallas_call(
        body, out_shape=jax.ShapeDtypeStruct((rows, lanes), F32), grid=(rows // tr,),
        in_specs=[pl.BlockSpec((N_DEV, tr, lanes), lambda i: (0, i, 0))], out_specs=pl.BlockSpec((tr, lanes), lambda i: (i, 0)),
        name=name, compiler_params=_params(("parallel",)))(stacked)


def _adamw(name, w, g, m, v, layer, prev=None):
    layers, rows, cols = w.shape
    tr = _pick_rows(rows, 256)
    c1 = 1.0 - ADAM_B1 ** ADAM_STEP
    c2 = 1.0 - ADAM_B2 ** ADAM_STEP

    def body(w_ref, g_ref, m_ref, v_ref, *rest):
        go_ref, d_ref, nm_ref, nv_ref = rest[-4:]
        gv = g_ref[...]
        nm = ADAM_B1 * m_ref[...] + (1.0 - ADAM_B1) * gv
        nv = ADAM_B2 * v_ref[...] + (1.0 - ADAM_B2) * (gv * gv)
        go_ref[...] = gv
        d_ref[...] = -ADAM_LR * ((nm / c1) / (jnp.sqrt(nv / c2) + ADAM_EPS) + ADAM_WD * w_ref[...])
        nm_ref[...] = nm
        nv_ref[...] = nv

    spec = pl.BlockSpec((None, tr, cols), lambda i: (layer, i, 0))
    prev = list(prev) if prev is not None else []
    return pl.pallas_call(
        body, out_shape=[jax.ShapeDtypeStruct((layers, rows, cols), F32)] * 4, grid=(rows // tr,),
        in_specs=[spec, pl.BlockSpec((tr, cols), lambda i: (i, 0)), spec, spec] + [ANY] * len(prev),
        out_specs=[spec] * 4, input_output_aliases={4 + i: i for i in range(len(prev))}, name=name,
        compiler_params=_params(("parallel",)))(w, g, m, v, *prev)


def _pack(vectors, pad_rows):
    flat = jnp.concatenate([t.reshape(-1) for t in vectors])
    rows = -(-flat.shape[0] // LANES)
    rows = -(-rows // pad_rows) * pad_rows
    return jnp.pad(flat, (0, rows * LANES - flat.shape[0])).reshape(rows, LANES)


def _unpack(packed, shapes):
    flat = packed.reshape(-1)
    out, off = [], 0
    for shp in shapes:
        size = math.prod(shp)
        out.append(flat[off:off + size].reshape(shp))
        off += size
    return out


def kernel(x, mem, positions, mix_norm, mem_norm, w_mem_kv, ffn_norm, w_gate, w_up, w_down, attn_w_in, attn_w_out, sgu_w_in, sgu_ln_g, sgu_ln_b, sgu_w_spatial, sgu_b_spatial, sgu_w_out, final_norm, loss_target, m_mix_norm, m_mem_norm, m_w_mem_kv, m_ffn_norm, m_w_gate, m_w_up, m_w_down, m_attn_w_in, m_attn_w_out, m_sgu_w_in, m_sgu_ln_g, m_sgu_ln_b, m_sgu_w_spatial, m_sgu_b_spatial, m_sgu_w_out, m_final_norm, v_mix_norm, v_mem_norm, v_w_mem_kv, v_ffn_norm, v_w_gate, v_w_up, v_w_down, v_attn_w_in, v_attn_w_out, v_sgu_w_in, v_sgu_ln_g, v_sgu_ln_b, v_sgu_w_spatial, v_sgu_b_spatial, v_sgu_w_out, v_final_norm):
    d_model = x.shape[2]
    x0, mem0, tgt = x[0], mem[0], loss_target[0]
    xi, yi, ci = _place()
    chip = 2 * xi + yi
    place = jnp.stack([ci, chip]).astype(jnp.int32)

    given_w = dict(mix_norm=mix_norm, mem_norm=mem_norm, w_mem_kv=w_mem_kv, ffn_norm=ffn_norm, w_gate=w_gate, w_up=w_up,
                   w_down=w_down, attn_w_in=attn_w_in, attn_w_out=attn_w_out, sgu_w_in=sgu_w_in, sgu_ln_g=sgu_ln_g,
                   sgu_ln_b=sgu_ln_b, sgu_w_spatial=sgu_w_spatial, sgu_b_spatial=sgu_b_spatial, sgu_w_out=sgu_w_out,
                   final_norm=final_norm)
    given_m = dict(mix_norm=m_mix_norm, mem_norm=m_mem_norm, w_mem_kv=m_w_mem_kv, ffn_norm=m_ffn_norm, w_gate=m_w_gate,
                   w_up=m_w_up, w_down=m_w_down, attn_w_in=m_attn_w_in, attn_w_out=m_attn_w_out, sgu_w_in=m_sgu_w_in,
                   sgu_ln_g=m_sgu_ln_g, sgu_ln_b=m_sgu_ln_b, sgu_w_spatial=m_sgu_w_spatial,
                   sgu_b_spatial=m_sgu_b_spatial, sgu_w_out=m_sgu_w_out, final_norm=m_final_norm)
    given_v = dict(mix_norm=v_mix_norm, mem_norm=v_mem_norm, w_mem_kv=v_w_mem_kv, ffn_norm=v_ffn_norm, w_gate=v_w_gate,
                   w_up=v_w_up, w_down=v_w_down, attn_w_in=v_attn_w_in, attn_w_out=v_attn_w_out, sgu_w_in=v_sgu_w_in,
                   sgu_ln_g=v_sgu_ln_g, sgu_ln_b=v_sgu_ln_b, sgu_w_spatial=v_sgu_w_spatial,
                   sgu_b_spatial=v_sgu_b_spatial, sgu_w_out=v_sgu_w_out, final_norm=v_final_norm)

    units = {"attn_w_in": ("attn_w_in", 0, "col"), "w_mem_kv0": ("w_mem_kv", 0, "row"), "attn_w_out": ("attn_w_out", 0, "col"),
             "w_gate0": ("w_gate", 0, "col"), "w_up0": ("w_up", 0, "col"), "w_down0": ("w_down", 0, "row"),
             "sgu_w_in": ("sgu_w_in", 0, "col"), "w_mem_kv1": ("w_mem_kv", 1, "row"), "sgu_w_out": ("sgu_w_out", 0, "row"),
             "w_gate1": ("w_gate", 1, "col"), "w_up1": ("w_up", 1, "col"), "w_down1": ("w_down", 1, "row")}
    gather_groups = [["attn_w_in"], ["w_mem_kv0", "attn_w_out"], ["w_gate0", "w_up0"],
                     ["w_down0", "sgu_w_in", "w_mem_kv1", "ln"], ["sgu_w_out", "w_gate1", "w_up1"], ["w_down1"]]

    first = _place_shard("place_attn_w_in", attn_w_in, 0, place, BF)
    in_flight, token = _gather_start("gather_start_0", [[first]])
    slabs = {u: _place_shard(f"place_{u}", given_w[arr], layer, place, BF, after=[token])
             for u, (arr, layer, _) in units.items() if u != "attn_w_in"}
    slabs["ln"] = _place_shard("place_ln", jnp.concatenate([sgu_ln_g, sgu_ln_b])[None], 0, place, F32, after=[token])
    rest, token = _gather_start("gather_start_1", [[slabs[u] for u in grp] for grp in gather_groups[1:]])
    in_flight += rest
    weights = {}

    def arrive(gi, after):
        ssem, rsem, arrs = in_flight[gi]
        for u, full in zip(gather_groups[gi], _gather_wait(f"gather_wait_{gi}", ssem, rsem, arrs, after)):
            weights[u] = full if u == "ln" else Weight(full, units[u][2])

    w_sp = sgu_w_spatial[0]
    b_t = sgu_b_spatial[0].T
    tables = _rope_tables(positions[0])

    def residual(acc, extra):
        return [extra[0] + acc[0]]

    def memory_kv(layer):
        mem_n = _rms_fwd(f"mem_norm_{layer}", mem0, mem_norm[layer:layer + 1])
        return mem_n, _mm_nn(f"mem_kv_{layer}", mem_n, weights[f"w_mem_kv{layer}"])[0]

    h0 = _rms_fwd("mix_norm_0", x0, mix_norm[0:1], after=[token])
    arrive(0, [h0])
    proj0 = _mm_nn("attn_in", h0, weights["attn_w_in"])[0]
    arrive(1, [proj0])
    qkv = _rope_fwd(proj0, tables)
    qs, ks, vs = qkv[0:3], qkv[3:6], qkv[6:9]
    outs, lses = [], []
    for g in range(len(DILATIONS)):
        o, l = _dil_fwd(g, qs[g], ks[g], vs[g])
        outs.append(o)
        lses.append(l)
    merged, lse = _attn_merge(outs, lses)
    mem_n0, kv0 = memory_kv(0)
    cat0 = _mem_fwd("mem_fwd_0", proj0, 9, kv0, merged, 1)
    x1, hf0 = _mm_nn("attn_out", cat0, weights["attn_w_out"], extras=[x0], epilogue=residual, norm_gain=ffn_norm[0:1])
    arrive(2, [x1])
    g0, u0, act0 = _gate_up("gate_up_0", hf0, weights["w_gate0"], weights["w_up0"])
    arrive(3, [act0])
    x2 = _mm_nn("down_0", act0, weights["w_down0"], extras=[x1], epilogue=residual)[0]

    ln_all = weights["ln"]
    ln_g = ln_all[:, 0, :].reshape(1, SGU_W)
    ln_b = ln_all[:, 1, :].reshape(1, SGU_W)
    h1 = _rms_fwd("mix_norm_1", x2, mix_norm[1:2])
    proj1 = _mm_nn("sgu_in", h1, weights["sgu_w_in"], out_dtypes=(BF,))[0]
    arrive(4, [proj1])
    sgu_out = _sgu_fwd(proj1, ln_g, ln_b, w_sp, b_t)
    mem_n1, kv1 = memory_kv(1)
    cat1 = _mem_fwd("mem_fwd_1", proj1, 6, kv1, sgu_out, 3)
    x3, hf1 = _mm_nn("sgu_out", cat1, weights["sgu_w_out"], extras=[x2], epilogue=residual, norm_gain=ffn_norm[1:2])
    g1, u1, act1 = _gate_up("gate_up_1", hf1, weights["w_gate1"], weights["w_up1"])
    arrive(5, [act1])
    x4 = _mm_nn("down_1", act1, weights["w_down1"], extras=[x3], epilogue=residual)[0]

    d4, d4_op, g_final, loss_part = _final_loss(x4, tgt, final_norm.reshape(1, d_model))
    loss = lax.psum(loss_part[0, 0], ("x", "y", "c"))

    outputs = {}

    def start_reduce(tag, names, grads):
        ssem, rsem, grads, lands, tok = _reduce_start(f"reduce_start_{tag}", grads)
        return dict(tag=tag, names=names, ssem=ssem, rsem=rsem, grads=grads, lands=lands), tok

    def finish_reduce(st, after):
        grads, lands = _reduce_wait(f"reduce_wait_{st['tag']}", st["ssem"], st["rsem"], st["grads"], st["lands"], after)
        totals = [_sum_pieces(f"sum_{u}", g, l, place) for u, g, l in zip(st["names"], grads, lands)]
        ssem, rsem, totals, tok = _half_start(f"half_start_{st['tag']}", totals)
        return dict(tag=st["tag"], names=st["names"], ssem=ssem, rsem=rsem, totals=totals), tok

    def finish_update(st, after):
        totals = _half_wait(f"half_wait_{st['tag']}", st["ssem"], st["rsem"], st["totals"], after)
        for u, tot in zip(st["names"], totals):
            arr, layer, _ = units[u]
            w = given_w[arr]
            outputs[arr] = _adamw(f"adamw_{u}", w, tot.reshape(w.shape[1:]), given_m[arr], given_v[arr], layer,
                                  outputs.get(arr))

    def ffn_bwd(layer, d_out, d_out_op, xin, h, g, u, act):
        wd, wg, wu = weights[f"w_down{layer}"], weights[f"w_gate{layer}"], weights[f"w_up{layer}"]
        gr_down = _mm_tn(f"d_down_{layer}", act, d_out_op, wd)
        dg, du = _mm_nt(f"d_act_{layer}", [d_out_op], [wd], out_dtypes=(BF, BF), extras=[g, u],
                        epilogue=_swiglu_bwd_epilogue, col_chunk=EPILOGUE_CHUNK)
        gr_gate = _mm_tn(f"d_gate_{layer}", h, dg, wg)
        gr_up = _mm_tn(f"d_up_{layer}", h, du, wu)
        st, tok = start_reduce(f"ffn{layer}", [f"w_down{layer}", f"w_gate{layer}", f"w_up{layer}"], [gr_down, gr_gate, gr_up])
        dh = _mm_nt(f"d_ffn_h_{layer}", [dg, du], [wg, wu], out_dtypes=(BF,), after=[tok])[0]
        d_in, d_in_op, g_norm = _rms_bwd(f"ffn_norm_bwd_{layer}", xin, ffn_norm[layer:layer + 1], dh, d_out)
        return st, d_in, d_in_op, g_norm

    def memory_bwd(layer, mem_n, dkv):
        dkv = dkv.astype(BF)
        wkv = weights[f"w_mem_kv{layer}"]
        gr = _mm_tn(f"d_mem_kv_{layer}", mem_n, dkv, wkv)
        d_mem_n = _mm_nt(f"d_mem_n_{layer}", [dkv], [wkv])[0]
        return gr, _rms_bwd(f"mem_norm_bwd_{layer}", mem0, mem_norm[layer:layer + 1], d_mem_n)[2]

    st_ffn1, d3, d3_op, g_ffn1 = ffn_bwd(1, d4, d4_op, x3, hf1, g1, u1, act1)
    gr_sgu_out = _mm_tn("d_sgu_out", cat1, d3_op, weights["sgu_w_out"])
    dcat1 = _mm_nt("d_cat_1", [d3_op], [weights["sgu_w_out"]], out_dtypes=(BF,))[0]
    st_ffn1, tok = finish_reduce(st_ffn1, [dcat1])
    dproj1, dkv1 = _mem_bwd("mem_bwd_1", proj1, 6, kv1, dcat1, 3, proj1.shape[1])
    gr_kv1, g_mem1 = memory_bwd(1, mem_n1, dkv1)
    dproj1, g_wsp, g_bsp_t, g_ln_g, g_ln_b = _sgu_bwd(proj1, dcat1, ln_g, ln_b, w_sp, b_t, dproj1)
    gr_sgu_in = _mm_tn("d_sgu_in", h1, dproj1, weights["sgu_w_in"], after=[tok])
    finish_update(st_ffn1, [gr_sgu_in])
    st_mix1, tok = start_reduce("mix1", ["sgu_w_out", "w_mem_kv1", "sgu_w_in"], [gr_sgu_out, gr_kv1, gr_sgu_in])
    dh1 = _mm_nt("d_h_1", [dproj1], [weights["sgu_w_in"]], out_dtypes=(BF,), after=[tok])[0]
    d2, d2_op, g_mix1 = _rms_bwd("mix_norm_bwd_1", x2, mix_norm[1:2], dh1, d3)

    st_ffn0, d1, d1_op, g_ffn0 = ffn_bwd(0, d2, d2_op, x1, hf0, g0, u0, act0)
    dev = 4 * xi + 2 * yi + ci
    small_a = [g_mix1, g_mem1, jnp.concatenate([g_ffn0, g_ffn1]), g_wsp, g_bsp_t[:, :SGU_GROUPS].T, g_final, g_ln_g, g_ln_b]
    sa_ssem, sa_rsem, sa_slots, tok = _small_start("small_start_a", _own_slot(_pack(small_a, LANES), dev))
    gr_attn_out = _mm_tn("d_attn_out", cat0, d1_op, weights["attn_w_out"], after=[tok])
    st_mix1, tok = finish_reduce(st_mix1, [gr_attn_out])
    dcat0 = _mm_nt("d_cat_0", [d1_op], [weights["attn_w_out"]], after=[tok])[0]
    dproj0, dkv0 = _mem_bwd("mem_bwd_0", proj0, 9, kv0, dcat0, 1, proj0.shape[1])
    finish_update(st_mix1, [dkv0])
    gr_kv0, g_mem0 = memory_bwd(0, mem_n0, dkv0)
    st_ffn0, tok = finish_reduce(st_ffn0, [g_mem0])
    d_merged, delta = _attn_delta(dcat0, cat0, after=[tok])
    dqs, dks, dvs = [], [], []
    for g in range(len(DILATIONS)):
        dq, dk, dv = _dil_bwd(g, qs[g], ks[g], vs[g], d_merged[g], lse[g], delta[g])
        dqs.append(dq)
        dks.append(dk)
        dvs.append(dv)
    dproj0 = _rope_bwd(dqs + dks + dvs, tables, dproj0)
    finish_update(st_ffn0, [dproj0])
    gr_attn_in = _mm_tn("d_attn_in", h0, dproj0, weights["attn_w_in"])
    st_mix0, tok = start_reduce("mix0", ["attn_w_out", "w_mem_kv0", "attn_w_in"], [gr_attn_out, gr_kv0, gr_attn_in])
    dh0 = _mm_nt("d_h_0", [dproj0], [weights["attn_w_in"]], out_dtypes=(BF,), after=[tok])[0]
    d0, _, g_mix0 = _rms_bwd("mix_norm_bwd_0", x0, mix_norm[0:1], dh0, d1)

    small_b = [g_mix0, g_mem0]
    sb_ssem, sb_rsem, sb_slots, tok = _small_start("small_start_b", _own_slot(_pack(small_b, 8), dev))
    sa_slots = _small_wait("small_wait_a", sa_ssem, sa_rsem, sa_slots, [tok])
    g_mix1, g_mem1, g_ffn, g_wsp, g_bsp, g_final, g_ln_g, g_ln_b = _unpack(_sum_devices("small_sum_a", sa_slots),
                                                                           [t.shape for t in small_a])
    sb_slots = _small_wait("small_wait_b", sb_ssem, sb_rsem, sb_slots, [g_final])
    g_mix0, g_mem0 = _unpack(_sum_devices("small_sum_b", sb_slots), [t.shape for t in small_b])
    st_mix0, tok = finish_reduce(st_mix0, [g_mix0])
    g_mix, g_mem = jnp.concatenate([g_mix0, g_mix1]), jnp.concatenate([g_mem0, g_mem1])
    shard_w = sgu_ln_g.shape[-1]
    g_ln_g = lax.dynamic_slice_in_dim(g_ln_g, chip * shard_w, shard_w, axis=1)
    g_ln_b = lax.dynamic_slice_in_dim(g_ln_b, chip * shard_w, shard_w, axis=1)
    small_names = ["mix_norm", "mem_norm", "ffn_norm", "sgu_w_spatial", "sgu_b_spatial", "final_norm", "sgu_ln_g",
                   "sgu_ln_b"]
    small_g = [g_mix, g_mem, g_ffn, g_wsp, g_bsp, g_final, g_ln_g, g_ln_b]
    small_shapes = [given_w[k].shape for k in small_names]
    packed = [_pack(t, LANES) for t in ([given_w[k] for k in small_names], small_g, [given_m[k] for k in small_names],
                                    [given_v[k] for k in small_names])]
    small_out = _adamw("adamw_small", packed[0][None], packed[1], packed[2][None], packed[3][None], 0)
    finish_update(st_mix0, [small_out[0]])
    for k, gk, dk, mk, vk in zip(small_names, *[_unpack(t[0], small_shapes) for t in small_out]):
        outputs[k] = (gk, dk, mk, vk)

    order = ["mix_norm", "mem_norm", "w_mem_kv", "ffn_norm", "w_gate", "w_up", "w_down", "attn_w_in", "attn_w_out",
             "sgu_w_in", "sgu_ln_g", "sgu_ln_b", "sgu_w_spatial", "sgu_b_spatial", "sgu_w_out", "final_norm"]
    return (loss, d0[None], *[outputs[k][0] for k in order], *[outputs[k][1] for k in order],
            *[outputs[k][2] for k in order], *[outputs[k][3] for k in order])
```

```python
import math

import jax
import jax.numpy as jnp
from jax import lax
from jax.experimental import pallas as pl
from jax.experimental.pallas import tpu as pltpu

F32 = jnp.float32
BF = jnp.bfloat16
MESH = pl.DeviceIdType.MESH

HEAD_DIM = 128
MEM_HEADS = 4
MEM_W = MEM_HEADS * HEAD_DIM
GROUP_W = 4 * HEAD_DIM
DILATIONS = (1, 4, 16)
BLK = 128
SGU_GROUPS = 12
SGU_W = SGU_GROUPS * HEAD_DIM
ROT_HALF = 16
ROPE_THETA = 500000.0
NORM_EPS = 1e-6
LN_EPS = 1e-5
NEG_INF = -1e30
SCALE = HEAD_DIM ** -0.5
ADAM_LR, ADAM_B1, ADAM_B2, ADAM_EPS, ADAM_WD, ADAM_STEP = 0.001, 0.9, 0.999, 1e-08, 0.01, 10

VMEM_LIMIT = 48 * 2 ** 20
VMEM_TILE_BUDGET = 38 * 2 ** 20
N_CHIPS = 4
N_DEV = 8
LANES = 128
EPILOGUE_CHUNK = 256

NT_DIMS = (((1,), (1,)), ((), ()))
TN_DIMS = (((0,), (0,)), ((), ()))
NN_DIMS = (((1,), (0,)), ((), ()))

ANY = pl.BlockSpec(memory_space=pl.ANY)
HBM = pl.BlockSpec(memory_space=pltpu.HBM)
SEM = pl.BlockSpec(memory_space=pltpu.SEMAPHORE)
EFFECT = pltpu.SideEffectType.DATAFLOW_SIDE_EFFECTING


def _params(sem):
    return pltpu.CompilerParams(dimension_semantics=sem, vmem_limit_bytes=VMEM_LIMIT)


def _pick(n, cap):
    if n <= cap:
        return n
    best = None
    for t in range(LANES, cap + 1, LANES):
        if n % t == 0:
            best = t
    assert best is not None, (n, cap)
    return best


def _pick_rows(n, cap):
    t = min(n, cap)
    while n % t:
        t //= 2
    return t


def _mm(name, dims, a_list, a_specs, b_list, b_specs, pairs, n_acc, acc_shape, grid, extras, e_specs,
        out_shapes, out_specs, epilogue, after=(), col_chunk=None, store=None, shard_width=None, norm_gain=None):
    na, nb, ne, no = len(a_list), len(b_list), len(extras), len(out_shapes)
    nk = grid[-1]
    ng = 0 if norm_gain is None else 1

    def products(a, b, cols=None):
        sums = [None] * n_acc
        for ai, bi, ci in pairs:
            bv = b[bi]
            if cols is None:
                bv = bv[...]
            elif dims == NT_DIMS:
                bv = bv[cols, :]
            else:
                bv = bv[:, cols]
            if bv.ndim == 3:
                bv = bv.reshape(-1, bv.shape[-1])
            prod = lax.dot_general(a[ai][...].astype(BF), bv.astype(BF), dims, preferred_element_type=F32)
            sums[ci] = prod if sums[ci] is None else sums[ci] + prod
        return sums

    def body(*refs):
        a = refs[:na]
        b = refs[na:na + nb]
        e = refs[na + nb:na + nb + ne]
        off = na + nb + ne + ng + len(after)
        o = refs[off:off + no]
        acc = refs[off + no:]

        def normed():
            if ng:
                xf = o[0][...]
                r = lax.rsqrt(jnp.mean(xf * xf, axis=-1, keepdims=True) + NORM_EPS)
                o[-1][...] = (xf * r * refs[na + nb + ne][...]).astype(o[-1].dtype)

        def finish(sums):
            outs = epilogue(sums, [r[...] for r in e])
            if store is not None:
                store(o, outs)
                return
            for r, v in zip(o, outs):
                r[...] = v.astype(r.dtype)
            normed()

        if nk == 1 and shard_width:
            (ai, bi, _), = pairs
            av = a[ai][...].astype(BF)
            if dims == NT_DIMS:
                total = None
                for j in range(N_CHIPS):
                    cols = slice(j * shard_width, (j + 1) * shard_width)
                    prod = lax.dot_general(av[:, cols], b[bi][j].astype(BF), dims, preferred_element_type=F32)
                    total = prod if total is None else total + prod
                finish([total])
                return
            for j in range(N_CHIPS):
                cols = slice(j * shard_width, (j + 1) * shard_width)
                prod = lax.dot_general(av, b[bi][j].astype(BF), dims, preferred_element_type=F32)
                outs = epilogue([prod], [r[:, cols] for r in e])
                for r, v in zip(o, outs):
                    r[:, cols] = v.astype(r.dtype)
            normed()
            return
        if nk == 1 and col_chunk:
            width = acc_shape[1]
            left = [r[...].astype(BF) for r in a]
            for c0 in range(0, width, col_chunk):
                cols = slice(c0, min(c0 + col_chunk, width))
                outs = epilogue(products(left, b, cols), [r[:, cols] for r in e])
                for r, v in zip(o, outs):
                    r[:, cols] = v.astype(r.dtype)
            return
        if nk == 1:
            finish(products(a, b))
            return
        k = pl.program_id(len(grid) - 1)

        @pl.when(k == 0)
        def _():
            for c, v in zip(acc, products(a, b)):
                c[...] = v

        @pl.when(jnp.logical_and(k > 0, k < nk - 1))
        def _():
            for c, v in zip(acc, products(a, b)):
                c[...] += v

        @pl.when(k == nk - 1)
        def _():
            finish([c[...] + v for c, v in zip(acc, products(a, b))])

    gains = [] if norm_gain is None else [norm_gain]
    ins = [*a_list, *b_list, *extras, *gains, *after]
    in_specs = [*a_specs, *b_specs, *e_specs, *[pl.BlockSpec(g.shape, lambda *_: (0, 0)) for g in gains],
                *([ANY] * len(after))]
    sem = ("parallel",) * (len(grid) - 1) + ("arbitrary",)
    scratch = [] if nk == 1 else [pltpu.VMEM(acc_shape, F32)] * n_acc
    return pl.pallas_call(
        body, out_shape=out_shapes, grid=grid, in_specs=in_specs, out_specs=out_specs, scratch_shapes=scratch,
        name=name, compiler_params=_params(sem))(*ins)


def _tile_bytes(blocks, single=()):
    size = lambda s, d: math.prod(s) * jnp.dtype(d).itemsize
    return sum(2 * size(s, d) for s, d in blocks) + sum(size(s, d) for s, d in single)


def _first(acc, extra):
    return [acc[0]]


def _sigmoid(x):
    return 0.5 * (1.0 + jnp.tanh(0.5 * x))


class Weight:
    def __init__(self, arr, axis):
        self.arr, self.axis = arr, axis
        _, self.rows, self.cols = arr.shape


SMALL_WEIGHT_BYTES = 8 * 2 ** 20


def _is_small(w):
    return w.arr.size * w.arr.dtype.itemsize <= SMALL_WEIGHT_BYTES


def _mm_nn(name, a, w, extras=(), epilogue=_first, out_dtypes=(F32,), after=(), norm_gain=None):
    m, kdim = a.shape
    b_spec, shard_width = None, None
    weight_buffers = 2
    if norm_gain is not None:
        out_dtypes = (*out_dtypes, BF)
    if w.axis == "col" and _is_small(w):
        n_total = tn = N_CHIPS * w.cols
        tk, gn, gk = kdim, 1, 1
        shard_width = w.cols
        b_spec = pl.BlockSpec((N_CHIPS, kdim, w.cols), lambda n, i, k: (0, 0, 0))
    elif w.axis == "col":
        n_total = N_CHIPS * w.cols
        tn = _pick(w.cols, 1408)
        tk = _pick(kdim, 2048)
        ncb = w.cols // tn
        gn, gk = N_CHIPS * ncb, kdim // tk
        b_map = lambda n, i, k: (n // ncb, k, n % ncb)
    elif kdim <= 2048 and norm_gain is not None:
        n_total = tn = w.cols
        tk, gn, gk = kdim, 1, 1
        weight_buffers = 1
        b_spec = pl.BlockSpec(w.arr.shape, lambda n, i, k: (0, 0, 0), pipeline_mode=pl.Buffered(1))
    elif kdim <= 2048:
        n_total = w.cols
        tn = _pick(w.cols, 1024)
        tk = kdim
        gn, gk = n_total // tn, 1
        b_spec = pl.BlockSpec((N_CHIPS, w.rows, tn), lambda n, i, k: (0, 0, n))
    else:
        n_total = w.cols
        tn = _pick(w.cols, 1024)
        tk = _pick(w.rows, 1408)
        nkb = w.rows // tk
        gn, gk = n_total // tn, N_CHIPS * nkb
        b_map = lambda n, i, k: (k // nkb, k % nkb, n)
    if b_spec is None:
        b_spec = pl.BlockSpec((None, tk, tn), b_map)
    for tm in (1024, 512, 256, 128):
        if m % tm:
            continue
        blocks = [((tm, tk), a.dtype)] + [((tm, tn), e.dtype) for e in extras]
        blocks += [((tm, tn), d) for d in out_dtypes] + [((tm, tn), BF)]
        weight = [((tk, tn), BF)]
        if _tile_bytes(blocks + (weight if weight_buffers == 2 else []), weight if weight_buffers == 1 else ()) <= VMEM_TILE_BUDGET:
            break
    assert norm_gain is None or tn == n_total, name
    o_spec = pl.BlockSpec((tm, tn), lambda n, i, k: (i, n))
    return _mm(
        name, NN_DIMS, [a], [pl.BlockSpec((tm, tk), lambda n, i, k: (i, k))],
        [w.arr], [b_spec], [(0, 0, 0)], 1, (tm, tn), (gn, m // tm, gk),
        list(extras), [o_spec] * len(extras),
        [jax.ShapeDtypeStruct((m, n_total), d) for d in out_dtypes], [o_spec] * len(out_dtypes), epilogue, after,
        shard_width=shard_width, norm_gain=norm_gain)


def _gate_up(name, h, wg, wu):
    m, kdim = h.shape
    tn = _pick(wg.cols, 1408)
    tk = _pick(kdim, 2048)
    ncb = wg.cols // tn
    single = kdim == tk
    for tm in (1024, 512, 256, 128):
        blocks = [((tm, tk), BF)] + [((tm, tn), BF)] * 3
        weights = [((tk, tn), BF)] * 2
        if m % tm == 0 and _tile_bytes(blocks + ([] if single else weights), weights if single else ()) <= VMEM_TILE_BUDGET:
            break
    b_spec = pl.BlockSpec((None, tk, tn), lambda n, i, k: (n // ncb, k, n % ncb),
                          pipeline_mode=pl.Buffered(1) if single else None)
    o_spec = pl.BlockSpec((tm, tn), lambda n, i, k: (i, n))
    n_total = N_CHIPS * wg.cols

    def epilogue(acc, extra):
        g, u = acc
        return [g, u, g * _sigmoid(g) * u]

    return _mm(
        name, NN_DIMS, [h], [pl.BlockSpec((tm, tk), lambda n, i, k: (i, k))], [wg.arr, wu.arr], [b_spec, b_spec],
        [(0, 0, 0), (0, 1, 1)], 2, (tm, tn), (N_CHIPS * ncb, m // tm, kdim // tk), [], [],
        [jax.ShapeDtypeStruct((m, n_total), BF)] * 3, [o_spec] * 3, epilogue, col_chunk=EPILOGUE_CHUNK)


def _mm_nt(name, dys, ws, out_dtypes=(F32,), extras=(), epilogue=_first, after=(), col_chunk=None):
    m = dys[0].shape[0]
    w0 = ws[0]
    npair = len(dys)
    b_spec, shard_width = None, None
    if w0.axis == "col" and npair == 1 and _is_small(w0):
        k_total = tko = w0.rows
        tkc = N_CHIPS * w0.cols
        go, gk = 1, 1
        shard_width = w0.cols
        b_spec = pl.BlockSpec(w0.arr.shape, lambda o, i, k: (0, 0, 0))
    elif w0.axis == "col":
        k_total = w0.rows
        tko = _pick(k_total, 1024)
        tkc = _pick(w0.cols, 1408)
        nkb = w0.cols // tkc
        go, gk = k_total // tko, N_CHIPS * nkb
        b_map = lambda o, i, k: (k // nkb, o, k % nkb)
    else:
        k_total = N_CHIPS * w0.rows
        tko = _pick(w0.rows, 1408)
        tkc = _pick(w0.cols, 2048)
        nob = w0.rows // tko
        go, gk = N_CHIPS * nob, w0.cols // tkc
        b_map = lambda o, i, k: (o // nob, o % nob, k)
    single = gk == 1
    for tm in (1024, 512, 256, 128):
        if m % tm:
            continue
        blocks = [((tm, tkc), d.dtype) for d in dys]
        blocks += [((tm, tko), e.dtype) for e in extras] + [((tm, tko), d) for d in out_dtypes]
        blocks += [((tm, tko), BF)]
        weights = [((tko, tkc), BF)] * npair
        if _tile_bytes(blocks + ([] if single else weights), weights if single else ()) <= VMEM_TILE_BUDGET:
            break
    if b_spec is None:
        b_spec = pl.BlockSpec((None, tko, tkc), b_map, pipeline_mode=pl.Buffered(1) if single else None)
    o_spec = pl.BlockSpec((tm, tko), lambda o, i, k: (i, o))
    return _mm(
        name, NT_DIMS, list(dys), [pl.BlockSpec((tm, tkc), lambda o, i, k: (i, k))] * npair,
        [w.arr for w in ws], [b_spec] * npair,
        [(i, i, 0) for i in range(npair)], 1, (tm, tko), (go, m // tm, gk), list(extras), [o_spec] * len(extras),
        [jax.ShapeDtypeStruct((m, k_total), d) for d in out_dtypes], [o_spec] * len(out_dtypes), epilogue, after,
        col_chunk if gk == 1 and shard_width is None else None, shard_width=shard_width)


def _mm_tn(name, a, dy, w, after=()):
    m, k_total = a.shape
    rows2 = w.rows // 2
    tn = _pick(w.cols, 1408)
    ncb = w.cols // tn
    epilogue, store = _first, None
    if k_total <= 2048 and w.axis == "col" and _is_small(w):
        tkr, tn = k_total, N_CHIPS * w.cols
        gr, gn = 1, 1
        o_spec = pl.BlockSpec((2, N_CHIPS, rows2, w.cols), lambda r, n, t: (0, 0, 0, 0))

        def store(o_refs, outs):
            for j in range(N_CHIPS):
                for h in range(2):
                    o_refs[0][h, j] = outs[0][h * rows2:(h + 1) * rows2, j * w.cols:(j + 1) * w.cols].astype(BF)
    elif k_total <= 2048 and w.axis == "col":
        tkr = k_total
        gr, gn = 1, N_CHIPS * ncb
        o_spec = pl.BlockSpec((2, None, rows2, tn), lambda r, n, t: (0, n // ncb, 0, n % ncb))
        epilogue = lambda acc, extra: [acc[0].reshape(2, rows2, tn)]
    elif k_total <= 2048:
        tkr = k_total
        gr, gn = 1, ncb
        o_spec = pl.BlockSpec((2, N_CHIPS, rows2, tn), lambda r, n, t: (0, 0, 0, n))

        def store(o_refs, outs):
            for j in range(N_CHIPS):
                for h in range(2):
                    lo = (2 * j + h) * rows2
                    o_refs[0][h, j] = outs[0][lo:lo + rows2].astype(BF)
    elif rows2 % LANES:
        tkr = w.rows
        assert w.axis == "row"
        gr, gn = N_CHIPS, ncb
        o_spec = pl.BlockSpec((2, None, rows2, tn), lambda r, n, t: (0, r, 0, n))
        epilogue = lambda acc, extra: [acc[0].reshape(2, rows2, tn)]
    else:
        tkr = _pick(rows2, 1408)
        nrb = rows2 // tkr
        if w.axis == "col":
            gr, gn = w.rows // tkr, N_CHIPS * ncb
            o_map = lambda r, n, t: (r // nrb, n // ncb, r % nrb, n % ncb)
        else:
            per = w.rows // tkr
            gr, gn = N_CHIPS * per, ncb
            o_map = lambda r, n, t: ((r % per) // nrb, r // per, (r % per) % nrb, n)
        o_spec = pl.BlockSpec((None, None, tkr, tn), o_map)
    for tmk in (1024, 512, 256, 128):
        blocks = [((tmk, tkr), a.dtype), ((tmk, tn), dy.dtype), ((tkr, tn), BF), ((tkr, tn), BF)]
        if m % tmk == 0 and _tile_bytes(blocks) <= VMEM_TILE_BUDGET:
            break
    return _mm(
        name, TN_DIMS, [a], [pl.BlockSpec((tmk, tkr), lambda r, n, t: (t, r))],
        [dy], [pl.BlockSpec((tmk, tn), lambda r, n, t: (t, n))], [(0, 0, 0)], 1, (tkr, tn), (gr, gn, m // tmk), [], [],
        [jax.ShapeDtypeStruct((2, N_CHIPS, rows2, w.cols), BF)], [o_spec], epilogue, after, store=store)[0]


def _rms_fwd(name, x, g, after=()):
    s, d = x.shape
    tr = _pick_rows(s, 512)

    def body(x_ref, g_ref, *rest):
        h_ref = rest[-1]
        xf = x_ref[...]
        r = lax.rsqrt(jnp.mean(xf * xf, axis=-1, keepdims=True) + NORM_EPS)
        h_ref[...] = (xf * r * g_ref[...]).astype(BF)

    return pl.pallas_call(
        body, out_shape=jax.ShapeDtypeStruct((s, d), BF), grid=(s // tr,),
        in_specs=[pl.BlockSpec((tr, d), lambda i: (i, 0)), pl.BlockSpec((1, d), lambda i: (0, 0))] + [ANY] * len(after),
        out_specs=pl.BlockSpec((tr, d), lambda i: (i, 0)), name=name, compiler_params=_params(("parallel",)))(x, g, *after)


def _rms_bwd(name, x, g, dh, dres=None):
    s, d = x.shape
    tr = _pick_rows(s, 256)
    has_res = dres is not None

    def body(*refs):
        if has_res:
            x_ref, g_ref, dh_ref, dres_ref, dx_ref, dxb_ref, dg_ref = refs
        else:
            x_ref, g_ref, dh_ref, dx_ref, dxb_ref, dg_ref = refs
        xf = x_ref[...]
        r = lax.rsqrt(jnp.mean(xf * xf, axis=-1, keepdims=True) + NORM_EPS)
        xr = xf * r
        dy = dh_ref[...].astype(F32)
        a = dy * g_ref[...]
        dx = r * (a - xr * jnp.mean(a * xr, axis=-1, keepdims=True))
        if has_res:
            dx = dx + dres_ref[...]
        dx_ref[...] = dx
        dxb_ref[...] = dx.astype(BF)

        @pl.when(pl.program_id(0) == 0)
        def _():
            dg_ref[...] = jnp.zeros_like(dg_ref)

        dg_ref[...] += jnp.sum(dy * xr, axis=0, keepdims=True)

    row = pl.BlockSpec((tr, d), lambda i: (i, 0))
    vec = pl.BlockSpec((1, d), lambda i: (0, 0))
    ins = [x, g, dh] + ([dres] if has_res else [])
    in_specs = [row, vec, row] + ([row] if has_res else [])
    return pl.pallas_call(
        body, out_shape=[jax.ShapeDtypeStruct((s, d), F32), jax.ShapeDtypeStruct((s, d), BF),
                         jax.ShapeDtypeStruct((1, d), F32)],
        grid=(s // tr,), in_specs=in_specs, out_specs=[row, row, vec], name=name,
        compiler_params=_params(("arbitrary",)))(*ins)


def _final_loss(x, tgt, g):
    s, d = x.shape
    tr = _pick_rows(s, 256)

    def body(x_ref, t_ref, g_ref, dx_ref, dxb_ref, dg_ref, loss_ref):
        xf = x_ref[...]
        gain = g_ref[...]
        r = lax.rsqrt(jnp.mean(xf * xf, axis=-1, keepdims=True) + NORM_EPS)
        xr = xf * r
        err = xr * gain - t_ref[...]
        dy = err * (1.0 / d)
        a = dy * gain
        dx = r * (a - xr * jnp.mean(a * xr, axis=-1, keepdims=True))
        dx_ref[...] = dx
        dxb_ref[...] = dx.astype(BF)

        @pl.when(pl.program_id(0) == 0)
        def _():
            dg_ref[...] = jnp.zeros_like(dg_ref)
            loss_ref[...] = jnp.zeros_like(loss_ref)

        dg_ref[...] += jnp.sum(dy * xr, axis=0, keepdims=True)
        part = 0.5 * jnp.sum(jnp.mean(err * err, axis=-1, keepdims=True), axis=0, keepdims=True)
        loss_ref[...] += jnp.broadcast_to(part, loss_ref.shape)

    row = pl.BlockSpec((tr, d), lambda i: (i, 0))
    vec = pl.BlockSpec((1, d), lambda i: (0, 0))
    return pl.pallas_call(
        body, out_shape=[jax.ShapeDtypeStruct((s, d), F32), jax.ShapeDtypeStruct((s, d), BF),
                         jax.ShapeDtypeStruct((1, d), F32), jax.ShapeDtypeStruct((8, LANES), F32)],
        grid=(s // tr,), in_specs=[row, row, vec],
        out_specs=[row, row, vec, pl.BlockSpec((8, LANES), lambda i: (0, 0))],
        name="final_loss", compiler_params=_params(("arbitrary",)))(x, tgt, g)


def _swiglu_bwd_epilogue(acc, extra):
    dact = acc[0]
    g, u = extra[0].astype(F32), extra[1].astype(F32)
    sig = _sigmoid(g)
    return [dact * u * sig * (1.0 + g * (1.0 - sig)), dact * g * sig]


GELU_C = math.sqrt(2.0 / math.pi)
GELU_A = 0.044715


def _gelu(x):
    return 0.5 * x * (1.0 + jnp.tanh(GELU_C * (x + GELU_A * x * x * x)))


def _gelu_both(x):
    x2 = x * x
    t = jnp.tanh(GELU_C * (x + GELU_A * x2 * x))
    half = 0.5 * (1.0 + t)
    return x * half, half + 0.5 * x * (1.0 - t * t) * GELU_C * (1.0 + 3.0 * GELU_A * x2)


def _rope_tables(positions):
    inv_freq = ROPE_THETA ** (-jnp.arange(ROT_HALF, dtype=F32) / ROT_HALF)
    ang = positions.astype(F32)[:, None] * inv_freq
    cos, sin = jnp.cos(ang), jnp.sin(ang)
    s = ang.shape[0]
    rest = HEAD_DIM - 2 * ROT_HALF
    zeros = jnp.zeros((s, ROT_HALF), F32)
    cos_t = jnp.concatenate([cos, cos, jnp.ones((s, rest), F32)], axis=1)
    sin_a = jnp.concatenate([-sin, zeros, jnp.zeros((s, rest), F32)], axis=1)
    sin_b = jnp.concatenate([zeros, sin, jnp.zeros((s, rest), F32)], axis=1)
    return cos_t, sin_a, sin_b


def _rope_head(xh, cos_t, sin_a, sin_b):
    up = pltpu.roll(xh, HEAD_DIM - ROT_HALF, 1)
    down = pltpu.roll(xh, ROT_HALF, 1)
    return xh * cos_t + up * sin_a + down * sin_b


def _residue(r, rows, dil):
    return slice(None) if dil == 1 else pl.ds(r, rows, stride=dil)


ROPE_TILE = 256
N_PARTS = 9
HEADS_PER_GROUP = GROUP_W // HEAD_DIM
N_HEADS_IN = N_PARTS * HEADS_PER_GROUP


def _rope_fwd(proj, tables):
    s = proj.shape[0]
    tm = _pick_rows(s, ROPE_TILE)

    def body(*refs):
        heads = refs[:N_HEADS_IN]
        c_ref, sa_ref, sb_ref = refs[N_HEADS_IN:N_HEADS_IN + 3]
        outs = refs[N_HEADS_IN + 3:]
        for g, dil in enumerate(DILATIONS):
            rows = tm // dil
            for r in range(dil):
                rs = _residue(r, rows, dil)
                cos_t, sin_a, sin_b = c_ref[rs, :], sa_ref[rs, :], sb_ref[rs, :]
                for kind in range(3):
                    part = 3 * kind + g
                    for h in range(HEADS_PER_GROUP):
                        xh = heads[part * HEADS_PER_GROUP + h][rs, :]
                        if kind < 2:
                            xh = _rope_head(xh, cos_t, sin_a, sin_b)
                        outs[part][r, :, h * HEAD_DIM:(h + 1) * HEAD_DIM] = xh.astype(BF)

    tab = pl.BlockSpec((tm, HEAD_DIM), lambda i: (i, 0))
    head_specs = [pl.BlockSpec((tm, HEAD_DIM), lambda i, j=j: (i, j)) for j in range(N_HEADS_IN)]
    shapes, specs = [], []
    for part in range(N_PARTS):
        dil = DILATIONS[part % 3]
        shapes.append(jax.ShapeDtypeStruct((dil, s // dil, GROUP_W), BF))
        specs.append(pl.BlockSpec((dil, tm // dil, GROUP_W), lambda i: (0, i, 0)))
    return pl.pallas_call(
        body, out_shape=shapes, grid=(s // tm,), in_specs=head_specs + [tab, tab, tab], out_specs=specs,
        name="rope_fwd", compiler_params=_params(("parallel",)))(*([proj] * N_HEADS_IN), *tables)


def _rope_bwd(parts, tables, into):
    s = into.shape[0]
    tm = _pick_rows(s, ROPE_TILE)

    def body(*refs):
        ins = refs[:N_PARTS]
        c_ref, sa_ref, sb_ref, into_ref, o_ref, scr = refs[N_PARTS:]
        for g, dil in enumerate(DILATIONS):
            rows = tm // dil
            for r in range(dil):
                rs = _residue(r, rows, dil)
                cos_t, sin_a, sin_b = c_ref[rs, :], -sa_ref[rs, :], -sb_ref[rs, :]
                for kind in range(3):
                    part = 3 * kind + g
                    for h in range(HEADS_PER_GROUP):
                        xh = ins[part][r, :, h * HEAD_DIM:(h + 1) * HEAD_DIM]
                        if kind < 2:
                            xh = _rope_head(xh, cos_t, sin_a, sin_b)
                        scr[part * HEADS_PER_GROUP + h, rs, :] = xh
        for j in range(N_HEADS_IN):
            o_ref[:, j * HEAD_DIM:(j + 1) * HEAD_DIM] = scr[j].astype(BF)

    tab = pl.BlockSpec((tm, HEAD_DIM), lambda i: (i, 0))
    i_specs = [pl.BlockSpec((DILATIONS[p % 3], tm // DILATIONS[p % 3], GROUP_W), lambda i: (0, i, 0))
               for p in range(N_PARTS)]
    return pl.pallas_call(
        body, out_shape=jax.ShapeDtypeStruct(into.shape, into.dtype), grid=(s // tm,),
        in_specs=i_specs + [tab] * 3 + [ANY], out_specs=pl.BlockSpec((tm, N_PARTS * GROUP_W), lambda i: (i, 0)),
        scratch_shapes=[pltpu.VMEM((N_HEADS_IN, tm, HEAD_DIM), F32)], input_output_aliases={N_PARTS + 3: 0},
        name="rope_bwd", compiler_params=_params(("parallel",)))(*parts, *tables, into)


def _band_mask(n):
    qi = lax.broadcasted_iota(jnp.int32, (BLK, 2 * BLK), 0)
    ki = lax.broadcasted_iota(jnp.int32, (BLK, 2 * BLK), 1)
    prev = jnp.logical_and(jnp.logical_and(ki < BLK, ki >= qi), n > 0)
    return jnp.logical_or(prev, jnp.logical_and(ki >= BLK, qi >= ki - BLK))


Q_BLOCKS = 2
Q_ROWS = Q_BLOCKS * BLK


STAT_LANES = HEAD_DIM // HEADS_PER_GROUP


def _stat_of(ref, rows, h):
    return ref[rows, h * STAT_LANES:h * STAT_LANES + 1]


def _pack_stats(cols):
    rows = cols[0].shape[0]
    lane = lax.broadcasted_iota(jnp.int32, (rows, HEAD_DIM), 1)
    tile = jnp.broadcast_to(cols[-1], (rows, HEAD_DIM))
    for h in range(HEADS_PER_GROUP - 2, -1, -1):
        tile = jnp.where(lane < (h + 1) * STAT_LANES, cols[h], tile)
    return tile


def _dil_specs(n_steps):
    last = n_steps - 1
    own = pl.BlockSpec((None, Q_ROWS, GROUP_W), lambda r, n: (r, jnp.minimum(n, last), 0))
    before = pl.BlockSpec((None, BLK, GROUP_W), lambda r, n: (r, jnp.maximum(Q_BLOCKS * n - 1, 0), 0))
    stat = pl.BlockSpec((None, Q_ROWS, HEAD_DIM), lambda r, n: (r, jnp.minimum(n, last), 0))
    return own, before, stat


def _dil_fwd(g, q, k, v):
    dil, length, _ = q.shape
    n_steps = length // Q_ROWS

    def body(q_ref, ko_ref, kb_ref, vo_ref, vb_ref, o_ref, lse_ref):
        n = pl.program_id(1)
        lse_cols = [[] for _ in range(Q_BLOCKS)]
        for h in range(GROUP_W // HEAD_DIM):
            sl = slice(h * HEAD_DIM, (h + 1) * HEAD_DIM)
            keys = jnp.concatenate([kb_ref[:, sl], ko_ref[:, sl]], axis=0)
            vals = jnp.concatenate([vb_ref[:, sl], vo_ref[:, sl]], axis=0)
            for j in range(Q_BLOCKS):
                rows, win = slice(j * BLK, (j + 1) * BLK), slice(j * BLK, (j + 2) * BLK)
                sc = lax.dot_general(q_ref[rows, sl], keys[win], NT_DIMS, preferred_element_type=F32) * SCALE
                sc = jnp.where(_band_mask(Q_BLOCKS * n + j), sc, NEG_INF)
                mx = jnp.max(sc, axis=-1, keepdims=True)
                p = jnp.exp(sc - mx)
                den = jnp.sum(p, axis=-1, keepdims=True)
                o_ref[rows, sl] = jnp.dot(p.astype(BF), vals[win], preferred_element_type=F32) / den
                lse_cols[j].append(mx + jnp.log(den))
        for j in range(Q_BLOCKS):
            lse_ref[j * BLK:(j + 1) * BLK, :] = _pack_stats(lse_cols[j])

    own, before, stat = _dil_specs(n_steps)
    return pl.pallas_call(
        body, out_shape=[jax.ShapeDtypeStruct(q.shape, F32), jax.ShapeDtypeStruct((dil, length, HEAD_DIM), F32)],
        grid=(dil, n_steps), in_specs=[own, own, before, own, before], out_specs=[own, stat], name=f"dil_fwd_{g}",
        compiler_params=_params(("parallel", "arbitrary")))(q, k, k, v, v)


def _dil_bwd(g, q, k, v, do, lse, delta):
    dil, length, _ = q.shape
    n_steps = length // Q_ROWS

    def body(q_ref, ko_ref, kb_ref, vo_ref, vb_ref, do_ref, lse_ref, dl_ref, dq_ref, dk_ref, dv_ref, ck_ref, cv_ref):
        n = pl.program_id(1)
        live = n < n_steps

        @pl.when(n == 0)
        def _():
            ck_ref[...] = jnp.zeros_like(ck_ref)
            cv_ref[...] = jnp.zeros_like(cv_ref)

        @pl.when(jnp.logical_not(live))
        def _():
            dk_ref[...] = ck_ref[...]
            dv_ref[...] = cv_ref[...]

        @pl.when(live)
        def _():
            for h in range(GROUP_W // HEAD_DIM):
                sl = slice(h * HEAD_DIM, (h + 1) * HEAD_DIM)
                keys = jnp.concatenate([kb_ref[:, sl], ko_ref[:, sl]], axis=0)
                vals = jnp.concatenate([vb_ref[:, sl], vo_ref[:, sl]], axis=0)
                dks, dvs = [], []
                for j in range(Q_BLOCKS):
                    rows, win = slice(j * BLK, (j + 1) * BLK), slice(j * BLK, (j + 2) * BLK)
                    qh, doh = q_ref[rows, sl], do_ref[rows, sl]
                    lse_h, dl_h = _stat_of(lse_ref, rows, h), _stat_of(dl_ref, rows, h)
                    sc = lax.dot_general(qh, keys[win], NT_DIMS, preferred_element_type=F32) * SCALE
                    p = jnp.where(_band_mask(Q_BLOCKS * n + j), jnp.exp(jnp.minimum(sc - lse_h, 0.0)), 0.0)
                    dp = lax.dot_general(doh, vals[win], NT_DIMS, preferred_element_type=F32)
                    ds = (p * (dp - dl_h) * SCALE).astype(BF)
                    dq_ref[rows, sl] = jnp.dot(ds, keys[win], preferred_element_type=F32)
                    dks.append(lax.dot_general(ds, qh, TN_DIMS, preferred_element_type=F32))
                    dvs.append(lax.dot_general(p.astype(BF), doh, TN_DIMS, preferred_element_type=F32))
                for out_ref, carry, parts in ((dk_ref, ck_ref, dks), (dv_ref, cv_ref, dvs)):
                    out_ref[:Q_ROWS - BLK, sl] = carry[:Q_ROWS - BLK, sl]
                    out_ref[Q_ROWS - BLK:, sl] = carry[Q_ROWS - BLK:, sl] + parts[0][:BLK]
                    for j in range(Q_BLOCKS - 1):
                        carry[j * BLK:(j + 1) * BLK, sl] = parts[j][BLK:] + parts[j + 1][:BLK]
                    carry[Q_ROWS - BLK:, sl] = parts[-1][BLK:]

    own, before, stat = _dil_specs(n_steps)
    behind = pl.BlockSpec((None, Q_ROWS, GROUP_W), lambda r, n: (r, jnp.maximum(n - 1, 0), 0))
    return pl.pallas_call(
        body, out_shape=[jax.ShapeDtypeStruct(q.shape, F32)] * 3, grid=(dil, n_steps + 1),
        in_specs=[own, own, before, own, before, own, stat, stat], out_specs=[own, behind, behind],
        scratch_shapes=[pltpu.VMEM((Q_ROWS, GROUP_W), F32)] * 2, name=f"dil_bwd_{g}",
        compiler_params=_params(("parallel", "arbitrary")))(q, k, k, v, v, do, lse, delta)


def _major_specs(s, tm, dtype, width=GROUP_W):
    shapes = [jax.ShapeDtypeStruct((dil, s // dil, width), dtype) for dil in DILATIONS]
    specs = [pl.BlockSpec((dil, tm // dil, width), lambda i: (0, i, 0)) for dil in DILATIONS]
    return shapes, specs


def _attn_merge(outs, lses):
    s = outs[0].shape[1]
    tm = _pick_rows(s, ROPE_TILE)

    def body(o0, o1, o2, l0, l1, l2, m_ref, e0, e1, e2, so1, so2, sl1, sl2, se):
        for dil, src, dst in ((DILATIONS[1], l1, sl1), (DILATIONS[2], l2, sl2)):
            for r in range(dil):
                dst[_residue(r, tm // dil, dil), :] = src[r]
        a, b, c = l0[0], sl1[...], sl2[...]
        mx = jnp.maximum(jnp.maximum(a, b), c)
        ea, eb, ec = jnp.exp(a - mx), jnp.exp(b - mx), jnp.exp(c - mx)
        den = ea + eb + ec
        wa, wb, wc = ea / den, eb / den, ec / den
        se[...] = mx + jnp.log(den)
        for dil, dst in zip(DILATIONS, (e0, e1, e2)):
            for r in range(dil):
                dst[r] = se[_residue(r, tm // dil, dil), :]
        for h in range(HEADS_PER_GROUP):
            sl = slice(h * HEAD_DIM, (h + 1) * HEAD_DIM)
            col = slice(h * STAT_LANES, h * STAT_LANES + 1)
            for dil, src, dst in ((DILATIONS[1], o1, so1), (DILATIONS[2], o2, so2)):
                for r in range(dil):
                    dst[h, _residue(r, tm // dil, dil), :] = src[r, :, sl]
            m_ref[:, sl] = (wa[:, col] * o0[0, :, sl] + wb[:, col] * so1[h] + wc[:, col] * so2[h]).astype(BF)

    shapes, specs = _major_specs(s, tm, F32)
    stat_shapes, stat_specs = _major_specs(s, tm, F32, HEAD_DIM)
    nat = pl.BlockSpec((tm, GROUP_W), lambda i: (i, 0))
    res = pl.pallas_call(
        body, out_shape=[jax.ShapeDtypeStruct((s, GROUP_W + MEM_W), BF)] + stat_shapes, grid=(s // tm,),
        in_specs=specs + stat_specs, out_specs=[nat] + stat_specs,
        scratch_shapes=[pltpu.VMEM((HEADS_PER_GROUP, tm, HEAD_DIM), F32)] * 2 + [pltpu.VMEM((tm, HEAD_DIM), F32)] * 3,
        name="attn_merge", compiler_params=_params(("parallel",)))(*outs, *lses)
    return res[0], res[1:]


def _attn_delta(dcat, merged, after=()):
    s = merged.shape[0]
    tm = _pick_rows(s, ROPE_TILE)

    def body(*refs):
        d_refs, m_ref = refs[:HEADS_PER_GROUP], refs[HEADS_PER_GROUP]
        do_refs, dl_refs, scr = refs[-7:-4], refs[-4:-1], refs[-1]
        sums = []
        for h in range(HEADS_PER_GROUP):
            sl = slice(h * HEAD_DIM, (h + 1) * HEAD_DIM)
            sums.append(jnp.sum(d_refs[h][...] * m_ref[:, sl].astype(F32), axis=-1, keepdims=True))
            for dil, do_ref in zip(DILATIONS, do_refs):
                for r in range(dil):
                    do_ref[r, :, sl] = d_refs[h][_residue(r, tm // dil, dil), :].astype(BF)
        scr[...] = _pack_stats(sums)
        for dil, dl_ref in zip(DILATIONS, dl_refs):
            for r in range(dil):
                dl_ref[r] = scr[_residue(r, tm // dil, dil), :]

    nat = pl.BlockSpec((tm, GROUP_W), lambda i: (i, 0))
    head_specs = [pl.BlockSpec((tm, HEAD_DIM), lambda i, h=h: (i, h)) for h in range(HEADS_PER_GROUP)]
    bf_shapes, specs = _major_specs(s, tm, BF)
    stat_shapes, stat_specs = _major_specs(s, tm, F32, HEAD_DIM)
    res = pl.pallas_call(
        body, out_shape=bf_shapes + stat_shapes, grid=(s // tm,), in_specs=head_specs + [nat] + [ANY] * len(after),
        out_specs=specs + stat_specs, scratch_shapes=[pltpu.VMEM((tm, HEAD_DIM), F32)], name="attn_delta",
        compiler_params=_params(("parallel",)))(*([dcat] * HEADS_PER_GROUP), merged, *after)
    return res[:3], res[3:]


def _mem_probs(qh, kh):
    sc = lax.dot_general(qh, kh, NT_DIMS, preferred_element_type=F32) * SCALE
    p = jnp.exp(sc - jnp.max(sc, axis=-1, keepdims=True))
    return p, jnp.sum(p, axis=-1, keepdims=True)


def _mem_fwd(name, proj, q_block, kv, into, out_block):
    s = proj.shape[0]
    tq = _pick_rows(s, 512)

    def body(q_ref, kv_ref, into_ref, o_ref):
        for h in range(MEM_HEADS):
            sl = slice(h * HEAD_DIM, (h + 1) * HEAD_DIM)
            vsl = slice(MEM_W + h * HEAD_DIM, MEM_W + (h + 1) * HEAD_DIM)
            p, den = _mem_probs(q_ref[:, sl].astype(BF), kv_ref[:, sl].astype(BF))
            out = jnp.dot(p.astype(BF), kv_ref[:, vsl].astype(BF), preferred_element_type=F32) / den
            o_ref[:, sl] = out.astype(o_ref.dtype)

    return pl.pallas_call(
        body, out_shape=jax.ShapeDtypeStruct(into.shape, into.dtype), grid=(s // tq,),
        in_specs=[pl.BlockSpec((tq, MEM_W), lambda i: (i, q_block)), pl.BlockSpec(kv.shape, lambda i: (0, 0)), ANY],
        out_specs=pl.BlockSpec((tq, MEM_W), lambda i: (i, out_block)), input_output_aliases={2: 0}, name=name,
        compiler_params=_params(("parallel",)))(proj, kv, into)


def _mem_bwd(name, proj, q_block, kv, dcat, d_block, width):
    s = proj.shape[0]
    tq = _pick_rows(s, 512)

    def body(q_ref, kv_ref, do_ref, dq_ref, dkv_ref):
        @pl.when(pl.program_id(0) == 0)
        def _():
            dkv_ref[...] = jnp.zeros_like(dkv_ref)

        for h in range(MEM_HEADS):
            sl = slice(h * HEAD_DIM, (h + 1) * HEAD_DIM)
            vsl = slice(MEM_W + h * HEAD_DIM, MEM_W + (h + 1) * HEAD_DIM)
            qh, kh, vh = q_ref[:, sl].astype(BF), kv_ref[:, sl].astype(BF), kv_ref[:, vsl].astype(BF)
            doh = do_ref[:, sl].astype(BF)
            p, den = _mem_probs(qh, kh)
            p = p / den
            dp = lax.dot_general(doh, vh, NT_DIMS, preferred_element_type=F32)
            ds = (p * (dp - jnp.sum(p * dp, axis=-1, keepdims=True)) * SCALE).astype(BF)
            dq_ref[:, sl] = jnp.dot(ds, kh, preferred_element_type=F32).astype(BF)
            dkv_ref[:, sl] += lax.dot_general(ds, qh, TN_DIMS, preferred_element_type=F32)
            dkv_ref[:, vsl] += lax.dot_general(p.astype(BF), doh, TN_DIMS, preferred_element_type=F32)

    whole = pl.BlockSpec(kv.shape, lambda i: (0, 0))
    return pl.pallas_call(
        body, out_shape=[jax.ShapeDtypeStruct((s, width), BF), jax.ShapeDtypeStruct(kv.shape, F32)], grid=(s // tq,),
        in_specs=[pl.BlockSpec((tq, MEM_W), lambda i: (i, q_block)), whole,
                  pl.BlockSpec((tq, MEM_W), lambda i: (i, d_block))],
        out_specs=[pl.BlockSpec((tq, MEM_W), lambda i: (i, width // MEM_W - 1)), whole], name=name,
        compiler_params=_params(("arbitrary",)))(proj, kv, dcat)


def _causal():
    t = lax.broadcasted_iota(jnp.int32, (BLK, BLK), 0)
    s = lax.broadcasted_iota(jnp.int32, (BLK, BLK), 1)
    return t >= s


def _sgu_norm(vg, ln_g, ln_b):
    mu = jnp.mean(vg, axis=-1, keepdims=True)
    cen = vg - mu
    rstd = lax.rsqrt(jnp.mean(cen * cen, axis=-1, keepdims=True) + LN_EPS)
    xhat = cen * rstd
    return xhat, rstd, xhat * ln_g + ln_b


def _sgu_fwd(proj, ln_g, ln_b, w_sp, b_t):
    s = proj.shape[0]

    def body(u_ref, v_ref, g_ref, b_ref, w_ref, bt_ref, o_ref):
        _, _, vn = _sgu_norm(_gelu(v_ref[...].astype(F32)), g_ref[...], b_ref[...])
        vn = vn.astype(BF)
        tri = _causal()
        for grp in range(SGU_GROUPS):
            sl = slice(grp * HEAD_DIM, (grp + 1) * HEAD_DIM)
            w = jnp.where(tri, w_ref[grp], 0.0).astype(BF)
            mixed = jnp.dot(w, vn[:, sl], preferred_element_type=F32) + bt_ref[:, grp:grp + 1]
            o_ref[:, sl] = (_gelu(u_ref[:, sl].astype(F32)) * mixed).astype(BF)

    vec = pl.BlockSpec((1, SGU_W), lambda i: (0, 0))
    return pl.pallas_call(
        body, out_shape=jax.ShapeDtypeStruct((s, SGU_W + MEM_W), BF), grid=(s // BLK,),
        in_specs=[pl.BlockSpec((BLK, SGU_W), lambda i: (i, 0)), pl.BlockSpec((BLK, SGU_W), lambda i: (i, 1)), vec, vec,
                  pl.BlockSpec(w_sp.shape, lambda i: (0, 0, 0)), pl.BlockSpec(b_t.shape, lambda i: (0, 0))],
        out_specs=pl.BlockSpec((BLK, SGU_W), lambda i: (i, 0)), name="sgu_fwd",
        compiler_params=_params(("parallel",)))(proj, proj, ln_g, ln_b, w_sp, b_t)


def _sgu_bwd(proj, dcat, ln_g, ln_b, w_sp, b_t, into):
    s = proj.shape[0]

    def body(u_ref, v_ref, d_ref, g_ref, b_ref, w_ref, bt_ref, into_ref, dp_ref, dw_ref, db_ref, dg_ref, dbeta_ref,
             dvn_ref):
        @pl.when(pl.program_id(0) == 0)
        def _():
            dw_ref[...] = jnp.zeros_like(dw_ref)
            db_ref[...] = jnp.zeros_like(db_ref)
            dg_ref[...] = jnp.zeros_like(dg_ref)
            dbeta_ref[...] = jnp.zeros_like(dbeta_ref)

        gain = g_ref[...]
        vg, v_slope = _gelu_both(v_ref[...].astype(F32))
        xhat, rstd, vn = _sgu_norm(vg, gain, b_ref[...])
        vn = vn.astype(BF)
        tri = _causal()
        lane = lax.broadcasted_iota(jnp.int32, (BLK, HEAD_DIM), 1)
        db_acc = jnp.zeros((BLK, HEAD_DIM), F32)
        for grp in range(SGU_GROUPS):
            sl = slice(grp * HEAD_DIM, (grp + 1) * HEAD_DIM)
            w = jnp.where(tri, w_ref[grp], 0.0).astype(BF)
            vn_g = vn[:, sl]
            mixed = jnp.dot(w, vn_g, preferred_element_type=F32) + bt_ref[:, grp:grp + 1]
            u_act, u_slope = _gelu_both(u_ref[:, sl].astype(F32))
            d_out = d_ref[:, sl].astype(F32)
            dp_ref[:, sl] = (d_out * mixed * u_slope).astype(BF)
            dmixed = d_out * u_act
            dm = dmixed.astype(BF)
            dvn_ref[:, sl] = lax.dot_general(w, dm, TN_DIMS, preferred_element_type=F32)
            dw = lax.dot_general(dm, vn_g, NT_DIMS, preferred_element_type=F32)
            dw_ref[grp] += jnp.where(tri, dw, 0.0)
            db_acc += jnp.where(lane == grp, jnp.sum(dmixed, axis=-1, keepdims=True), 0.0)
        db_ref[...] += db_acc
        dvn = dvn_ref[...]
        dg_ref[...] += jnp.sum(dvn * xhat, axis=0, keepdims=True)
        dbeta_ref[...] += jnp.sum(dvn, axis=0, keepdims=True)
        dxh = dvn * gain
        dvg = rstd * (dxh - jnp.mean(dxh, axis=-1, keepdims=True) - xhat * jnp.mean(dxh * xhat, axis=-1, keepdims=True))
        dp_ref[:, SGU_W:] = (dvg * v_slope).astype(BF)

    vec = pl.BlockSpec((1, SGU_W), lambda i: (0, 0))
    row = pl.BlockSpec((BLK, SGU_W), lambda i: (i, 0))
    w_spec = pl.BlockSpec(w_sp.shape, lambda i: (0, 0, 0))
    sq = pl.BlockSpec((BLK, HEAD_DIM), lambda i: (0, 0))
    return pl.pallas_call(
        body,
        out_shape=[jax.ShapeDtypeStruct(into.shape, into.dtype),
                   jax.ShapeDtypeStruct(w_sp.shape, F32), jax.ShapeDtypeStruct((BLK, HEAD_DIM), F32),
                   jax.ShapeDtypeStruct((1, SGU_W), F32), jax.ShapeDtypeStruct((1, SGU_W), F32)],
        grid=(s // BLK,),
        in_specs=[row, pl.BlockSpec((BLK, SGU_W), lambda i: (i, 1)), row, vec, vec, w_spec,
                  pl.BlockSpec(b_t.shape, lambda i: (0, 0)), ANY],
        out_specs=[pl.BlockSpec((BLK, 2 * SGU_W), lambda i: (i, 0)), w_spec, sq, vec, vec],
        scratch_shapes=[pltpu.VMEM((BLK, SGU_W), F32)], input_output_aliases={7: 0}, name="sgu_bwd",
        compiler_params=_params(("arbitrary",)))(proj, proj, dcat, ln_g, ln_b, w_sp, b_t, into)


def _place():
    return lax.axis_index("x"), lax.axis_index("y"), lax.axis_index("c")


def _other_chips(x, y):
    return [(1 - x, y), (x, 1 - y), (1 - x, 1 - y)]


def _peer(x, y, c, mask):
    return (1 - x if mask & 4 else x, 1 - y if mask & 2 else y, 1 - c if mask & 1 else c)


def _in_hbm(a):
    return pltpu.with_memory_space_constraint(a, pltpu.HBM)


def _token_spec():
    return jax.ShapeDtypeStruct((8, LANES), F32), pl.BlockSpec(memory_space=pltpu.VMEM)


def _remote(src, dst, ssem, rsem, to):
    return pltpu.make_async_remote_copy(src_ref=src, dst_ref=dst, send_sem=ssem, recv_sem=rsem, device_id=to,
                                        device_id_type=MESH)


def _place_shard(name, src, layer, place, dtype, after=()):
    _, rows, cols = src.shape
    tr = _pick_rows(rows, 512)

    def body(p_ref, s_ref, *rest):
        rest[-1][...] = s_ref[...].astype(dtype)

    grid_spec = pltpu.PrefetchScalarGridSpec(
        num_scalar_prefetch=1, grid=(rows // tr,),
        in_specs=[pl.BlockSpec((None, tr, cols), lambda i, p: (layer, i, 0))] + [ANY] * len(after),
        out_specs=pl.BlockSpec((None, tr, cols), lambda i, p: (p[1], i, 0)))
    return pl.pallas_call(body, out_shape=jax.ShapeDtypeStruct((N_CHIPS, rows, cols), dtype), grid_spec=grid_spec,
                          name=name, compiler_params=_params(("parallel",)))(place, src, *after)


def _gather_copies(bufs, ssem, rsem):
    x, y, c = _place()
    me = 2 * x + y
    copies = []
    for ai, buf in enumerate(bufs):
        for k, (ox, oy) in enumerate(_other_chips(x, y)):
            copies.append(_remote(buf.at[me], buf.at[me], ssem.at[3 * ai + k], rsem.at[3 * ai + k], (ox, oy, c)))
    return copies


def _reduce_copies(grads, lands, ssem, rsem):
    x, y, c = _place()
    copies = []
    for a, (gr, land) in enumerate(zip(grads, lands)):
        for mask in range(1, N_DEV):
            px, py, pc = _peer(x, y, c, mask)
            copies.append(_remote(gr.at[pc, 2 * px + py], land.at[mask - 1], ssem.at[7 * a + mask - 1],
                                  rsem.at[7 * a + mask - 1], (px, py, pc)))
    return copies


def _half_copies(totals, ssem, rsem):
    x, y, c = _place()
    return [_remote(t.at[c], t.at[c], ssem.at[a], rsem.at[a], (x, y, 1 - c)) for a, t in enumerate(totals)]


def _gather_start(name, groups):
    flat = [s for grp in groups for s in grp]
    n, ng = len(flat), len(groups)

    def body(*refs):
        ins = refs[:n]
        sems = refs[n:n + 2 * ng]
        token = refs[-1]
        idx = 0
        for gi, grp in enumerate(groups):
            for cp in _gather_copies(ins[idx:idx + len(grp)], sems[2 * gi], sems[2 * gi + 1]):
                cp.start()
            idx += len(grp)
        token[...] = jnp.zeros_like(token)

    tok_shape, tok_spec = _token_spec()
    sem_shapes = []
    for grp in groups:
        sem_shapes += [pltpu.SemaphoreType.DMA((3 * len(grp),))] * 2
    res = pl.pallas_call(
        body, name=name,
        out_shape=(*sem_shapes, *[pltpu.HBM(s.shape, s.dtype) for s in flat], tok_shape),
        in_specs=[HBM] * n, out_specs=(*[SEM] * (2 * ng), *[HBM] * n, tok_spec),
        input_output_aliases={i: 2 * ng + i for i in range(n)},
        compiler_params=pltpu.CompilerParams(has_side_effects=EFFECT))(*[_in_hbm(s) for s in flat])
    out, idx = [], 2 * ng
    for gi, grp in enumerate(groups):
        out.append((res[2 * gi], res[2 * gi + 1], list(res[idx:idx + len(grp)])))
        idx += len(grp)
    return out, res[-1]


def _gather_wait(name, ssem, rsem, slabs, after):
    n = len(slabs)

    def body(*refs):
        for cp in _gather_copies(refs[:n], refs[n], refs[n + 1]):
            cp.wait_send()
            cp.wait_recv()

    return pl.pallas_call(
        body, name=name, out_shape=tuple(pltpu.HBM(s.shape, s.dtype) for s in slabs),
        in_specs=[HBM] * n + [SEM, SEM] + [ANY] * len(after), out_specs=tuple([HBM] * n),
        input_output_aliases={i: i for i in range(n)},
        compiler_params=pltpu.CompilerParams(has_side_effects=EFFECT))(*slabs, ssem, rsem, *after)


def _reduce_start(name, grads):
    n = len(grads)
    lands = [lax.empty((N_DEV - 1, *g.shape[2:]), g.dtype) for g in grads]

    def body(*refs):
        token = refs[-1]
        for cp in _reduce_copies(refs[:n], refs[n:2 * n], refs[2 * n], refs[2 * n + 1]):
            cp.start()
        token[...] = jnp.zeros_like(token)

    tok_shape, tok_spec = _token_spec()
    sems = [pltpu.SemaphoreType.DMA((7 * n,))] * 2
    res = pl.pallas_call(
        body, name=name,
        out_shape=(*sems, *[pltpu.HBM(g.shape, g.dtype) for g in grads], *[pltpu.HBM(l.shape, l.dtype) for l in lands],
                   tok_shape),
        in_specs=[HBM] * (2 * n), out_specs=(SEM, SEM, *[HBM] * (2 * n), tok_spec),
        input_output_aliases={i: 2 + i for i in range(2 * n)},
        compiler_params=pltpu.CompilerParams(has_side_effects=EFFECT))(*[_in_hbm(t) for t in (*grads, *lands)])
    return res[0], res[1], list(res[2:2 + n]), list(res[2 + n:2 + 2 * n]), res[-1]


def _reduce_wait(name, ssem, rsem, grads, lands, after):
    n = len(grads)

    def body(*refs):
        for cp in _reduce_copies(refs[:n], refs[n:2 * n], refs[2 * n], refs[2 * n + 1]):
            cp.wait_send()
            cp.wait_recv()

    res = pl.pallas_call(
        body, name=name, out_shape=tuple(pltpu.HBM(t.shape, t.dtype) for t in (*grads, *lands)),
        in_specs=[HBM] * (2 * n) + [SEM, SEM] + [ANY] * len(after), out_specs=tuple([HBM] * (2 * n)),
        input_output_aliases={i: i for i in range(2 * n)},
        compiler_params=pltpu.CompilerParams(has_side_effects=EFFECT))(*grads, *lands, ssem, rsem, *after)
    return list(res[:n]), list(res[n:])


def _sum_pieces(name, grad, land, place):
    _, _, rows, cols = grad.shape
    tr = _pick_rows(rows, 256)

    def body(p_ref, g_ref, l_ref, o_ref):
        tot = g_ref[...].astype(F32)
        for k in range(N_DEV - 1):
            tot = tot + l_ref[k].astype(F32)
        o_ref[...] = tot

    grid_spec = pltpu.PrefetchScalarGridSpec(
        num_scalar_prefetch=1, grid=(rows // tr,),
        in_specs=[pl.BlockSpec((None, None, tr, cols), lambda i, p: (p[0], p[1], i, 0)),
                  pl.BlockSpec((N_DEV - 1, tr, cols), lambda i, p: (0, i, 0))],
        out_specs=pl.BlockSpec((None, tr, cols), lambda i, p: (p[0], i, 0)))
    return pl.pallas_call(body, out_shape=jax.ShapeDtypeStruct((2, rows, cols), F32), grid_spec=grid_spec, name=name,
                          compiler_params=_params(("parallel",)))(place, grad, land)


def _half_start(name, totals):
    n = len(totals)

    def body(*refs):
        token = refs[-1]
        for cp in _half_copies(refs[:n], refs[n], refs[n + 1]):
            cp.start()
        token[...] = jnp.zeros_like(token)

    tok_shape, tok_spec = _token_spec()
    res = pl.pallas_call(
        body, name=name,
        out_shape=(pltpu.SemaphoreType.DMA((n,)), pltpu.SemaphoreType.DMA((n,)),
                   *[pltpu.HBM(t.shape, t.dtype) for t in totals], tok_shape),
        in_specs=[HBM] * n, out_specs=(SEM, SEM, *[HBM] * n, tok_spec),
        input_output_aliases={i: 2 + i for i in range(n)},
        compiler_params=pltpu.CompilerParams(has_side_effects=EFFECT))(*[_in_hbm(t) for t in totals])
    return res[0], res[1], list(res[2:2 + n]), res[-1]


def _half_wait(name, ssem, rsem, totals, after):
    n = len(totals)

    def body(*refs):
        for cp in _half_copies(refs[:n], refs[n], refs[n + 1]):
            cp.wait_send()
            cp.wait_recv()

    res = pl.pallas_call(
        body, name=name, out_shape=tuple(pltpu.HBM(t.shape, t.dtype) for t in totals),
        in_specs=[HBM] * n + [SEM, SEM] + [ANY] * len(after), out_specs=tuple([HBM] * n),
        input_output_aliases={i: i for i in range(n)},
        compiler_params=pltpu.CompilerParams(has_side_effects=EFFECT))(*totals, ssem, rsem, *after)
    return list(res)


def _small_copies(bufs, ssem, rsem):
    x, y, c = _place()
    mine = bufs[0].at[4 * x + 2 * y + c]
    return [_remote(mine, mine, ssem.at[mask - 1], rsem.at[mask - 1], _peer(x, y, c, mask)) for mask in range(1, N_DEV)]


def _small_start(name, slots):
    def body(s_ref, ssem, rsem, thru, token):
        for cp in _small_copies([s_ref], ssem, rsem):
            cp.start()
        token[...] = jnp.zeros_like(token)

    tok_shape, tok_spec = _token_spec()
    sems = [pltpu.SemaphoreType.DMA((N_DEV - 1,))] * 2
    return pl.pallas_call(
        body, name=name, out_shape=(*sems, pltpu.HBM(slots.shape, slots.dtype), tok_shape), in_specs=[HBM],
        out_specs=(SEM, SEM, HBM, tok_spec), input_output_aliases={0: 2},
        compiler_params=pltpu.CompilerParams(has_side_effects=EFFECT))(_in_hbm(slots))


def _small_wait(name, ssem, rsem, slots, after):
    def body(*refs):
        for cp in _small_copies([refs[0]], refs[1], refs[2]):
            cp.wait_send()
            cp.wait_recv()

    return pl.pallas_call(
        body, name=name, out_shape=pltpu.HBM(slots.shape, slots.dtype), in_specs=[HBM, SEM, SEM] + [ANY] * len(after),
        out_specs=HBM, input_output_aliases={0: 0},
        compiler_params=pltpu.CompilerParams(has_side_effects=EFFECT))(slots, ssem, rsem, *after)


def _own_slot(small, me):
    return lax.dynamic_update_slice(jnp.zeros((N_DEV, *small.shape), small.dtype), small[None], (me, 0, 0))


def _sum_devices(name, stacked):
    _, rows, lanes = stacked.shape
    tr = _pick_rows(rows, 512)

    def body(s_ref, o_ref):
        tot = s_ref[0]
        for k in range(1, N_DEV):
            tot = tot + s_ref[k]
        o_ref[...] = tot

    return pl.pallas_call(
        body, out_shape=jax.ShapeDtypeStruct((rows, lanes), F32), grid=(rows // tr,),
        in_specs=[pl.BlockSpec((N_DEV, tr, lanes), lambda i: (0, i, 0))], out_specs=pl.BlockSpec((tr, lanes), lambda i: (i, 0)),
        name=name, compiler_params=_params(("parallel",)))(stacked)


def _adamw(name, w, g, m, v, layer, prev=None):
    layers, rows, cols = w.shape
    tr = _pick_rows(rows, 256)
    c1 = 1.0 - ADAM_B1 ** ADAM_STEP
    c2 = 1.0 - ADAM_B2 ** ADAM_STEP

    def body(w_ref, g_ref, m_ref, v_ref, *rest):
        go_ref, d_ref, nm_ref, nv_ref = rest[-4:]
        gv = g_ref[...]
        nm = ADAM_B1 * m_ref[...] + (1.0 - ADAM_B1) * gv
        nv = ADAM_B2 * v_ref[...] + (1.0 - ADAM_B2) * (gv * gv)
        go_ref[...] = gv
        d_ref[...] = -ADAM_LR * ((nm / c1) / (jnp.sqrt(nv / c2) + ADAM_EPS) + ADAM_WD * w_ref[...])
        nm_ref[...] = nm
        nv_ref[...] = nv

    spec = pl.BlockSpec((None, tr, cols), lambda i: (layer, i, 0))
    prev = list(prev) if prev is not None else []
    return pl.pallas_call(
        body, out_shape=[jax.ShapeDtypeStruct((layers, rows, cols), F32)] * 4, grid=(rows // tr,),
        in_specs=[spec, pl.BlockSpec((tr, cols), lambda i: (i, 0)), spec, spec] + [ANY] * len(prev),
        out_specs=[spec] * 4, input_output_aliases={4 + i: i for i in range(len(prev))}, name=name,
        compiler_params=_params(("parallel",)))(w, g, m, v, *prev)


def _pack(vectors, pad_rows):
    flat = jnp.concatenate([t.reshape(-1) for t in vectors])
    rows = -(-flat.shape[0] // LANES)
    rows = -(-rows // pad_rows) * pad_rows
    return jnp.pad(flat, (0, rows * LANES - flat.shape[0])).reshape(rows, LANES)


def _unpack(packed, shapes):
    flat = packed.reshape(-1)
    out, off = [], 0
    for shp in shapes:
        size = math.prod(shp)
        out.append(flat[off:off + size].reshape(shp))
        off += size
    return out


def kernel(x, mem, positions, mix_norm, mem_norm, w_mem_kv, ffn_norm, w_gate, w_up, w_down, attn_w_in, attn_w_out, sgu_w_in, sgu_ln_g, sgu_ln_b, sgu_w_spatial, sgu_b_spatial, sgu_w_out, final_norm, loss_target, m_mix_norm, m_mem_norm, m_w_mem_kv, m_ffn_norm, m_w_gate, m_w_up, m_w_down, m_attn_w_in, m_attn_w_out, m_sgu_w_in, m_sgu_ln_g, m_sgu_ln_b, m_sgu_w_spatial, m_sgu_b_spatial, m_sgu_w_out, m_final_norm, v_mix_norm, v_mem_norm, v_w_mem_kv, v_ffn_norm, v_w_gate, v_w_up, v_w_down, v_attn_w_in, v_attn_w_out, v_sgu_w_in, v_sgu_ln_g, v_sgu_ln_b, v_sgu_w_spatial, v_sgu_b_spatial, v_sgu_w_out, v_final_norm):
    d_model = x.shape[2]
    x0, mem0, tgt = x[0], mem[0], loss_target[0]
    xi, yi, ci = _place()
    chip = 2 * xi + yi
    place = jnp.stack([ci, chip]).astype(jnp.int32)

    given_w = dict(mix_norm=mix_norm, mem_norm=mem_norm, w_mem_kv=w_mem_kv, ffn_norm=ffn_norm, w_gate=w_gate, w_up=w_up,
                   w_down=w_down, attn_w_in=attn_w_in, attn_w_out=attn_w_out, sgu_w_in=sgu_w_in, sgu_ln_g=sgu_ln_g,
                   sgu_ln_b=sgu_ln_b, sgu_w_spatial=sgu_w_spatial, sgu_b_spatial=sgu_b_spatial, sgu_w_out=sgu_w_out,
                   final_norm=final_norm)
    given_m = dict(mix_norm=m_mix_norm, mem_norm=m_mem_norm, w_mem_kv=m_w_mem_kv, ffn_norm=m_ffn_norm, w_gate=m_w_gate,
                   w_up=m_w_up, w_down=m_w_down, attn_w_in=m_attn_w_in, attn_w_out=m_attn_w_out, sgu_w_in=m_sgu_w_in,
                   sgu_ln_g=m_sgu_ln_g, sgu_ln_b=m_sgu_ln_b, sgu_w_spatial=m_sgu_w_spatial,
                   sgu_b_spatial=m_sgu_b_spatial, sgu_w_out=m_sgu_w_out, final_norm=m_final_norm)
    given_v = dict(mix_norm=v_mix_norm, mem_norm=v_mem_norm, w_mem_kv=v_w_mem_kv, ffn_norm=v_ffn_norm, w_gate=v_w_gate,
                   w_up=v_w_up, w_down=v_w_down, attn_w_in=v_attn_w_in, attn_w_out=v_attn_w_out, sgu_w_in=v_sgu_w_in,
                   sgu_ln_g=v_sgu_ln_g, sgu_ln_b=v_sgu_ln_b, sgu_w_spatial=v_sgu_w_spatial,
                   sgu_b_spatial=v_sgu_b_spatial, sgu_w_out=v_sgu_w_out, final_norm=v_final_norm)

    units = {"attn_w_in": ("attn_w_in", 0, "col"), "w_mem_kv0": ("w_mem_kv", 0, "row"), "attn_w_out": ("attn_w_out", 0, "col"),
             "w_gate0": ("w_gate", 0, "col"), "w_up0": ("w_up", 0, "col"), "w_down0": ("w_down", 0, "row"),
             "sgu_w_in": ("sgu_w_in", 0, "col"), "w_mem_kv1": ("w_mem_kv", 1, "row"), "sgu_w_out": ("sgu_w_out", 0, "row"),
             "w_gate1": ("w_gate", 1, "col"), "w_up1": ("w_up", 1, "col"), "w_down1": ("w_down", 1, "row")}
    gather_groups = [["attn_w_in"], ["w_mem_kv0", "attn_w_out"], ["w_gate0", "w_up0"],
                     ["w_down0", "sgu_w_in", "w_mem_kv1", "ln"], ["sgu_w_out", "w_gate1", "w_up1"], ["w_down1"]]

    first = _place_shard("place_attn_w_in", attn_w_in, 0, place, BF)
    in_flight, token = _gather_start("gather_start_0", [[first]])
    slabs = {u: _place_shard(f"place_{u}", given_w[arr], layer, place, BF, after=[token])
             for u, (arr, layer, _) in units.items() if u != "attn_w_in"}
    slabs["ln"] = _place_shard("place_ln", jnp.concatenate([sgu_ln_g, sgu_ln_b])[None], 0, place, F32, after=[token])
    rest, token = _gather_start("gather_start_1", [[slabs[u] for u in grp] for grp in gather_groups[1:]])
    in_flight += rest
    weights = {}

    def arrive(gi, after):
        ssem, rsem, arrs = in_flight[gi]
        for u, full in zip(gather_groups[gi], _gather_wait(f"gather_wait_{gi}", ssem, rsem, arrs, after)):
            weights[u] = full if u == "ln" else Weight(full, units[u][2])

    w_sp = sgu_w_spatial[0]
    b_t = sgu_b_spatial[0].T
    tables = _rope_tables(positions[0])

    def residual(acc, extra):
        return [extra[0] + acc[0]]

    def memory_kv(layer):
        mem_n = _rms_fwd(f"mem_norm_{layer}", mem0, mem_norm[layer:layer + 1])
        return mem_n, _mm_nn(f"mem_kv_{layer}", mem_n, weights[f"w_mem_kv{layer}"])[0]

    h0 = _rms_fwd("mix_norm_0", x0, mix_norm[0:1], after=[token])
    arrive(0, [h0])
    proj0 = _mm_nn("attn_in", h0, weights["attn_w_in"])[0]
    arrive(1, [proj0])
    qkv = _rope_fwd(proj0, tables)
    qs, ks, vs = qkv[0:3], qkv[3:6], qkv[6:9]
    outs, lses = [], []
    for g in range(len(DILATIONS)):
        o, l = _dil_fwd(g, qs[g], ks[g], vs[g])
        outs.append(o)
        lses.append(l)
    merged, lse = _attn_merge(outs, lses)
    mem_n0, kv0 = memory_kv(0)
    cat0 = _mem_fwd("mem_fwd_0", proj0, 9, kv0, merged, 1)
    x1, hf0 = _mm_nn("attn_out", cat0, weights["attn_w_out"], extras=[x0], epilogue=residual, norm_gain=ffn_norm[0:1])
    arrive(2, [x1])
    g0, u0, act0 = _gate_up("gate_up_0", hf0, weights["w_gate0"], weights["w_up0"])
    arrive(3, [act0])
    x2 = _mm_nn("down_0", act0, weights["w_down0"], extras=[x1], epilogue=residual)[0]

    ln_all = weights["ln"]
    ln_g = ln_all[:, 0, :].reshape(1, SGU_W)
    ln_b = ln_all[:, 1, :].reshape(1, SGU_W)
    h1 = _rms_fwd("mix_norm_1", x2, mix_norm[1:2])
    proj1 = _mm_nn("sgu_in", h1, weights["sgu_w_in"], out_dtypes=(BF,))[0]
    arrive(4, [proj1])
    sgu_out = _sgu_fwd(proj1, ln_g, ln_b, w_sp, b_t)
    mem_n1, kv1 = memory_kv(1)
    cat1 = _mem_fwd("mem_fwd_1", proj1, 6, kv1, sgu_out, 3)
    x3, hf1 = _mm_nn("sgu_out", cat1, weights["sgu_w_out"], extras=[x2], epilogue=residual, norm_gain=ffn_norm[1:2])
    g1, u1, act1 = _gate_up("gate_up_1", hf1, weights["w_gate1"], weights["w_up1"])
    arrive(5, [act1])
    x4 = _mm_nn("down_1", act1, weights["w_down1"], extras=[x3], epilogue=residual)[0]

    d4, d4_op, g_final, loss_part = _final_loss(x4, tgt, final_norm.reshape(1, d_model))
    loss = lax.psum(loss_part[0, 0], ("x", "y", "c"))

    outputs = {}

    def start_reduce(tag, names, grads):
        ssem, rsem, grads, lands, tok = _reduce_start(f"reduce_start_{tag}", grads)
        return dict(tag=tag, names=names, ssem=ssem, rsem=rsem, grads=grads, lands=lands), tok

    def finish_reduce(st, after):
        grads, lands = _reduce_wait(f"reduce_wait_{st['tag']}", st["ssem"], st["rsem"], st["grads"], st["lands"], after)
        totals = [_sum_pieces(f"sum_{u}", g, l, place) for u, g, l in zip(st["names"], grads, lands)]
        ssem, rsem, totals, tok = _half_start(f"half_start_{st['tag']}", totals)
        return dict(tag=st["tag"], names=st["names"], ssem=ssem, rsem=rsem, totals=totals), tok

    def finish_update(st, after):
        totals = _half_wait(f"half_wait_{st['tag']}", st["ssem"], st["rsem"], st["totals"], after)
        for u, tot in zip(st["names"], totals):
            arr, layer, _ = units[u]
            w = given_w[arr]
            outputs[arr] = _adamw(f"adamw_{u}", w, tot.reshape(w.shape[1:]), given_m[arr], given_v[arr], layer,
                                  outputs.get(arr))

    def ffn_bwd(layer, d_out, d_out_op, xin, h, g, u, act):
        wd, wg, wu = weights[f"w_down{layer}"], weights[f"w_gate{layer}"], weights[f"w_up{layer}"]
        gr_down = _mm_tn(f"d_down_{layer}", act, d_out_op, wd)
        dg, du = _mm_nt(f"d_act_{layer}", [d_out_op], [wd], out_dtypes=(BF, BF), extras=[g, u],
                        epilogue=_swiglu_bwd_epilogue, col_chunk=EPILOGUE_CHUNK)
        gr_gate = _mm_tn(f"d_gate_{layer}", h, dg, wg)
        gr_up = _mm_tn(f"d_up_{layer}", h, du, wu)
        st, tok = start_reduce(f"ffn{layer}", [f"w_down{layer}", f"w_gate{layer}", f"w_up{layer}"], [gr_down, gr_gate, gr_up])
        dh = _mm_nt(f"d_ffn_h_{layer}", [dg, du], [wg, wu], out_dtypes=(BF,), after=[tok])[0]
        d_in, d_in_op, g_norm = _rms_bwd(f"ffn_norm_bwd_{layer}", xin, ffn_norm[layer:layer + 1], dh, d_out)
        return st, d_in, d_in_op, g_norm

    def memory_bwd(layer, mem_n, dkv):
        dkv = dkv.astype(BF)
        wkv = weights[f"w_mem_kv{layer}"]
        gr = _mm_tn(f"d_mem_kv_{layer}", mem_n, dkv, wkv)
        d_mem_n = _mm_nt(f"d_mem_n_{layer}", [dkv], [wkv])[0]
        return gr, _rms_bwd(f"mem_norm_bwd_{layer}", mem0, mem_norm[layer:layer + 1], d_mem_n)[2]

    st_ffn1, d3, d3_op, g_ffn1 = ffn_bwd(1, d4, d4_op, x3, hf1, g1, u1, act1)
    gr_sgu_out = _mm_tn("d_sgu_out", cat1, d3_op, weights["sgu_w_out"])
    dcat1 = _mm_nt("d_cat_1", [d3_op], [weights["sgu_w_out"]], out_dtypes=(BF,))[0]
    st_ffn1, tok = finish_reduce(st_ffn1, [dcat1])
    dproj1, dkv1 = _mem_bwd("mem_bwd_1", proj1, 6, kv1, dcat1, 3, proj1.shape[1])
    gr_kv1, g_mem1 = memory_bwd(1, mem_n1, dkv1)
    dproj1, g_wsp, g_bsp_t, g_ln_g, g_ln_b = _sgu_bwd(proj1, dcat1, ln_g, ln_b, w_sp, b_t, dproj1)
    gr_sgu_in = _mm_tn("d_sgu_in", h1, dproj1, weights["sgu_w_in"], after=[tok])
    finish_update(st_ffn1, [gr_sgu_in])
    st_mix1, tok = start_reduce("mix1", ["sgu_w_out", "w_mem_kv1", "sgu_w_in"], [gr_sgu_out, gr_kv1, gr_sgu_in])
    dh1 = _mm_nt("d_h_1", [dproj1], [weights["sgu_w_in"]], out_dtypes=(BF,), after=[tok])[0]
    d2, d2_op, g_mix1 = _rms_bwd("mix_norm_bwd_1", x2, mix_norm[1:2], dh1, d3)

    st_ffn0, d1, d1_op, g_ffn0 = ffn_bwd(0, d2, d2_op, x1, hf0, g0, u0, act0)
    dev = 4 * xi + 2 * yi + ci
    small_a = [g_mix1, g_mem1, jnp.concatenate([g_ffn0, g_ffn1]), g_wsp, g_bsp_t[:, :SGU_GROUPS].T, g_final, g_ln_g, g_ln_b]
    sa_ssem, sa_rsem, sa_slots, tok = _small_start("small_start_a", _own_slot(_pack(small_a, LANES), dev))
    gr_attn_out = _mm_tn("d_attn_out", cat0, d1_op, weights["attn_w_out"], after=[tok])
    st_mix1, tok = finish_reduce(st_mix1, [gr_attn_out])
    dcat0 = _mm_nt("d_cat_0", [d1_op], [weights["attn_w_out"]], after=[tok])[0]
    dproj0, dkv0 = _mem_bwd("mem_bwd_0", proj0, 9, kv0, dcat0, 1, proj0.shape[1])
    finish_update(st_mix1, [dkv0])
    gr_kv0, g_mem0 = memory_bwd(0, mem_n0, dkv0)
    st_ffn0, tok = finish_reduce(st_ffn0, [g_mem0])
    d_merged, delta = _attn_delta(dcat0, cat0, after=[tok])
    dqs, dks, dvs = [], [], []
    for g in range(len(DILATIONS)):
        dq, dk, dv = _dil_bwd(g, qs[g], ks[g], vs[g], d_merged[g], lse[g], delta[g])
        dqs.append(dq)
        dks.append(dk)
        dvs.append(dv)
    dproj0 = _rope_bwd(dqs + dks + dvs, tables, dproj0)
    finish_update(st_ffn0, [dproj0])
    gr_attn_in = _mm_tn("d_attn_in", h0, dproj0, weights["attn_w_in"])
    st_mix0, tok = start_reduce("mix0", ["attn_w_out", "w_mem_kv0", "attn_w_in"], [gr_attn_out, gr_kv0, gr_attn_in])
    dh0 = _mm_nt("d_h_0", [dproj0], [weights["attn_w_in"]], out_dtypes=(BF,), after=[tok])[0]
    d0, _, g_mix0 = _rms_bwd("mix_norm_bwd_0", x0, mix_norm[0:1], dh0, d1)

    small_b = [g_mix0, g_mem0]
    sb_ssem, sb_rsem, sb_slots, tok = _small_start("small_start_b", _own_slot(_pack(small_b, 8), dev))
    sa_slots = _small_wait("small_wait_a", sa_ssem, sa_rsem, sa_slots, [tok])
    g_mix1, g_mem1, g_ffn, g_wsp, g_bsp, g_final, g_ln_g, g_ln_b = _unpack(_sum_devices("small_sum_a", sa_slots),
                                                                           [t.shape for t in small_a])
    sb_slots = _small_wait("small_wait_b", sb_ssem, sb_rsem, sb_slots, [g_final])
    g_mix0, g_mem0 = _unpack(_sum_devices("small_sum_b", sb_slots), [t.shape for t in small_b])
    st_mix0, tok = finish_reduce(st_mix0, [g_mix0])
    g_mix, g_mem = jnp.concatenate([g_mix0, g_mix1]), jnp.concatenate([g_mem0, g_mem1])
    shard_w = sgu_ln_g.shape[-1]
    g_ln_g = lax.dynamic_slice_in_dim(g_ln_g, chip * shard_w, shard_w, axis=1)
    g_ln_b = lax.dynamic_slice_in_dim(g_ln_b, chip * shard_w, shard_w, axis=1)
    small_names = ["mix_norm", "mem_norm", "ffn_norm", "sgu_w_spatial", "sgu_b_spatial", "final_norm", "sgu_ln_g",
                   "sgu_ln_b"]
    small_g = [g_mix, g_mem, g_ffn, g_wsp, g_bsp, g_final, g_ln_g, g_ln_b]
    small_shapes = [given_w[k].shape for k in small_names]
    packed = [_pack(t, LANES) for t in ([given_w[k] for k in small_names], small_g, [given_m[k] for k in small_names],
                                    [given_v[k] for k in small_names])]
    small_out = _adamw("adamw_small", packed[0][None], packed[1], packed[2][None], packed[3][None], 0)
    finish_update(st_mix0, [small_out[0]])
    for k, gk, dk, mk, vk in zip(small_names, *[_unpack(t[0], small_shapes) for t in small_out]):
        outputs[k] = (gk, dk, mk, vk)

    order = ["mix_norm", "mem_norm", "w_mem_kv", "ffn_norm", "w_gate", "w_up", "w_down", "attn_w_in", "attn_w_out",
             "sgu_w_in", "sgu_ln_g", "sgu_ln_b", "sgu_w_spatial", "sgu_b_spatial", "sgu_w_out", "final_norm"]
    return (loss, d0[None], *[outputs[k][0] for k in order], *[outputs[k][1] for k in order],
            *[outputs[k][2] for k in order], *[outputs[k][3] for k in order])
```

```python
import math

import jax
import jax.numpy as jnp
from jax import lax
from jax.experimental import pallas as pl
from jax.experimental.pallas import tpu as pltpu

F32 = jnp.float32
BF = jnp.bfloat16
MESH = pl.DeviceIdType.MESH

HEAD_DIM = 128
MEM_HEADS = 4
MEM_W = MEM_HEADS * HEAD_DIM
GROUP_W = 4 * HEAD_DIM
DILATIONS = (1, 4, 16)
BLK = 128
SGU_GROUPS = 12
SGU_W = SGU_GROUPS * HEAD_DIM
ROT_HALF = 16
ROPE_THETA = 500000.0
NORM_EPS = 1e-6
LN_EPS = 1e-5
NEG_INF = -1e30
SCALE = HEAD_DIM ** -0.5
ADAM_LR, ADAM_B1, ADAM_B2, ADAM_EPS, ADAM_WD, ADAM_STEP = 0.001, 0.9, 0.999, 1e-08, 0.01, 10

VMEM_LIMIT = 48 * 2 ** 20
VMEM_TILE_BUDGET = 38 * 2 ** 20
N_CHIPS = 4
N_DEV = 8
LANES = 128
EPILOGUE_CHUNK = 256

NT_DIMS = (((1,), (1,)), ((), ()))
TN_DIMS = (((0,), (0,)), ((), ()))
NN_DIMS = (((1,), (0,)), ((), ()))

ANY = pl.BlockSpec(memory_space=pl.ANY)
HBM = pl.BlockSpec(memory_space=pltpu.HBM)
SEM = pl.BlockSpec(memory_space=pltpu.SEMAPHORE)
EFFECT = pltpu.SideEffectType.DATAFLOW_SIDE_EFFECTING


def _params(sem):
    return pltpu.CompilerParams(dimension_semantics=sem, vmem_limit_bytes=VMEM_LIMIT)


def _pick(n, cap):
    if n <= cap:
        return n
    best = None
    for t in range(LANES, cap + 1, LANES):
        if n % t == 0:
            best = t
    assert best is not None, (n, cap)
    return best


def _pick_rows(n, cap):
    t = min(n, cap)
    while n % t:
        t //= 2
    return t


def _mm(name, dims, a_list, a_specs, b_list, b_specs, pairs, n_acc, acc_shape, grid, extras, e_specs,
        out_shapes, out_specs, epilogue, after=(), col_chunk=None, store=None, shard_width=None, norm_gain=None):
    na, nb, ne, no = len(a_list), len(b_list), len(extras), len(out_shapes)
    nk = grid[-1]
    ng = 0 if norm_gain is None else 1

    def products(a, b, cols=None):
        sums = [None] * n_acc
        for ai, bi, ci in pairs:
            bv = b[bi]
            if cols is None:
                bv = bv[...]
            elif dims == NT_DIMS:
                bv = bv[cols, :]
            else:
                bv = bv[:, cols]
            if bv.ndim == 3:
                bv = bv.reshape(-1, bv.shape[-1])
            prod = lax.dot_general(a[ai][...].astype(BF), bv.astype(BF), dims, preferred_element_type=F32)
            sums[ci] = prod if sums[ci] is None else sums[ci] + prod
        return sums

    def body(*refs):
        a = refs[:na]
        b = refs[na:na + nb]
        e = refs[na + nb:na + nb + ne]
        off = na + nb + ne + ng + len(after)
        o = refs[off:off + no]
        acc = refs[off + no:]

        def normed():
            if ng:
                xf = o[0][...]
                r = lax.rsqrt(jnp.mean(xf * xf, axis=-1, keepdims=True) + NORM_EPS)
                o[-1][...] = (xf * r * refs[na + nb + ne][...]).astype(o[-1].dtype)

        def finish(sums):
            outs = epilogue(sums, [r[...] for r in e])
            if store is not None:
                store(o, outs)
                return
            for r, v in zip(o, outs):
                r[...] = v.astype(r.dtype)
            normed()

        if nk == 1 and shard_width:
            (ai, bi, _), = pairs
            av = a[ai][...].astype(BF)
            if dims == NT_DIMS:
                total = None
                for j in range(N_CHIPS):
                    cols = slice(j * shard_width, (j + 1) * shard_width)
                    prod = lax.dot_general(av[:, cols], b[bi][j].astype(BF), dims, preferred_element_type=F32)
                    total = prod if total is None else total + prod
                finish([total])
                return
            for j in range(N_CHIPS):
                cols = slice(j * shard_width, (j + 1) * shard_width)
                prod = lax.dot_general(av, b[bi][j].astype(BF), dims, preferred_element_type=F32)
                outs = epilogue([prod], [r[:, cols] for r in e])
                for r, v in zip(o, outs):
                    r[:, cols] = v.astype(r.dtype)
            normed()
            return
        if nk == 1 and col_chunk:
            width = acc_shape[1]
            left = [r[...].astype(BF) for r in a]
            for c0 in range(0, width, col_chunk):
                cols = slice(c0, min(c0 + col_chunk, width))
                outs = epilogue(products(left, b, cols), [r[:, cols] for r in e])
                for r, v in zip(o, outs):
                    r[:, cols] = v.astype(r.dtype)
            return
        if nk == 1:
            finish(products(a, b))
            return
        k = pl.program_id(len(grid) - 1)

        @pl.when(k == 0)
        def _():
            for c, v in zip(acc, products(a, b)):
                c[...] = v

        @pl.when(jnp.logical_and(k > 0, k < nk - 1))
        def _():
            for c, v in zip(acc, products(a, b)):
                c[...] += v

        @pl.when(k == nk - 1)
        def _():
            finish([c[...] + v for c, v in zip(acc, products(a, b))])

    gains = [] if norm_gain is None else [norm_gain]
    ins = [*a_list, *b_list, *extras, *gains, *after]
    in_specs = [*a_specs, *b_specs, *e_specs, *[pl.BlockSpec(g.shape, lambda *_: (0, 0)) for g in gains],
                *([ANY] * len(after))]
    sem = ("parallel",) * (len(grid) - 1) + ("arbitrary",)
    scratch = [] if nk == 1 else [pltpu.VMEM(acc_shape, F32)] * n_acc
    return pl.pallas_call(
        body, out_shape=out_shapes, grid=grid, in_specs=in_specs, out_specs=out_specs, scratch_shapes=scratch,
        name=name, compiler_params=_params(sem))(*ins)


def _tile_bytes(blocks, single=()):
    size = lambda s, d: math.prod(s) * jnp.dtype(d).itemsize
    return sum(2 * size(s, d) for s, d in blocks) + sum(size(s, d) for s, d in single)


def _first(acc, extra):
    return [acc[0]]


def _sigmoid(x):
    return 0.5 * (1.0 + jnp.tanh(0.5 * x))


class Weight:
    def __init__(self, arr, axis):
        self.arr, self.axis = arr, axis
        _, self.rows, self.cols = arr.shape


SMALL_WEIGHT_BYTES = 8 * 2 ** 20


def _is_small(w):
    return w.arr.size * w.arr.dtype.itemsize <= SMALL_WEIGHT_BYTES


def _mm_nn(name, a, w, extras=(), epilogue=_first, out_dtypes=(F32,), after=(), norm_gain=None):
    m, kdim = a.shape
    b_spec, shard_width = None, None
    weight_buffers = 2
    if norm_gain is not None:
        out_dtypes = (*out_dtypes, BF)
    if w.axis == "col" and _is_small(w):
        n_total = tn = N_CHIPS * w.cols
        tk, gn, gk = kdim, 1, 1
        shard_width = w.cols
        b_spec = pl.BlockSpec((N_CHIPS, kdim, w.cols), lambda n, i, k: (0, 0, 0))
    elif w.axis == "col":
        n_total = N_CHIPS * w.cols
        tn = _pick(w.cols, 1408)
        tk = _pick(kdim, 2048)
        ncb = w.cols // tn
        gn, gk = N_CHIPS * ncb, kdim // tk
        b_map = lambda n, i, k: (n // ncb, k, n % ncb)
    elif kdim <= 2048 and norm_gain is not None:
        n_total = tn = w.cols
        tk, gn, gk = kdim, 1, 1
        weight_buffers = 1
        b_spec = pl.BlockSpec(w.arr.shape, lambda n, i, k: (0, 0, 0), pipeline_mode=pl.Buffered(1))
    elif kdim <= 2048:
        n_total = w.cols
        tn = _pick(w.cols, 1024)
        tk = kdim
        gn, gk = n_total // tn, 1
        b_spec = pl.BlockSpec((N_CHIPS, w.rows, tn), lambda n, i, k: (0, 0, n))
    else:
        n_total = w.cols
        tn = _pick(w.cols, 1024)
        tk = _pick(w.rows, 1408)
        nkb = w.rows // tk
        gn, gk = n_total // tn, N_CHIPS * nkb
        b_map = lambda n, i, k: (k // nkb, k % nkb, n)
    if b_spec is None:
        b_spec = pl.BlockSpec((None, tk, tn), b_map)
    for tm in (1024, 512, 256, 128):
        if m % tm:
            continue
        blocks = [((tm, tk), a.dtype)] + [((tm, tn), e.dtype) for e in extras]
        blocks += [((tm, tn), d) for d in out_dtypes] + [((tm, tn), BF)]
        weight = [((tk, tn), BF)]
        if _tile_bytes(blocks + (weight if weight_buffers == 2 else []), weight if weight_buffers == 1 else ()) <= VMEM_TILE_BUDGET:
            break
    assert norm_gain is None or tn == n_total, name
    o_spec = pl.BlockSpec((tm, tn), lambda n, i, k: (i, n))
    return _mm(
        name, NN_DIMS, [a], [pl.BlockSpec((tm, tk), lambda n, i, k: (i, k))],
        [w.arr], [b_spec], [(0, 0, 0)], 1, (tm, tn), (gn, m // tm, gk),
        list(extras), [o_spec] * len(extras),
        [jax.ShapeDtypeStruct((m, n_total), d) for d in out_dtypes], [o_spec] * len(out_dtypes), epilogue, after,
        shard_width=shard_width, norm_gain=norm_gain)


def _gate_up(name, h, wg, wu):
    m, kdim = h.shape
    tn = _pick(wg.cols, 1408)
    tk = _pick(kdim, 2048)
    ncb = wg.cols // tn
    single = kdim == tk
    for tm in (1024, 512, 256, 128):
        blocks = [((tm, tk), BF)] + [((tm, tn), BF)] * 3
        weights = [((tk, tn), BF)] * 2
        if m % tm == 0 and _tile_bytes(blocks + ([] if single else weights), weights if single else ()) <= VMEM_TILE_BUDGET:
            break
    b_spec = pl.BlockSpec((None, tk, tn), lambda n, i, k: (n // ncb, k, n % ncb),
                          pipeline_mode=pl.Buffered(1) if single else None)
    o_spec = pl.BlockSpec((tm, tn), lambda n, i, k: (i, n))
    n_total = N_CHIPS * wg.cols

    def epilogue(acc, extra):
        g, u = acc
        return [g, u, g * _sigmoid(g) * u]

    return _mm(
        name, NN_DIMS, [h], [pl.BlockSpec((tm, tk), lambda n, i, k: (i, k))], [wg.arr, wu.arr], [b_spec, b_spec],
        [(0, 0, 0), (0, 1, 1)], 2, (tm, tn), (N_CHIPS * ncb, m // tm, kdim // tk), [], [],
        [jax.ShapeDtypeStruct((m, n_total), BF)] * 3, [o_spec] * 3, epilogue, col_chunk=EPILOGUE_CHUNK)


def _mm_nt(name, dys, ws, out_dtypes=(F32,), extras=(), epilogue=_first, after=(), col_chunk=None):
    m = dys[0].shape[0]
    w0 = ws[0]
    npair = len(dys)
    b_spec, shard_width = None, None
    if w0.axis == "col" and npair == 1 and _is_small(w0):
        k_total = tko = w0.rows
        tkc = N_CHIPS * w0.cols
        go, gk = 1, 1
        shard_width = w0.cols
        b_spec = pl.BlockSpec(w0.arr.shape, lambda o, i, k: (0, 0, 0))
    elif w0.axis == "col":
        k_total = w0.rows
        tko = _pick(k_total, 1024)
        tkc = _pick(w0.cols, 1408)
        nkb = w0.cols // tkc
        go, gk = k_total // tko, N_CHIPS * nkb
        b_map = lambda o, i, k: (k // nkb, o, k % nkb)
    else:
        k_total = N_CHIPS * w0.rows
        tko = _pick(w0.rows, 1408)
        tkc = _pick(w0.cols, 2048)
        nob = w0.rows // tko
        go, gk = N_CHIPS * nob, w0.cols // tkc
        b_map = lambda o, i, k: (o // nob, o % nob, k)
    single = gk == 1
    for tm in (1024, 512, 256, 128):
        if m % tm:
            continue
        blocks = [((tm, tkc), d.dtype) for d in dys]
        blocks += [((tm, tko), e.dtype) for e in extras] + [((tm, tko), d) for d in out_dtypes]
        blocks += [((tm, tko), BF)]
        weights = [((tko, tkc), BF)] * npair
        if _tile_bytes(blocks + ([] if single else weights), weights if single else ()) <= VMEM_TILE_BUDGET:
            break
    if b_spec is None:
        b_spec = pl.BlockSpec((None, tko, tkc), b_map, pipeline_mode=pl.Buffered(1) if single else None)
    o_spec = pl.BlockSpec((tm, tko), lambda o, i, k: (i, o))
    return _mm(
        name, NT_DIMS, list(dys), [pl.BlockSpec((tm, tkc), lambda o, i, k: (i, k))] * npair,
        [w.arr for w in ws], [b_spec] * npair,
        [(i, i, 0) for i in range(npair)], 1, (tm, tko), (go, m // tm, gk), list(extras), [o_spec] * len(extras),
        [jax.ShapeDtypeStruct((m, k_total), d) for d in out_dtypes], [o_spec] * len(out_dtypes), epilogue, after,
        col_chunk if gk == 1 and shard_width is None else None, shard_width=shard_width)


def _mm_tn(name, a, dy, w, after=()):
    m, k_total = a.shape
    rows2 = w.rows // 2
    tn = _pick(w.cols, 1408)
    ncb = w.cols // tn
    epilogue, store = _first, None
    if k_total <= 2048 and w.axis == "col" and _is_small(w):
        tkr, tn = k_total, N_CHIPS * w.cols
        gr, gn = 1, 1
        o_spec = pl.BlockSpec((2, N_CHIPS, rows2, w.cols), lambda r, n, t: (0, 0, 0, 0))

        def store(o_refs, outs):
            for j in range(N_CHIPS):
                for h in range(2):
                    o_refs[0][h, j] = outs[0][h * rows2:(h + 1) * rows2, j * w.cols:(j + 1) * w.cols].astype(BF)
    elif k_total <= 2048 and w.axis == "col":
        tkr = k_total
        gr, gn = 1, N_CHIPS * ncb
        o_spec = pl.BlockSpec((2, None, rows2, tn), lambda r, n, t: (0, n // ncb, 0, n % ncb))
        epilogue = lambda acc, extra: [acc[0].reshape(2, rows2, tn)]
    elif k_total <= 2048:
        tkr = k_total
        gr, gn = 1, ncb
        o_spec = pl.BlockSpec((2, N_CHIPS, rows2, tn), lambda r, n, t: (0, 0, 0, n))

        def store(o_refs, outs):
            for j in range(N_CHIPS):
                for h in range(2):
                    lo = (2 * j + h) * rows2
                    o_refs[0][h, j] = outs[0][lo:lo + rows2].astype(BF)
    elif rows2 % LANES:
        tkr = w.rows
        assert w.axis == "row"
        gr, gn = N_CHIPS, ncb
        o_spec = pl.BlockSpec((2, None, rows2, tn), lambda r, n, t: (0, r, 0, n))
        epilogue = lambda acc, extra: [acc[0].reshape(2, rows2, tn)]
    else:
        tkr = _pick(rows2, 1408)
        nrb = rows2 // tkr
        if w.axis == "col":
            gr, gn = w.rows // tkr, N_CHIPS * ncb
            o_map = lambda r, n, t: (r // nrb, n // ncb, r % nrb, n % ncb)
        else:
            per = w.rows // tkr
            gr, gn = N_CHIPS * per, ncb
            o_map = lambda r, n, t: ((r % per) // nrb, r // per, (r % per) % nrb, n)
        o_spec = pl.BlockSpec((None, None, tkr, tn), o_map)
    for tmk in (1024, 512, 256, 128):
        blocks = [((tmk, tkr), a.dtype), ((tmk, tn), dy.dtype), ((tkr, tn), BF), ((tkr, tn), BF)]
        if m % tmk == 0 and _tile_bytes(blocks) <= VMEM_TILE_BUDGET:
            break
    return _mm(
        name, TN_DIMS, [a], [pl.BlockSpec((tmk, tkr), lambda r, n, t: (t, r))],
        [dy], [pl.BlockSpec((tmk, tn), lambda r, n, t: (t, n))], [(0, 0, 0)], 1, (tkr, tn), (gr, gn, m // tmk), [], [],
        [jax.ShapeDtypeStruct((2, N_CHIPS, rows2, w.cols), BF)], [o_spec], epilogue, after, store=store)[0]


def _rms_fwd(name, x, g, after=()):
    s, d = x.shape
    tr = _pick_rows(s, 512)

    def body(x_ref, g_ref, *rest):
        h_ref = rest[-1]
        xf = x_ref[...]
        r = lax.rsqrt(jnp.mean(xf * xf, axis=-1, keepdims=True) + NORM_EPS)
        h_ref[...] = (xf * r * g_ref[...]).astype(BF)

    return pl.pallas_call(
        body, out_shape=jax.ShapeDtypeStruct((s, d), BF), grid=(s // tr,),
        in_specs=[pl.BlockSpec((tr, d), lambda i: (i, 0)), pl.BlockSpec((1, d), lambda i: (0, 0))] + [ANY] * len(after),
        out_specs=pl.BlockSpec((tr, d), lambda i: (i, 0)), name=name, compiler_params=_params(("parallel",)))(x, g, *after)


def _rms_bwd(name, x, g, dh, dres=None):
    s, d = x.shape
    tr = _pick_rows(s, 256)
    has_res = dres is not None

    def body(*refs):
        if has_res:
            x_ref, g_ref, dh_ref, dres_ref, dx_ref, dxb_ref, dg_ref = refs
        else:
            x_ref, g_ref, dh_ref, dx_ref, dxb_ref, dg_ref = refs
        xf = x_ref[...]
        r = lax.rsqrt(jnp.mean(xf * xf, axis=-1, keepdims=True) + NORM_EPS)
        xr = xf * r
        dy = dh_ref[...].astype(F32)
        a = dy * g_ref[...]
        dx = r * (a - xr * jnp.mean(a * xr, axis=-1, keepdims=True))
        if has_res:
            dx = dx + dres_ref[...]
        dx_ref[...] = dx
        dxb_ref[...] = dx.astype(BF)

        @pl.when(pl.program_id(0) == 0)
        def _():
            dg_ref[...] = jnp.zeros_like(dg_ref)

        dg_ref[...] += jnp.sum(dy * xr, axis=0, keepdims=True)

    row = pl.BlockSpec((tr, d), lambda i: (i, 0))
    vec = pl.BlockSpec((1, d), lambda i: (0, 0))
    ins = [x, g, dh] + ([dres] if has_res else [])
    in_specs = [row, vec, row] + ([row] if has_res else [])
    return pl.pallas_call(
        body, out_shape=[jax.ShapeDtypeStruct((s, d), F32), jax.ShapeDtypeStruct((s, d), BF),
                         jax.ShapeDtypeStruct((1, d), F32)],
        grid=(s // tr,), in_specs=in_specs, out_specs=[row, row, vec], name=name,
        compiler_params=_params(("arbitrary",)))(*ins)


def _final_loss(x, tgt, g):
    s, d = x.shape
    tr = _pick_rows(s, 256)

    def body(x_ref, t_ref, g_ref, dx_ref, dxb_ref, dg_ref, loss_ref):
        xf = x_ref[...]
        gain = g_ref[...]
        r = lax.rsqrt(jnp.mean(xf * xf, axis=-1, keepdims=True) + NORM_EPS)
        xr = xf * r
        err = xr * gain - t_ref[...]
        dy = err * (1.0 / d)
        a = dy * gain
        dx = r * (a - xr * jnp.mean(a * xr, axis=-1, keepdims=True))
        dx_ref[...] = dx
        dxb_ref[...] = dx.astype(BF)

        @pl.when(pl.program_id(0) == 0)
        def _():
            dg_ref[...] = jnp.zeros_like(dg_ref)
            loss_ref[...] = jnp.zeros_like(loss_ref)

        dg_ref[...] += jnp.sum(dy * xr, axis=0, keepdims=True)
        part = 0.5 * jnp.sum(jnp.mean(err * err, axis=-1, keepdims=True), axis=0, keepdims=True)
        loss_ref[...] += jnp.broadcast_to(part, loss_ref.shape)

    row = pl.BlockSpec((tr, d), lambda i: (i, 0))
    vec = pl.BlockSpec((1, d), lambda i: (0, 0))
    return pl.pallas_call(
        body, out_shape=[jax.ShapeDtypeStruct((s, d), F32), jax.ShapeDtypeStruct((s, d), BF),
                         jax.ShapeDtypeStruct((1, d), F32), jax.ShapeDtypeStruct((8, LANES), F32)],
        grid=(s // tr,), in_specs=[row, row, vec],
        out_specs=[row, row, vec, pl.BlockSpec((8, LANES), lambda i: (0, 0))],
        name="final_loss", compiler_params=_params(("arbitrary",)))(x, tgt, g)


def _swiglu_bwd_epilogue(acc, extra):
    dact = acc[0]
    g, u = extra[0].astype(F32), extra[1].astype(F32)
    sig = _sigmoid(g)
    return [dact * u * sig * (1.0 + g * (1.0 - sig)), dact * g * sig]


GELU_C = math.sqrt(2.0 / math.pi)
GELU_A = 0.044715


def _gelu(x):
    return 0.5 * x * (1.0 + jnp.tanh(GELU_C * (x + GELU_A * x * x * x)))


def _gelu_both(x):
    x2 = x * x
    t = jnp.tanh(GELU_C * (x + GELU_A * x2 * x))
    half = 0.5 * (1.0 + t)
    return x * half, half + 0.5 * x * (1.0 - t * t) * GELU_C * (1.0 + 3.0 * GELU_A * x2)


def _rope_tables(positions):
    inv_freq = ROPE_THETA ** (-jnp.arange(ROT_HALF, dtype=F32) / ROT_HALF)
    ang = positions.astype(F32)[:, None] * inv_freq
    cos, sin = jnp.cos(ang), jnp.sin(ang)
    s = ang.shape[0]
    rest = HEAD_DIM - 2 * ROT_HALF
    zeros = jnp.zeros((s, ROT_HALF), F32)
    cos_t = jnp.concatenate([cos, cos, jnp.ones((s, rest), F32)], axis=1)
    sin_a = jnp.concatenate([-sin, zeros, jnp.zeros((s, rest), F32)], axis=1)
    sin_b = jnp.concatenate([zeros, sin, jnp.zeros((s, rest), F32)], axis=1)
    return cos_t, sin_a, sin_b


def _rope_head(xh, cos_t, sin_a, sin_b):
    up = pltpu.roll(xh, HEAD_DIM - ROT_HALF, 1)
    down = pltpu.roll(xh, ROT_HALF, 1)
    return xh * cos_t + up * sin_a + down * sin_b


def _residue(r, rows, dil):
    return slice(None) if dil == 1 else pl.ds(r, rows, stride=dil)


ROPE_TILE = 256
N_PARTS = 9
HEADS_PER_GROUP = GROUP_W // HEAD_DIM
N_HEADS_IN = N_PARTS * HEADS_PER_GROUP


def _rope_fwd(proj, tables):
    s = proj.shape[0]
    tm = _pick_rows(s, ROPE_TILE)

    def body(*refs):
        heads = refs[:N_HEADS_IN]
        c_ref, sa_ref, sb_ref = refs[N_HEADS_IN:N_HEADS_IN + 3]
        outs = refs[N_HEADS_IN + 3:]
        for g, dil in enumerate(DILATIONS):
            rows = tm // dil
            for r in range(dil):
                rs = _residue(r, rows, dil)
                cos_t, sin_a, sin_b = c_ref[rs, :], sa_ref[rs, :], sb_ref[rs, :]
                for kind in range(3):
                    part = 3 * kind + g
                    for h in range(HEADS_PER_GROUP):
                        xh = heads[part * HEADS_PER_GROUP + h][rs, :]
                        if kind < 2:
                            xh = _rope_head(xh, cos_t, sin_a, sin_b)
                        outs[part][r, :, h * HEAD_DIM:(h + 1) * HEAD_DIM] = xh.astype(BF)

    tab = pl.BlockSpec((tm, HEAD_DIM), lambda i: (i, 0))
    head_specs = [pl.BlockSpec((tm, HEAD_DIM), lambda i, j=j: (i, j)) for j in range(N_HEADS_IN)]
    shapes, specs = [], []
    for part in range(N_PARTS):
        dil = DILATIONS[part % 3]
        shapes.append(jax.ShapeDtypeStruct((dil, s // dil, GROUP_W), BF))
        specs.append(pl.BlockSpec((dil, tm // dil, GROUP_W), lambda i: (0, i, 0)))
    return pl.pallas_call(
        body, out_shape=shapes, grid=(s // tm,), in_specs=head_specs + [tab, tab, tab], out_specs=specs,
        name="rope_fwd", compiler_params=_params(("parallel",)))(*([proj] * N_HEADS_IN), *tables)


def _rope_bwd(parts, tables, into):
    s = into.shape[0]
    tm = _pick_rows(s, ROPE_TILE)

    def body(*refs):
        ins = refs[:N_PARTS]
        c_ref, sa_ref, sb_ref, into_ref, o_ref, scr = refs[N_PARTS:]
        for g, dil in enumerate(DILATIONS):
            rows = tm // dil
            for r in range(dil):
                rs = _residue(r, rows, dil)
                cos_t, sin_a, sin_b = c_ref[rs, :], -sa_ref[rs, :], -sb_ref[rs, :]
                for kind in range(3):
                    part = 3 * kind + g
                    for h in range(HEADS_PER_GROUP):
                        xh = ins[part][r, :, h * HEAD_DIM:(h + 1) * HEAD_DIM]
                        if kind < 2:
                            xh = _rope_head(xh, cos_t, sin_a, sin_b)
                        scr[part * HEADS_PER_GROUP + h, rs, :] = xh
        for j in range(N_HEADS_IN):
            o_ref[:, j * HEAD_DIM:(j + 1) * HEAD_DIM] = scr[j].astype(BF)

    tab = pl.BlockSpec((tm, HEAD_DIM), lambda i: (i, 0))
    i_specs = [pl.BlockSpec((DILATIONS[p % 3], tm // DILATIONS[p % 3], GROUP_W), lambda i: (0, i, 0))
               for p in range(N_PARTS)]
    return pl.pallas_call(
        body, out_shape=jax.ShapeDtypeStruct(into.shape, into.dtype), grid=(s // tm,),
        in_specs=i_specs + [tab] * 3 + [ANY], out_specs=pl.BlockSpec((tm, N_PARTS * GROUP_W), lambda i: (i, 0)),
        scratch_shapes=[pltpu.VMEM((N_HEADS_IN, tm, HEAD_DIM), F32)], input_output_aliases={N_PARTS + 3: 0},
        name="rope_bwd", compiler_params=_params(("parallel",)))(*parts, *tables, into)


def _band_mask(n):
    qi = lax.broadcasted_iota(jnp.int32, (BLK, 2 * BLK), 0)
    ki = lax.broadcasted_iota(jnp.int32, (BLK, 2 * BLK), 1)
    prev = jnp.logical_and(jnp.logical_and(ki < BLK, ki >= qi), n > 0)
    return jnp.logical_or(prev, jnp.logical_and(ki >= BLK, qi >= ki - BLK))


Q_BLOCKS = 4
Q_ROWS = Q_BLOCKS * BLK


STAT_LANES = HEAD_DIM // HEADS_PER_GROUP


def _stat_of(ref, rows, h):
    return ref[rows, h * STAT_LANES:h * STAT_LANES + 1]


def _pack_stats(cols):
    rows = cols[0].shape[0]
    lane = lax.broadcasted_iota(jnp.int32, (rows, HEAD_DIM), 1)
    tile = jnp.broadcast_to(cols[-1], (rows, HEAD_DIM))
    for h in range(HEADS_PER_GROUP - 2, -1, -1):
        tile = jnp.where(lane < (h + 1) * STAT_LANES, cols[h], tile)
    return tile


def _dil_specs(n_steps):
    last = n_steps - 1
    own = pl.BlockSpec((None, Q_ROWS, GROUP_W), lambda r, n: (r, jnp.minimum(n, last), 0))
    before = pl.BlockSpec((None, BLK, GROUP_W), lambda r, n: (r, jnp.maximum(Q_BLOCKS * n - 1, 0), 0))
    stat = pl.BlockSpec((None, Q_ROWS, HEAD_DIM), lambda r, n: (r, jnp.minimum(n, last), 0))
    return own, before, stat


def _dil_fwd(g, q, k, v):
    dil, length, _ = q.shape
    n_steps = length // Q_ROWS

    def body(q_ref, ko_ref, kb_ref, vo_ref, vb_ref, o_ref, lse_ref):
        n = pl.program_id(1)
        lse_cols = [[] for _ in range(Q_BLOCKS)]
        for h in range(GROUP_W // HEAD_DIM):
            sl = slice(h * HEAD_DIM, (h + 1) * HEAD_DIM)
            keys = jnp.concatenate([kb_ref[:, sl], ko_ref[:, sl]], axis=0)
            vals = jnp.concatenate([vb_ref[:, sl], vo_ref[:, sl]], axis=0)
            for j in range(Q_BLOCKS):
                rows, win = slice(j * BLK, (j + 1) * BLK), slice(j * BLK, (j + 2) * BLK)
                sc = lax.dot_general(q_ref[rows, sl], keys[win], NT_DIMS, preferred_element_type=F32) * SCALE
                sc = jnp.where(_band_mask(Q_BLOCKS * n + j), sc, NEG_INF)
                mx = jnp.max(sc, axis=-1, keepdims=True)
                p = jnp.exp(sc - mx)
                den = jnp.sum(p, axis=-1, keepdims=True)
                o_ref[rows, sl] = jnp.dot(p.astype(BF), vals[win], preferred_element_type=F32) / den
                lse_cols[j].append(mx + jnp.log(den))
        for j in range(Q_BLOCKS):
            lse_ref[j * BLK:(j + 1) * BLK, :] = _pack_stats(lse_cols[j])

    own, before, stat = _dil_specs(n_steps)
    return pl.pallas_call(
        body, out_shape=[jax.ShapeDtypeStruct(q.shape, F32), jax.ShapeDtypeStruct((dil, length, HEAD_DIM), F32)],
        grid=(dil, n_steps), in_specs=[own, own, before, own, before], out_specs=[own, stat], name=f"dil_fwd_{g}",
        compiler_params=_params(("parallel", "arbitrary")))(q, k, k, v, v)


def _dil_bwd(g, q, k, v, do, lse, delta):
    dil, length, _ = q.shape
    n_steps = length // Q_ROWS

    def body(q_ref, ko_ref, kb_ref, vo_ref, vb_ref, do_ref, lse_ref, dl_ref, dq_ref, dk_ref, dv_ref, ck_ref, cv_ref):
        n = pl.program_id(1)
        live = n < n_steps

        @pl.when(n == 0)
        def _():
            ck_ref[...] = jnp.zeros_like(ck_ref)
            cv_ref[...] = jnp.zeros_like(cv_ref)

        @pl.when(jnp.logical_not(live))
        def _():
            dk_ref[...] = ck_ref[...]
            dv_ref[...] = cv_ref[...]

        @pl.when(live)
        def _():
            for h in range(GROUP_W // HEAD_DIM):
                sl = slice(h * HEAD_DIM, (h + 1) * HEAD_DIM)
                keys = jnp.concatenate([kb_ref[:, sl], ko_ref[:, sl]], axis=0)
                vals = jnp.concatenate([vb_ref[:, sl], vo_ref[:, sl]], axis=0)
                dks, dvs = [], []
                for j in range(Q_BLOCKS):
                    rows, win = slice(j * BLK, (j + 1) * BLK), slice(j * BLK, (j + 2) * BLK)
                    qh, doh = q_ref[rows, sl], do_ref[rows, sl]
                    lse_h, dl_h = _stat_of(lse_ref, rows, h), _stat_of(dl_ref, rows, h)
                    sc = lax.dot_general(qh, keys[win], NT_DIMS, preferred_element_type=F32) * SCALE
                    p = jnp.where(_band_mask(Q_BLOCKS * n + j), jnp.exp(jnp.minimum(sc - lse_h, 0.0)), 0.0)
                    dp = lax.dot_general(doh, vals[win], NT_DIMS, preferred_element_type=F32)
                    ds = (p * (dp - dl_h) * SCALE).astype(BF)
                    dq_ref[rows, sl] = jnp.dot(ds, keys[win], preferred_element_type=F32)
                    dks.append(lax.dot_general(ds, qh, TN_DIMS, preferred_element_type=F32))
                    dvs.append(lax.dot_general(p.astype(BF), doh, TN_DIMS, preferred_element_type=F32))
                for out_ref, carry, parts in ((dk_ref, ck_ref, dks), (dv_ref, cv_ref, dvs)):
                    out_ref[:Q_ROWS - BLK, sl] = carry[:Q_ROWS - BLK, sl]
                    out_ref[Q_ROWS - BLK:, sl] = carry[Q_ROWS - BLK:, sl] + parts[0][:BLK]
                    for j in range(Q_BLOCKS - 1):
                        carry[j * BLK:(j + 1) * BLK, sl] = parts[j][BLK:] + parts[j + 1][:BLK]
                    carry[Q_ROWS - BLK:, sl] = parts[-1][BLK:]

    own, before, stat = _dil_specs(n_steps)
    behind = pl.BlockSpec((None, Q_ROWS, GROUP_W), lambda r, n: (r, jnp.maximum(n - 1, 0), 0))
    return pl.pallas_call(
        body, out_shape=[jax.ShapeDtypeStruct(q.shape, F32)] * 3, grid=(dil, n_steps + 1),
        in_specs=[own, own, before, own, before, own, stat, stat], out_specs=[own, behind, behind],
        scratch_shapes=[pltpu.VMEM((Q_ROWS, GROUP_W), F32)] * 2, name=f"dil_bwd_{g}",
        compiler_params=_params(("parallel", "arbitrary")))(q, k, k, v, v, do, lse, delta)


def _major_specs(s, tm, dtype, width=GROUP_W):
    shapes = [jax.ShapeDtypeStruct((dil, s // dil, width), dtype) for dil in DILATIONS]
    specs = [pl.BlockSpec((dil, tm // dil, width), lambda i: (0, i, 0)) for dil in DILATIONS]
    return shapes, specs


def _attn_merge(outs, lses):
    s = outs[0].shape[1]
    tm = _pick_rows(s, ROPE_TILE)

    def body(o0, o1, o2, l0, l1, l2, m_ref, e0, e1, e2, so1, so2, sl1, sl2, se):
        for dil, src, dst in ((DILATIONS[1], l1, sl1), (DILATIONS[2], l2, sl2)):
            for r in range(dil):
                dst[_residue(r, tm // dil, dil), :] = src[r]
        a, b, c = l0[0], sl1[...], sl2[...]
        mx = jnp.maximum(jnp.maximum(a, b), c)
        ea, eb, ec = jnp.exp(a - mx), jnp.exp(b - mx), jnp.exp(c - mx)
        den = ea + eb + ec
        wa, wb, wc = ea / den, eb / den, ec / den
        se[...] = mx + jnp.log(den)
        for dil, dst in zip(DILATIONS, (e0, e1, e2)):
            for r in range(dil):
                dst[r] = se[_residue(r, tm // dil, dil), :]
        for h in range(HEADS_PER_GROUP):
            sl = slice(h * HEAD_DIM, (h + 1) * HEAD_DIM)
            col = slice(h * STAT_LANES, h * STAT_LANES + 1)
            for dil, src, dst in ((DILATIONS[1], o1, so1), (DILATIONS[2], o2, so2)):
                for r in range(dil):
                    dst[h, _residue(r, tm // dil, dil), :] = src[r, :, sl]
            m_ref[:, sl] = (wa[:, col] * o0[0, :, sl] + wb[:, col] * so1[h] + wc[:, col] * so2[h]).astype(BF)

    shapes, specs = _major_specs(s, tm, F32)
    stat_shapes, stat_specs = _major_specs(s, tm, F32, HEAD_DIM)
    nat = pl.BlockSpec((tm, GROUP_W), lambda i: (i, 0))
    res = pl.pallas_call(
        body, out_shape=[jax.ShapeDtypeStruct((s, GROUP_W + MEM_W), BF)] + stat_shapes, grid=(s // tm,),
        in_specs=specs + stat_specs, out_specs=[nat] + stat_specs,
        scratch_shapes=[pltpu.VMEM((HEADS_PER_GROUP, tm, HEAD_DIM), F32)] * 2 + [pltpu.VMEM((tm, HEAD_DIM), F32)] * 3,
        name="attn_merge", compiler_params=_params(("parallel",)))(*outs, *lses)
    return res[0], res[1:]


def _attn_delta(dcat, merged, after=()):
    s = merged.shape[0]
    tm = _pick_rows(s, ROPE_TILE)

    def body(*refs):
        d_refs, m_ref = refs[:HEADS_PER_GROUP], refs[HEADS_PER_GROUP]
        do_refs, dl_refs, scr = refs[-7:-4], refs[-4:-1], refs[-1]
        sums = []
        for h in range(HEADS_PER_GROUP):
            sl = slice(h * HEAD_DIM, (h + 1) * HEAD_DIM)
            sums.append(jnp.sum(d_refs[h][...] * m_ref[:, sl].astype(F32), axis=-1, keepdims=True))
            for dil, do_ref in zip(DILATIONS, do_refs):
                for r in range(dil):
                    do_ref[r, :, sl] = d_refs[h][_residue(r, tm // dil, dil), :].astype(BF)
        scr[...] = _pack_stats(sums)
        for dil, dl_ref in zip(DILATIONS, dl_refs):
            for r in range(dil):
                dl_ref[r] = scr[_residue(r, tm // dil, dil), :]

    nat = pl.BlockSpec((tm, GROUP_W), lambda i: (i, 0))
    head_specs = [pl.BlockSpec((tm, HEAD_DIM), lambda i, h=h: (i, h)) for h in range(HEADS_PER_GROUP)]
    bf_shapes, specs = _major_specs(s, tm, BF)
    stat_shapes, stat_specs = _major_specs(s, tm, F32, HEAD_DIM)
    res = pl.pallas_call(
        body, out_shape=bf_shapes + stat_shapes, grid=(s // tm,), in_specs=head_specs + [nat] + [ANY] * len(after),
        out_specs=specs + stat_specs, scratch_shapes=[pltpu.VMEM((tm, HEAD_DIM), F32)], name="attn_delta",
        compiler_params=_params(("parallel",)))(*([dcat] * HEADS_PER_GROUP), merged, *after)
    return res[:3], res[3:]


def _mem_probs(qh, kh):
    sc = lax.dot_general(qh, kh, NT_DIMS, preferred_element_type=F32) * SCALE
    p = jnp.exp(sc - jnp.max(sc, axis=-1, keepdims=True))
    return p, jnp.sum(p, axis=-1, keepdims=True)


def _mem_fwd(name, proj, q_block, kv, into, out_block):
    s = proj.shape[0]
    tq = _pick_rows(s, 512)

    def body(q_ref, kv_ref, into_ref, o_ref):
        for h in range(MEM_HEADS):
            sl = slice(h * HEAD_DIM, (h + 1) * HEAD_DIM)
            vsl = slice(MEM_W + h * HEAD_DIM, MEM_W + (h + 1) * HEAD_DIM)
            p, den = _mem_probs(q_ref[:, sl].astype(BF), kv_ref[:, sl].astype(BF))
            out = jnp.dot(p.astype(BF), kv_ref[:, vsl].astype(BF), preferred_element_type=F32) / den
            o_ref[:, sl] = out.astype(o_ref.dtype)

    return pl.pallas_call(
        body, out_shape=jax.ShapeDtypeStruct(into.shape, into.dtype), grid=(s // tq,),
        in_specs=[pl.BlockSpec((tq, MEM_W), lambda i: (i, q_block)), pl.BlockSpec(kv.shape, lambda i: (0, 0)), ANY],
        out_specs=pl.BlockSpec((tq, MEM_W), lambda i: (i, out_block)), input_output_aliases={2: 0}, name=name,
        compiler_params=_params(("parallel",)))(proj, kv, into)


def _mem_bwd(name, proj, q_block, kv, dcat, d_block, width):
    s = proj.shape[0]
    tq = _pick_rows(s, 512)

    def body(q_ref, kv_ref, do_ref, dq_ref, dkv_ref):
        @pl.when(pl.program_id(0) == 0)
        def _():
            dkv_ref[...] = jnp.zeros_like(dkv_ref)

        for h in range(MEM_HEADS):
            sl = slice(h * HEAD_DIM, (h + 1) * HEAD_DIM)
            vsl = slice(MEM_W + h * HEAD_DIM, MEM_W + (h + 1) * HEAD_DIM)
            qh, kh, vh = q_ref[:, sl].astype(BF), kv_ref[:, sl].astype(BF), kv_ref[:, vsl].astype(BF)
            doh = do_ref[:, sl].astype(BF)
            p, den = _mem_probs(qh, kh)
            p = p / den
            dp = lax.dot_general(doh, vh, NT_DIMS, preferred_element_type=F32)
            ds = (p * (dp - jnp.sum(p * dp, axis=-1, keepdims=True)) * SCALE).astype(BF)
            dq_ref[:, sl] = jnp.dot(ds, kh, preferred_element_type=F32).astype(BF)
            dkv_ref[:, sl] += lax.dot_general(ds, qh, TN_DIMS, preferred_element_type=F32)
            dkv_ref[:, vsl] += lax.dot_general(p.astype(BF), doh, TN_DIMS, preferred_element_type=F32)

    whole = pl.BlockSpec(kv.shape, lambda i: (0, 0))
    return pl.pallas_call(
        body, out_shape=[jax.ShapeDtypeStruct((s, width), BF), jax.ShapeDtypeStruct(kv.shape, F32)], grid=(s // tq,),
        in_specs=[pl.BlockSpec((tq, MEM_W), lambda i: (i, q_block)), whole,
                  pl.BlockSpec((tq, MEM_W), lambda i: (i, d_block))],
        out_specs=[pl.BlockSpec((tq, MEM_W), lambda i: (i, width // MEM_W - 1)), whole], name=name,
        compiler_params=_params(("arbitrary",)))(proj, kv, dcat)


def _causal():
    t = lax.broadcasted_iota(jnp.int32, (BLK, BLK), 0)
    s = lax.broadcasted_iota(jnp.int32, (BLK, BLK), 1)
    return t >= s


def _sgu_norm(vg, ln_g, ln_b):
    mu = jnp.mean(vg, axis=-1, keepdims=True)
    cen = vg - mu
    rstd = lax.rsqrt(jnp.mean(cen * cen, axis=-1, keepdims=True) + LN_EPS)
    xhat = cen * rstd
    return xhat, rstd, xhat * ln_g + ln_b


def _sgu_fwd(proj, ln_g, ln_b, w_sp, b_t):
    s = proj.shape[0]

    def body(u_ref, v_ref, g_ref, b_ref, w_ref, bt_ref, o_ref):
        _, _, vn = _sgu_norm(_gelu(v_ref[...].astype(F32)), g_ref[...], b_ref[...])
        vn = vn.astype(BF)
        tri = _causal()
        for grp in range(SGU_GROUPS):
            sl = slice(grp * HEAD_DIM, (grp + 1) * HEAD_DIM)
            w = jnp.where(tri, w_ref[grp], 0.0).astype(BF)
            mixed = jnp.dot(w, vn[:, sl], preferred_element_type=F32) + bt_ref[:, grp:grp + 1]
            o_ref[:, sl] = (_gelu(u_ref[:, sl].astype(F32)) * mixed).astype(BF)

    vec = pl.BlockSpec((1, SGU_W), lambda i: (0, 0))
    return pl.pallas_call(
        body, out_shape=jax.ShapeDtypeStruct((s, SGU_W + MEM_W), BF), grid=(s // BLK,),
        in_specs=[pl.BlockSpec((BLK, SGU_W), lambda i: (i, 0)), pl.BlockSpec((BLK, SGU_W), lambda i: (i, 1)), vec, vec,
                  pl.BlockSpec(w_sp.shape, lambda i: (0, 0, 0)), pl.BlockSpec(b_t.shape, lambda i: (0, 0))],
        out_specs=pl.BlockSpec((BLK, SGU_W), lambda i: (i, 0)), name="sgu_fwd",
        compiler_params=_params(("parallel",)))(proj, proj, ln_g, ln_b, w_sp, b_t)


def _sgu_bwd(proj, dcat, ln_g, ln_b, w_sp, b_t, into):
    s = proj.shape[0]

    def body(u_ref, v_ref, d_ref, g_ref, b_ref, w_ref, bt_ref, into_ref, dp_ref, dw_ref, db_ref, dg_ref, dbeta_ref,
             dvn_ref):
        @pl.when(pl.program_id(0) == 0)
        def _():
            dw_ref[...] = jnp.zeros_like(dw_ref)
            db_ref[...] = jnp.zeros_like(db_ref)
            dg_ref[...] = jnp.zeros_like(dg_ref)
            dbeta_ref[...] = jnp.zeros_like(dbeta_ref)

        gain = g_ref[...]
        vg, v_slope = _gelu_both(v_ref[...].astype(F32))
        xhat, rstd, vn = _sgu_norm(vg, gain, b_ref[...])
        vn = vn.astype(BF)
        tri = _causal()
        lane = lax.broadcasted_iota(jnp.int32, (BLK, HEAD_DIM), 1)
        db_acc = jnp.zeros((BLK, HEAD_DIM), F32)
        for grp in range(SGU_GROUPS):
            sl = slice(grp * HEAD_DIM, (grp + 1) * HEAD_DIM)
            w = jnp.where(tri, w_ref[grp], 0.0).astype(BF)
            vn_g = vn[:, sl]
            mixed = jnp.dot(w, vn_g, preferred_element_type=F32) + bt_ref[:, grp:grp + 1]
            u_act, u_slope = _gelu_both(u_ref[:, sl].astype(F32))
            d_out = d_ref[:, sl].astype(F32)
            dp_ref[:, sl] = (d_out * mixed * u_slope).astype(BF)
            dmixed = d_out * u_act
            dm = dmixed.astype(BF)
            dvn_ref[:, sl] = lax.dot_general(w, dm, TN_DIMS, preferred_element_type=F32)
            dw = lax.dot_general(dm, vn_g, NT_DIMS, preferred_element_type=F32)
            dw_ref[grp] += jnp.where(tri, dw, 0.0)
            db_acc += jnp.where(lane == grp, jnp.sum(dmixed, axis=-1, keepdims=True), 0.0)
        db_ref[...] += db_acc
        dvn = dvn_ref[...]
        dg_ref[...] += jnp.sum(dvn * xhat, axis=0, keepdims=True)
        dbeta_ref[...] += jnp.sum(dvn, axis=0, keepdims=True)
        dxh = dvn * gain
        dvg = rstd * (dxh - jnp.mean(dxh, axis=-1, keepdims=True) - xhat * jnp.mean(dxh * xhat, axis=-1, keepdims=True))
        dp_ref[:, SGU_W:] = (dvg * v_slope).astype(BF)

    vec = pl.BlockSpec((1, SGU_W), lambda i: (0, 0))
    row = pl.BlockSpec((BLK, SGU_W), lambda i: (i, 0))
    w_spec = pl.BlockSpec(w_sp.shape, lambda i: (0, 0, 0))
    sq = pl.BlockSpec((BLK, HEAD_DIM), lambda i: (0, 0))
    return pl.pallas_call(
        body,
        out_shape=[jax.ShapeDtypeStruct(into.shape, into.dtype),
                   jax.ShapeDtypeStruct(w_sp.shape, F32), jax.ShapeDtypeStruct((BLK, HEAD_DIM), F32),
                   jax.ShapeDtypeStruct((1, SGU_W), F32), jax.ShapeDtypeStruct((1, SGU_W), F32)],
        grid=(s // BLK,),
        in_specs=[row, pl.BlockSpec((BLK, SGU_W), lambda i: (i, 1)), row, vec, vec, w_spec,
                  pl.BlockSpec(b_t.shape, lambda i: (0, 0)), ANY],
        out_specs=[pl.BlockSpec((BLK, 2 * SGU_W), lambda i: (i, 0)), w_spec, sq, vec, vec],
        scratch_shapes=[pltpu.VMEM((BLK, SGU_W), F32)], input_output_aliases={7: 0}, name="sgu_bwd",
        compiler_params=_params(("arbitrary",)))(proj, proj, dcat, ln_g, ln_b, w_sp, b_t, into)


def _place():
    return lax.axis_index("x"), lax.axis_index("y"), lax.axis_index("c")


def _other_chips(x, y):
    return [(1 - x, y), (x, 1 - y), (1 - x, 1 - y)]


def _peer(x, y, c, mask):
    return (1 - x if mask & 4 else x, 1 - y if mask & 2 else y, 1 - c if mask & 1 else c)


def _in_hbm(a):
    return pltpu.with_memory_space_constraint(a, pltpu.HBM)


def _token_spec():
    return jax.ShapeDtypeStruct((8, LANES), F32), pl.BlockSpec(memory_space=pltpu.VMEM)


def _remote(src, dst, ssem, rsem, to):
    return pltpu.make_async_remote_copy(src_ref=src, dst_ref=dst, send_sem=ssem, recv_sem=rsem, device_id=to,
                                        device_id_type=MESH)


def _place_shard(name, src, layer, place, dtype, after=()):
    _, rows, cols = src.shape
    tr = _pick_rows(rows, 512)

    def body(p_ref, s_ref, *rest):
        rest[-1][...] = s_ref[...].astype(dtype)

    grid_spec = pltpu.PrefetchScalarGridSpec(
        num_scalar_prefetch=1, grid=(rows // tr,),
        in_specs=[pl.BlockSpec((None, tr, cols), lambda i, p: (layer, i, 0))] + [ANY] * len(after),
        out_specs=pl.BlockSpec((None, tr, cols), lambda i, p: (p[1], i, 0)))
    return pl.pallas_call(body, out_shape=jax.ShapeDtypeStruct((N_CHIPS, rows, cols), dtype), grid_spec=grid_spec,
                          name=name, compiler_params=_params(("parallel",)))(place, src, *after)


def _gather_copies(bufs, ssem, rsem):
    x, y, c = _place()
    me = 2 * x + y
    copies = []
    for ai, buf in enumerate(bufs):
        for k, (ox, oy) in enumerate(_other_chips(x, y)):
            copies.append(_remote(buf.at[me], buf.at[me], ssem.at[3 * ai + k], rsem.at[3 * ai + k], (ox, oy, c)))
    return copies


def _reduce_copies(grads, lands, ssem, rsem):
    x, y, c = _place()
    copies = []
    for a, (gr, land) in enumerate(zip(grads, lands)):
        for mask in range(1, N_DEV):
            px, py, pc = _peer(x, y, c, mask)
            copies.append(_remote(gr.at[pc, 2 * px + py], land.at[mask - 1], ssem.at[7 * a + mask - 1],
                                  rsem.at[7 * a + mask - 1], (px, py, pc)))
    return copies


def _half_copies(totals, ssem, rsem):
    x, y, c = _place()
    return [_remote(t.at[c], t.at[c], ssem.at[a], rsem.at[a], (x, y, 1 - c)) for a, t in enumerate(totals)]


def _gather_start(name, groups):
    flat = [s for grp in groups for s in grp]
    n, ng = len(flat), len(groups)

    def body(*refs):
        ins = refs[:n]
        sems = refs[n:n + 2 * ng]
        token = refs[-1]
        idx = 0
        for gi, grp in enumerate(groups):
            for cp in _gather_copies(ins[idx:idx + len(grp)], sems[2 * gi], sems[2 * gi + 1]):
                cp.start()
            idx += len(grp)
        token[...] = jnp.zeros_like(token)

    tok_shape, tok_spec = _token_spec()
    sem_shapes = []
    for grp in groups:
        sem_shapes += [pltpu.SemaphoreType.DMA((3 * len(grp),))] * 2
    res = pl.pallas_call(
        body, name=name,
        out_shape=(*sem_shapes, *[pltpu.HBM(s.shape, s.dtype) for s in flat], tok_shape),
        in_specs=[HBM] * n, out_specs=(*[SEM] * (2 * ng), *[HBM] * n, tok_spec),
        input_output_aliases={i: 2 * ng + i for i in range(n)},
        compiler_params=pltpu.CompilerParams(has_side_effects=EFFECT))(*[_in_hbm(s) for s in flat])
    out, idx = [], 2 * ng
    for gi, grp in enumerate(groups):
        out.append((res[2 * gi], res[2 * gi + 1], list(res[idx:idx + len(grp)])))
        idx += len(grp)
    return out, res[-1]


def _gather_wait(name, ssem, rsem, slabs, after):
    n = len(slabs)

    def body(*refs):
        for cp in _gather_copies(refs[:n], refs[n], refs[n + 1]):
            cp.wait_send()
            cp.wait_recv()

    return pl.pallas_call(
        body, name=name, out_shape=tuple(pltpu.HBM(s.shape, s.dtype) for s in slabs),
        in_specs=[HBM] * n + [SEM, SEM] + [ANY] * len(after), out_specs=tuple([HBM] * n),
        input_output_aliases={i: i for i in range(n)},
        compiler_params=pltpu.CompilerParams(has_side_effects=EFFECT))(*slabs, ssem, rsem, *after)


def _reduce_start(name, grads):
    n = len(grads)
    lands = [lax.empty((N_DEV - 1, *g.shape[2:]), g.dtype) for g in grads]

    def body(*refs):
        token = refs[-1]
        for cp in _reduce_copies(refs[:n], refs[n:2 * n], refs[2 * n], refs[2 * n + 1]):
            cp.start()
        token[...] = jnp.zeros_like(token)

    tok_shape, tok_spec = _token_spec()
    sems = [pltpu.SemaphoreType.DMA((7 * n,))] * 2
    res = pl.pallas_call(
        body, name=name,
        out_shape=(*sems, *[pltpu.HBM(g.shape, g.dtype) for g in grads], *[pltpu.HBM(l.shape, l.dtype) for l in lands],
                   tok_shape),
        in_specs=[HBM] * (2 * n), out_specs=(SEM, SEM, *[HBM] * (2 * n), tok_spec),
        input_output_aliases={i: 2 + i for i in range(2 * n)},
        compiler_params=pltpu.CompilerParams(has_side_effects=EFFECT))(*[_in_hbm(t) for t in (*grads, *lands)])
    return res[0], res[1], list(res[2:2 + n]), list(res[2 + n:2 + 2 * n]), res[-1]


def _reduce_wait(name, ssem, rsem, grads, lands, after):
    n = len(grads)

    def body(*refs):
        for cp in _reduce_copies(refs[:n], refs[n:2 * n], refs[2 * n], refs[2 * n + 1]):
            cp.wait_send()
            cp.wait_recv()

    res = pl.pallas_call(
        body, name=name, out_shape=tuple(pltpu.HBM(t.shape, t.dtype) for t in (*grads, *lands)),
        in_specs=[HBM] * (2 * n) + [SEM, SEM] + [ANY] * len(after), out_specs=tuple([HBM] * (2 * n)),
        input_output_aliases={i: i for i in range(2 * n)},
        compiler_params=pltpu.CompilerParams(has_side_effects=EFFECT))(*grads, *lands, ssem, rsem, *after)
    return list(res[:n]), list(res[n:])


def _sum_pieces(name, grad, land, place):
    _, _, rows, cols = grad.shape
    tr = _pick_rows(rows, 256)

    def body(p_ref, g_ref, l_ref, o_ref):
        tot = g_ref[...].astype(F32)
        for k in range(N_DEV - 1):
            tot = tot + l_ref[k].astype(F32)
        o_ref[...] = tot

    grid_spec = pltpu.PrefetchScalarGridSpec(
        num_scalar_prefetch=1, grid=(rows // tr,),
        in_specs=[pl.BlockSpec((None, None, tr, cols), lambda i, p: (p[0], p[1], i, 0)),
                  pl.BlockSpec((N_DEV - 1, tr, cols), lambda i, p: (0, i, 0))],
        out_specs=pl.BlockSpec((None, tr, cols), lambda i, p: (p[0], i, 0)))
    return pl.pallas_call(body, out_shape=jax.ShapeDtypeStruct((2, rows, cols), F32), grid_spec=grid_spec, name=name,
                          compiler_params=_params(("parallel",)))(place, grad, land)


def _half_start(name, totals):
    n = len(totals)

    def body(*refs):
        token = refs[-1]
        for cp in _half_copies(refs[:n], refs[n], refs[n + 1]):
            cp.start()
        token[...] = jnp.zeros_like(token)

    tok_shape, tok_spec = _token_spec()
    res = pl.pallas_call(
        body, name=name,
        out_shape=(pltpu.SemaphoreType.DMA((n,)), pltpu.SemaphoreType.DMA((n,)),
                   *[pltpu.HBM(t.shape, t.dtype) for t in totals], tok_shape),
        in_specs=[HBM] * n, out_specs=(SEM, SEM, *[HBM] * n, tok_spec),
        input_output_aliases={i: 2 + i for i in range(n)},
        compiler_params=pltpu.CompilerParams(has_side_effects=EFFECT))(*[_in_hbm(t) for t in totals])
    return res[0], res[1], list(res[2:2 + n]), res[-1]


def _half_wait(name, ssem, rsem, totals, after):
    n = len(totals)

    def body(*refs):
        for cp in _half_copies(refs[:n], refs[n], refs[n + 1]):
            cp.wait_send()
            cp.wait_recv()

    res = pl.pallas_call(
        body, name=name, out_shape=tuple(pltpu.HBM(t.shape, t.dtype) for t in totals),
        in_specs=[HBM] * n + [SEM, SEM] + [ANY] * len(after), out_specs=tuple([HBM] * n),
        input_output_aliases={i: i for i in range(n)},
        compiler_params=pltpu.CompilerParams(has_side_effects=EFFECT))(*totals, ssem, rsem, *after)
    return list(res)


def _small_copies(bufs, ssem, rsem):
    x, y, c = _place()
    mine = bufs[0].at[4 * x + 2 * y + c]
    return [_remote(mine, mine, ssem.at[mask - 1], rsem.at[mask - 1], _peer(x, y, c, mask)) for mask in range(1, N_DEV)]


def _small_start(name, slots):
    def body(s_ref, ssem, rsem, thru, token):
        for cp in _small_copies([s_ref], ssem, rsem):
            cp.start()
        token[...] = jnp.zeros_like(token)

    tok_shape, tok_spec = _token_spec()
    sems = [pltpu.SemaphoreType.DMA((N_DEV - 1,))] * 2
    return pl.pallas_call(
        body, name=name, out_shape=(*sems, pltpu.HBM(slots.shape, slots.dtype), tok_shape), in_specs=[HBM],
        out_specs=(SEM, SEM, HBM, tok_spec), input_output_aliases={0: 2},
        compiler_params=pltpu.CompilerParams(has_side_effects=EFFECT))(_in_hbm(slots))


def _small_wait(name, ssem, rsem, slots, after):
    def body(*refs):
        for cp in _small_copies([refs[0]], refs[1], refs[2]):
            cp.wait_send()
            cp.wait_recv()

    return pl.pallas_call(
        body, name=name, out_shape=pltpu.HBM(slots.shape, slots.dtype), in_specs=[HBM, SEM, SEM] + [ANY] * len(after),
        out_specs=HBM, input_output_aliases={0: 0},
        compiler_params=pltpu.CompilerParams(has_side_effects=EFFECT))(slots, ssem, rsem, *after)


def _own_slot(small, me):
    return lax.dynamic_update_slice(jnp.zeros((N_DEV, *small.shape), small.dtype), small[None], (me, 0, 0))


def _sum_devices(name, stacked):
    _, rows, lanes = stacked.shape
    tr = _pick_rows(rows, 512)

    def body(s_ref, o_ref):
        tot = s_ref[0]
        for k in range(1, N_DEV):
            tot = tot + s_ref[k]
        o_ref[...] = tot

    return pl.pallas_call(
        body, out_shape=jax.ShapeDtypeStruct((rows, lanes), F32), grid=(rows // tr,),
        in_specs=[pl.BlockSpec((N_DEV, tr, lanes), lambda i: (0, i, 0))], out_specs=pl.BlockSpec((tr, lanes), lambda i: (i, 0)),
        name=name, compiler_params=_params(("parallel",)))(stacked)


def _adamw(name, w, g, m, v, layer, prev=None):
    layers, rows, cols = w.shape
    tr = _pick_rows(rows, 256)
    c1 = 1.0 - ADAM_B1 ** ADAM_STEP
    c2 = 1.0 - ADAM_B2 ** ADAM_STEP

    def body(w_ref, g_ref, m_ref, v_ref, *rest):
        go_ref, d_ref, nm_ref, nv_ref = rest[-4:]
        gv = g_ref[...]
        nm = ADAM_B1 * m_ref[...] + (1.0 - ADAM_B1) * gv
        nv = ADAM_B2 * v_ref[...] + (1.0 - ADAM_B2) * (gv * gv)
        go_ref[...] = gv
        d_ref[...] = -ADAM_LR * ((nm / c1) / (jnp.sqrt(nv / c2) + ADAM_EPS) + ADAM_WD * w_ref[...])
        nm_ref[...] = nm
        nv_ref[...] = nv

    spec = pl.BlockSpec((None, tr, cols), lambda i: (layer, i, 0))
    prev = list(prev) if prev is not None else []
    return pl.pallas_call(
        body, out_shape=[jax.ShapeDtypeStruct((layers, rows, cols), F32)] * 4, grid=(rows // tr,),
        in_specs=[spec, pl.BlockSpec((tr, cols), lambda i: (i, 0)), spec, spec] + [ANY] * len(prev),
        out_specs=[spec] * 4, input_output_aliases={4 + i: i for i in range(len(prev))}, name=name,
        compiler_params=_params(("parallel",)))(w, g, m, v, *prev)


def _pack(vectors, pad_rows):
    flat = jnp.concatenate([t.reshape(-1) for t in vectors])
    rows = -(-flat.shape[0] // LANES)
    rows = -(-rows // pad_rows) * pad_rows
    return jnp.pad(flat, (0, rows * LANES - flat.shape[0])).reshape(rows, LANES)


def _unpack(packed, shapes):
    flat = packed.reshape(-1)
    out, off = [], 0
    for shp in shapes:
        size = math.prod(shp)
        out.append(flat[off:off + size].reshape(shp))
        off += size
    return out


def kernel(x, mem, positions, mix_norm, mem_norm, w_mem_kv, ffn_norm, w_gate, w_up, w_down, attn_w_in, attn_w_out, sgu_w_in, sgu_ln_g, sgu_ln_b, sgu_w_spatial, sgu_b_spatial, sgu_w_out, final_norm, loss_target, m_mix_norm, m_mem_norm, m_w_mem_kv, m_ffn_norm, m_w_gate, m_w_up, m_w_down, m_attn_w_in, m_attn_w_out, m_sgu_w_in, m_sgu_ln_g, m_sgu_ln_b, m_sgu_w_spatial, m_sgu_b_spatial, m_sgu_w_out, m_final_norm, v_mix_norm, v_mem_norm, v_w_mem_kv, v_ffn_norm, v_w_gate, v_w_up, v_w_down, v_attn_w_in, v_attn_w_out, v_sgu_w_in, v_sgu_ln_g, v_sgu_ln_b, v_sgu_w_spatial, v_sgu_b_spatial, v_sgu_w_out, v_final_norm):
    d_model = x.shape[2]
    x0, mem0, tgt = x[0], mem[0], loss_target[0]
    xi, yi, ci = _place()
    chip = 2 * xi + yi
    place = jnp.stack([ci, chip]).astype(jnp.int32)

    given_w = dict(mix_norm=mix_norm, mem_norm=mem_norm, w_mem_kv=w_mem_kv, ffn_norm=ffn_norm, w_gate=w_gate, w_up=w_up,
                   w_down=w_down, attn_w_in=attn_w_in, attn_w_out=attn_w_out, sgu_w_in=sgu_w_in, sgu_ln_g=sgu_ln_g,
                   sgu_ln_b=sgu_ln_b, sgu_w_spatial=sgu_w_spatial, sgu_b_spatial=sgu_b_spatial, sgu_w_out=sgu_w_out,
                   final_norm=final_norm)
    given_m = dict(mix_norm=m_mix_norm, mem_norm=m_mem_norm, w_mem_kv=m_w_mem_kv, ffn_norm=m_ffn_norm, w_gate=m_w_gate,
                   w_up=m_w_up, w_down=m_w_down, attn_w_in=m_attn_w_in, attn_w_out=m_attn_w_out, sgu_w_in=m_sgu_w_in,
                   sgu_ln_g=m_sgu_ln_g, sgu_ln_b=m_sgu_ln_b, sgu_w_spatial=m_sgu_w_spatial,
                   sgu_b_spatial=m_sgu_b_spatial, sgu_w_out=m_sgu_w_out, final_norm=m_final_norm)
    given_v = dict(mix_norm=v_mix_norm, mem_norm=v_mem_norm, w_mem_kv=v_w_mem_kv, ffn_norm=v_ffn_norm, w_gate=v_w_gate,
                   w_up=v_w_up, w_down=v_w_down, attn_w_in=v_attn_w_in, attn_w_out=v_attn_w_out, sgu_w_in=v_sgu_w_in,
                   sgu_ln_g=v_sgu_ln_g, sgu_ln_b=v_sgu_ln_b, sgu_w_spatial=v_sgu_w_spatial,
                   sgu_b_spatial=v_sgu_b_spatial, sgu_w_out=v_sgu_w_out, final_norm=v_final_norm)

    units = {"attn_w_in": ("attn_w_in", 0, "col"), "w_mem_kv0": ("w_mem_kv", 0, "row"), "attn_w_out": ("attn_w_out", 0, "col"),
             "w_gate0": ("w_gate", 0, "col"), "w_up0": ("w_up", 0, "col"), "w_down0": ("w_down", 0, "row"),
             "sgu_w_in": ("sgu_w_in", 0, "col"), "w_mem_kv1": ("w_mem_kv", 1, "row"), "sgu_w_out": ("sgu_w_out", 0, "row"),
             "w_gate1": ("w_gate", 1, "col"), "w_up1": ("w_up", 1, "col"), "w_down1": ("w_down", 1, "row")}
    gather_groups = [["attn_w_in"], ["w_mem_kv0", "attn_w_out"], ["w_gate0", "w_up0"],
                     ["w_down0", "sgu_w_in", "w_mem_kv1", "ln"], ["sgu_w_out"], ["w_gate1", "w_up1"], ["w_down1"]]

    first = _place_shard("place_attn_w_in", attn_w_in, 0, place, BF)
    in_flight, token = _gather_start("gather_start_0", [[first]])
    slabs = {u: _place_shard(f"place_{u}", given_w[arr], layer, place, BF, after=[token])
             for u, (arr, layer, _) in units.items() if u != "attn_w_in"}
    slabs["ln"] = _place_shard("place_ln", jnp.concatenate([sgu_ln_g, sgu_ln_b])[None], 0, place, F32, after=[token])
    rest, token = _gather_start("gather_start_1", [[slabs[u] for u in grp] for grp in gather_groups[1:]])
    in_flight += rest
    weights = {}

    def arrive(gi, after):
        ssem, rsem, arrs = in_flight[gi]
        for u, full in zip(gather_groups[gi], _gather_wait(f"gather_wait_{gi}", ssem, rsem, arrs, after)):
            weights[u] = full if u == "ln" else Weight(full, units[u][2])

    w_sp = sgu_w_spatial[0]
    b_t = sgu_b_spatial[0].T
    tables = _rope_tables(positions[0])

    def residual(acc, extra):
        return [extra[0] + acc[0]]

    def memory_kv(layer):
        mem_n = _rms_fwd(f"mem_norm_{layer}", mem0, mem_norm[layer:layer + 1])
        return mem_n, _mm_nn(f"mem_kv_{layer}", mem_n, weights[f"w_mem_kv{layer}"])[0]

    h0 = _rms_fwd("mix_norm_0", x0, mix_norm[0:1], after=[token])
    arrive(0, [h0])
    proj0 = _mm_nn("attn_in", h0, weights["attn_w_in"])[0]
    arrive(1, [proj0])
    qkv = _rope_fwd(proj0, tables)
    qs, ks, vs = qkv[0:3], qkv[3:6], qkv[6:9]
    outs, lses = [], []
    for g in range(len(DILATIONS)):
        o, l = _dil_fwd(g, qs[g], ks[g], vs[g])
        outs.append(o)
        lses.append(l)
    merged, lse = _attn_merge(outs, lses)
    mem_n0, kv0 = memory_kv(0)
    cat0 = _mem_fwd("mem_fwd_0", proj0, 9, kv0, merged, 1)
    x1, hf0 = _mm_nn("attn_out", cat0, weights["attn_w_out"], extras=[x0], epilogue=residual, norm_gain=ffn_norm[0:1])
    arrive(2, [x1])
    g0, u0, act0 = _gate_up("gate_up_0", hf0, weights["w_gate0"], weights["w_up0"])
    arrive(3, [act0])
    x2 = _mm_nn("down_0", act0, weights["w_down0"], extras=[x1], epilogue=residual)[0]

    ln_all = weights["ln"]
    ln_g = ln_all[:, 0, :].reshape(1, SGU_W)
    ln_b = ln_all[:, 1, :].reshape(1, SGU_W)
    h1 = _rms_fwd("mix_norm_1", x2, mix_norm[1:2])
    proj1 = _mm_nn("sgu_in", h1, weights["sgu_w_in"], out_dtypes=(BF,))[0]
    sgu_out = _sgu_fwd(proj1, ln_g, ln_b, w_sp, b_t)
    mem_n1, kv1 = memory_kv(1)
    cat1 = _mem_fwd("mem_fwd_1", proj1, 6, kv1, sgu_out, 3)
    arrive(4, [cat1])
    x3, hf1 = _mm_nn("sgu_out", cat1, weights["sgu_w_out"], extras=[x2], epilogue=residual, norm_gain=ffn_norm[1:2])
    arrive(5, [x3])
    g1, u1, act1 = _gate_up("gate_up_1", hf1, weights["w_gate1"], weights["w_up1"])
    arrive(6, [act1])
    x4 = _mm_nn("down_1", act1, weights["w_down1"], extras=[x3], epilogue=residual)[0]

    d4, d4_op, g_final, loss_part = _final_loss(x4, tgt, final_norm.reshape(1, d_model))
    loss = lax.psum(loss_part[0, 0], ("x", "y", "c"))

    outputs = {}

    def start_reduce(tag, names, grads):
        ssem, rsem, grads, lands, tok = _reduce_start(f"reduce_start_{tag}", grads)
        return dict(tag=tag, names=names, ssem=ssem, rsem=rsem, grads=grads, lands=lands), tok

    def finish_reduce(st, after):
        grads, lands = _reduce_wait(f"reduce_wait_{st['tag']}", st["ssem"], st["rsem"], st["grads"], st["lands"], after)
        totals = [_sum_pieces(f"sum_{u}", g, l, place) for u, g, l in zip(st["names"], grads, lands)]
        ssem, rsem, totals, tok = _half_start(f"half_start_{st['tag']}", totals)
        return dict(tag=st["tag"], names=st["names"], ssem=ssem, rsem=rsem, totals=totals), tok

    def finish_update(st, after):
        totals = _half_wait(f"half_wait_{st['tag']}", st["ssem"], st["rsem"], st["totals"], after)
        for u, tot in zip(st["names"], totals):
            arr, layer, _ = units[u]
            w = given_w[arr]
            outputs[arr] = _adamw(f"adamw_{u}", w, tot.reshape(w.shape[1:]), given_m[arr], given_v[arr], layer,
                                  outputs.get(arr))

    def ffn_bwd(layer, d_out, d_out_op, xin, h, g, u, act):
        wd, wg, wu = weights[f"w_down{layer}"], weights[f"w_gate{layer}"], weights[f"w_up{layer}"]
        gr_down = _mm_tn(f"d_down_{layer}", act, d_out_op, wd)
        dg, du = _mm_nt(f"d_act_{layer}", [d_out_op], [wd], out_dtypes=(BF, BF), extras=[g, u],
                        epilogue=_swiglu_bwd_epilogue, col_chunk=EPILOGUE_CHUNK)
        gr_gate = _mm_tn(f"d_gate_{layer}", h, dg, wg)
        gr_up = _mm_tn(f"d_up_{layer}", h, du, wu)
        st, tok = start_reduce(f"ffn{layer}", [f"w_down{layer}", f"w_gate{layer}", f"w_up{layer}"], [gr_down, gr_gate, gr_up])
        dh = _mm_nt(f"d_ffn_h_{layer}", [dg, du], [wg, wu], out_dtypes=(BF,), after=[tok])[0]
        d_in, d_in_op, g_norm = _rms_bwd(f"ffn_norm_bwd_{layer}", xin, ffn_norm[layer:layer + 1], dh, d_out)
        return st, d_in, d_in_op, g_norm

    def memory_bwd(layer, mem_n, dkv):
        dkv = dkv.astype(BF)
        wkv = weights[f"w_mem_kv{layer}"]
        gr = _mm_tn(f"d_mem_kv_{layer}", mem_n, dkv, wkv)
        d_mem_n = _mm_nt(f"d_mem_n_{layer}", [dkv], [wkv])[0]
        return gr, _rms_bwd(f"mem_norm_bwd_{layer}", mem0, mem_norm[layer:layer + 1], d_mem_n)[2]

    st_ffn1, d3, d3_op, g_ffn1 = ffn_bwd(1, d4, d4_op, x3, hf1, g1, u1, act1)
    gr_sgu_out = _mm_tn("d_sgu_out", cat1, d3_op, weights["sgu_w_out"])
    dcat1 = _mm_nt("d_cat_1", [d3_op], [weights["sgu_w_out"]], out_dtypes=(BF,))[0]
    st_ffn1, tok = finish_reduce(st_ffn1, [dcat1])
    dproj1, dkv1 = _mem_bwd("mem_bwd_1", proj1, 6, kv1, dcat1, 3, proj1.shape[1])
    gr_kv1, g_mem1 = memory_bwd(1, mem_n1, dkv1)
    dproj1, g_wsp, g_bsp_t, g_ln_g, g_ln_b = _sgu_bwd(proj1, dcat1, ln_g, ln_b, w_sp, b_t, dproj1)
    gr_sgu_in = _mm_tn("d_sgu_in", h1, dproj1, weights["sgu_w_in"], after=[tok])
    finish_update(st_ffn1, [gr_sgu_in])
    st_mix1, tok = start_reduce("mix1", ["sgu_w_out", "w_mem_kv1", "sgu_w_in"], [gr_sgu_out, gr_kv1, gr_sgu_in])
    dh1 = _mm_nt("d_h_1", [dproj1], [weights["sgu_w_in"]], out_dtypes=(BF,), after=[tok])[0]
    d2, d2_op, g_mix1 = _rms_bwd("mix_norm_bwd_1", x2, mix_norm[1:2], dh1, d3)

    st_ffn0, d1, d1_op, g_ffn0 = ffn_bwd(0, d2, d2_op, x1, hf0, g0, u0, act0)
    dev = 4 * xi + 2 * yi + ci
    small_a = [g_mix1, g_mem1, jnp.concatenate([g_ffn0, g_ffn1]), g_wsp, g_bsp_t[:, :SGU_GROUPS].T, g_final, g_ln_g, g_ln_b]
    sa_ssem, sa_rsem, sa_slots, tok = _small_start("small_start_a", _own_slot(_pack(small_a, LANES), dev))
    gr_attn_out = _mm_tn("d_attn_out", cat0, d1_op, weights["attn_w_out"], after=[tok])
    st_mix1, tok = finish_reduce(st_mix1, [gr_attn_out])
    dcat0 = _mm_nt("d_cat_0", [d1_op], [weights["attn_w_out"]], after=[tok])[0]
    dproj0, dkv0 = _mem_bwd("mem_bwd_0", proj0, 9, kv0, dcat0, 1, proj0.shape[1])
    finish_update(st_mix1, [dkv0])
    gr_kv0, g_mem0 = memory_bwd(0, mem_n0, dkv0)
    st_ffn0, tok = finish_reduce(st_ffn0, [g_mem0])
    d_merged, delta = _attn_delta(dcat0, cat0, after=[tok])
    dqs, dks, dvs = [], [], []
    for g in range(len(DILATIONS)):
        dq, dk, dv = _dil_bwd(g, qs[g], ks[g], vs[g], d_merged[g], lse[g], delta[g])
        dqs.append(dq)
        dks.append(dk)
        dvs.append(dv)
    dproj0 = _rope_bwd(dqs + dks + dvs, tables, dproj0)
    finish_update(st_ffn0, [dproj0])
    gr_attn_in = _mm_tn("d_attn_in", h0, dproj0, weights["attn_w_in"])
    st_mix0, tok = start_reduce("mix0", ["attn_w_out", "w_mem_kv0", "attn_w_in"], [gr_attn_out, gr_kv0, gr_attn_in])
    dh0 = _mm_nt("d_h_0", [dproj0], [weights["attn_w_in"]], out_dtypes=(BF,), after=[tok])[0]
    d0, _, g_mix0 = _rms_bwd("mix_norm_bwd_0", x0, mix_norm[0:1], dh0, d1)

    small_b = [g_mix0, g_mem0]
    sb_ssem, sb_rsem, sb_slots, tok = _small_start("small_start_b", _own_slot(_pack(small_b, 8), dev))
    sa_slots = _small_wait("small_wait_a", sa_ssem, sa_rsem, sa_slots, [tok])
    g_mix1, g_mem1, g_ffn, g_wsp, g_bsp, g_final, g_ln_g, g_ln_b = _unpack(_sum_devices("small_sum_a", sa_slots),
                                                                           [t.shape for t in small_a])
    sb_slots = _small_wait("small_wait_b", sb_ssem, sb_rsem, sb_slots, [g_final])
    g_mix0, g_mem0 = _unpack(_sum_devices("small_sum_b", sb_slots), [t.shape for t in small_b])
    st_mix0, tok = finish_reduce(st_mix0, [g_mix0])
    g_mix, g_mem = jnp.concatenate([g_mix0, g_mix1]), jnp.concatenate([g_mem0, g_mem1])
    shard_w = sgu_ln_g.shape[-1]
    g_ln_g = lax.dynamic_slice_in_dim(g_ln_g, chip * shard_w, shard_w, axis=1)
    g_ln_b = lax.dynamic_slice_in_dim(g_ln_b, chip * shard_w, shard_w, axis=1)
    small_names = ["mix_norm", "mem_norm", "ffn_norm", "sgu_w_spatial", "sgu_b_spatial", "final_norm", "sgu_ln_g",
                   "sgu_ln_b"]
    small_g = [g_mix, g_mem, g_ffn, g_wsp, g_bsp, g_final, g_ln_g, g_ln_b]
    small_shapes = [given_w[k].shape for k in small_names]
    packed = [_pack(t, LANES) for t in ([given_w[k] for k in small_names], small_g, [given_m[k] for k in small_names],
                                    [given_v[k] for k in small_names])]
    small_out = _adamw("adamw_small", packed[0][None], packed[1], packed[2][None], packed[3][None], 0)
    finish_update(st_mix0, [small_out[0]])
    for k, gk, dk, mk, vk in zip(small_names, *[_unpack(t[0], small_shapes) for t in small_out]):
        outputs[k] = (gk, dk, mk, vk)

    order = ["mix_norm", "mem_norm", "w_mem_kv", "ffn_norm", "w_gate", "w_up", "w_down", "attn_w_in", "attn_w_out",
             "sgu_w_in", "sgu_ln_g", "sgu_ln_b", "sgu_w_spatial", "sgu_b_spatial", "sgu_w_out", "final_norm"]
    return (loss, d0[None], *[outputs[k][0] for k in order], *[outputs[k][1] for k in order],
            *[outputs[k][2] for k in order], *[outputs[k][3] for k in order])
```

```python
import math

import jax
import jax.numpy as jnp
from jax import lax
from jax.experimental import pallas as pl
from jax.experimental.pallas import tpu as pltpu

F32 = jnp.float32
BF = jnp.bfloat16
MESH = pl.DeviceIdType.MESH

HEAD_DIM = 128
MEM_HEADS = 4
MEM_W = MEM_HEADS * HEAD_DIM
GROUP_W = 4 * HEAD_DIM
DILATIONS = (1, 4, 16)
BLK = 128
SGU_GROUPS = 12
SGU_W = SGU_GROUPS * HEAD_DIM
ROT_HALF = 16
ROPE_THETA = 500000.0
NORM_EPS = 1e-6
LN_EPS = 1e-5
NEG_INF = -1e30
SCALE = HEAD_DIM ** -0.5
ADAM_LR, ADAM_B1, ADAM_B2, ADAM_EPS, ADAM_WD, ADAM_STEP = 0.001, 0.9, 0.999, 1e-08, 0.01, 10

VMEM_LIMIT = 48 * 2 ** 20
VMEM_TILE_BUDGET = 38 * 2 ** 20
N_CHIPS = 4
N_DEV = 8
LANES = 128
EPILOGUE_CHUNK = 256

NT_DIMS = (((1,), (1,)), ((), ()))
TN_DIMS = (((0,), (0,)), ((), ()))
NN_DIMS = (((1,), (0,)), ((), ()))

ANY = pl.BlockSpec(memory_space=pl.ANY)
HBM = pl.BlockSpec(memory_space=pltpu.HBM)
SEM = pl.BlockSpec(memory_space=pltpu.SEMAPHORE)
EFFECT = pltpu.SideEffectType.DATAFLOW_SIDE_EFFECTING


def _params(sem):
    return pltpu.CompilerParams(dimension_semantics=sem, vmem_limit_bytes=VMEM_LIMIT)


def _pick(n, cap):
    if n <= cap:
        return n
    best = None
    for t in range(LANES, cap + 1, LANES):
        if n % t == 0:
            best = t
    assert best is not None, (n, cap)
    return best


def _pick_rows(n, cap):
    t = min(n, cap)
    while n % t:
        t //= 2
    return t


def _mm(name, dims, a_list, a_specs, b_list, b_specs, pairs, n_acc, acc_shape, grid, extras, e_specs,
        out_shapes, out_specs, epilogue, after=(), col_chunk=None, store=None, shard_width=None, norm_gain=None):
    na, nb, ne, no = len(a_list), len(b_list), len(extras), len(out_shapes)
    nk = grid[-1]
    ng = 0 if norm_gain is None else 1

    def products(a, b, cols=None):
        sums = [None] * n_acc
        for ai, bi, ci in pairs:
            bv = b[bi]
            if cols is None:
                bv = bv[...]
            elif dims == NT_DIMS:
                bv = bv[cols, :]
            else:
                bv = bv[:, cols]
            if bv.ndim == 3:
                bv = bv.reshape(-1, bv.shape[-1])
            prod = lax.dot_general(a[ai][...].astype(BF), bv.astype(BF), dims, preferred_element_type=F32)
            sums[ci] = prod if sums[ci] is None else sums[ci] + prod
        return sums

    def body(*refs):
        a = refs[:na]
        b = refs[na:na + nb]
        e = refs[na + nb:na + nb + ne]
        off = na + nb + ne + ng + len(after)
        o = refs[off:off + no]
        acc = refs[off + no:]

        def normed():
            if ng:
                xf = o[0][...]
                r = lax.rsqrt(jnp.mean(xf * xf, axis=-1, keepdims=True) + NORM_EPS)
                o[-1][...] = (xf * r * refs[na + nb + ne][...]).astype(o[-1].dtype)

        def finish(sums):
            outs = epilogue(sums, [r[...] for r in e])
            if store is not None:
                store(o, outs)
                return
            for r, v in zip(o, outs):
                r[...] = v.astype(r.dtype)
            normed()

        if nk == 1 and shard_width:
            (ai, bi, _), = pairs
            av = a[ai][...].astype(BF)
            if dims == NT_DIMS:
                total = None
                for j in range(N_CHIPS):
                    cols = slice(j * shard_width, (j + 1) * shard_width)
                    prod = lax.dot_general(av[:, cols], b[bi][j].astype(BF), dims, preferred_element_type=F32)
                    total = prod if total is None else total + prod
                finish([total])
                return
            for j in range(N_CHIPS):
                cols = slice(j * shard_width, (j + 1) * shard_width)
                prod = lax.dot_general(av, b[bi][j].astype(BF), dims, preferred_element_type=F32)
                outs = epilogue([prod], [r[:, cols] for r in e])
                for r, v in zip(o, outs):
                    r[:, cols] = v.astype(r.dtype)
            normed()
            return
        if nk == 1 and col_chunk:
            width = acc_shape[1]
            left = [r[...].astype(BF) for r in a]
            for c0 in range(0, width, col_chunk):
                cols = slice(c0, min(c0 + col_chunk, width))
                outs = epilogue(products(left, b, cols), [r[:, cols] for r in e])
                for r, v in zip(o, outs):
                    r[:, cols] = v.astype(r.dtype)
            return
        if nk == 1:
            finish(products(a, b))
            return
        k = pl.program_id(len(grid) - 1)

        @pl.when(k == 0)
        def _():
            for c, v in zip(acc, products(a, b)):
                c[...] = v

        @pl.when(jnp.logical_and(k > 0, k < nk - 1))
        def _():
            for c, v in zip(acc, products(a, b)):
                c[...] += v

        @pl.when(k == nk - 1)
        def _():
            finish([c[...] + v for c, v in zip(acc, products(a, b))])

    gains = [] if norm_gain is None else [norm_gain]
    ins = [*a_list, *b_list, *extras, *gains, *after]
    in_specs = [*a_specs, *b_specs, *e_specs, *[pl.BlockSpec(g.shape, lambda *_: (0, 0)) for g in gains],
                *([ANY] * len(after))]
    sem = ("parallel",) * (len(grid) - 1) + ("arbitrary",)
    scratch = [] if nk == 1 else [pltpu.VMEM(acc_shape, F32)] * n_acc
    return pl.pallas_call(
        body, out_shape=out_shapes, grid=grid, in_specs=in_specs, out_specs=out_specs, scratch_shapes=scratch,
        name=name, compiler_params=_params(sem))(*ins)


def _tile_bytes(blocks, single=()):
    size = lambda s, d: math.prod(s) * jnp.dtype(d).itemsize
    return sum(2 * size(s, d) for s, d in blocks) + sum(size(s, d) for s, d in single)


def _first(acc, extra):
    return [acc[0]]


def _sigmoid(x):
    return 0.5 * (1.0 + jnp.tanh(0.5 * x))


class Weight:
    def __init__(self, arr, axis):
        self.arr, self.axis = arr, axis
        _, self.rows, self.cols = arr.shape


SMALL_WEIGHT_BYTES = 8 * 2 ** 20


def _is_small(w):
    return w.arr.size * w.arr.dtype.itemsize <= SMALL_WEIGHT_BYTES


def _mm_nn(name, a, w, extras=(), epilogue=_first, out_dtypes=(F32,), after=(), norm_gain=None):
    m, kdim = a.shape
    b_spec, shard_width = None, None
    weight_buffers = 2
    if norm_gain is not None:
        out_dtypes = (*out_dtypes, BF)
    if w.axis == "col" and _is_small(w):
        n_total = tn = N_CHIPS * w.cols
        tk, gn, gk = kdim, 1, 1
        shard_width = w.cols
        b_spec = pl.BlockSpec((N_CHIPS, kdim, w.cols), lambda n, i, k: (0, 0, 0))
    elif w.axis == "col":
        n_total = N_CHIPS * w.cols
        tn = _pick(w.cols, 1408)
        tk = _pick(kdim, 2048)
        ncb = w.cols // tn
        gn, gk = N_CHIPS * ncb, kdim // tk
        b_map = lambda n, i, k: (n // ncb, k, n % ncb)
    elif kdim <= 2048 and norm_gain is not None:
        n_total = tn = w.cols
        tk, gn, gk = kdim, 1, 1
        weight_buffers = 1
        b_spec = pl.BlockSpec(w.arr.shape, lambda n, i, k: (0, 0, 0), pipeline_mode=pl.Buffered(1))
    elif kdim <= 2048:
        n_total = w.cols
        tn = _pick(w.cols, 1024)
        tk = kdim
        gn, gk = n_total // tn, 1
        b_spec = pl.BlockSpec((N_CHIPS, w.rows, tn), lambda n, i, k: (0, 0, n))
    else:
        n_total = w.cols
        tn = _pick(w.cols, 1024)
        tk = _pick(w.rows, 1408)
        nkb = w.rows // tk
        gn, gk = n_total // tn, N_CHIPS * nkb
        b_map = lambda n, i, k: (k // nkb, k % nkb, n)
    if b_spec is None:
        b_spec = pl.BlockSpec((None, tk, tn), b_map)
    for tm in (1024, 512, 256, 128):
        if m % tm:
            continue
        blocks = [((tm, tk), a.dtype)] + [((tm, tn), e.dtype) for e in extras]
        blocks += [((tm, tn), d) for d in out_dtypes] + [((tm, tn), BF)]
        weight = [((tk, tn), BF)]
        if _tile_bytes(blocks + (weight if weight_buffers == 2 else []), weight if weight_buffers == 1 else ()) <= VMEM_TILE_BUDGET:
            break
    assert norm_gain is None or tn == n_total, name
    o_spec = pl.BlockSpec((tm, tn), lambda n, i, k: (i, n))
    return _mm(
        name, NN_DIMS, [a], [pl.BlockSpec((tm, tk), lambda n, i, k: (i, k))],
        [w.arr], [b_spec], [(0, 0, 0)], 1, (tm, tn), (gn, m // tm, gk),
        list(extras), [o_spec] * len(extras),
        [jax.ShapeDtypeStruct((m, n_total), d) for d in out_dtypes], [o_spec] * len(out_dtypes), epilogue, after,
        shard_width=shard_width, norm_gain=norm_gain)


def _gate_up(name, h, wg, wu):
    m, kdim = h.shape
    tn = _pick(wg.cols, 1408)
    tk = _pick(kdim, 2048)
    ncb = wg.cols // tn
    single = kdim == tk
    for tm in (1024, 512, 256, 128):
        blocks = [((tm, tk), BF)] + [((tm, tn), BF)] * 3
        weights = [((tk, tn), BF)] * 2
        if m % tm == 0 and _tile_bytes(blocks + ([] if single else weights), weights if single else ()) <= VMEM_TILE_BUDGET:
            break
    b_spec = pl.BlockSpec((None, tk, tn), lambda n, i, k: (n // ncb, k, n % ncb),
                          pipeline_mode=pl.Buffered(1) if single else None)
    o_spec = pl.BlockSpec((tm, tn), lambda n, i, k: (i, n))
    n_total = N_CHIPS * wg.cols

    def epilogue(acc, extra):
        g, u = acc
        return [g, u, g * _sigmoid(g) * u]

    return _mm(
        name, NN_DIMS, [h], [pl.BlockSpec((tm, tk), lambda n, i, k: (i, k))], [wg.arr, wu.arr], [b_spec, b_spec],
        [(0, 0, 0), (0, 1, 1)], 2, (tm, tn), (N_CHIPS * ncb, m // tm, kdim // tk), [], [],
        [jax.ShapeDtypeStruct((m, n_total), BF)] * 3, [o_spec] * 3, epilogue, col_chunk=EPILOGUE_CHUNK)


def _mm_nt(name, dys, ws, out_dtypes=(F32,), extras=(), epilogue=_first, after=(), col_chunk=None):
    m = dys[0].shape[0]
    w0 = ws[0]
    npair = len(dys)
    b_spec, shard_width = None, None
    if w0.axis == "col" and npair == 1 and _is_small(w0):
        k_total = tko = w0.rows
        tkc = N_CHIPS * w0.cols
        go, gk = 1, 1
        shard_width = w0.cols
        b_spec = pl.BlockSpec(w0.arr.shape, lambda o, i, k: (0, 0, 0))
    elif w0.axis == "col":
        k_total = w0.rows
        tko = _pick(k_total, 1024)
        tkc = _pick(w0.cols, 1408)
        nkb = w0.cols // tkc
        go, gk = k_total // tko, N_CHIPS * nkb
        b_map = lambda o, i, k: (k // nkb, o, k % nkb)
    else:
        k_total = N_CHIPS * w0.rows
        tko = _pick(w0.rows, 1408)
        tkc = _pick(w0.cols, 2048)
        nob = w0.rows // tko
        go, gk = N_CHIPS * nob, w0.cols // tkc
        b_map = lambda o, i, k: (o // nob, o % nob, k)
    single = gk == 1
    for tm in (1024, 512, 256, 128):
        if m % tm:
            continue
        blocks = [((tm, tkc), d.dtype) for d in dys]
        blocks += [((tm, tko), e.dtype) for e in extras] + [((tm, tko), d) for d in out_dtypes]
        blocks += [((tm, tko), BF)]
        weights = [((tko, tkc), BF)] * npair
        if _tile_bytes(blocks + ([] if single else weights), weights if single else ()) <= VMEM_TILE_BUDGET:
            break
    if b_spec is None:
        b_spec = pl.BlockSpec((None, tko, tkc), b_map, pipeline_mode=pl.Buffered(1) if single else None)
    o_spec = pl.BlockSpec((tm, tko), lambda o, i, k: (i, o))
    return _mm(
        name, NT_DIMS, list(dys), [pl.BlockSpec((tm, tkc), lambda o, i, k: (i, k))] * npair,
        [w.arr for w in ws], [b_spec] * npair,
        [(i, i, 0) for i in range(npair)], 1, (tm, tko), (go, m // tm, gk), list(extras), [o_spec] * len(extras),
        [jax.ShapeDtypeStruct((m, k_total), d) for d in out_dtypes], [o_spec] * len(out_dtypes), epilogue, after,
        col_chunk if gk == 1 and shard_width is None else None, shard_width=shard_width)


def _mm_tn(name, a, dy, w, after=()):
    m, k_total = a.shape
    rows2 = w.rows // 2
    tn = _pick(w.cols, 1408)
    ncb = w.cols // tn
    epilogue, store = _first, None
    if k_total <= 2048 and w.axis == "col" and _is_small(w):
        tkr, tn = k_total, N_CHIPS * w.cols
        gr, gn = 1, 1
        o_spec = pl.BlockSpec((2, N_CHIPS, rows2, w.cols), lambda r, n, t: (0, 0, 0, 0))

        def store(o_refs, outs):
            for j in range(N_CHIPS):
                for h in range(2):
                    o_refs[0][h, j] = outs[0][h * rows2:(h + 1) * rows2, j * w.cols:(j + 1) * w.cols].astype(BF)
    elif k_total <= 2048 and w.axis == "col":
        tkr = k_total
        gr, gn = 1, N_CHIPS * ncb
        o_spec = pl.BlockSpec((2, None, rows2, tn), lambda r, n, t: (0, n // ncb, 0, n % ncb))
        epilogue = lambda acc, extra: [acc[0].reshape(2, rows2, tn)]
    elif k_total <= 2048:
        tkr = k_total
        gr, gn = 1, ncb
        o_spec = pl.BlockSpec((2, N_CHIPS, rows2, tn), lambda r, n, t: (0, 0, 0, n))

        def store(o_refs, outs):
            for j in range(N_CHIPS):
                for h in range(2):
                    lo = (2 * j + h) * rows2
                    o_refs[0][h, j] = outs[0][lo:lo + rows2].astype(BF)
    elif rows2 % LANES:
        tkr = w.rows
        assert w.axis == "row"
        gr, gn = N_CHIPS, ncb
        o_spec = pl.BlockSpec((2, None, rows2, tn), lambda r, n, t: (0, r, 0, n))
        epilogue = lambda acc, extra: [acc[0].reshape(2, rows2, tn)]
    else:
        tkr = _pick(rows2, 1408)
        nrb = rows2 // tkr
        if w.axis == "col":
            gr, gn = w.rows // tkr, N_CHIPS * ncb
            o_map = lambda r, n, t: (r // nrb, n // ncb, r % nrb, n % ncb)
        else:
            per = w.rows // tkr
            gr, gn = N_CHIPS * per, ncb
            o_map = lambda r, n, t: ((r % per) // nrb, r // per, (r % per) % nrb, n)
        o_spec = pl.BlockSpec((None, None, tkr, tn), o_map)
    for tmk in (1024, 512, 256, 128):
        blocks = [((tmk, tkr), a.dtype), ((tmk, tn), dy.dtype), ((tkr, tn), BF), ((tkr, tn), BF)]
        if m % tmk == 0 and _tile_bytes(blocks) <= VMEM_TILE_BUDGET:
            break
    return _mm(
        name, TN_DIMS, [a], [pl.BlockSpec((tmk, tkr), lambda r, n, t: (t, r))],
        [dy], [pl.BlockSpec((tmk, tn), lambda r, n, t: (t, n))], [(0, 0, 0)], 1, (tkr, tn), (gr, gn, m // tmk), [], [],
        [jax.ShapeDtypeStruct((2, N_CHIPS, rows2, w.cols), BF)], [o_spec], epilogue, after, store=store)[0]


def _rms_fwd(name, x, g, after=()):
    s, d = x.shape
    tr = _pick_rows(s, 512)

    def body(x_ref, g_ref, *rest):
        h_ref = rest[-1]
        xf = x_ref[...]
        r = lax.rsqrt(jnp.mean(xf * xf, axis=-1, keepdims=True) + NORM_EPS)
        h_ref[...] = (xf * r * g_ref[...]).astype(BF)

    return pl.pallas_call(
        body, out_shape=jax.ShapeDtypeStruct((s, d), BF), grid=(s // tr,),
        in_specs=[pl.BlockSpec((tr, d), lambda i: (i, 0)), pl.BlockSpec((1, d), lambda i: (0, 0))] + [ANY] * len(after),
        out_specs=pl.BlockSpec((tr, d), lambda i: (i, 0)), name=name, compiler_params=_params(("parallel",)))(x, g, *after)


def _rms_bwd(name, x, g, dh, dres=None):
    s, d = x.shape
    tr = _pick_rows(s, 256)
    has_res = dres is not None

    def body(*refs):
        if has_res:
            x_ref, g_ref, dh_ref, dres_ref, dx_ref, dxb_ref, dg_ref = refs
        else:
            x_ref, g_ref, dh_ref, dx_ref, dxb_ref, dg_ref = refs
        xf = x_ref[...]
        r = lax.rsqrt(jnp.mean(xf * xf, axis=-1, keepdims=True) + NORM_EPS)
        xr = xf * r
        dy = dh_ref[...].astype(F32)
        a = dy * g_ref[...]
        dx = r * (a - xr * jnp.mean(a * xr, axis=-1, keepdims=True))
        if has_res:
            dx = dx + dres_ref[...]
        dx_ref[...] = dx
        dxb_ref[...] = dx.astype(BF)

        @pl.when(pl.program_id(0) == 0)
        def _():
            dg_ref[...] = jnp.zeros_like(dg_ref)

        dg_ref[...] += jnp.sum(dy * xr, axis=0, keepdims=True)

    row = pl.BlockSpec((tr, d), lambda i: (i, 0))
    vec = pl.BlockSpec((1, d), lambda i: (0, 0))
    ins = [x, g, dh] + ([dres] if has_res else [])
    in_specs = [row, vec, row] + ([row] if has_res else [])
    return pl.pallas_call(
        body, out_shape=[jax.ShapeDtypeStruct((s, d), F32), jax.ShapeDtypeStruct((s, d), BF),
                         jax.ShapeDtypeStruct((1, d), F32)],
        grid=(s // tr,), in_specs=in_specs, out_specs=[row, row, vec], name=name,
        compiler_params=_params(("arbitrary",)))(*ins)


def _final_loss(x, tgt, g):
    s, d = x.shape
    tr = _pick_rows(s, 256)

    def body(x_ref, t_ref, g_ref, dx_ref, dxb_ref, dg_ref, loss_ref):
        xf = x_ref[...]
        gain = g_ref[...]
        r = lax.rsqrt(jnp.mean(xf * xf, axis=-1, keepdims=True) + NORM_EPS)
        xr = xf * r
        err = xr * gain - t_ref[...]
        dy = err * (1.0 / d)
        a = dy * gain
        dx = r * (a - xr * jnp.mean(a * xr, axis=-1, keepdims=True))
        dx_ref[...] = dx
        dxb_ref[...] = dx.astype(BF)

        @pl.when(pl.program_id(0) == 0)
        def _():
            dg_ref[...] = jnp.zeros_like(dg_ref)
            loss_ref[...] = jnp.zeros_like(loss_ref)

        dg_ref[...] += jnp.sum(dy * xr, axis=0, keepdims=True)
        part = 0.5 * jnp.sum(jnp.mean(err * err, axis=-1, keepdims=True), axis=0, keepdims=True)
        loss_ref[...] += jnp.broadcast_to(part, loss_ref.shape)

    row = pl.BlockSpec((tr, d), lambda i: (i, 0))
    vec = pl.BlockSpec((1, d), lambda i: (0, 0))
    return pl.pallas_call(
        body, out_shape=[jax.ShapeDtypeStruct((s, d), F32), jax.ShapeDtypeStruct((s, d), BF),
                         jax.ShapeDtypeStruct((1, d), F32), jax.ShapeDtypeStruct((8, LANES), F32)],
        grid=(s // tr,), in_specs=[row, row, vec],
        out_specs=[row, row, vec, pl.BlockSpec((8, LANES), lambda i: (0, 0))],
        name="final_loss", compiler_params=_params(("arbitrary",)))(x, tgt, g)


def _swiglu_bwd_epilogue(acc, extra):
    dact = acc[0]
    g, u = extra[0].astype(F32), extra[1].astype(F32)
    sig = _sigmoid(g)
    return [dact * u * sig * (1.0 + g * (1.0 - sig)), dact * g * sig]


GELU_C = math.sqrt(2.0 / math.pi)
GELU_A = 0.044715


def _gelu(x):
    return 0.5 * x * (1.0 + jnp.tanh(GELU_C * (x + GELU_A * x * x * x)))


def _gelu_both(x):
    x2 = x * x
    t = jnp.tanh(GELU_C * (x + GELU_A * x2 * x))
    half = 0.5 * (1.0 + t)
    return x * half, half + 0.5 * x * (1.0 - t * t) * GELU_C * (1.0 + 3.0 * GELU_A * x2)


def _rope_tables(positions):
    inv_freq = ROPE_THETA ** (-jnp.arange(ROT_HALF, dtype=F32) / ROT_HALF)
    ang = positions.astype(F32)[:, None] * inv_freq
    cos, sin = jnp.cos(ang), jnp.sin(ang)
    s = ang.shape[0]
    rest = HEAD_DIM - 2 * ROT_HALF
    zeros = jnp.zeros((s, ROT_HALF), F32)
    cos_t = jnp.concatenate([cos, cos, jnp.ones((s, rest), F32)], axis=1)
    sin_a = jnp.concatenate([-sin, zeros, jnp.zeros((s, rest), F32)], axis=1)
    sin_b = jnp.concatenate([zeros, sin, jnp.zeros((s, rest), F32)], axis=1)
    return cos_t, sin_a, sin_b


def _rope_head(xh, cos_t, sin_a, sin_b):
    up = pltpu.roll(xh, HEAD_DIM - ROT_HALF, 1)
    down = pltpu.roll(xh, ROT_HALF, 1)
    return xh * cos_t + up * sin_a + down * sin_b


def _residue(r, rows, dil):
    return slice(None) if dil == 1 else pl.ds(r, rows, stride=dil)


ROPE_TILE = 256
N_PARTS = 9
HEADS_PER_GROUP = GROUP_W // HEAD_DIM
N_HEADS_IN = N_PARTS * HEADS_PER_GROUP


def _rope_fwd(proj, tables):
    s = proj.shape[0]
    tm = _pick_rows(s, ROPE_TILE)

    def body(*refs):
        heads = refs[:N_HEADS_IN]
        c_ref, sa_ref, sb_ref = refs[N_HEADS_IN:N_HEADS_IN + 3]
        outs = refs[N_HEADS_IN + 3:]
        for g, dil in enumerate(DILATIONS):
            rows = tm // dil
            for r in range(dil):
                rs = _residue(r, rows, dil)
                cos_t, sin_a, sin_b = c_ref[rs, :], sa_ref[rs, :], sb_ref[rs, :]
                for kind in range(3):
                    part = 3 * kind + g
                    for h in range(HEADS_PER_GROUP):
                        xh = heads[part * HEADS_PER_GROUP + h][rs, :]
                        if kind < 2:
                            xh = _rope_head(xh, cos_t, sin_a, sin_b)
                        outs[part][r, :, h * HEAD_DIM:(h + 1) * HEAD_DIM] = xh.astype(BF)

    tab = pl.BlockSpec((tm, HEAD_DIM), lambda i: (i, 0))
    head_specs = [pl.BlockSpec((tm, HEAD_DIM), lambda i, j=j: (i, j)) for j in range(N_HEADS_IN)]
    shapes, specs = [], []
    for part in range(N_PARTS):
        dil = DILATIONS[part % 3]
        shapes.append(jax.ShapeDtypeStruct((dil, s // dil, GROUP_W), BF))
        specs.append(pl.BlockSpec((dil, tm // dil, GROUP_W), lambda i: (0, i, 0)))
    return pl.pallas_call(
        body, out_shape=shapes, grid=(s // tm,), in_specs=head_specs + [tab, tab, tab], out_specs=specs,
        name="rope_fwd", compiler_params=_params(("parallel",)))(*([proj] * N_HEADS_IN), *tables)


def _rope_bwd(parts, tables, into):
    s = into.shape[0]
    tm = _pick_rows(s, ROPE_TILE)

    def body(*refs):
        ins = refs[:N_PARTS]
        c_ref, sa_ref, sb_ref, into_ref, o_ref, scr = refs[N_PARTS:]
        for g, dil in enumerate(DILATIONS):
            rows = tm // dil
            for r in range(dil):
                rs = _residue(r, rows, dil)
                cos_t, sin_a, sin_b = c_ref[rs, :], -sa_ref[rs, :], -sb_ref[rs, :]
                for kind in range(3):
                    part = 3 * kind + g
                    for h in range(HEADS_PER_GROUP):
                        xh = ins[part][r, :, h * HEAD_DIM:(h + 1) * HEAD_DIM]
                        if kind < 2:
                            xh = _rope_head(xh, cos_t, sin_a, sin_b)
                        scr[part * HEADS_PER_GROUP + h, rs, :] = xh
        for j in range(N_HEADS_IN):
            o_ref[:, j * HEAD_DIM:(j + 1) * HEAD_DIM] = scr[j].astype(BF)

    tab = pl.BlockSpec((tm, HEAD_DIM), lambda i: (i, 0))
    i_specs = [pl.BlockSpec((DILATIONS[p % 3], tm // DILATIONS[p % 3], GROUP_W), lambda i: (0, i, 0))
               for p in range(N_PARTS)]
    return pl.pallas_call(
        body, out_shape=jax.ShapeDtypeStruct(into.shape, into.dtype), grid=(s // tm,),
        in_specs=i_specs + [tab] * 3 + [ANY], out_specs=pl.BlockSpec((tm, N_PARTS * GROUP_W), lambda i: (i, 0)),
        scratch_shapes=[pltpu.VMEM((N_HEADS_IN, tm, HEAD_DIM), F32)], input_output_aliases={N_PARTS + 3: 0},
        name="rope_bwd", compiler_params=_params(("parallel",)))(*parts, *tables, into)


def _band_mask(n):
    qi = lax.broadcasted_iota(jnp.int32, (BLK, 2 * BLK), 0)
    ki = lax.broadcasted_iota(jnp.int32, (BLK, 2 * BLK), 1)
    prev = jnp.logical_and(jnp.logical_and(ki < BLK, ki >= qi), n > 0)
    return jnp.logical_or(prev, jnp.logical_and(ki >= BLK, qi >= ki - BLK))


Q_BLOCKS = 2
Q_ROWS = Q_BLOCKS * BLK


STAT_LANES = HEAD_DIM // HEADS_PER_GROUP


def _stat_of(ref, rows, h):
    return ref[rows, h * STAT_LANES:h * STAT_LANES + 1]


def _pack_stats(cols):
    rows = cols[0].shape[0]
    lane = lax.broadcasted_iota(jnp.int32, (rows, HEAD_DIM), 1)
    tile = jnp.broadcast_to(cols[-1], (rows, HEAD_DIM))
    for h in range(HEADS_PER_GROUP - 2, -1, -1):
        tile = jnp.where(lane < (h + 1) * STAT_LANES, cols[h], tile)
    return tile


def _dil_specs(n_steps):
    last = n_steps - 1
    own = pl.BlockSpec((None, Q_ROWS, GROUP_W), lambda r, n: (r, jnp.minimum(n, last), 0))
    before = pl.BlockSpec((None, BLK, GROUP_W), lambda r, n: (r, jnp.maximum(Q_BLOCKS * n - 1, 0), 0))
    stat = pl.BlockSpec((None, Q_ROWS, HEAD_DIM), lambda r, n: (r, jnp.minimum(n, last), 0))
    return own, before, stat


def _dil_fwd(g, q, k, v):
    dil, length, _ = q.shape
    n_steps = length // Q_ROWS

    def body(q_ref, ko_ref, kb_ref, vo_ref, vb_ref, o_ref, lse_ref):
        n = pl.program_id(1)
        lse_cols = [[] for _ in range(Q_BLOCKS)]
        for h in range(GROUP_W // HEAD_DIM):
            sl = slice(h * HEAD_DIM, (h + 1) * HEAD_DIM)
            keys = jnp.concatenate([kb_ref[:, sl], ko_ref[:, sl]], axis=0)
            vals = jnp.concatenate([vb_ref[:, sl], vo_ref[:, sl]], axis=0)
            for j in range(Q_BLOCKS):
                rows, win = slice(j * BLK, (j + 1) * BLK), slice(j * BLK, (j + 2) * BLK)
                sc = lax.dot_general(q_ref[rows, sl], keys[win], NT_DIMS, preferred_element_type=F32) * SCALE
                sc = jnp.where(_band_mask(Q_BLOCKS * n + j), sc, NEG_INF)
                mx = jnp.max(sc, axis=-1, keepdims=True)
                p = jnp.exp(sc - mx)
                den = jnp.sum(p, axis=-1, keepdims=True)
                o_ref[rows, sl] = jnp.dot(p.astype(BF), vals[win], preferred_element_type=F32) / den
                lse_cols[j].append(mx + jnp.log(den))
        for j in range(Q_BLOCKS):
            lse_ref[j * BLK:(j + 1) * BLK, :] = _pack_stats(lse_cols[j])

    own, before, stat = _dil_specs(n_steps)
    return pl.pallas_call(
        body, out_shape=[jax.ShapeDtypeStruct(q.shape, F32), jax.ShapeDtypeStruct((dil, length, HEAD_DIM), F32)],
        grid=(dil, n_steps), in_specs=[own, own, before, own, before], out_specs=[own, stat], name=f"dil_fwd_{g}",
        compiler_params=_params(("parallel", "arbitrary")))(q, k, k, v, v)


def _dil_bwd(g, q, k, v, do, lse, delta):
    dil, length, _ = q.shape
    n_steps = length // Q_ROWS

    def body(q_ref, ko_ref, kb_ref, vo_ref, vb_ref, do_ref, lse_ref, dl_ref, dq_ref, dk_ref, dv_ref, ck_ref, cv_ref):
        n = pl.program_id(1)
        live = n < n_steps

        @pl.when(n == 0)
        def _():
            ck_ref[...] = jnp.zeros_like(ck_ref)
            cv_ref[...] = jnp.zeros_like(cv_ref)

        @pl.when(jnp.logical_not(live))
        def _():
            dk_ref[...] = ck_ref[...]
            dv_ref[...] = cv_ref[...]

        @pl.when(live)
        def _():
            for h in range(GROUP_W // HEAD_DIM):
                sl = slice(h * HEAD_DIM, (h + 1) * HEAD_DIM)
                keys = jnp.concatenate([kb_ref[:, sl], ko_ref[:, sl]], axis=0)
                vals = jnp.concatenate([vb_ref[:, sl], vo_ref[:, sl]], axis=0)
                dks, dvs = [], []
                for j in range(Q_BLOCKS):
                    rows, win = slice(j * BLK, (j + 1) * BLK), slice(j * BLK, (j + 2) * BLK)
                    qh, doh = q_ref[rows, sl], do_ref[rows, sl]
                    lse_h, dl_h = _stat_of(lse_ref, rows, h), _stat_of(dl_ref, rows, h)
                    sc = lax.dot_general(qh, keys[win], NT_DIMS, preferred_element_type=F32) * SCALE
                    p = jnp.where(_band_mask(Q_BLOCKS * n + j), jnp.exp(jnp.minimum(sc - lse_h, 0.0)), 0.0)
                    dp = lax.dot_general(doh, vals[win], NT_DIMS, preferred_element_type=F32)
                    ds = (p * (dp - dl_h) * SCALE).astype(BF)
                    dq_ref[rows, sl] = jnp.dot(ds, keys[win], preferred_element_type=F32)
                    dks.append(lax.dot_general(ds, qh, TN_DIMS, preferred_element_type=F32))
                    dvs.append(lax.dot_general(p.astype(BF), doh, TN_DIMS, preferred_element_type=F32))
                for out_ref, carry, parts in ((dk_ref, ck_ref, dks), (dv_ref, cv_ref, dvs)):
                    out_ref[:Q_ROWS - BLK, sl] = carry[:Q_ROWS - BLK, sl]
                    out_ref[Q_ROWS - BLK:, sl] = carry[Q_ROWS - BLK:, sl] + parts[0][:BLK]
                    for j in range(Q_BLOCKS - 1):
                        carry[j * BLK:(j + 1) * BLK, sl] = parts[j][BLK:] + parts[j + 1][:BLK]
                    carry[Q_ROWS - BLK:, sl] = parts[-1][BLK:]

    own, before, stat = _dil_specs(n_steps)
    behind = pl.BlockSpec((None, Q_ROWS, GROUP_W), lambda r, n: (r, jnp.maximum(n - 1, 0), 0))
    return pl.pallas_call(
        body, out_shape=[jax.ShapeDtypeStruct(q.shape, F32)] * 3, grid=(dil, n_steps + 1),
        in_specs=[own, own, before, own, before, own, stat, stat], out_specs=[own, behind, behind],
        scratch_shapes=[pltpu.VMEM((Q_ROWS, GROUP_W), F32)] * 2, name=f"dil_bwd_{g}",
        compiler_params=_params(("parallel", "arbitrary")))(q, k, k, v, v, do, lse, delta)


def _major_specs(s, tm, dtype, width=GROUP_W):
    shapes = [jax.ShapeDtypeStruct((dil, s // dil, width), dtype) for dil in DILATIONS]
    specs = [pl.BlockSpec((dil, tm // dil, width), lambda i: (0, i, 0)) for dil in DILATIONS]
    return shapes, specs


def _attn_merge(outs, lses):
    s = outs[0].shape[1]
    tm = _pick_rows(s, ROPE_TILE)

    def body(o0, o1, o2, l0, l1, l2, m_ref, e0, e1, e2, so1, so2, sl1, sl2, se):
        for dil, src, dst in ((DILATIONS[1], l1, sl1), (DILATIONS[2], l2, sl2)):
            for r in range(dil):
                dst[_residue(r, tm // dil, dil), :] = src[r]
        a, b, c = l0[0], sl1[...], sl2[...]
        mx = jnp.maximum(jnp.maximum(a, b), c)
        ea, eb, ec = jnp.exp(a - mx), jnp.exp(b - mx), jnp.exp(c - mx)
        den = ea + eb + ec
        wa, wb, wc = ea / den, eb / den, ec / den
        se[...] = mx + jnp.log(den)
        for dil, dst in zip(DILATIONS, (e0, e1, e2)):
            for r in range(dil):
                dst[r] = se[_residue(r, tm // dil, dil), :]
        for h in range(HEADS_PER_GROUP):
            sl = slice(h * HEAD_DIM, (h + 1) * HEAD_DIM)
            col = slice(h * STAT_LANES, h * STAT_LANES + 1)
            for dil, src, dst in ((DILATIONS[1], o1, so1), (DILATIONS[2], o2, so2)):
                for r in range(dil):
                    dst[h, _residue(r, tm // dil, dil), :] = src[r, :, sl]
            m_ref[:, sl] = (wa[:, col] * o0[0, :, sl] + wb[:, col] * so1[h] + wc[:, col] * so2[h]).astype(BF)

    shapes, specs = _major_specs(s, tm, F32)
    stat_shapes, stat_specs = _major_specs(s, tm, F32, HEAD_DIM)
    nat = pl.BlockSpec((tm, GROUP_W), lambda i: (i, 0))
    res = pl.pallas_call(
        body, out_shape=[jax.ShapeDtypeStruct((s, GROUP_W + MEM_W), BF)] + stat_shapes, grid=(s // tm,),
        in_specs=specs + stat_specs, out_specs=[nat] + stat_specs,
        scratch_shapes=[pltpu.VMEM((HEADS_PER_GROUP, tm, HEAD_DIM), F32)] * 2 + [pltpu.VMEM((tm, HEAD_DIM), F32)] * 3,
        name="attn_merge", compiler_params=_params(("parallel",)))(*outs, *lses)
    return res[0], res[1:]


def _attn_delta(dcat, merged, after=()):
    s = merged.shape[0]
    tm = _pick_rows(s, ROPE_TILE)

    def body(*refs):
        d_refs, m_ref = refs[:HEADS_PER_GROUP], refs[HEADS_PER_GROUP]
        do_refs, dl_refs, scr = refs[-7:-4], refs[-4:-1], refs[-1]
        sums = []
        for h in range(HEADS_PER_GROUP):
            sl = slice(h * HEAD_DIM, (h + 1) * HEAD_DIM)
            sums.append(jnp.sum(d_refs[h][...] * m_ref[:, sl].astype(F32), axis=-1, keepdims=True))
            for dil, do_ref in zip(DILATIONS, do_refs):
                for r in range(dil):
                    do_ref[r, :, sl] = d_refs[h][_residue(r, tm // dil, dil), :].astype(BF)
        scr[...] = _pack_stats(sums)
        for dil, dl_ref in zip(DILATIONS, dl_refs):
            for r in range(dil):
                dl_ref[r] = scr[_residue(r, tm // dil, dil), :]

    nat = pl.BlockSpec((tm, GROUP_W), lambda i: (i, 0))
    head_specs = [pl.BlockSpec((tm, HEAD_DIM), lambda i, h=h: (i, h)) for h in range(HEADS_PER_GROUP)]
    bf_shapes, specs = _major_specs(s, tm, BF)
    stat_shapes, stat_specs = _major_specs(s, tm, F32, HEAD_DIM)
    res = pl.pallas_call(
        body, out_shape=bf_shapes + stat_shapes, grid=(s // tm,), in_specs=head_specs + [nat] + [ANY] * len(after),
        out_specs=specs + stat_specs, scratch_shapes=[pltpu.VMEM((tm, HEAD_DIM), F32)], name="attn_delta",
        compiler_params=_params(("parallel",)))(*([dcat] * HEADS_PER_GROUP), merged, *after)
    return res[:3], res[3:]


def _mem_probs(qh, kh):
    sc = lax.dot_general(qh, kh, NT_DIMS, preferred_element_type=F32) * SCALE
    p = jnp.exp(sc - jnp.max(sc, axis=-1, keepdims=True))
    return p, jnp.sum(p, axis=-1, keepdims=True)


def _mem_fwd(name, proj, q_block, kv, into, out_block):
    s = proj.shape[0]
    tq = _pick_rows(s, 512)

    def body(q_ref, kv_ref, into_ref, o_ref):
        for h in range(MEM_HEADS):
            sl = slice(h * HEAD_DIM, (h + 1) * HEAD_DIM)
            vsl = slice(MEM_W + h * HEAD_DIM, MEM_W + (h + 1) * HEAD_DIM)
            p, den = _mem_probs(q_ref[:, sl].astype(BF), kv_ref[:, sl].astype(BF))
            out = jnp.dot(p.astype(BF), kv_ref[:, vsl].astype(BF), preferred_element_type=F32) / den
            o_ref[:, sl] = out.astype(o_ref.dtype)

    return pl.pallas_call(
        body, out_shape=jax.ShapeDtypeStruct(into.shape, into.dtype), grid=(s // tq,),
        in_specs=[pl.BlockSpec((tq, MEM_W), lambda i: (i, q_block)), pl.BlockSpec(kv.shape, lambda i: (0, 0)), ANY],
        out_specs=pl.BlockSpec((tq, MEM_W), lambda i: (i, out_block)), input_output_aliases={2: 0}, name=name,
        compiler_params=_params(("parallel",)))(proj, kv, into)


def _mem_bwd(name, proj, q_block, kv, dcat, d_block, width):
    s = proj.shape[0]
    tq = _pick_rows(s, 512)

    def body(q_ref, kv_ref, do_ref, dq_ref, dkv_ref):
        @pl.when(pl.program_id(0) == 0)
        def _():
            dkv_ref[...] = jnp.zeros_like(dkv_ref)

        for h in range(MEM_HEADS):
            sl = slice(h * HEAD_DIM, (h + 1) * HEAD_DIM)
            vsl = slice(MEM_W + h * HEAD_DIM, MEM_W + (h + 1) * HEAD_DIM)
            qh, kh, vh = q_ref[:, sl].astype(BF), kv_ref[:, sl].astype(BF), kv_ref[:, vsl].astype(BF)
            doh = do_ref[:, sl].astype(BF)
            p, den = _mem_probs(qh, kh)
            p = p / den
            dp = lax.dot_general(doh, vh, NT_DIMS, preferred_element_type=F32)
            ds = (p * (dp - jnp.sum(p * dp, axis=-1, keepdims=True)) * SCALE).astype(BF)
            dq_ref[:, sl] = jnp.dot(ds, kh, preferred_element_type=F32).astype(BF)
            dkv_ref[:, sl] += lax.dot_general(ds, qh, TN_DIMS, preferred_element_type=F32)
            dkv_ref[:, vsl] += lax.dot_general(p.astype(BF), doh, TN_DIMS, preferred_element_type=F32)

    whole = pl.BlockSpec(kv.shape, lambda i: (0, 0))
    return pl.pallas_call(
        body, out_shape=[jax.ShapeDtypeStruct((s, width), BF), jax.ShapeDtypeStruct(kv.shape, F32)], grid=(s // tq,),
        in_specs=[pl.BlockSpec((tq, MEM_W), lambda i: (i, q_block)), whole,
                  pl.BlockSpec((tq, MEM_W), lambda i: (i, d_block))],
        out_specs=[pl.BlockSpec((tq, MEM_W), lambda i: (i, width // MEM_W - 1)), whole], name=name,
        compiler_params=_params(("arbitrary",)))(proj, kv, dcat)


def _causal():
    t = lax.broadcasted_iota(jnp.int32, (BLK, BLK), 0)
    s = lax.broadcasted_iota(jnp.int32, (BLK, BLK), 1)
    return t >= s


def _sgu_norm(vg, ln_g, ln_b):
    mu = jnp.mean(vg, axis=-1, keepdims=True)
    cen = vg - mu
    rstd = lax.rsqrt(jnp.mean(cen * cen, axis=-1, keepdims=True) + LN_EPS)
    xhat = cen * rstd
    return xhat, rstd, xhat * ln_g + ln_b


def _sgu_fwd(proj, ln_g, ln_b, w_sp, b_t):
    s = proj.shape[0]

    def body(u_ref, v_ref, g_ref, b_ref, w_ref, bt_ref, o_ref):
        _, _, vn = _sgu_norm(_gelu(v_ref[...].astype(F32)), g_ref[...], b_ref[...])
        vn = vn.astype(BF)
        tri = _causal()
        for grp in range(SGU_GROUPS):
            sl = slice(grp * HEAD_DIM, (grp + 1) * HEAD_DIM)
            w = jnp.where(tri, w_ref[grp], 0.0).astype(BF)
            mixed = jnp.dot(w, vn[:, sl], preferred_element_type=F32) + bt_ref[:, grp:grp + 1]
            o_ref[:, sl] = (_gelu(u_ref[:, sl].astype(F32)) * mixed).astype(BF)

    vec = pl.BlockSpec((1, SGU_W), lambda i: (0, 0))
    return pl.pallas_call(
        body, out_shape=jax.ShapeDtypeStruct((s, SGU_W + MEM_W), BF), grid=(s // BLK,),
        in_specs=[pl.BlockSpec((BLK, SGU_W), lambda i: (i, 0)), pl.BlockSpec((BLK, SGU_W), lambda i: (i, 1)), vec, vec,
                  pl.BlockSpec(w_sp.shape, lambda i: (0, 0, 0)), pl.BlockSpec(b_t.shape, lambda i: (0, 0))],
        out_specs=pl.BlockSpec((BLK, SGU_W), lambda i: (i, 0)), name="sgu_fwd",
        compiler_params=_params(("parallel",)))(proj, proj, ln_g, ln_b, w_sp, b_t)


def _sgu_bwd(proj, dcat, ln_g, ln_b, w_sp, b_t, into):
    s = proj.shape[0]

    def body(u_ref, v_ref, d_ref, g_ref, b_ref, w_ref, bt_ref, into_ref, dp_ref, dw_ref, db_ref, dg_ref, dbeta_ref,
             dvn_ref):
        @pl.when(pl.program_id(0) == 0)
        def _():
            dw_ref[...] = jnp.zeros_like(dw_ref)
            db_ref[...] = jnp.zeros_like(db_ref)
            dg_ref[...] = jnp.zeros_like(dg_ref)
            dbeta_ref[...] = jnp.zeros_like(dbeta_ref)

        gain = g_ref[...]
        vg, v_slope = _gelu_both(v_ref[...].astype(F32))
        xhat, rstd, vn = _sgu_norm(vg, gain, b_ref[...])
        vn = vn.astype(BF)
        tri = _causal()
        lane = lax.broadcasted_iota(jnp.int32, (BLK, HEAD_DIM), 1)
        db_acc = jnp.zeros((BLK, HEAD_DIM), F32)
        for grp in range(SGU_GROUPS):
            sl = slice(grp * HEAD_DIM, (grp + 1) * HEAD_DIM)
            w = jnp.where(tri, w_ref[grp], 0.0).astype(BF)
            vn_g = vn[:, sl]
            mixed = jnp.dot(w, vn_g, preferred_element_type=F32) + bt_ref[:, grp:grp + 1]
            u_act, u_slope = _gelu_both(u_ref[:, sl].astype(F32))
            d_out = d_ref[:, sl].astype(F32)
            dp_ref[:, sl] = (d_out * mixed * u_slope).astype(BF)
            dmixed = d_out * u_act
            dm = dmixed.astype(BF)
            dvn_ref[:, sl] = lax.dot_general(w, dm, TN_DIMS, preferred_element_type=F32)
            dw = lax.dot_general(dm, vn_g, NT_DIMS, preferred_element_type=F32)
            dw_ref[grp] += jnp.where(tri, dw, 0.0)
            db_acc += jnp.where(lane == grp, jnp.sum(dmixed, axis=-1, keepdims=True), 0.0)
        db_ref[...] += db_acc
        dvn = dvn_ref[...]
        dg_ref[...] += jnp.sum(dvn * xhat, axis=0, keepdims=True)
        dbeta_ref[...] += jnp.sum(dvn, axis=0, keepdims=True)
        dxh = dvn * gain
        dvg = rstd * (dxh - jnp.mean(dxh, axis=-1, keepdims=True) - xhat * jnp.mean(dxh * xhat, axis=-1, keepdims=True))
        dp_ref[:, SGU_W:] = (dvg * v_slope).astype(BF)

    vec = pl.BlockSpec((1, SGU_W), lambda i: (0, 0))
    row = pl.BlockSpec((BLK, SGU_W), lambda i: (i, 0))
    w_spec = pl.BlockSpec(w_sp.shape, lambda i: (0, 0, 0))
    sq = pl.BlockSpec((BLK, HEAD_DIM), lambda i: (0, 0))
    return pl.pallas_call(
        body,
        out_shape=[jax.ShapeDtypeStruct(into.shape, into.dtype),
                   jax.ShapeDtypeStruct(w_sp.shape, F32), jax.ShapeDtypeStruct((BLK, HEAD_DIM), F32),
                   jax.ShapeDtypeStruct((1, SGU_W), F32), jax.ShapeDtypeStruct((1, SGU_W), F32)],
        grid=(s // BLK,),
        in_specs=[row, pl.BlockSpec((BLK, SGU_W), lambda i: (i, 1)), row, vec, vec, w_spec,
                  pl.BlockSpec(b_t.shape, lambda i: (0, 0)), ANY],
        out_specs=[pl.BlockSpec((BLK, 2 * SGU_W), lambda i: (i, 0)), w_spec, sq, vec, vec],
        scratch_shapes=[pltpu.VMEM((BLK, SGU_W), F32)], input_output_aliases={7: 0}, name="sgu_bwd",
        compiler_params=_params(("arbitrary",)))(proj, proj, dcat, ln_g, ln_b, w_sp, b_t, into)


def _place():
    return lax.axis_index("x"), lax.axis_index("y"), lax.axis_index("c")


def _other_chips(x, y):
    return [(1 - x, y), (x, 1 - y), (1 - x, 1 - y)]


def _peer(x, y, c, mask):
    return (1 - x if mask & 4 else x, 1 - y if mask & 2 else y, 1 - c if mask & 1 else c)


def _in_hbm(a):
    return pltpu.with_memory_space_constraint(a, pltpu.HBM)


def _token_spec():
    return jax.ShapeDtypeStruct((8, LANES), F32), pl.BlockSpec(memory_space=pltpu.VMEM)


def _remote(src, dst, ssem, rsem, to):
    return pltpu.make_async_remote_copy(src_ref=src, dst_ref=dst, send_sem=ssem, recv_sem=rsem, device_id=to,
                                        device_id_type=MESH)


def _place_shard(name, src, layer, place, dtype, after=()):
    _, rows, cols = src.shape
    tr = _pick_rows(rows, 512)

    def body(p_ref, s_ref, *rest):
        rest[-1][...] = s_ref[...].astype(dtype)

    grid_spec = pltpu.PrefetchScalarGridSpec(
        num_scalar_prefetch=1, grid=(rows // tr,),
        in_specs=[pl.BlockSpec((None, tr, cols), lambda i, p: (layer, i, 0))] + [ANY] * len(after),
        out_specs=pl.BlockSpec((None, tr, cols), lambda i, p: (p[1], i, 0)))
    return pl.pallas_call(body, out_shape=jax.ShapeDtypeStruct((N_CHIPS, rows, cols), dtype), grid_spec=grid_spec,
                          name=name, compiler_params=_params(("parallel",)))(place, src, *after)


def _gather_copies(bufs, ssem, rsem):
    x, y, c = _place()
    me = 2 * x + y
    copies = []
    for ai, buf in enumerate(bufs):
        for k, (ox, oy) in enumerate(_other_chips(x, y)):
            copies.append(_remote(buf.at[me], buf.at[me], ssem.at[3 * ai + k], rsem.at[3 * ai + k], (ox, oy, c)))
    return copies


def _reduce_copies(grads, lands, ssem, rsem):
    x, y, c = _place()
    copies = []
    for a, (gr, land) in enumerate(zip(grads, lands)):
        for mask in range(1, N_DEV):
            px, py, pc = _peer(x, y, c, mask)
            copies.append(_remote(gr.at[pc, 2 * px + py], land.at[mask - 1], ssem.at[7 * a + mask - 1],
                                  rsem.at[7 * a + mask - 1], (px, py, pc)))
    return copies


def _half_copies(totals, ssem, rsem):
    x, y, c = _place()
    return [_remote(t.at[c], t.at[c], ssem.at[a], rsem.at[a], (x, y, 1 - c)) for a, t in enumerate(totals)]


def _gather_start(name, groups):
    flat = [s for grp in groups for s in grp]
    n, ng = len(flat), len(groups)

    def body(*refs):
        ins = refs[:n]
        sems = refs[n:n + 2 * ng]
        token = refs[-1]
        idx = 0
        for gi, grp in enumerate(groups):
            for cp in _gather_copies(ins[idx:idx + len(grp)], sems[2 * gi], sems[2 * gi + 1]):
                cp.start()
            idx += len(grp)
        token[...] = jnp.zeros_like(token)

    tok_shape, tok_spec = _token_spec()
    sem_shapes = []
    for grp in groups:
        sem_shapes += [pltpu.SemaphoreType.DMA((3 * len(grp),))] * 2
    res = pl.pallas_call(
        body, name=name,
        out_shape=(*sem_shapes, *[pltpu.HBM(s.shape, s.dtype) for s in flat], tok_shape),
        in_specs=[HBM] * n, out_specs=(*[SEM] * (2 * ng), *[HBM] * n, tok_spec),
        input_output_aliases={i: 2 * ng + i for i in range(n)},
        compiler_params=pltpu.CompilerParams(has_side_effects=EFFECT))(*[_in_hbm(s) for s in flat])
    out, idx = [], 2 * ng
    for gi, grp in enumerate(groups):
        out.append((res[2 * gi], res[2 * gi + 1], list(res[idx:idx + len(grp)])))
        idx += len(grp)
    return out, res[-1]


def _gather_wait(name, ssem, rsem, slabs, after):
    n = len(slabs)

    def body(*refs):
        for cp in _gather_copies(refs[:n], refs[n], refs[n + 1]):
            cp.wait_send()
            cp.wait_recv()

    return pl.pallas_call(
        body, name=name, out_shape=tuple(pltpu.HBM(s.shape, s.dtype) for s in slabs),
        in_specs=[HBM] * n + [SEM, SEM] + [ANY] * len(after), out_specs=tuple([HBM] * n),
        input_output_aliases={i: i for i in range(n)},
        compiler_params=pltpu.CompilerParams(has_side_effects=EFFECT))(*slabs, ssem, rsem, *after)


def _reduce_start(name, grads):
    n = len(grads)
    lands = [lax.empty((N_DEV - 1, *g.shape[2:]), g.dtype) for g in grads]

    def body(*refs):
        token = refs[-1]
        for cp in _reduce_copies(refs[:n], refs[n:2 * n], refs[2 * n], refs[2 * n + 1]):
            cp.start()
        token[...] = jnp.zeros_like(token)

    tok_shape, tok_spec = _token_spec()
    sems = [pltpu.SemaphoreType.DMA((7 * n,))] * 2
    res = pl.pallas_call(
        body, name=name,
        out_shape=(*sems, *[pltpu.HBM(g.shape, g.dtype) for g in grads], *[pltpu.HBM(l.shape, l.dtype) for l in lands],
                   tok_shape),
        in_specs=[HBM] * (2 * n), out_specs=(SEM, SEM, *[HBM] * (2 * n), tok_spec),
        input_output_aliases={i: 2 + i for i in range(2 * n)},
        compiler_params=pltpu.CompilerParams(has_side_effects=EFFECT))(*[_in_hbm(t) for t in (*grads, *lands)])
    return res[0], res[1], list(res[2:2 + n]), list(res[2 + n:2 + 2 * n]), res[-1]


def _reduce_wait(name, ssem, rsem, grads, lands, after):
    n = len(grads)

    def body(*refs):
        for cp in _reduce_copies(refs[:n], refs[n:2 * n], refs[2 * n], refs[2 * n + 1]):
            cp.wait_send()
            cp.wait_recv()

    res = pl.pallas_call(
        body, name=name, out_shape=tuple(pltpu.HBM(t.shape, t.dtype) for t in (*grads, *lands)),
        in_specs=[HBM] * (2 * n) + [SEM, SEM] + [ANY] * len(after), out_specs=tuple([HBM] * (2 * n)),
        input_output_aliases={i: i for i in range(2 * n)},
        compiler_params=pltpu.CompilerParams(has_side_effects=EFFECT))(*grads, *lands, ssem, rsem, *after)
    return list(res[:n]), list(res[n:])


def _sum_pieces(name, grad, land, place):
    _, _, rows, cols = grad.shape
    tr = _pick_rows(rows, 256)

    def body(p_ref, g_ref, l_ref, o_ref):
        tot = g_ref[...].astype(F32)
        for k in range(N_DEV - 1):
            tot = tot + l_ref[k].astype(F32)
        o_ref[...] = tot

    grid_spec = pltpu.PrefetchScalarGridSpec(
        num_scalar_prefetch=1, grid=(rows // tr,),
        in_specs=[pl.BlockSpec((None, None, tr, cols), lambda i, p: (p[0], p[1], i, 0)),
                  pl.BlockSpec((N_DEV - 1, tr, cols), lambda i, p: (0, i, 0))],
        out_specs=pl.BlockSpec((None, tr, cols), lambda i, p: (p[0], i, 0)))
    return pl.pallas_call(body, out_shape=jax.ShapeDtypeStruct((2, rows, cols), F32), grid_spec=grid_spec, name=name,
                          compiler_params=_params(("parallel",)))(place, grad, land)


def _first_copies(bufs, ssem, rsem):
    x, y, c = _place()
    me = 2 * x + y
    copies = []
    for a, buf in enumerate(bufs):
        for k, (ox, oy) in enumerate(_other_chips(x, y)):
            copies.append(_remote(buf.at[me, c], buf.at[me, c], ssem.at[3 * a + k], rsem.at[3 * a + k], (ox, oy, c)))
    return copies


def _pass_copies(bufs, ssem, rsem):
    x, y, c = _place()
    copies = []
    for a, buf in enumerate(bufs):
        for k, (ox, oy) in enumerate(_other_chips(x, y)):
            landed = buf.at[2 * ox + oy, c]
            copies.append(_remote(landed, landed, ssem.at[3 * a + k], rsem.at[3 * a + k], (x, y, 1 - c)))
    return copies


def _exchange_start(name, arrays, copies_fn, n_sems):
    n = len(arrays)

    def body(*refs):
        token = refs[-1]
        for cp in copies_fn(refs[:n], refs[n], refs[n + 1]):
            cp.start()
        token[...] = jnp.zeros_like(token)

    tok_shape, tok_spec = _token_spec()
    res = pl.pallas_call(
        body, name=name,
        out_shape=(pltpu.SemaphoreType.DMA((n_sems,)), pltpu.SemaphoreType.DMA((n_sems,)),
                   *[pltpu.HBM(t.shape, t.dtype) for t in arrays], tok_shape),
        in_specs=[HBM] * n, out_specs=(SEM, SEM, *[HBM] * n, tok_spec),
        input_output_aliases={i: 2 + i for i in range(n)},
        compiler_params=pltpu.CompilerParams(has_side_effects=EFFECT))(*[_in_hbm(t) for t in arrays])
    return res[0], res[1], list(res[2:2 + n]), res[-1]


def _exchange_wait(name, ssem, rsem, arrays, copies_fn, after):
    n = len(arrays)

    def body(*refs):
        for cp in copies_fn(refs[:n], refs[n], refs[n + 1]):
            cp.wait_send()
            cp.wait_recv()

    res = pl.pallas_call(
        body, name=name, out_shape=tuple(pltpu.HBM(t.shape, t.dtype) for t in arrays),
        in_specs=[HBM] * n + [SEM, SEM] + [ANY] * len(after), out_specs=tuple([HBM] * n),
        input_output_aliases={i: i for i in range(n)},
        compiler_params=pltpu.CompilerParams(has_side_effects=EFFECT))(*arrays, ssem, rsem, *after)
    return list(res)


def _half_start(name, totals):
    n = len(totals)

    def body(*refs):
        token = refs[-1]
        for cp in _half_copies(refs[:n], refs[n], refs[n + 1]):
            cp.start()
        token[...] = jnp.zeros_like(token)

    tok_shape, tok_spec = _token_spec()
    res = pl.pallas_call(
        body, name=name,
        out_shape=(pltpu.SemaphoreType.DMA((n,)), pltpu.SemaphoreType.DMA((n,)),
                   *[pltpu.HBM(t.shape, t.dtype) for t in totals], tok_shape),
        in_specs=[HBM] * n, out_specs=(SEM, SEM, *[HBM] * n, tok_spec),
        input_output_aliases={i: 2 + i for i in range(n)},
        compiler_params=pltpu.CompilerParams(has_side_effects=EFFECT))(*[_in_hbm(t) for t in totals])
    return res[0], res[1], list(res[2:2 + n]), res[-1]


def _half_wait(name, ssem, rsem, totals, after):
    n = len(totals)

    def body(*refs):
        for cp in _half_copies(refs[:n], refs[n], refs[n + 1]):
            cp.wait_send()
            cp.wait_recv()

    res = pl.pallas_call(
        body, name=name, out_shape=tuple(pltpu.HBM(t.shape, t.dtype) for t in totals),
        in_specs=[HBM] * n + [SEM, SEM] + [ANY] * len(after), out_specs=tuple([HBM] * n),
        input_output_aliases={i: i for i in range(n)},
        compiler_params=pltpu.CompilerParams(has_side_effects=EFFECT))(*totals, ssem, rsem, *after)
    return list(res)


def _small_copies(bufs, ssem, rsem):
    x, y, c = _place()
    mine = bufs[0].at[4 * x + 2 * y + c]
    return [_remote(mine, mine, ssem.at[mask - 1], rsem.at[mask - 1], _peer(x, y, c, mask)) for mask in range(1, N_DEV)]


def _small_start(name, slots):
    def body(s_ref, ssem, rsem, thru, token):
        for cp in _small_copies([s_ref], ssem, rsem):
            cp.start()
        token[...] = jnp.zeros_like(token)

    tok_shape, tok_spec = _token_spec()
    sems = [pltpu.SemaphoreType.DMA((N_DEV - 1,))] * 2
    return pl.pallas_call(
        body, name=name, out_shape=(*sems, pltpu.HBM(slots.shape, slots.dtype), tok_shape), in_specs=[HBM],
        out_specs=(SEM, SEM, HBM, tok_spec), input_output_aliases={0: 2},
        compiler_params=pltpu.CompilerParams(has_side_effects=EFFECT))(_in_hbm(slots))


def _small_wait(name, ssem, rsem, slots, after):
    def body(*refs):
        for cp in _small_copies([refs[0]], refs[1], refs[2]):
            cp.wait_send()
            cp.wait_recv()

    return pl.pallas_call(
        body, name=name, out_shape=pltpu.HBM(slots.shape, slots.dtype), in_specs=[HBM, SEM, SEM] + [ANY] * len(after),
        out_specs=HBM, input_output_aliases={0: 0},
        compiler_params=pltpu.CompilerParams(has_side_effects=EFFECT))(slots, ssem, rsem, *after)


def _own_slot(small, me):
    return lax.dynamic_update_slice(jnp.zeros((N_DEV, *small.shape), small.dtype), small[None], (me, 0, 0))


def _sum_devices(name, stacked):
    _, rows, lanes = stacked.shape
    tr = _pick_rows(rows, 512)

    def body(s_ref, o_ref):
        tot = s_ref[0]
        for k in range(1, N_DEV):
            tot = tot + s_ref[k]
        o_ref[...] = tot

    return pl.pallas_call(
        body, out_shape=jax.ShapeDtypeStruct((rows, lanes), F32), grid=(rows // tr,),
        in_specs=[pl.BlockSpec((N_DEV, tr, lanes), lambda i: (0, i, 0))], out_specs=pl.BlockSpec((tr, lanes), lambda i: (i, 0)),
        name=name, compiler_params=_params(("parallel",)))(stacked)


def _adamw(name, w, g, m, v, layer, prev=None):
    layers, rows, cols = w.shape
    tr = _pick_rows(rows, 256)
    c1 = 1.0 - ADAM_B1 ** ADAM_STEP
    c2 = 1.0 - ADAM_B2 ** ADAM_STEP

    def body(w_ref, g_ref, m_ref, v_ref, *rest):
        go_ref, d_ref, nm_ref, nv_ref = rest[-4:]
        gv = g_ref[...]
        nm = ADAM_B1 * m_ref[...] + (1.0 - ADAM_B1) * gv
        nv = ADAM_B2 * v_ref[...] + (1.0 - ADAM_B2) * (gv * gv)
        go_ref[...] = gv
        d_ref[...] = -ADAM_LR * ((nm / c1) / (jnp.sqrt(nv / c2) + ADAM_EPS) + ADAM_WD * w_ref[...])
        nm_ref[...] = nm
        nv_ref[...] = nv

    spec = pl.BlockSpec((None, tr, cols), lambda i: (layer, i, 0))
    prev = list(prev) if prev is not None else []
    return pl.pallas_call(
        body, out_shape=[jax.ShapeDtypeStruct((layers, rows, cols), F32)] * 4, grid=(rows // tr,),
        in_specs=[spec, pl.BlockSpec((tr, cols), lambda i: (i, 0)), spec, spec] + [ANY] * len(prev),
        out_specs=[spec] * 4, input_output_aliases={4 + i: i for i in range(len(prev))}, name=name,
        compiler_params=_params(("parallel",)))(w, g, m, v, *prev)


def _pack(vectors, pad_rows):
    flat = jnp.concatenate([t.reshape(-1) for t in vectors])
    rows = -(-flat.shape[0] // LANES)
    rows = -(-rows // pad_rows) * pad_rows
    return jnp.pad(flat, (0, rows * LANES - flat.shape[0])).reshape(rows, LANES)


def _unpack(packed, shapes):
    flat = packed.reshape(-1)
    out, off = [], 0
    for shp in shapes:
        size = math.prod(shp)
        out.append(flat[off:off + size].reshape(shp))
        off += size
    return out


def kernel(x, mem, positions, mix_norm, mem_norm, w_mem_kv, ffn_norm, w_gate, w_up, w_down, attn_w_in, attn_w_out, sgu_w_in, sgu_ln_g, sgu_ln_b, sgu_w_spatial, sgu_b_spatial, sgu_w_out, final_norm, loss_target, m_mix_norm, m_mem_norm, m_w_mem_kv, m_ffn_norm, m_w_gate, m_w_up, m_w_down, m_attn_w_in, m_attn_w_out, m_sgu_w_in, m_sgu_ln_g, m_sgu_ln_b, m_sgu_w_spatial, m_sgu_b_spatial, m_sgu_w_out, m_final_norm, v_mix_norm, v_mem_norm, v_w_mem_kv, v_ffn_norm, v_w_gate, v_w_up, v_w_down, v_attn_w_in, v_attn_w_out, v_sgu_w_in, v_sgu_ln_g, v_sgu_ln_b, v_sgu_w_spatial, v_sgu_b_spatial, v_sgu_w_out, v_final_norm):
    d_model = x.shape[2]
    x0, mem0, tgt = x[0], mem[0], loss_target[0]
    xi, yi, ci = _place()
    chip = 2 * xi + yi
    place = jnp.stack([ci, chip]).astype(jnp.int32)

    given_w = dict(mix_norm=mix_norm, mem_norm=mem_norm, w_mem_kv=w_mem_kv, ffn_norm=ffn_norm, w_gate=w_gate, w_up=w_up,
                   w_down=w_down, attn_w_in=attn_w_in, attn_w_out=attn_w_out, sgu_w_in=sgu_w_in, sgu_ln_g=sgu_ln_g,
                   sgu_ln_b=sgu_ln_b, sgu_w_spatial=sgu_w_spatial, sgu_b_spatial=sgu_b_spatial, sgu_w_out=sgu_w_out,
                   final_norm=final_norm)
    given_m = dict(mix_norm=m_mix_norm, mem_norm=m_mem_norm, w_mem_kv=m_w_mem_kv, ffn_norm=m_ffn_norm, w_gate=m_w_gate,
                   w_up=m_w_up, w_down=m_w_down, attn_w_in=m_attn_w_in, attn_w_out=m_attn_w_out, sgu_w_in=m_sgu_w_in,
                   sgu_ln_g=m_sgu_ln_g, sgu_ln_b=m_sgu_ln_b, sgu_w_spatial=m_sgu_w_spatial,
                   sgu_b_spatial=m_sgu_b_spatial, sgu_w_out=m_sgu_w_out, final_norm=m_final_norm)
    given_v = dict(mix_norm=v_mix_norm, mem_norm=v_mem_norm, w_mem_kv=v_w_mem_kv, ffn_norm=v_ffn_norm, w_gate=v_w_gate,
                   w_up=v_w_up, w_down=v_w_down, attn_w_in=v_attn_w_in, attn_w_out=v_attn_w_out, sgu_w_in=v_sgu_w_in,
                   sgu_ln_g=v_sgu_ln_g, sgu_ln_b=v_sgu_ln_b, sgu_w_spatial=v_sgu_w_spatial,
                   sgu_b_spatial=v_sgu_b_spatial, sgu_w_out=v_sgu_w_out, final_norm=v_final_norm)

    units = {"attn_w_in": ("attn_w_in", 0, "col"), "w_mem_kv0": ("w_mem_kv", 0, "row"), "attn_w_out": ("attn_w_out", 0, "col"),
             "w_gate0": ("w_gate", 0, "col"), "w_up0": ("w_up", 0, "col"), "w_down0": ("w_down", 0, "row"),
             "sgu_w_in": ("sgu_w_in", 0, "col"), "w_mem_kv1": ("w_mem_kv", 1, "row"), "sgu_w_out": ("sgu_w_out", 0, "row"),
             "w_gate1": ("w_gate", 1, "col"), "w_up1": ("w_up", 1, "col"), "w_down1": ("w_down", 1, "row")}
    gather_groups = [["attn_w_in"], ["w_mem_kv0", "attn_w_out"], ["w_gate0", "w_up0"],
                     ["w_down0", "sgu_w_in", "w_mem_kv1", "ln"], ["sgu_w_out", "w_gate1", "w_up1"], ["w_down1"]]

    first = _place_shard("place_attn_w_in", attn_w_in, 0, place, BF)
    first = first.reshape(N_CHIPS, 2, first.shape[1] // 2, first.shape[2])
    first_ssem, first_rsem, first_thru, token = _exchange_start("gather_start_0", [first], _first_copies, 3)
    in_flight = [None]
    slabs = {u: _place_shard(f"place_{u}", given_w[arr], layer, place, BF, after=[token])
             for u, (arr, layer, _) in units.items() if u != "attn_w_in"}
    slabs["ln"] = _place_shard("place_ln", jnp.concatenate([sgu_ln_g, sgu_ln_b])[None], 0, place, F32, after=[token])
    rest, token = _gather_start("gather_start_1", [[slabs[u] for u in grp] for grp in gather_groups[1:]])
    in_flight += rest
    weights = {}

    def arrive(gi, after):
        ssem, rsem, arrs = in_flight[gi]
        for u, full in zip(gather_groups[gi], _gather_wait(f"gather_wait_{gi}", ssem, rsem, arrs, after)):
            weights[u] = full if u == "ln" else Weight(full, units[u][2])

    w_sp = sgu_w_spatial[0]
    b_t = sgu_b_spatial[0].T
    tables = _rope_tables(positions[0])

    def residual(acc, extra):
        return [extra[0] + acc[0]]

    def memory_kv(layer):
        mem_n = _rms_fwd(f"mem_norm_{layer}", mem0, mem_norm[layer:layer + 1])
        return mem_n, _mm_nn(f"mem_kv_{layer}", mem_n, weights[f"w_mem_kv{layer}"])[0]

    h0 = _rms_fwd("mix_norm_0", x0, mix_norm[0:1], after=[token])
    landed = _exchange_wait("gather_wait_0", first_ssem, first_rsem, first_thru, _first_copies, [h0])
    pass_ssem, pass_rsem, passing, _ = _exchange_start("pass_start_0", landed, _pass_copies, 3)
    whole = _exchange_wait("pass_wait_0", pass_ssem, pass_rsem, passing, _pass_copies, [])[0]
    weights["attn_w_in"] = Weight(whole.reshape(N_CHIPS, -1, whole.shape[-1]), "col")
    proj0 = _mm_nn("attn_in", h0, weights["attn_w_in"])[0]
    arrive(1, [proj0])
    qkv = _rope_fwd(proj0, tables)
    qs, ks, vs = qkv[0:3], qkv[3:6], qkv[6:9]
    outs, lses = [], []
    for g in range(len(DILATIONS)):
        o, l = _dil_fwd(g, qs[g], ks[g], vs[g])
        outs.append(o)
        lses.append(l)
    merged, lse = _attn_merge(outs, lses)
    mem_n0, kv0 = memory_kv(0)
    cat0 = _mem_fwd("mem_fwd_0", proj0, 9, kv0, merged, 1)
    x1, hf0 = _mm_nn("attn_out", cat0, weights["attn_w_out"], extras=[x0], epilogue=residual, norm_gain=ffn_norm[0:1])
    arrive(2, [x1])
    g0, u0, act0 = _gate_up("gate_up_0", hf0, weights["w_gate0"], weights["w_up0"])
    arrive(3, [act0])
    x2 = _mm_nn("down_0", act0, weights["w_down0"], extras=[x1], epilogue=residual)[0]

    ln_all = weights["ln"]
    ln_g = ln_all[:, 0, :].reshape(1, SGU_W)
    ln_b = ln_all[:, 1, :].reshape(1, SGU_W)
    h1 = _rms_fwd("mix_norm_1", x2, mix_norm[1:2])
    proj1 = _mm_nn("sgu_in", h1, weights["sgu_w_in"], out_dtypes=(BF,))[0]
    arrive(4, [proj1])
    sgu_out = _sgu_fwd(proj1, ln_g, ln_b, w_sp, b_t)
    mem_n1, kv1 = memory_kv(1)
    cat1 = _mem_fwd("mem_fwd_1", proj1, 6, kv1, sgu_out, 3)
    x3, hf1 = _mm_nn("sgu_out", cat1, weights["sgu_w_out"], extras=[x2], epilogue=residual, norm_gain=ffn_norm[1:2])
    g1, u1, act1 = _gate_up("gate_up_1", hf1, weights["w_gate1"], weights["w_up1"])
    arrive(5, [act1])
    x4 = _mm_nn("down_1", act1, weights["w_down1"], extras=[x3], epilogue=residual)[0]

    d4, d4_op, g_final, loss_part = _final_loss(x4, tgt, final_norm.reshape(1, d_model))
    loss = lax.psum(loss_part[0, 0], ("x", "y", "c"))

    outputs = {}

    def start_reduce(tag, names, grads):
        ssem, rsem, grads, lands, tok = _reduce_start(f"reduce_start_{tag}", grads)
        return dict(tag=tag, names=names, ssem=ssem, rsem=rsem, grads=grads, lands=lands), tok

    def finish_reduce(st, after):
        grads, lands = _reduce_wait(f"reduce_wait_{st['tag']}", st["ssem"], st["rsem"], st["grads"], st["lands"], after)
        totals = [_sum_pieces(f"sum_{u}", g, l, place) for u, g, l in zip(st["names"], grads, lands)]
        ssem, rsem, totals, tok = _half_start(f"half_start_{st['tag']}", totals)
        return dict(tag=st["tag"], names=st["names"], ssem=ssem, rsem=rsem, totals=totals), tok

    def finish_update(st, after):
        totals = _half_wait(f"half_wait_{st['tag']}", st["ssem"], st["rsem"], st["totals"], after)
        for u, tot in zip(st["names"], totals):
            arr, layer, _ = units[u]
            w = given_w[arr]
            outputs[arr] = _adamw(f"adamw_{u}", w, tot.reshape(w.shape[1:]), given_m[arr], given_v[arr], layer,
                                  outputs.get(arr))

    def ffn_bwd(layer, d_out, d_out_op, xin, h, g, u, act):
        wd, wg, wu = weights[f"w_down{layer}"], weights[f"w_gate{layer}"], weights[f"w_up{layer}"]
        gr_down = _mm_tn(f"d_down_{layer}", act, d_out_op, wd)
        dg, du = _mm_nt(f"d_act_{layer}", [d_out_op], [wd], out_dtypes=(BF, BF), extras=[g, u],
                        epilogue=_swiglu_bwd_epilogue, col_chunk=EPILOGUE_CHUNK)
        gr_gate = _mm_tn(f"d_gate_{layer}", h, dg, wg)
        gr_up = _mm_tn(f"d_up_{layer}", h, du, wu)
        st, tok = start_reduce(f"ffn{layer}", [f"w_down{layer}", f"w_gate{layer}", f"w_up{layer}"], [gr_down, gr_gate, gr_up])
        dh = _mm_nt(f"d_ffn_h_{layer}", [dg, du], [wg, wu], out_dtypes=(BF,), after=[tok])[0]
        d_in, d_in_op, g_norm = _rms_bwd(f"ffn_norm_bwd_{layer}", xin, ffn_norm[layer:layer + 1], dh, d_out)
        return st, d_in, d_in_op, g_norm

    def memory_bwd(layer, mem_n, dkv):
        dkv = dkv.astype(BF)
        wkv = weights[f"w_mem_kv{layer}"]
        gr = _mm_tn(f"d_mem_kv_{layer}", mem_n, dkv, wkv)
        d_mem_n = _mm_nt(f"d_mem_n_{layer}", [dkv], [wkv])[0]
        return gr, _rms_bwd(f"mem_norm_bwd_{layer}", mem0, mem_norm[layer:layer + 1], d_mem_n)[2]

    st_ffn1, d3, d3_op, g_ffn1 = ffn_bwd(1, d4, d4_op, x3, hf1, g1, u1, act1)
    gr_sgu_out = _mm_tn("d_sgu_out", cat1, d3_op, weights["sgu_w_out"])
    dcat1 = _mm_nt("d_cat_1", [d3_op], [weights["sgu_w_out"]], out_dtypes=(BF,))[0]
    st_ffn1, tok = finish_reduce(st_ffn1, [dcat1])
    dproj1, dkv1 = _mem_bwd("mem_bwd_1", proj1, 6, kv1, dcat1, 3, proj1.shape[1])
    gr_kv1, g_mem1 = memory_bwd(1, mem_n1, dkv1)
    dproj1, g_wsp, g_bsp_t, g_ln_g, g_ln_b = _sgu_bwd(proj1, dcat1, ln_g, ln_b, w_sp, b_t, dproj1)
    gr_sgu_in = _mm_tn("d_sgu_in", h1, dproj1, weights["sgu_w_in"], after=[tok])
    finish_update(st_ffn1, [gr_sgu_in])
    st_mix1, tok = start_reduce("mix1", ["sgu_w_out", "w_mem_kv1", "sgu_w_in"], [gr_sgu_out, gr_kv1, gr_sgu_in])
    dh1 = _mm_nt("d_h_1", [dproj1], [weights["sgu_w_in"]], out_dtypes=(BF,), after=[tok])[0]
    d2, d2_op, g_mix1 = _rms_bwd("mix_norm_bwd_1", x2, mix_norm[1:2], dh1, d3)

    st_ffn0, d1, d1_op, g_ffn0 = ffn_bwd(0, d2, d2_op, x1, hf0, g0, u0, act0)
    dev = 4 * xi + 2 * yi + ci
    small_a = [g_mix1, g_mem1, jnp.concatenate([g_ffn0, g_ffn1]), g_wsp, g_bsp_t[:, :SGU_GROUPS].T, g_final, g_ln_g, g_ln_b]
    sa_ssem, sa_rsem, sa_slots, tok = _small_start("small_start_a", _own_slot(_pack(small_a, LANES), dev))
    gr_attn_out = _mm_tn("d_attn_out", cat0, d1_op, weights["attn_w_out"], after=[tok])
    st_mix1, tok = finish_reduce(st_mix1, [gr_attn_out])
    dcat0 = _mm_nt("d_cat_0", [d1_op], [weights["attn_w_out"]], after=[tok])[0]
    dproj0, dkv0 = _mem_bwd("mem_bwd_0", proj0, 9, kv0, dcat0, 1, proj0.shape[1])
    finish_update(st_mix1, [dkv0])
    gr_kv0, g_mem0 = memory_bwd(0, mem_n0, dkv0)
    st_ffn0, tok = finish_reduce(st_ffn0, [g_mem0])
    d_merged, delta = _attn_delta(dcat0, cat0, after=[tok])
    dqs, dks, dvs = [], [], []
    for g in range(len(DILATIONS)):
        dq, dk, dv = _dil_bwd(g, qs[g], ks[g], vs[g], d_merged[g], lse[g], delta[g])
        dqs.append(dq)
        dks.append(dk)
        dvs.append(dv)
    dproj0 = _rope_bwd(dqs + dks + dvs, tables, dproj0)
    finish_update(st_ffn0, [dproj0])
    gr_attn_in = _mm_tn("d_attn_in", h0, dproj0, weights["attn_w_in"])
    st_mix0, tok = start_reduce("mix0", ["attn_w_out", "w_mem_kv0", "attn_w_in"], [gr_attn_out, gr_kv0, gr_attn_in])
    dh0 = _mm_nt("d_h_0", [dproj0], [weights["attn_w_in"]], out_dtypes=(BF,), after=[tok])[0]
    d0, _, g_mix0 = _rms_bwd("mix_norm_bwd_0", x0, mix_norm[0:1], dh0, d1)

    small_b = [g_mix0, g_mem0]
    sb_ssem, sb_rsem, sb_slots, tok = _small_start("small_start_b", _own_slot(_pack(small_b, 8), dev))
    sa_slots = _small_wait("small_wait_a", sa_ssem, sa_rsem, sa_slots, [tok])
    g_mix1, g_mem1, g_ffn, g_wsp, g_bsp, g_final, g_ln_g, g_ln_b = _unpack(_sum_devices("small_sum_a", sa_slots),
                                                                           [t.shape for t in small_a])
    sb_slots = _small_wait("small_wait_b", sb_ssem, sb_rsem, sb_slots, [g_final])
    g_mix0, g_mem0 = _unpack(_sum_devices("small_sum_b", sb_slots), [t.shape for t in small_b])
    st_mix0, tok = finish_reduce(st_mix0, [g_mix0])
    g_mix, g_mem = jnp.concatenate([g_mix0, g_mix1]), jnp.concatenate([g_mem0, g_mem1])
    shard_w = sgu_ln_g.shape[-1]
    g_ln_g = lax.dynamic_slice_in_dim(g_ln_g, chip * shard_w, shard_w, axis=1)
    g_ln_b = lax.dynamic_slice_in_dim(g_ln_b, chip * shard_w, shard_w, axis=1)
    small_names = ["mix_norm", "mem_norm", "ffn_norm", "sgu_w_spatial", "sgu_b_spatial", "final_norm", "sgu_ln_g",
                   "sgu_ln_b"]
    small_g = [g_mix, g_mem, g_ffn, g_wsp, g_bsp, g_final, g_ln_g, g_ln_b]
    small_shapes = [given_w[k].shape for k in small_names]
    packed = [_pack(t, LANES) for t in ([given_w[k] for k in small_names], small_g, [given_m[k] for k in small_names],
                                    [given_v[k] for k in small_names])]
    small_out = _adamw("adamw_small", packed[0][None], packed[1], packed[2][None], packed[3][None], 0)
    finish_update(st_mix0, [small_out[0]])
    for k, gk, dk, mk, vk in zip(small_names, *[_unpack(t[0], small_shapes) for t in small_out]):
        outputs[k] = (gk, dk, mk, vk)

    order = ["mix_norm", "mem_norm", "w_mem_kv", "ffn_norm", "w_gate", "w_up", "w_down", "attn_w_in", "attn_w_out",
             "sgu_w_in", "sgu_ln_g", "sgu_ln_b", "sgu_w_spatial", "sgu_b_spatial", "sgu_w_out", "final_norm"]
    return (loss, d0[None], *[outputs[k][0] for k in order], *[outputs[k][1] for k in order],
            *[outputs[k][2] for k in order], *[outputs[k][3] for k in order])
```

```python
import math

import jax
import jax.numpy as jnp
from jax import lax
from jax.experimental import pallas as pl
from jax.experimental.pallas import tpu as pltpu

F32 = jnp.float32
BF = jnp.bfloat16
MESH = pl.DeviceIdType.MESH

HEAD_DIM = 128
MEM_HEADS = 4
MEM_W = MEM_HEADS * HEAD_DIM
GROUP_W = 4 * HEAD_DIM
DILATIONS = (1, 4, 16)
BLK = 128
SGU_GROUPS = 12
SGU_W = SGU_GROUPS * HEAD_DIM
ROT_HALF = 16
ROPE_THETA = 500000.0
NORM_EPS = 1e-6
LN_EPS = 1e-5
NEG_INF = -1e30
SCALE = HEAD_DIM ** -0.5
ADAM_LR, ADAM_B1, ADAM_B2, ADAM_EPS, ADAM_WD, ADAM_STEP = 0.001, 0.9, 0.999, 1e-08, 0.01, 10

VMEM_LIMIT = 48 * 2 ** 20
VMEM_TILE_BUDGET = 38 * 2 ** 20
N_CHIPS = 4
N_DEV = 8
LANES = 128
EPILOGUE_CHUNK = 256

NT_DIMS = (((1,), (1,)), ((), ()))
TN_DIMS = (((0,), (0,)), ((), ()))
NN_DIMS = (((1,), (0,)), ((), ()))

ANY = pl.BlockSpec(memory_space=pl.ANY)
HBM = pl.BlockSpec(memory_space=pltpu.HBM)
SEM = pl.BlockSpec(memory_space=pltpu.SEMAPHORE)
EFFECT = pltpu.SideEffectType.DATAFLOW_SIDE_EFFECTING


def _params(sem):
    return pltpu.CompilerParams(dimension_semantics=sem, vmem_limit_bytes=VMEM_LIMIT)


def _pick(n, cap):
    if n <= cap:
        return n
    best = None
    for t in range(LANES, cap + 1, LANES):
        if n % t == 0:
            best = t
    assert best is not None, (n, cap)
    return best


def _pick_rows(n, cap):
    t = min(n, cap)
    while n % t:
        t //= 2
    return t


def _mm(name, dims, a_list, a_specs, b_list, b_specs, pairs, n_acc, acc_shape, grid, extras, e_specs,
        out_shapes, out_specs, epilogue, after=(), col_chunk=None, store=None, shard_width=None, norm_gain=None):
    na, nb, ne, no = len(a_list), len(b_list), len(extras), len(out_shapes)
    nk = grid[-1]
    ng = 0 if norm_gain is None else 1

    def products(a, b, cols=None):
        sums = [None] * n_acc
        for ai, bi, ci in pairs:
            bv = b[bi]
            if cols is None:
                bv = bv[...]
            elif dims == NT_DIMS:
                bv = bv[cols, :]
            else:
                bv = bv[:, cols]
            if bv.ndim == 3:
                bv = bv.reshape(-1, bv.shape[-1])
            prod = lax.dot_general(a[ai][...].astype(BF), bv.astype(BF), dims, preferred_element_type=F32)
            sums[ci] = prod if sums[ci] is None else sums[ci] + prod
        return sums

    def body(*refs):
        a = refs[:na]
        b = refs[na:na + nb]
        e = refs[na + nb:na + nb + ne]
        off = na + nb + ne + ng + len(after)
        o = refs[off:off + no]
        acc = refs[off + no:]

        def normed():
            if ng:
                xf = o[0][...]
                r = lax.rsqrt(jnp.mean(xf * xf, axis=-1, keepdims=True) + NORM_EPS)
                o[-1][...] = (xf * r * refs[na + nb + ne][...]).astype(o[-1].dtype)

        def finish(sums):
            outs = epilogue(sums, [r[...] for r in e])
            if store is not None:
                store(o, outs)
                return
            for r, v in zip(o, outs):
                r[...] = v.astype(r.dtype)
            normed()

        if nk == 1 and shard_width:
            (ai, bi, _), = pairs
            av = a[ai][...].astype(BF)
            if dims == NT_DIMS:
                total = None
                for j in range(N_CHIPS):
                    cols = slice(j * shard_width, (j + 1) * shard_width)
                    prod = lax.dot_general(av[:, cols], b[bi][j].astype(BF), dims, preferred_element_type=F32)
                    total = prod if total is None else total + prod
                finish([total])
                return
            for j in range(N_CHIPS):
                cols = slice(j * shard_width, (j + 1) * shard_width)
                prod = lax.dot_general(av, b[bi][j].astype(BF), dims, preferred_element_type=F32)
                outs = epilogue([prod], [r[:, cols] for r in e])
                for r, v in zip(o, outs):
                    r[:, cols] = v.astype(r.dtype)
            normed()
            return
        if nk == 1 and col_chunk:
            width = acc_shape[1]
            left = [r[...].astype(BF) for r in a]
            for c0 in range(0, width, col_chunk):
                cols = slice(c0, min(c0 + col_chunk, width))
                outs = epilogue(products(left, b, cols), [r[:, cols] for r in e])
                for r, v in zip(o, outs):
                    r[:, cols] = v.astype(r.dtype)
            return
        if nk == 1:
            finish(products(a, b))
            return
        k = pl.program_id(len(grid) - 1)

        @pl.when(k == 0)
        def _():
            for c, v in zip(acc, products(a, b)):
                c[...] = v

        @pl.when(jnp.logical_and(k > 0, k < nk - 1))
        def _():
            for c, v in zip(acc, products(a, b)):
                c[...] += v

        @pl.when(k == nk - 1)
        def _():
            finish([c[...] + v for c, v in zip(acc, products(a, b))])

    gains = [] if norm_gain is None else [norm_gain]
    ins = [*a_list, *b_list, *extras, *gains, *after]
    in_specs = [*a_specs, *b_specs, *e_specs, *[pl.BlockSpec(g.shape, lambda *_: (0, 0)) for g in gains],
                *([ANY] * len(after))]
    sem = ("parallel",) * (len(grid) - 1) + ("arbitrary",)
    scratch = [] if nk == 1 else [pltpu.VMEM(acc_shape, F32)] * n_acc
    return pl.pallas_call(
        body, out_shape=out_shapes, grid=grid, in_specs=in_specs, out_specs=out_specs, scratch_shapes=scratch,
        name=name, compiler_params=_params(sem))(*ins)


def _tile_bytes(blocks, single=()):
    size = lambda s, d: math.prod(s) * jnp.dtype(d).itemsize
    return sum(2 * size(s, d) for s, d in blocks) + sum(size(s, d) for s, d in single)


def _first(acc, extra):
    return [acc[0]]


def _sigmoid(x):
    return 0.5 * (1.0 + jnp.tanh(0.5 * x))


class Weight:
    def __init__(self, arr, axis):
        self.arr, self.axis = arr, axis
        _, self.rows, self.cols = arr.shape


SMALL_WEIGHT_BYTES = 8 * 2 ** 20


def _is_small(w):
    return w.arr.size * w.arr.dtype.itemsize <= SMALL_WEIGHT_BYTES


def _mm_nn(name, a, w, extras=(), epilogue=_first, out_dtypes=(F32,), after=(), norm_gain=None):
    m, kdim = a.shape
    b_spec, shard_width = None, None
    weight_buffers = 2
    if norm_gain is not None:
        out_dtypes = (*out_dtypes, BF)
    if w.axis == "col" and _is_small(w):
        n_total = tn = N_CHIPS * w.cols
        tk, gn, gk = kdim, 1, 1
        shard_width = w.cols
        b_spec = pl.BlockSpec((N_CHIPS, kdim, w.cols), lambda n, i, k: (0, 0, 0))
    elif w.axis == "col":
        n_total = N_CHIPS * w.cols
        tn = _pick(w.cols, 1408)
        tk = _pick(kdim, 2048)
        ncb = w.cols // tn
        gn, gk = N_CHIPS * ncb, kdim // tk
        b_map = lambda n, i, k: (n // ncb, k, n % ncb)
    elif kdim <= 2048 and norm_gain is not None:
        n_total = tn = w.cols
        tk, gn, gk = kdim, 1, 1
        weight_buffers = 1
        b_spec = pl.BlockSpec(w.arr.shape, lambda n, i, k: (0, 0, 0), pipeline_mode=pl.Buffered(1))
    elif kdim <= 2048:
        n_total = w.cols
        tn = _pick(w.cols, 1024)
        tk = kdim
        gn, gk = n_total // tn, 1
        b_spec = pl.BlockSpec((N_CHIPS, w.rows, tn), lambda n, i, k: (0, 0, n))
    else:
        n_total = w.cols
        tn = _pick(w.cols, 1024)
        tk = _pick(w.rows, 1408)
        nkb = w.rows // tk
        gn, gk = n_total // tn, N_CHIPS * nkb
        b_map = lambda n, i, k: (k // nkb, k % nkb, n)
    if b_spec is None:
        b_spec = pl.BlockSpec((None, tk, tn), b_map)
    for tm in (1024, 512, 256, 128):
        if m % tm:
            continue
        blocks = [((tm, tk), a.dtype)] + [((tm, tn), e.dtype) for e in extras]
        blocks += [((tm, tn), d) for d in out_dtypes] + [((tm, tn), BF)]
        weight = [((tk, tn), BF)]
        if _tile_bytes(blocks + (weight if weight_buffers == 2 else []), weight if weight_buffers == 1 else ()) <= VMEM_TILE_BUDGET:
            break
    assert norm_gain is None or tn == n_total, name
    o_spec = pl.BlockSpec((tm, tn), lambda n, i, k: (i, n))
    return _mm(
        name, NN_DIMS, [a], [pl.BlockSpec((tm, tk), lambda n, i, k: (i, k))],
        [w.arr], [b_spec], [(0, 0, 0)], 1, (tm, tn), (gn, m // tm, gk),
        list(extras), [o_spec] * len(extras),
        [jax.ShapeDtypeStruct((m, n_total), d) for d in out_dtypes], [o_spec] * len(out_dtypes), epilogue, after,
        shard_width=shard_width, norm_gain=norm_gain)


def _gate_up(name, h, wg, wu):
    m, kdim = h.shape
    tn = _pick(wg.cols, 1408)
    tk = _pick(kdim, 2048)
    ncb = wg.cols // tn
    single = kdim == tk
    for tm in (1024, 512, 256, 128):
        blocks = [((tm, tk), BF)] + [((tm, tn), BF)] * 3
        weights = [((tk, tn), BF)] * 2
        if m % tm == 0 and _tile_bytes(blocks + ([] if single else weights), weights if single else ()) <= VMEM_TILE_BUDGET:
            break
    b_spec = pl.BlockSpec((None, tk, tn), lambda n, i, k: (n // ncb, k, n % ncb),
                          pipeline_mode=pl.Buffered(1) if single else None)
    o_spec = pl.BlockSpec((tm, tn), lambda n, i, k: (i, n))
    n_total = N_CHIPS * wg.cols

    def epilogue(acc, extra):
        g, u = acc
        return [g, u, g * _sigmoid(g) * u]

    return _mm(
        name, NN_DIMS, [h], [pl.BlockSpec((tm, tk), lambda n, i, k: (i, k))], [wg.arr, wu.arr], [b_spec, b_spec],
        [(0, 0, 0), (0, 1, 1)], 2, (tm, tn), (N_CHIPS * ncb, m // tm, kdim // tk), [], [],
        [jax.ShapeDtypeStruct((m, n_total), BF)] * 3, [o_spec] * 3, epilogue, col_chunk=EPILOGUE_CHUNK)


def _mm_nt(name, dys, ws, out_dtypes=(F32,), extras=(), epilogue=_first, after=(), col_chunk=None):
    m = dys[0].shape[0]
    w0 = ws[0]
    npair = len(dys)
    b_spec, shard_width = None, None
    if w0.axis == "col" and npair == 1 and _is_small(w0):
        k_total = tko = w0.rows
        tkc = N_CHIPS * w0.cols
        go, gk = 1, 1
        shard_width = w0.cols
        b_spec = pl.BlockSpec(w0.arr.shape, lambda o, i, k: (0, 0, 0))
    elif w0.axis == "col":
        k_total = w0.rows
        tko = _pick(k_total, 1024)
        tkc = _pick(w0.cols, 1408)
        nkb = w0.cols // tkc
        go, gk = k_total // tko, N_CHIPS * nkb
        b_map = lambda o, i, k: (k // nkb, o, k % nkb)
    else:
        k_total = N_CHIPS * w0.rows
        tko = _pick(w0.rows, 1408)
        tkc = _pick(w0.cols, 2048)
        nob = w0.rows // tko
        go, gk = N_CHIPS * nob, w0.cols // tkc
        b_map = lambda o, i, k: (o // nob, o % nob, k)
    single = gk == 1
    for tm in (1024, 512, 256, 128):
        if m % tm:
            continue
        blocks = [((tm, tkc), d.dtype) for d in dys]
        blocks += [((tm, tko), e.dtype) for e in extras] + [((tm, tko), d) for d in out_dtypes]
        blocks += [((tm, tko), BF)]
        weights = [((tko, tkc), BF)] * npair
        if _tile_bytes(blocks + ([] if single else weights), weights if single else ()) <= VMEM_TILE_BUDGET:
            break
    if b_spec is None:
        b_spec = pl.BlockSpec((None, tko, tkc), b_map, pipeline_mode=pl.Buffered(1) if single else None)
    o_spec = pl.BlockSpec((tm, tko), lambda o, i, k: (i, o))
    return _mm(
        name, NT_DIMS, list(dys), [pl.BlockSpec((tm, tkc), lambda o, i, k: (i, k))] * npair,
        [w.arr for w in ws], [b_spec] * npair,
        [(i, i, 0) for i in range(npair)], 1, (tm, tko), (go, m // tm, gk), list(extras), [o_spec] * len(extras),
        [jax.ShapeDtypeStruct((m, k_total), d) for d in out_dtypes], [o_spec] * len(out_dtypes), epilogue, after,
        col_chunk if gk == 1 and shard_width is None else None, shard_width=shard_width)


def _mm_tn(name, a, dy, w, after=()):
    m, k_total = a.shape
    rows2 = w.rows // 2
    tn = _pick(w.cols, 1408)
    ncb = w.cols // tn
    epilogue, store = _first, None
    if k_total <= 2048 and w.axis == "col" and _is_small(w):
        tkr, tn = k_total, N_CHIPS * w.cols
        gr, gn = 1, 1
        o_spec = pl.BlockSpec((2, N_CHIPS, rows2, w.cols), lambda r, n, t: (0, 0, 0, 0))

        def store(o_refs, outs):
            for j in range(N_CHIPS):
                for h in range(2):
                    o_refs[0][h, j] = outs[0][h * rows2:(h + 1) * rows2, j * w.cols:(j + 1) * w.cols].astype(BF)
    elif k_total <= 2048 and w.axis == "col":
        tkr = k_total
        gr, gn = 1, N_CHIPS * ncb
        o_spec = pl.BlockSpec((2, None, rows2, tn), lambda r, n, t: (0, n // ncb, 0, n % ncb))
        epilogue = lambda acc, extra: [acc[0].reshape(2, rows2, tn)]
    elif k_total <= 2048:
        tkr = k_total
        gr, gn = 1, ncb
        o_spec = pl.BlockSpec((2, N_CHIPS, rows2, tn), lambda r, n, t: (0, 0, 0, n))

        def store(o_refs, outs):
            for j in range(N_CHIPS):
                for h in range(2):
                    lo = (2 * j + h) * rows2
                    o_refs[0][h, j] = outs[0][lo:lo + rows2].astype(BF)
    elif rows2 % LANES:
        tkr = w.rows
        assert w.axis == "row"
        gr, gn = N_CHIPS, ncb
        o_spec = pl.BlockSpec((2, None, rows2, tn), lambda r, n, t: (0, r, 0, n))
        epilogue = lambda acc, extra: [acc[0].reshape(2, rows2, tn)]
    else:
        tkr = _pick(rows2, 1408)
        nrb = rows2 // tkr
        if w.axis == "col":
            gr, gn = w.rows // tkr, N_CHIPS * ncb
            o_map = lambda r, n, t: (r // nrb, n // ncb, r % nrb, n % ncb)
        else:
            per = w.rows // tkr
            gr, gn = N_CHIPS * per, ncb
            o_map = lambda r, n, t: ((r % per) // nrb, r // per, (r % per) % nrb, n)
        o_spec = pl.BlockSpec((None, None, tkr, tn), o_map)
    for tmk in (1024, 512, 256, 128):
        blocks = [((tmk, tkr), a.dtype), ((tmk, tn), dy.dtype), ((tkr, tn), BF), ((tkr, tn), BF)]
        if m % tmk == 0 and _tile_bytes(blocks) <= VMEM_TILE_BUDGET:
            break
    return _mm(
        name, TN_DIMS, [a], [pl.BlockSpec((tmk, tkr), lambda r, n, t: (t, r))],
        [dy], [pl.BlockSpec((tmk, tn), lambda r, n, t: (t, n))], [(0, 0, 0)], 1, (tkr, tn), (gr, gn, m // tmk), [], [],
        [jax.ShapeDtypeStruct((2, N_CHIPS, rows2, w.cols), BF)], [o_spec], epilogue, after, store=store)[0]


def _rms_fwd(name, x, g, after=()):
    s, d = x.shape
    tr = _pick_rows(s, 512)

    def body(x_ref, g_ref, *rest):
        h_ref = rest[-1]
        xf = x_ref[...]
        r = lax.rsqrt(jnp.mean(xf * xf, axis=-1, keepdims=True) + NORM_EPS)
        h_ref[...] = (xf * r * g_ref[...]).astype(BF)

    return pl.pallas_call(
        body, out_shape=jax.ShapeDtypeStruct((s, d), BF), grid=(s // tr,),
        in_specs=[pl.BlockSpec((tr, d), lambda i: (i, 0)), pl.BlockSpec((1, d), lambda i: (0, 0))] + [ANY] * len(after),
        out_specs=pl.BlockSpec((tr, d), lambda i: (i, 0)), name=name, compiler_params=_params(("parallel",)))(x, g, *after)


def _rms_bwd(name, x, g, dh, dres=None):
    s, d = x.shape
    tr = _pick_rows(s, 256)
    has_res = dres is not None

    def body(*refs):
        if has_res:
            x_ref, g_ref, dh_ref, dres_ref, dx_ref, dxb_ref, dg_ref = refs
        else:
            x_ref, g_ref, dh_ref, dx_ref, dxb_ref, dg_ref = refs
        xf = x_ref[...]
        r = lax.rsqrt(jnp.mean(xf * xf, axis=-1, keepdims=True) + NORM_EPS)
        xr = xf * r
        dy = dh_ref[...].astype(F32)
        a = dy * g_ref[...]
        dx = r * (a - xr * jnp.mean(a * xr, axis=-1, keepdims=True))
        if has_res:
            dx = dx + dres_ref[...]
        dx_ref[...] = dx
        dxb_ref[...] = dx.astype(BF)

        @pl.when(pl.program_id(0) == 0)
        def _():
            dg_ref[...] = jnp.zeros_like(dg_ref)

        dg_ref[...] += jnp.sum(dy * xr, axis=0, keepdims=True)

    row = pl.BlockSpec((tr, d), lambda i: (i, 0))
    vec = pl.BlockSpec((1, d), lambda i: (0, 0))
    ins = [x, g, dh] + ([dres] if has_res else [])
    in_specs = [row, vec, row] + ([row] if has_res else [])
    return pl.pallas_call(
        body, out_shape=[jax.ShapeDtypeStruct((s, d), F32), jax.ShapeDtypeStruct((s, d), BF),
                         jax.ShapeDtypeStruct((1, d), F32)],
        grid=(s // tr,), in_specs=in_specs, out_specs=[row, row, vec], name=name,
        compiler_params=_params(("arbitrary",)))(*ins)


def _final_loss(x, tgt, g):
    s, d = x.shape
    tr = _pick_rows(s, 256)

    def body(x_ref, t_ref, g_ref, dx_ref, dxb_ref, dg_ref, loss_ref):
        xf = x_ref[...]
        gain = g_ref[...]
        r = lax.rsqrt(jnp.mean(xf * xf, axis=-1, keepdims=True) + NORM_EPS)
        xr = xf * r
        err = xr * gain - t_ref[...]
        dy = err * (1.0 / d)
        a = dy * gain
        dx = r * (a - xr * jnp.mean(a * xr, axis=-1, keepdims=True))
        dx_ref[...] = dx
        dxb_ref[...] = dx.astype(BF)

        @pl.when(pl.program_id(0) == 0)
        def _():
            dg_ref[...] = jnp.zeros_like(dg_ref)
            loss_ref[...] = jnp.zeros_like(loss_ref)

        dg_ref[...] += jnp.sum(dy * xr, axis=0, keepdims=True)
        part = 0.5 * jnp.sum(jnp.mean(err * err, axis=-1, keepdims=True), axis=0, keepdims=True)
        loss_ref[...] += jnp.broadcast_to(part, loss_ref.shape)

    row = pl.BlockSpec((tr, d), lambda i: (i, 0))
    vec = pl.BlockSpec((1, d), lambda i: (0, 0))
    return pl.pallas_call(
        body, out_shape=[jax.ShapeDtypeStruct((s, d), F32), jax.ShapeDtypeStruct((s, d), BF),
                         jax.ShapeDtypeStruct((1, d), F32), jax.ShapeDtypeStruct((8, LANES), F32)],
        grid=(s // tr,), in_specs=[row, row, vec],
        out_specs=[row, row, vec, pl.BlockSpec((8, LANES), lambda i: (0, 0))],
        name="final_loss", compiler_params=_params(("arbitrary",)))(x, tgt, g)


def _swiglu_bwd_epilogue(acc, extra):
    dact = acc[0]
    g, u = extra[0].astype(F32), extra[1].astype(F32)
    sig = _sigmoid(g)
    return [dact * u * sig * (1.0 + g * (1.0 - sig)), dact * g * sig]


GELU_C = math.sqrt(2.0 / math.pi)
GELU_A = 0.044715


def _gelu(x):
    return 0.5 * x * (1.0 + jnp.tanh(GELU_C * (x + GELU_A * x * x * x)))


def _gelu_both(x):
    x2 = x * x
    t = jnp.tanh(GELU_C * (x + GELU_A * x2 * x))
    half = 0.5 * (1.0 + t)
    return x * half, half + 0.5 * x * (1.0 - t * t) * GELU_C * (1.0 + 3.0 * GELU_A * x2)


def _rope_tables(positions):
    inv_freq = ROPE_THETA ** (-jnp.arange(ROT_HALF, dtype=F32) / ROT_HALF)
    ang = positions.astype(F32)[:, None] * inv_freq
    cos, sin = jnp.cos(ang), jnp.sin(ang)
    s = ang.shape[0]
    rest = HEAD_DIM - 2 * ROT_HALF
    zeros = jnp.zeros((s, ROT_HALF), F32)
    cos_t = jnp.concatenate([cos, cos, jnp.ones((s, rest), F32)], axis=1)
    sin_a = jnp.concatenate([-sin, zeros, jnp.zeros((s, rest), F32)], axis=1)
    sin_b = jnp.concatenate([zeros, sin, jnp.zeros((s, rest), F32)], axis=1)
    return cos_t, sin_a, sin_b


def _rope_head(xh, cos_t, sin_a, sin_b):
    up = pltpu.roll(xh, HEAD_DIM - ROT_HALF, 1)
    down = pltpu.roll(xh, ROT_HALF, 1)
    return xh * cos_t + up * sin_a + down * sin_b


def _residue(r, rows, dil):
    return slice(None) if dil == 1 else pl.ds(r, rows, stride=dil)


ROPE_TILE = 256
N_PARTS = 9
HEADS_PER_GROUP = GROUP_W // HEAD_DIM
N_HEADS_IN = N_PARTS * HEADS_PER_GROUP


def _rope_fwd(proj, tables):
    s = proj.shape[0]
    tm = _pick_rows(s, ROPE_TILE)

    def body(*refs):
        heads = refs[:N_HEADS_IN]
        c_ref, sa_ref, sb_ref = refs[N_HEADS_IN:N_HEADS_IN + 3]
        outs = refs[N_HEADS_IN + 3:]
        for g, dil in enumerate(DILATIONS):
            rows = tm // dil
            for r in range(dil):
                rs = _residue(r, rows, dil)
                cos_t, sin_a, sin_b = c_ref[rs, :], sa_ref[rs, :], sb_ref[rs, :]
                for kind in range(3):
                    part = 3 * kind + g
                    for h in range(HEADS_PER_GROUP):
                        xh = heads[part * HEADS_PER_GROUP + h][rs, :]
                        if kind < 2:
                            xh = _rope_head(xh, cos_t, sin_a, sin_b)
                        outs[part][r, :, h * HEAD_DIM:(h + 1) * HEAD_DIM] = xh.astype(BF)

    tab = pl.BlockSpec((tm, HEAD_DIM), lambda i: (i, 0))
    head_specs = [pl.BlockSpec((tm, HEAD_DIM), lambda i, j=j: (i, j)) for j in range(N_HEADS_IN)]
    shapes, specs = [], []
    for part in range(N_PARTS):
        dil = DILATIONS[part % 3]
        shapes.append(jax.ShapeDtypeStruct((dil, s // dil, GROUP_W), BF))
        specs.append(pl.BlockSpec((dil, tm // dil, GROUP_W), lambda i: (0, i, 0)))
    return pl.pallas_call(
        body, out_shape=shapes, grid=(s // tm,), in_specs=head_specs + [tab, tab, tab], out_specs=specs,
        name="rope_fwd", compiler_params=_params(("parallel",)))(*([proj] * N_HEADS_IN), *tables)


def _rope_bwd(parts, tables, into):
    s = into.shape[0]
    tm = _pick_rows(s, ROPE_TILE)

    def body(*refs):
        ins = refs[:N_PARTS]
        c_ref, sa_ref, sb_ref, into_ref, o_ref, scr = refs[N_PARTS:]
        for g, dil in enumerate(DILATIONS):
            rows = tm // dil
            for r in range(dil):
                rs = _residue(r, rows, dil)
                cos_t, sin_a, sin_b = c_ref[rs, :], -sa_ref[rs, :], -sb_ref[rs, :]
                for kind in range(3):
                    part = 3 * kind + g
                    for h in range(HEADS_PER_GROUP):
                        xh = ins[part][r, :, h * HEAD_DIM:(h + 1) * HEAD_DIM]
                        if kind < 2:
                            xh = _rope_head(xh, cos_t, sin_a, sin_b)
                        scr[part * HEADS_PER_GROUP + h, rs, :] = xh
        for j in range(N_HEADS_IN):
            o_ref[:, j * HEAD_DIM:(j + 1) * HEAD_DIM] = scr[j].astype(BF)

    tab = pl.BlockSpec((tm, HEAD_DIM), lambda i: (i, 0))
    i_specs = [pl.BlockSpec((DILATIONS[p % 3], tm // DILATIONS[p % 3], GROUP_W), lambda i: (0, i, 0))
               for p in range(N_PARTS)]
    return pl.pallas_call(
        body, out_shape=jax.ShapeDtypeStruct(into.shape, into.dtype), grid=(s // tm,),
        in_specs=i_specs + [tab] * 3 + [ANY], out_specs=pl.BlockSpec((tm, N_PARTS * GROUP_W), lambda i: (i, 0)),
        scratch_shapes=[pltpu.VMEM((N_HEADS_IN, tm, HEAD_DIM), F32)], input_output_aliases={N_PARTS + 3: 0},
        name="rope_bwd", compiler_params=_params(("parallel",)))(*parts, *tables, into)


def _band_mask(n):
    qi = lax.broadcasted_iota(jnp.int32, (BLK, 2 * BLK), 0)
    ki = lax.broadcasted_iota(jnp.int32, (BLK, 2 * BLK), 1)
    prev = jnp.logical_and(jnp.logical_and(ki < BLK, ki >= qi), n > 0)
    return jnp.logical_or(prev, jnp.logical_and(ki >= BLK, qi >= ki - BLK))


Q_BLOCKS = 4
Q_ROWS = Q_BLOCKS * BLK


STAT_LANES = HEAD_DIM // HEADS_PER_GROUP


def _stat_of(ref, rows, h):
    return ref[rows, h * STAT_LANES:h * STAT_LANES + 1]


def _pack_stats(cols):
    rows = cols[0].shape[0]
    lane = lax.broadcasted_iota(jnp.int32, (rows, HEAD_DIM), 1)
    tile = jnp.broadcast_to(cols[-1], (rows, HEAD_DIM))
    for h in range(HEADS_PER_GROUP - 2, -1, -1):
        tile = jnp.where(lane < (h + 1) * STAT_LANES, cols[h], tile)
    return tile


def _dil_specs(n_steps):
    last = n_steps - 1
    own = pl.BlockSpec((None, Q_ROWS, GROUP_W), lambda r, n: (r, jnp.minimum(n, last), 0))
    before = pl.BlockSpec((None, BLK, GROUP_W), lambda r, n: (r, jnp.maximum(Q_BLOCKS * n - 1, 0), 0))
    stat = pl.BlockSpec((None, Q_ROWS, HEAD_DIM), lambda r, n: (r, jnp.minimum(n, last), 0))
    return own, before, stat


def _dil_fwd(g, q, k, v):
    dil, length, _ = q.shape
    n_steps = length // Q_ROWS

    def body(q_ref, ko_ref, kb_ref, vo_ref, vb_ref, o_ref, lse_ref):
        n = pl.program_id(1)
        lse_cols = [[] for _ in range(Q_BLOCKS)]
        for h in range(GROUP_W // HEAD_DIM):
            sl = slice(h * HEAD_DIM, (h + 1) * HEAD_DIM)
            keys = jnp.concatenate([kb_ref[:, sl], ko_ref[:, sl]], axis=0)
            vals = jnp.concatenate([vb_ref[:, sl], vo_ref[:, sl]], axis=0)
            for j in range(Q_BLOCKS):
                rows, win = slice(j * BLK, (j + 1) * BLK), slice(j * BLK, (j + 2) * BLK)
                sc = lax.dot_general(q_ref[rows, sl], keys[win], NT_DIMS, preferred_element_type=F32) * SCALE
                sc = jnp.where(_band_mask(Q_BLOCKS * n + j), sc, NEG_INF)
                mx = jnp.max(sc, axis=-1, keepdims=True)
                p = jnp.exp(sc - mx)
                den = jnp.sum(p, axis=-1, keepdims=True)
                o_ref[rows, sl] = jnp.dot(p.astype(BF), vals[win], preferred_element_type=F32) / den
                lse_cols[j].append(mx + jnp.log(den))
        for j in range(Q_BLOCKS):
            lse_ref[j * BLK:(j + 1) * BLK, :] = _pack_stats(lse_cols[j])

    own, before, stat = _dil_specs(n_steps)
    return pl.pallas_call(
        body, out_shape=[jax.ShapeDtypeStruct(q.shape, F32), jax.ShapeDtypeStruct((dil, length, HEAD_DIM), F32)],
        grid=(dil, n_steps), in_specs=[own, own, before, own, before], out_specs=[own, stat], name=f"dil_fwd_{g}",
        compiler_params=_params(("parallel", "arbitrary")))(q, k, k, v, v)


def _dil_bwd(g, q, k, v, do, lse, delta):
    dil, length, _ = q.shape
    n_steps = length // Q_ROWS

    def body(q_ref, ko_ref, kb_ref, vo_ref, vb_ref, do_ref, lse_ref, dl_ref, dq_ref, dk_ref, dv_ref, ck_ref, cv_ref):
        n = pl.program_id(1)
        live = n < n_steps

        @pl.when(n == 0)
        def _():
            ck_ref[...] = jnp.zeros_like(ck_ref)
            cv_ref[...] = jnp.zeros_like(cv_ref)

        @pl.when(jnp.logical_not(live))
        def _():
            dk_ref[...] = ck_ref[...]
            dv_ref[...] = cv_ref[...]

        @pl.when(live)
        def _():
            for h in range(GROUP_W // HEAD_DIM):
                sl = slice(h * HEAD_DIM, (h + 1) * HEAD_DIM)
                keys = jnp.concatenate([kb_ref[:, sl], ko_ref[:, sl]], axis=0)
                vals = jnp.concatenate([vb_ref[:, sl], vo_ref[:, sl]], axis=0)
                dks, dvs = [], []
                for j in range(Q_BLOCKS):
                    rows, win = slice(j * BLK, (j + 1) * BLK), slice(j * BLK, (j + 2) * BLK)
                    qh, doh = q_ref[rows, sl], do_ref[rows, sl]
                    lse_h, dl_h = _stat_of(lse_ref, rows, h), _stat_of(dl_ref, rows, h)
                    sc = lax.dot_general(qh, keys[win], NT_DIMS, preferred_element_type=F32) * SCALE
                    p = jnp.where(_band_mask(Q_BLOCKS * n + j), jnp.exp(jnp.minimum(sc - lse_h, 0.0)), 0.0)
                    dp = lax.dot_general(doh, vals[win], NT_DIMS, preferred_element_type=F32)
                    ds = (p * (dp - dl_h) * SCALE).astype(BF)
                    dq_ref[rows, sl] = jnp.dot(ds, keys[win], preferred_element_type=F32)
                    dks.append(lax.dot_general(ds, qh, TN_DIMS, preferred_element_type=F32))
                    dvs.append(lax.dot_general(p.astype(BF), doh, TN_DIMS, preferred_element_type=F32))
                for out_ref, carry, parts in ((dk_ref, ck_ref, dks), (dv_ref, cv_ref, dvs)):
                    out_ref[:Q_ROWS - BLK, sl] = carry[:Q_ROWS - BLK, sl]
                    out_ref[Q_ROWS - BLK:, sl] = carry[Q_ROWS - BLK:, sl] + parts[0][:BLK]
                    for j in range(Q_BLOCKS - 1):
                        carry[j * BLK:(j + 1) * BLK, sl] = parts[j][BLK:] + parts[j + 1][:BLK]
                    carry[Q_ROWS - BLK:, sl] = parts[-1][BLK:]

    own, before, stat = _dil_specs(n_steps)
    behind = pl.BlockSpec((None, Q_ROWS, GROUP_W), lambda r, n: (r, jnp.maximum(n - 1, 0), 0))
    return pl.pallas_call(
        body, out_shape=[jax.ShapeDtypeStruct(q.shape, F32)] * 3, grid=(dil, n_steps + 1),
        in_specs=[own, own, before, own, before, own, stat, stat], out_specs=[own, behind, behind],
        scratch_shapes=[pltpu.VMEM((Q_ROWS, GROUP_W), F32)] * 2, name=f"dil_bwd_{g}",
        compiler_params=_params(("parallel", "arbitrary")))(q, k, k, v, v, do, lse, delta)


def _major_specs(s, tm, dtype, width=GROUP_W):
    shapes = [jax.ShapeDtypeStruct((dil, s // dil, width), dtype) for dil in DILATIONS]
    specs = [pl.BlockSpec((dil, tm // dil, width), lambda i: (0, i, 0)) for dil in DILATIONS]
    return shapes, specs


def _attn_merge(outs, lses):
    s = outs[0].shape[1]
    tm = _pick_rows(s, ROPE_TILE)

    def body(o0, o1, o2, l0, l1, l2, m_ref, e0, e1, e2, so1, so2, sl1, sl2, se):
        for dil, src, dst in ((DILATIONS[1], l1, sl1), (DILATIONS[2], l2, sl2)):
            for r in range(dil):
                dst[_residue(r, tm // dil, dil), :] = src[r]
        a, b, c = l0[0], sl1[...], sl2[...]
        mx = jnp.maximum(jnp.maximum(a, b), c)
        ea, eb, ec = jnp.exp(a - mx), jnp.exp(b - mx), jnp.exp(c - mx)
        den = ea + eb + ec
        wa, wb, wc = ea / den, eb / den, ec / den
        se[...] = mx + jnp.log(den)
        for dil, dst in zip(DILATIONS, (e0, e1, e2)):
            for r in range(dil):
                dst[r] = se[_residue(r, tm // dil, dil), :]
        for h in range(HEADS_PER_GROUP):
            sl = slice(h * HEAD_DIM, (h + 1) * HEAD_DIM)
            col = slice(h * STAT_LANES, h * STAT_LANES + 1)
            for dil, src, dst in ((DILATIONS[1], o1, so1), (DILATIONS[2], o2, so2)):
                for r in range(dil):
                    dst[h, _residue(r, tm // dil, dil), :] = src[r, :, sl]
            m_ref[:, sl] = (wa[:, col] * o0[0, :, sl] + wb[:, col] * so1[h] + wc[:, col] * so2[h]).astype(BF)

    shapes, specs = _major_specs(s, tm, F32)
    stat_shapes, stat_specs = _major_specs(s, tm, F32, HEAD_DIM)
    nat = pl.BlockSpec((tm, GROUP_W), lambda i: (i, 0))
    res = pl.pallas_call(
        body, out_shape=[jax.ShapeDtypeStruct((s, GROUP_W + MEM_W), BF)] + stat_shapes, grid=(s // tm,),
        in_specs=specs + stat_specs, out_specs=[nat] + stat_specs,
        scratch_shapes=[pltpu.VMEM((HEADS_PER_GROUP, tm, HEAD_DIM), F32)] * 2 + [pltpu.VMEM((tm, HEAD_DIM), F32)] * 3,
        name="attn_merge", compiler_params=_params(("parallel",)))(*outs, *lses)
    return res[0], res[1:]


def _attn_delta(dcat, merged, after=()):
    s = merged.shape[0]
    tm = _pick_rows(s, ROPE_TILE)

    def body(*refs):
        d_refs, m_ref = refs[:HEADS_PER_GROUP], refs[HEADS_PER_GROUP]
        do_refs, dl_refs, scr = refs[-7:-4], refs[-4:-1], refs[-1]
        sums = []
        for h in range(HEADS_PER_GROUP):
            sl = slice(h * HEAD_DIM, (h + 1) * HEAD_DIM)
            sums.append(jnp.sum(d_refs[h][...] * m_ref[:, sl].astype(F32), axis=-1, keepdims=True))
            for dil, do_ref in zip(DILATIONS, do_refs):
                for r in range(dil):
                    do_ref[r, :, sl] = d_refs[h][_residue(r, tm // dil, dil), :].astype(BF)
        scr[...] = _pack_stats(sums)
        for dil, dl_ref in zip(DILATIONS, dl_refs):
            for r in range(dil):
                dl_ref[r] = scr[_residue(r, tm // dil, dil), :]

    nat = pl.BlockSpec((tm, GROUP_W), lambda i: (i, 0))
    head_specs = [pl.BlockSpec((tm, HEAD_DIM), lambda i, h=h: (i, h)) for h in range(HEADS_PER_GROUP)]
    bf_shapes, specs = _major_specs(s, tm, BF)
    stat_shapes, stat_specs = _major_specs(s, tm, F32, HEAD_DIM)
    res = pl.pallas_call(
        body, out_shape=bf_shapes + stat_shapes, grid=(s // tm,), in_specs=head_specs + [nat] + [ANY] * len(after),
        out_specs=specs + stat_specs, scratch_shapes=[pltpu.VMEM((tm, HEAD_DIM), F32)], name="attn_delta",
        compiler_params=_params(("parallel",)))(*([dcat] * HEADS_PER_GROUP), merged, *after)
    return res[:3], res[3:]


def _mem_probs(qh, kh):
    sc = lax.dot_general(qh, kh, NT_DIMS, preferred_element_type=F32) * SCALE
    p = jnp.exp(sc - jnp.max(sc, axis=-1, keepdims=True))
    return p, jnp.sum(p, axis=-1, keepdims=True)


def _mem_fwd(name, proj, q_block, kv, into, out_block):
    s = proj.shape[0]
    tq = _pick_rows(s, 512)

    def body(q_ref, kv_ref, into_ref, o_ref):
        for h in range(MEM_HEADS):
            sl = slice(h * HEAD_DIM, (h + 1) * HEAD_DIM)
            vsl = slice(MEM_W + h * HEAD_DIM, MEM_W + (h + 1) * HEAD_DIM)
            p, den = _mem_probs(q_ref[:, sl].astype(BF), kv_ref[:, sl].astype(BF))
            out = jnp.dot(p.astype(BF), kv_ref[:, vsl].astype(BF), preferred_element_type=F32) / den
            o_ref[:, sl] = out.astype(o_ref.dtype)

    return pl.pallas_call(
        body, out_shape=jax.ShapeDtypeStruct(into.shape, into.dtype), grid=(s // tq,),
        in_specs=[pl.BlockSpec((tq, MEM_W), lambda i: (i, q_block)), pl.BlockSpec(kv.shape, lambda i: (0, 0)), ANY],
        out_specs=pl.BlockSpec((tq, MEM_W), lambda i: (i, out_block)), input_output_aliases={2: 0}, name=name,
        compiler_params=_params(("parallel",)))(proj, kv, into)


def _mem_bwd(name, proj, q_block, kv, dcat, d_block, width):
    s = proj.shape[0]
    tq = _pick_rows(s, 512)

    def body(q_ref, kv_ref, do_ref, dq_ref, dkv_ref):
        @pl.when(pl.program_id(0) == 0)
        def _():
            dkv_ref[...] = jnp.zeros_like(dkv_ref)

        for h in range(MEM_HEADS):
            sl = slice(h * HEAD_DIM, (h + 1) * HEAD_DIM)
            vsl = slice(MEM_W + h * HEAD_DIM, MEM_W + (h + 1) * HEAD_DIM)
            qh, kh, vh = q_ref[:, sl].astype(BF), kv_ref[:, sl].astype(BF), kv_ref[:, vsl].astype(BF)
            doh = do_ref[:, sl].astype(BF)
            p, den = _mem_probs(qh, kh)
            p = p / den
            dp = lax.dot_general(doh, vh, NT_DIMS, preferred_element_type=F32)
            ds = (p * (dp - jnp.sum(p * dp, axis=-1, keepdims=True)) * SCALE).astype(BF)
            dq_ref[:, sl] = jnp.dot(ds, kh, preferred_element_type=F32).astype(BF)
            dkv_ref[:, sl] += lax.dot_general(ds, qh, TN_DIMS, preferred_element_type=F32)
            dkv_ref[:, vsl] += lax.dot_general(p.astype(BF), doh, TN_DIMS, preferred_element_type=F32)

    whole = pl.BlockSpec(kv.shape, lambda i: (0, 0))
    return pl.pallas_call(
        body, out_shape=[jax.ShapeDtypeStruct((s, width), BF), jax.ShapeDtypeStruct(kv.shape, F32)], grid=(s // tq,),
        in_specs=[pl.BlockSpec((tq, MEM_W), lambda i: (i, q_block)), whole,
                  pl.BlockSpec((tq, MEM_W), lambda i: (i, d_block))],
        out_specs=[pl.BlockSpec((tq, MEM_W), lambda i: (i, width // MEM_W - 1)), whole], name=name,
        compiler_params=_params(("arbitrary",)))(proj, kv, dcat)


def _causal():
    t = lax.broadcasted_iota(jnp.int32, (BLK, BLK), 0)
    s = lax.broadcasted_iota(jnp.int32, (BLK, BLK), 1)
    return t >= s


def _sgu_norm(vg, ln_g, ln_b):
    mu = jnp.mean(vg, axis=-1, keepdims=True)
    cen = vg - mu
    rstd = lax.rsqrt(jnp.mean(cen * cen, axis=-1, keepdims=True) + LN_EPS)
    xhat = cen * rstd
    return xhat, rstd, xhat * ln_g + ln_b


def _sgu_fwd(proj, ln_g, ln_b, w_sp, b_t):
    s = proj.shape[0]

    def body(u_ref, v_ref, g_ref, b_ref, w_ref, bt_ref, o_ref):
        _, _, vn = _sgu_norm(_gelu(v_ref[...].astype(F32)), g_ref[...], b_ref[...])
        vn = vn.astype(BF)
        tri = _causal()
        for grp in range(SGU_GROUPS):
            sl = slice(grp * HEAD_DIM, (grp + 1) * HEAD_DIM)
            w = jnp.where(tri, w_ref[grp], 0.0).astype(BF)
            mixed = jnp.dot(w, vn[:, sl], preferred_element_type=F32) + bt_ref[:, grp:grp + 1]
            o_ref[:, sl] = (_gelu(u_ref[:, sl].astype(F32)) * mixed).astype(BF)

    vec = pl.BlockSpec((1, SGU_W), lambda i: (0, 0))
    return pl.pallas_call(
        body, out_shape=jax.ShapeDtypeStruct((s, SGU_W + MEM_W), BF), grid=(s // BLK,),
        in_specs=[pl.BlockSpec((BLK, SGU_W), lambda i: (i, 0)), pl.BlockSpec((BLK, SGU_W), lambda i: (i, 1)), vec, vec,
                  pl.BlockSpec(w_sp.shape, lambda i: (0, 0, 0)), pl.BlockSpec(b_t.shape, lambda i: (0, 0))],
        out_specs=pl.BlockSpec((BLK, SGU_W), lambda i: (i, 0)), name="sgu_fwd",
        compiler_params=_params(("parallel",)))(proj, proj, ln_g, ln_b, w_sp, b_t)


def _sgu_bwd(proj, dcat, ln_g, ln_b, w_sp, b_t, into):
    s = proj.shape[0]

    def body(u_ref, v_ref, d_ref, g_ref, b_ref, w_ref, bt_ref, into_ref, dp_ref, dw_ref, db_ref, dg_ref, dbeta_ref,
             dvn_ref):
        @pl.when(pl.program_id(0) == 0)
        def _():
            dw_ref[...] = jnp.zeros_like(dw_ref)
            db_ref[...] = jnp.zeros_like(db_ref)
            dg_ref[...] = jnp.zeros_like(dg_ref)
            dbeta_ref[...] = jnp.zeros_like(dbeta_ref)

        gain = g_ref[...]
        vg, v_slope = _gelu_both(v_ref[...].astype(F32))
        xhat, rstd, vn = _sgu_norm(vg, gain, b_ref[...])
        vn = vn.astype(BF)
        tri = _causal()
        lane = lax.broadcasted_iota(jnp.int32, (BLK, HEAD_DIM), 1)
        db_acc = jnp.zeros((BLK, HEAD_DIM), F32)
        for grp in range(SGU_GROUPS):
            sl = slice(grp * HEAD_DIM, (grp + 1) * HEAD_DIM)
            w = jnp.where(tri, w_ref[grp], 0.0).astype(BF)
            vn_g = vn[:, sl]
            mixed = jnp.dot(w, vn_g, preferred_element_type=F32) + bt_ref[:, grp:grp + 1]
            u_act, u_slope = _gelu_both(u_ref[:, sl].astype(F32))
            d_out = d_ref[:, sl].astype(F32)
            dp_ref[:, sl] = (d_out * mixed * u_slope).astype(BF)
            dmixed = d_out * u_act
            dm = dmixed.astype(BF)
            dvn_ref[:, sl] = lax.dot_general(w, dm, TN_DIMS, preferred_element_type=F32)
            dw = lax.dot_general(dm, vn_g, NT_DIMS, preferred_element_type=F32)
            dw_ref[grp] += jnp.where(tri, dw, 0.0)
            db_acc += jnp.where(lane == grp, jnp.sum(dmixed, axis=-1, keepdims=True), 0.0)
        db_ref[...] += db_acc
        dvn = dvn_ref[...]
        dg_ref[...] += jnp.sum(dvn * xhat, axis=0, keepdims=True)
        dbeta_ref[...] += jnp.sum(dvn, axis=0, keepdims=True)
        dxh = dvn * gain
        dvg = rstd * (dxh - jnp.mean(dxh, axis=-1, keepdims=True) - xhat * jnp.mean(dxh * xhat, axis=-1, keepdims=True))
        dp_ref[:, SGU_W:] = (dvg * v_slope).astype(BF)

    vec = pl.BlockSpec((1, SGU_W), lambda i: (0, 0))
    row = pl.BlockSpec((BLK, SGU_W), lambda i: (i, 0))
    w_spec = pl.BlockSpec(w_sp.shape, lambda i: (0, 0, 0))
    sq = pl.BlockSpec((BLK, HEAD_DIM), lambda i: (0, 0))
    return pl.pallas_call(
        body,
        out_shape=[jax.ShapeDtypeStruct(into.shape, into.dtype),
                   jax.ShapeDtypeStruct(w_sp.shape, F32), jax.ShapeDtypeStruct((BLK, HEAD_DIM), F32),
                   jax.ShapeDtypeStruct((1, SGU_W), F32), jax.ShapeDtypeStruct((1, SGU_W), F32)],
        grid=(s // BLK,),
        in_specs=[row, pl.BlockSpec((BLK, SGU_W), lambda i: (i, 1)), row, vec, vec, w_spec,
                  pl.BlockSpec(b_t.shape, lambda i: (0, 0)), ANY],
        out_specs=[pl.BlockSpec((BLK, 2 * SGU_W), lambda i: (i, 0)), w_spec, sq, vec, vec],
        scratch_shapes=[pltpu.VMEM((BLK, SGU_W), F32)], input_output_aliases={7: 0}, name="sgu_bwd",
        compiler_params=_params(("arbitrary",)))(proj, proj, dcat, ln_g, ln_b, w_sp, b_t, into)


def _place():
    return lax.axis_index("x"), lax.axis_index("y"), lax.axis_index("c")


def _other_chips(x, y):
    return [(1 - x, y), (x, 1 - y), (1 - x, 1 - y)]


def _peer(x, y, c, mask):
    return (1 - x if mask & 4 else x, 1 - y if mask & 2 else y, 1 - c if mask & 1 else c)


def _in_hbm(a):
    return pltpu.with_memory_space_constraint(a, pltpu.HBM)


def _token_spec():
    return jax.ShapeDtypeStruct((8, LANES), F32), pl.BlockSpec(memory_space=pltpu.VMEM)


def _remote(src, dst, ssem, rsem, to):
    return pltpu.make_async_remote_copy(src_ref=src, dst_ref=dst, send_sem=ssem, recv_sem=rsem, device_id=to,
                                        device_id_type=MESH)


def _place_shard(name, src, layer, place, dtype, after=()):
    _, rows, cols = src.shape
    tr = _pick_rows(rows, 512)

    def body(p_ref, s_ref, *rest):
        rest[-1][...] = s_ref[...].astype(dtype)

    grid_spec = pltpu.PrefetchScalarGridSpec(
        num_scalar_prefetch=1, grid=(rows // tr,),
        in_specs=[pl.BlockSpec((None, tr, cols), lambda i, p: (layer, i, 0))] + [ANY] * len(after),
        out_specs=pl.BlockSpec((None, tr, cols), lambda i, p: (p[1], i, 0)))
    return pl.pallas_call(body, out_shape=jax.ShapeDtypeStruct((N_CHIPS, rows, cols), dtype), grid_spec=grid_spec,
                          name=name, compiler_params=_params(("parallel",)))(place, src, *after)


def _gather_copies(bufs, ssem, rsem):
    x, y, c = _place()
    me = 2 * x + y
    copies = []
    for ai, buf in enumerate(bufs):
        for k, (ox, oy) in enumerate(_other_chips(x, y)):
            copies.append(_remote(buf.at[me], buf.at[me], ssem.at[3 * ai + k], rsem.at[3 * ai + k], (ox, oy, c)))
    return copies


def _reduce_copies(grads, lands, ssem, rsem):
    x, y, c = _place()
    copies = []
    for a, (gr, land) in enumerate(zip(grads, lands)):
        for mask in range(1, N_DEV):
            px, py, pc = _peer(x, y, c, mask)
            copies.append(_remote(gr.at[pc, 2 * px + py], land.at[mask - 1], ssem.at[7 * a + mask - 1],
                                  rsem.at[7 * a + mask - 1], (px, py, pc)))
    return copies


def _half_copies(totals, ssem, rsem):
    x, y, c = _place()
    return [_remote(t.at[c], t.at[c], ssem.at[a], rsem.at[a], (x, y, 1 - c)) for a, t in enumerate(totals)]


def _gather_start(name, groups):
    flat = [s for grp in groups for s in grp]
    n, ng = len(flat), len(groups)

    def body(*refs):
        ins = refs[:n]
        sems = refs[n:n + 2 * ng]
        token = refs[-1]
        idx = 0
        for gi, grp in enumerate(groups):
            for cp in _gather_copies(ins[idx:idx + len(grp)], sems[2 * gi], sems[2 * gi + 1]):
                cp.start()
            idx += len(grp)
        token[...] = jnp.zeros_like(token)

    tok_shape, tok_spec = _token_spec()
    sem_shapes = []
    for grp in groups:
        sem_shapes += [pltpu.SemaphoreType.DMA((3 * len(grp),))] * 2
    res = pl.pallas_call(
        body, name=name,
        out_shape=(*sem_shapes, *[pltpu.HBM(s.shape, s.dtype) for s in flat], tok_shape),
        in_specs=[HBM] * n, out_specs=(*[SEM] * (2 * ng), *[HBM] * n, tok_spec),
        input_output_aliases={i: 2 * ng + i for i in range(n)},
        compiler_params=pltpu.CompilerParams(has_side_effects=EFFECT))(*[_in_hbm(s) for s in flat])
    out, idx = [], 2 * ng
    for gi, grp in enumerate(groups):
        out.append((res[2 * gi], res[2 * gi + 1], list(res[idx:idx + len(grp)])))
        idx += len(grp)
    return out, res[-1]


def _gather_wait(name, ssem, rsem, slabs, after):
    n = len(slabs)

    def body(*refs):
        for cp in _gather_copies(refs[:n], refs[n], refs[n + 1]):
            cp.wait_send()
            cp.wait_recv()

    return pl.pallas_call(
        body, name=name, out_shape=tuple(pltpu.HBM(s.shape, s.dtype) for s in slabs),
        in_specs=[HBM] * n + [SEM, SEM] + [ANY] * len(after), out_specs=tuple([HBM] * n),
        input_output_aliases={i: i for i in range(n)},
        compiler_params=pltpu.CompilerParams(has_side_effects=EFFECT))(*slabs, ssem, rsem, *after)


def _reduce_start(name, grads):
    n = len(grads)
    lands = [lax.empty((N_DEV - 1, *g.shape[2:]), g.dtype) for g in grads]

    def body(*refs):
        token = refs[-1]
        for cp in _reduce_copies(refs[:n], refs[n:2 * n], refs[2 * n], refs[2 * n + 1]):
            cp.start()
        token[...] = jnp.zeros_like(token)

    tok_shape, tok_spec = _token_spec()
    sems = [pltpu.SemaphoreType.DMA((7 * n,))] * 2
    res = pl.pallas_call(
        body, name=name,
        out_shape=(*sems, *[pltpu.HBM(g.shape, g.dtype) for g in grads], *[pltpu.HBM(l.shape, l.dtype) for l in lands],
                   tok_shape),
        in_specs=[HBM] * (2 * n), out_specs=(SEM, SEM, *[HBM] * (2 * n), tok_spec),
        input_output_aliases={i: 2 + i for i in range(2 * n)},
        compiler_params=pltpu.CompilerParams(has_side_effects=EFFECT))(*[_in_hbm(t) for t in (*grads, *lands)])
    return res[0], res[1], list(res[2:2 + n]), list(res[2 + n:2 + 2 * n]), res[-1]


def _reduce_wait(name, ssem, rsem, grads, lands, after):
    n = len(grads)

    def body(*refs):
        for cp in _reduce_copies(refs[:n], refs[n:2 * n], refs[2 * n], refs[2 * n + 1]):
            cp.wait_send()
            cp.wait_recv()

    res = pl.pallas_call(
        body, name=name, out_shape=tuple(pltpu.HBM(t.shape, t.dtype) for t in (*grads, *lands)),
        in_specs=[HBM] * (2 * n) + [SEM, SEM] + [ANY] * len(after), out_specs=tuple([HBM] * (2 * n)),
        input_output_aliases={i: i for i in range(2 * n)},
        compiler_params=pltpu.CompilerParams(has_side_effects=EFFECT))(*grads, *lands, ssem, rsem, *after)
    return list(res[:n]), list(res[n:])


def _sum_pieces(name, grad, land, place):
    _, _, rows, cols = grad.shape
    tr = _pick_rows(rows, 256)

    def body(p_ref, g_ref, l_ref, o_ref):
        tot = g_ref[...].astype(F32)
        for k in range(N_DEV - 1):
            tot = tot + l_ref[k].astype(F32)
        o_ref[...] = tot

    grid_spec = pltpu.PrefetchScalarGridSpec(
        num_scalar_prefetch=1, grid=(rows // tr,),
        in_specs=[pl.BlockSpec((None, None, tr, cols), lambda i, p: (p[0], p[1], i, 0)),
                  pl.BlockSpec((N_DEV - 1, tr, cols), lambda i, p: (0, i, 0))],
        out_specs=pl.BlockSpec((None, tr, cols), lambda i, p: (p[0], i, 0)))
    return pl.pallas_call(body, out_shape=jax.ShapeDtypeStruct((2, rows, cols), F32), grid_spec=grid_spec, name=name,
                          compiler_params=_params(("parallel",)))(place, grad, land)


def _first_copies(bufs, ssem, rsem):
    x, y, c = _place()
    me = 2 * x + y
    copies = []
    for a, buf in enumerate(bufs):
        for k, (ox, oy) in enumerate(_other_chips(x, y)):
            copies.append(_remote(buf.at[me, c], buf.at[me, c], ssem.at[3 * a + k], rsem.at[3 * a + k], (ox, oy, c)))
    return copies


def _pass_copies(bufs, ssem, rsem):
    x, y, c = _place()
    copies = []
    for a, buf in enumerate(bufs):
        for k, (ox, oy) in enumerate(_other_chips(x, y)):
            landed = buf.at[2 * ox + oy, c]
            copies.append(_remote(landed, landed, ssem.at[3 * a + k], rsem.at[3 * a + k], (x, y, 1 - c)))
    return copies


def _exchange_start(name, arrays, copies_fn, n_sems):
    n = len(arrays)

    def body(*refs):
        token = refs[-1]
        for cp in copies_fn(refs[:n], refs[n], refs[n + 1]):
            cp.start()
        token[...] = jnp.zeros_like(token)

    tok_shape, tok_spec = _token_spec()
    res = pl.pallas_call(
        body, name=name,
        out_shape=(pltpu.SemaphoreType.DMA((n_sems,)), pltpu.SemaphoreType.DMA((n_sems,)),
                   *[pltpu.HBM(t.shape, t.dtype) for t in arrays], tok_shape),
        in_specs=[HBM] * n, out_specs=(SEM, SEM, *[HBM] * n, tok_spec),
        input_output_aliases={i: 2 + i for i in range(n)},
        compiler_params=pltpu.CompilerParams(has_side_effects=EFFECT))(*[_in_hbm(t) for t in arrays])
    return res[0], res[1], list(res[2:2 + n]), res[-1]


def _exchange_wait(name, ssem, rsem, arrays, copies_fn, after):
    n = len(arrays)

    def body(*refs):
        for cp in copies_fn(refs[:n], refs[n], refs[n + 1]):
            cp.wait_send()
            cp.wait_recv()

    res = pl.pallas_call(
        body, name=name, out_shape=tuple(pltpu.HBM(t.shape, t.dtype) for t in arrays),
        in_specs=[HBM] * n + [SEM, SEM] + [ANY] * len(after), out_specs=tuple([HBM] * n),
        input_output_aliases={i: i for i in range(n)},
        compiler_params=pltpu.CompilerParams(has_side_effects=EFFECT))(*arrays, ssem, rsem, *after)
    return list(res)


def _half_start(name, totals):
    n = len(totals)

    def body(*refs):
        token = refs[-1]
        for cp in _half_copies(refs[:n], refs[n], refs[n + 1]):
            cp.start()
        token[...] = jnp.zeros_like(token)

    tok_shape, tok_spec = _token_spec()
    res = pl.pallas_call(
        body, name=name,
        out_shape=(pltpu.SemaphoreType.DMA((n,)), pltpu.SemaphoreType.DMA((n,)),
                   *[pltpu.HBM(t.shape, t.dtype) for t in totals], tok_shape),
        in_specs=[HBM] * n, out_specs=(SEM, SEM, *[HBM] * n, tok_spec),
        input_output_aliases={i: 2 + i for i in range(n)},
        compiler_params=pltpu.CompilerParams(has_side_effects=EFFECT))(*[_in_hbm(t) for t in totals])
    return res[0], res[1], list(res[2:2 + n]), res[-1]


def _half_wait(name, ssem, rsem, totals, after):
    n = len(totals)

    def body(*refs):
        for cp in _half_copies(refs[:n], refs[n], refs[n + 1]):
            cp.wait_send()
            cp.wait_recv()

    res = pl.pallas_call(
        body, name=name, out_shape=tuple(pltpu.HBM(t.shape, t.dtype) for t in totals),
        in_specs=[HBM] * n + [SEM, SEM] + [ANY] * len(after), out_specs=tuple([HBM] * n),
        input_output_aliases={i: i for i in range(n)},
        compiler_params=pltpu.CompilerParams(has_side_effects=EFFECT))(*totals, ssem, rsem, *after)
    return list(res)


def _small_copies(bufs, ssem, rsem):
    x, y, c = _place()
    mine = bufs[0].at[4 * x + 2 * y + c]
    return [_remote(mine, mine, ssem.at[mask - 1], rsem.at[mask - 1], _peer(x, y, c, mask)) for mask in range(1, N_DEV)]


def _small_start(name, slots):
    def body(s_ref, ssem, rsem, thru, token):
        for cp in _small_copies([s_ref], ssem, rsem):
            cp.start()
        token[...] = jnp.zeros_like(token)

    tok_shape, tok_spec = _token_spec()
    sems = [pltpu.SemaphoreType.DMA((N_DEV - 1,))] * 2
    return pl.pallas_call(
        body, name=name, out_shape=(*sems, pltpu.HBM(slots.shape, slots.dtype), tok_shape), in_specs=[HBM],
        out_specs=(SEM, SEM, HBM, tok_spec), input_output_aliases={0: 2},
        compiler_params=pltpu.CompilerParams(has_side_effects=EFFECT))(_in_hbm(slots))


def _small_wait(name, ssem, rsem, slots, after):
    def body(*refs):
        for cp in _small_copies([refs[0]], refs[1], refs[2]):
            cp.wait_send()
            cp.wait_recv()

    return pl.pallas_call(
        body, name=name, out_shape=pltpu.HBM(slots.shape, slots.dtype), in_specs=[HBM, SEM, SEM] + [ANY] * len(after),
        out_specs=HBM, input_output_aliases={0: 0},
        compiler_params=pltpu.CompilerParams(has_side_effects=EFFECT))(slots, ssem, rsem, *after)


def _own_slot(small, me):
    return lax.dynamic_update_slice(jnp.zeros((N_DEV, *small.shape), small.dtype), small[None], (me, 0, 0))


def _sum_devices(name, stacked):
    _, rows, lanes = stacked.shape
    tr = _pick_rows(rows, 512)

    def body(s_ref, o_ref):
        tot = s_ref[0]
        for k in range(1, N_DEV):
            tot = tot + s_ref[k]
        o_ref[...] = tot

    return pl.pallas_call(
        body, out_shape=jax.ShapeDtypeStruct((rows, lanes), F32), grid=(rows // tr,),
        in_specs=[pl.BlockSpec((N_DEV, tr, lanes), lambda i: (0, i, 0))], out_specs=pl.BlockSpec((tr, lanes), lambda i: (i, 0)),
        name=name, compiler_params=_params(("parallel",)))(stacked)


def _adamw(name, w, g, m, v, layer, prev=None):
    layers, rows, cols = w.shape
    tr = _pick_rows(rows, 256)
    c1 = 1.0 - ADAM_B1 ** ADAM_STEP
    c2 = 1.0 - ADAM_B2 ** ADAM_STEP

    def body(w_ref, g_ref, m_ref, v_ref, *rest):
        go_ref, d_ref, nm_ref, nv_ref = rest[-4:]
        gv = g_ref[...]
        nm = ADAM_B1 * m_ref[...] + (1.0 - ADAM_B1) * gv
        nv = ADAM_B2 * v_ref[...] + (1.0 - ADAM_B2) * (gv * gv)
        go_ref[...] = gv
        d_ref[...] = -ADAM_LR * ((nm / c1) / (jnp.sqrt(nv / c2) + ADAM_EPS) + ADAM_WD * w_ref[...])
        nm_ref[...] = nm
        nv_ref[...] = nv

    spec = pl.BlockSpec((None, tr, cols), lambda i: (layer, i, 0))
    prev = list(prev) if prev is not None else []
    return pl.pallas_call(
        body, out_shape=[jax.ShapeDtypeStruct((layers, rows, cols), F32)] * 4, grid=(rows // tr,),
        in_specs=[spec, pl.BlockSpec((tr, cols), lambda i: (i, 0)), spec, spec] + [ANY] * len(prev),
        out_specs=[spec] * 4, input_output_aliases={4 + i: i for i in range(len(prev))}, name=name,
        compiler_params=_params(("parallel",)))(w, g, m, v, *prev)


def _pack(vectors, pad_rows):
    flat = jnp.concatenate([t.reshape(-1) for t in vectors])
    rows = -(-flat.shape[0] // LANES)
    rows = -(-rows // pad_rows) * pad_rows
    return jnp.pad(flat, (0, rows * LANES - flat.shape[0])).reshape(rows, LANES)


def _unpack(packed, shapes):
    flat = packed.reshape(-1)
    out, off = [], 0
    for shp in shapes:
        size = math.prod(shp)
        out.append(flat[off:off + size].reshape(shp))
        off += size
    return out


def kernel(x, mem, positions, mix_norm, mem_norm, w_mem_kv, ffn_norm, w_gate, w_up, w_down, attn_w_in, attn_w_out, sgu_w_in, sgu_ln_g, sgu_ln_b, sgu_w_spatial, sgu_b_spatial, sgu_w_out, final_norm, loss_target, m_mix_norm, m_mem_norm, m_w_mem_kv, m_ffn_norm, m_w_gate, m_w_up, m_w_down, m_attn_w_in, m_attn_w_out, m_sgu_w_in, m_sgu_ln_g, m_sgu_ln_b, m_sgu_w_spatial, m_sgu_b_spatial, m_sgu_w_out, m_final_norm, v_mix_norm, v_mem_norm, v_w_mem_kv, v_ffn_norm, v_w_gate, v_w_up, v_w_down, v_attn_w_in, v_attn_w_out, v_sgu_w_in, v_sgu_ln_g, v_sgu_ln_b, v_sgu_w_spatial, v_sgu_b_spatial, v_sgu_w_out, v_final_norm):
    d_model = x.shape[2]
    x0, mem0, tgt = x[0], mem[0], loss_target[0]
    xi, yi, ci = _place()
    chip = 2 * xi + yi
    place = jnp.stack([ci, chip]).astype(jnp.int32)

    given_w = dict(mix_norm=mix_norm, mem_norm=mem_norm, w_mem_kv=w_mem_kv, ffn_norm=ffn_norm, w_gate=w_gate, w_up=w_up,
                   w_down=w_down, attn_w_in=attn_w_in, attn_w_out=attn_w_out, sgu_w_in=sgu_w_in, sgu_ln_g=sgu_ln_g,
                   sgu_ln_b=sgu_ln_b, sgu_w_spatial=sgu_w_spatial, sgu_b_spatial=sgu_b_spatial, sgu_w_out=sgu_w_out,
                   final_norm=final_norm)
    given_m = dict(mix_norm=m_mix_norm, mem_norm=m_mem_norm, w_mem_kv=m_w_mem_kv, ffn_norm=m_ffn_norm, w_gate=m_w_gate,
                   w_up=m_w_up, w_down=m_w_down, attn_w_in=m_attn_w_in, attn_w_out=m_attn_w_out, sgu_w_in=m_sgu_w_in,
                   sgu_ln_g=m_sgu_ln_g, sgu_ln_b=m_sgu_ln_b, sgu_w_spatial=m_sgu_w_spatial,
                   sgu_b_spatial=m_sgu_b_spatial, sgu_w_out=m_sgu_w_out, final_norm=m_final_norm)
    given_v = dict(mix_norm=v_mix_norm, mem_norm=v_mem_norm, w_mem_kv=v_w_mem_kv, ffn_norm=v_ffn_norm, w_gate=v_w_gate,
                   w_up=v_w_up, w_down=v_w_down, attn_w_in=v_attn_w_in, attn_w_out=v_attn_w_out, sgu_w_in=v_sgu_w_in,
                   sgu_ln_g=v_sgu_ln_g, sgu_ln_b=v_sgu_ln_b, sgu_w_spatial=v_sgu_w_spatial,
                   sgu_b_spatial=v_sgu_b_spatial, sgu_w_out=v_sgu_w_out, final_norm=v_final_norm)

    units = {"attn_w_in": ("attn_w_in", 0, "col"), "w_mem_kv0": ("w_mem_kv", 0, "row"), "attn_w_out": ("attn_w_out", 0, "col"),
             "w_gate0": ("w_gate", 0, "col"), "w_up0": ("w_up", 0, "col"), "w_down0": ("w_down", 0, "row"),
             "sgu_w_in": ("sgu_w_in", 0, "col"), "w_mem_kv1": ("w_mem_kv", 1, "row"), "sgu_w_out": ("sgu_w_out", 0, "row"),
             "w_gate1": ("w_gate", 1, "col"), "w_up1": ("w_up", 1, "col"), "w_down1": ("w_down", 1, "row")}
    gather_groups = [["attn_w_in"], ["w_mem_kv0", "attn_w_out"], ["w_gate0", "w_up0"],
                     ["w_down0", "sgu_w_in", "w_mem_kv1", "ln"], ["sgu_w_out"], ["w_gate1", "w_up1"], ["w_down1"]]

    first = _place_shard("place_attn_w_in", attn_w_in, 0, place, BF)
    first = first.reshape(N_CHIPS, 2, first.shape[1] // 2, first.shape[2])
    first_ssem, first_rsem, first_thru, token = _exchange_start("gather_start_0", [first], _first_copies, 3)
    in_flight = [None]
    slabs = {u: _place_shard(f"place_{u}", given_w[arr], layer, place, BF, after=[token])
             for u, (arr, layer, _) in units.items() if u != "attn_w_in"}
    slabs["ln"] = _place_shard("place_ln", jnp.concatenate([sgu_ln_g, sgu_ln_b])[None], 0, place, F32, after=[token])
    rest, token = _gather_start("gather_start_1", [[slabs[u] for u in grp] for grp in gather_groups[1:]])
    in_flight += rest
    weights = {}

    def arrive(gi, after):
        ssem, rsem, arrs = in_flight[gi]
        for u, full in zip(gather_groups[gi], _gather_wait(f"gather_wait_{gi}", ssem, rsem, arrs, after)):
            weights[u] = full if u == "ln" else Weight(full, units[u][2])

    w_sp = sgu_w_spatial[0]
    b_t = sgu_b_spatial[0].T
    tables = _rope_tables(positions[0])

    def residual(acc, extra):
        return [extra[0] + acc[0]]

    def memory_kv(layer):
        mem_n = _rms_fwd(f"mem_norm_{layer}", mem0, mem_norm[layer:layer + 1])
        return mem_n, _mm_nn(f"mem_kv_{layer}", mem_n, weights[f"w_mem_kv{layer}"])[0]

    h0 = _rms_fwd("mix_norm_0", x0, mix_norm[0:1], after=[token])
    landed = _exchange_wait("gather_wait_0", first_ssem, first_rsem, first_thru, _first_copies, [h0])
    pass_ssem, pass_rsem, passing, _ = _exchange_start("pass_start_0", landed, _pass_copies, 3)
    whole = _exchange_wait("pass_wait_0", pass_ssem, pass_rsem, passing, _pass_copies, [])[0]
    weights["attn_w_in"] = Weight(whole.reshape(N_CHIPS, -1, whole.shape[-1]), "col")
    proj0 = _mm_nn("attn_in", h0, weights["attn_w_in"])[0]
    arrive(1, [proj0])
    qkv = _rope_fwd(proj0, tables)
    qs, ks, vs = qkv[0:3], qkv[3:6], qkv[6:9]
    outs, lses = [], []
    for g in range(len(DILATIONS)):
        o, l = _dil_fwd(g, qs[g], ks[g], vs[g])
        outs.append(o)
        lses.append(l)
    merged, lse = _attn_merge(outs, lses)
    mem_n0, kv0 = memory_kv(0)
    cat0 = _mem_fwd("mem_fwd_0", proj0, 9, kv0, merged, 1)
    x1, hf0 = _mm_nn("attn_out", cat0, weights["attn_w_out"], extras=[x0], epilogue=residual, norm_gain=ffn_norm[0:1])
    arrive(2, [x1])
    g0, u0, act0 = _gate_up("gate_up_0", hf0, weights["w_gate0"], weights["w_up0"])
    arrive(3, [act0])
    x2 = _mm_nn("down_0", act0, weights["w_down0"], extras=[x1], epilogue=residual)[0]

    ln_all = weights["ln"]
    ln_g = ln_all[:, 0, :].reshape(1, SGU_W)
    ln_b = ln_all[:, 1, :].reshape(1, SGU_W)
    h1 = _rms_fwd("mix_norm_1", x2, mix_norm[1:2])
    proj1 = _mm_nn("sgu_in", h1, weights["sgu_w_in"], out_dtypes=(BF,))[0]
    sgu_out = _sgu_fwd(proj1, ln_g, ln_b, w_sp, b_t)
    mem_n1, kv1 = memory_kv(1)
    cat1 = _mem_fwd("mem_fwd_1", proj1, 6, kv1, sgu_out, 3)
    arrive(4, [cat1])
    x3, hf1 = _mm_nn("sgu_out", cat1, weights["sgu_w_out"], extras=[x2], epilogue=residual, norm_gain=ffn_norm[1:2])
    arrive(5, [x3])
    g1, u1, act1 = _gate_up("gate_up_1", hf1, weights["w_gate1"], weights["w_up1"])
    arrive(6, [act1])
    x4 = _mm_nn("down_1", act1, weights["w_down1"], extras=[x3], epilogue=residual)[0]

    d4, d4_op, g_final, loss_part = _final_loss(x4, tgt, final_norm.reshape(1, d_model))
    loss = lax.psum(loss_part[0, 0], ("x", "y", "c"))

    outputs = {}

    def start_reduce(tag, names, grads):
        ssem, rsem, grads, lands, tok = _reduce_start(f"reduce_start_{tag}", grads)
        return dict(tag=tag, names=names, ssem=ssem, rsem=rsem, grads=grads, lands=lands), tok

    def finish_reduce(st, after):
        grads, lands = _reduce_wait(f"reduce_wait_{st['tag']}", st["ssem"], st["rsem"], st["grads"], st["lands"], after)
        totals = [_sum_pieces(f"sum_{u}", g, l, place) for u, g, l in zip(st["names"], grads, lands)]
        ssem, rsem, totals, tok = _half_start(f"half_start_{st['tag']}", totals)
        return dict(tag=st["tag"], names=st["names"], ssem=ssem, rsem=rsem, totals=totals), tok

    def finish_update(st, after):
        totals = _half_wait(f"half_wait_{st['tag']}", st["ssem"], st["rsem"], st["totals"], after)
        for u, tot in zip(st["names"], totals):
            arr, layer, _ = units[u]
            w = given_w[arr]
            outputs[arr] = _adamw(f"adamw_{u}", w, tot.reshape(w.shape[1:]), given_m[arr], given_v[arr], layer,
                                  outputs.get(arr))

    def ffn_bwd(layer, d_out, d_out_op, xin, h, g, u, act):
        wd, wg, wu = weights[f"w_down{layer}"], weights[f"w_gate{layer}"], weights[f"w_up{layer}"]
        gr_down = _mm_tn(f"d_down_{layer}", act, d_out_op, wd)
        dg, du = _mm_nt(f"d_act_{layer}", [d_out_op], [wd], out_dtypes=(BF, BF), extras=[g, u],
                        epilogue=_swiglu_bwd_epilogue, col_chunk=EPILOGUE_CHUNK)
        gr_gate = _mm_tn(f"d_gate_{layer}", h, dg, wg)
        gr_up = _mm_tn(f"d_up_{layer}", h, du, wu)
        st, tok = start_reduce(f"ffn{layer}", [f"w_down{layer}", f"w_gate{layer}", f"w_up{layer}"], [gr_down, gr_gate, gr_up])
        dh = _mm_nt(f"d_ffn_h_{layer}", [dg, du], [wg, wu], out_dtypes=(BF,), after=[tok])[0]
        d_in, d_in_op, g_norm = _rms_bwd(f"ffn_norm_bwd_{layer}", xin, ffn_norm[layer:layer + 1], dh, d_out)
        return st, d_in, d_in_op, g_norm

    def memory_bwd(layer, mem_n, dkv):
        dkv = dkv.astype(BF)
        wkv = weights[f"w_mem_kv{layer}"]
        gr = _mm_tn(f"d_mem_kv_{layer}", mem_n, dkv, wkv)
        d_mem_n = _mm_nt(f"d_mem_n_{layer}", [dkv], [wkv])[0]
        return gr, _rms_bwd(f"mem_norm_bwd_{layer}", mem0, mem_norm[layer:layer + 1], d_mem_n)[2]

    st_ffn1, d3, d3_op, g_ffn1 = ffn_bwd(1, d4, d4_op, x3, hf1, g1, u1, act1)
    gr_sgu_out = _mm_tn("d_sgu_out", cat1, d3_op, weights["sgu_w_out"])
    dcat1 = _mm_nt("d_cat_1", [d3_op], [weights["sgu_w_out"]], out_dtypes=(BF,))[0]
    st_ffn1, tok = finish_reduce(st_ffn1, [dcat1])
    dproj1, dkv1 = _mem_bwd("mem_bwd_1", proj1, 6, kv1, dcat1, 3, proj1.shape[1])
    gr_kv1, g_mem1 = memory_bwd(1, mem_n1, dkv1)
    dproj1, g_wsp, g_bsp_t, g_ln_g, g_ln_b = _sgu_bwd(proj1, dcat1, ln_g, ln_b, w_sp, b_t, dproj1)
    gr_sgu_in = _mm_tn("d_sgu_in", h1, dproj1, weights["sgu_w_in"], after=[tok])
    finish_update(st_ffn1, [gr_sgu_in])
    st_mix1, tok = start_reduce("mix1", ["sgu_w_out", "w_mem_kv1", "sgu_w_in"], [gr_sgu_out, gr_kv1, gr_sgu_in])
    dh1 = _mm_nt("d_h_1", [dproj1], [weights["sgu_w_in"]], out_dtypes=(BF,), after=[tok])[0]
    d2, d2_op, g_mix1 = _rms_bwd("mix_norm_bwd_1", x2, mix_norm[1:2], dh1, d3)

    st_ffn0, d1, d1_op, g_ffn0 = ffn_bwd(0, d2, d2_op, x1, hf0, g0, u0, act0)
    dev = 4 * xi + 2 * yi + ci
    small_a = [g_mix1, g_mem1, jnp.concatenate([g_ffn0, g_ffn1]), g_wsp, g_bsp_t[:, :SGU_GROUPS].T, g_final, g_ln_g, g_ln_b]
    sa_ssem, sa_rsem, sa_slots, tok = _small_start("small_start_a", _own_slot(_pack(small_a, LANES), dev))
    gr_attn_out = _mm_tn("d_attn_out", cat0, d1_op, weights["attn_w_out"], after=[tok])
    st_mix1, tok = finish_reduce(st_mix1, [gr_attn_out])
    dcat0 = _mm_nt("d_cat_0", [d1_op], [weights["attn_w_out"]], after=[tok])[0]
    dproj0, dkv0 = _mem_bwd("mem_bwd_0", proj0, 9, kv0, dcat0, 1, proj0.shape[1])
    finish_update(st_mix1, [dkv0])
    gr_kv0, g_mem0 = memory_bwd(0, mem_n0, dkv0)
    st_ffn0, tok = finish_reduce(st_ffn0, [g_mem0])
    d_merged, delta = _attn_delta(dcat0, cat0, after=[tok])
    dqs, dks, dvs = [], [], []
    for g in range(len(DILATIONS)):
        dq, dk, dv = _dil_bwd(g, qs[g], ks[g], vs[g], d_merged[g], lse[g], delta[g])
        dqs.append(dq)
        dks.append(dk)
        dvs.append(dv)
    dproj0 = _rope_bwd(dqs + dks + dvs, tables, dproj0)
    finish_update(st_ffn0, [dproj0])
    gr_attn_in = _mm_tn("d_attn_in", h0, dproj0, weights["attn_w_in"])
    st_mix0, tok = start_reduce("mix0", ["attn_w_out", "w_mem_kv0", "attn_w_in"], [gr_attn_out, gr_kv0, gr_attn_in])
    dh0 = _mm_nt("d_h_0", [dproj0], [weights["attn_w_in"]], out_dtypes=(BF,), after=[tok])[0]
    d0, _, g_mix0 = _rms_bwd("mix_norm_bwd_0", x0, mix_norm[0:1], dh0, d1)

    small_b = [g_mix0, g_mem0]
    sb_ssem, sb_rsem, sb_slots, tok = _small_start("small_start_b", _own_slot(_pack(small_b, 8), dev))
    sa_slots = _small_wait("small_wait_a", sa_ssem, sa_rsem, sa_slots, [tok])
    g_mix1, g_mem1, g_ffn, g_wsp, g_bsp, g_final, g_ln_g, g_ln_b = _unpack(_sum_devices("small_sum_a", sa_slots),
                                                                           [t.shape for t in small_a])
    sb_slots = _small_wait("small_wait_b", sb_ssem, sb_rsem, sb_slots, [g_final])
    g_mix0, g_mem0 = _unpack(_sum_devices("small_sum_b", sb_slots), [t.shape for t in small_b])
    st_mix0, tok = finish_reduce(st_mix0, [g_mix0])
    g_mix, g_mem = jnp.concatenate([g_mix0, g_mix1]), jnp.concatenate([g_mem0, g_mem1])
    shard_w = sgu_ln_g.shape[-1]
    g_ln_g = lax.dynamic_slice_in_dim(g_ln_g, chip * shard_w, shard_w, axis=1)
    g_ln_b = lax.dynamic_slice_in_dim(g_ln_b, chip * shard_w, shard_w, axis=1)
    small_names = ["mix_norm", "mem_norm", "ffn_norm", "sgu_w_spatial", "sgu_b_spatial", "final_norm", "sgu_ln_g",
                   "sgu_ln_b"]
    small_g = [g_mix, g_mem, g_ffn, g_wsp, g_bsp, g_final, g_ln_g, g_ln_b]
    small_shapes = [given_w[k].shape for k in small_names]
    packed = [_pack(t, LANES) for t in ([given_w[k] for k in small_names], small_g, [given_m[k] for k in small_names],
                                    [given_v[k] for k in small_names])]
    small_out = _adamw("adamw_small", packed[0][None], packed[1], packed[2][None], packed[3][None], 0)
    finish_update(st_mix0, [small_out[0]])
    for k, gk, dk, mk, vk in zip(small_names, *[_unpack(t[0], small_shapes) for t in small_out]):
        outputs[k] = (gk, dk, mk, vk)

    order = ["mix_norm", "mem_norm", "w_mem_kv", "ffn_norm", "w_gate", "w_up", "w_down", "attn_w_in", "attn_w_out",
             "sgu_w_in", "sgu_ln_g", "sgu_ln_b", "sgu_w_spatial", "sgu_b_spatial", "sgu_w_out", "final_norm"]
    return (loss, d0[None], *[outputs[k][0] for k in order], *[outputs[k][1] for k in order],
            *[outputs[k][2] for k in order], *[outputs[k][3] for k in order])
```

```python
import math

import jax
import jax.numpy as jnp
from jax import lax
from jax.experimental import pallas as pl
from jax.experimental.pallas import tpu as pltpu

F32 = jnp.float32
BF = jnp.bfloat16
MESH = pl.DeviceIdType.MESH

HEAD_DIM = 128
MEM_HEADS = 4
MEM_W = MEM_HEADS * HEAD_DIM
GROUP_W = 4 * HEAD_DIM
DILATIONS = (1, 4, 16)
BLK = 128
SGU_GROUPS = 12
SGU_W = SGU_GROUPS * HEAD_DIM
ROT_HALF = 16
ROPE_THETA = 500000.0
NORM_EPS = 1e-6
LN_EPS = 1e-5
NEG_INF = -1e30
SCALE = HEAD_DIM ** -0.5
ADAM_LR, ADAM_B1, ADAM_B2, ADAM_EPS, ADAM_WD, ADAM_STEP = 0.001, 0.9, 0.999, 1e-08, 0.01, 10

VMEM_LIMIT = 48 * 2 ** 20
VMEM_TILE_BUDGET = 38 * 2 ** 20
N_CHIPS = 4
N_DEV = 8
LANES = 128
EPILOGUE_CHUNK = 512

NT_DIMS = (((1,), (1,)), ((), ()))
TN_DIMS = (((0,), (0,)), ((), ()))
NN_DIMS = (((1,), (0,)), ((), ()))

ANY = pl.BlockSpec(memory_space=pl.ANY)
HBM = pl.BlockSpec(memory_space=pltpu.HBM)
SEM = pl.BlockSpec(memory_space=pltpu.SEMAPHORE)
EFFECT = pltpu.SideEffectType.DATAFLOW_SIDE_EFFECTING


def _params(sem):
    return pltpu.CompilerParams(dimension_semantics=sem, vmem_limit_bytes=VMEM_LIMIT)


def _pick(n, cap):
    if n <= cap:
        return n
    best = None
    for t in range(LANES, cap + 1, LANES):
        if n % t == 0:
            best = t
    assert best is not None, (n, cap)
    return best


def _pick_rows(n, cap):
    t = min(n, cap)
    while n % t:
        t //= 2
    return t


def _mm(name, dims, a_list, a_specs, b_list, b_specs, pairs, n_acc, acc_shape, grid, extras, e_specs,
        out_shapes, out_specs, epilogue, after=(), col_chunk=None, store=None, shard_width=None, norm_gain=None):
    na, nb, ne, no = len(a_list), len(b_list), len(extras), len(out_shapes)
    nk = grid[-1]
    ng = 0 if norm_gain is None else 1

    def products(a, b, cols=None):
        sums = [None] * n_acc
        for ai, bi, ci in pairs:
            bv = b[bi]
            if cols is None:
                bv = bv[...]
            elif dims == NT_DIMS:
                bv = bv[cols, :]
            else:
                bv = bv[:, cols]
            if bv.ndim == 3:
                bv = bv.reshape(-1, bv.shape[-1])
            prod = lax.dot_general(a[ai][...].astype(BF), bv.astype(BF), dims, preferred_element_type=F32)
            sums[ci] = prod if sums[ci] is None else sums[ci] + prod
        return sums

    def body(*refs):
        a = refs[:na]
        b = refs[na:na + nb]
        e = refs[na + nb:na + nb + ne]
        off = na + nb + ne + ng + len(after)
        o = refs[off:off + no]
        acc = refs[off + no:]

        def normed():
            if ng:
                xf = o[0][...]
                r = lax.rsqrt(jnp.mean(xf * xf, axis=-1, keepdims=True) + NORM_EPS)
                o[-1][...] = (xf * r * refs[na + nb + ne][...]).astype(o[-1].dtype)

        def finish(sums):
            outs = epilogue(sums, [r[...] for r in e])
            if store is not None:
                store(o, outs)
                return
            for r, v in zip(o, outs):
                r[...] = v.astype(r.dtype)
            normed()

        if nk == 1 and shard_width:
            (ai, bi, _), = pairs
            av = a[ai][...].astype(BF)
            if dims == NT_DIMS:
                total = None
                for j in range(N_CHIPS):
                    cols = slice(j * shard_width, (j + 1) * shard_width)
                    prod = lax.dot_general(av[:, cols], b[bi][j].astype(BF), dims, preferred_element_type=F32)
                    total = prod if total is None else total + prod
                finish([total])
                return
            for j in range(N_CHIPS):
                cols = slice(j * shard_width, (j + 1) * shard_width)
                prod = lax.dot_general(av, b[bi][j].astype(BF), dims, preferred_element_type=F32)
                outs = epilogue([prod], [r[:, cols] for r in e])
                for r, v in zip(o, outs):
                    r[:, cols] = v.astype(r.dtype)
            normed()
            return
        if nk == 1 and col_chunk:
            width = acc_shape[1]
            left = [r[...].astype(BF) for r in a]
            for c0 in range(0, width, col_chunk):
                cols = slice(c0, min(c0 + col_chunk, width))
                outs = epilogue(products(left, b, cols), [r[:, cols] for r in e])
                for r, v in zip(o, outs):
                    r[:, cols] = v.astype(r.dtype)
            return
        if nk == 1:
            finish(products(a, b))
            return
        k = pl.program_id(len(grid) - 1)

        @pl.when(k == 0)
        def _():
            for c, v in zip(acc, products(a, b)):
                c[...] = v

        @pl.when(jnp.logical_and(k > 0, k < nk - 1))
        def _():
            for c, v in zip(acc, products(a, b)):
                c[...] += v

        @pl.when(k == nk - 1)
        def _():
            finish([c[...] + v for c, v in zip(acc, products(a, b))])

    gains = [] if norm_gain is None else [norm_gain]
    ins = [*a_list, *b_list, *extras, *gains, *after]
    in_specs = [*a_specs, *b_specs, *e_specs, *[pl.BlockSpec(g.shape, lambda *_: (0, 0)) for g in gains],
                *([ANY] * len(after))]
    sem = ("parallel",) * (len(grid) - 1) + ("arbitrary",)
    scratch = [] if nk == 1 else [pltpu.VMEM(acc_shape, F32)] * n_acc
    return pl.pallas_call(
        body, out_shape=out_shapes, grid=grid, in_specs=in_specs, out_specs=out_specs, scratch_shapes=scratch,
        name=name, compiler_params=_params(sem))(*ins)


def _tile_bytes(blocks, single=()):
    size = lambda s, d: math.prod(s) * jnp.dtype(d).itemsize
    return sum(2 * size(s, d) for s, d in blocks) + sum(size(s, d) for s, d in single)


def _first(acc, extra):
    return [acc[0]]


def _sigmoid(x):
    return 0.5 * (1.0 + jnp.tanh(0.5 * x))


class Weight:
    def __init__(self, arr, axis):
        self.arr, self.axis = arr, axis
        _, self.rows, self.cols = arr.shape


SMALL_WEIGHT_BYTES = 8 * 2 ** 20


def _is_small(w):
    return w.arr.size * w.arr.dtype.itemsize <= SMALL_WEIGHT_BYTES


def _mm_nn(name, a, w, extras=(), epilogue=_first, out_dtypes=(F32,), after=(), norm_gain=None):
    m, kdim = a.shape
    b_spec, shard_width = None, None
    weight_buffers = 2
    if norm_gain is not None:
        out_dtypes = (*out_dtypes, BF)
    if w.axis == "col" and _is_small(w):
        n_total = tn = N_CHIPS * w.cols
        tk, gn, gk = kdim, 1, 1
        shard_width = w.cols
        b_spec = pl.BlockSpec((N_CHIPS, kdim, w.cols), lambda n, i, k: (0, 0, 0))
    elif w.axis == "col":
        n_total = N_CHIPS * w.cols
        tn = _pick(w.cols, 1408)
        tk = _pick(kdim, 2048)
        ncb = w.cols // tn
        gn, gk = N_CHIPS * ncb, kdim // tk
        b_map = lambda n, i, k: (n // ncb, k, n % ncb)
    elif kdim <= 2048 and norm_gain is not None:
        n_total = tn = w.cols
        tk, gn, gk = kdim, 1, 1
        weight_buffers = 1
        b_spec = pl.BlockSpec(w.arr.shape, lambda n, i, k: (0, 0, 0), pipeline_mode=pl.Buffered(1))
    elif kdim <= 2048:
        n_total = w.cols
        tn = _pick(w.cols, 1024)
        tk = kdim
        gn, gk = n_total // tn, 1
        b_spec = pl.BlockSpec((N_CHIPS, w.rows, tn), lambda n, i, k: (0, 0, n))
    else:
        n_total = w.cols
        tn = _pick(w.cols, 1024)
        tk = _pick(w.rows, 1408)
        nkb = w.rows // tk
        gn, gk = n_total // tn, N_CHIPS * nkb
        b_map = lambda n, i, k: (k // nkb, k % nkb, n)
    if b_spec is None:
        b_spec = pl.BlockSpec((None, tk, tn), b_map)
    for tm in (1024, 512, 256, 128):
        if m % tm:
            continue
        blocks = [((tm, tk), a.dtype)] + [((tm, tn), e.dtype) for e in extras]
        blocks += [((tm, tn), d) for d in out_dtypes] + [((tm, tn), BF)]
        weight = [((tk, tn), BF)]
        if _tile_bytes(blocks + (weight if weight_buffers == 2 else []), weight if weight_buffers == 1 else ()) <= VMEM_TILE_BUDGET:
            break
    assert norm_gain is None or tn == n_total, name
    o_spec = pl.BlockSpec((tm, tn), lambda n, i, k: (i, n))
    return _mm(
        name, NN_DIMS, [a], [pl.BlockSpec((tm, tk), lambda n, i, k: (i, k))],
        [w.arr], [b_spec], [(0, 0, 0)], 1, (tm, tn), (gn, m // tm, gk),
        list(extras), [o_spec] * len(extras),
        [jax.ShapeDtypeStruct((m, n_total), d) for d in out_dtypes], [o_spec] * len(out_dtypes), epilogue, after,
        shard_width=shard_width, norm_gain=norm_gain)


def _gate_up(name, h, wg, wu):
    m, kdim = h.shape
    tn = _pick(wg.cols, 1408)
    tk = _pick(kdim, 2048)
    ncb = wg.cols // tn
    single = kdim == tk
    for tm in (1024, 512, 256, 128):
        blocks = [((tm, tk), BF)] + [((tm, tn), BF)] * 3
        weights = [((tk, tn), BF)] * 2
        if m % tm == 0 and _tile_bytes(blocks + ([] if single else weights), weights if single else ()) <= VMEM_TILE_BUDGET:
            break
    b_spec = pl.BlockSpec((None, tk, tn), lambda n, i, k: (n // ncb, k, n % ncb),
                          pipeline_mode=pl.Buffered(1) if single else None)
    o_spec = pl.BlockSpec((tm, tn), lambda n, i, k: (i, n))
    n_total = N_CHIPS * wg.cols

    def epilogue(acc, extra):
        g, u = acc
        return [g, u, g * _sigmoid(g) * u]

    return _mm(
        name, NN_DIMS, [h], [pl.BlockSpec((tm, tk), lambda n, i, k: (i, k))], [wg.arr, wu.arr], [b_spec, b_spec],
        [(0, 0, 0), (0, 1, 1)], 2, (tm, tn), (N_CHIPS * ncb, m // tm, kdim // tk), [], [],
        [jax.ShapeDtypeStruct((m, n_total), BF)] * 3, [o_spec] * 3, epilogue, col_chunk=EPILOGUE_CHUNK)


def _mm_nt(name, dys, ws, out_dtypes=(F32,), extras=(), epilogue=_first, after=(), col_chunk=None):
    m = dys[0].shape[0]
    w0 = ws[0]
    npair = len(dys)
    b_spec, shard_width = None, None
    if w0.axis == "col" and npair == 1 and _is_small(w0):
        k_total = tko = w0.rows
        tkc = N_CHIPS * w0.cols
        go, gk = 1, 1
        shard_width = w0.cols
        b_spec = pl.BlockSpec(w0.arr.shape, lambda o, i, k: (0, 0, 0))
    elif w0.axis == "col":
        k_total = w0.rows
        tko = _pick(k_total, 1024)
        tkc = _pick(w0.cols, 1408)
        nkb = w0.cols // tkc
        go, gk = k_total // tko, N_CHIPS * nkb
        b_map = lambda o, i, k: (k // nkb, o, k % nkb)
    else:
        k_total = N_CHIPS * w0.rows
        tko = _pick(w0.rows, 1408)
        tkc = _pick(w0.cols, 2048)
        nob = w0.rows // tko
        go, gk = N_CHIPS * nob, w0.cols // tkc
        b_map = lambda o, i, k: (o // nob, o % nob, k)
    single = gk == 1
    for tm in (1024, 512, 256, 128):
        if m % tm:
            continue
        blocks = [((tm, tkc), d.dtype) for d in dys]
        blocks += [((tm, tko), e.dtype) for e in extras] + [((tm, tko), d) for d in out_dtypes]
        blocks += [((tm, tko), BF)]
        weights = [((tko, tkc), BF)] * npair
        if _tile_bytes(blocks + ([] if single else weights), weights if single else ()) <= VMEM_TILE_BUDGET:
            break
    if b_spec is None:
        b_spec = pl.BlockSpec((None, tko, tkc), b_map, pipeline_mode=pl.Buffered(1) if single else None)
    o_spec = pl.BlockSpec((tm, tko), lambda o, i, k: (i, o))
    return _mm(
        name, NT_DIMS, list(dys), [pl.BlockSpec((tm, tkc), lambda o, i, k: (i, k))] * npair,
        [w.arr for w in ws], [b_spec] * npair,
        [(i, i, 0) for i in range(npair)], 1, (tm, tko), (go, m // tm, gk), list(extras), [o_spec] * len(extras),
        [jax.ShapeDtypeStruct((m, k_total), d) for d in out_dtypes], [o_spec] * len(out_dtypes), epilogue, after,
        col_chunk if gk == 1 and shard_width is None else None, shard_width=shard_width)


def _mm_tn(name, a, dy, w, after=()):
    m, k_total = a.shape
    rows2 = w.rows // 2
    tn = _pick(w.cols, 1408)
    ncb = w.cols // tn
    epilogue, store = _first, None
    if k_total <= 2048 and w.axis == "col" and _is_small(w):
        tkr, tn = k_total, N_CHIPS * w.cols
        gr, gn = 1, 1
        o_spec = pl.BlockSpec((2, N_CHIPS, rows2, w.cols), lambda r, n, t: (0, 0, 0, 0))

        def store(o_refs, outs):
            for j in range(N_CHIPS):
                for h in range(2):
                    o_refs[0][h, j] = outs[0][h * rows2:(h + 1) * rows2, j * w.cols:(j + 1) * w.cols].astype(BF)
    elif k_total <= 2048 and w.axis == "col":
        tkr = k_total
        gr, gn = 1, N_CHIPS * ncb
        o_spec = pl.BlockSpec((2, None, rows2, tn), lambda r, n, t: (0, n // ncb, 0, n % ncb))
        epilogue = lambda acc, extra: [acc[0].reshape(2, rows2, tn)]
    elif k_total <= 2048:
        tkr = k_total
        gr, gn = 1, ncb
        o_spec = pl.BlockSpec((2, N_CHIPS, rows2, tn), lambda r, n, t: (0, 0, 0, n))

        def store(o_refs, outs):
            for j in range(N_CHIPS):
                for h in range(2):
                    lo = (2 * j + h) * rows2
                    o_refs[0][h, j] = outs[0][lo:lo + rows2].astype(BF)
    elif rows2 % LANES:
        tkr = w.rows
        assert w.axis == "row"
        gr, gn = N_CHIPS, ncb
        o_spec = pl.BlockSpec((2, None, rows2, tn), lambda r, n, t: (0, r, 0, n))
        epilogue = lambda acc, extra: [acc[0].reshape(2, rows2, tn)]
    else:
        tkr = _pick(rows2, 1408)
        nrb = rows2 // tkr
        if w.axis == "col":
            gr, gn = w.rows // tkr, N_CHIPS * ncb
            o_map = lambda r, n, t: (r // nrb, n // ncb, r % nrb, n % ncb)
        else:
            per = w.rows // tkr
            gr, gn = N_CHIPS * per, ncb
            o_map = lambda r, n, t: ((r % per) // nrb, r // per, (r % per) % nrb, n)
        o_spec = pl.BlockSpec((None, None, tkr, tn), o_map)
    for tmk in (1024, 512, 256, 128):
        blocks = [((tmk, tkr), a.dtype), ((tmk, tn), dy.dtype), ((tkr, tn), BF), ((tkr, tn), BF)]
        if m % tmk == 0 and _tile_bytes(blocks) <= VMEM_TILE_BUDGET:
            break
    return _mm(
        name, TN_DIMS, [a], [pl.BlockSpec((tmk, tkr), lambda r, n, t: (t, r))],
        [dy], [pl.BlockSpec((tmk, tn), lambda r, n, t: (t, n))], [(0, 0, 0)], 1, (tkr, tn), (gr, gn, m // tmk), [], [],
        [jax.ShapeDtypeStruct((2, N_CHIPS, rows2, w.cols), BF)], [o_spec], epilogue, after, store=store)[0]


def _rms_fwd(name, x, g, after=()):
    s, d = x.shape
    tr = _pick_rows(s, 512)

    def body(x_ref, g_ref, *rest):
        h_ref = rest[-1]
        xf = x_ref[...]
        r = lax.rsqrt(jnp.mean(xf * xf, axis=-1, keepdims=True) + NORM_EPS)
        h_ref[...] = (xf * r * g_ref[...]).astype(BF)

    return pl.pallas_call(
        body, out_shape=jax.ShapeDtypeStruct((s, d), BF), grid=(s // tr,),
        in_specs=[pl.BlockSpec((tr, d), lambda i: (i, 0)), pl.BlockSpec((1, d), lambda i: (0, 0))] + [ANY] * len(after),
        out_specs=pl.BlockSpec((tr, d), lambda i: (i, 0)), name=name, compiler_params=_params(("parallel",)))(x, g, *after)


def _rms_bwd(name, x, g, dh, dres=None):
    s, d = x.shape
    tr = _pick_rows(s, 256)
    has_res = dres is not None

    def body(*refs):
        if has_res:
            x_ref, g_ref, dh_ref, dres_ref, dx_ref, dxb_ref, dg_ref = refs
        else:
            x_ref, g_ref, dh_ref, dx_ref, dxb_ref, dg_ref = refs
        xf = x_ref[...]
        r = lax.rsqrt(jnp.mean(xf * xf, axis=-1, keepdims=True) + NORM_EPS)
        xr = xf * r
        dy = dh_ref[...].astype(F32)
        a = dy * g_ref[...]
        dx = r * (a - xr * jnp.mean(a * xr, axis=-1, keepdims=True))
        if has_res:
            dx = dx + dres_ref[...]
        dx_ref[...] = dx
        dxb_ref[...] = dx.astype(BF)

        @pl.when(pl.program_id(0) == 0)
        def _():
            dg_ref[...] = jnp.zeros_like(dg_ref)

        dg_ref[...] += jnp.sum(dy * xr, axis=0, keepdims=True)

    row = pl.BlockSpec((tr, d), lambda i: (i, 0))
    vec = pl.BlockSpec((1, d), lambda i: (0, 0))
    ins = [x, g, dh] + ([dres] if has_res else [])
    in_specs = [row, vec, row] + ([row] if has_res else [])
    return pl.pallas_call(
        body, out_shape=[jax.ShapeDtypeStruct((s, d), F32), jax.ShapeDtypeStruct((s, d), BF),
                         jax.ShapeDtypeStruct((1, d), F32)],
        grid=(s // tr,), in_specs=in_specs, out_specs=[row, row, vec], name=name,
        compiler_params=_params(("arbitrary",)))(*ins)


def _final_loss(x, tgt, g):
    s, d = x.shape
    tr = _pick_rows(s, 256)

    def body(x_ref, t_ref, g_ref, dx_ref, dxb_ref, dg_ref, loss_ref):
        xf = x_ref[...]
        gain = g_ref[...]
        r = lax.rsqrt(jnp.mean(xf * xf, axis=-1, keepdims=True) + NORM_EPS)
        xr = xf * r
        err = xr * gain - t_ref[...]
        dy = err * (1.0 / d)
        a = dy * gain
        dx = r * (a - xr * jnp.mean(a * xr, axis=-1, keepdims=True))
        dx_ref[...] = dx
        dxb_ref[...] = dx.astype(BF)

        @pl.when(pl.program_id(0) == 0)
        def _():
            dg_ref[...] = jnp.zeros_like(dg_ref)
            loss_ref[...] = jnp.zeros_like(loss_ref)

        dg_ref[...] += jnp.sum(dy * xr, axis=0, keepdims=True)
        part = 0.5 * jnp.sum(jnp.mean(err * err, axis=-1, keepdims=True), axis=0, keepdims=True)
        loss_ref[...] += jnp.broadcast_to(part, loss_ref.shape)

    row = pl.BlockSpec((tr, d), lambda i: (i, 0))
    vec = pl.BlockSpec((1, d), lambda i: (0, 0))
    return pl.pallas_call(
        body, out_shape=[jax.ShapeDtypeStruct((s, d), F32), jax.ShapeDtypeStruct((s, d), BF),
                         jax.ShapeDtypeStruct((1, d), F32), jax.ShapeDtypeStruct((8, LANES), F32)],
        grid=(s // tr,), in_specs=[row, row, vec],
        out_specs=[row, row, vec, pl.BlockSpec((8, LANES), lambda i: (0, 0))],
        name="final_loss", compiler_params=_params(("arbitrary",)))(x, tgt, g)


def _swiglu_bwd_epilogue(acc, extra):
    dact = acc[0]
    g, u = extra[0].astype(F32), extra[1].astype(F32)
    sig = _sigmoid(g)
    return [dact * u * sig * (1.0 + g * (1.0 - sig)), dact * g * sig]


GELU_C = math.sqrt(2.0 / math.pi)
GELU_A = 0.044715


def _gelu(x):
    return 0.5 * x * (1.0 + jnp.tanh(GELU_C * (x + GELU_A * x * x * x)))


def _gelu_both(x):
    x2 = x * x
    t = jnp.tanh(GELU_C * (x + GELU_A * x2 * x))
    half = 0.5 * (1.0 + t)
    return x * half, half + 0.5 * x * (1.0 - t * t) * GELU_C * (1.0 + 3.0 * GELU_A * x2)


def _rope_tables(positions):
    inv_freq = ROPE_THETA ** (-jnp.arange(ROT_HALF, dtype=F32) / ROT_HALF)
    ang = positions.astype(F32)[:, None] * inv_freq
    cos, sin = jnp.cos(ang), jnp.sin(ang)
    s = ang.shape[0]
    rest = HEAD_DIM - 2 * ROT_HALF
    zeros = jnp.zeros((s, ROT_HALF), F32)
    cos_t = jnp.concatenate([cos, cos, jnp.ones((s, rest), F32)], axis=1)
    sin_a = jnp.concatenate([-sin, zeros, jnp.zeros((s, rest), F32)], axis=1)
    sin_b = jnp.concatenate([zeros, sin, jnp.zeros((s, rest), F32)], axis=1)
    return cos_t, sin_a, sin_b


def _rope_head(xh, cos_t, sin_a, sin_b):
    up = pltpu.roll(xh, HEAD_DIM - ROT_HALF, 1)
    down = pltpu.roll(xh, ROT_HALF, 1)
    return xh * cos_t + up * sin_a + down * sin_b


def _residue(r, rows, dil):
    return slice(None) if dil == 1 else pl.ds(r, rows, stride=dil)


ROPE_TILE = 256
N_PARTS = 9
HEADS_PER_GROUP = GROUP_W // HEAD_DIM
N_HEADS_IN = N_PARTS * HEADS_PER_GROUP


def _rope_fwd(proj, tables):
    s = proj.shape[0]
    tm = _pick_rows(s, ROPE_TILE)

    def body(*refs):
        heads = refs[:N_HEADS_IN]
        c_ref, sa_ref, sb_ref = refs[N_HEADS_IN:N_HEADS_IN + 3]
        outs = refs[N_HEADS_IN + 3:]
        for g, dil in enumerate(DILATIONS):
            rows = tm // dil
            for r in range(dil):
                rs = _residue(r, rows, dil)
                cos_t, sin_a, sin_b = c_ref[rs, :], sa_ref[rs, :], sb_ref[rs, :]
                for kind in range(3):
                    part = 3 * kind + g
                    for h in range(HEADS_PER_GROUP):
                        xh = heads[part * HEADS_PER_GROUP + h][rs, :]
                        if kind < 2:
                            xh = _rope_head(xh, cos_t, sin_a, sin_b)
                        outs[part][r, :, h * HEAD_DIM:(h + 1) * HEAD_DIM] = xh.astype(BF)

    tab = pl.BlockSpec((tm, HEAD_DIM), lambda i: (i, 0))
    head_specs = [pl.BlockSpec((tm, HEAD_DIM), lambda i, j=j: (i, j)) for j in range(N_HEADS_IN)]
    shapes, specs = [], []
    for part in range(N_PARTS):
        dil = DILATIONS[part % 3]
        shapes.append(jax.ShapeDtypeStruct((dil, s // dil, GROUP_W), BF))
        specs.append(pl.BlockSpec((dil, tm // dil, GROUP_W), lambda i: (0, i, 0)))
    return pl.pallas_call(
        body, out_shape=shapes, grid=(s // tm,), in_specs=head_specs + [tab, tab, tab], out_specs=specs,
        name="rope_fwd", compiler_params=_params(("parallel",)))(*([proj] * N_HEADS_IN), *tables)


def _rope_bwd(parts, tables, into):
    s = into.shape[0]
    tm = _pick_rows(s, ROPE_TILE)

    def body(*refs):
        ins = refs[:N_PARTS]
        c_ref, sa_ref, sb_ref, into_ref, o_ref, scr = refs[N_PARTS:]
        for g, dil in enumerate(DILATIONS):
            rows = tm // dil
            for r in range(dil):
                rs = _residue(r, rows, dil)
                cos_t, sin_a, sin_b = c_ref[rs, :], -sa_ref[rs, :], -sb_ref[rs, :]
                for kind in range(3):
                    part = 3 * kind + g
                    for h in range(HEADS_PER_GROUP):
                        xh = ins[part][r, :, h * HEAD_DIM:(h + 1) * HEAD_DIM]
                        if kind < 2:
                            xh = _rope_head(xh, cos_t, sin_a, sin_b)
                        scr[part * HEADS_PER_GROUP + h, rs, :] = xh
        for j in range(N_HEADS_IN):
            o_ref[:, j * HEAD_DIM:(j + 1) * HEAD_DIM] = scr[j].astype(BF)

    tab = pl.BlockSpec((tm, HEAD_DIM), lambda i: (i, 0))
    i_specs = [pl.BlockSpec((DILATIONS[p % 3], tm // DILATIONS[p % 3], GROUP_W), lambda i: (0, i, 0))
               for p in range(N_PARTS)]
    return pl.pallas_call(
        body, out_shape=jax.ShapeDtypeStruct(into.shape, into.dtype), grid=(s // tm,),
        in_specs=i_specs + [tab] * 3 + [ANY], out_specs=pl.BlockSpec((tm, N_PARTS * GROUP_W), lambda i: (i, 0)),
        scratch_shapes=[pltpu.VMEM((N_HEADS_IN, tm, HEAD_DIM), F32)], input_output_aliases={N_PARTS + 3: 0},
        name="rope_bwd", compiler_params=_params(("parallel",)))(*parts, *tables, into)


def _band_mask(n):
    qi = lax.broadcasted_iota(jnp.int32, (BLK, 2 * BLK), 0)
    ki = lax.broadcasted_iota(jnp.int32, (BLK, 2 * BLK), 1)
    prev = jnp.logical_and(jnp.logical_and(ki < BLK, ki >= qi), n > 0)
    return jnp.logical_or(prev, jnp.logical_and(ki >= BLK, qi >= ki - BLK))


Q_BLOCKS = 4
Q_ROWS = Q_BLOCKS * BLK


STAT_LANES = HEAD_DIM // HEADS_PER_GROUP


def _stat_of(ref, rows, h):
    return ref[rows, h * STAT_LANES:h * STAT_LANES + 1]


def _pack_stats(cols):
    rows = cols[0].shape[0]
    lane = lax.broadcasted_iota(jnp.int32, (rows, HEAD_DIM), 1)
    tile = jnp.broadcast_to(cols[-1], (rows, HEAD_DIM))
    for h in range(HEADS_PER_GROUP - 2, -1, -1):
        tile = jnp.where(lane < (h + 1) * STAT_LANES, cols[h], tile)
    return tile


def _dil_specs(n_steps):
    last = n_steps - 1
    own = pl.BlockSpec((None, Q_ROWS, GROUP_W), lambda r, n: (r, jnp.minimum(n, last), 0))
    before = pl.BlockSpec((None, BLK, GROUP_W), lambda r, n: (r, jnp.maximum(Q_BLOCKS * n - 1, 0), 0))
    stat = pl.BlockSpec((None, Q_ROWS, HEAD_DIM), lambda r, n: (r, jnp.minimum(n, last), 0))
    return own, before, stat


def _dil_fwd(g, q, k, v):
    dil, length, _ = q.shape
    n_steps = length // Q_ROWS

    def body(q_ref, ko_ref, kb_ref, vo_ref, vb_ref, o_ref, lse_ref):
        n = pl.program_id(1)
        lse_cols = [[] for _ in range(Q_BLOCKS)]
        for h in range(GROUP_W // HEAD_DIM):
            sl = slice(h * HEAD_DIM, (h + 1) * HEAD_DIM)
            keys = jnp.concatenate([kb_ref[:, sl], ko_ref[:, sl]], axis=0)
            vals = jnp.concatenate([vb_ref[:, sl], vo_ref[:, sl]], axis=0)
            for j in range(Q_BLOCKS):
                rows, win = slice(j * BLK, (j + 1) * BLK), slice(j * BLK, (j + 2) * BLK)
                sc = lax.dot_general(q_ref[rows, sl], keys[win], NT_DIMS, preferred_element_type=F32) * SCALE
                sc = jnp.where(_band_mask(Q_BLOCKS * n + j), sc, NEG_INF)
                mx = jnp.max(sc, axis=-1, keepdims=True)
                p = jnp.exp(sc - mx)
                den = jnp.sum(p, axis=-1, keepdims=True)
                o_ref[rows, sl] = jnp.dot(p.astype(BF), vals[win], preferred_element_type=F32) / den
                lse_cols[j].append(mx + jnp.log(den))
        for j in range(Q_BLOCKS):
            lse_ref[j * BLK:(j + 1) * BLK, :] = _pack_stats(lse_cols[j])

    own, before, stat = _dil_specs(n_steps)
    return pl.pallas_call(
        body, out_shape=[jax.ShapeDtypeStruct(q.shape, F32), jax.ShapeDtypeStruct((dil, length, HEAD_DIM), F32)],
        grid=(dil, n_steps), in_specs=[own, own, before, own, before], out_specs=[own, stat], name=f"dil_fwd_{g}",
        compiler_params=_params(("parallel", "arbitrary")))(q, k, k, v, v)


def _dil_bwd(g, q, k, v, do, lse, delta):
    dil, length, _ = q.shape
    n_steps = length // Q_ROWS

    def body(q_ref, ko_ref, kb_ref, vo_ref, vb_ref, do_ref, lse_ref, dl_ref, dq_ref, dk_ref, dv_ref, ck_ref, cv_ref):
        n = pl.program_id(1)
        live = n < n_steps

        @pl.when(n == 0)
        def _():
            ck_ref[...] = jnp.zeros_like(ck_ref)
            cv_ref[...] = jnp.zeros_like(cv_ref)

        @pl.when(jnp.logical_not(live))
        def _():
            dk_ref[...] = ck_ref[...]
            dv_ref[...] = cv_ref[...]

        @pl.when(live)
        def _():
            for h in range(GROUP_W // HEAD_DIM):
                sl = slice(h * HEAD_DIM, (h + 1) * HEAD_DIM)
                keys = jnp.concatenate([kb_ref[:, sl], ko_ref[:, sl]], axis=0)
                vals = jnp.concatenate([vb_ref[:, sl], vo_ref[:, sl]], axis=0)
                dks, dvs = [], []
                for j in range(Q_BLOCKS):
                    rows, win = slice(j * BLK, (j + 1) * BLK), slice(j * BLK, (j + 2) * BLK)
                    qh, doh = q_ref[rows, sl], do_ref[rows, sl]
                    lse_h, dl_h = _stat_of(lse_ref, rows, h), _stat_of(dl_ref, rows, h)
                    sc = lax.dot_general(qh, keys[win], NT_DIMS, preferred_element_type=F32) * SCALE
                    p = jnp.where(_band_mask(Q_BLOCKS * n + j), jnp.exp(jnp.minimum(sc - lse_h, 0.0)), 0.0)
                    dp = lax.dot_general(doh, vals[win], NT_DIMS, preferred_element_type=F32)
                    ds = (p * (dp - dl_h) * SCALE).astype(BF)
                    dq_ref[rows, sl] = jnp.dot(ds, keys[win], preferred_element_type=F32)
                    dks.append(lax.dot_general(ds, qh, TN_DIMS, preferred_element_type=F32))
                    dvs.append(lax.dot_general(p.astype(BF), doh, TN_DIMS, preferred_element_type=F32))
                for out_ref, carry, parts in ((dk_ref, ck_ref, dks), (dv_ref, cv_ref, dvs)):
                    out_ref[:Q_ROWS - BLK, sl] = carry[:Q_ROWS - BLK, sl]
                    out_ref[Q_ROWS - BLK:, sl] = carry[Q_ROWS - BLK:, sl] + parts[0][:BLK]
                    for j in range(Q_BLOCKS - 1):
                        carry[j * BLK:(j + 1) * BLK, sl] = parts[j][BLK:] + parts[j + 1][:BLK]
                    carry[Q_ROWS - BLK:, sl] = parts[-1][BLK:]

    own, before, stat = _dil_specs(n_steps)
    behind = pl.BlockSpec((None, Q_ROWS, GROUP_W), lambda r, n: (r, jnp.maximum(n - 1, 0), 0))
    return pl.pallas_call(
        body, out_shape=[jax.ShapeDtypeStruct(q.shape, F32)] * 3, grid=(dil, n_steps + 1),
        in_specs=[own, own, before, own, before, own, stat, stat], out_specs=[own, behind, behind],
        scratch_shapes=[pltpu.VMEM((Q_ROWS, GROUP_W), F32)] * 2, name=f"dil_bwd_{g}",
        compiler_params=_params(("parallel", "arbitrary")))(q, k, k, v, v, do, lse, delta)


def _major_specs(s, tm, dtype, width=GROUP_W):
    shapes = [jax.ShapeDtypeStruct((dil, s // dil, width), dtype) for dil in DILATIONS]
    specs = [pl.BlockSpec((dil, tm // dil, width), lambda i: (0, i, 0)) for dil in DILATIONS]
    return shapes, specs


def _attn_merge(outs, lses):
    s = outs[0].shape[1]
    tm = _pick_rows(s, ROPE_TILE)

    def body(o0, o1, o2, l0, l1, l2, m_ref, e0, e1, e2, so1, so2, sl1, sl2, se):
        for dil, src, dst in ((DILATIONS[1], l1, sl1), (DILATIONS[2], l2, sl2)):
            for r in range(dil):
                dst[_residue(r, tm // dil, dil), :] = src[r]
        a, b, c = l0[0], sl1[...], sl2[...]
        mx = jnp.maximum(jnp.maximum(a, b), c)
        ea, eb, ec = jnp.exp(a - mx), jnp.exp(b - mx), jnp.exp(c - mx)
        den = ea + eb + ec
        wa, wb, wc = ea / den, eb / den, ec / den
        se[...] = mx + jnp.log(den)
        for dil, dst in zip(DILATIONS, (e0, e1, e2)):
            for r in range(dil):
                dst[r] = se[_residue(r, tm // dil, dil), :]
        for h in range(HEADS_PER_GROUP):
            sl = slice(h * HEAD_DIM, (h + 1) * HEAD_DIM)
            col = slice(h * STAT_LANES, h * STAT_LANES + 1)
            for dil, src, dst in ((DILATIONS[1], o1, so1), (DILATIONS[2], o2, so2)):
                for r in range(dil):
                    dst[h, _residue(r, tm // dil, dil), :] = src[r, :, sl]
            m_ref[:, sl] = (wa[:, col] * o0[0, :, sl] + wb[:, col] * so1[h] + wc[:, col] * so2[h]).astype(BF)

    shapes, specs = _major_specs(s, tm, F32)
    stat_shapes, stat_specs = _major_specs(s, tm, F32, HEAD_DIM)
    nat = pl.BlockSpec((tm, GROUP_W), lambda i: (i, 0))
    res = pl.pallas_call(
        body, out_shape=[jax.ShapeDtypeStruct((s, GROUP_W + MEM_W), BF)] + stat_shapes, grid=(s // tm,),
        in_specs=specs + stat_specs, out_specs=[nat] + stat_specs,
        scratch_shapes=[pltpu.VMEM((HEADS_PER_GROUP, tm, HEAD_DIM), F32)] * 2 + [pltpu.VMEM((tm, HEAD_DIM), F32)] * 3,
        name="attn_merge", compiler_params=_params(("parallel",)))(*outs, *lses)
    return res[0], res[1:]


def _attn_delta(dcat, merged, after=()):
    s = merged.shape[0]
    tm = _pick_rows(s, ROPE_TILE)

    def body(*refs):
        d_refs, m_ref = refs[:HEADS_PER_GROUP], refs[HEADS_PER_GROUP]
        do_refs, dl_refs, scr = refs[-7:-4], refs[-4:-1], refs[-1]
        sums = []
        for h in range(HEADS_PER_GROUP):
            sl = slice(h * HEAD_DIM, (h + 1) * HEAD_DIM)
            sums.append(jnp.sum(d_refs[h][...] * m_ref[:, sl].astype(F32), axis=-1, keepdims=True))
            for dil, do_ref in zip(DILATIONS, do_refs):
                for r in range(dil):
                    do_ref[r, :, sl] = d_refs[h][_residue(r, tm // dil, dil), :].astype(BF)
        scr[...] = _pack_stats(sums)
        for dil, dl_ref in zip(DILATIONS, dl_refs):
            for r in range(dil):
                dl_ref[r] = scr[_residue(r, tm // dil, dil), :]

    nat = pl.BlockSpec((tm, GROUP_W), lambda i: (i, 0))
    head_specs = [pl.BlockSpec((tm, HEAD_DIM), lambda i, h=h: (i, h)) for h in range(HEADS_PER_GROUP)]
    bf_shapes, specs = _major_specs(s, tm, BF)
    stat_shapes, stat_specs = _major_specs(s, tm, F32, HEAD_DIM)
    res = pl.pallas_call(
        body, out_shape=bf_shapes + stat_shapes, grid=(s // tm,), in_specs=head_specs + [nat] + [ANY] * len(after),
        out_specs=specs + stat_specs, scratch_shapes=[pltpu.VMEM((tm, HEAD_DIM), F32)], name="attn_delta",
        compiler_params=_params(("parallel",)))(*([dcat] * HEADS_PER_GROUP), merged, *after)
    return res[:3], res[3:]


def _mem_probs(qh, kh):
    sc = lax.dot_general(qh, kh, NT_DIMS, preferred_element_type=F32) * SCALE
    p = jnp.exp(sc - jnp.max(sc, axis=-1, keepdims=True))
    return p, jnp.sum(p, axis=-1, keepdims=True)


def _mem_fwd(name, proj, q_block, kv, into, out_block):
    s = proj.shape[0]
    tq = _pick_rows(s, 512)

    def body(q_ref, kv_ref, into_ref, o_ref):
        for h in range(MEM_HEADS):
            sl = slice(h * HEAD_DIM, (h + 1) * HEAD_DIM)
            vsl = slice(MEM_W + h * HEAD_DIM, MEM_W + (h + 1) * HEAD_DIM)
            p, den = _mem_probs(q_ref[:, sl].astype(BF), kv_ref[:, sl].astype(BF))
            out = jnp.dot(p.astype(BF), kv_ref[:, vsl].astype(BF), preferred_element_type=F32) / den
            o_ref[:, sl] = out.astype(o_ref.dtype)

    return pl.pallas_call(
        body, out_shape=jax.ShapeDtypeStruct(into.shape, into.dtype), grid=(s // tq,),
        in_specs=[pl.BlockSpec((tq, MEM_W), lambda i: (i, q_block)), pl.BlockSpec(kv.shape, lambda i: (0, 0)), ANY],
        out_specs=pl.BlockSpec((tq, MEM_W), lambda i: (i, out_block)), input_output_aliases={2: 0}, name=name,
        compiler_params=_params(("parallel",)))(proj, kv, into)


def _mem_bwd(name, proj, q_block, kv, dcat, d_block, width):
    s = proj.shape[0]
    tq = _pick_rows(s, 512)

    def body(q_ref, kv_ref, do_ref, dq_ref, dkv_ref):
        @pl.when(pl.program_id(0) == 0)
        def _():
            dkv_ref[...] = jnp.zeros_like(dkv_ref)

        for h in range(MEM_HEADS):
            sl = slice(h * HEAD_DIM, (h + 1) * HEAD_DIM)
            vsl = slice(MEM_W + h * HEAD_DIM, MEM_W + (h + 1) * HEAD_DIM)
            qh, kh, vh = q_ref[:, sl].astype(BF), kv_ref[:, sl].astype(BF), kv_ref[:, vsl].astype(BF)
            doh = do_ref[:, sl].astype(BF)
            p, den = _mem_probs(qh, kh)
            p = p / den
            dp = lax.dot_general(doh, vh, NT_DIMS, preferred_element_type=F32)
            ds = (p * (dp - jnp.sum(p * dp, axis=-1, keepdims=True)) * SCALE).astype(BF)
            dq_ref[:, sl] = jnp.dot(ds, kh, preferred_element_type=F32).astype(BF)
            dkv_ref[:, sl] += lax.dot_general(ds, qh, TN_DIMS, preferred_element_type=F32)
            dkv_ref[:, vsl] += lax.dot_general(p.astype(BF), doh, TN_DIMS, preferred_element_type=F32)

    whole = pl.BlockSpec(kv.shape, lambda i: (0, 0))
    return pl.pallas_call(
        body, out_shape=[jax.ShapeDtypeStruct((s, width), BF), jax.ShapeDtypeStruct(kv.shape, F32)], grid=(s // tq,),
        in_specs=[pl.BlockSpec((tq, MEM_W), lambda i: (i, q_block)), whole,
                  pl.BlockSpec((tq, MEM_W), lambda i: (i, d_block))],
        out_specs=[pl.BlockSpec((tq, MEM_W), lambda i: (i, width // MEM_W - 1)), whole], name=name,
        compiler_params=_params(("arbitrary",)))(proj, kv, dcat)


def _causal():
    t = lax.broadcasted_iota(jnp.int32, (BLK, BLK), 0)
    s = lax.broadcasted_iota(jnp.int32, (BLK, BLK), 1)
    return t >= s


def _sgu_norm(vg, ln_g, ln_b):
    mu = jnp.mean(vg, axis=-1, keepdims=True)
    cen = vg - mu
    rstd = lax.rsqrt(jnp.mean(cen * cen, axis=-1, keepdims=True) + LN_EPS)
    xhat = cen * rstd
    return xhat, rstd, xhat * ln_g + ln_b


def _sgu_fwd(proj, ln_g, ln_b, w_sp, b_t):
    s = proj.shape[0]

    def body(u_ref, v_ref, g_ref, b_ref, w_ref, bt_ref, o_ref):
        _, _, vn = _sgu_norm(_gelu(v_ref[...].astype(F32)), g_ref[...], b_ref[...])
        vn = vn.astype(BF)
        tri = _causal()
        for grp in range(SGU_GROUPS):
            sl = slice(grp * HEAD_DIM, (grp + 1) * HEAD_DIM)
            w = jnp.where(tri, w_ref[grp], 0.0).astype(BF)
            mixed = jnp.dot(w, vn[:, sl], preferred_element_type=F32) + bt_ref[:, grp:grp + 1]
            o_ref[:, sl] = (_gelu(u_ref[:, sl].astype(F32)) * mixed).astype(BF)

    vec = pl.BlockSpec((1, SGU_W), lambda i: (0, 0))
    return pl.pallas_call(
        body, out_shape=jax.ShapeDtypeStruct((s, SGU_W + MEM_W), BF), grid=(s // BLK,),
        in_specs=[pl.BlockSpec((BLK, SGU_W), lambda i: (i, 0)), pl.BlockSpec((BLK, SGU_W), lambda i: (i, 1)), vec, vec,
                  pl.BlockSpec(w_sp.shape, lambda i: (0, 0, 0)), pl.BlockSpec(b_t.shape, lambda i: (0, 0))],
        out_specs=pl.BlockSpec((BLK, SGU_W), lambda i: (i, 0)), name="sgu_fwd",
        compiler_params=_params(("parallel",)))(proj, proj, ln_g, ln_b, w_sp, b_t)


def _sgu_bwd(proj, dcat, ln_g, ln_b, w_sp, b_t, into):
    s = proj.shape[0]

    def body(u_ref, v_ref, d_ref, g_ref, b_ref, w_ref, bt_ref, into_ref, dp_ref, dw_ref, db_ref, dg_ref, dbeta_ref,
             dvn_ref):
        @pl.when(pl.program_id(0) == 0)
        def _():
            dw_ref[...] = jnp.zeros_like(dw_ref)
            db_ref[...] = jnp.zeros_like(db_ref)
            dg_ref[...] = jnp.zeros_like(dg_ref)
            dbeta_ref[...] = jnp.zeros_like(dbeta_ref)

        gain = g_ref[...]
        vg, v_slope = _gelu_both(v_ref[...].astype(F32))
        xhat, rstd, vn = _sgu_norm(vg, gain, b_ref[...])
        vn = vn.astype(BF)
        tri = _causal()
        lane = lax.broadcasted_iota(jnp.int32, (BLK, HEAD_DIM), 1)
        db_acc = jnp.zeros((BLK, HEAD_DIM), F32)
        for grp in range(SGU_GROUPS):
            sl = slice(grp * HEAD_DIM, (grp + 1) * HEAD_DIM)
            w = jnp.where(tri, w_ref[grp], 0.0).astype(BF)
            vn_g = vn[:, sl]
            mixed = jnp.dot(w, vn_g, preferred_element_type=F32) + bt_ref[:, grp:grp + 1]
            u_act, u_slope = _gelu_both(u_ref[:, sl].astype(F32))
            d_out = d_ref[:, sl].astype(F32)
            dp_ref[:, sl] = (d_out * mixed * u_slope).astype(BF)
            dmixed = d_out * u_act
            dm = dmixed.astype(BF)
            dvn_ref[:, sl] = lax.dot_general(w, dm, TN_DIMS, preferred_element_type=F32)
            dw = lax.dot_general(dm, vn_g, NT_DIMS, preferred_element_type=F32)
            dw_ref[grp] += jnp.where(tri, dw, 0.0)
            db_acc += jnp.where(lane == grp, jnp.sum(dmixed, axis=-1, keepdims=True), 0.0)
        db_ref[...] += db_acc
        dvn = dvn_ref[...]
        dg_ref[...] += jnp.sum(dvn * xhat, axis=0, keepdims=True)
        dbeta_ref[...] += jnp.sum(dvn, axis=0, keepdims=True)
        dxh = dvn * gain
        dvg = rstd * (dxh - jnp.mean(dxh, axis=-1, keepdims=True) - xhat * jnp.mean(dxh * xhat, axis=-1, keepdims=True))
        dp_ref[:, SGU_W:] = (dvg * v_slope).astype(BF)

    vec = pl.BlockSpec((1, SGU_W), lambda i: (0, 0))
    row = pl.BlockSpec((BLK, SGU_W), lambda i: (i, 0))
    w_spec = pl.BlockSpec(w_sp.shape, lambda i: (0, 0, 0))
    sq = pl.BlockSpec((BLK, HEAD_DIM), lambda i: (0, 0))
    return pl.pallas_call(
        body,
        out_shape=[jax.ShapeDtypeStruct(into.shape, into.dtype),
                   jax.ShapeDtypeStruct(w_sp.shape, F32), jax.ShapeDtypeStruct((BLK, HEAD_DIM), F32),
                   jax.ShapeDtypeStruct((1, SGU_W), F32), jax.ShapeDtypeStruct((1, SGU_W), F32)],
        grid=(s // BLK,),
        in_specs=[row, pl.BlockSpec((BLK, SGU_W), lambda i: (i, 1)), row, vec, vec, w_spec,
                  pl.BlockSpec(b_t.shape, lambda i: (0, 0)), ANY],
        out_specs=[pl.BlockSpec((BLK, 2 * SGU_W), lambda i: (i, 0)), w_spec, sq, vec, vec],
        scratch_shapes=[pltpu.VMEM((BLK, SGU_W), F32)], input_output_aliases={7: 0}, name="sgu_bwd",
        compiler_params=_params(("arbitrary",)))(proj, proj, dcat, ln_g, ln_b, w_sp, b_t, into)


def _place():
    return lax.axis_index("x"), lax.axis_index("y"), lax.axis_index("c")


def _other_chips(x, y):
    return [(1 - x, y), (x, 1 - y), (1 - x, 1 - y)]


def _peer(x, y, c, mask):
    return (1 - x if mask & 4 else x, 1 - y if mask & 2 else y, 1 - c if mask & 1 else c)


def _in_hbm(a):
    return pltpu.with_memory_space_constraint(a, pltpu.HBM)


def _token_spec():
    return jax.ShapeDtypeStruct((8, LANES), F32), pl.BlockSpec(memory_space=pltpu.VMEM)


def _remote(src, dst, ssem, rsem, to):
    return pltpu.make_async_remote_copy(src_ref=src, dst_ref=dst, send_sem=ssem, recv_sem=rsem, device_id=to,
                                        device_id_type=MESH)


def _place_shard(name, src, layer, place, dtype, after=()):
    _, rows, cols = src.shape
    tr = _pick_rows(rows, 512)

    def body(p_ref, s_ref, *rest):
        rest[-1][...] = s_ref[...].astype(dtype)

    grid_spec = pltpu.PrefetchScalarGridSpec(
        num_scalar_prefetch=1, grid=(rows // tr,),
        in_specs=[pl.BlockSpec((None, tr, cols), lambda i, p: (layer, i, 0))] + [ANY] * len(after),
        out_specs=pl.BlockSpec((None, tr, cols), lambda i, p: (p[1], i, 0)))
    return pl.pallas_call(body, out_shape=jax.ShapeDtypeStruct((N_CHIPS, rows, cols), dtype), grid_spec=grid_spec,
                          name=name, compiler_params=_params(("parallel",)))(place, src, *after)


def _gather_copies(bufs, ssem, rsem):
    x, y, c = _place()
    me = 2 * x + y
    copies = []
    for ai, buf in enumerate(bufs):
        for k, (ox, oy) in enumerate(_other_chips(x, y)):
            copies.append(_remote(buf.at[me], buf.at[me], ssem.at[3 * ai + k], rsem.at[3 * ai + k], (ox, oy, c)))
    return copies


def _reduce_copies(grads, lands, ssem, rsem):
    x, y, c = _place()
    copies = []
    for a, (gr, land) in enumerate(zip(grads, lands)):
        for mask in range(1, N_DEV):
            px, py, pc = _peer(x, y, c, mask)
            copies.append(_remote(gr.at[pc, 2 * px + py], land.at[mask - 1], ssem.at[7 * a + mask - 1],
                                  rsem.at[7 * a + mask - 1], (px, py, pc)))
    return copies


def _half_copies(totals, ssem, rsem):
    x, y, c = _place()
    return [_remote(t.at[c], t.at[c], ssem.at[a], rsem.at[a], (x, y, 1 - c)) for a, t in enumerate(totals)]


def _gather_start(name, groups):
    flat = [s for grp in groups for s in grp]
    n, ng = len(flat), len(groups)

    def body(*refs):
        ins = refs[:n]
        sems = refs[n:n + 2 * ng]
        token = refs[-1]
        idx = 0
        for gi, grp in enumerate(groups):
            for cp in _gather_copies(ins[idx:idx + len(grp)], sems[2 * gi], sems[2 * gi + 1]):
                cp.start()
            idx += len(grp)
        token[...] = jnp.zeros_like(token)

    tok_shape, tok_spec = _token_spec()
    sem_shapes = []
    for grp in groups:
        sem_shapes += [pltpu.SemaphoreType.DMA((3 * len(grp),))] * 2
    res = pl.pallas_call(
        body, name=name,
        out_shape=(*sem_shapes, *[pltpu.HBM(s.shape, s.dtype) for s in flat], tok_shape),
        in_specs=[HBM] * n, out_specs=(*[SEM] * (2 * ng), *[HBM] * n, tok_spec),
        input_output_aliases={i: 2 * ng + i for i in range(n)},
        compiler_params=pltpu.CompilerParams(has_side_effects=EFFECT))(*[_in_hbm(s) for s in flat])
    out, idx = [], 2 * ng
    for gi, grp in enumerate(groups):
        out.append((res[2 * gi], res[2 * gi + 1], list(res[idx:idx + len(grp)])))
        idx += len(grp)
    return out, res[-1]


def _gather_wait(name, ssem, rsem, slabs, after):
    n = len(slabs)

    def body(*refs):
        for cp in _gather_copies(refs[:n], refs[n], refs[n + 1]):
            cp.wait_send()
            cp.wait_recv()

    return pl.pallas_call(
        body, name=name, out_shape=tuple(pltpu.HBM(s.shape, s.dtype) for s in slabs),
        in_specs=[HBM] * n + [SEM, SEM] + [ANY] * len(after), out_specs=tuple([HBM] * n),
        input_output_aliases={i: i for i in range(n)},
        compiler_params=pltpu.CompilerParams(has_side_effects=EFFECT))(*slabs, ssem, rsem, *after)


def _reduce_start(name, grads):
    n = len(grads)
    lands = [lax.empty((N_DEV - 1, *g.shape[2:]), g.dtype) for g in grads]

    def body(*refs):
        token = refs[-1]
        for cp in _reduce_copies(refs[:n], refs[n:2 * n], refs[2 * n], refs[2 * n + 1]):
            cp.start()
        token[...] = jnp.zeros_like(token)

    tok_shape, tok_spec = _token_spec()
    sems = [pltpu.SemaphoreType.DMA((7 * n,))] * 2
    res = pl.pallas_call(
        body, name=name,
        out_shape=(*sems, *[pltpu.HBM(g.shape, g.dtype) for g in grads], *[pltpu.HBM(l.shape, l.dtype) for l in lands],
                   tok_shape),
        in_specs=[HBM] * (2 * n), out_specs=(SEM, SEM, *[HBM] * (2 * n), tok_spec),
        input_output_aliases={i: 2 + i for i in range(2 * n)},
        compiler_params=pltpu.CompilerParams(has_side_effects=EFFECT))(*[_in_hbm(t) for t in (*grads, *lands)])
    return res[0], res[1], list(res[2:2 + n]), list(res[2 + n:2 + 2 * n]), res[-1]


def _reduce_wait(name, ssem, rsem, grads, lands, after):
    n = len(grads)

    def body(*refs):
        for cp in _reduce_copies(refs[:n], refs[n:2 * n], refs[2 * n], refs[2 * n + 1]):
            cp.wait_send()
            cp.wait_recv()

    res = pl.pallas_call(
        body, name=name, out_shape=tuple(pltpu.HBM(t.shape, t.dtype) for t in (*grads, *lands)),
        in_specs=[HBM] * (2 * n) + [SEM, SEM] + [ANY] * len(after), out_specs=tuple([HBM] * (2 * n)),
        input_output_aliases={i: i for i in range(2 * n)},
        compiler_params=pltpu.CompilerParams(has_side_effects=EFFECT))(*grads, *lands, ssem, rsem, *after)
    return list(res[:n]), list(res[n:])


def _sum_pieces(name, grad, land, place):
    _, _, rows, cols = grad.shape
    tr = _pick_rows(rows, 256)

    def body(p_ref, g_ref, l_ref, o_ref):
        tot = g_ref[...].astype(F32)
        for k in range(N_DEV - 1):
            tot = tot + l_ref[k].astype(F32)
        o_ref[...] = tot

    grid_spec = pltpu.PrefetchScalarGridSpec(
        num_scalar_prefetch=1, grid=(rows // tr,),
        in_specs=[pl.BlockSpec((None, None, tr, cols), lambda i, p: (p[0], p[1], i, 0)),
                  pl.BlockSpec((N_DEV - 1, tr, cols), lambda i, p: (0, i, 0))],
        out_specs=pl.BlockSpec((None, tr, cols), lambda i, p: (p[0], i, 0)))
    return pl.pallas_call(body, out_shape=jax.ShapeDtypeStruct((2, rows, cols), F32), grid_spec=grid_spec, name=name,
                          compiler_params=_params(("parallel",)))(place, grad, land)


def _first_copies(bufs, ssem, rsem):
    x, y, c = _place()
    me = 2 * x + y
    copies = []
    for a, buf in enumerate(bufs):
        for k, (ox, oy) in enumerate(_other_chips(x, y)):
            copies.append(_remote(buf.at[me, c], buf.at[me, c], ssem.at[3 * a + k], rsem.at[3 * a + k], (ox, oy, c)))
    return copies


def _pass_copies(bufs, ssem, rsem):
    x, y, c = _place()
    copies = []
    for a, buf in enumerate(bufs):
        for k, (ox, oy) in enumerate(_other_chips(x, y)):
            landed = buf.at[2 * ox + oy, c]
            copies.append(_remote(landed, landed, ssem.at[3 * a + k], rsem.at[3 * a + k], (x, y, 1 - c)))
    return copies


def _exchange_start(name, arrays, copies_fn, n_sems):
    n = len(arrays)

    def body(*refs):
        token = refs[-1]
        for cp in copies_fn(refs[:n], refs[n], refs[n + 1]):
            cp.start()
        token[...] = jnp.zeros_like(token)

    tok_shape, tok_spec = _token_spec()
    res = pl.pallas_call(
        body, name=name,
        out_shape=(pltpu.SemaphoreType.DMA((n_sems,)), pltpu.SemaphoreType.DMA((n_sems,)),
                   *[pltpu.HBM(t.shape, t.dtype) for t in arrays], tok_shape),
        in_specs=[HBM] * n, out_specs=(SEM, SEM, *[HBM] * n, tok_spec),
        input_output_aliases={i: 2 + i for i in range(n)},
        compiler_params=pltpu.CompilerParams(has_side_effects=EFFECT))(*[_in_hbm(t) for t in arrays])
    return res[0], res[1], list(res[2:2 + n]), res[-1]


def _exchange_wait(name, ssem, rsem, arrays, copies_fn, after):
    n = len(arrays)

    def body(*refs):
        for cp in copies_fn(refs[:n], refs[n], refs[n + 1]):
            cp.wait_send()
            cp.wait_recv()

    res = pl.pallas_call(
        body, name=name, out_shape=tuple(pltpu.HBM(t.shape, t.dtype) for t in arrays),
        in_specs=[HBM] * n + [SEM, SEM] + [ANY] * len(after), out_specs=tuple([HBM] * n),
        input_output_aliases={i: i for i in range(n)},
        compiler_params=pltpu.CompilerParams(has_side_effects=EFFECT))(*arrays, ssem, rsem, *after)
    return list(res)


def _half_start(name, totals):
    n = len(totals)

    def body(*refs):
        token = refs[-1]
        for cp in _half_copies(refs[:n], refs[n], refs[n + 1]):
            cp.start()
        token[...] = jnp.zeros_like(token)

    tok_shape, tok_spec = _token_spec()
    res = pl.pallas_call(
        body, name=name,
        out_shape=(pltpu.SemaphoreType.DMA((n,)), pltpu.SemaphoreType.DMA((n,)),
                   *[pltpu.HBM(t.shape, t.dtype) for t in totals], tok_shape),
        in_specs=[HBM] * n, out_specs=(SEM, SEM, *[HBM] * n, tok_spec),
        input_output_aliases={i: 2 + i for i in range(n)},
        compiler_params=pltpu.CompilerParams(has_side_effects=EFFECT))(*[_in_hbm(t) for t in totals])
    return res[0], res[1], list(res[2:2 + n]), res[-1]


def _half_wait(name, ssem, rsem, totals, after):
    n = len(totals)

    def body(*refs):
        for cp in _half_copies(refs[:n], refs[n], refs[n + 1]):
            cp.wait_send()
            cp.wait_recv()

    res = pl.pallas_call(
        body, name=name, out_shape=tuple(pltpu.HBM(t.shape, t.dtype) for t in totals),
        in_specs=[HBM] * n + [SEM, SEM] + [ANY] * len(after), out_specs=tuple([HBM] * n),
        input_output_aliases={i: i for i in range(n)},
        compiler_params=pltpu.CompilerParams(has_side_effects=EFFECT))(*totals, ssem, rsem, *after)
    return list(res)


def _small_copies(bufs, ssem, rsem):
    x, y, c = _place()
    mine = bufs[0].at[4 * x + 2 * y + c]
    return [_remote(mine, mine, ssem.at[mask - 1], rsem.at[mask - 1], _peer(x, y, c, mask)) for mask in range(1, N_DEV)]


def _small_start(name, slots):
    def body(s_ref, ssem, rsem, thru, token):
        for cp in _small_copies([s_ref], ssem, rsem):
            cp.start()
        token[...] = jnp.zeros_like(token)

    tok_shape, tok_spec = _token_spec()
    sems = [pltpu.SemaphoreType.DMA((N_DEV - 1,))] * 2
    return pl.pallas_call(
        body, name=name, out_shape=(*sems, pltpu.HBM(slots.shape, slots.dtype), tok_shape), in_specs=[HBM],
        out_specs=(SEM, SEM, HBM, tok_spec), input_output_aliases={0: 2},
        compiler_params=pltpu.CompilerParams(has_side_effects=EFFECT))(_in_hbm(slots))


def _small_wait(name, ssem, rsem, slots, after):
    def body(*refs):
        for cp in _small_copies([refs[0]], refs[1], refs[2]):
            cp.wait_send()
            cp.wait_recv()

    return pl.pallas_call(
        body, name=name, out_shape=pltpu.HBM(slots.shape, slots.dtype), in_specs=[HBM, SEM, SEM] + [ANY] * len(after),
        out_specs=HBM, input_output_aliases={0: 0},
        compiler_params=pltpu.CompilerParams(has_side_effects=EFFECT))(slots, ssem, rsem, *after)


def _own_slot(small, me):
    return lax.dynamic_update_slice(jnp.zeros((N_DEV, *small.shape), small.dtype), small[None], (me, 0, 0))


def _sum_devices(name, stacked):
    _, rows, lanes = stacked.shape
    tr = _pick_rows(rows, 512)

    def body(s_ref, o_ref):
        tot = s_ref[0]
        for k in range(1, N_DEV):
            tot = tot + s_ref[k]
        o_ref[...] = tot

    return pl.pallas_call(
        body, out_shape=jax.ShapeDtypeStruct((rows, lanes), F32), grid=(rows // tr,),
        in_specs=[pl.BlockSpec((N_DEV, tr, lanes), lambda i: (0, i, 0))], out_specs=pl.BlockSpec((tr, lanes), lambda i: (i, 0)),
        name=name, compiler_params=_params(("parallel",)))(stacked)


def _adamw(name, w, g, m, v, layer, prev=None):
    layers, rows, cols = w.shape
    tr = _pick_rows(rows, 256)
    c1 = 1.0 - ADAM_B1 ** ADAM_STEP
    c2 = 1.0 - ADAM_B2 ** ADAM_STEP

    def body(w_ref, g_ref, m_ref, v_ref, *rest):
        go_ref, d_ref, nm_ref, nv_ref = rest[-4:]
        gv = g_ref[...]
        nm = ADAM_B1 * m_ref[...] + (1.0 - ADAM_B1) * gv
        nv = ADAM_B2 * v_ref[...] + (1.0 - ADAM_B2) * (gv * gv)
        go_ref[...] = gv
        d_ref[...] = -ADAM_LR * ((nm / c1) / (jnp.sqrt(nv / c2) + ADAM_EPS) + ADAM_WD * w_ref[...])
        nm_ref[...] = nm
        nv_ref[...] = nv

    spec = pl.BlockSpec((None, tr, cols), lambda i: (layer, i, 0))
    prev = list(prev) if prev is not None else []
    return pl.pallas_call(
        body, out_shape=[jax.ShapeDtypeStruct((layers, rows, cols), F32)] * 4, grid=(rows // tr,),
        in_specs=[spec, pl.BlockSpec((tr, cols), lambda i: (i, 0)), spec, spec] + [ANY] * len(prev),
        out_specs=[spec] * 4, input_output_aliases={4 + i: i for i in range(len(prev))}, name=name,
        compiler_params=_params(("parallel",)))(w, g, m, v, *prev)


def _pack(vectors, pad_rows):
    flat = jnp.concatenate([t.reshape(-1) for t in vectors])
    rows = -(-flat.shape[0] // LANES)
    rows = -(-rows // pad_rows) * pad_rows
    return jnp.pad(flat, (0, rows * LANES - flat.shape[0])).reshape(rows, LANES)


def _unpack(packed, shapes):
    flat = packed.reshape(-1)
    out, off = [], 0
    for shp in shapes:
        size = math.prod(shp)
        out.append(flat[off:off + size].reshape(shp))
        off += size
    return out


def kernel(x, mem, positions, mix_norm, mem_norm, w_mem_kv, ffn_norm, w_gate, w_up, w_down, attn_w_in, attn_w_out, sgu_w_in, sgu_ln_g, sgu_ln_b, sgu_w_spatial, sgu_b_spatial, sgu_w_out, final_norm, loss_target, m_mix_norm, m_mem_norm, m_w_mem_kv, m_ffn_norm, m_w_gate, m_w_up, m_w_down, m_attn_w_in, m_attn_w_out, m_sgu_w_in, m_sgu_ln_g, m_sgu_ln_b, m_sgu_w_spatial, m_sgu_b_spatial, m_sgu_w_out, m_final_norm, v_mix_norm, v_mem_norm, v_w_mem_kv, v_ffn_norm, v_w_gate, v_w_up, v_w_down, v_attn_w_in, v_attn_w_out, v_sgu_w_in, v_sgu_ln_g, v_sgu_ln_b, v_sgu_w_spatial, v_sgu_b_spatial, v_sgu_w_out, v_final_norm):
    d_model = x.shape[2]
    x0, mem0, tgt = x[0], mem[0], loss_target[0]
    xi, yi, ci = _place()
    chip = 2 * xi + yi
    place = jnp.stack([ci, chip]).astype(jnp.int32)

    given_w = dict(mix_norm=mix_norm, mem_norm=mem_norm, w_mem_kv=w_mem_kv, ffn_norm=ffn_norm, w_gate=w_gate, w_up=w_up,
                   w_down=w_down, attn_w_in=attn_w_in, attn_w_out=attn_w_out, sgu_w_in=sgu_w_in, sgu_ln_g=sgu_ln_g,
                   sgu_ln_b=sgu_ln_b, sgu_w_spatial=sgu_w_spatial, sgu_b_spatial=sgu_b_spatial, sgu_w_out=sgu_w_out,
                   final_norm=final_norm)
    given_m = dict(mix_norm=m_mix_norm, mem_norm=m_mem_norm, w_mem_kv=m_w_mem_kv, ffn_norm=m_ffn_norm, w_gate=m_w_gate,
                   w_up=m_w_up, w_down=m_w_down, attn_w_in=m_attn_w_in, attn_w_out=m_attn_w_out, sgu_w_in=m_sgu_w_in,
                   sgu_ln_g=m_sgu_ln_g, sgu_ln_b=m_sgu_ln_b, sgu_w_spatial=m_sgu_w_spatial,
                   sgu_b_spatial=m_sgu_b_spatial, sgu_w_out=m_sgu_w_out, final_norm=m_final_norm)
    given_v = dict(mix_norm=v_mix_norm, mem_norm=v_mem_norm, w_mem_kv=v_w_mem_kv, ffn_norm=v_ffn_norm, w_gate=v_w_gate,
                   w_up=v_w_up, w_down=v_w_down, attn_w_in=v_attn_w_in, attn_w_out=v_attn_w_out, sgu_w_in=v_sgu_w_in,
                   sgu_ln_g=v_sgu_ln_g, sgu_ln_b=v_sgu_ln_b, sgu_w_spatial=v_sgu_w_spatial,
                   sgu_b_spatial=v_sgu_b_spatial, sgu_w_out=v_sgu_w_out, final_norm=v_final_norm)

    units = {"attn_w_in": ("attn_w_in", 0, "col"), "w_mem_kv0": ("w_mem_kv", 0, "row"), "attn_w_out": ("attn_w_out", 0, "col"),
             "w_gate0": ("w_gate", 0, "col"), "w_up0": ("w_up", 0, "col"), "w_down0": ("w_down", 0, "row"),
             "sgu_w_in": ("sgu_w_in", 0, "col"), "w_mem_kv1": ("w_mem_kv", 1, "row"), "sgu_w_out": ("sgu_w_out", 0, "row"),
             "w_gate1": ("w_gate", 1, "col"), "w_up1": ("w_up", 1, "col"), "w_down1": ("w_down", 1, "row")}
    gather_groups = [["attn_w_in"], ["w_mem_kv0", "attn_w_out"], ["w_gate0", "w_up0"],
                     ["w_down0", "sgu_w_in", "w_mem_kv1", "ln"], ["sgu_w_out"], ["w_gate1", "w_up1"], ["w_down1"]]

    first = _place_shard("place_attn_w_in", attn_w_in, 0, place, BF)
    first = first.reshape(N_CHIPS, 2, first.shape[1] // 2, first.shape[2])
    first_ssem, first_rsem, first_thru, token = _exchange_start("gather_start_0", [first], _first_copies, 3)
    in_flight = [None]
    slabs = {u: _place_shard(f"place_{u}", given_w[arr], layer, place, BF, after=[token])
             for u, (arr, layer, _) in units.items() if u != "attn_w_in"}
    slabs["ln"] = _place_shard("place_ln", jnp.concatenate([sgu_ln_g, sgu_ln_b])[None], 0, place, F32, after=[token])
    rest, token = _gather_start("gather_start_1", [[slabs[u] for u in grp] for grp in gather_groups[1:]])
    in_flight += rest
    weights = {}

    def arrive(gi, after):
        ssem, rsem, arrs = in_flight[gi]
        for u, full in zip(gather_groups[gi], _gather_wait(f"gather_wait_{gi}", ssem, rsem, arrs, after)):
            weights[u] = full if u == "ln" else Weight(full, units[u][2])

    w_sp = sgu_w_spatial[0]
    b_t = sgu_b_spatial[0].T
    tables = _rope_tables(positions[0])

    def residual(acc, extra):
        return [extra[0] + acc[0]]

    def memory_kv(layer):
        mem_n = _rms_fwd(f"mem_norm_{layer}", mem0, mem_norm[layer:layer + 1])
        return mem_n, _mm_nn(f"mem_kv_{layer}", mem_n, weights[f"w_mem_kv{layer}"])[0]

    h0 = _rms_fwd("mix_norm_0", x0, mix_norm[0:1], after=[token])
    landed = _exchange_wait("gather_wait_0", first_ssem, first_rsem, first_thru, _first_copies, [h0])
    pass_ssem, pass_rsem, passing, _ = _exchange_start("pass_start_0", landed, _pass_copies, 3)
    whole = _exchange_wait("pass_wait_0", pass_ssem, pass_rsem, passing, _pass_copies, [])[0]
    weights["attn_w_in"] = Weight(whole.reshape(N_CHIPS, -1, whole.shape[-1]), "col")
    proj0 = _mm_nn("attn_in", h0, weights["attn_w_in"])[0]
    arrive(1, [proj0])
    qkv = _rope_fwd(proj0, tables)
    qs, ks, vs = qkv[0:3], qkv[3:6], qkv[6:9]
    outs, lses = [], []
    for g in range(len(DILATIONS)):
        o, l = _dil_fwd(g, qs[g], ks[g], vs[g])
        outs.append(o)
        lses.append(l)
    merged, lse = _attn_merge(outs, lses)
    mem_n0, kv0 = memory_kv(0)
    cat0 = _mem_fwd("mem_fwd_0", proj0, 9, kv0, merged, 1)
    x1, hf0 = _mm_nn("attn_out", cat0, weights["attn_w_out"], extras=[x0], epilogue=residual, norm_gain=ffn_norm[0:1])
    arrive(2, [x1])
    g0, u0, act0 = _gate_up("gate_up_0", hf0, weights["w_gate0"], weights["w_up0"])
    arrive(3, [act0])
    x2 = _mm_nn("down_0", act0, weights["w_down0"], extras=[x1], epilogue=residual)[0]

    ln_all = weights["ln"]
    ln_g = ln_all[:, 0, :].reshape(1, SGU_W)
    ln_b = ln_all[:, 1, :].reshape(1, SGU_W)
    h1 = _rms_fwd("mix_norm_1", x2, mix_norm[1:2])
    proj1 = _mm_nn("sgu_in", h1, weights["sgu_w_in"], out_dtypes=(BF,))[0]
    sgu_out = _sgu_fwd(proj1, ln_g, ln_b, w_sp, b_t)
    mem_n1, kv1 = memory_kv(1)
    cat1 = _mem_fwd("mem_fwd_1", proj1, 6, kv1, sgu_out, 3)
    arrive(4, [cat1])
    x3, hf1 = _mm_nn("sgu_out", cat1, weights["sgu_w_out"], extras=[x2], epilogue=residual, norm_gain=ffn_norm[1:2])
    arrive(5, [x3])
    g1, u1, act1 = _gate_up("gate_up_1", hf1, weights["w_gate1"], weights["w_up1"])
    arrive(6, [act1])
    x4 = _mm_nn("down_1", act1, weights["w_down1"], extras=[x3], epilogue=residual)[0]

    d4, d4_op, g_final, loss_part = _final_loss(x4, tgt, final_norm.reshape(1, d_model))
    loss = lax.psum(loss_part[0, 0], ("x", "y", "c"))

    outputs = {}

    def start_reduce(tag, names, grads):
        ssem, rsem, grads, lands, tok = _reduce_start(f"reduce_start_{tag}", grads)
        return dict(tag=tag, names=names, ssem=ssem, rsem=rsem, grads=grads, lands=lands), tok

    def finish_reduce(st, after):
        grads, lands = _reduce_wait(f"reduce_wait_{st['tag']}", st["ssem"], st["rsem"], st["grads"], st["lands"], after)
        totals = [_sum_pieces(f"sum_{u}", g, l, place) for u, g, l in zip(st["names"], grads, lands)]
        ssem, rsem, totals, tok = _half_start(f"half_start_{st['tag']}", totals)
        return dict(tag=st["tag"], names=st["names"], ssem=ssem, rsem=rsem, totals=totals), tok

    def finish_update(st, after):
        totals = _half_wait(f"half_wait_{st['tag']}", st["ssem"], st["rsem"], st["totals"], after)
        for u, tot in zip(st["names"], totals):
            arr, layer, _ = units[u]
            w = given_w[arr]
            outputs[arr] = _adamw(f"adamw_{u}", w, tot.reshape(w.shape[1:]), given_m[arr], given_v[arr], layer,
                                  outputs.get(arr))

    def ffn_bwd(layer, d_out, d_out_op, xin, h, g, u, act):
        wd, wg, wu = weights[f"w_down{layer}"], weights[f"w_gate{layer}"], weights[f"w_up{layer}"]
        gr_down = _mm_tn(f"d_down_{layer}", act, d_out_op, wd)
        dg, du = _mm_nt(f"d_act_{layer}", [d_out_op], [wd], out_dtypes=(BF, BF), extras=[g, u],
                        epilogue=_swiglu_bwd_epilogue, col_chunk=EPILOGUE_CHUNK)
        gr_gate = _mm_tn(f"d_gate_{layer}", h, dg, wg)
        gr_up = _mm_tn(f"d_up_{layer}", h, du, wu)
        st, tok = start_reduce(f"ffn{layer}", [f"w_down{layer}", f"w_gate{layer}", f"w_up{layer}"], [gr_down, gr_gate, gr_up])
        dh = _mm_nt(f"d_ffn_h_{layer}", [dg, du], [wg, wu], out_dtypes=(BF,), after=[tok])[0]
        d_in, d_in_op, g_norm = _rms_bwd(f"ffn_norm_bwd_{layer}", xin, ffn_norm[layer:layer + 1], dh, d_out)
        return st, d_in, d_in_op, g_norm

    def memory_bwd(layer, mem_n, dkv):
        dkv = dkv.astype(BF)
        wkv = weights[f"w_mem_kv{layer}"]
        gr = _mm_tn(f"d_mem_kv_{layer}", mem_n, dkv, wkv)
        d_mem_n = _mm_nt(f"d_mem_n_{layer}", [dkv], [wkv])[0]
        return gr, _rms_bwd(f"mem_norm_bwd_{layer}", mem0, mem_norm[layer:layer + 1], d_mem_n)[2]

    st_ffn1, d3, d3_op, g_ffn1 = ffn_bwd(1, d4, d4_op, x3, hf1, g1, u1, act1)
    gr_sgu_out = _mm_tn("d_sgu_out", cat1, d3_op, weights["sgu_w_out"])
    dcat1 = _mm_nt("d_cat_1", [d3_op], [weights["sgu_w_out"]], out_dtypes=(BF,))[0]
    st_ffn1, tok = finish_reduce(st_ffn1, [dcat1])
    dproj1, dkv1 = _mem_bwd("mem_bwd_1", proj1, 6, kv1, dcat1, 3, proj1.shape[1])
    gr_kv1, g_mem1 = memory_bwd(1, mem_n1, dkv1)
    dproj1, g_wsp, g_bsp_t, g_ln_g, g_ln_b = _sgu_bwd(proj1, dcat1, ln_g, ln_b, w_sp, b_t, dproj1)
    gr_sgu_in = _mm_tn("d_sgu_in", h1, dproj1, weights["sgu_w_in"], after=[tok])
    finish_update(st_ffn1, [gr_sgu_in])
    st_mix1, tok = start_reduce("mix1", ["sgu_w_out", "w_mem_kv1", "sgu_w_in"], [gr_sgu_out, gr_kv1, gr_sgu_in])
    dh1 = _mm_nt("d_h_1", [dproj1], [weights["sgu_w_in"]], out_dtypes=(BF,), after=[tok])[0]
    d2, d2_op, g_mix1 = _rms_bwd("mix_norm_bwd_1", x2, mix_norm[1:2], dh1, d3)

    st_ffn0, d1, d1_op, g_ffn0 = ffn_bwd(0, d2, d2_op, x1, hf0, g0, u0, act0)
    dev = 4 * xi + 2 * yi + ci
    small_a = [g_mix1, g_mem1, jnp.concatenate([g_ffn0, g_ffn1]), g_wsp, g_bsp_t[:, :SGU_GROUPS].T, g_final, g_ln_g, g_ln_b]
    sa_ssem, sa_rsem, sa_slots, tok = _small_start("small_start_a", _own_slot(_pack(small_a, LANES), dev))
    gr_attn_out = _mm_tn("d_attn_out", cat0, d1_op, weights["attn_w_out"], after=[tok])
    st_mix1, tok = finish_reduce(st_mix1, [gr_attn_out])
    dcat0 = _mm_nt("d_cat_0", [d1_op], [weights["attn_w_out"]], after=[tok])[0]
    dproj0, dkv0 = _mem_bwd("mem_bwd_0", proj0, 9, kv0, dcat0, 1, proj0.shape[1])
    finish_update(st_mix1, [dkv0])
    gr_kv0, g_mem0 = memory_bwd(0, mem_n0, dkv0)
    st_ffn0, tok = finish_reduce(st_ffn0, [g_mem0])
    d_merged, delta = _attn_delta(dcat0, cat0, after=[tok])
    dqs, dks, dvs = [], [], []
    for g in range(len(DILATIONS)):
        dq, dk, dv = _dil_bwd(g, qs[g], ks[g], vs[g], d_merged[g], lse[g], delta[g])
        dqs.append(dq)
        dks.append(dk)
        dvs.append(dv)
    dproj0 = _rope_bwd(dqs + dks + dvs, tables, dproj0)
    finish_update(st_ffn0, [dproj0])
    gr_attn_in = _mm_tn("d_attn_in", h0, dproj0, weights["attn_w_in"])
    st_mix0, tok = start_reduce("mix0", ["attn_w_out", "w_mem_kv0", "attn_w_in"], [gr_attn_out, gr_kv0, gr_attn_in])
    dh0 = _mm_nt("d_h_0", [dproj0], [weights["attn_w_in"]], out_dtypes=(BF,), after=[tok])[0]
    d0, _, g_mix0 = _rms_bwd("mix_norm_bwd_0", x0, mix_norm[0:1], dh0, d1)

    small_b = [g_mix0, g_mem0]
    sb_ssem, sb_rsem, sb_slots, tok = _small_start("small_start_b", _own_slot(_pack(small_b, 8), dev))
    sa_slots = _small_wait("small_wait_a", sa_ssem, sa_rsem, sa_slots, [tok])
    g_mix1, g_mem1, g_ffn, g_wsp, g_bsp, g_final, g_ln_g, g_ln_b = _unpack(_sum_devices("small_sum_a", sa_slots),
                                                                           [t.shape for t in small_a])
    sb_slots = _small_wait("small_wait_b", sb_ssem, sb_rsem, sb_slots, [g_final])
    g_mix0, g_mem0 = _unpack(_sum_devices("small_sum_b", sb_slots), [t.shape for t in small_b])
    st_mix0, tok = finish_reduce(st_mix0, [g_mix0])
    g_mix, g_mem = jnp.concatenate([g_mix0, g_mix1]), jnp.concatenate([g_mem0, g_mem1])
    shard_w = sgu_ln_g.shape[-1]
    g_ln_g = lax.dynamic_slice_in_dim(g_ln_g, chip * shard_w, shard_w, axis=1)
    g_ln_b = lax.dynamic_slice_in_dim(g_ln_b, chip * shard_w, shard_w, axis=1)
    small_names = ["mix_norm", "mem_norm", "ffn_norm", "sgu_w_spatial", "sgu_b_spatial", "final_norm", "sgu_ln_g",
                   "sgu_ln_b"]
    small_g = [g_mix, g_mem, g_ffn, g_wsp, g_bsp, g_final, g_ln_g, g_ln_b]
    small_shapes = [given_w[k].shape for k in small_names]
    packed = [_pack(t, LANES) for t in ([given_w[k] for k in small_names], small_g, [given_m[k] for k in small_names],
                                    [given_v[k] for k in small_names])]
    small_out = _adamw("adamw_small", packed[0][None], packed[1], packed[2][None], packed[3][None], 0)
    finish_update(st_mix0, [small_out[0]])
    for k, gk, dk, mk, vk in zip(small_names, *[_unpack(t[0], small_shapes) for t in small_out]):
        outputs[k] = (gk, dk, mk, vk)

    order = ["mix_norm", "mem_norm", "w_mem_kv", "ffn_norm", "w_gate", "w_up", "w_down", "attn_w_in", "attn_w_out",
             "sgu_w_in", "sgu_ln_g", "sgu_ln_b", "sgu_w_spatial", "sgu_b_spatial", "sgu_w_out", "final_norm"]
    return (loss, d0[None], *[outputs[k][0] for k in order], *[outputs[k][1] for k in order],
            *[outputs[k][2] for k in order], *[outputs[k][3] for k in order])
```

```python
import math

import jax
import jax.numpy as jnp
from jax import lax
from jax.experimental import pallas as pl
from jax.experimental.pallas import tpu as pltpu

F32 = jnp.float32
BF = jnp.bfloat16
MESH = pl.DeviceIdType.MESH

HEAD_DIM = 128
MEM_HEADS = 4
MEM_W = MEM_HEADS * HEAD_DIM
GROUP_W = 4 * HEAD_DIM
DILATIONS = (1, 4, 16)
BLK = 128
SGU_GROUPS = 12
SGU_W = SGU_GROUPS * HEAD_DIM
ROT_HALF = 16
ROPE_THETA = 500000.0
NORM_EPS = 1e-6
LN_EPS = 1e-5
NEG_INF = -1e30
SCALE = HEAD_DIM ** -0.5
ADAM_LR, ADAM_B1, ADAM_B2, ADAM_EPS, ADAM_WD, ADAM_STEP = 0.001, 0.9, 0.999, 1e-08, 0.01, 10

VMEM_LIMIT = 48 * 2 ** 20
VMEM_TILE_BUDGET = 38 * 2 ** 20
N_CHIPS = 4
N_DEV = 8
LANES = 128
EPILOGUE_CHUNK = 768

NT_DIMS = (((1,), (1,)), ((), ()))
TN_DIMS = (((0,), (0,)), ((), ()))
NN_DIMS = (((1,), (0,)), ((), ()))

ANY = pl.BlockSpec(memory_space=pl.ANY)
HBM = pl.BlockSpec(memory_space=pltpu.HBM)
SEM = pl.BlockSpec(memory_space=pltpu.SEMAPHORE)
EFFECT = pltpu.SideEffectType.DATAFLOW_SIDE_EFFECTING


def _params(sem):
    return pltpu.CompilerParams(dimension_semantics=sem, vmem_limit_bytes=VMEM_LIMIT)


def _pick(n, cap):
    if n <= cap:
        return n
    best = None
    for t in range(LANES, cap + 1, LANES):
        if n % t == 0:
            best = t
    assert best is not None, (n, cap)
    return best


def _pick_rows(n, cap):
    t = min(n, cap)
    while n % t:
        t //= 2
    return t


def _mm(name, dims, a_list, a_specs, b_list, b_specs, pairs, n_acc, acc_shape, grid, extras, e_specs,
        out_shapes, out_specs, epilogue, after=(), col_chunk=None, store=None, shard_width=None, norm_gain=None):
    na, nb, ne, no = len(a_list), len(b_list), len(extras), len(out_shapes)
    nk = grid[-1]
    ng = 0 if norm_gain is None else 1

    def products(a, b, cols=None):
        sums = [None] * n_acc
        for ai, bi, ci in pairs:
            bv = b[bi]
            if cols is None:
                bv = bv[...]
            elif dims == NT_DIMS:
                bv = bv[cols, :]
            else:
                bv = bv[:, cols]
            if bv.ndim == 3:
                bv = bv.reshape(-1, bv.shape[-1])
            prod = lax.dot_general(a[ai][...].astype(BF), bv.astype(BF), dims, preferred_element_type=F32)
            sums[ci] = prod if sums[ci] is None else sums[ci] + prod
        return sums

    def body(*refs):
        a = refs[:na]
        b = refs[na:na + nb]
        e = refs[na + nb:na + nb + ne]
        off = na + nb + ne + ng + len(after)
        o = refs[off:off + no]
        acc = refs[off + no:]

        def normed():
            if ng:
                xf = o[0][...]
                r = lax.rsqrt(jnp.mean(xf * xf, axis=-1, keepdims=True) + NORM_EPS)
                o[-1][...] = (xf * r * refs[na + nb + ne][...]).astype(o[-1].dtype)

        def finish(sums):
            outs = epilogue(sums, [r[...] for r in e])
            if store is not None:
                store(o, outs)
                return
            for r, v in zip(o, outs):
                r[...] = v.astype(r.dtype)
            normed()

        if nk == 1 and shard_width:
            (ai, bi, _), = pairs
            av = a[ai][...].astype(BF)
            if dims == NT_DIMS:
                total = None
                for j in range(N_CHIPS):
                    cols = slice(j * shard_width, (j + 1) * shard_width)
                    prod = lax.dot_general(av[:, cols], b[bi][j].astype(BF), dims, preferred_element_type=F32)
                    total = prod if total is None else total + prod
                finish([total])
                return
            for j in range(N_CHIPS):
                cols = slice(j * shard_width, (j + 1) * shard_width)
                prod = lax.dot_general(av, b[bi][j].astype(BF), dims, preferred_element_type=F32)
                outs = epilogue([prod], [r[:, cols] for r in e])
                for r, v in zip(o, outs):
                    r[:, cols] = v.astype(r.dtype)
            normed()
            return
        if nk == 1 and col_chunk:
            width = acc_shape[1]
            left = [r[...].astype(BF) for r in a]
            for c0 in range(0, width, col_chunk):
                cols = slice(c0, min(c0 + col_chunk, width))
                outs = epilogue(products(left, b, cols), [r[:, cols] for r in e])
                for r, v in zip(o, outs):
                    r[:, cols] = v.astype(r.dtype)
            return
        if nk == 1:
            finish(products(a, b))
            return
        k = pl.program_id(len(grid) - 1)

        @pl.when(k == 0)
        def _():
            for c, v in zip(acc, products(a, b)):
                c[...] = v

        @pl.when(jnp.logical_and(k > 0, k < nk - 1))
        def _():
            for c, v in zip(acc, products(a, b)):
                c[...] += v

        @pl.when(k == nk - 1)
        def _():
            finish([c[...] + v for c, v in zip(acc, products(a, b))])

    gains = [] if norm_gain is None else [norm_gain]
    ins = [*a_list, *b_list, *extras, *gains, *after]
    in_specs = [*a_specs, *b_specs, *e_specs, *[pl.BlockSpec(g.shape, lambda *_: (0, 0)) for g in gains],
                *([ANY] * len(after))]
    sem = ("parallel",) * (len(grid) - 1) + ("arbitrary",)
    scratch = [] if nk == 1 else [pltpu.VMEM(acc_shape, F32)] * n_acc
    return pl.pallas_call(
        body, out_shape=out_shapes, grid=grid, in_specs=in_specs, out_specs=out_specs, scratch_shapes=scratch,
        name=name, compiler_params=_params(sem))(*ins)


def _tile_bytes(blocks, single=()):
    size = lambda s, d: math.prod(s) * jnp.dtype(d).itemsize
    return sum(2 * size(s, d) for s, d in blocks) + sum(size(s, d) for s, d in single)


def _first(acc, extra):
    return [acc[0]]


def _sigmoid(x):
    return 0.5 * (1.0 + jnp.tanh(0.5 * x))


class Weight:
    def __init__(self, arr, axis):
        self.arr, self.axis = arr, axis
        _, self.rows, self.cols = arr.shape


SMALL_WEIGHT_BYTES = 8 * 2 ** 20


def _is_small(w):
    return w.arr.size * w.arr.dtype.itemsize <= SMALL_WEIGHT_BYTES


def _mm_nn(name, a, w, extras=(), epilogue=_first, out_dtypes=(F32,), after=(), norm_gain=None):
    m, kdim = a.shape
    b_spec, shard_width = None, None
    weight_buffers = 2
    if norm_gain is not None:
        out_dtypes = (*out_dtypes, BF)
    if w.axis == "col" and _is_small(w):
        n_total = tn = N_CHIPS * w.cols
        tk, gn, gk = kdim, 1, 1
        shard_width = w.cols
        b_spec = pl.BlockSpec((N_CHIPS, kdim, w.cols), lambda n, i, k: (0, 0, 0))
    elif w.axis == "col":
        n_total = N_CHIPS * w.cols
        tn = _pick(w.cols, 1408)
        tk = _pick(kdim, 2048)
        ncb = w.cols // tn
        gn, gk = N_CHIPS * ncb, kdim // tk
        b_map = lambda n, i, k: (n // ncb, k, n % ncb)
    elif kdim <= 2048 and norm_gain is not None:
        n_total = tn = w.cols
        tk, gn, gk = kdim, 1, 1
        weight_buffers = 1
        b_spec = pl.BlockSpec(w.arr.shape, lambda n, i, k: (0, 0, 0), pipeline_mode=pl.Buffered(1))
    elif kdim <= 2048:
        n_total = w.cols
        tn = _pick(w.cols, 1024)
        tk = kdim
        gn, gk = n_total // tn, 1
        b_spec = pl.BlockSpec((N_CHIPS, w.rows, tn), lambda n, i, k: (0, 0, n))
    else:
        n_total = w.cols
        tn = _pick(w.cols, 1024)
        tk = _pick(w.rows, 1408)
        nkb = w.rows // tk
        gn, gk = n_total // tn, N_CHIPS * nkb
        b_map = lambda n, i, k: (k // nkb, k % nkb, n)
    if b_spec is None:
        b_spec = pl.BlockSpec((None, tk, tn), b_map)
    for tm in (1024, 512, 256, 128):
        if m % tm:
            continue
        blocks = [((tm, tk), a.dtype)] + [((tm, tn), e.dtype) for e in extras]
        blocks += [((tm, tn), d) for d in out_dtypes] + [((tm, tn), BF)]
        weight = [((tk, tn), BF)]
        if _tile_bytes(blocks + (weight if weight_buffers == 2 else []), weight if weight_buffers == 1 else ()) <= VMEM_TILE_BUDGET:
            break
    assert norm_gain is None or tn == n_total, name
    o_spec = pl.BlockSpec((tm, tn), lambda n, i, k: (i, n))
    return _mm(
        name, NN_DIMS, [a], [pl.BlockSpec((tm, tk), lambda n, i, k: (i, k))],
        [w.arr], [b_spec], [(0, 0, 0)], 1, (tm, tn), (gn, m // tm, gk),
        list(extras), [o_spec] * len(extras),
        [jax.ShapeDtypeStruct((m, n_total), d) for d in out_dtypes], [o_spec] * len(out_dtypes), epilogue, after,
        shard_width=shard_width, norm_gain=norm_gain)


def _gate_up(name, h, wg, wu):
    m, kdim = h.shape
    tn = _pick(wg.cols, 1408)
    tk = _pick(kdim, 2048)
    ncb = wg.cols // tn
    single = kdim == tk
    for tm in (1024, 512, 256, 128):
        blocks = [((tm, tk), BF)] + [((tm, tn), BF)] * 3
        weights = [((tk, tn), BF)] * 2
        if m % tm == 0 and _tile_bytes(blocks + ([] if single else weights), weights if single else ()) <= VMEM_TILE_BUDGET:
            break
    b_spec = pl.BlockSpec((None, tk, tn), lambda n, i, k: (n // ncb, k, n % ncb),
                          pipeline_mode=pl.Buffered(1) if single else None)
    o_spec = pl.BlockSpec((tm, tn), lambda n, i, k: (i, n))
    n_total = N_CHIPS * wg.cols

    def epilogue(acc, extra):
        g, u = acc
        return [g, u, g * _sigmoid(g) * u]

    return _mm(
        name, NN_DIMS, [h], [pl.BlockSpec((tm, tk), lambda n, i, k: (i, k))], [wg.arr, wu.arr], [b_spec, b_spec],
        [(0, 0, 0), (0, 1, 1)], 2, (tm, tn), (N_CHIPS * ncb, m // tm, kdim // tk), [], [],
        [jax.ShapeDtypeStruct((m, n_total), BF)] * 3, [o_spec] * 3, epilogue, col_chunk=EPILOGUE_CHUNK)


def _mm_nt(name, dys, ws, out_dtypes=(F32,), extras=(), epilogue=_first, after=(), col_chunk=None):
    m = dys[0].shape[0]
    w0 = ws[0]
    npair = len(dys)
    b_spec, shard_width = None, None
    if w0.axis == "col" and npair == 1 and _is_small(w0):
        k_total = tko = w0.rows
        tkc = N_CHIPS * w0.cols
        go, gk = 1, 1
        shard_width = w0.cols
        b_spec = pl.BlockSpec(w0.arr.shape, lambda o, i, k: (0, 0, 0))
    elif w0.axis == "col":
        k_total = w0.rows
        tko = _pick(k_total, 1024)
        tkc = _pick(w0.cols, 1408)
        nkb = w0.cols // tkc
        go, gk = k_total // tko, N_CHIPS * nkb
        b_map = lambda o, i, k: (k // nkb, o, k % nkb)
    else:
        k_total = N_CHIPS * w0.rows
        tko = _pick(w0.rows, 1408)
        tkc = _pick(w0.cols, 2048)
        nob = w0.rows // tko
        go, gk = N_CHIPS * nob, w0.cols // tkc
        b_map = lambda o, i, k: (o // nob, o % nob, k)
    single = gk == 1
    for tm in (1024, 512, 256, 128):
        if m % tm:
            continue
        blocks = [((tm, tkc), d.dtype) for d in dys]
        blocks += [((tm, tko), e.dtype) for e in extras] + [((tm, tko), d) for d in out_dtypes]
        blocks += [((tm, tko), BF)]
        weights = [((tko, tkc), BF)] * npair
        if _tile_bytes(blocks + ([] if single else weights), weights if single else ()) <= VMEM_TILE_BUDGET:
            break
    if b_spec is None:
        b_spec = pl.BlockSpec((None, tko, tkc), b_map, pipeline_mode=pl.Buffered(1) if single else None)
    o_spec = pl.BlockSpec((tm, tko), lambda o, i, k: (i, o))
    return _mm(
        name, NT_DIMS, list(dys), [pl.BlockSpec((tm, tkc), lambda o, i, k: (i, k))] * npair,
        [w.arr for w in ws], [b_spec] * npair,
        [(i, i, 0) for i in range(npair)], 1, (tm, tko), (go, m // tm, gk), list(extras), [o_spec] * len(extras),
        [jax.ShapeDtypeStruct((m, k_total), d) for d in out_dtypes], [o_spec] * len(out_dtypes), epilogue, after,
        col_chunk if gk == 1 and shard_width is None else None, shard_width=shard_width)


def _mm_tn(name, a, dy, w, after=()):
    m, k_total = a.shape
    rows2 = w.rows // 2
    tn = _pick(w.cols, 1408)
    ncb = w.cols // tn
    epilogue, store = _first, None
    if k_total <= 2048 and w.axis == "col" and _is_small(w):
        tkr, tn = k_total, N_CHIPS * w.cols
        gr, gn = 1, 1
        o_spec = pl.BlockSpec((2, N_CHIPS, rows2, w.cols), lambda r, n, t: (0, 0, 0, 0))

        def store(o_refs, outs):
            for j in range(N_CHIPS):
                for h in range(2):
                    o_refs[0][h, j] = outs[0][h * rows2:(h + 1) * rows2, j * w.cols:(j + 1) * w.cols].astype(BF)
    elif k_total <= 2048 and w.axis == "col":
        tkr = k_total
        gr, gn = 1, N_CHIPS * ncb
        o_spec = pl.BlockSpec((2, None, rows2, tn), lambda r, n, t: (0, n // ncb, 0, n % ncb))
        epilogue = lambda acc, extra: [acc[0].reshape(2, rows2, tn)]
    elif k_total <= 2048:
        tkr = k_total
        gr, gn = 1, ncb
        o_spec = pl.BlockSpec((2, N_CHIPS, rows2, tn), lambda r, n, t: (0, 0, 0, n))

        def store(o_refs, outs):
            for j in range(N_CHIPS):
                for h in range(2):
                    lo = (2 * j + h) * rows2
                    o_refs[0][h, j] = outs[0][lo:lo + rows2].astype(BF)
    elif rows2 % LANES:
        tkr = w.rows
        assert w.axis == "row"
        gr, gn = N_CHIPS, ncb
        o_spec = pl.BlockSpec((2, None, rows2, tn), lambda r, n, t: (0, r, 0, n))
        epilogue = lambda acc, extra: [acc[0].reshape(2, rows2, tn)]
    else:
        tkr = _pick(rows2, 1408)
        nrb = rows2 // tkr
        if w.axis == "col":
            gr, gn = w.rows // tkr, N_CHIPS * ncb
            o_map = lambda r, n, t: (r // nrb, n // ncb, r % nrb, n % ncb)
        else:
            per = w.rows // tkr
            gr, gn = N_CHIPS * per, ncb
            o_map = lambda r, n, t: ((r % per) // nrb, r // per, (r % per) % nrb, n)
        o_spec = pl.BlockSpec((None, None, tkr, tn), o_map)
    for tmk in (1024, 512, 256, 128):
        blocks = [((tmk, tkr), a.dtype), ((tmk, tn), dy.dtype), ((tkr, tn), BF), ((tkr, tn), BF)]
        if m % tmk == 0 and _tile_bytes(blocks) <= VMEM_TILE_BUDGET:
            break
    return _mm(
        name, TN_DIMS, [a], [pl.BlockSpec((tmk, tkr), lambda r, n, t: (t, r))],
        [dy], [pl.BlockSpec((tmk, tn), lambda r, n, t: (t, n))], [(0, 0, 0)], 1, (tkr, tn), (gr, gn, m // tmk), [], [],
        [jax.ShapeDtypeStruct((2, N_CHIPS, rows2, w.cols), BF)], [o_spec], epilogue, after, store=store)[0]


def _rms_fwd(name, x, g, after=()):
    s, d = x.shape
    tr = _pick_rows(s, 512)

    def body(x_ref, g_ref, *rest):
        h_ref = rest[-1]
        xf = x_ref[...]
        r = lax.rsqrt(jnp.mean(xf * xf, axis=-1, keepdims=True) + NORM_EPS)
        h_ref[...] = (xf * r * g_ref[...]).astype(BF)

    return pl.pallas_call(
        body, out_shape=jax.ShapeDtypeStruct((s, d), BF), grid=(s // tr,),
        in_specs=[pl.BlockSpec((tr, d), lambda i: (i, 0)), pl.BlockSpec((1, d), lambda i: (0, 0))] + [ANY] * len(after),
        out_specs=pl.BlockSpec((tr, d), lambda i: (i, 0)), name=name, compiler_params=_params(("parallel",)))(x, g, *after)


def _rms_bwd(name, x, g, dh, dres=None):
    s, d = x.shape
    tr = _pick_rows(s, 256)
    has_res = dres is not None

    def body(*refs):
        if has_res:
            x_ref, g_ref, dh_ref, dres_ref, dx_ref, dxb_ref, dg_ref = refs
        else:
            x_ref, g_ref, dh_ref, dx_ref, dxb_ref, dg_ref = refs
        xf = x_ref[...]
        r = lax.rsqrt(jnp.mean(xf * xf, axis=-1, keepdims=True) + NORM_EPS)
        xr = xf * r
        dy = dh_ref[...].astype(F32)
        a = dy * g_ref[...]
        dx = r * (a - xr * jnp.mean(a * xr, axis=-1, keepdims=True))
        if has_res:
            dx = dx + dres_ref[...]
        dx_ref[...] = dx
        dxb_ref[...] = dx.astype(BF)

        @pl.when(pl.program_id(0) == 0)
        def _():
            dg_ref[...] = jnp.zeros_like(dg_ref)

        dg_ref[...] += jnp.sum(dy * xr, axis=0, keepdims=True)

    row = pl.BlockSpec((tr, d), lambda i: (i, 0))
    vec = pl.BlockSpec((1, d), lambda i: (0, 0))
    ins = [x, g, dh] + ([dres] if has_res else [])
    in_specs = [row, vec, row] + ([row] if has_res else [])
    return pl.pallas_call(
        body, out_shape=[jax.ShapeDtypeStruct((s, d), F32), jax.ShapeDtypeStruct((s, d), BF),
                         jax.ShapeDtypeStruct((1, d), F32)],
        grid=(s // tr,), in_specs=in_specs, out_specs=[row, row, vec], name=name,
        compiler_params=_params(("arbitrary",)))(*ins)


def _final_loss(x, tgt, g):
    s, d = x.shape
    tr = _pick_rows(s, 256)

    def body(x_ref, t_ref, g_ref, dx_ref, dxb_ref, dg_ref, loss_ref):
        xf = x_ref[...]
        gain = g_ref[...]
        r = lax.rsqrt(jnp.mean(xf * xf, axis=-1, keepdims=True) + NORM_EPS)
        xr = xf * r
        err = xr * gain - t_ref[...]
        dy = err * (1.0 / d)
        a = dy * gain
        dx = r * (a - xr * jnp.mean(a * xr, axis=-1, keepdims=True))
        dx_ref[...] = dx
        dxb_ref[...] = dx.astype(BF)

        @pl.when(pl.program_id(0) == 0)
        def _():
            dg_ref[...] = jnp.zeros_like(dg_ref)
            loss_ref[...] = jnp.zeros_like(loss_ref)

        dg_ref[...] += jnp.sum(dy * xr, axis=0, keepdims=True)
        part = 0.5 * jnp.sum(jnp.mean(err * err, axis=-1, keepdims=True), axis=0, keepdims=True)
        loss_ref[...] += jnp.broadcast_to(part, loss_ref.shape)

    row = pl.BlockSpec((tr, d), lambda i: (i, 0))
    vec = pl.BlockSpec((1, d), lambda i: (0, 0))
    return pl.pallas_call(
        body, out_shape=[jax.ShapeDtypeStruct((s, d), F32), jax.ShapeDtypeStruct((s, d), BF),
                         jax.ShapeDtypeStruct((1, d), F32), jax.ShapeDtypeStruct((8, LANES), F32)],
        grid=(s // tr,), in_specs=[row, row, vec],
        out_specs=[row, row, vec, pl.BlockSpec((8, LANES), lambda i: (0, 0))],
        name="final_loss", compiler_params=_params(("arbitrary",)))(x, tgt, g)


def _swiglu_bwd_epilogue(acc, extra):
    dact = acc[0]
    g, u = extra[0].astype(F32), extra[1].astype(F32)
    sig = _sigmoid(g)
    return [dact * u * sig * (1.0 + g * (1.0 - sig)), dact * g * sig]


GELU_C = math.sqrt(2.0 / math.pi)
GELU_A = 0.044715


def _gelu(x):
    return 0.5 * x * (1.0 + jnp.tanh(GELU_C * (x + GELU_A * x * x * x)))


def _gelu_both(x):
    x2 = x * x
    t = jnp.tanh(GELU_C * (x + GELU_A * x2 * x))
    half = 0.5 * (1.0 + t)
    return x * half, half + 0.5 * x * (1.0 - t * t) * GELU_C * (1.0 + 3.0 * GELU_A * x2)


def _rope_tables(positions):
    inv_freq = ROPE_THETA ** (-jnp.arange(ROT_HALF, dtype=F32) / ROT_HALF)
    ang = positions.astype(F32)[:, None] * inv_freq
    cos, sin = jnp.cos(ang), jnp.sin(ang)
    s = ang.shape[0]
    rest = HEAD_DIM - 2 * ROT_HALF
    zeros = jnp.zeros((s, ROT_HALF), F32)
    cos_t = jnp.concatenate([cos, cos, jnp.ones((s, rest), F32)], axis=1)
    sin_a = jnp.concatenate([-sin, zeros, jnp.zeros((s, rest), F32)], axis=1)
    sin_b = jnp.concatenate([zeros, sin, jnp.zeros((s, rest), F32)], axis=1)
    return cos_t, sin_a, sin_b


def _rope_head(xh, cos_t, sin_a, sin_b):
    up = pltpu.roll(xh, HEAD_DIM - ROT_HALF, 1)
    down = pltpu.roll(xh, ROT_HALF, 1)
    return xh * cos_t + up * sin_a + down * sin_b


def _residue(r, rows, dil):
    return slice(None) if dil == 1 else pl.ds(r, rows, stride=dil)


ROPE_TILE = 256
N_PARTS = 9
HEADS_PER_GROUP = GROUP_W // HEAD_DIM
N_HEADS_IN = N_PARTS * HEADS_PER_GROUP


def _rope_fwd(proj, tables):
    s = proj.shape[0]
    tm = _pick_rows(s, ROPE_TILE)

    def body(*refs):
        heads = refs[:N_HEADS_IN]
        c_ref, sa_ref, sb_ref = refs[N_HEADS_IN:N_HEADS_IN + 3]
        outs = refs[N_HEADS_IN + 3:]
        for g, dil in enumerate(DILATIONS):
            rows = tm // dil
            for r in range(dil):
                rs = _residue(r, rows, dil)
                cos_t, sin_a, sin_b = c_ref[rs, :], sa_ref[rs, :], sb_ref[rs, :]
                for kind in range(3):
                    part = 3 * kind + g
                    for h in range(HEADS_PER_GROUP):
                        xh = heads[part * HEADS_PER_GROUP + h][rs, :]
                        if kind < 2:
                            xh = _rope_head(xh, cos_t, sin_a, sin_b)
                        outs[part][r, :, h * HEAD_DIM:(h + 1) * HEAD_DIM] = xh.astype(BF)

    tab = pl.BlockSpec((tm, HEAD_DIM), lambda i: (i, 0))
    head_specs = [pl.BlockSpec((tm, HEAD_DIM), lambda i, j=j: (i, j)) for j in range(N_HEADS_IN)]
    shapes, specs = [], []
    for part in range(N_PARTS):
        dil = DILATIONS[part % 3]
        shapes.append(jax.ShapeDtypeStruct((dil, s // dil, GROUP_W), BF))
        specs.append(pl.BlockSpec((dil, tm // dil, GROUP_W), lambda i: (0, i, 0)))
    return pl.pallas_call(
        body, out_shape=shapes, grid=(s // tm,), in_specs=head_specs + [tab, tab, tab], out_specs=specs,
        name="rope_fwd", compiler_params=_params(("parallel",)))(*([proj] * N_HEADS_IN), *tables)


def _rope_bwd(parts, tables, into):
    s = into.shape[0]
    tm = _pick_rows(s, ROPE_TILE)

    def body(*refs):
        ins = refs[:N_PARTS]
        c_ref, sa_ref, sb_ref, into_ref, o_ref, scr = refs[N_PARTS:]
        for g, dil in enumerate(DILATIONS):
            rows = tm // dil
            for r in range(dil):
                rs = _residue(r, rows, dil)
                cos_t, sin_a, sin_b = c_ref[rs, :], -sa_ref[rs, :], -sb_ref[rs, :]
                for kind in range(3):
                    part = 3 * kind + g
                    for h in range(HEADS_PER_GROUP):
                        xh = ins[part][r, :, h * HEAD_DIM:(h + 1) * HEAD_DIM]
                        if kind < 2:
                            xh = _rope_head(xh, cos_t, sin_a, sin_b)
                        scr[part * HEADS_PER_GROUP + h, rs, :] = xh
        for j in range(N_HEADS_IN):
            o_ref[:, j * HEAD_DIM:(j + 1) * HEAD_DIM] = scr[j].astype(BF)

    tab = pl.BlockSpec((tm, HEAD_DIM), lambda i: (i, 0))
    i_specs = [pl.BlockSpec((DILATIONS[p % 3], tm // DILATIONS[p % 3], GROUP_W), lambda i: (0, i, 0))
               for p in range(N_PARTS)]
    return pl.pallas_call(
        body, out_shape=jax.ShapeDtypeStruct(into.shape, into.dtype), grid=(s // tm,),
        in_specs=i_specs + [tab] * 3 + [ANY], out_specs=pl.BlockSpec((tm, N_PARTS * GROUP_W), lambda i: (i, 0)),
        scratch_shapes=[pltpu.VMEM((N_HEADS_IN, tm, HEAD_DIM), F32)], input_output_aliases={N_PARTS + 3: 0},
        name="rope_bwd", compiler_params=_params(("parallel",)))(*parts, *tables, into)


def _band_mask(n):
    qi = lax.broadcasted_iota(jnp.int32, (BLK, 2 * BLK), 0)
    ki = lax.broadcasted_iota(jnp.int32, (BLK, 2 * BLK), 1)
    prev = jnp.logical_and(jnp.logical_and(ki < BLK, ki >= qi), n > 0)
    return jnp.logical_or(prev, jnp.logical_and(ki >= BLK, qi >= ki - BLK))


Q_BLOCKS = 4
Q_ROWS = Q_BLOCKS * BLK


STAT_LANES = HEAD_DIM // HEADS_PER_GROUP


def _stat_of(ref, rows, h):
    return ref[rows, h * STAT_LANES:h * STAT_LANES + 1]


def _pack_stats(cols):
    rows = cols[0].shape[0]
    lane = lax.broadcasted_iota(jnp.int32, (rows, HEAD_DIM), 1)
    tile = jnp.broadcast_to(cols[-1], (rows, HEAD_DIM))
    for h in range(HEADS_PER_GROUP - 2, -1, -1):
        tile = jnp.where(lane < (h + 1) * STAT_LANES, cols[h], tile)
    return tile


def _dil_specs(n_steps):
    last = n_steps - 1
    own = pl.BlockSpec((None, Q_ROWS, GROUP_W), lambda r, n: (r, jnp.minimum(n, last), 0))
    before = pl.BlockSpec((None, BLK, GROUP_W), lambda r, n: (r, jnp.maximum(Q_BLOCKS * n - 1, 0), 0))
    stat = pl.BlockSpec((None, Q_ROWS, HEAD_DIM), lambda r, n: (r, jnp.minimum(n, last), 0))
    return own, before, stat


def _dil_fwd(g, q, k, v):
    dil, length, _ = q.shape
    n_steps = length // Q_ROWS

    def body(q_ref, ko_ref, kb_ref, vo_ref, vb_ref, o_ref, lse_ref):
        n = pl.program_id(1)
        lse_cols = [[] for _ in range(Q_BLOCKS)]
        for h in range(GROUP_W // HEAD_DIM):
            sl = slice(h * HEAD_DIM, (h + 1) * HEAD_DIM)
            keys = jnp.concatenate([kb_ref[:, sl], ko_ref[:, sl]], axis=0)
            vals = jnp.concatenate([vb_ref[:, sl], vo_ref[:, sl]], axis=0)
            for j in range(Q_BLOCKS):
                rows, win = slice(j * BLK, (j + 1) * BLK), slice(j * BLK, (j + 2) * BLK)
                sc = lax.dot_general(q_ref[rows, sl], keys[win], NT_DIMS, preferred_element_type=F32) * SCALE
                sc = jnp.where(_band_mask(Q_BLOCKS * n + j), sc, NEG_INF)
                mx = jnp.max(sc, axis=-1, keepdims=True)
                p = jnp.exp(sc - mx)
                den = jnp.sum(p, axis=-1, keepdims=True)
                o_ref[rows, sl] = jnp.dot(p.astype(BF), vals[win], preferred_element_type=F32) / den
                lse_cols[j].append(mx + jnp.log(den))
        for j in range(Q_BLOCKS):
            lse_ref[j * BLK:(j + 1) * BLK, :] = _pack_stats(lse_cols[j])

    own, before, stat = _dil_specs(n_steps)
    return pl.pallas_call(
        body, out_shape=[jax.ShapeDtypeStruct(q.shape, F32), jax.ShapeDtypeStruct((dil, length, HEAD_DIM), F32)],
        grid=(dil, n_steps), in_specs=[own, own, before, own, before], out_specs=[own, stat], name=f"dil_fwd_{g}",
        compiler_params=_params(("parallel", "arbitrary")))(q, k, k, v, v)


def _dil_bwd(g, q, k, v, do, lse, delta):
    dil, length, _ = q.shape
    n_steps = length // Q_ROWS

    def body(q_ref, ko_ref, kb_ref, vo_ref, vb_ref, do_ref, lse_ref, dl_ref, dq_ref, dk_ref, dv_ref, ck_ref, cv_ref):
        n = pl.program_id(1)
        live = n < n_steps

        @pl.when(n == 0)
        def _():
            ck_ref[...] = jnp.zeros_like(ck_ref)
            cv_ref[...] = jnp.zeros_like(cv_ref)

        @pl.when(jnp.logical_not(live))
        def _():
            dk_ref[...] = ck_ref[...]
            dv_ref[...] = cv_ref[...]

        @pl.when(live)
        def _():
            for h in range(GROUP_W // HEAD_DIM):
                sl = slice(h * HEAD_DIM, (h + 1) * HEAD_DIM)
                keys = jnp.concatenate([kb_ref[:, sl], ko_ref[:, sl]], axis=0)
                vals = jnp.concatenate([vb_ref[:, sl], vo_ref[:, sl]], axis=0)
                dks, dvs = [], []
                for j in range(Q_BLOCKS):
                    rows, win = slice(j * BLK, (j + 1) * BLK), slice(j * BLK, (j + 2) * BLK)
                    qh, doh = q_ref[rows, sl], do_ref[rows, sl]
                    lse_h, dl_h = _stat_of(lse_ref, rows, h), _stat_of(dl_ref, rows, h)
                    sc = lax.dot_general(qh, keys[win], NT_DIMS, preferred_element_type=F32) * SCALE
                    p = jnp.where(_band_mask(Q_BLOCKS * n + j), jnp.exp(jnp.minimum(sc - lse_h, 0.0)), 0.0)
                    dp = lax.dot_general(doh, vals[win], NT_DIMS, preferred_element_type=F32)
                    ds = (p * (dp - dl_h) * SCALE).astype(BF)
                    dq_ref[rows, sl] = jnp.dot(ds, keys[win], preferred_element_type=F32)
                    dks.append(lax.dot_general(ds, qh, TN_DIMS, preferred_element_type=F32))
                    dvs.append(lax.dot_general(p.astype(BF), doh, TN_DIMS, preferred_element_type=F32))
                for out_ref, carry, parts in ((dk_ref, ck_ref, dks), (dv_ref, cv_ref, dvs)):
                    out_ref[:Q_ROWS - BLK, sl] = carry[:Q_ROWS - BLK, sl]
                    out_ref[Q_ROWS - BLK:, sl] = carry[Q_ROWS - BLK:, sl] + parts[0][:BLK]
                    for j in range(Q_BLOCKS - 1):
                        carry[j * BLK:(j + 1) * BLK, sl] = parts[j][BLK:] + parts[j + 1][:BLK]
                    carry[Q_ROWS - BLK:, sl] = parts[-1][BLK:]

    own, before, stat = _dil_specs(n_steps)
    behind = pl.BlockSpec((None, Q_ROWS, GROUP_W), lambda r, n: (r, jnp.maximum(n - 1, 0), 0))
    return pl.pallas_call(
        body, out_shape=[jax.ShapeDtypeStruct(q.shape, F32)] * 3, grid=(dil, n_steps + 1),
        in_specs=[own, own, before, own, before, own, stat, stat], out_specs=[own, behind, behind],
        scratch_shapes=[pltpu.VMEM((Q_ROWS, GROUP_W), F32)] * 2, name=f"dil_bwd_{g}",
        compiler_params=_params(("parallel", "arbitrary")))(q, k, k, v, v, do, lse, delta)


def _major_specs(s, tm, dtype, width=GROUP_W):
    shapes = [jax.ShapeDtypeStruct((dil, s // dil, width), dtype) for dil in DILATIONS]
    specs = [pl.BlockSpec((dil, tm // dil, width), lambda i: (0, i, 0)) for dil in DILATIONS]
    return shapes, specs


def _attn_merge(outs, lses):
    s = outs[0].shape[1]
    tm = _pick_rows(s, ROPE_TILE)

    def body(o0, o1, o2, l0, l1, l2, m_ref, e0, e1, e2, so1, so2, sl1, sl2, se):
        for dil, src, dst in ((DILATIONS[1], l1, sl1), (DILATIONS[2], l2, sl2)):
            for r in range(dil):
                dst[_residue(r, tm // dil, dil), :] = src[r]
        a, b, c = l0[0], sl1[...], sl2[...]
        mx = jnp.maximum(jnp.maximum(a, b), c)
        ea, eb, ec = jnp.exp(a - mx), jnp.exp(b - mx), jnp.exp(c - mx)
        den = ea + eb + ec
        wa, wb, wc = ea / den, eb / den, ec / den
        se[...] = mx + jnp.log(den)
        for dil, dst in zip(DILATIONS, (e0, e1, e2)):
            for r in range(dil):
                dst[r] = se[_residue(r, tm // dil, dil), :]
        for h in range(HEADS_PER_GROUP):
            sl = slice(h * HEAD_DIM, (h + 1) * HEAD_DIM)
            col = slice(h * STAT_LANES, h * STAT_LANES + 1)
            for dil, src, dst in ((DILATIONS[1], o1, so1), (DILATIONS[2], o2, so2)):
                for r in range(dil):
                    dst[h, _residue(r, tm // dil, dil), :] = src[r, :, sl]
            m_ref[:, sl] = (wa[:, col] * o0[0, :, sl] + wb[:, col] * so1[h] + wc[:, col] * so2[h]).astype(BF)

    shapes, specs = _major_specs(s, tm, F32)
    stat_shapes, stat_specs = _major_specs(s, tm, F32, HEAD_DIM)
    nat = pl.BlockSpec((tm, GROUP_W), lambda i: (i, 0))
    res = pl.pallas_call(
        body, out_shape=[jax.ShapeDtypeStruct((s, GROUP_W + MEM_W), BF)] + stat_shapes, grid=(s // tm,),
        in_specs=specs + stat_specs, out_specs=[nat] + stat_specs,
        scratch_shapes=[pltpu.VMEM((HEADS_PER_GROUP, tm, HEAD_DIM), F32)] * 2 + [pltpu.VMEM((tm, HEAD_DIM), F32)] * 3,
        name="attn_merge", compiler_params=_params(("parallel",)))(*outs, *lses)
    return res[0], res[1:]


def _attn_delta(dcat, merged, after=()):
    s = merged.shape[0]
    tm = _pick_rows(s, ROPE_TILE)

    def body(*refs):
        d_refs, m_ref = refs[:HEADS_PER_GROUP], refs[HEADS_PER_GROUP]
        do_refs, dl_refs, scr = refs[-7:-4], refs[-4:-1], refs[-1]
        sums = []
        for h in range(HEADS_PER_GROUP):
            sl = slice(h * HEAD_DIM, (h + 1) * HEAD_DIM)
            sums.append(jnp.sum(d_refs[h][...] * m_ref[:, sl].astype(F32), axis=-1, keepdims=True))
            for dil, do_ref in zip(DILATIONS, do_refs):
                for r in range(dil):
                    do_ref[r, :, sl] = d_refs[h][_residue(r, tm // dil, dil), :].astype(BF)
        scr[...] = _pack_stats(sums)
        for dil, dl_ref in zip(DILATIONS, dl_refs):
            for r in range(dil):
                dl_ref[r] = scr[_residue(r, tm // dil, dil), :]

    nat = pl.BlockSpec((tm, GROUP_W), lambda i: (i, 0))
    head_specs = [pl.BlockSpec((tm, HEAD_DIM), lambda i, h=h: (i, h)) for h in range(HEADS_PER_GROUP)]
    bf_shapes, specs = _major_specs(s, tm, BF)
    stat_shapes, stat_specs = _major_specs(s, tm, F32, HEAD_DIM)
    res = pl.pallas_call(
        body, out_shape=bf_shapes + stat_shapes, grid=(s // tm,), in_specs=head_specs + [nat] + [ANY] * len(after),
        out_specs=specs + stat_specs, scratch_shapes=[pltpu.VMEM((tm, HEAD_DIM), F32)], name="attn_delta",
        compiler_params=_params(("parallel",)))(*([dcat] * HEADS_PER_GROUP), merged, *after)
    return res[:3], res[3:]


def _mem_probs(qh, kh):
    sc = lax.dot_general(qh, kh, NT_DIMS, preferred_element_type=F32) * SCALE
    p = jnp.exp(sc - jnp.max(sc, axis=-1, keepdims=True))
    return p, jnp.sum(p, axis=-1, keepdims=True)


def _mem_fwd(name, proj, q_block, kv, into, out_block):
    s = proj.shape[0]
    tq = _pick_rows(s, 512)

    def body(q_ref, kv_ref, into_ref, o_ref):
        for h in range(MEM_HEADS):
            sl = slice(h * HEAD_DIM, (h + 1) * HEAD_DIM)
            vsl = slice(MEM_W + h * HEAD_DIM, MEM_W + (h + 1) * HEAD_DIM)
            p, den = _mem_probs(q_ref[:, sl].astype(BF), kv_ref[:, sl].astype(BF))
            out = jnp.dot(p.astype(BF), kv_ref[:, vsl].astype(BF), preferred_element_type=F32) / den
            o_ref[:, sl] = out.astype(o_ref.dtype)

    return pl.pallas_call(
        body, out_shape=jax.ShapeDtypeStruct(into.shape, into.dtype), grid=(s // tq,),
        in_specs=[pl.BlockSpec((tq, MEM_W), lambda i: (i, q_block)), pl.BlockSpec(kv.shape, lambda i: (0, 0)), ANY],
        out_specs=pl.BlockSpec((tq, MEM_W), lambda i: (i, out_block)), input_output_aliases={2: 0}, name=name,
        compiler_params=_params(("parallel",)))(proj, kv, into)


def _mem_bwd(name, proj, q_block, kv, dcat, d_block, width):
    s = proj.shape[0]
    tq = _pick_rows(s, 512)

    def body(q_ref, kv_ref, do_ref, dq_ref, dkv_ref):
        @pl.when(pl.program_id(0) == 0)
        def _():
            dkv_ref[...] = jnp.zeros_like(dkv_ref)

        for h in range(MEM_HEADS):
            sl = slice(h * HEAD_DIM, (h + 1) * HEAD_DIM)
            vsl = slice(MEM_W + h * HEAD_DIM, MEM_W + (h + 1) * HEAD_DIM)
            qh, kh, vh = q_ref[:, sl].astype(BF), kv_ref[:, sl].astype(BF), kv_ref[:, vsl].astype(BF)
            doh = do_ref[:, sl].astype(BF)
            p, den = _mem_probs(qh, kh)
            p = p / den
            dp = lax.dot_general(doh, vh, NT_DIMS, preferred_element_type=F32)
            ds = (p * (dp - jnp.sum(p * dp, axis=-1, keepdims=True)) * SCALE).astype(BF)
            dq_ref[:, sl] = jnp.dot(ds, kh, preferred_element_type=F32).astype(BF)
            dkv_ref[:, sl] += lax.dot_general(ds, qh, TN_DIMS, preferred_element_type=F32)
            dkv_ref[:, vsl] += lax.dot_general(p.astype(BF), doh, TN_DIMS, preferred_element_type=F32)

    whole = pl.BlockSpec(kv.shape, lambda i: (0, 0))
    return pl.pallas_call(
        body, out_shape=[jax.ShapeDtypeStruct((s, width), BF), jax.ShapeDtypeStruct(kv.shape, F32)], grid=(s // tq,),
        in_specs=[pl.BlockSpec((tq, MEM_W), lambda i: (i, q_block)), whole,
                  pl.BlockSpec((tq, MEM_W), lambda i: (i, d_block))],
        out_specs=[pl.BlockSpec((tq, MEM_W), lambda i: (i, width // MEM_W - 1)), whole], name=name,
        compiler_params=_params(("arbitrary",)))(proj, kv, dcat)


def _causal():
    t = lax.broadcasted_iota(jnp.int32, (BLK, BLK), 0)
    s = lax.broadcasted_iota(jnp.int32, (BLK, BLK), 1)
    return t >= s


def _sgu_norm(vg, ln_g, ln_b):
    mu = jnp.mean(vg, axis=-1, keepdims=True)
    cen = vg - mu
    rstd = lax.rsqrt(jnp.mean(cen * cen, axis=-1, keepdims=True) + LN_EPS)
    xhat = cen * rstd
    return xhat, rstd, xhat * ln_g + ln_b


def _sgu_fwd(proj, ln_g, ln_b, w_sp, b_t):
    s = proj.shape[0]

    def body(u_ref, v_ref, g_ref, b_ref, w_ref, bt_ref, o_ref):
        _, _, vn = _sgu_norm(_gelu(v_ref[...].astype(F32)), g_ref[...], b_ref[...])
        vn = vn.astype(BF)
        tri = _causal()
        for grp in range(SGU_GROUPS):
            sl = slice(grp * HEAD_DIM, (grp + 1) * HEAD_DIM)
            w = jnp.where(tri, w_ref[grp], 0.0).astype(BF)
            mixed = jnp.dot(w, vn[:, sl], preferred_element_type=F32) + bt_ref[:, grp:grp + 1]
            o_ref[:, sl] = (_gelu(u_ref[:, sl].astype(F32)) * mixed).astype(BF)

    vec = pl.BlockSpec((1, SGU_W), lambda i: (0, 0))
    return pl.pallas_call(
        body, out_shape=jax.ShapeDtypeStruct((s, SGU_W + MEM_W), BF), grid=(s // BLK,),
        in_specs=[pl.BlockSpec((BLK, SGU_W), lambda i: (i, 0)), pl.BlockSpec((BLK, SGU_W), lambda i: (i, 1)), vec, vec,
                  pl.BlockSpec(w_sp.shape, lambda i: (0, 0, 0)), pl.BlockSpec(b_t.shape, lambda i: (0, 0))],
        out_specs=pl.BlockSpec((BLK, SGU_W), lambda i: (i, 0)), name="sgu_fwd",
        compiler_params=_params(("parallel",)))(proj, proj, ln_g, ln_b, w_sp, b_t)


def _sgu_bwd(proj, dcat, ln_g, ln_b, w_sp, b_t, into):
    s = proj.shape[0]

    def body(u_ref, v_ref, d_ref, g_ref, b_ref, w_ref, bt_ref, into_ref, dp_ref, dw_ref, db_ref, dg_ref, dbeta_ref,
             dvn_ref):
        @pl.when(pl.program_id(0) == 0)
        def _():
            dw_ref[...] = jnp.zeros_like(dw_ref)
            db_ref[...] = jnp.zeros_like(db_ref)
            dg_ref[...] = jnp.zeros_like(dg_ref)
            dbeta_ref[...] = jnp.zeros_like(dbeta_ref)

        gain = g_ref[...]
        vg, v_slope = _gelu_both(v_ref[...].astype(F32))
        xhat, rstd, vn = _sgu_norm(vg, gain, b_ref[...])
        vn = vn.astype(BF)
        tri = _causal()
        lane = lax.broadcasted_iota(jnp.int32, (BLK, HEAD_DIM), 1)
        db_acc = jnp.zeros((BLK, HEAD_DIM), F32)
        for grp in range(SGU_GROUPS):
            sl = slice(grp * HEAD_DIM, (grp + 1) * HEAD_DIM)
            w = jnp.where(tri, w_ref[grp], 0.0).astype(BF)
            vn_g = vn[:, sl]
            mixed = jnp.dot(w, vn_g, preferred_element_type=F32) + bt_ref[:, grp:grp + 1]
            u_act, u_slope = _gelu_both(u_ref[:, sl].astype(F32))
            d_out = d_ref[:, sl].astype(F32)
            dp_ref[:, sl] = (d_out * mixed * u_slope).astype(BF)
            dmixed = d_out * u_act
            dm = dmixed.astype(BF)
            dvn_ref[:, sl] = lax.dot_general(w, dm, TN_DIMS, preferred_element_type=F32)
            dw = lax.dot_general(dm, vn_g, NT_DIMS, preferred_element_type=F32)
            dw_ref[grp] += jnp.where(tri, dw, 0.0)
            db_acc += jnp.where(lane == grp, jnp.sum(dmixed, axis=-1, keepdims=True), 0.0)
        db_ref[...] += db_acc
        dvn = dvn_ref[...]
        dg_ref[...] += jnp.sum(dvn * xhat, axis=0, keepdims=True)
        dbeta_ref[...] += jnp.sum(dvn, axis=0, keepdims=True)
        dxh = dvn * gain
        dvg = rstd * (dxh - jnp.mean(dxh, axis=-1, keepdims=True) - xhat * jnp.mean(dxh * xhat, axis=-1, keepdims=True))
        dp_ref[:, SGU_W:] = (dvg * v_slope).astype(BF)

    vec = pl.BlockSpec((1, SGU_W), lambda i: (0, 0))
    row = pl.BlockSpec((BLK, SGU_W), lambda i: (i, 0))
    w_spec = pl.BlockSpec(w_sp.shape, lambda i: (0, 0, 0))
    sq = pl.BlockSpec((BLK, HEAD_DIM), lambda i: (0, 0))
    return pl.pallas_call(
        body,
        out_shape=[jax.ShapeDtypeStruct(into.shape, into.dtype),
                   jax.ShapeDtypeStruct(w_sp.shape, F32), jax.ShapeDtypeStruct((BLK, HEAD_DIM), F32),
                   jax.ShapeDtypeStruct((1, SGU_W), F32), jax.ShapeDtypeStruct((1, SGU_W), F32)],
        grid=(s // BLK,),
        in_specs=[row, pl.BlockSpec((BLK, SGU_W), lambda i: (i, 1)), row, vec, vec, w_spec,
                  pl.BlockSpec(b_t.shape, lambda i: (0, 0)), ANY],
        out_specs=[pl.BlockSpec((BLK, 2 * SGU_W), lambda i: (i, 0)), w_spec, sq, vec, vec],
        scratch_shapes=[pltpu.VMEM((BLK, SGU_W), F32)], input_output_aliases={7: 0}, name="sgu_bwd",
        compiler_params=_params(("arbitrary",)))(proj, proj, dcat, ln_g, ln_b, w_sp, b_t, into)


def _place():
    return lax.axis_index("x"), lax.axis_index("y"), lax.axis_index("c")


def _other_chips(x, y):
    return [(1 - x, y), (x, 1 - y), (1 - x, 1 - y)]


def _peer(x, y, c, mask):
    return (1 - x if mask & 4 else x, 1 - y if mask & 2 else y, 1 - c if mask & 1 else c)


def _in_hbm(a):
    return pltpu.with_memory_space_constraint(a, pltpu.HBM)


def _token_spec():
    return jax.ShapeDtypeStruct((8, LANES), F32), pl.BlockSpec(memory_space=pltpu.VMEM)


def _remote(src, dst, ssem, rsem, to):
    return pltpu.make_async_remote_copy(src_ref=src, dst_ref=dst, send_sem=ssem, recv_sem=rsem, device_id=to,
                                        device_id_type=MESH)


def _place_shard(name, src, layer, place, dtype, after=()):
    _, rows, cols = src.shape
    tr = _pick_rows(rows, 512)

    def body(p_ref, s_ref, *rest):
        rest[-1][...] = s_ref[...].astype(dtype)

    grid_spec = pltpu.PrefetchScalarGridSpec(
        num_scalar_prefetch=1, grid=(rows // tr,),
        in_specs=[pl.BlockSpec((None, tr, cols), lambda i, p: (layer, i, 0))] + [ANY] * len(after),
        out_specs=pl.BlockSpec((None, tr, cols), lambda i, p: (p[1], i, 0)))
    return pl.pallas_call(body, out_shape=jax.ShapeDtypeStruct((N_CHIPS, rows, cols), dtype), grid_spec=grid_spec,
                          name=name, compiler_params=_params(("parallel",)))(place, src, *after)


def _gather_copies(bufs, ssem, rsem):
    x, y, c = _place()
    me = 2 * x + y
    copies = []
    for ai, buf in enumerate(bufs):
        for k, (ox, oy) in enumerate(_other_chips(x, y)):
            copies.append(_remote(buf.at[me], buf.at[me], ssem.at[3 * ai + k], rsem.at[3 * ai + k], (ox, oy, c)))
    return copies


def _reduce_copies(grads, lands, ssem, rsem):
    x, y, c = _place()
    copies = []
    for a, (gr, land) in enumerate(zip(grads, lands)):
        for mask in range(1, N_DEV):
            px, py, pc = _peer(x, y, c, mask)
            copies.append(_remote(gr.at[pc, 2 * px + py], land.at[mask - 1], ssem.at[7 * a + mask - 1],
                                  rsem.at[7 * a + mask - 1], (px, py, pc)))
    return copies


def _half_copies(totals, ssem, rsem):
    x, y, c = _place()
    return [_remote(t.at[c], t.at[c], ssem.at[a], rsem.at[a], (x, y, 1 - c)) for a, t in enumerate(totals)]


def _gather_start(name, groups):
    flat = [s for grp in groups for s in grp]
    n, ng = len(flat), len(groups)

    def body(*refs):
        ins = refs[:n]
        sems = refs[n:n + 2 * ng]
        token = refs[-1]
        idx = 0
        for gi, grp in enumerate(groups):
            for cp in _gather_copies(ins[idx:idx + len(grp)], sems[2 * gi], sems[2 * gi + 1]):
                cp.start()
            idx += len(grp)
        token[...] = jnp.zeros_like(token)

    tok_shape, tok_spec = _token_spec()
    sem_shapes = []
    for grp in groups:
        sem_shapes += [pltpu.SemaphoreType.DMA((3 * len(grp),))] * 2
    res = pl.pallas_call(
        body, name=name,
        out_shape=(*sem_shapes, *[pltpu.HBM(s.shape, s.dtype) for s in flat], tok_shape),
        in_specs=[HBM] * n, out_specs=(*[SEM] * (2 * ng), *[HBM] * n, tok_spec),
        input_output_aliases={i: 2 * ng + i for i in range(n)},
        compiler_params=pltpu.CompilerParams(has_side_effects=EFFECT))(*[_in_hbm(s) for s in flat])
    out, idx = [], 2 * ng
    for gi, grp in enumerate(groups):
        out.append((res[2 * gi], res[2 * gi + 1], list(res[idx:idx + len(grp)])))
        idx += len(grp)
    return out, res[-1]


def _gather_wait(name, ssem, rsem, slabs, after):
    n = len(slabs)

    def body(*refs):
        for cp in _gather_copies(refs[:n], refs[n], refs[n + 1]):
            cp.wait_send()
            cp.wait_recv()

    return pl.pallas_call(
        body, name=name, out_shape=tuple(pltpu.HBM(s.shape, s.dtype) for s in slabs),
        in_specs=[HBM] * n + [SEM, SEM] + [ANY] * len(after), out_specs=tuple([HBM] * n),
        input_output_aliases={i: i for i in range(n)},
        compiler_params=pltpu.CompilerParams(has_side_effects=EFFECT))(*slabs, ssem, rsem, *after)


def _reduce_start(name, grads):
    n = len(grads)
    lands = [lax.empty((N_DEV - 1, *g.shape[2:]), g.dtype) for g in grads]

    def body(*refs):
        token = refs[-1]
        for cp in _reduce_copies(refs[:n], refs[n:2 * n], refs[2 * n], refs[2 * n + 1]):
            cp.start()
        token[...] = jnp.zeros_like(token)

    tok_shape, tok_spec = _token_spec()
    sems = [pltpu.SemaphoreType.DMA((7 * n,))] * 2
    res = pl.pallas_call(
        body, name=name,
        out_shape=(*sems, *[pltpu.HBM(g.shape, g.dtype) for g in grads], *[pltpu.HBM(l.shape, l.dtype) for l in lands],
                   tok_shape),
        in_specs=[HBM] * (2 * n), out_specs=(SEM, SEM, *[HBM] * (2 * n), tok_spec),
        input_output_aliases={i: 2 + i for i in range(2 * n)},
        compiler_params=pltpu.CompilerParams(has_side_effects=EFFECT))(*[_in_hbm(t) for t in (*grads, *lands)])
    return res[0], res[1], list(res[2:2 + n]), list(res[2 + n:2 + 2 * n]), res[-1]


def _reduce_wait(name, ssem, rsem, grads, lands, after):
    n = len(grads)

    def body(*refs):
        for cp in _reduce_copies(refs[:n], refs[n:2 * n], refs[2 * n], refs[2 * n + 1]):
            cp.wait_send()
            cp.wait_recv()

    res = pl.pallas_call(
        body, name=name, out_shape=tuple(pltpu.HBM(t.shape, t.dtype) for t in (*grads, *lands)),
        in_specs=[HBM] * (2 * n) + [SEM, SEM] + [ANY] * len(after), out_specs=tuple([HBM] * (2 * n)),
        input_output_aliases={i: i for i in range(2 * n)},
        compiler_params=pltpu.CompilerParams(has_side_effects=EFFECT))(*grads, *lands, ssem, rsem, *after)
    return list(res[:n]), list(res[n:])


def _sum_pieces(name, grad, land, place):
    _, _, rows, cols = grad.shape
    tr = _pick_rows(rows, 256)

    def body(p_ref, g_ref, l_ref, o_ref):
        tot = g_ref[...].astype(F32)
        for k in range(N_DEV - 1):
            tot = tot + l_ref[k].astype(F32)
        o_ref[...] = tot

    grid_spec = pltpu.PrefetchScalarGridSpec(
        num_scalar_prefetch=1, grid=(rows // tr,),
        in_specs=[pl.BlockSpec((None, None, tr, cols), lambda i, p: (p[0], p[1], i, 0)),
                  pl.BlockSpec((N_DEV - 1, tr, cols), lambda i, p: (0, i, 0))],
        out_specs=pl.BlockSpec((None, tr, cols), lambda i, p: (p[0], i, 0)))
    return pl.pallas_call(body, out_shape=jax.ShapeDtypeStruct((2, rows, cols), F32), grid_spec=grid_spec, name=name,
                          compiler_params=_params(("parallel",)))(place, grad, land)


def _first_copies(bufs, ssem, rsem):
    x, y, c = _place()
    me = 2 * x + y
    copies = []
    for a, buf in enumerate(bufs):
        for k, (ox, oy) in enumerate(_other_chips(x, y)):
            copies.append(_remote(buf.at[me, c], buf.at[me, c], ssem.at[3 * a + k], rsem.at[3 * a + k], (ox, oy, c)))
    return copies


def _pass_copies(bufs, ssem, rsem):
    x, y, c = _place()
    copies = []
    for a, buf in enumerate(bufs):
        for k, (ox, oy) in enumerate(_other_chips(x, y)):
            landed = buf.at[2 * ox + oy, c]
            copies.append(_remote(landed, landed, ssem.at[3 * a + k], rsem.at[3 * a + k], (x, y, 1 - c)))
    return copies


def _exchange_start(name, arrays, copies_fn, n_sems):
    n = len(arrays)

    def body(*refs):
        token = refs[-1]
        for cp in copies_fn(refs[:n], refs[n], refs[n + 1]):
            cp.start()
        token[...] = jnp.zeros_like(token)

    tok_shape, tok_spec = _token_spec()
    res = pl.pallas_call(
        body, name=name,
        out_shape=(pltpu.SemaphoreType.DMA((n_sems,)), pltpu.SemaphoreType.DMA((n_sems,)),
                   *[pltpu.HBM(t.shape, t.dtype) for t in arrays], tok_shape),
        in_specs=[HBM] * n, out_specs=(SEM, SEM, *[HBM] * n, tok_spec),
        input_output_aliases={i: 2 + i for i in range(n)},
        compiler_params=pltpu.CompilerParams(has_side_effects=EFFECT))(*[_in_hbm(t) for t in arrays])
    return res[0], res[1], list(res[2:2 + n]), res[-1]


def _exchange_wait(name, ssem, rsem, arrays, copies_fn, after):
    n = len(arrays)

    def body(*refs):
        for cp in copies_fn(refs[:n], refs[n], refs[n + 1]):
            cp.wait_send()
            cp.wait_recv()

    res = pl.pallas_call(
        body, name=name, out_shape=tuple(pltpu.HBM(t.shape, t.dtype) for t in arrays),
        in_specs=[HBM] * n + [SEM, SEM] + [ANY] * len(after), out_specs=tuple([HBM] * n),
        input_output_aliases={i: i for i in range(n)},
        compiler_params=pltpu.CompilerParams(has_side_effects=EFFECT))(*arrays, ssem, rsem, *after)
    return list(res)


def _half_start(name, totals):
    n = len(totals)

    def body(*refs):
        token = refs[-1]
        for cp in _half_copies(refs[:n], refs[n], refs[n + 1]):
            cp.start()
        token[...] = jnp.zeros_like(token)

    tok_shape, tok_spec = _token_spec()
    res = pl.pallas_call(
        body, name=name,
        out_shape=(pltpu.SemaphoreType.DMA((n,)), pltpu.SemaphoreType.DMA((n,)),
                   *[pltpu.HBM(t.shape, t.dtype) for t in totals], tok_shape),
        in_specs=[HBM] * n, out_specs=(SEM, SEM, *[HBM] * n, tok_spec),
        input_output_aliases={i: 2 + i for i in range(n)},
        compiler_params=pltpu.CompilerParams(has_side_effects=EFFECT))(*[_in_hbm(t) for t in totals])
    return res[0], res[1], list(res[2:2 + n]), res[-1]


def _half_wait(name, ssem, rsem, totals, after):
    n = len(totals)

    def body(*refs):
        for cp in _half_copies(refs[:n], refs[n], refs[n + 1]):
            cp.wait_send()
            cp.wait_recv()

    res = pl.pallas_call(
        body, name=name, out_shape=tuple(pltpu.HBM(t.shape, t.dtype) for t in totals),
        in_specs=[HBM] * n + [SEM, SEM] + [ANY] * len(after), out_specs=tuple([HBM] * n),
        input_output_aliases={i: i for i in range(n)},
        compiler_params=pltpu.CompilerParams(has_side_effects=EFFECT))(*totals, ssem, rsem, *after)
    return list(res)


def _small_copies(bufs, ssem, rsem):
    x, y, c = _place()
    mine = bufs[0].at[4 * x + 2 * y + c]
    return [_remote(mine, mine, ssem.at[mask - 1], rsem.at[mask - 1], _peer(x, y, c, mask)) for mask in range(1, N_DEV)]


def _small_start(name, slots):
    def body(s_ref, ssem, rsem, thru, token):
        for cp in _small_copies([s_ref], ssem, rsem):
            cp.start()
        token[...] = jnp.zeros_like(token)

    tok_shape, tok_spec = _token_spec()
    sems = [pltpu.SemaphoreType.DMA((N_DEV - 1,))] * 2
    return pl.pallas_call(
        body, name=name, out_shape=(*sems, pltpu.HBM(slots.shape, slots.dtype), tok_shape), in_specs=[HBM],
        out_specs=(SEM, SEM, HBM, tok_spec), input_output_aliases={0: 2},
        compiler_params=pltpu.CompilerParams(has_side_effects=EFFECT))(_in_hbm(slots))


def _small_wait(name, ssem, rsem, slots, after):
    def body(*refs):
        for cp in _small_copies([refs[0]], refs[1], refs[2]):
            cp.wait_send()
            cp.wait_recv()

    return pl.pallas_call(
        body, name=name, out_shape=pltpu.HBM(slots.shape, slots.dtype), in_specs=[HBM, SEM, SEM] + [ANY] * len(after),
        out_specs=HBM, input_output_aliases={0: 0},
        compiler_params=pltpu.CompilerParams(has_side_effects=EFFECT))(slots, ssem, rsem, *after)


def _own_slot(small, me):
    return lax.dynamic_update_slice(jnp.zeros((N_DEV, *small.shape), small.dtype), small[None], (me, 0, 0))


def _sum_devices(name, stacked):
    _, rows, lanes = stacked.shape
    tr = _pick_rows(rows, 512)

    def body(s_ref, o_ref):
        tot = s_ref[0]
        for k in range(1, N_DEV):
            tot = tot + s_ref[k]
        o_ref[...] = tot

    return pl.pallas_call(
        body, out_shape=jax.ShapeDtypeStruct((rows, lanes), F32), grid=(rows // tr,),
        in_specs=[pl.BlockSpec((N_DEV, tr, lanes), lambda i: (0, i, 0))], out_specs=pl.BlockSpec((tr, lanes), lambda i: (i, 0)),
        name=name, compiler_params=_params(("parallel",)))(stacked)


def _adamw(name, w, g, m, v, layer, prev=None):
    layers, rows, cols = w.shape
    tr = _pick_rows(rows, 256)
    c1 = 1.0 - ADAM_B1 ** ADAM_STEP
    c2 = 1.0 - ADAM_B2 ** ADAM_STEP

    def body(w_ref, g_ref, m_ref, v_ref, *rest):
        go_ref, d_ref, nm_ref, nv_ref = rest[-4:]
        gv = g_ref[...]
        nm = ADAM_B1 * m_ref[...] + (1.0 - ADAM_B1) * gv
        nv = ADAM_B2 * v_ref[...] + (1.0 - ADAM_B2) * (gv * gv)
        go_ref[...] = gv
        d_ref[...] = -ADAM_LR * ((nm / c1) / (jnp.sqrt(nv / c2) + ADAM_EPS) + ADAM_WD * w_ref[...])
        nm_ref[...] = nm
        nv_ref[...] = nv

    spec = pl.BlockSpec((None, tr, cols), lambda i: (layer, i, 0))
    prev = list(prev) if prev is not None else []
    return pl.pallas_call(
        body, out_shape=[jax.ShapeDtypeStruct((layers, rows, cols), F32)] * 4, grid=(rows // tr,),
        in_specs=[spec, pl.BlockSpec((tr, cols), lambda i: (i, 0)), spec, spec] + [ANY] * len(prev),
        out_specs=[spec] * 4, input_output_aliases={4 + i: i for i in range(len(prev))}, name=name,
        compiler_params=_params(("parallel",)))(w, g, m, v, *prev)


def _pack(vectors, pad_rows):
    flat = jnp.concatenate([t.reshape(-1) for t in vectors])
    rows = -(-flat.shape[0] // LANES)
    rows = -(-rows // pad_rows) * pad_rows
    return jnp.pad(flat, (0, rows * LANES - flat.shape[0])).reshape(rows, LANES)


def _unpack(packed, shapes):
    flat = packed.reshape(-1)
    out, off = [], 0
    for shp in shapes:
        size = math.prod(shp)
        out.append(flat[off:off + size].reshape(shp))
        off += size
    return out


def kernel(x, mem, positions, mix_norm, mem_norm, w_mem_kv, ffn_norm, w_gate, w_up, w_down, attn_w_in, attn_w_out, sgu_w_in, sgu_ln_g, sgu_ln_b, sgu_w_spatial, sgu_b_spatial, sgu_w_out, final_norm, loss_target, m_mix_norm, m_mem_norm, m_w_mem_kv, m_ffn_norm, m_w_gate, m_w_up, m_w_down, m_attn_w_in, m_attn_w_out, m_sgu_w_in, m_sgu_ln_g, m_sgu_ln_b, m_sgu_w_spatial, m_sgu_b_spatial, m_sgu_w_out, m_final_norm, v_mix_norm, v_mem_norm, v_w_mem_kv, v_ffn_norm, v_w_gate, v_w_up, v_w_down, v_attn_w_in, v_attn_w_out, v_sgu_w_in, v_sgu_ln_g, v_sgu_ln_b, v_sgu_w_spatial, v_sgu_b_spatial, v_sgu_w_out, v_final_norm):
    d_model = x.shape[2]
    x0, mem0, tgt = x[0], mem[0], loss_target[0]
    xi, yi, ci = _place()
    chip = 2 * xi + yi
    place = jnp.stack([ci, chip]).astype(jnp.int32)

    given_w = dict(mix_norm=mix_norm, mem_norm=mem_norm, w_mem_kv=w_mem_kv, ffn_norm=ffn_norm, w_gate=w_gate, w_up=w_up,
                   w_down=w_down, attn_w_in=attn_w_in, attn_w_out=attn_w_out, sgu_w_in=sgu_w_in, sgu_ln_g=sgu_ln_g,
                   sgu_ln_b=sgu_ln_b, sgu_w_spatial=sgu_w_spatial, sgu_b_spatial=sgu_b_spatial, sgu_w_out=sgu_w_out,
                   final_norm=final_norm)
    given_m = dict(mix_norm=m_mix_norm, mem_norm=m_mem_norm, w_mem_kv=m_w_mem_kv, ffn_norm=m_ffn_norm, w_gate=m_w_gate,
                   w_up=m_w_up, w_down=m_w_down, attn_w_in=m_attn_w_in, attn_w_out=m_attn_w_out, sgu_w_in=m_sgu_w_in,
                   sgu_ln_g=m_sgu_ln_g, sgu_ln_b=m_sgu_ln_b, sgu_w_spatial=m_sgu_w_spatial,
                   sgu_b_spatial=m_sgu_b_spatial, sgu_w_out=m_sgu_w_out, final_norm=m_final_norm)
    given_v = dict(mix_norm=v_mix_norm, mem_norm=v_mem_norm, w_mem_kv=v_w_mem_kv, ffn_norm=v_ffn_norm, w_gate=v_w_gate,
                   w_up=v_w_up, w_down=v_w_down, attn_w_in=v_attn_w_in, attn_w_out=v_attn_w_out, sgu_w_in=v_sgu_w_in,
                   sgu_ln_g=v_sgu_ln_g, sgu_ln_b=v_sgu_ln_b, sgu_w_spatial=v_sgu_w_spatial,
                   sgu_b_spatial=v_sgu_b_spatial, sgu_w_out=v_sgu_w_out, final_norm=v_final_norm)

    units = {"attn_w_in": ("attn_w_in", 0, "col"), "w_mem_kv0": ("w_mem_kv", 0, "row"), "attn_w_out": ("attn_w_out", 0, "col"),
             "w_gate0": ("w_gate", 0, "col"), "w_up0": ("w_up", 0, "col"), "w_down0": ("w_down", 0, "row"),
             "sgu_w_in": ("sgu_w_in", 0, "col"), "w_mem_kv1": ("w_mem_kv", 1, "row"), "sgu_w_out": ("sgu_w_out", 0, "row"),
             "w_gate1": ("w_gate", 1, "col"), "w_up1": ("w_up", 1, "col"), "w_down1": ("w_down", 1, "row")}
    gather_groups = [["attn_w_in"], ["w_mem_kv0", "attn_w_out"], ["w_gate0", "w_up0"],
                     ["w_down0", "sgu_w_in", "w_mem_kv1", "ln"], ["sgu_w_out"], ["w_gate1", "w_up1"], ["w_down1"]]

    first = _place_shard("place_attn_w_in", attn_w_in, 0, place, BF)
    first = first.reshape(N_CHIPS, 2, first.shape[1] // 2, first.shape[2])
    first_ssem, first_rsem, first_thru, token = _exchange_start("gather_start_0", [first], _first_copies, 3)
    in_flight = [None]
    slabs = {u: _place_shard(f"place_{u}", given_w[arr], layer, place, BF, after=[token])
             for u, (arr, layer, _) in units.items() if u != "attn_w_in"}
    slabs["ln"] = _place_shard("place_ln", jnp.concatenate([sgu_ln_g, sgu_ln_b])[None], 0, place, F32, after=[token])
    rest, token = _gather_start("gather_start_1", [[slabs[u] for u in grp] for grp in gather_groups[1:]])
    in_flight += rest
    weights = {}

    def arrive(gi, after):
        ssem, rsem, arrs = in_flight[gi]
        for u, full in zip(gather_groups[gi], _gather_wait(f"gather_wait_{gi}", ssem, rsem, arrs, after)):
            weights[u] = full if u == "ln" else Weight(full, units[u][2])

    w_sp = sgu_w_spatial[0]
    b_t = sgu_b_spatial[0].T
    tables = _rope_tables(positions[0])

    def residual(acc, extra):
        return [extra[0] + acc[0]]

    def memory_kv(layer):
        mem_n = _rms_fwd(f"mem_norm_{layer}", mem0, mem_norm[layer:layer + 1])
        return mem_n, _mm_nn(f"mem_kv_{layer}", mem_n, weights[f"w_mem_kv{layer}"])[0]

    h0 = _rms_fwd("mix_norm_0", x0, mix_norm[0:1], after=[token])
    landed = _exchange_wait("gather_wait_0", first_ssem, first_rsem, first_thru, _first_copies, [h0])
    pass_ssem, pass_rsem, passing, _ = _exchange_start("pass_start_0", landed, _pass_copies, 3)
    whole = _exchange_wait("pass_wait_0", pass_ssem, pass_rsem, passing, _pass_copies, [])[0]
    weights["attn_w_in"] = Weight(whole.reshape(N_CHIPS, -1, whole.shape[-1]), "col")
    proj0 = _mm_nn("attn_in", h0, weights["attn_w_in"])[0]
    arrive(1, [proj0])
    qkv = _rope_fwd(proj0, tables)
    qs, ks, vs = qkv[0:3], qkv[3:6], qkv[6:9]
    outs, lses = [], []
    for g in range(len(DILATIONS)):
        o, l = _dil_fwd(g, qs[g], ks[g], vs[g])
        outs.append(o)
        lses.append(l)
    merged, lse = _attn_merge(outs, lses)
    mem_n0, kv0 = memory_kv(0)
    cat0 = _mem_fwd("mem_fwd_0", proj0, 9, kv0, merged, 1)
    x1, hf0 = _mm_nn("attn_out", cat0, weights["attn_w_out"], extras=[x0], epilogue=residual, norm_gain=ffn_norm[0:1])
    arrive(2, [x1])
    g0, u0, act0 = _gate_up("gate_up_0", hf0, weights["w_gate0"], weights["w_up0"])
    arrive(3, [act0])
    x2 = _mm_nn("down_0", act0, weights["w_down0"], extras=[x1], epilogue=residual)[0]

    ln_all = weights["ln"]
    ln_g = ln_all[:, 0, :].reshape(1, SGU_W)
    ln_b = ln_all[:, 1, :].reshape(1, SGU_W)
    h1 = _rms_fwd("mix_norm_1", x2, mix_norm[1:2])
    proj1 = _mm_nn("sgu_in", h1, weights["sgu_w_in"], out_dtypes=(BF,))[0]
    sgu_out = _sgu_fwd(proj1, ln_g, ln_b, w_sp, b_t)
    mem_n1, kv1 = memory_kv(1)
    cat1 = _mem_fwd("mem_fwd_1", proj1, 6, kv1, sgu_out, 3)
    arrive(4, [cat1])
    x3, hf1 = _mm_nn("sgu_out", cat1, weights["sgu_w_out"], extras=[x2], epilogue=residual, norm_gain=ffn_norm[1:2])
    arrive(5, [x3])
    g1, u1, act1 = _gate_up("gate_up_1", hf1, weights["w_gate1"], weights["w_up1"])
    arrive(6, [act1])
    x4 = _mm_nn("down_1", act1, weights["w_down1"], extras=[x3], epilogue=residual)[0]

    d4, d4_op, g_final, loss_part = _final_loss(x4, tgt, final_norm.reshape(1, d_model))
    loss = lax.psum(loss_part[0, 0], ("x", "y", "c"))

    outputs = {}

    def start_reduce(tag, names, grads):
        ssem, rsem, grads, lands, tok = _reduce_start(f"reduce_start_{tag}", grads)
        return dict(tag=tag, names=names, ssem=ssem, rsem=rsem, grads=grads, lands=lands), tok

    def finish_reduce(st, after):
        grads, lands = _reduce_wait(f"reduce_wait_{st['tag']}", st["ssem"], st["rsem"], st["grads"], st["lands"], after)
        totals = [_sum_pieces(f"sum_{u}", g, l, place) for u, g, l in zip(st["names"], grads, lands)]
        ssem, rsem, totals, tok = _half_start(f"half_start_{st['tag']}", totals)
        return dict(tag=st["tag"], names=st["names"], ssem=ssem, rsem=rsem, totals=totals), tok

    def finish_update(st, after):
        totals = _half_wait(f"half_wait_{st['tag']}", st["ssem"], st["rsem"], st["totals"], after)
        for u, tot in zip(st["names"], totals):
            arr, layer, _ = units[u]
            w = given_w[arr]
            outputs[arr] = _adamw(f"adamw_{u}", w, tot.reshape(w.shape[1:]), given_m[arr], given_v[arr], layer,
                                  outputs.get(arr))

    def ffn_bwd(layer, d_out, d_out_op, xin, h, g, u, act):
        wd, wg, wu = weights[f"w_down{layer}"], weights[f"w_gate{layer}"], weights[f"w_up{layer}"]
        gr_down = _mm_tn(f"d_down_{layer}", act, d_out_op, wd)
        dg, du = _mm_nt(f"d_act_{layer}", [d_out_op], [wd], out_dtypes=(BF, BF), extras=[g, u],
                        epilogue=_swiglu_bwd_epilogue, col_chunk=EPILOGUE_CHUNK)
        gr_gate = _mm_tn(f"d_gate_{layer}", h, dg, wg)
        gr_up = _mm_tn(f"d_up_{layer}", h, du, wu)
        st, tok = start_reduce(f"ffn{layer}", [f"w_down{layer}", f"w_gate{layer}", f"w_up{layer}"], [gr_down, gr_gate, gr_up])
        dh = _mm_nt(f"d_ffn_h_{layer}", [dg, du], [wg, wu], out_dtypes=(BF,), after=[tok])[0]
        d_in, d_in_op, g_norm = _rms_bwd(f"ffn_norm_bwd_{layer}", xin, ffn_norm[layer:layer + 1], dh, d_out)
        return st, d_in, d_in_op, g_norm

    def memory_bwd(layer, mem_n, dkv):
        dkv = dkv.astype(BF)
        wkv = weights[f"w_mem_kv{layer}"]
        gr = _mm_tn(f"d_mem_kv_{layer}", mem_n, dkv, wkv)
        d_mem_n = _mm_nt(f"d_mem_n_{layer}", [dkv], [wkv])[0]
        return gr, _rms_bwd(f"mem_norm_bwd_{layer}", mem0, mem_norm[layer:layer + 1], d_mem_n)[2]

    st_ffn1, d3, d3_op, g_ffn1 = ffn_bwd(1, d4, d4_op, x3, hf1, g1, u1, act1)
    gr_sgu_out = _mm_tn("d_sgu_out", cat1, d3_op, weights["sgu_w_out"])
    dcat1 = _mm_nt("d_cat_1", [d3_op], [weights["sgu_w_out"]], out_dtypes=(BF,))[0]
    st_ffn1, tok = finish_reduce(st_ffn1, [dcat1])
    dproj1, dkv1 = _mem_bwd("mem_bwd_1", proj1, 6, kv1, dcat1, 3, proj1.shape[1])
    gr_kv1, g_mem1 = memory_bwd(1, mem_n1, dkv1)
    dproj1, g_wsp, g_bsp_t, g_ln_g, g_ln_b = _sgu_bwd(proj1, dcat1, ln_g, ln_b, w_sp, b_t, dproj1)
    gr_sgu_in = _mm_tn("d_sgu_in", h1, dproj1, weights["sgu_w_in"], after=[tok])
    finish_update(st_ffn1, [gr_sgu_in])
    st_mix1, tok = start_reduce("mix1", ["sgu_w_out", "w_mem_kv1", "sgu_w_in"], [gr_sgu_out, gr_kv1, gr_sgu_in])
    dh1 = _mm_nt("d_h_1", [dproj1], [weights["sgu_w_in"]], out_dtypes=(BF,), after=[tok])[0]
    d2, d2_op, g_mix1 = _rms_bwd("mix_norm_bwd_1", x2, mix_norm[1:2], dh1, d3)

    st_ffn0, d1, d1_op, g_ffn0 = ffn_bwd(0, d2, d2_op, x1, hf0, g0, u0, act0)
    dev = 4 * xi + 2 * yi + ci
    small_a = [g_mix1, g_mem1, jnp.concatenate([g_ffn0, g_ffn1]), g_wsp, g_bsp_t[:, :SGU_GROUPS].T, g_final, g_ln_g, g_ln_b]
    sa_ssem, sa_rsem, sa_slots, tok = _small_start("small_start_a", _own_slot(_pack(small_a, LANES), dev))
    gr_attn_out = _mm_tn("d_attn_out", cat0, d1_op, weights["attn_w_out"], after=[tok])
    st_mix1, tok = finish_reduce(st_mix1, [gr_attn_out])
    dcat0 = _mm_nt("d_cat_0", [d1_op], [weights["attn_w_out"]], after=[tok])[0]
    dproj0, dkv0 = _mem_bwd("mem_bwd_0", proj0, 9, kv0, dcat0, 1, proj0.shape[1])
    finish_update(st_mix1, [dkv0])
    gr_kv0, g_mem0 = memory_bwd(0, mem_n0, dkv0)
    st_ffn0, tok = finish_reduce(st_ffn0, [g_mem0])
    d_merged, delta = _attn_delta(dcat0, cat0, after=[tok])
    dqs, dks, dvs = [], [], []
    for g in range(len(DILATIONS)):
        dq, dk, dv = _dil_bwd(g, qs[g], ks[g], vs[g], d_merged[g], lse[g], delta[g])
        dqs.append(dq)
        dks.append(dk)
        dvs.append(dv)
    dproj0 = _rope_bwd(dqs + dks + dvs, tables, dproj0)
    finish_update(st_ffn0, [dproj0])
    gr_attn_in = _mm_tn("d_attn_in", h0, dproj0, weights["attn_w_in"])
    st_mix0, tok = start_reduce("mix0", ["attn_w_out", "w_mem_kv0", "attn_w_in"], [gr_attn_out, gr_kv0, gr_attn_in])
    dh0 = _mm_nt("d_h_0", [dproj0], [weights["attn_w_in"]], out_dtypes=(BF,), after=[tok])[0]
    d0, _, g_mix0 = _rms_bwd("mix_norm_bwd_0", x0, mix_norm[0:1], dh0, d1)

    small_b = [g_mix0, g_mem0]
    sb_ssem, sb_rsem, sb_slots, tok = _small_start("small_start_b", _own_slot(_pack(small_b, 8), dev))
    sa_slots = _small_wait("small_wait_a", sa_ssem, sa_rsem, sa_slots, [tok])
    g_mix1, g_mem1, g_ffn, g_wsp, g_bsp, g_final, g_ln_g, g_ln_b = _unpack(_sum_devices("small_sum_a", sa_slots),
                                                                           [t.shape for t in small_a])
    sb_slots = _small_wait("small_wait_b", sb_ssem, sb_rsem, sb_slots, [g_final])
    g_mix0, g_mem0 = _unpack(_sum_devices("small_sum_b", sb_slots), [t.shape for t in small_b])
    st_mix0, tok = finish_reduce(st_mix0, [g_mix0])
    g_mix, g_mem = jnp.concatenate([g_mix0, g_mix1]), jnp.concatenate([g_mem0, g_mem1])
    shard_w = sgu_ln_g.shape[-1]
    g_ln_g = lax.dynamic_slice_in_dim(g_ln_g, chip * shard_w, shard_w, axis=1)
    g_ln_b = lax.dynamic_slice_in_dim(g_ln_b, chip * shard_w, shard_w, axis=1)
    small_names = ["mix_norm", "mem_norm", "ffn_norm", "sgu_w_spatial", "sgu_b_spatial", "final_norm", "sgu_ln_g",
                   "sgu_ln_b"]
    small_g = [g_mix, g_mem, g_ffn, g_wsp, g_bsp, g_final, g_ln_g, g_ln_b]
    small_shapes = [given_w[k].shape for k in small_names]
    packed = [_pack(t, LANES) for t in ([given_w[k] for k in small_names], small_g, [given_m[k] for k in small_names],
                                    [given_v[k] for k in small_names])]
    small_out = _adamw("adamw_small", packed[0][None], packed[1], packed[2][None], packed[3][None], 0)
    finish_update(st_mix0, [small_out[0]])
    for k, gk, dk, mk, vk in zip(small_names, *[_unpack(t[0], small_shapes) for t in small_out]):
        outputs[k] = (gk, dk, mk, vk)

    order = ["mix_norm", "mem_norm", "w_mem_kv", "ffn_norm", "w_gate", "w_up", "w_down", "attn_w_in", "attn_w_out",
             "sgu_w_in", "sgu_ln_g", "sgu_ln_b", "sgu_w_spatial", "sgu_b_spatial", "sgu_w_out", "final_norm"]
    return (loss, d0[None], *[outputs[k][0] for k in order], *[outputs[k][1] for k in order],
            *[outputs[k][2] for k in order], *[outputs[k][3] for k in order])
```
